```python
import math
import jax, jax.numpy as jnp
from jax import lax
import numpy as np

D_MODEL = 1024
BATCH = 32
SEQ = 2048
DEPTH = 1

HEAD_DIM = 64
FOX_HEADS = D_MODEL // 2 // HEAD_DIM
RWKV_HEADS = D_MODEL // 2 // HEAD_DIM
FOX_WIDTH = FOX_HEADS * HEAD_DIM
RWKV_WIDTH = RWKV_HEADS * HEAD_DIM
D_MIX = FOX_WIDTH + RWKV_WIDTH
Q_BLOCK = 128
RMS_EPS = 1e-6
LNX_EPS = 64e-5
DECAY_LORA = max(32, int(round(1.8 * D_MODEL ** 0.5 / 32)) * 32)
ICLR_LORA = max(32, int(round(1.8 * D_MODEL ** 0.5 / 32)) * 32)
GATE_LORA = max(32, int(round(0.6 * D_MODEL ** 0.8 / 32)) * 32)
FOX_SPLITS = [FOX_WIDTH, FOX_WIDTH, FOX_WIDTH, FOX_WIDTH, FOX_HEADS, FOX_HEADS, FOX_HEADS]
RWKV_SPLITS = [RWKV_WIDTH, RWKV_WIDTH, RWKV_WIDTH, DECAY_LORA, ICLR_LORA, GATE_LORA]
FOX_COLS = sum(FOX_SPLITS)
RWKV_COLS = sum(RWKV_SPLITS)
D_IN_PROJ = FOX_COLS + RWKV_COLS
FORGET_BIAS_MEAN = 3.0
N_EXPERTS = 64
N_GROUPS = 8
TOPK_GROUPS = 4
TOP_K = 6
D_EXPERT = 256
D_SHARED = 256
ROUTED_SCALE = 2.5
EXPERT_BLOCK = 256

kernel_name = "hybrid_fox_rwkv7_moe_adaln"


def rms_norm(x, gain, eps=RMS_EPS):
    xf = x.astype(jnp.float32)
    y = xf * lax.rsqrt(jnp.mean(xf * xf, axis=-1, keepdims=True) + eps)
    return (y * gain.astype(jnp.float32)).astype(x.dtype)


def shift_prev(z):
    pad = [(0, 0), (1, 0)] + [(0, 0)] * (z.ndim - 2)
    return jnp.pad(z, pad)[:, :-1]


def split_cols(z, sizes):
    cuts = [int(s) for s in np.cumsum(sizes)[:-1]]
    return jnp.split(z, cuts, axis=-1)


def swiglu(x, wg, wu, wd):
    return (jax.nn.silu(x @ wg) * (x @ wu)) @ wd


def fox_attention(q, k, v, cum):
    B, T, H, D = q.shape
    nb = T // Q_BLOCK
    scale = D ** -0.5
    qb = q.reshape(B, nb, Q_BLOCK, H, D).transpose(1, 0, 3, 2, 4)
    cb = cum.reshape(B, nb, Q_BLOCK, H).transpose(1, 0, 3, 2)
    kh = k.transpose(0, 2, 1, 3)
    vh = v.transpose(0, 2, 1, 3)
    ck = cum.transpose(0, 2, 1)
    key_pos = jnp.arange(T)

    def block(args):
        q_blk, c_blk, i = args
        s = jnp.einsum('bhqd,bhkd->bhqk', q_blk, kh).astype(jnp.float32) * scale
        s = s + (c_blk[..., :, None] - ck[:, :, None, :])
        q_pos = i * Q_BLOCK + jnp.arange(Q_BLOCK)
        s = jnp.where(key_pos[None, :] <= q_pos[:, None], s, -jnp.inf)
        p = jax.nn.softmax(s, axis=-1)
        return jnp.einsum('bhqk,bhkd->bhqd', p.astype(vh.dtype), vh)

    o = lax.map(block, (qb, cb, jnp.arange(nb)))
    return o.transpose(1, 0, 3, 2, 4).reshape(B, T, H, D)


def fox_mixer(zf, qn_g, kn_g, on_g, forget_b):
    B, T, _ = zf.shape
    q, k, v, g, f_logit, a_k, a_v = split_cols(zf, FOX_SPLITS)
    heads = lambda t: t.reshape(B, T, FOX_HEADS, HEAD_DIM)
    q, k, v = heads(q), heads(k), heads(v)
    a_k = jax.nn.sigmoid(a_k)[..., None]
    a_v = jax.nn.sigmoid(a_v)[..., None]
    k = a_k * shift_prev(k) + (1 - a_k) * k
    v = a_v * shift_prev(v) + (1 - a_v) * v
    q = rms_norm(q, qn_g)
    k = rms_norm(k, kn_g)
    log_f = jax.nn.log_sigmoid((f_logit + forget_b).astype(jnp.float32))
    cum = jnp.cumsum(log_f, axis=1)
    o = fox_attention(q, k, v, cum)
    o = rms_norm(o, on_g) * jax.nn.sigmoid(heads(g))
    return o.reshape(B, T, FOX_WIDTH)


def wkv7_scan(r, w, k, v, a, b):
    B, T, H, N = r.shape

    def step(S, inp):
        r_t, w_t, k_t, v_t, a_t, b_t = inp
        Sa = jnp.einsum('bhij,bhj->bhi', S, a_t)
        S = S * w_t[:, :, None, :] + Sa[..., None] * b_t[:, :, None, :] + v_t[..., None] * k_t[:, :, None, :]
        return S, jnp.einsum('bhij,bhj->bhi', S, r_t)

    xs = tuple(t.transpose(1, 0, 2, 3) for t in (r, w, k, v, a, b))
    S0 = jnp.zeros((B, H, N, N), jnp.float32)
    _, ys = lax.scan(step, S0, xs)
    return ys.transpose(1, 0, 2, 3)


def rwkv7_mixer(zr, mu, w0, decay_up, a0, iclr_up, gate_up, k_k, k_a, r_k, lnx_g, lnx_b):
    B, T, _ = zr.shape
    f32 = jnp.float32
    zr = zr + mu * (shift_prev(zr) - zr)
    r, k, v, wd, ad, gd = split_cols(zr, RWKV_SPLITS)
    w_log = -jax.nn.softplus(-(w0 + jnp.tanh(wd) @ decay_up).astype(f32)) - 0.5
    decay = jnp.exp(-jnp.exp(w_log))
    a = jax.nn.sigmoid((a0 + ad @ iclr_up).astype(f32))
    g = jax.nn.sigmoid(gd) @ gate_up
    heads = lambda t: t.astype(f32).reshape(B, T, RWKV_HEADS, HEAD_DIM)
    kk = heads(k * k_k)
    kk = kk / jnp.maximum(jnp.sqrt(jnp.sum(kk * kk, axis=-1, keepdims=True)), 1e-12)
    k = k.astype(f32) * (1 + (a - 1) * k_a.astype(f32))
    rh, kh, vh, ah, wh = heads(r), heads(k), heads(v), heads(a), heads(decay)
    o = wkv7_scan(rh, wh, kh, vh, -kk, kk * ah)
    mean = jnp.mean(o, axis=-1, keepdims=True)
    var = jnp.mean(jnp.square(o - mean), axis=-1, keepdims=True)
    o = (o - mean) * lax.rsqrt(var + LNX_EPS)
    o = o * lnx_g.astype(f32).reshape(RWKV_HEADS, HEAD_DIM) + lnx_b.astype(f32).reshape(RWKV_HEADS, HEAD_DIM)
    o = o + jnp.sum(rh * kh * r_k.astype(f32), axis=-1, keepdims=True) * vh
    return (o.reshape(B, T, RWKV_WIDTH) * g.astype(f32)).astype(zr.dtype)


def routed_experts(h, idx, wts, w_gate, w_up, w_down):
    n_tok, d = h.shape
    n_assign = n_tok * TOP_K
    flat_e = idx.reshape(-1).astype(jnp.int32)
    order = jnp.argsort(flat_e)
    sorted_e = flat_e[order]
    counts = jnp.bincount(flat_e, length=N_EXPERTS).astype(jnp.int32)
    padded = (counts + EXPERT_BLOCK - 1) // EXPERT_BLOCK * EXPERT_BLOCK
    pad_end = jnp.cumsum(padded)
    pad_start = pad_end - padded
    grp_start = jnp.cumsum(counts) - counts
    dest = pad_start[sorted_e] + jnp.arange(n_assign, dtype=jnp.int32) - grp_start[sorted_e]
    n_blocks = -(-n_assign // EXPERT_BLOCK) + N_EXPERTS
    n_rows = n_blocks * EXPERT_BLOCK
    row_tok = jnp.zeros((n_rows,), jnp.int32).at[dest].set((order // TOP_K).astype(jnp.int32))
    row_w = jnp.zeros((n_rows,), wts.dtype).at[dest].set(wts.reshape(-1)[order])
    blk_start = jnp.arange(n_blocks, dtype=pad_end.dtype) * EXPERT_BLOCK
    blk_e = jnp.minimum(jnp.searchsorted(pad_end, blk_start, side='right'), N_EXPERTS - 1)

    def block(args):
        tok, w, e = args
        y = swiglu(h[tok], w_gate[e], w_up[e], w_down[e])
        return y * w[:, None]

    ys = lax.map(block, (row_tok.reshape(n_blocks, EXPERT_BLOCK), row_w.reshape(n_blocks, EXPERT_BLOCK), blk_e))
    return jax.ops.segment_sum(ys.reshape(n_rows, d), row_tok, num_segments=n_tok)


def moe_ffn(h, router_w, router_bias, w_gate, w_up, w_down, sw_gate, sw_up, sw_down):
    B, T, D = h.shape
    hf = h.reshape(B * T, D)
    n = hf.shape[0]
    scores = jax.nn.sigmoid((hf @ router_w).astype(jnp.float32))
    sel = scores + router_bias.astype(jnp.float32)
    grp = sel.reshape(n, N_GROUPS, N_EXPERTS // N_GROUPS)
    grp_score = jnp.sum(lax.top_k(grp, 2)[0], axis=-1)
    _, top_g = lax.top_k(grp_score, TOPK_GROUPS)
    gmask = jnp.any(top_g[..., None] == jnp.arange(N_GROUPS), axis=-2)
    emask = jnp.repeat(gmask, N_EXPERTS // N_GROUPS, axis=-1)
    _, idx = lax.top_k(jnp.where(emask, sel, -jnp.inf), TOP_K)
    wts = jnp.take_along_axis(scores, idx, axis=-1)
    wts = wts / jnp.sum(wts, axis=-1, keepdims=True) * ROUTED_SCALE
    routed = routed_experts(hf, idx, wts.astype(h.dtype), w_gate, w_up, w_down)
    shared = swiglu(hf, sw_gate, sw_up, sw_down)
    return (routed + shared).reshape(B, T, D)


def setup_inputs(seed: int = 0) -> dict:
    key = jax.random.key(seed)
    ks = iter(jax.random.split(key, 40))
    nrm = lambda shape, s: jax.random.normal(next(ks), shape, jnp.float32) * s
    gain = lambda shape: 1.0 + nrm(shape, 0.02)
    L, D = DEPTH, D_MODEL
    return {
        "x": nrm((BATCH, SEQ, D), 1.0),
        "c": nrm((BATCH, D), 1.0),
        "norm1_g": gain((L, D)),
        "norm2_g": gain((L, D)),
        "ada_w": nrm((L, D, 6 * D), 0.5 * D ** -0.5),
        "ada_b": nrm((L, 6 * D), 0.02),
        "w_in": nrm((L, D, D_IN_PROJ), D ** -0.5),
        "w_out": nrm((L, D_MIX, D), D_MIX ** -0.5),
        "fox_qn_g": gain((L, FOX_HEADS, HEAD_DIM)),
        "fox_kn_g": gain((L, FOX_HEADS, HEAD_DIM)),
        "fox_on_g": gain((L, FOX_HEADS, HEAD_DIM)),
        "fox_forget_b": FORGET_BIAS_MEAN + nrm((L, FOX_HEADS), 0.5),
        "rw_mu": jax.random.uniform(next(ks), (L, RWKV_COLS), jnp.float32),
        "rw_w0": jax.random.uniform(next(ks), (L, RWKV_WIDTH), jnp.float32, -6.0, -1.0),
        "rw_decay_up": nrm((L, DECAY_LORA, RWKV_WIDTH), 0.1 * DECAY_LORA ** -0.5),
        "rw_a0": nrm((L, RWKV_WIDTH), 0.1),
        "rw_iclr_up": nrm((L, ICLR_LORA, RWKV_WIDTH), 0.1 * ICLR_LORA ** -0.5),
        "rw_gate_up": nrm((L, GATE_LORA, RWKV_WIDTH), GATE_LORA ** -0.5),
        "rw_k_k": 0.85 + nrm((L, RWKV_WIDTH), 0.05),
        "rw_k_a": 1.0 + nrm((L, RWKV_WIDTH), 0.05),
        "rw_r_k": nrm((L, RWKV_HEADS, HEAD_DIM), 0.1),
        "rw_lnx_g": gain((L, RWKV_WIDTH)),
        "rw_lnx_b": nrm((L, RWKV_WIDTH), 0.02),
        "router_w": nrm((L, D, N_EXPERTS), D ** -0.5),
        "router_bias": nrm((L, N_EXPERTS), 0.01),
        "exp_w_gate": nrm((L, N_EXPERTS, D, D_EXPERT), D ** -0.5),
        "exp_w_up": nrm((L, N_EXPERTS, D, D_EXPERT), D ** -0.5),
        "exp_w_down": nrm((L, N_EXPERTS, D_EXPERT, D), D_EXPERT ** -0.5),
        "sh_w_gate": nrm((L, D, D_SHARED), D ** -0.5),
        "sh_w_up": nrm((L, D, D_SHARED), D ** -0.5),
        "sh_w_down": nrm((L, D_SHARED, D), D_SHARED ** -0.5),
        "final_g": gain((D,)),
    }


def reference(x, c, norm1_g, norm2_g, ada_w, ada_b, w_in, w_out,
              fox_qn_g, fox_kn_g, fox_on_g, fox_forget_b,
              rw_mu, rw_w0, rw_decay_up, rw_a0, rw_iclr_up, rw_gate_up, rw_k_k, rw_k_a, rw_r_k, rw_lnx_g, rw_lnx_b,
              router_w, router_bias, exp_w_gate, exp_w_up, exp_w_down, sh_w_gate, sh_w_up, sh_w_down, final_g):
    for l in range(DEPTH):
        mod = jax.nn.silu(c) @ ada_w[l] + ada_b[l]
        sh1, sc1, g1, sh2, sc2, g2 = jnp.split(mod[:, None, :], 6, axis=-1)
        h = rms_norm(x, norm1_g[l]) * (1 + sc1) + sh1
        z = h @ w_in[l]
        y_fox = fox_mixer(z[..., :FOX_COLS], fox_qn_g[l], fox_kn_g[l], fox_on_g[l], fox_forget_b[l])
        y_rwkv = rwkv7_mixer(z[..., FOX_COLS:], rw_mu[l], rw_w0[l], rw_decay_up[l], rw_a0[l], rw_iclr_up[l],
                             rw_gate_up[l], rw_k_k[l], rw_k_a[l], rw_r_k[l], rw_lnx_g[l], rw_lnx_b[l])
        x = x + g1 * (jnp.concatenate([y_fox, y_rwkv], axis=-1) @ w_out[l])
        h = rms_norm(x, norm2_g[l]) * (1 + sc2) + sh2
        x = x + g2 * moe_ffn(h, router_w[l], router_bias[l], exp_w_gate[l], exp_w_up[l], exp_w_down[l],
                             sh_w_gate[l], sh_w_up[l], sh_w_down[l])
    return rms_norm(x, final_g)
```

```python
import functools

import jax
import jax.numpy as jnp
import numpy as np
from jax import lax
from jax.experimental import pallas as pl
from jax.experimental.pallas import tpu as pltpu

F32 = jnp.float32
BF16 = jnp.bfloat16
I32 = jnp.int32
HIGHEST = lax.Precision.HIGHEST

D_MODEL = 1024
HEAD_DIM = 64
N_HEADS = 8
HALF = N_HEADS * HEAD_DIM
RMS_EPS = 1e-6
LNX_EPS = 64e-5
DECAY_LORA = 64
ICLR_LORA = 64
GATE_LORA = 160
N_EXPERTS = 64
N_GROUPS = 8
GROUP_SIZE = N_EXPERTS // N_GROUPS
TOPK_GROUPS = 4
TOP_K = 6
D_EXPERT = 256
D_SHARED = 256
ROUTED_SCALE = 2.5
EXPERT_BLOCK = 256

LANES = 128
Z_MAIN = 4 * HALF + 3 * HALF
Z_SMALL = 5 * LANES
VMEM_LIMIT = 56 * 1024 * 1024


def _cparams(semantics):
    return pltpu.CompilerParams(dimension_semantics=semantics, vmem_limit_bytes=VMEM_LIMIT)


def _mm(a, b):
    return jnp.dot(a.astype(BF16), b.astype(BF16), preferred_element_type=F32)


def _mm_nt(a, b):
    return lax.dot_general(a.astype(BF16), b.astype(BF16), (((1,), (1,)), ((), ())),
                           preferred_element_type=F32)


def _mm_tn(a, b):
    return lax.dot_general(a.astype(BF16), b.astype(BF16), (((0,), (0,)), ((), ())),
                           preferred_element_type=F32)


def _mm_f32(a, b):
    return jnp.dot(a, b, precision=HIGHEST, preferred_element_type=F32)


def _sigmoid(x):
    return 1.0 / (1.0 + jnp.exp(-x))


def _softplus(x):
    return jnp.maximum(x, 0.0) + jnp.log(1.0 + jnp.exp(-jnp.abs(x)))


def _silu(x):
    return x * _sigmoid(x)


def _pack_halves(x):
    w = x.shape[1] // 2
    bits = lambda t: lax.bitcast_convert_type(t.astype(jnp.bfloat16).astype(F32), jnp.uint32)
    return (bits(x[:, 0:w]) >> 16) | (bits(x[:, w:2 * w]) & jnp.uint32(0xFFFF0000))


def _unpack_halves(p):
    lo = lax.bitcast_convert_type(p << 16, F32)
    hi = lax.bitcast_convert_type(p & jnp.uint32(0xFFFF0000), F32)
    return lo, hi


def _shift_rows(z, carry_ref, first):
    rows = z.shape[0]
    prev_row = jnp.where(first, 0.0, carry_ref[0:1, :])
    prev = pltpu.roll(z, 1, 0)
    row0 = lax.broadcasted_iota(I32, (rows, 1), 0) == 0
    prev = jnp.where(row0, prev_row, prev)
    carry_ref[0:1, :] = z[rows - 1:rows, :]
    return prev


def _ada_kernel(c_ref, w_ref, b_ref, o_ref):
    o_ref[...] = _mm_f32(_silu(c_ref[...]), w_ref[...]) + b_ref[...]


def _ada(c, ada_w, ada_b):
    B, D = c.shape
    n_out = ada_w.shape[1]
    tn = 512
    return pl.pallas_call(
        _ada_kernel,
        grid=(n_out // tn,),
        in_specs=[pl.BlockSpec((B, D), lambda j: (0, 0)),
                  pl.BlockSpec((D, tn), lambda j: (0, j)),
                  pl.BlockSpec((1, tn), lambda j: (0, j))],
        out_specs=pl.BlockSpec((B, tn), lambda j: (0, j)),
        out_shape=jax.ShapeDtypeStruct((B, n_out), F32),
        compiler_params=_cparams(("arbitrary",)),
        name="ada",
    )(c, ada_w, ada_b.reshape(1, n_out))


def _inproj_kernel(x_ref, g_ref, sh_ref, sc_ref, w_ref, mub_ref, mus_ref, zm_ref, zs_ref,
                   carry_b, carry_s, *, tiles_per_seq):
    first = (pl.program_id(0) % tiles_per_seq) == 0
    x = x_ref[...]
    h = x * lax.rsqrt(jnp.mean(x * x, axis=-1, keepdims=True) + RMS_EPS) * g_ref[...]
    hb = (h * (1.0 + sc_ref[...]) + sh_ref[...]).astype(BF16)
    nf = 4 * HALF
    zm_ref[:, 0:nf] = jnp.dot(hb, w_ref[:, 0:nf], preferred_element_type=F32).astype(BF16)
    zr = jnp.dot(hb, w_ref[:, nf:Z_MAIN], preferred_element_type=F32)
    zr = zr + mub_ref[...] * (_shift_rows(zr, carry_b, first) - zr)
    zm_ref[:, nf:Z_MAIN] = zr.astype(BF16)
    zs = jnp.dot(hb, w_ref[:, Z_MAIN:Z_MAIN + Z_SMALL], preferred_element_type=F32)
    zs_ref[...] = zs + mus_ref[...] * (_shift_rows(zs, carry_s, first) - zs)


def _inproj(x2d, mod4, norm_g, w_all, mu_big, mu_small, T):
    N, D = x2d.shape
    tm = min(512, T)
    tps = T // tm
    vec = lambda j: pl.BlockSpec((None, None, 1, D), lambda i: (i // tps, j, 0, 0))
    return pl.pallas_call(
        functools.partial(_inproj_kernel, tiles_per_seq=tps),
        grid=(N // tm,),
        in_specs=[pl.BlockSpec((tm, D), lambda i: (i, 0)),
                  pl.BlockSpec((1, D), lambda i: (0, 0)),
                  vec(0), vec(1),
                  pl.BlockSpec((D, Z_MAIN + Z_SMALL), lambda i: (0, 0)),
                  pl.BlockSpec((1, 3 * HALF), lambda i: (0, 0)),
                  pl.BlockSpec((1, Z_SMALL), lambda i: (0, 0))],
        out_specs=[pl.BlockSpec((tm, Z_MAIN), lambda i: (i, 0)),
                   pl.BlockSpec((tm, Z_SMALL), lambda i: (i, 0))],
        out_shape=[jax.ShapeDtypeStruct((N, Z_MAIN), BF16),
                   jax.ShapeDtypeStruct((N, Z_SMALL), F32)],
        scratch_shapes=[pltpu.VMEM((8, 3 * HALF), F32), pltpu.VMEM((8, Z_SMALL), F32)],
        compiler_params=_cparams(("arbitrary",)),
        name="inproj",
    )(x2d, norm_g.reshape(1, D), mod4, mod4, w_all, mu_big, mu_small)


def _foxprep_kernel(q_ref, k_ref, v_ref, zs_ref, fb_ref, qg_ref, kg_ref, tri_ref, eexp_ref,
                    esum_ref, e8_ref, plq_ref, plk_ref, plv_ref, cq_ref, ck_ref, cv_ref,
                    qp_ref, kp_ref, vp_ref, carry_k, carry_v, carry_c, *, tiles_per_seq):
    first = (pl.program_id(0) % tiles_per_seq) == 0
    zs = zs_ref[...]
    logf = -_softplus(-(zs + fb_ref[...]))
    cum = _mm_f32(tri_ref[...], logf) + jnp.where(first, 0.0, carry_c[0:1, :])
    carry_c[0:1, :] = cum[cum.shape[0] - 1:, :]
    c_hi = cum.astype(BF16)
    r1 = cum - c_hi.astype(F32)
    c_mid = r1.astype(BF16)
    c_lo = (r1 - c_mid.astype(F32)).astype(BF16)
    a_full = _mm_f32(_sigmoid(zs), eexp_ref[...])
    k = k_ref[...].astype(F32)
    v = v_ref[...].astype(F32)
    a_k = a_full[:, 0:HALF]
    a_v = a_full[:, HALF:2 * HALF]
    k = a_k * _shift_rows(k, carry_k, first) + (1.0 - a_k) * k
    v = a_v * _shift_rows(v, carry_v, first) + (1.0 - a_v) * v
    q = q_ref[...].astype(F32)

    def head_rms(t, gain):
        ms = _mm_f32(t * t, esum_ref[...]) * (1.0 / HEAD_DIM)
        inv = _mm_f32(lax.rsqrt(ms + RMS_EPS), e8_ref[...])
        return t * inv * gain

    qn = head_rms(q, qg_ref[...]) * (HEAD_DIM ** -0.5)
    kn = head_rms(k, kg_ref[...])
    aug = [c_hi, c_mid, c_lo]
    lhs_q = jnp.concatenate([qn.astype(BF16)] + aug, axis=1)
    lhs_k = jnp.concatenate([kn.astype(BF16)] + aug, axis=1)
    qp_ref[...] = (jnp.dot(lhs_q, plq_ref[...], preferred_element_type=F32) + cq_ref[...]).astype(BF16)
    kp_ref[...] = (jnp.dot(lhs_k, plk_ref[...], preferred_element_type=F32) + ck_ref[...]).astype(BF16)
    vp_ref[...] = (jnp.dot(v.astype(BF16), plv_ref[...], preferred_element_type=F32)
                   + cv_ref[...]).astype(BF16)


def _fox_constants(tm):
    hp = N_HEADS * LANES
    eexp = np.zeros((LANES, 2 * HALF), np.float32)
    esum = np.zeros((HALF, LANES), np.float32)
    e8 = np.zeros((LANES, HALF), np.float32)
    plq = np.zeros((HALF + 3 * LANES, hp), np.float32)
    plk = np.zeros((HALF + 3 * LANES, hp), np.float32)
    plv = np.zeros((HALF, hp), np.float32)
    cq = np.zeros((1, hp), np.float32)
    ck = np.zeros((1, hp), np.float32)
    cv = np.zeros((1, hp), np.float32)
    for h in range(N_HEADS):
        sl = slice(h * HEAD_DIM, (h + 1) * HEAD_DIM)
        eexp[8 + h, sl] = 1.0
        eexp[16 + h, HALF + h * HEAD_DIM:HALF + (h + 1) * HEAD_DIM] = 1.0
        esum[sl, h] = 1.0
        e8[h, sl] = 1.0
        base = h * LANES
        for d in range(HEAD_DIM):
            plq[h * HEAD_DIM + d, base + d] = 1.0
            plk[h * HEAD_DIM + d, base + d] = 1.0
            plv[h * HEAD_DIM + d, base + d] = 1.0
        for j in range(3):
            plq[HALF + j * LANES + h, base + HEAD_DIM + j] = 1.0
            plk[HALF + j * LANES + h, base + HEAD_DIM + 3 + j] = -1.0
            cq[0, base + HEAD_DIM + 3 + j] = 1.0
            ck[0, base + HEAD_DIM + j] = 1.0
        cv[0, base + HEAD_DIM] = 1.0
    tri = np.tril(np.ones((tm, tm), np.float32))
    bf = lambda a: jnp.asarray(a, BF16)
    return dict(tri=jnp.asarray(tri), eexp=jnp.asarray(eexp), esum=jnp.asarray(esum), e8=jnp.asarray(e8),
                plq=bf(plq), plk=bf(plk), plv=bf(plv), cq=jnp.asarray(cq), ck=jnp.asarray(ck),
                cv=jnp.asarray(cv))


def _foxprep(zm, zs, forget_b, qn_g, kn_g, T):
    N = zm.shape[0]
    tm = min(256, T)
    tps = T // tm
    cst = _fox_constants(tm)
    hp = N_HEADS * LANES
    fb = jnp.zeros((1, LANES), F32).at[0, :N_HEADS].set(forget_b)
    full = lambda a: pl.BlockSpec(a.shape, lambda i: (0, 0))
    consts = [fb, qn_g.reshape(1, HALF), kn_g.reshape(1, HALF), cst["tri"], cst["eexp"], cst["esum"],
              cst["e8"], cst["plq"], cst["plk"], cst["plv"], cst["cq"], cst["ck"], cst["cv"]]
    return pl.pallas_call(
        functools.partial(_foxprep_kernel, tiles_per_seq=tps),
        grid=(N // tm,),
        in_specs=[pl.BlockSpec((tm, HALF), lambda i: (i, 0)),
                  pl.BlockSpec((tm, HALF), lambda i: (i, 1)),
                  pl.BlockSpec((tm, HALF), lambda i: (i, 2)),
                  pl.BlockSpec((tm, LANES), lambda i: (i, 0))] + [full(a) for a in consts],
        out_specs=[pl.BlockSpec((tm, hp), lambda i: (i, 0))] * 3,
        out_shape=[jax.ShapeDtypeStruct((N, hp), BF16)] * 3,
        scratch_shapes=[pltpu.VMEM((8, HALF), F32), pltpu.VMEM((8, HALF), F32),
                        pltpu.VMEM((8, LANES), F32)],
        compiler_params=_cparams(("arbitrary",)),
        name="foxprep",
    )(zm, zm, zm, zs, *consts)


def _attn_kernel(q_ref, k_ref, v_ref, g_ref, ong_ref, bd_ref, o_ref, *, tq):
    i = pl.program_id(2)
    row = lax.broadcasted_iota(I32, (tq, tq), 0)
    col = lax.broadcasted_iota(I32, (tq, tq), 1)
    causal = col <= row
    outs = []
    for hh in range(2):
        lanes = slice(hh * LANES, (hh + 1) * LANES)
        q = q_ref[:, lanes]

        def step(j, carry, masked, lanes=lanes, q=q):
            m, acc = carry
            start = pl.multiple_of(j * tq, tq)
            k = k_ref[pl.ds(start, tq), lanes]
            v = v_ref[pl.ds(start, tq), lanes]
            s = lax.dot_general(q, k, (((1,), (1,)), ((), ())), preferred_element_type=F32)
            if masked:
                s = jnp.where(causal, s, -jnp.inf)
            m_new = jnp.maximum(m, jnp.max(s, axis=1, keepdims=True))
            p = jnp.exp(s - m_new)
            acc = jnp.exp(m - m_new) * acc + jnp.dot(p.astype(BF16), v, preferred_element_type=F32)
            return m_new, acc

        init = (jnp.full((tq, 1), -jnp.inf, F32), jnp.zeros((tq, LANES), F32))
        carry = lax.fori_loop(0, i, functools.partial(step, masked=False), init)
        _, acc = step(i, carry, True)
        outs.append(acc / acc[:, HEAD_DIM:HEAD_DIM + 1])
    lane = lax.broadcasted_iota(I32, (tq, LANES), 1)
    o = jnp.where(lane < HEAD_DIM, outs[0], pltpu.roll(outs[1], HEAD_DIM, 1))
    ms = _mm_f32(o * o, bd_ref[...]) * (1.0 / HEAD_DIM)
    y = o * lax.rsqrt(ms + RMS_EPS) * ong_ref[...] * _sigmoid(g_ref[...].astype(F32))
    o_ref[...] = y.astype(BF16)


def _attention(qp, kp, vp, zm, on_g, B, T):
    N = qp.shape[0]
    tq = min(256, T)
    nq = T // tq
    pairs = N_HEADS // 2
    bd = np.zeros((LANES, LANES), np.float32)
    bd[:HEAD_DIM, :HEAD_DIM] = 1.0
    bd[HEAD_DIM:, HEAD_DIM:] = 1.0
    g_col0 = 3 * HALF // LANES
    return pl.pallas_call(
        functools.partial(_attn_kernel, tq=tq),
        grid=(B, pairs, nq),
        in_specs=[pl.BlockSpec((tq, 2 * LANES), lambda b, p, i: (b * nq + i, p)),
                  pl.BlockSpec((T, 2 * LANES), lambda b, p, i: (b, p)),
                  pl.BlockSpec((T, 2 * LANES), lambda b, p, i: (b, p)),
                  pl.BlockSpec((tq, LANES), lambda b, p, i: (b * nq + i, g_col0 + p)),
                  pl.BlockSpec((None, 1, LANES), lambda b, p, i: (p, 0, 0)),
                  pl.BlockSpec((LANES, LANES), lambda b, p, i: (0, 0))],
        out_specs=pl.BlockSpec((tq, LANES), lambda b, p, i: (b * nq + i, p)),
        out_shape=jax.ShapeDtypeStruct((N, HALF), BF16),
        compiler_params=_cparams(("arbitrary", "arbitrary", "arbitrary")),
        name="attn",
    )(qp, kp, vp, zm, on_g.reshape(pairs, 1, LANES), jnp.asarray(bd))


def _rwkv_kernel(r_ref, k_ref, v_ref, zs_ref, w0_ref, dup_ref, a0_ref, iup_ref, gup_ref, kk_ref,
                 ka_ref, rk_ref, lng_ref, lnb_ref, tri_ref, esum_ref, e8_ref, o_ref,
                 state, ybuf, *, chunk):
    C = chunk
    c_idx = pl.program_id(1)

    @pl.when(c_idx == 0)
    def _():
        state[...] = jnp.zeros_like(state)

    r = r_ref[...].astype(F32)
    k = k_ref[...].astype(F32)
    v = v_ref[...].astype(F32)
    zs = zs_ref[...]
    wd = zs[:, LANES:2 * LANES]
    ad = zs[:, 2 * LANES:3 * LANES]
    gd = zs[:, 3 * LANES:5 * LANES]
    wl = w0_ref[...] + _mm(jnp.tanh(wd), dup_ref[...])
    lw = -jnp.exp(-_softplus(-wl) - 0.5)
    a = _sigmoid(a0_ref[...] + _mm(ad, iup_ref[...]))
    g = _mm(_sigmoid(gd), gup_ref[...])

    def seg_sum(t):
        return _mm_f32(_mm_f32(t, esum_ref[...]), e8_ref[...])

    kk = k * kk_ref[...]
    kk = kk / jnp.maximum(jnp.sqrt(seg_sum(kk * kk)), 1e-12)
    k2 = k * (1.0 + (a - 1.0) * ka_ref[...])
    na = -kk
    nb = kk * a
    cl = _mm_f32(tri_ref[...], lw)
    cl_end = cl[C - 1:C, :]
    e_neg = jnp.exp(-cl)
    e_tail = jnp.exp(cl_end - cl)
    rt = r * jnp.exp(cl)
    at = na * jnp.exp(cl - lw)
    kh = k2 * e_neg
    bh = nb * e_neg
    kb = k2 * e_tail
    bb = nb * e_tail
    p_end = jnp.exp(cl_end)
    bonus = seg_sum(r * k2 * rk_ref[...]) * v

    row = lax.broadcasted_iota(I32, (C, C), 0)
    col = lax.broadcasted_iota(I32, (C, C), 1)
    lower = col <= row
    strict = col < row
    eye = (col == row).astype(F32)
    rowk = lax.broadcasted_iota(I32, (HEAD_DIM, HEAD_DIM), 0)
    colk = lax.broadcasted_iota(I32, (HEAD_DIM, HEAD_DIM), 1)
    eye_k = rowk == colk
    n_sq = int(np.log2(C)) - 1

    for h in range(N_HEADS):
        sl = slice(h * HEAD_DIM, (h + 1) * HEAD_DIM)
        rt_h, at_h, kh_h, bh_h = rt[:, sl], at[:, sl], kh[:, sl], bh[:, sl]
        kb_h, bb_h, v_h = kb[:, sl], bb[:, sl], v[:, sl]
        a_rk = jnp.where(lower, _mm_nt(rt_h, kh_h), 0.0)
        a_rb = jnp.where(lower, _mm_nt(rt_h, bh_h), 0.0)
        a_ak = jnp.where(strict, _mm_nt(at_h, kh_h), 0.0)
        a_ab = jnp.where(strict, _mm_nt(at_h, bh_h), 0.0)
        tinv = eye + a_ab
        pw = a_ab
        for _ in range(n_sq):
            pw = _mm(pw, pw)
            tinv = tinv + _mm(tinv, pw)
        at2 = _mm(tinv, at_h)
        u0 = _mm(tinv, _mm(a_ak, v_h))
        r2 = rt_h + _mm(a_rb, at2)
        y0 = _mm(a_rb, u0) + _mm(a_rk, v_h)
        m_mat = jnp.where(eye_k, p_end[:, sl], 0.0) + _mm_tn(bb_h, at2)
        g_mat = _mm_tn(bb_h, u0) + _mm_tn(kb_h, v_h)
        h0 = state[h]
        ybuf[:, sl] = _mm(r2, h0) + y0
        state[h] = _mm(m_mat, h0) + g_mat

    o = ybuf[...]
    mean = seg_sum(o) * (1.0 / HEAD_DIM)
    cen = o - mean
    var = seg_sum(cen * cen) * (1.0 / HEAD_DIM)
    o = cen * lax.rsqrt(var + LNX_EPS) * lng_ref[...] + lnb_ref[...] + bonus
    o_ref[...] = (o * g).astype(BF16)


def _rwkv(zm, zs, p, B, T):
    N = zm.shape[0]
    C = min(64, T)
    nc = T // C
    esum = np.zeros((HALF, LANES), np.float32)
    e8 = np.zeros((LANES, HALF), np.float32)
    for h in range(N_HEADS):
        esum[h * HEAD_DIM:(h + 1) * HEAD_DIM, h] = 1.0
        e8[h, h * HEAD_DIM:(h + 1) * HEAD_DIM] = 1.0
    tri = np.tril(np.ones((C, C), np.float32))
    pad_rows = lambda w, rows: jnp.zeros((rows, HALF), F32).at[:w.shape[0]].set(w)
    row = lambda a: a.reshape(1, HALF)
    consts = [row(p["rw_w0"]), pad_rows(p["rw_decay_up"], LANES), row(p["rw_a0"]),
              pad_rows(p["rw_iclr_up"], LANES), pad_rows(p["rw_gate_up"], 2 * LANES),
              row(p["rw_k_k"]), row(p["rw_k_a"]), row(p["rw_r_k"]), row(p["rw_lnx_g"]),
              row(p["rw_lnx_b"]), jnp.asarray(tri), jnp.asarray(esum), jnp.asarray(e8)]
    full = lambda a: pl.BlockSpec(a.shape, lambda b, c: (0, 0))
    rcol = 4 * HALF // HALF
    return pl.pallas_call(
        functools.partial(_rwkv_kernel, chunk=C),
        grid=(B, nc),
        in_specs=[pl.BlockSpec((C, HALF), lambda b, c: (b * nc + c, rcol)),
                  pl.BlockSpec((C, HALF), lambda b, c: (b * nc + c, rcol + 1)),
                  pl.BlockSpec((C, HALF), lambda b, c: (b * nc + c, rcol + 2)),
                  pl.BlockSpec((C, Z_SMALL), lambda b, c: (b * nc + c, 0))] + [full(a) for a in consts],
        out_specs=pl.BlockSpec((C, HALF), lambda b, c: (b * nc + c, 0)),
        out_shape=jax.ShapeDtypeStruct((N, HALF), BF16),
        scratch_shapes=[pltpu.VMEM((N_HEADS, HEAD_DIM, HEAD_DIM), F32), pltpu.VMEM((C, HALF), F32)],
        compiler_params=_cparams(("arbitrary", "arbitrary")),
        name="rwkv",
    )(zm, zm, zm, zs, *consts)


def _first_index(mask, iota, big):
    return jnp.min(jnp.where(mask, iota, big), axis=0, keepdims=True)


def _outproj_kernel(x_ref, yf_ref, yr_ref, g1_ref, sh2_ref, sc2_ref, g2_ref, n2g_ref, wo_ref, rwt_ref,
                    rb_ref, swg_ref, swu_ref, swd_ref, ustrict_ref,
                    xs_ref, hp_ref, idx_ref, wts_ref, rank_ref, cnt_ref, carry, *, tm):
    @pl.when(pl.program_id(0) == 0)
    def _():
        carry[...] = jnp.zeros_like(carry)

    d = (jnp.dot(yf_ref[...], wo_ref[0:HALF, :], preferred_element_type=F32)
         + jnp.dot(yr_ref[...], wo_ref[HALF:2 * HALF, :], preferred_element_type=F32))
    x2 = x_ref[...] + g1_ref[...] * d
    h = x2 * lax.rsqrt(jnp.mean(x2 * x2, axis=-1, keepdims=True) + RMS_EPS) * n2g_ref[...]
    h = h * (1.0 + sc2_ref[...]) + sh2_ref[...]
    hb = h.astype(BF16)
    act = _silu(jnp.dot(hb, swg_ref[...], preferred_element_type=F32)) * jnp.dot(
        hb, swu_ref[...], preferred_element_type=F32)
    shared = jnp.dot(act.astype(BF16), swd_ref[...], preferred_element_type=F32)
    xs_ref[...] = x2 + g2_ref[...] * shared
    half = D_MODEL // 2
    hp_ref[...] = _pack_halves(h)

    logits = lax.dot_general(rwt_ref[...], h, (((1,), (1,)), ((), ())), precision=HIGHEST,
                             preferred_element_type=F32)
    scores = _sigmoid(logits)
    sel = scores + rb_ref[:, 0:1]
    neg = -jnp.inf
    sel3 = sel.reshape(N_GROUPS, GROUP_SIZE, tm)
    io_in = lax.broadcasted_iota(I32, (N_GROUPS, GROUP_SIZE, tm), 1)
    m1 = jnp.max(sel3, axis=1, keepdims=True)
    f1 = jnp.min(jnp.where(sel3 == m1, io_in, GROUP_SIZE), axis=1, keepdims=True)
    m2 = jnp.max(jnp.where(io_in == f1, neg, sel3), axis=1, keepdims=True)
    gs = (m1 + m2).reshape(N_GROUPS, tm)
    io_g = lax.broadcasted_iota(I32, (N_GROUPS, tm), 0)
    gmask = jnp.zeros((N_GROUPS, tm), jnp.bool_)
    for _ in range(TOPK_GROUPS):
        mg = jnp.max(gs, axis=0, keepdims=True)
        fg = _first_index(gs == mg, io_g, N_GROUPS)
        pick = io_g == fg
        gmask = jnp.logical_or(gmask, pick)
        gs = jnp.where(pick, neg, gs)
    emask = jnp.broadcast_to(gmask.reshape(N_GROUPS, 1, tm), (N_GROUPS, GROUP_SIZE, tm)).reshape(N_EXPERTS, tm)
    cur = jnp.where(emask, sel, neg)
    io_e = lax.broadcasted_iota(I32, (N_EXPERTS, tm), 0)
    picks, idxs, wts = [], [], []
    for _ in range(TOP_K):
        me = jnp.max(cur, axis=0, keepdims=True)
        fe = _first_index(cur == me, io_e, N_EXPERTS)
        pick = io_e == fe
        picks.append(pick)
        idxs.append(fe)
        wts.append(jnp.sum(jnp.where(pick, scores, 0.0), axis=0, keepdims=True))
        cur = jnp.where(pick, neg, cur)
    wsum = wts[0]
    for w in wts[1:]:
        wsum = wsum + w
    zero_i = jnp.zeros((1, tm), I32)
    zero_f = jnp.zeros((1, tm), F32)
    idx_ref[...] = jnp.concatenate(idxs + [zero_i, zero_i], axis=0)
    wts_ref[...] = jnp.concatenate([w / wsum * ROUTED_SCALE for w in wts] + [zero_f, zero_f], axis=0)
    cnt = picks[0].astype(F32)
    for pk in picks[1:]:
        cnt = cnt + pk.astype(F32)
    excl = jnp.dot(cnt.astype(BF16), ustrict_ref[...], preferred_element_type=F32) + carry[:, 0:1]
    ranks = [jnp.sum(jnp.where(pk, excl, 0.0), axis=0, keepdims=True).astype(I32) for pk in picks]
    rank_ref[...] = jnp.concatenate(ranks + [zero_i, zero_i], axis=0)
    total = carry[...] + jnp.sum(cnt, axis=1, keepdims=True)
    carry[...] = total
    cnt_ref[...] = total.astype(I32)


def _outproj(x2d, yf, yr, mod4, norm2_g, w_out, router_w, router_bias, swg, swu, swd, T):
    N, D = x2d.shape
    tm = min(512, T)
    tps = T // tm
    vec = lambda j: pl.BlockSpec((None, None, 1, D), lambda i: (i // tps, j, 0, 0))
    full = lambda a: pl.BlockSpec(a.shape, lambda i: (0, 0))
    ustrict = jnp.asarray(np.triu(np.ones((tm, tm), np.float32), 1), BF16)
    rb = jnp.broadcast_to(router_bias.reshape(N_EXPERTS, 1), (N_EXPERTS, LANES))
    consts = [norm2_g.reshape(1, D), w_out.astype(BF16), router_w.T, rb, swg.astype(BF16),
              swu.astype(BF16), swd.astype(BF16), ustrict]
    small = lambda dt: jax.ShapeDtypeStruct((8, N), dt)
    return pl.pallas_call(
        functools.partial(_outproj_kernel, tm=tm),
        grid=(N // tm,),
        in_specs=[pl.BlockSpec((tm, D), lambda i: (i, 0)),
                  pl.BlockSpec((tm, HALF), lambda i: (i, 0)),
                  pl.BlockSpec((tm, HALF), lambda i: (i, 0)),
                  vec(2), vec(3), vec(4), vec(5)] + [full(a) for a in consts],
        out_specs=[pl.BlockSpec((tm, D), lambda i: (i, 0)),
                   pl.BlockSpec((tm, D // 2), lambda i: (i, 0)),
                   pl.BlockSpec((8, tm), lambda i: (0, i)),
                   pl.BlockSpec((8, tm), lambda i: (0, i)),
                   pl.BlockSpec((8, tm), lambda i: (0, i)),
                   pl.BlockSpec((N_EXPERTS, LANES), lambda i: (0, 0))],
        out_shape=[jax.ShapeDtypeStruct((N, D), F32), jax.ShapeDtypeStruct((N, D // 2), jnp.uint32),
                   small(I32), small(F32), small(I32),
                   jax.ShapeDtypeStruct((N_EXPERTS, LANES), I32)],
        scratch_shapes=[pltpu.VMEM((N_EXPERTS, LANES), F32)],
        compiler_params=_cparams(("arbitrary",)),
        name="outproj",
    )(x2d, yf, yr, mod4, mod4, mod4, mod4, *consts)


def _dispatch_kernel(pend_ref, padded_ref, nu_ref, dest_ref, hp_ref, xs_ref, zeros, sem, zsem, *, tm, n_blocks):
    @pl.when(pl.program_id(0) == 0)
    def _():
        zeros[...] = jnp.zeros_like(zeros)

        def block_copy(start):
            return pltpu.make_async_copy(zeros, xs_ref.at[pl.ds(pl.multiple_of(start, EXPERT_BLOCK),
                                                                 EXPERT_BLOCK), :], zsem)

        def pad_loop(fn):
            def body(e, _):
                @pl.when(padded_ref[e] > 0)
                def _():
                    fn(block_copy(pend_ref[e] - EXPERT_BLOCK))
                return 0
            lax.fori_loop(0, N_EXPERTS, body, 0)

        def tail_loop(fn):
            def body(b, _):
                fn(block_copy(b * EXPERT_BLOCK))
                return 0
            lax.fori_loop(nu_ref[0], n_blocks, body, 0)

        pad_loop(lambda cp: cp.start())
        tail_loop(lambda cp: cp.start())
        pad_loop(lambda cp: cp.wait())
        tail_loop(lambda cp: cp.wait())

    def row_copy(j, k):
        return pltpu.make_async_copy(hp_ref.at[pl.ds(j, 1), :],
                                     xs_ref.at[pl.ds(dest_ref[0, k, j], 1), :], sem)

    def issue(j, _):
        for k in range(TOP_K):
            row_copy(j, k).start()
        return 0

    def drain(j, _):
        for k in range(TOP_K):
            row_copy(j, k).wait()
        return 0

    lax.fori_loop(0, tm, issue, 0)
    lax.fori_loop(0, tm, drain, 0)


def _dispatch(hp, dest, pad_end, padded, n_used, n_rows):
    N, W = hp.shape
    tm = min(256, N)
    n_blocks = n_rows // EXPERT_BLOCK
    dest3 = dest.reshape(8, N // tm, tm).transpose(1, 0, 2)
    return pl.pallas_call(
        functools.partial(_dispatch_kernel, tm=tm, n_blocks=n_blocks),
        grid_spec=pltpu.PrefetchScalarGridSpec(
            num_scalar_prefetch=3, grid=(N // tm,),
            in_specs=[pl.BlockSpec((1, 8, tm), lambda i, *_: (i, 0, 0), memory_space=pltpu.SMEM),
                      pl.BlockSpec((tm, W), lambda i, *_: (i, 0))],
            out_specs=pl.BlockSpec(memory_space=pl.ANY),
            scratch_shapes=[pltpu.VMEM((EXPERT_BLOCK, W), jnp.uint32), pltpu.SemaphoreType.DMA,
                            pltpu.SemaphoreType.DMA]),
        out_shape=jax.ShapeDtypeStruct((n_rows, W), jnp.uint32),
        compiler_params=_cparams(("arbitrary",)),
        name="dispatch",
    )(pad_end, padded, n_used, dest3, hp)


def _experts_kernel(be_ref, nu_ref, xs_ref, wg_ref, wu_ref, wd_ref, ys_ref):
    del be_ref
    live = pl.program_id(0) < nu_ref[0]

    @pl.when(jnp.logical_not(live))
    def _():
        ys_ref[...] = jnp.zeros_like(ys_ref)

    @pl.when(live)
    def _():
        half = D_MODEL // 2
        lo, hi = _unpack_halves(xs_ref[...])
        lo = lo.astype(BF16)
        hi = hi.astype(BF16)
        gate = (jnp.dot(lo, wg_ref[0:half, :], preferred_element_type=F32)
                + jnp.dot(hi, wg_ref[half:D_MODEL, :], preferred_element_type=F32))
        up = (jnp.dot(lo, wu_ref[0:half, :], preferred_element_type=F32)
              + jnp.dot(hi, wu_ref[half:D_MODEL, :], preferred_element_type=F32))
        y = jnp.dot((_silu(gate) * up).astype(BF16), wd_ref[...], preferred_element_type=F32)
        ys_ref[...] = _pack_halves(y)


def _experts(xs, blk_e, n_used, wg, wu, wd):
    n_rows, W = xs.shape
    n_blocks = n_rows // EXPERT_BLOCK
    row_map = lambda i, be, nu: (jnp.minimum(i, nu[0] - 1), 0)
    return pl.pallas_call(
        _experts_kernel,
        grid_spec=pltpu.PrefetchScalarGridSpec(
            num_scalar_prefetch=2, grid=(n_blocks,),
            in_specs=[pl.BlockSpec((EXPERT_BLOCK, W), row_map),
                      pl.BlockSpec((None, D_MODEL, D_EXPERT), lambda i, be, nu: (be[i], 0, 0)),
                      pl.BlockSpec((None, D_MODEL, D_EXPERT), lambda i, be, nu: (be[i], 0, 0)),
                      pl.BlockSpec((None, D_EXPERT, D_MODEL), lambda i, be, nu: (be[i], 0, 0))],
            out_specs=pl.BlockSpec((EXPERT_BLOCK, W), lambda i, be, nu: (i, 0))),
        out_shape=jax.ShapeDtypeStruct((n_rows, W), jnp.uint32),
        compiler_params=_cparams(("arbitrary",)),
        name="experts",
    )(blk_e, n_used, xs, wg, wu, wd)


def _combine_kernel(dest_ref, ys_ref, xs_ref, wts_ref, g2_ref, fg_ref, o_ref, buf, sem, *, tm):
    def row_copy(j, k):
        return pltpu.make_async_copy(ys_ref.at[pl.ds(dest_ref[0, k, j], 1), :],
                                     buf.at[k, pl.ds(j, 1), :], sem)

    def issue(j, _):
        for k in range(TOP_K):
            row_copy(j, k).start()
        return 0

    def drain(j, _):
        for k in range(TOP_K):
            row_copy(j, k).wait()
        return 0

    lax.fori_loop(0, tm, issue, 0)
    lax.fori_loop(0, tm, drain, 0)
    w = wts_ref[...]
    lo = jnp.zeros((tm, D_MODEL // 2), F32)
    hi = jnp.zeros((tm, D_MODEL // 2), F32)
    for k in range(TOP_K):
        yk_lo, yk_hi = _unpack_halves(buf[k])
        wk = w[:, k:k + 1]
        lo = lo + wk * yk_lo
        hi = hi + wk * yk_hi
    x3 = xs_ref[...] + g2_ref[...] * jnp.concatenate([lo, hi], axis=1)
    y = x3 * lax.rsqrt(jnp.mean(x3 * x3, axis=-1, keepdims=True) + RMS_EPS) * fg_ref[...]
    o_ref[...] = y


def _combine(ys, dest, xsr, wts_t, mod4, final_g, T):
    N, D = xsr.shape
    W = ys.shape[1]
    tm = min(128, T)
    tps = T // tm
    dest3 = dest.reshape(8, N // tm, tm).transpose(1, 0, 2)
    return pl.pallas_call(
        functools.partial(_combine_kernel, tm=tm),
        grid=(N // tm,),
        in_specs=[pl.BlockSpec((1, 8, tm), lambda i: (i, 0, 0), memory_space=pltpu.SMEM),
                  pl.BlockSpec(memory_space=pl.ANY),
                  pl.BlockSpec((tm, D), lambda i: (i, 0)),
                  pl.BlockSpec((tm, 8), lambda i: (i, 0)),
                  pl.BlockSpec((None, None, 1, D), lambda i: (i // tps, 5, 0, 0)),
                  pl.BlockSpec((1, D), lambda i: (0, 0))],
        out_specs=pl.BlockSpec((tm, D), lambda i: (i, 0)),
        out_shape=jax.ShapeDtypeStruct((N, D), F32),
        scratch_shapes=[pltpu.VMEM((TOP_K, tm, W), jnp.uint32), pltpu.SemaphoreType.DMA],
        compiler_params=_cparams(("arbitrary",)),
        name="combine",
    )(dest3, ys, xsr, wts_t, mod4, final_g.reshape(1, D))


def _pack_w_in(w_in):
    D = w_in.shape[0]
    fox_cols = 4 * HALF + 3 * N_HEADS
    wf = w_in[:, :fox_cols]
    wr = w_in[:, fox_cols:]
    o = 3 * HALF
    pad = lambda a, n: jnp.concatenate([a, jnp.zeros((D, n - a.shape[1]), a.dtype)], axis=1)
    parts = [wf[:, :4 * HALF], wr[:, :o],
             pad(wf[:, 4 * HALF:], LANES),
             pad(wr[:, o:o + DECAY_LORA], LANES),
             pad(wr[:, o + DECAY_LORA:o + DECAY_LORA + ICLR_LORA], LANES),
             pad(wr[:, o + DECAY_LORA + ICLR_LORA:], 2 * LANES)]
    return jnp.concatenate(parts, axis=1).astype(BF16)


def _pack_mu(mu):
    o = 3 * HALF
    pad = lambda a, n: jnp.concatenate([a, jnp.zeros((n - a.shape[0],), a.dtype)])
    small = jnp.concatenate([jnp.zeros((LANES,), mu.dtype),
                             pad(mu[o:o + DECAY_LORA], LANES),
                             pad(mu[o + DECAY_LORA:o + DECAY_LORA + ICLR_LORA], LANES),
                             pad(mu[o + DECAY_LORA + ICLR_LORA:], 2 * LANES)])
    return mu[:o].reshape(1, o), small.reshape(1, Z_SMALL)


def kernel(x, c, norm1_g, norm2_g, ada_w, ada_b, w_in, w_out, fox_qn_g, fox_kn_g, fox_on_g, fox_forget_b,
           rw_mu, rw_w0, rw_decay_up, rw_a0, rw_iclr_up, rw_gate_up, rw_k_k, rw_k_a, rw_r_k, rw_lnx_g,
           rw_lnx_b, router_w, router_bias, exp_w_gate, exp_w_up, exp_w_down, sh_w_gate, sh_w_up,
           sh_w_down, final_g):
    B, T, D = x.shape
    N = B * T
    depth = norm1_g.shape[0]
    assert depth == 1, "the combine kernel fuses the final RMSNorm, so exactly one layer is supported"
    xf = x.reshape(N, D)
    for l in range(depth):
        mod4 = _ada(c, ada_w[l], ada_b[l]).reshape(B, 6, 1, D)
        mu_big, mu_small = _pack_mu(rw_mu[l])
        zm, zs = _inproj(xf, mod4, norm1_g[l], _pack_w_in(w_in[l]), mu_big, mu_small, T)
        qp, kp, vp = _foxprep(zm, zs, fox_forget_b[l], fox_qn_g[l], fox_kn_g[l], T)
        y_fox = _attention(qp, kp, vp, zm, fox_on_g[l], B, T)
        rw = dict(rw_w0=rw_w0[l], rw_decay_up=rw_decay_up[l], rw_a0=rw_a0[l], rw_iclr_up=rw_iclr_up[l],
                  rw_gate_up=rw_gate_up[l], rw_k_k=rw_k_k[l], rw_k_a=rw_k_a[l], rw_r_k=rw_r_k[l],
                  rw_lnx_g=rw_lnx_g[l], rw_lnx_b=rw_lnx_b[l])
        y_rwkv = _rwkv(zm, zs, rw, B, T)
        xsr, hp, idx_t, wts_t, rank_t, cnt = _outproj(
            xf, y_fox, y_rwkv, mod4, norm2_g[l], w_out[l], router_w[l], router_bias[l],
            sh_w_gate[l], sh_w_up[l], sh_w_down[l], T)
        counts = cnt[:, 0]
        padded = (counts + EXPERT_BLOCK - 1) // EXPERT_BLOCK * EXPERT_BLOCK
        pad_end = jnp.cumsum(padded).astype(I32)
        pad_start = pad_end - padded
        n_blocks = -(-(N * TOP_K) // EXPERT_BLOCK) + N_EXPERTS
        n_rows = n_blocks * EXPERT_BLOCK
        dest = jnp.take(pad_start, idx_t, axis=0) + rank_t
        blk_start = jnp.arange(n_blocks, dtype=I32) * EXPERT_BLOCK
        blk_e = jnp.minimum(jnp.searchsorted(pad_end, blk_start, side="right"), N_EXPERTS - 1).astype(I32)
        n_used = (pad_end[N_EXPERTS - 1:] // EXPERT_BLOCK).astype(I32)
        xs = _dispatch(hp, dest, pad_end, padded.astype(I32), n_used, n_rows)
        ys = _experts(xs, blk_e, n_used, exp_w_gate[l].astype(BF16), exp_w_up[l].astype(BF16),
                      exp_w_down[l].astype(BF16))
        xf = _combine(ys, dest, xsr, wts_t.T, mod4, final_g, T)
    return xf.reshape(B, T, D)
```

```python
import functools

import jax
import jax.numpy as jnp
import numpy as np
from jax import lax
from jax.experimental import pallas as pl
from jax.experimental.pallas import tpu as pltpu

F32 = jnp.float32
BF16 = jnp.bfloat16
I32 = jnp.int32
HIGHEST = lax.Precision.HIGHEST

D_MODEL = 1024
HEAD_DIM = 64
N_HEADS = 8
HALF = N_HEADS * HEAD_DIM
RMS_EPS = 1e-6
LNX_EPS = 64e-5
DECAY_LORA = 64
ICLR_LORA = 64
GATE_LORA = 160
N_EXPERTS = 64
N_GROUPS = 8
GROUP_SIZE = N_EXPERTS // N_GROUPS
TOPK_GROUPS = 4
TOP_K = 6
D_EXPERT = 256
D_SHARED = 256
ROUTED_SCALE = 2.5
EXPERT_BLOCK = 256

LANES = 128
Z_MAIN = 4 * HALF + 3 * HALF
Z_SMALL = 5 * LANES
VMEM_LIMIT = 56 * 1024 * 1024
RWKV_CHUNK = 64
RWKV_SEQS_PER_STEP = 2


def _cparams(semantics):
    return pltpu.CompilerParams(dimension_semantics=semantics, vmem_limit_bytes=VMEM_LIMIT)


def _mm(a, b):
    return jnp.dot(a.astype(BF16), b.astype(BF16), preferred_element_type=F32)


def _mm_nt(a, b):
    return lax.dot_general(a.astype(BF16), b.astype(BF16), (((1,), (1,)), ((), ())),
                           preferred_element_type=F32)


def _mm_tn(a, b):
    return lax.dot_general(a.astype(BF16), b.astype(BF16), (((0,), (0,)), ((), ())),
                           preferred_element_type=F32)


def _mm_f32(a, b):
    return jnp.dot(a, b, precision=HIGHEST, preferred_element_type=F32)


def _sigmoid(x):
    return 1.0 / (1.0 + jnp.exp(-x))


def _softplus(x):
    return jnp.maximum(x, 0.0) + jnp.log(1.0 + jnp.exp(-jnp.abs(x)))


def _silu(x):
    return x * _sigmoid(x)


def _pack_halves(x):
    w = x.shape[1] // 2
    bits = lambda t: lax.bitcast_convert_type(t.astype(jnp.bfloat16).astype(F32), jnp.uint32)
    return (bits(x[:, 0:w]) >> 16) | (bits(x[:, w:2 * w]) & jnp.uint32(0xFFFF0000))


def _unpack_halves(p):
    lo = lax.bitcast_convert_type(p << 16, F32)
    hi = lax.bitcast_convert_type(p & jnp.uint32(0xFFFF0000), F32)
    return lo, hi


def _shift_rows(z, carry_ref, first):
    rows = z.shape[0]
    prev_row = jnp.where(first, 0.0, carry_ref[0:1, :])
    prev = pltpu.roll(z, 1, 0)
    row0 = lax.broadcasted_iota(I32, (rows, 1), 0) == 0
    prev = jnp.where(row0, prev_row, prev)
    carry_ref[0:1, :] = z[rows - 1:rows, :]
    return prev


def _ada_kernel(c_ref, w_ref, b_ref, o_ref):
    o_ref[...] = _mm_f32(_silu(c_ref[...]), w_ref[...]) + b_ref[...]


def _ada(c, ada_w, ada_b):
    B, D = c.shape
    n_out = ada_w.shape[1]
    tn = 512
    return pl.pallas_call(
        _ada_kernel,
        grid=(n_out // tn,),
        in_specs=[pl.BlockSpec((B, D), lambda j: (0, 0)),
                  pl.BlockSpec((D, tn), lambda j: (0, j)),
                  pl.BlockSpec((1, tn), lambda j: (0, j))],
        out_specs=pl.BlockSpec((B, tn), lambda j: (0, j)),
        out_shape=jax.ShapeDtypeStruct((B, n_out), F32),
        compiler_params=_cparams(("arbitrary",)),
        name="ada",
    )(c, ada_w, ada_b.reshape(1, n_out))


def _inproj_kernel(x_ref, g_ref, sh_ref, sc_ref, w_ref, mub_ref, mus_ref, zm_ref, zs_ref,
                   carry_b, carry_s, *, tiles_per_seq):
    first = (pl.program_id(0) % tiles_per_seq) == 0
    x = x_ref[...]
    h = x * lax.rsqrt(jnp.mean(x * x, axis=-1, keepdims=True) + RMS_EPS) * g_ref[...]
    hb = (h * (1.0 + sc_ref[...]) + sh_ref[...]).astype(BF16)
    nf = 4 * HALF
    zm_ref[:, 0:nf] = jnp.dot(hb, w_ref[:, 0:nf], preferred_element_type=F32).astype(BF16)
    zr = jnp.dot(hb, w_ref[:, nf:Z_MAIN], preferred_element_type=F32)
    zr = zr + mub_ref[...] * (_shift_rows(zr, carry_b, first) - zr)
    zm_ref[:, nf:Z_MAIN] = zr.astype(BF16)
    zs = jnp.dot(hb, w_ref[:, Z_MAIN:Z_MAIN + Z_SMALL], preferred_element_type=F32)
    zs_ref[...] = zs + mus_ref[...] * (_shift_rows(zs, carry_s, first) - zs)


def _inproj(x2d, mod4, norm_g, w_all, mu_big, mu_small, T):
    N, D = x2d.shape
    tm = min(512, T)
    tps = T // tm
    vec = lambda j: pl.BlockSpec((None, None, 1, D), lambda i: (i // tps, j, 0, 0))
    return pl.pallas_call(
        functools.partial(_inproj_kernel, tiles_per_seq=tps),
        grid=(N // tm,),
        in_specs=[pl.BlockSpec((tm, D), lambda i: (i, 0)),
                  pl.BlockSpec((1, D), lambda i: (0, 0)),
                  vec(0), vec(1),
                  pl.BlockSpec((D, Z_MAIN + Z_SMALL), lambda i: (0, 0)),
                  pl.BlockSpec((1, 3 * HALF), lambda i: (0, 0)),
                  pl.BlockSpec((1, Z_SMALL), lambda i: (0, 0))],
        out_specs=[pl.BlockSpec((tm, Z_MAIN), lambda i: (i, 0)),
                   pl.BlockSpec((tm, Z_SMALL), lambda i: (i, 0))],
        out_shape=[jax.ShapeDtypeStruct((N, Z_MAIN), BF16),
                   jax.ShapeDtypeStruct((N, Z_SMALL), F32)],
        scratch_shapes=[pltpu.VMEM((8, 3 * HALF), F32), pltpu.VMEM((8, Z_SMALL), F32)],
        compiler_params=_cparams(("arbitrary",)),
        name="inproj",
    )(x2d, norm_g.reshape(1, D), mod4, mod4, w_all, mu_big, mu_small)


def _foxprep_kernel(q_ref, k_ref, v_ref, zs_ref, fb_ref, qg_ref, kg_ref, tri_ref, eexp_ref,
                    esum_ref, e8_ref, plq_ref, plk_ref, plv_ref, cq_ref, ck_ref, cv_ref,
                    qp_ref, kp_ref, vp_ref, carry_k, carry_v, carry_c, *, tiles_per_seq):
    first = (pl.program_id(0) % tiles_per_seq) == 0
    zs = zs_ref[...]
    logf = -_softplus(-(zs + fb_ref[...]))
    cum = _mm_f32(tri_ref[...], logf) + jnp.where(first, 0.0, carry_c[0:1, :])
    carry_c[0:1, :] = cum[cum.shape[0] - 1:, :]
    c_hi = cum.astype(BF16)
    r1 = cum - c_hi.astype(F32)
    c_mid = r1.astype(BF16)
    c_lo = (r1 - c_mid.astype(F32)).astype(BF16)
    a_full = _mm_f32(_sigmoid(zs), eexp_ref[...])
    k = k_ref[...].astype(F32)
    v = v_ref[...].astype(F32)
    a_k = a_full[:, 0:HALF]
    a_v = a_full[:, HALF:2 * HALF]
    k = a_k * _shift_rows(k, carry_k, first) + (1.0 - a_k) * k
    v = a_v * _shift_rows(v, carry_v, first) + (1.0 - a_v) * v
    q = q_ref[...].astype(F32)

    def head_rms(t, gain):
        ms = _mm_f32(t * t, esum_ref[...]) * (1.0 / HEAD_DIM)
        inv = _mm_f32(lax.rsqrt(ms + RMS_EPS), e8_ref[...])
        return t * inv * gain

    qn = head_rms(q, qg_ref[...]) * (HEAD_DIM ** -0.5)
    kn = head_rms(k, kg_ref[...])
    aug = [c_hi, c_mid, c_lo]
    lhs_q = jnp.concatenate([qn.astype(BF16)] + aug, axis=1)
    lhs_k = jnp.concatenate([kn.astype(BF16)] + aug, axis=1)
    qp_ref[...] = (jnp.dot(lhs_q, plq_ref[...], preferred_element_type=F32) + cq_ref[...]).astype(BF16)
    kp_ref[...] = (jnp.dot(lhs_k, plk_ref[...], preferred_element_type=F32) + ck_ref[...]).astype(BF16)
    vp_ref[...] = (jnp.dot(v.astype(BF16), plv_ref[...], preferred_element_type=F32)
                   + cv_ref[...]).astype(BF16)


def _fox_constants(tm):
    hp = N_HEADS * LANES
    eexp = np.zeros((LANES, 2 * HALF), np.float32)
    esum = np.zeros((HALF, LANES), np.float32)
    e8 = np.zeros((LANES, HALF), np.float32)
    plq = np.zeros((HALF + 3 * LANES, hp), np.float32)
    plk = np.zeros((HALF + 3 * LANES, hp), np.float32)
    plv = np.zeros((HALF, hp), np.float32)
    cq = np.zeros((1, hp), np.float32)
    ck = np.zeros((1, hp), np.float32)
    cv = np.zeros((1, hp), np.float32)
    for h in range(N_HEADS):
        sl = slice(h * HEAD_DIM, (h + 1) * HEAD_DIM)
        eexp[8 + h, sl] = 1.0
        eexp[16 + h, HALF + h * HEAD_DIM:HALF + (h + 1) * HEAD_DIM] = 1.0
        esum[sl, h] = 1.0
        e8[h, sl] = 1.0
        base = h * LANES
        for d in range(HEAD_DIM):
            plq[h * HEAD_DIM + d, base + d] = 1.0
            plk[h * HEAD_DIM + d, base + d] = 1.0
            plv[h * HEAD_DIM + d, base + d] = 1.0
        for j in range(3):
            plq[HALF + j * LANES + h, base + HEAD_DIM + j] = 1.0
            plk[HALF + j * LANES + h, base + HEAD_DIM + 3 + j] = -1.0
            cq[0, base + HEAD_DIM + 3 + j] = 1.0
            ck[0, base + HEAD_DIM + j] = 1.0
        cv[0, base + HEAD_DIM] = 1.0
    tri = np.tril(np.ones((tm, tm), np.float32))
    bf = lambda a: jnp.asarray(a, BF16)
    return dict(tri=jnp.asarray(tri), eexp=jnp.asarray(eexp), esum=jnp.asarray(esum), e8=jnp.asarray(e8),
                plq=bf(plq), plk=bf(plk), plv=bf(plv), cq=jnp.asarray(cq), ck=jnp.asarray(ck),
                cv=jnp.asarray(cv))


def _foxprep(zm, zs, forget_b, qn_g, kn_g, T):
    N = zm.shape[0]
    tm = min(256, T)
    tps = T // tm
    cst = _fox_constants(tm)
    hp = N_HEADS * LANES
    fb = jnp.zeros((1, LANES), F32).at[0, :N_HEADS].set(forget_b)
    full = lambda a: pl.BlockSpec(a.shape, lambda i: (0, 0))
    consts = [fb, qn_g.reshape(1, HALF), kn_g.reshape(1, HALF), cst["tri"], cst["eexp"], cst["esum"],
              cst["e8"], cst["plq"], cst["plk"], cst["plv"], cst["cq"], cst["ck"], cst["cv"]]
    return pl.pallas_call(
        functools.partial(_foxprep_kernel, tiles_per_seq=tps),
        grid=(N // tm,),
        in_specs=[pl.BlockSpec((tm, HALF), lambda i: (i, 0)),
                  pl.BlockSpec((tm, HALF), lambda i: (i, 1)),
                  pl.BlockSpec((tm, HALF), lambda i: (i, 2)),
                  pl.BlockSpec((tm, LANES), lambda i: (i, 0))] + [full(a) for a in consts],
        out_specs=[pl.BlockSpec((tm, hp), lambda i: (i, 0))] * 3,
        out_shape=[jax.ShapeDtypeStruct((N, hp), BF16)] * 3,
        scratch_shapes=[pltpu.VMEM((8, HALF), F32), pltpu.VMEM((8, HALF), F32),
                        pltpu.VMEM((8, LANES), F32)],
        compiler_params=_cparams(("arbitrary",)),
        name="foxprep",
    )(zm, zm, zm, zs, *consts)


def _attn_kernel(q_ref, k_ref, v_ref, g_ref, ong_ref, bd_ref, o_ref, *, tq):
    i = pl.program_id(2)
    row = lax.broadcasted_iota(I32, (tq, tq), 0)
    col = lax.broadcasted_iota(I32, (tq, tq), 1)
    causal = col <= row
    outs = []
    for hh in range(2):
        lanes = slice(hh * LANES, (hh + 1) * LANES)
        q = q_ref[:, lanes]

        def step(j, carry, masked, lanes=lanes, q=q):
            m, acc = carry
            start = pl.multiple_of(j * tq, tq)
            k = k_ref[pl.ds(start, tq), lanes]
            v = v_ref[pl.ds(start, tq), lanes]
            s = lax.dot_general(q, k, (((1,), (1,)), ((), ())), preferred_element_type=F32)
            if masked:
                s = jnp.where(causal, s, -jnp.inf)
            m_new = jnp.maximum(m, jnp.max(s, axis=1, keepdims=True))
            p = jnp.exp(s - m_new)
            acc = jnp.exp(m - m_new) * acc + jnp.dot(p.astype(BF16), v, preferred_element_type=F32)
            return m_new, acc

        init = (jnp.full((tq, 1), -jnp.inf, F32), jnp.zeros((tq, LANES), F32))
        carry = lax.fori_loop(0, i, functools.partial(step, masked=False), init)
        _, acc = step(i, carry, True)
        outs.append(acc / acc[:, HEAD_DIM:HEAD_DIM + 1])
    lane = lax.broadcasted_iota(I32, (tq, LANES), 1)
    o = jnp.where(lane < HEAD_DIM, outs[0], pltpu.roll(outs[1], HEAD_DIM, 1))
    ms = _mm_f32(o * o, bd_ref[...]) * (1.0 / HEAD_DIM)
    y = o * lax.rsqrt(ms + RMS_EPS) * ong_ref[...] * _sigmoid(g_ref[...].astype(F32))
    o_ref[...] = y.astype(BF16)


def _attention(qp, kp, vp, zm, on_g, B, T):
    N = qp.shape[0]
    tq = min(256, T)
    nq = T // tq
    pairs = N_HEADS // 2
    bd = np.zeros((LANES, LANES), np.float32)
    bd[:HEAD_DIM, :HEAD_DIM] = 1.0
    bd[HEAD_DIM:, HEAD_DIM:] = 1.0
    g_col0 = 3 * HALF // LANES
    return pl.pallas_call(
        functools.partial(_attn_kernel, tq=tq),
        grid=(B, pairs, nq),
        in_specs=[pl.BlockSpec((tq, 2 * LANES), lambda b, p, i: (b * nq + i, p)),
                  pl.BlockSpec((T, 2 * LANES), lambda b, p, i: (b, p)),
                  pl.BlockSpec((T, 2 * LANES), lambda b, p, i: (b, p)),
                  pl.BlockSpec((tq, LANES), lambda b, p, i: (b * nq + i, g_col0 + p)),
                  pl.BlockSpec((None, 1, LANES), lambda b, p, i: (p, 0, 0)),
                  pl.BlockSpec((LANES, LANES), lambda b, p, i: (0, 0))],
        out_specs=pl.BlockSpec((tq, LANES), lambda b, p, i: (b * nq + i, p)),
        out_shape=jax.ShapeDtypeStruct((N, HALF), BF16),
        compiler_params=_cparams(("arbitrary", "arbitrary", "arbitrary")),
        name="attn",
    )(qp, kp, vp, zm, on_g.reshape(pairs, 1, LANES), jnp.asarray(bd))


def _mm_split(m01, x):
    hi = x.astype(BF16)
    r1 = x - hi.astype(F32)
    mid = r1.astype(BF16)
    lo = (r1 - mid.astype(F32)).astype(BF16)
    dot = lambda t: jnp.dot(m01, t, preferred_element_type=F32)
    return dot(hi) + dot(mid) + dot(lo)


def _rwkv_kernel(r_ref, k_ref, v_ref, zs_ref, w0_ref, dup_ref, a0_ref, iup_ref, gup_ref, kk_ref,
                 ka_ref, rk_ref, lng_ref, lnb_ref, tri_ref, o_ref, state, ybuf, *, chunk, nb):
    C = chunk
    c_idx = pl.program_id(1)

    @pl.when(c_idx == 0)
    def _():
        state[...] = jnp.zeros_like(state)

    R = nb * C
    r = r_ref[...].reshape(R, HALF).astype(F32)
    k = k_ref[...].reshape(R, HALF).astype(F32)
    v = v_ref[...].reshape(R, HALF).astype(F32)
    zs = zs_ref[...].reshape(R, Z_SMALL)
    wd = zs[:, LANES:2 * LANES]
    ad = zs[:, 2 * LANES:3 * LANES]
    gd = zs[:, 3 * LANES:5 * LANES]
    wl = w0_ref[...] + _mm(jnp.tanh(wd), dup_ref[...])
    lw = -jnp.exp(-_softplus(-wl) - 0.5)
    a = _sigmoid(a0_ref[...] + _mm(ad, iup_ref[...]))
    g = _mm(_sigmoid(gd), gup_ref[...])
    kk = k * kk_ref[...]
    k2 = k * (1.0 + (a - 1.0) * ka_ref[...])
    cl = _mm_split(tri_ref[...], lw)
    cl_end = jnp.concatenate(
        [jnp.broadcast_to(cl[(bi + 1) * C - 1:(bi + 1) * C, :], (C, HALF)) for bi in range(nb)], axis=0)
    e_neg = jnp.exp(-cl)
    e_tail = jnp.exp(cl_end - cl)
    pre = dict(rt=r * jnp.exp(cl), at=-kk * jnp.exp(cl - lw), kh=k2 * e_neg, bh=kk * a * e_neg,
               kb=k2 * e_tail, bb=kk * a * e_tail, v=v, kk=kk, pend=jnp.exp(cl_end),
               rkr=r * k2 * rk_ref[...], g=g)

    row = lax.broadcasted_iota(I32, (C, C), 0)
    col = lax.broadcasted_iota(I32, (C, C), 1)
    lower = col <= row
    strict = col < row
    rowk = lax.broadcasted_iota(I32, (HEAD_DIM, HEAD_DIM), 0)
    colk = lax.broadcasted_iota(I32, (HEAD_DIM, HEAD_DIM), 1)
    eye_k = rowk == colk
    n_sq = int(np.log2(C)) - 1
    units = [(bi, h) for bi in range(nb) for h in range(N_HEADS)]

    def part(name, bi, h):
        return pre[name][bi * C:(bi + 1) * C, h * HEAD_DIM:(h + 1) * HEAD_DIM]

    ops = []
    for bi, h in units:
        kk_h = part("kk", bi, h)
        inv = 1.0 / jnp.maximum(jnp.sqrt(jnp.sum(kk_h * kk_h, axis=1, keepdims=True)), 1e-12)
        ops.append(dict(rt=part("rt", bi, h), at=part("at", bi, h) * inv, kh=part("kh", bi, h),
                        bh=part("bh", bi, h) * inv, kb=part("kb", bi, h), bb=part("bb", bi, h) * inv,
                        v=part("v", bi, h)))
    ra = [jnp.concatenate([o["rt"], o["at"]], axis=0).astype(BF16) for o in ops]
    g1 = [_mm_nt(x, o["kh"]) for x, o in zip(ra, ops)]
    g2 = [_mm_nt(x, o["bh"]) for x, o in zip(ra, ops)]
    a_rb = [jnp.where(lower, t[0:C], 0.0).astype(BF16) for t in g2]
    pw = [jnp.where(strict, t[C:2 * C], 0.0) for t in g2]
    av = [_mm(jnp.concatenate([jnp.where(lower, t[0:C], 0.0), jnp.where(strict, t[C:2 * C], 0.0)], axis=0),
              o["v"]) for t, o in zip(g1, ops)]
    xs = [jnp.concatenate([o["at"], t[C:2 * C]], axis=1) for o, t in zip(ops, av)]
    for level in range(n_sq + 1):
        pb = [p.astype(BF16) for p in pw]
        xs = [x + _mm(p, x) for p, x in zip(pb, xs)]
        if level < n_sq:
            pw = [jnp.dot(p, p, preferred_element_type=F32) for p in pb]
    xb = [x.astype(BF16) for x in xs]
    rb = [jnp.dot(p, x, preferred_element_type=F32) for p, x in zip(a_rb, xb)]
    bx = [_mm_tn(o["bb"], x) for o, x in zip(ops, xb)]
    kv = [_mm_tn(o["kb"], o["v"]) for o in ops]
    for u, (bi, h) in enumerate(units):
        o = ops[u]
        sl = slice(h * HEAD_DIM, (h + 1) * HEAD_DIM)
        r2 = o["rt"] + rb[u][:, 0:HEAD_DIM]
        y0 = rb[u][:, HEAD_DIM:2 * HEAD_DIM] + av[u][0:C]
        m_mat = jnp.where(eye_k, part("pend", bi, h)[0:1, :], 0.0) + bx[u][:, 0:HEAD_DIM]
        g_mat = bx[u][:, HEAD_DIM:2 * HEAD_DIM] + kv[u]
        out = _mm(jnp.concatenate([r2, m_mat], axis=0), state[bi, h])
        state[bi, h] = out[C:C + HEAD_DIM] + g_mat
        y = out[0:C] + y0
        cen = y - jnp.sum(y, axis=1, keepdims=True) * (1.0 / HEAD_DIM)
        var = jnp.sum(cen * cen, axis=1, keepdims=True) * (1.0 / HEAD_DIM)
        bonus = jnp.sum(part("rkr", bi, h), axis=1, keepdims=True) * o["v"]
        y = cen * lax.rsqrt(var + LNX_EPS) * lng_ref[:, sl] + lnb_ref[:, sl] + bonus
        ybuf[bi, :, sl] = y * part("g", bi, h)
    o_ref[...] = ybuf[...].astype(BF16)


def _rwkv(zm, zs, p, B, T):
    N = zm.shape[0]
    C = min(RWKV_CHUNK, T)
    nc = T // C
    nb = RWKV_SEQS_PER_STEP if B % RWKV_SEQS_PER_STEP == 0 else 1
    tri = np.kron(np.eye(nb, dtype=np.float32), np.tril(np.ones((C, C), np.float32)))
    pad_rows = lambda w, rows: jnp.zeros((rows, HALF), F32).at[:w.shape[0]].set(w)
    row = lambda a: a.reshape(1, HALF)
    consts = [row(p["rw_w0"]), pad_rows(p["rw_decay_up"], LANES), row(p["rw_a0"]),
              pad_rows(p["rw_iclr_up"], LANES), pad_rows(p["rw_gate_up"], 2 * LANES),
              row(p["rw_k_k"]), row(p["rw_k_a"]), row(p["rw_r_k"]), row(p["rw_lnx_g"]),
              row(p["rw_lnx_b"]), jnp.asarray(tri, BF16)]
    full = lambda a: pl.BlockSpec(a.shape, lambda b, c: (0, 0))
    rcol = 4 * HALF // HALF
    zm3 = zm.reshape(B, T, Z_MAIN)
    out = pl.pallas_call(
        functools.partial(_rwkv_kernel, chunk=C, nb=nb),
        grid=(B // nb, nc),
        in_specs=[pl.BlockSpec((nb, C, HALF), lambda b, c: (b, c, rcol)),
                  pl.BlockSpec((nb, C, HALF), lambda b, c: (b, c, rcol + 1)),
                  pl.BlockSpec((nb, C, HALF), lambda b, c: (b, c, rcol + 2)),
                  pl.BlockSpec((nb, C, Z_SMALL), lambda b, c: (b, c, 0))] + [full(a) for a in consts],
        out_specs=pl.BlockSpec((nb, C, HALF), lambda b, c: (b, c, 0)),
        out_shape=jax.ShapeDtypeStruct((B, T, HALF), BF16),
        scratch_shapes=[pltpu.VMEM((nb, N_HEADS, HEAD_DIM, HEAD_DIM), F32), pltpu.VMEM((nb, C, HALF), F32)],
        compiler_params=_cparams(("arbitrary", "arbitrary")),
        name="rwkv",
    )(zm3, zm3, zm3, zs.reshape(B, T, Z_SMALL), *consts)
    return out.reshape(N, HALF)


def _first_index(mask, iota, big):
    return jnp.min(jnp.where(mask, iota, big), axis=0, keepdims=True)


def _outproj_kernel(x_ref, yf_ref, yr_ref, g1_ref, sh2_ref, sc2_ref, g2_ref, n2g_ref, wo_ref, rwt_ref,
                    rb_ref, swg_ref, swu_ref, swd_ref, ustrict_ref,
                    xs_ref, hp_ref, idx_ref, wts_ref, rank_ref, cnt_ref, carry, *, tm):
    @pl.when(pl.program_id(0) == 0)
    def _():
        carry[...] = jnp.zeros_like(carry)

    d = (jnp.dot(yf_ref[...], wo_ref[0:HALF, :], preferred_element_type=F32)
         + jnp.dot(yr_ref[...], wo_ref[HALF:2 * HALF, :], preferred_element_type=F32))
    x2 = x_ref[...] + g1_ref[...] * d
    h = x2 * lax.rsqrt(jnp.mean(x2 * x2, axis=-1, keepdims=True) + RMS_EPS) * n2g_ref[...]
    h = h * (1.0 + sc2_ref[...]) + sh2_ref[...]
    hb = h.astype(BF16)
    act = _silu(jnp.dot(hb, swg_ref[...], preferred_element_type=F32)) * jnp.dot(
        hb, swu_ref[...], preferred_element_type=F32)
    shared = jnp.dot(act.astype(BF16), swd_ref[...], preferred_element_type=F32)
    xs_ref[...] = x2 + g2_ref[...] * shared
    half = D_MODEL // 2
    hp_ref[...] = _pack_halves(h)

    logits = lax.dot_general(rwt_ref[...], h, (((1,), (1,)), ((), ())), precision=HIGHEST,
                             preferred_element_type=F32)
    scores = _sigmoid(logits)
    sel = scores + rb_ref[:, 0:1]
    neg = -jnp.inf
    sel3 = sel.reshape(N_GROUPS, GROUP_SIZE, tm)
    io_in = lax.broadcasted_iota(I32, (N_GROUPS, GROUP_SIZE, tm), 1)
    m1 = jnp.max(sel3, axis=1, keepdims=True)
    f1 = jnp.min(jnp.where(sel3 == m1, io_in, GROUP_SIZE), axis=1, keepdims=True)
    m2 = jnp.max(jnp.where(io_in == f1, neg, sel3), axis=1, keepdims=True)
    gs = (m1 + m2).reshape(N_GROUPS, tm)
    io_g = lax.broadcasted_iota(I32, (N_GROUPS, tm), 0)
    gmask = jnp.zeros((N_GROUPS, tm), jnp.bool_)
    for _ in range(TOPK_GROUPS):
        mg = jnp.max(gs, axis=0, keepdims=True)
        fg = _first_index(gs == mg, io_g, N_GROUPS)
        pick = io_g == fg
        gmask = jnp.logical_or(gmask, pick)
        gs = jnp.where(pick, neg, gs)
    emask = jnp.broadcast_to(gmask.reshape(N_GROUPS, 1, tm), (N_GROUPS, GROUP_SIZE, tm)).reshape(N_EXPERTS, tm)
    cur = jnp.where(emask, sel, neg)
    io_e = lax.broadcasted_iota(I32, (N_EXPERTS, tm), 0)
    picks, idxs, wts = [], [], []
    for _ in range(TOP_K):
        me = jnp.max(cur, axis=0, keepdims=True)
        fe = _first_index(cur == me, io_e, N_EXPERTS)
        pick = io_e == fe
        picks.append(pick)
        idxs.append(fe)
        wts.append(jnp.sum(jnp.where(pick, scores, 0.0), axis=0, keepdims=True))
        cur = jnp.where(pick, neg, cur)
    wsum = wts[0]
    for w in wts[1:]:
        wsum = wsum + w
    zero_i = jnp.zeros((1, tm), I32)
    zero_f = jnp.zeros((1, tm), F32)
    idx_ref[...] = jnp.concatenate(idxs + [zero_i, zero_i], axis=0)
    wts_ref[...] = jnp.concatenate([w / wsum * ROUTED_SCALE for w in wts] + [zero_f, zero_f], axis=0)
    cnt = picks[0].astype(F32)
    for pk in picks[1:]:
        cnt = cnt + pk.astype(F32)
    excl = jnp.dot(cnt.astype(BF16), ustrict_ref[...], preferred_element_type=F32) + carry[:, 0:1]
    ranks = [jnp.sum(jnp.where(pk, excl, 0.0), axis=0, keepdims=True).astype(I32) for pk in picks]
    rank_ref[...] = jnp.concatenate(ranks + [zero_i, zero_i], axis=0)
    total = carry[...] + jnp.sum(cnt, axis=1, keepdims=True)
    carry[...] = total
    cnt_ref[...] = total.astype(I32)


def _outproj(x2d, yf, yr, mod4, norm2_g, w_out, router_w, router_bias, swg, swu, swd, T):
    N, D = x2d.shape
    tm = min(512, T)
    tps = T // tm
    vec = lambda j: pl.BlockSpec((None, None, 1, D), lambda i: (i // tps, j, 0, 0))
    full = lambda a: pl.BlockSpec(a.shape, lambda i: (0, 0))
    ustrict = jnp.asarray(np.triu(np.ones((tm, tm), np.float32), 1), BF16)
    rb = jnp.broadcast_to(router_bias.reshape(N_EXPERTS, 1), (N_EXPERTS, LANES))
    consts = [norm2_g.reshape(1, D), w_out.astype(BF16), router_w.T, rb, swg.astype(BF16),
              swu.astype(BF16), swd.astype(BF16), ustrict]
    small = lambda dt: jax.ShapeDtypeStruct((8, N), dt)
    return pl.pallas_call(
        functools.partial(_outproj_kernel, tm=tm),
        grid=(N // tm,),
        in_specs=[pl.BlockSpec((tm, D), lambda i: (i, 0)),
                  pl.BlockSpec((tm, HALF), lambda i: (i, 0)),
                  pl.BlockSpec((tm, HALF), lambda i: (i, 0)),
                  vec(2), vec(3), vec(4), vec(5)] + [full(a) for a in consts],
        out_specs=[pl.BlockSpec((tm, D), lambda i: (i, 0)),
                   pl.BlockSpec((tm, D // 2), lambda i: (i, 0)),
                   pl.BlockSpec((8, tm), lambda i: (0, i)),
                   pl.BlockSpec((8, tm), lambda i: (0, i)),
                   pl.BlockSpec((8, tm), lambda i: (0, i)),
                   pl.BlockSpec((N_EXPERTS, LANES), lambda i: (0, 0))],
        out_shape=[jax.ShapeDtypeStruct((N, D), F32), jax.ShapeDtypeStruct((N, D // 2), jnp.uint32),
                   small(I32), small(F32), small(I32),
                   jax.ShapeDtypeStruct((N_EXPERTS, LANES), I32)],
        scratch_shapes=[pltpu.VMEM((N_EXPERTS, LANES), F32)],
        compiler_params=_cparams(("arbitrary",)),
        name="outproj",
    )(x2d, yf, yr, mod4, mod4, mod4, mod4, *consts)


def _dispatch_kernel(pend_ref, padded_ref, nu_ref, dest_ref, hp_ref, xs_ref, zeros, sem, zsem, *, tm, n_blocks):
    @pl.when(pl.program_id(0) == 0)
    def _():
        zeros[...] = jnp.zeros_like(zeros)

        def block_copy(start):
            return pltpu.make_async_copy(zeros, xs_ref.at[pl.ds(pl.multiple_of(start, EXPERT_BLOCK),
                                                                 EXPERT_BLOCK), :], zsem)

        def pad_loop(fn):
            def body(e, _):
                @pl.when(padded_ref[e] > 0)
                def _():
                    fn(block_copy(pend_ref[e] - EXPERT_BLOCK))
                return 0
            lax.fori_loop(0, N_EXPERTS, body, 0)

        def tail_loop(fn):
            def body(b, _):
                fn(block_copy(b * EXPERT_BLOCK))
                return 0
            lax.fori_loop(nu_ref[0], n_blocks, body, 0)

        pad_loop(lambda cp: cp.start())
        tail_loop(lambda cp: cp.start())
        pad_loop(lambda cp: cp.wait())
        tail_loop(lambda cp: cp.wait())

    def row_copy(j, k):
        return pltpu.make_async_copy(hp_ref.at[pl.ds(j, 1), :],
                                     xs_ref.at[pl.ds(dest_ref[0, k, j], 1), :], sem)

    def issue(j, _):
        for k in range(TOP_K):
            row_copy(j, k).start()
        return 0

    def drain(j, _):
        for k in range(TOP_K):
            row_copy(j, k).wait()
        return 0

    lax.fori_loop(0, tm, issue, 0)
    lax.fori_loop(0, tm, drain, 0)


def _dispatch(hp, dest, pad_end, padded, n_used, n_rows):
    N, W = hp.shape
    tm = min(256, N)
    n_blocks = n_rows // EXPERT_BLOCK
    dest3 = dest.reshape(8, N // tm, tm).transpose(1, 0, 2)
    return pl.pallas_call(
        functools.partial(_dispatch_kernel, tm=tm, n_blocks=n_blocks),
        grid_spec=pltpu.PrefetchScalarGridSpec(
            num_scalar_prefetch=3, grid=(N // tm,),
            in_specs=[pl.BlockSpec((1, 8, tm), lambda i, *_: (i, 0, 0), memory_space=pltpu.SMEM),
                      pl.BlockSpec((tm, W), lambda i, *_: (i, 0))],
            out_specs=pl.BlockSpec(memory_space=pl.ANY),
            scratch_shapes=[pltpu.VMEM((EXPERT_BLOCK, W), jnp.uint32), pltpu.SemaphoreType.DMA,
                            pltpu.SemaphoreType.DMA]),
        out_shape=jax.ShapeDtypeStruct((n_rows, W), jnp.uint32),
        compiler_params=_cparams(("arbitrary",)),
        name="dispatch",
    )(pad_end, padded, n_used, dest3, hp)


def _experts_kernel(be_ref, nu_ref, xs_ref, wgf_ref, wuf_ref, wdf_ref, ys_ref, wg_ref, wu_ref, wd_ref):
    i = pl.program_id(0)
    live = i < nu_ref[0]
    new_expert = jnp.logical_or(i == 0, be_ref[i] != be_ref[jnp.maximum(i - 1, 0)])

    @pl.when(jnp.logical_not(live))
    def _():
        ys_ref[...] = jnp.zeros_like(ys_ref)

    @pl.when(jnp.logical_and(live, new_expert))
    def _():
        wg_ref[...] = wgf_ref[...].astype(BF16)
        wu_ref[...] = wuf_ref[...].astype(BF16)
        wd_ref[...] = wdf_ref[...].astype(BF16)

    @pl.when(live)
    def _():
        half = D_MODEL // 2
        lo, hi = _unpack_halves(xs_ref[...])
        lo = lo.astype(BF16)
        hi = hi.astype(BF16)
        gate = (jnp.dot(lo, wg_ref[0:half, :], preferred_element_type=F32)
                + jnp.dot(hi, wg_ref[half:D_MODEL, :], preferred_element_type=F32))
        up = (jnp.dot(lo, wu_ref[0:half, :], preferred_element_type=F32)
              + jnp.dot(hi, wu_ref[half:D_MODEL, :], preferred_element_type=F32))
        y = jnp.dot((_silu(gate) * up).astype(BF16), wd_ref[...], preferred_element_type=F32)
        ys_ref[...] = _pack_halves(y)


def _experts(xs, blk_e, n_used, wg, wu, wd):
    n_rows, W = xs.shape
    n_blocks = n_rows // EXPERT_BLOCK
    row_map = lambda i, be, nu: (jnp.minimum(i, nu[0] - 1), 0)
    return pl.pallas_call(
        _experts_kernel,
        grid_spec=pltpu.PrefetchScalarGridSpec(
            num_scalar_prefetch=2, grid=(n_blocks,),
            in_specs=[pl.BlockSpec((EXPERT_BLOCK, W), row_map),
                      pl.BlockSpec((None, D_MODEL, D_EXPERT), lambda i, be, nu: (be[i], 0, 0)),
                      pl.BlockSpec((None, D_MODEL, D_EXPERT), lambda i, be, nu: (be[i], 0, 0)),
                      pl.BlockSpec((None, D_EXPERT, D_MODEL), lambda i, be, nu: (be[i], 0, 0))],
            out_specs=pl.BlockSpec((EXPERT_BLOCK, W), lambda i, be, nu: (i, 0)),
            scratch_shapes=[pltpu.VMEM((D_MODEL, D_EXPERT), BF16), pltpu.VMEM((D_MODEL, D_EXPERT), BF16),
                            pltpu.VMEM((D_EXPERT, D_MODEL), BF16)]),
        out_shape=jax.ShapeDtypeStruct((n_rows, W), jnp.uint32),
        compiler_params=_cparams(("arbitrary",)),
        name="experts",
    )(blk_e, n_used, xs, wg, wu, wd)


def _combine_kernel(dest_ref, ys_ref, xs_ref, wts_ref, g2_ref, fg_ref, o_ref, buf, sem, *, tm):
    def row_copy(j, k):
        return pltpu.make_async_copy(ys_ref.at[pl.ds(dest_ref[0, k, j], 1), :],
                                     buf.at[k, pl.ds(j, 1), :], sem)

    def issue(j, _):
        for k in range(TOP_K):
            row_copy(j, k).start()
        return 0

    def drain(j, _):
        for k in range(TOP_K):
            row_copy(j, k).wait()
        return 0

    lax.fori_loop(0, tm, issue, 0)
    lax.fori_loop(0, tm, drain, 0)
    w = wts_ref[...]
    lo = jnp.zeros((tm, D_MODEL // 2), F32)
    hi = jnp.zeros((tm, D_MODEL // 2), F32)
    for k in range(TOP_K):
        yk_lo, yk_hi = _unpack_halves(buf[k])
        wk = w[:, k:k + 1]
        lo = lo + wk * yk_lo
        hi = hi + wk * yk_hi
    x3 = xs_ref[...] + g2_ref[...] * jnp.concatenate([lo, hi], axis=1)
    y = x3 * lax.rsqrt(jnp.mean(x3 * x3, axis=-1, keepdims=True) + RMS_EPS) * fg_ref[...]
    o_ref[...] = y


def _combine(ys, dest, xsr, wts_t, mod4, final_g, T):
    N, D = xsr.shape
    W = ys.shape[1]
    tm = min(128, T)
    tps = T // tm
    dest3 = dest.reshape(8, N // tm, tm).transpose(1, 0, 2)
    return pl.pallas_call(
        functools.partial(_combine_kernel, tm=tm),
        grid=(N // tm,),
        in_specs=[pl.BlockSpec((1, 8, tm), lambda i: (i, 0, 0), memory_space=pltpu.SMEM),
                  pl.BlockSpec(memory_space=pl.ANY),
                  pl.BlockSpec((tm, D), lambda i: (i, 0)),
                  pl.BlockSpec((tm, 8), lambda i: (i, 0)),
                  pl.BlockSpec((None, None, 1, D), lambda i: (i // tps, 5, 0, 0)),
                  pl.BlockSpec((1, D), lambda i: (0, 0))],
        out_specs=pl.BlockSpec((tm, D), lambda i: (i, 0)),
        out_shape=jax.ShapeDtypeStruct((N, D), F32),
        scratch_shapes=[pltpu.VMEM((TOP_K, tm, W), jnp.uint32), pltpu.SemaphoreType.DMA],
        compiler_params=_cparams(("arbitrary",)),
        name="combine",
    )(dest3, ys, xsr, wts_t, mod4, final_g.reshape(1, D))


def _pack_w_in(w_in):
    D = w_in.shape[0]
    fox_cols = 4 * HALF + 3 * N_HEADS
    wf = w_in[:, :fox_cols]
    wr = w_in[:, fox_cols:]
    o = 3 * HALF
    pad = lambda a, n: jnp.concatenate([a, jnp.zeros((D, n - a.shape[1]), a.dtype)], axis=1)
    parts = [wf[:, :4 * HALF], wr[:, :o],
             pad(wf[:, 4 * HALF:], LANES),
             pad(wr[:, o:o + DECAY_LORA], LANES),
             pad(wr[:, o + DECAY_LORA:o + DECAY_LORA + ICLR_LORA], LANES),
             pad(wr[:, o + DECAY_LORA + ICLR_LORA:], 2 * LANES)]
    return jnp.concatenate(parts, axis=1).astype(BF16)


def _pack_mu(mu):
    o = 3 * HALF
    pad = lambda a, n: jnp.concatenate([a, jnp.zeros((n - a.shape[0],), a.dtype)])
    small = jnp.concatenate([jnp.zeros((LANES,), mu.dtype),
                             pad(mu[o:o + DECAY_LORA], LANES),
                             pad(mu[o + DECAY_LORA:o + DECAY_LORA + ICLR_LORA], LANES),
                             pad(mu[o + DECAY_LORA + ICLR_LORA:], 2 * LANES)])
    return mu[:o].reshape(1, o), small.reshape(1, Z_SMALL)


def kernel(x, c, norm1_g, norm2_g, ada_w, ada_b, w_in, w_out, fox_qn_g, fox_kn_g, fox_on_g, fox_forget_b,
           rw_mu, rw_w0, rw_decay_up, rw_a0, rw_iclr_up, rw_gate_up, rw_k_k, rw_k_a, rw_r_k, rw_lnx_g,
           rw_lnx_b, router_w, router_bias, exp_w_gate, exp_w_up, exp_w_down, sh_w_gate, sh_w_up,
           sh_w_down, final_g):
    B, T, D = x.shape
    N = B * T
    depth = norm1_g.shape[0]
    assert depth == 1, "the combine kernel fuses the final RMSNorm, so exactly one layer is supported"
    xf = x.reshape(N, D)
    for l in range(depth):
        mod4 = _ada(c, ada_w[l], ada_b[l]).reshape(B, 6, 1, D)
        mu_big, mu_small = _pack_mu(rw_mu[l])
        zm, zs = _inproj(xf, mod4, norm1_g[l], _pack_w_in(w_in[l]), mu_big, mu_small, T)
        qp, kp, vp = _foxprep(zm, zs, fox_forget_b[l], fox_qn_g[l], fox_kn_g[l], T)
        y_fox = _attention(qp, kp, vp, zm, fox_on_g[l], B, T)
        rw = dict(rw_w0=rw_w0[l], rw_decay_up=rw_decay_up[l], rw_a0=rw_a0[l], rw_iclr_up=rw_iclr_up[l],
                  rw_gate_up=rw_gate_up[l], rw_k_k=rw_k_k[l], rw_k_a=rw_k_a[l], rw_r_k=rw_r_k[l],
                  rw_lnx_g=rw_lnx_g[l], rw_lnx_b=rw_lnx_b[l])
        y_rwkv = _rwkv(zm, zs, rw, B, T)
        xsr, hp, idx_t, wts_t, rank_t, cnt = _outproj(
            xf, y_fox, y_rwkv, mod4, norm2_g[l], w_out[l], router_w[l], router_bias[l],
            sh_w_gate[l], sh_w_up[l], sh_w_down[l], T)
        counts = cnt[:, 0]
        padded = (counts + EXPERT_BLOCK - 1) // EXPERT_BLOCK * EXPERT_BLOCK
        pad_end = jnp.cumsum(padded).astype(I32)
        pad_start = pad_end - padded
        n_blocks = -(-(N * TOP_K) // EXPERT_BLOCK) + N_EXPERTS
        n_rows = n_blocks * EXPERT_BLOCK
        experts = jnp.arange(N_EXPERTS, dtype=I32)
        dest = rank_t + jnp.sum(jnp.where(idx_t[:, :, None] == experts, pad_start, 0), axis=-1)
        blk_start = jnp.arange(n_blocks, dtype=I32) * EXPERT_BLOCK
        blk_e = jnp.minimum(jnp.sum((pad_end[None, :] <= blk_start[:, None]).astype(I32), axis=1),
                            N_EXPERTS - 1)
        n_used = (pad_end[N_EXPERTS - 1:] // EXPERT_BLOCK).astype(I32)
        xs = _dispatch(hp, dest, pad_end, padded.astype(I32), n_used, n_rows)
        ys = _experts(xs, blk_e, n_used, exp_w_gate[l], exp_w_up[l], exp_w_down[l])
        xf = _combine(ys, dest, xsr, wts_t.T, mod4, final_g, T)
    return xf.reshape(B, T, D)
```

```python
import functools

import jax
import jax.numpy as jnp
import numpy as np
from jax import lax
from jax.experimental import pallas as pl
from jax.experimental.pallas import tpu as pltpu

F32 = jnp.float32
BF16 = jnp.bfloat16
I32 = jnp.int32
HIGHEST = lax.Precision.HIGHEST

D_MODEL = 1024
HEAD_DIM = 64
N_HEADS = 8
HALF = N_HEADS * HEAD_DIM
RMS_EPS = 1e-6
LNX_EPS = 64e-5
LOG2E = 1.4426950408889634
DECAY_LORA = 64
ICLR_LORA = 64
GATE_LORA = 160
N_EXPERTS = 64
N_GROUPS = 8
GROUP_SIZE = N_EXPERTS // N_GROUPS
TOPK_GROUPS = 4
TOP_K = 6
D_EXPERT = 256
D_SHARED = 256
ROUTED_SCALE = 2.5
EXPERT_BLOCK = 256

LANES = 128
Z_MAIN = 4 * HALF + 3 * HALF
Z_SMALL = 5 * LANES
VMEM_LIMIT = 56 * 1024 * 1024
DMA_UNROLL = 8
RWKV_CHUNK = 64
RWKV_SEQS_PER_STEP = 2


def _cparams(semantics):
    return pltpu.CompilerParams(dimension_semantics=semantics, vmem_limit_bytes=VMEM_LIMIT)


def _mm(a, b):
    return jnp.dot(a.astype(BF16), b.astype(BF16), preferred_element_type=F32)


def _mm_nt(a, b):
    return lax.dot_general(a.astype(BF16), b.astype(BF16), (((1,), (1,)), ((), ())),
                           preferred_element_type=F32)


def _mm_tn(a, b):
    return lax.dot_general(a.astype(BF16), b.astype(BF16), (((0,), (0,)), ((), ())),
                           preferred_element_type=F32)


def _mm_f32(a, b):
    return jnp.dot(a, b, precision=HIGHEST, preferred_element_type=F32)


def _sigmoid(x):
    return 1.0 / (1.0 + jnp.exp(-x))


def _softplus(x):
    return jnp.maximum(x, 0.0) + jnp.log(1.0 + jnp.exp(-jnp.abs(x)))


def _silu(x):
    return x * _sigmoid(x)


def _pack_halves(x):
    w = x.shape[1] // 2
    bits = lambda t: lax.bitcast_convert_type(t.astype(jnp.bfloat16).astype(F32), jnp.uint32)
    return (bits(x[:, 0:w]) >> 16) | (bits(x[:, w:2 * w]) & jnp.uint32(0xFFFF0000))


def _unpack_halves(p):
    lo = lax.bitcast_convert_type(p << 16, F32)
    hi = lax.bitcast_convert_type(p & jnp.uint32(0xFFFF0000), F32)
    return lo, hi


def _shift_rows(z, carry_ref, first):
    rows = z.shape[0]
    prev_row = jnp.where(first, 0.0, carry_ref[0:1, :])
    prev = pltpu.roll(z, 1, 0)
    row0 = lax.broadcasted_iota(I32, (rows, 1), 0) == 0
    prev = jnp.where(row0, prev_row, prev)
    carry_ref[0:1, :] = z[rows - 1:rows, :]
    return prev


def _ada_kernel(c_ref, w_ref, b_ref, o_ref):
    o_ref[...] = _mm_f32(_silu(c_ref[...]), w_ref[...]) + b_ref[...]


def _ada(c, ada_w, ada_b):
    B, D = c.shape
    n_out = ada_w.shape[1]
    tn = 512
    return pl.pallas_call(
        _ada_kernel,
        grid=(n_out // tn,),
        in_specs=[pl.BlockSpec((B, D), lambda j: (0, 0)),
                  pl.BlockSpec((D, tn), lambda j: (0, j)),
                  pl.BlockSpec((1, tn), lambda j: (0, j))],
        out_specs=pl.BlockSpec((B, tn), lambda j: (0, j)),
        out_shape=jax.ShapeDtypeStruct((B, n_out), F32),
        compiler_params=_cparams(("arbitrary",)),
        name="ada",
    )(c, ada_w, ada_b.reshape(1, n_out))


def _inproj_kernel(x_ref, g_ref, sh_ref, sc_ref, w_ref, mub_ref, mus_ref, zm_ref, zs_ref,
                   carry_b, carry_s, *, tiles_per_seq):
    first = (pl.program_id(0) % tiles_per_seq) == 0
    x = x_ref[...]
    h = x * lax.rsqrt(jnp.mean(x * x, axis=-1, keepdims=True) + RMS_EPS) * g_ref[...]
    hb = (h * (1.0 + sc_ref[...]) + sh_ref[...]).astype(BF16)
    nf = 4 * HALF
    zm_ref[:, 0:nf] = jnp.dot(hb, w_ref[:, 0:nf], preferred_element_type=F32).astype(BF16)
    zr = jnp.dot(hb, w_ref[:, nf:Z_MAIN], preferred_element_type=F32)
    zr = zr + mub_ref[...] * (_shift_rows(zr, carry_b, first) - zr)
    zm_ref[:, nf:Z_MAIN] = zr.astype(BF16)
    zs = jnp.dot(hb, w_ref[:, Z_MAIN:Z_MAIN + Z_SMALL], preferred_element_type=F32)
    zs_ref[...] = zs + mus_ref[...] * (_shift_rows(zs, carry_s, first) - zs)


def _inproj(x2d, mod4, norm_g, w_all, mu_big, mu_small, T):
    N, D = x2d.shape
    tm = min(512, T)
    tps = T // tm
    vec = lambda j: pl.BlockSpec((None, None, 1, D), lambda i: (i // tps, j, 0, 0))
    return pl.pallas_call(
        functools.partial(_inproj_kernel, tiles_per_seq=tps),
        grid=(N // tm,),
        in_specs=[pl.BlockSpec((tm, D), lambda i: (i, 0)),
                  pl.BlockSpec((1, D), lambda i: (0, 0)),
                  vec(0), vec(1),
                  pl.BlockSpec((D, Z_MAIN + Z_SMALL), lambda i: (0, 0)),
                  pl.BlockSpec((1, 3 * HALF), lambda i: (0, 0)),
                  pl.BlockSpec((1, Z_SMALL), lambda i: (0, 0))],
        out_specs=[pl.BlockSpec((tm, Z_MAIN), lambda i: (i, 0)),
                   pl.BlockSpec((tm, Z_SMALL), lambda i: (i, 0))],
        out_shape=[jax.ShapeDtypeStruct((N, Z_MAIN), BF16),
                   jax.ShapeDtypeStruct((N, Z_SMALL), F32)],
        scratch_shapes=[pltpu.VMEM((8, 3 * HALF), F32), pltpu.VMEM((8, Z_SMALL), F32)],
        compiler_params=_cparams(("arbitrary",)),
        name="inproj",
    )(x2d, norm_g.reshape(1, D), mod4, mod4, w_all, mu_big, mu_small)


def _foxprep_kernel(q_ref, k_ref, v_ref, zs_ref, fb_ref, qg_ref, kg_ref, tri_ref, eexp_ref,
                    esum_ref, e8_ref, plq_ref, plk_ref, plv_ref, cq_ref, ck_ref, cv_ref,
                    qp_ref, kp_ref, vp_ref, carry_k, carry_v, carry_c, *, tiles_per_seq):
    first = (pl.program_id(0) % tiles_per_seq) == 0
    zs = zs_ref[...]
    logf = -_softplus(-(zs + fb_ref[...]))
    cum = _mm_f32(tri_ref[...], logf) + jnp.where(first, 0.0, carry_c[0:1, :])
    carry_c[0:1, :] = cum[cum.shape[0] - 1:, :]
    cum2 = cum * LOG2E
    c_hi = cum2.astype(BF16)
    r1 = cum2 - c_hi.astype(F32)
    c_mid = r1.astype(BF16)
    c_lo = (r1 - c_mid.astype(F32)).astype(BF16)
    a_full = _mm_f32(_sigmoid(zs), eexp_ref[...])
    k = k_ref[...].astype(F32)
    v = v_ref[...].astype(F32)
    a_k = a_full[:, 0:HALF]
    a_v = a_full[:, HALF:2 * HALF]
    k = a_k * _shift_rows(k, carry_k, first) + (1.0 - a_k) * k
    v = a_v * _shift_rows(v, carry_v, first) + (1.0 - a_v) * v
    q = q_ref[...].astype(F32)

    def head_rms(t, gain):
        ms = _mm_f32(t * t, esum_ref[...]) * (1.0 / HEAD_DIM)
        inv = _mm_f32(lax.rsqrt(ms + RMS_EPS), e8_ref[...])
        return t * inv * gain

    qn = head_rms(q, qg_ref[...]) * (HEAD_DIM ** -0.5 * LOG2E)
    kn = head_rms(k, kg_ref[...])
    aug = [c_hi, c_mid, c_lo]
    lhs_q = jnp.concatenate([qn.astype(BF16)] + aug, axis=1)
    lhs_k = jnp.concatenate([kn.astype(BF16)] + aug, axis=1)
    qp_ref[...] = (jnp.dot(lhs_q, plq_ref[...], preferred_element_type=F32) + cq_ref[...]).astype(BF16)
    kp_ref[...] = (jnp.dot(lhs_k, plk_ref[...], preferred_element_type=F32) + ck_ref[...]).astype(BF16)
    vp_ref[...] = (jnp.dot(v.astype(BF16), plv_ref[...], preferred_element_type=F32)
                   + cv_ref[...]).astype(BF16)


def _fox_constants(tm):
    hp = N_HEADS * LANES
    eexp = np.zeros((LANES, 2 * HALF), np.float32)
    esum = np.zeros((HALF, LANES), np.float32)
    e8 = np.zeros((LANES, HALF), np.float32)
    plq = np.zeros((HALF + 3 * LANES, hp), np.float32)
    plk = np.zeros((HALF + 3 * LANES, hp), np.float32)
    plv = np.zeros((HALF, hp), np.float32)
    cq = np.zeros((1, hp), np.float32)
    ck = np.zeros((1, hp), np.float32)
    cv = np.zeros((1, hp), np.float32)
    for h in range(N_HEADS):
        sl = slice(h * HEAD_DIM, (h + 1) * HEAD_DIM)
        eexp[8 + h, sl] = 1.0
        eexp[16 + h, HALF + h * HEAD_DIM:HALF + (h + 1) * HEAD_DIM] = 1.0
        esum[sl, h] = 1.0
        e8[h, sl] = 1.0
        base = h * LANES
        for d in range(HEAD_DIM):
            plq[h * HEAD_DIM + d, base + d] = 1.0
            plk[h * HEAD_DIM + d, base + d] = 1.0
            plv[h * HEAD_DIM + d, base + d] = 1.0
        for j in range(3):
            plq[HALF + j * LANES + h, base + HEAD_DIM + j] = 1.0
            plk[HALF + j * LANES + h, base + HEAD_DIM + 3 + j] = -1.0
            cq[0, base + HEAD_DIM + 3 + j] = 1.0
            ck[0, base + HEAD_DIM + j] = 1.0
        cv[0, base + HEAD_DIM] = 1.0
    tri = np.tril(np.ones((tm, tm), np.float32))
    bf = lambda a: jnp.asarray(a, BF16)
    return dict(tri=jnp.asarray(tri), eexp=jnp.asarray(eexp), esum=jnp.asarray(esum), e8=jnp.asarray(e8),
                plq=bf(plq), plk=bf(plk), plv=bf(plv), cq=jnp.asarray(cq), ck=jnp.asarray(ck),
                cv=jnp.asarray(cv))


def _foxprep(zm, zs, forget_b, qn_g, kn_g, T):
    N = zm.shape[0]
    tm = min(256, T)
    tps = T // tm
    cst = _fox_constants(tm)
    hp = N_HEADS * LANES
    fb = jnp.zeros((1, LANES), F32).at[0, :N_HEADS].set(forget_b)
    full = lambda a: pl.BlockSpec(a.shape, lambda i: (0, 0))
    consts = [fb, qn_g.reshape(1, HALF), kn_g.reshape(1, HALF), cst["tri"], cst["eexp"], cst["esum"],
              cst["e8"], cst["plq"], cst["plk"], cst["plv"], cst["cq"], cst["ck"], cst["cv"]]
    return pl.pallas_call(
        functools.partial(_foxprep_kernel, tiles_per_seq=tps),
        grid=(N // tm,),
        in_specs=[pl.BlockSpec((tm, HALF), lambda i: (i, 0)),
                  pl.BlockSpec((tm, HALF), lambda i: (i, 1)),
                  pl.BlockSpec((tm, HALF), lambda i: (i, 2)),
                  pl.BlockSpec((tm, LANES), lambda i: (i, 0))] + [full(a) for a in consts],
        out_specs=[pl.BlockSpec((tm, hp), lambda i: (i, 0))] * 3,
        out_shape=[jax.ShapeDtypeStruct((N, hp), BF16)] * 3,
        scratch_shapes=[pltpu.VMEM((8, HALF), F32), pltpu.VMEM((8, HALF), F32),
                        pltpu.VMEM((8, LANES), F32)],
        compiler_params=_cparams(("arbitrary",)),
        name="foxprep",
    )(zm, zm, zm, zs, *consts)


def _attn_kernel(q_ref, k_ref, v_ref, g_ref, ong_ref, o_ref, *, tq):
    i = pl.program_id(2)
    row = lax.broadcasted_iota(I32, (tq, tq), 0)
    col = lax.broadcasted_iota(I32, (tq, tq), 1)
    causal = col <= row
    heads = range(2)
    lanes = [slice(hh * LANES, (hh + 1) * LANES) for hh in heads]
    qs = [q_ref[:, lanes[hh]] for hh in heads]

    def scores(j):
        start = pl.multiple_of(j * tq, tq)
        return tuple(lax.dot_general(qs[hh], k_ref[pl.ds(start, tq), lanes[hh]], (((1,), (1,)), ((), ())),
                                     preferred_element_type=F32) for hh in heads)

    def update(j, s, m, acc, masked):
        start = pl.multiple_of(j * tq, tq)
        m_out, acc_out = [], []
        for hh in heads:
            sh = jnp.where(causal, s[hh], -jnp.inf) if masked else s[hh]
            m_new = jnp.maximum(m[hh], jnp.max(sh, axis=1, keepdims=True))
            p = jnp.exp2(sh - m_new)
            v = v_ref[pl.ds(start, tq), lanes[hh]]
            acc_out.append(jnp.exp2(m[hh] - m_new) * acc[hh]
                           + jnp.dot(p.astype(BF16), v, preferred_element_type=F32))
            m_out.append(m_new)
        return tuple(m_out), tuple(acc_out)

    def body(j, carry):
        s, m, acc = carry
        s_next = scores(j + 1)
        m, acc = update(j, s, m, acc, False)
        return s_next, m, acc

    init = (scores(0), tuple(jnp.full((tq, 1), -jnp.inf, F32) for _ in heads),
            tuple(jnp.zeros((tq, LANES), F32) for _ in heads))
    s, m, acc = lax.fori_loop(0, i, body, init)
    _, acc = update(i, s, m, acc, True)
    lane = lax.broadcasted_iota(I32, (tq, LANES), 1)
    outs = []
    for hh in heads:
        o = acc[hh] / acc[hh][:, HEAD_DIM:HEAD_DIM + 1]
        ms = jnp.sum(jnp.where(lane < HEAD_DIM, o * o, 0.0), axis=1, keepdims=True) * (1.0 / HEAD_DIM)
        outs.append(o * lax.rsqrt(ms + RMS_EPS))
    o = jnp.where(lane < HEAD_DIM, outs[0], pltpu.roll(outs[1], HEAD_DIM, 1))
    y = o * ong_ref[...] * _sigmoid(g_ref[...].astype(F32))
    o_ref[...] = y.astype(BF16)


def _attention(qp, kp, vp, zm, on_g, B, T):
    N = qp.shape[0]
    tq = min(256, T)
    nq = T // tq
    pairs = N_HEADS // 2
    g_col0 = 3 * HALF // LANES
    return pl.pallas_call(
        functools.partial(_attn_kernel, tq=tq),
        grid=(B, pairs, nq),
        in_specs=[pl.BlockSpec((tq, 2 * LANES), lambda b, p, i: (b * nq + i, p)),
                  pl.BlockSpec((T, 2 * LANES), lambda b, p, i: (b, p)),
                  pl.BlockSpec((T, 2 * LANES), lambda b, p, i: (b, p)),
                  pl.BlockSpec((tq, LANES), lambda b, p, i: (b * nq + i, g_col0 + p)),
                  pl.BlockSpec((None, 1, LANES), lambda b, p, i: (p, 0, 0))],
        out_specs=pl.BlockSpec((tq, LANES), lambda b, p, i: (b * nq + i, p)),
        out_shape=jax.ShapeDtypeStruct((N, HALF), BF16),
        compiler_params=_cparams(("arbitrary", "arbitrary", "arbitrary")),
        name="attn",
    )(qp, kp, vp, zm, on_g.reshape(pairs, 1, LANES))


def _mm_split(m01, x):
    hi = x.astype(BF16)
    r1 = x - hi.astype(F32)
    mid = r1.astype(BF16)
    lo = (r1 - mid.astype(F32)).astype(BF16)
    dot = lambda t: jnp.dot(m01, t, preferred_element_type=F32)
    return dot(hi) + dot(mid) + dot(lo)


def _rwkv_kernel(r_ref, k_ref, v_ref, zs_ref, w0_ref, dup_ref, a0_ref, iup_ref, gup_ref, kk_ref,
                 ka_ref, rk_ref, lng_ref, lnb_ref, tri_ref, o_ref, state, ybuf, *, chunk, nb):
    C = chunk
    c_idx = pl.program_id(1)

    @pl.when(c_idx == 0)
    def _():
        state[...] = jnp.zeros_like(state)

    R = nb * C
    r = r_ref[...].reshape(R, HALF).astype(F32)
    k = k_ref[...].reshape(R, HALF).astype(F32)
    v = v_ref[...].reshape(R, HALF).astype(F32)
    zs = zs_ref[...].reshape(R, Z_SMALL)
    wd = zs[:, LANES:2 * LANES]
    ad = zs[:, 2 * LANES:3 * LANES]
    gd = zs[:, 3 * LANES:5 * LANES]
    wl = w0_ref[...] + _mm(jnp.tanh(wd), dup_ref[...])
    lw = -jnp.exp(-_softplus(-wl) - 0.5)
    a = _sigmoid(a0_ref[...] + _mm(ad, iup_ref[...]))
    g = _mm(_sigmoid(gd), gup_ref[...])
    kk = k * kk_ref[...]
    k2 = k * (1.0 + (a - 1.0) * ka_ref[...])
    cl = _mm_split(tri_ref[...], lw)
    cl_end = jnp.concatenate(
        [jnp.broadcast_to(cl[(bi + 1) * C - 1:(bi + 1) * C, :], (C, HALF)) for bi in range(nb)], axis=0)
    e_neg = jnp.exp(-cl)
    e_tail = jnp.exp(cl_end - cl)
    pre = dict(rt=r * jnp.exp(cl), at=-kk * jnp.exp(cl - lw), kh=k2 * e_neg, bh=kk * a * e_neg,
               kb=k2 * e_tail, bb=kk * a * e_tail, v=v, kk=kk, pend=jnp.exp(cl_end),
               rkr=r * k2 * rk_ref[...], g=g)

    row = lax.broadcasted_iota(I32, (C, C), 0)
    col = lax.broadcasted_iota(I32, (C, C), 1)
    lower = col <= row
    strict = col < row
    rowk = lax.broadcasted_iota(I32, (HEAD_DIM, HEAD_DIM), 0)
    colk = lax.broadcasted_iota(I32, (HEAD_DIM, HEAD_DIM), 1)
    eye_k = rowk == colk
    n_sq = int(np.log2(C)) - 1
    units = [(bi, h) for bi in range(nb) for h in range(N_HEADS)]

    def part(name, bi, h):
        return pre[name][bi * C:(bi + 1) * C, h * HEAD_DIM:(h + 1) * HEAD_DIM]

    ops = []
    for bi, h in units:
        kk_h = part("kk", bi, h)
        inv = 1.0 / jnp.maximum(jnp.sqrt(jnp.sum(kk_h * kk_h, axis=1, keepdims=True)), 1e-12)
        ops.append(dict(rt=part("rt", bi, h), at=part("at", bi, h) * inv, kh=part("kh", bi, h),
                        bh=part("bh", bi, h) * inv, kb=part("kb", bi, h), bb=part("bb", bi, h) * inv,
                        v=part("v", bi, h)))
    ra = [jnp.concatenate([o["rt"], o["at"]], axis=0).astype(BF16) for o in ops]
    g1 = [_mm_nt(x, o["kh"]) for x, o in zip(ra, ops)]
    g2 = [_mm_nt(x, o["bh"]) for x, o in zip(ra, ops)]
    a_rb = [jnp.where(lower, t[0:C], 0.0).astype(BF16) for t in g2]
    pw = [jnp.where(strict, t[C:2 * C], 0.0) for t in g2]
    av = [_mm(jnp.concatenate([jnp.where(lower, t[0:C], 0.0), jnp.where(strict, t[C:2 * C], 0.0)], axis=0),
              o["v"]) for t, o in zip(g1, ops)]
    xs = [jnp.concatenate([o["at"], t[C:2 * C]], axis=1) for o, t in zip(ops, av)]
    for level in range(n_sq + 1):
        pb = [p.astype(BF16) for p in pw]
        xs = [x + _mm(p, x) for p, x in zip(pb, xs)]
        if level < n_sq:
            pw = [jnp.dot(p, p, preferred_element_type=F32) for p in pb]
    xb = [x.astype(BF16) for x in xs]
    rb = [jnp.dot(p, x, preferred_element_type=F32) for p, x in zip(a_rb, xb)]
    bx = [_mm_tn(o["bb"], x) for o, x in zip(ops, xb)]
    kv = [_mm_tn(o["kb"], o["v"]) for o in ops]
    for u, (bi, h) in enumerate(units):
        o = ops[u]
        sl = slice(h * HEAD_DIM, (h + 1) * HEAD_DIM)
        r2 = o["rt"] + rb[u][:, 0:HEAD_DIM]
        y0 = rb[u][:, HEAD_DIM:2 * HEAD_DIM] + av[u][0:C]
        m_mat = jnp.where(eye_k, part("pend", bi, h)[0:1, :], 0.0) + bx[u][:, 0:HEAD_DIM]
        g_mat = bx[u][:, HEAD_DIM:2 * HEAD_DIM] + kv[u]
        out = _mm(jnp.concatenate([r2, m_mat], axis=0), state[bi, h])
        state[bi, h] = out[C:C + HEAD_DIM] + g_mat
        y = out[0:C] + y0
        cen = y - jnp.sum(y, axis=1, keepdims=True) * (1.0 / HEAD_DIM)
        var = jnp.sum(cen * cen, axis=1, keepdims=True) * (1.0 / HEAD_DIM)
        bonus = jnp.sum(part("rkr", bi, h), axis=1, keepdims=True) * o["v"]
        y = cen * lax.rsqrt(var + LNX_EPS) * lng_ref[:, sl] + lnb_ref[:, sl] + bonus
        ybuf[bi, :, sl] = y * part("g", bi, h)
    o_ref[...] = ybuf[...].astype(BF16)


def _rwkv(zm, zs, p, B, T):
    N = zm.shape[0]
    C = min(RWKV_CHUNK, T)
    nc = T // C
    nb = RWKV_SEQS_PER_STEP if B % RWKV_SEQS_PER_STEP == 0 else 1
    tri = np.kron(np.eye(nb, dtype=np.float32), np.tril(np.ones((C, C), np.float32)))
    pad_rows = lambda w, rows: jnp.zeros((rows, HALF), F32).at[:w.shape[0]].set(w)
    row = lambda a: a.reshape(1, HALF)
    consts = [row(p["rw_w0"]), pad_rows(p["rw_decay_up"], LANES), row(p["rw_a0"]),
              pad_rows(p["rw_iclr_up"], LANES), pad_rows(p["rw_gate_up"], 2 * LANES),
              row(p["rw_k_k"]), row(p["rw_k_a"]), row(p["rw_r_k"]), row(p["rw_lnx_g"]),
              row(p["rw_lnx_b"]), jnp.asarray(tri, BF16)]
    full = lambda a: pl.BlockSpec(a.shape, lambda b, c: (0, 0))
    rcol = 4 * HALF // HALF
    zm3 = zm.reshape(B, T, Z_MAIN)
    out = pl.pallas_call(
        functools.partial(_rwkv_kernel, chunk=C, nb=nb),
        grid=(B // nb, nc),
        in_specs=[pl.BlockSpec((nb, C, HALF), lambda b, c: (b, c, rcol)),
                  pl.BlockSpec((nb, C, HALF), lambda b, c: (b, c, rcol + 1)),
                  pl.BlockSpec((nb, C, HALF), lambda b, c: (b, c, rcol + 2)),
                  pl.BlockSpec((nb, C, Z_SMALL), lambda b, c: (b, c, 0))] + [full(a) for a in consts],
        out_specs=pl.BlockSpec((nb, C, HALF), lambda b, c: (b, c, 0)),
        out_shape=jax.ShapeDtypeStruct((B, T, HALF), BF16),
        scratch_shapes=[pltpu.VMEM((nb, N_HEADS, HEAD_DIM, HEAD_DIM), F32), pltpu.VMEM((nb, C, HALF), F32)],
        compiler_params=_cparams(("arbitrary", "arbitrary")),
        name="rwkv",
    )(zm3, zm3, zm3, zs.reshape(B, T, Z_SMALL), *consts)
    return out.reshape(N, HALF)


def _first_index(mask, iota, big):
    return jnp.min(jnp.where(mask, iota, big), axis=0, keepdims=True)


def _outproj_kernel(x_ref, yf_ref, yr_ref, g1_ref, sh2_ref, sc2_ref, g2_ref, n2g_ref, wo_ref, rwt_ref,
                    rb_ref, swg_ref, swu_ref, swd_ref, ustrict_ref,
                    xs_ref, hp_ref, idx_ref, wts_ref, rank_ref, cnt_ref, carry, *, tm):
    @pl.when(pl.program_id(0) == 0)
    def _():
        carry[...] = jnp.zeros_like(carry)

    d = (jnp.dot(yf_ref[...], wo_ref[0:HALF, :], preferred_element_type=F32)
         + jnp.dot(yr_ref[...], wo_ref[HALF:2 * HALF, :], preferred_element_type=F32))
    x2 = x_ref[...] + g1_ref[...] * d
    h = x2 * lax.rsqrt(jnp.mean(x2 * x2, axis=-1, keepdims=True) + RMS_EPS) * n2g_ref[...]
    h = h * (1.0 + sc2_ref[...]) + sh2_ref[...]
    hb = h.astype(BF16)
    act = _silu(jnp.dot(hb, swg_ref[...], preferred_element_type=F32)) * jnp.dot(
        hb, swu_ref[...], preferred_element_type=F32)
    shared = jnp.dot(act.astype(BF16), swd_ref[...], preferred_element_type=F32)
    xs_ref[...] = x2 + g2_ref[...] * shared
    half = D_MODEL // 2
    hp_ref[...] = _pack_halves(h)

    logits = lax.dot_general(rwt_ref[...], h, (((1,), (1,)), ((), ())), precision=HIGHEST,
                             preferred_element_type=F32)
    scores = _sigmoid(logits)
    sel = scores + rb_ref[:, 0:1]
    neg = -jnp.inf
    sel3 = sel.reshape(N_GROUPS, GROUP_SIZE, tm)
    io_in = lax.broadcasted_iota(I32, (N_GROUPS, GROUP_SIZE, tm), 1)
    m1 = jnp.max(sel3, axis=1, keepdims=True)
    f1 = jnp.min(jnp.where(sel3 == m1, io_in, GROUP_SIZE), axis=1, keepdims=True)
    m2 = jnp.max(jnp.where(io_in == f1, neg, sel3), axis=1, keepdims=True)
    gs = (m1 + m2).reshape(N_GROUPS, tm)
    io_g = lax.broadcasted_iota(I32, (N_GROUPS, tm), 0)
    gmask = jnp.zeros((N_GROUPS, tm), jnp.bool_)
    for _ in range(TOPK_GROUPS):
        mg = jnp.max(gs, axis=0, keepdims=True)
        fg = _first_index(gs == mg, io_g, N_GROUPS)
        pick = io_g == fg
        gmask = jnp.logical_or(gmask, pick)
        gs = jnp.where(pick, neg, gs)
    emask = jnp.broadcast_to(gmask.reshape(N_GROUPS, 1, tm), (N_GROUPS, GROUP_SIZE, tm)).reshape(N_EXPERTS, tm)
    cur = jnp.where(emask, sel, neg)
    io_e = lax.broadcasted_iota(I32, (N_EXPERTS, tm), 0)
    picks, idxs, wts = [], [], []
    for _ in range(TOP_K):
        me = jnp.max(cur, axis=0, keepdims=True)
        fe = _first_index(cur == me, io_e, N_EXPERTS)
        pick = io_e == fe
        picks.append(pick)
        idxs.append(fe)
        wts.append(jnp.sum(jnp.where(pick, scores, 0.0), axis=0, keepdims=True))
        cur = jnp.where(pick, neg, cur)
    wsum = wts[0]
    for w in wts[1:]:
        wsum = wsum + w
    zero_i = jnp.zeros((1, tm), I32)
    zero_f = jnp.zeros((1, tm), F32)
    idx_ref[...] = jnp.concatenate(idxs + [zero_i, zero_i], axis=0)
    wts_ref[...] = jnp.concatenate([w / wsum * ROUTED_SCALE for w in wts] + [zero_f, zero_f], axis=0)
    cnt = picks[0].astype(F32)
    for pk in picks[1:]:
        cnt = cnt + pk.astype(F32)
    excl = jnp.dot(cnt.astype(BF16), ustrict_ref[...], preferred_element_type=F32) + carry[:, 0:1]
    ranks = [jnp.sum(jnp.where(pk, excl, 0.0), axis=0, keepdims=True).astype(I32) for pk in picks]
    rank_ref[...] = jnp.concatenate(ranks + [zero_i, zero_i], axis=0)
    total = carry[...] + jnp.sum(cnt, axis=1, keepdims=True)
    carry[...] = total
    cnt_ref[...] = total.astype(I32)


def _outproj(x2d, yf, yr, mod4, norm2_g, w_out, router_w, router_bias, swg, swu, swd, T):
    N, D = x2d.shape
    tm = min(512, T)
    tps = T // tm
    vec = lambda j: pl.BlockSpec((None, None, 1, D), lambda i: (i // tps, j, 0, 0))
    full = lambda a: pl.BlockSpec(a.shape, lambda i: (0, 0))
    ustrict = jnp.asarray(np.triu(np.ones((tm, tm), np.float32), 1), BF16)
    rb = jnp.broadcast_to(router_bias.reshape(N_EXPERTS, 1), (N_EXPERTS, LANES))
    consts = [norm2_g.reshape(1, D), w_out.astype(BF16), router_w.T, rb, swg.astype(BF16),
              swu.astype(BF16), swd.astype(BF16), ustrict]
    small = lambda dt: jax.ShapeDtypeStruct((8, N), dt)
    return pl.pallas_call(
        functools.partial(_outproj_kernel, tm=tm),
        grid=(N // tm,),
        in_specs=[pl.BlockSpec((tm, D), lambda i: (i, 0)),
                  pl.BlockSpec((tm, HALF), lambda i: (i, 0)),
                  pl.BlockSpec((tm, HALF), lambda i: (i, 0)),
                  vec(2), vec(3), vec(4), vec(5)] + [full(a) for a in consts],
        out_specs=[pl.BlockSpec((tm, D), lambda i: (i, 0)),
                   pl.BlockSpec((tm, D // 2), lambda i: (i, 0)),
                   pl.BlockSpec((8, tm), lambda i: (0, i)),
                   pl.BlockSpec((8, tm), lambda i: (0, i)),
                   pl.BlockSpec((8, tm), lambda i: (0, i)),
                   pl.BlockSpec((N_EXPERTS, LANES), lambda i: (0, 0))],
        out_shape=[jax.ShapeDtypeStruct((N, D), F32), jax.ShapeDtypeStruct((N, D // 2), jnp.uint32),
                   small(I32), small(F32), small(I32),
                   jax.ShapeDtypeStruct((N_EXPERTS, LANES), I32)],
        scratch_shapes=[pltpu.VMEM((N_EXPERTS, LANES), F32)],
        compiler_params=_cparams(("arbitrary",)),
        name="outproj",
    )(x2d, yf, yr, mod4, mod4, mod4, mod4, *consts)


def _dispatch_kernel(pend_ref, padded_ref, nu_ref, dest_ref, hp_ref, xs_ref, zeros, sem, zsem, *, tm, n_blocks):
    @pl.when(pl.program_id(0) == 0)
    def _():
        zeros[...] = jnp.zeros_like(zeros)

        def block_copy(start):
            return pltpu.make_async_copy(zeros, xs_ref.at[pl.ds(pl.multiple_of(start, EXPERT_BLOCK),
                                                                 EXPERT_BLOCK), :], zsem)

        def pad_loop(fn):
            def body(e, _):
                @pl.when(padded_ref[e] > 0)
                def _():
                    fn(block_copy(pend_ref[e] - EXPERT_BLOCK))
                return 0
            lax.fori_loop(0, N_EXPERTS, body, 0)

        def tail_loop(fn):
            def body(b, _):
                fn(block_copy(b * EXPERT_BLOCK))
                return 0
            lax.fori_loop(nu_ref[0], n_blocks, body, 0)

        pad_loop(lambda cp: cp.start())
        tail_loop(lambda cp: cp.start())
        pad_loop(lambda cp: cp.wait())
        tail_loop(lambda cp: cp.wait())

    def row_copy(j, k):
        return pltpu.make_async_copy(hp_ref.at[pl.ds(j, 1), :],
                                     xs_ref.at[pl.ds(dest_ref[0, 0, j * 8 + k], 1), :], sem)

    def issue(j, _):
        for k in range(TOP_K):
            row_copy(j, k).start()
        return 0

    def drain(j, _):
        for k in range(TOP_K):
            row_copy(j, k).wait()
        return 0

    lax.fori_loop(0, tm, issue, 0, unroll=DMA_UNROLL)
    lax.fori_loop(0, tm, drain, 0, unroll=DMA_UNROLL)


def _dest_tiles(dest, tm):
    n = dest.shape[1]
    return dest.T.reshape(n // tm, 1, tm * 8)


def _dispatch(hp, dest, pad_end, padded, n_used, n_rows):
    N, W = hp.shape
    tm = min(256, N)
    n_blocks = n_rows // EXPERT_BLOCK
    dest3 = _dest_tiles(dest, tm)
    return pl.pallas_call(
        functools.partial(_dispatch_kernel, tm=tm, n_blocks=n_blocks),
        grid_spec=pltpu.PrefetchScalarGridSpec(
            num_scalar_prefetch=3, grid=(N // tm,),
            in_specs=[pl.BlockSpec((1, 1, tm * 8), lambda i, *_: (i, 0, 0), memory_space=pltpu.SMEM),
                      pl.BlockSpec((tm, W), lambda i, *_: (i, 0))],
            out_specs=pl.BlockSpec(memory_space=pl.ANY),
            scratch_shapes=[pltpu.VMEM((EXPERT_BLOCK, W), jnp.uint32), pltpu.SemaphoreType.DMA,
                            pltpu.SemaphoreType.DMA]),
        out_shape=jax.ShapeDtypeStruct((n_rows, W), jnp.uint32),
        compiler_params=_cparams(("arbitrary",)),
        name="dispatch",
    )(pad_end, padded, n_used, dest3, hp)


def _experts_kernel(be_ref, nu_ref, xs_ref, wgf_ref, wuf_ref, wdf_ref, ys_ref, wg_ref, wu_ref, wd_ref):
    i = pl.program_id(0)
    live = i < nu_ref[0]
    new_expert = jnp.logical_or(i == 0, be_ref[i] != be_ref[jnp.maximum(i - 1, 0)])

    @pl.when(jnp.logical_not(live))
    def _():
        ys_ref[...] = jnp.zeros_like(ys_ref)

    @pl.when(jnp.logical_and(live, new_expert))
    def _():
        wg_ref[...] = wgf_ref[...].astype(BF16)
        wu_ref[...] = wuf_ref[...].astype(BF16)
        wd_ref[...] = wdf_ref[...].astype(BF16)

    @pl.when(live)
    def _():
        half = D_MODEL // 2
        lo, hi = _unpack_halves(xs_ref[...])
        lo = lo.astype(BF16)
        hi = hi.astype(BF16)
        gate = (jnp.dot(lo, wg_ref[0:half, :], preferred_element_type=F32)
                + jnp.dot(hi, wg_ref[half:D_MODEL, :], preferred_element_type=F32))
        up = (jnp.dot(lo, wu_ref[0:half, :], preferred_element_type=F32)
              + jnp.dot(hi, wu_ref[half:D_MODEL, :], preferred_element_type=F32))
        y = jnp.dot((_silu(gate) * up).astype(BF16), wd_ref[...], preferred_element_type=F32)
        ys_ref[...] = _pack_halves(y)


def _experts(xs, blk_e, n_used, wg, wu, wd):
    n_rows, W = xs.shape
    n_blocks = n_rows // EXPERT_BLOCK
    row_map = lambda i, be, nu: (jnp.minimum(i, nu[0] - 1), 0)
    return pl.pallas_call(
        _experts_kernel,
        grid_spec=pltpu.PrefetchScalarGridSpec(
            num_scalar_prefetch=2, grid=(n_blocks,),
            in_specs=[pl.BlockSpec((EXPERT_BLOCK, W), row_map),
                      pl.BlockSpec((None, D_MODEL, D_EXPERT), lambda i, be, nu: (be[i], 0, 0)),
                      pl.BlockSpec((None, D_MODEL, D_EXPERT), lambda i, be, nu: (be[i], 0, 0)),
                      pl.BlockSpec((None, D_EXPERT, D_MODEL), lambda i, be, nu: (be[i], 0, 0))],
            out_specs=pl.BlockSpec((EXPERT_BLOCK, W), lambda i, be, nu: (i, 0)),
            scratch_shapes=[pltpu.VMEM((D_MODEL, D_EXPERT), BF16), pltpu.VMEM((D_MODEL, D_EXPERT), BF16),
                            pltpu.VMEM((D_EXPERT, D_MODEL), BF16)]),
        out_shape=jax.ShapeDtypeStruct((n_rows, W), jnp.uint32),
        compiler_params=_cparams(("arbitrary",)),
        name="experts",
    )(blk_e, n_used, xs, wg, wu, wd)


def _combine_kernel(dest_ref, ys_ref, xs_ref, wts_ref, g2_ref, fg_ref, o_ref, buf, sem, *, tm):
    def row_copy(j, k):
        return pltpu.make_async_copy(ys_ref.at[pl.ds(dest_ref[0, 0, j * 8 + k], 1), :],
                                     buf.at[k, pl.ds(j, 1), :], sem)

    def issue(j, _):
        for k in range(TOP_K):
            row_copy(j, k).start()
        return 0

    def drain(j, _):
        for k in range(TOP_K):
            row_copy(j, k).wait()
        return 0

    lax.fori_loop(0, tm, issue, 0, unroll=DMA_UNROLL)
    lax.fori_loop(0, tm, drain, 0, unroll=DMA_UNROLL)
    w = wts_ref[...]
    lo = jnp.zeros((tm, D_MODEL // 2), F32)
    hi = jnp.zeros((tm, D_MODEL // 2), F32)
    for k in range(TOP_K):
        yk_lo, yk_hi = _unpack_halves(buf[k])
        wk = w[:, k:k + 1]
        lo = lo + wk * yk_lo
        hi = hi + wk * yk_hi
    x3 = xs_ref[...] + g2_ref[...] * jnp.concatenate([lo, hi], axis=1)
    y = x3 * lax.rsqrt(jnp.mean(x3 * x3, axis=-1, keepdims=True) + RMS_EPS) * fg_ref[...]
    o_ref[...] = y


def _combine(ys, dest, xsr, wts_t, mod4, final_g, T):
    N, D = xsr.shape
    W = ys.shape[1]
    tm = min(128, T)
    tps = T // tm
    dest3 = _dest_tiles(dest, tm)
    return pl.pallas_call(
        functools.partial(_combine_kernel, tm=tm),
        grid=(N // tm,),
        in_specs=[pl.BlockSpec((1, 1, tm * 8), lambda i: (i, 0, 0), memory_space=pltpu.SMEM),
                  pl.BlockSpec(memory_space=pl.ANY),
                  pl.BlockSpec((tm, D), lambda i: (i, 0)),
                  pl.BlockSpec((tm, 8), lambda i: (i, 0)),
                  pl.BlockSpec((None, None, 1, D), lambda i: (i // tps, 5, 0, 0)),
                  pl.BlockSpec((1, D), lambda i: (0, 0))],
        out_specs=pl.BlockSpec((tm, D), lambda i: (i, 0)),
        out_shape=jax.ShapeDtypeStruct((N, D), F32),
        scratch_shapes=[pltpu.VMEM((TOP_K, tm, W), jnp.uint32), pltpu.SemaphoreType.DMA],
        compiler_params=_cparams(("arbitrary",)),
        name="combine",
    )(dest3, ys, xsr, wts_t, mod4, final_g.reshape(1, D))


def _pack_w_in(w_in):
    D = w_in.shape[0]
    fox_cols = 4 * HALF + 3 * N_HEADS
    wf = w_in[:, :fox_cols]
    wr = w_in[:, fox_cols:]
    o = 3 * HALF
    pad = lambda a, n: jnp.concatenate([a, jnp.zeros((D, n - a.shape[1]), a.dtype)], axis=1)
    parts = [wf[:, :4 * HALF], wr[:, :o],
             pad(wf[:, 4 * HALF:], LANES),
             pad(wr[:, o:o + DECAY_LORA], LANES),
             pad(wr[:, o + DECAY_LORA:o + DECAY_LORA + ICLR_LORA], LANES),
             pad(wr[:, o + DECAY_LORA + ICLR_LORA:], 2 * LANES)]
    return jnp.concatenate(parts, axis=1).astype(BF16)


def _pack_mu(mu):
    o = 3 * HALF
    pad = lambda a, n: jnp.concatenate([a, jnp.zeros((n - a.shape[0],), a.dtype)])
    small = jnp.concatenate([jnp.zeros((LANES,), mu.dtype),
                             pad(mu[o:o + DECAY_LORA], LANES),
                             pad(mu[o + DECAY_LORA:o + DECAY_LORA + ICLR_LORA], LANES),
                             pad(mu[o + DECAY_LORA + ICLR_LORA:], 2 * LANES)])
    return mu[:o].reshape(1, o), small.reshape(1, Z_SMALL)


def kernel(x, c, norm1_g, norm2_g, ada_w, ada_b, w_in, w_out, fox_qn_g, fox_kn_g, fox_on_g, fox_forget_b,
           rw_mu, rw_w0, rw_decay_up, rw_a0, rw_iclr_up, rw_gate_up, rw_k_k, rw_k_a, rw_r_k, rw_lnx_g,
           rw_lnx_b, router_w, router_bias, exp_w_gate, exp_w_up, exp_w_down, sh_w_gate, sh_w_up,
           sh_w_down, final_g):
    B, T, D = x.shape
    N = B * T
    depth = norm1_g.shape[0]
    assert depth == 1, "the combine kernel fuses the final RMSNorm, so exactly one layer is supported"
    xf = x.reshape(N, D)
    for l in range(depth):
        mod4 = _ada(c, ada_w[l], ada_b[l]).reshape(B, 6, 1, D)
        mu_big, mu_small = _pack_mu(rw_mu[l])
        zm, zs = _inproj(xf, mod4, norm1_g[l], _pack_w_in(w_in[l]), mu_big, mu_small, T)
        qp, kp, vp = _foxprep(zm, zs, fox_forget_b[l], fox_qn_g[l], fox_kn_g[l], T)
        y_fox = _attention(qp, kp, vp, zm, fox_on_g[l], B, T)
        rw = dict(rw_w0=rw_w0[l], rw_decay_up=rw_decay_up[l], rw_a0=rw_a0[l], rw_iclr_up=rw_iclr_up[l],
                  rw_gate_up=rw_gate_up[l], rw_k_k=rw_k_k[l], rw_k_a=rw_k_a[l], rw_r_k=rw_r_k[l],
                  rw_lnx_g=rw_lnx_g[l], rw_lnx_b=rw_lnx_b[l])
        y_rwkv = _rwkv(zm, zs, rw, B, T)
        xsr, hp, idx_t, wts_t, rank_t, cnt = _outproj(
            xf, y_fox, y_rwkv, mod4, norm2_g[l], w_out[l], router_w[l], router_bias[l],
            sh_w_gate[l], sh_w_up[l], sh_w_down[l], T)
        counts = cnt[:, 0]
        padded = (counts + EXPERT_BLOCK - 1) // EXPERT_BLOCK * EXPERT_BLOCK
        pad_end = jnp.cumsum(padded).astype(I32)
        pad_start = pad_end - padded
        n_blocks = -(-(N * TOP_K) // EXPERT_BLOCK) + N_EXPERTS
        n_rows = n_blocks * EXPERT_BLOCK
        experts = jnp.arange(N_EXPERTS, dtype=I32)
        dest = rank_t + jnp.sum(jnp.where(idx_t[:, :, None] == experts, pad_start, 0), axis=-1)
        blk_start = jnp.arange(n_blocks, dtype=I32) * EXPERT_BLOCK
        blk_e = jnp.minimum(jnp.sum((pad_end[None, :] <= blk_start[:, None]).astype(I32), axis=1),
                            N_EXPERTS - 1)
        n_used = (pad_end[N_EXPERTS - 1:] // EXPERT_BLOCK).astype(I32)
        xs = _dispatch(hp, dest, pad_end, padded.astype(I32), n_used, n_rows)
        ys = _experts(xs, blk_e, n_used, exp_w_gate[l], exp_w_up[l], exp_w_down[l])
        xf = _combine(ys, dest, xsr, wts_t.T, mod4, final_g, T)
    return xf.reshape(B, T, D)
```

```python
import functools

import jax
import jax.numpy as jnp
import numpy as np
from jax import lax
from jax.experimental import pallas as pl
from jax.experimental.pallas import tpu as pltpu

F32 = jnp.float32
BF16 = jnp.bfloat16
I32 = jnp.int32
HIGHEST = lax.Precision.HIGHEST

D_MODEL = 1024
HEAD_DIM = 64
N_HEADS = 8
HALF = N_HEADS * HEAD_DIM
RMS_EPS = 1e-6
LNX_EPS = 64e-5
LOG2E = 1.4426950408889634
DECAY_LORA = 64
ICLR_LORA = 64
GATE_LORA = 160
N_EXPERTS = 64
N_GROUPS = 8
GROUP_SIZE = N_EXPERTS // N_GROUPS
TOPK_GROUPS = 4
TOP_K = 6
D_EXPERT = 256
D_SHARED = 256
ROUTED_SCALE = 2.5
EXPERT_BLOCK = 512

LANES = 128
Z_MAIN = 4 * HALF + 3 * HALF
Z_SMALL = 5 * LANES
VMEM_LIMIT = 56 * 1024 * 1024
DMA_UNROLL = 4
WAIT_CHUNK = 16
MOE_TILE = 256
GRANULE = 8
MOE_SORTED_ROWS = MOE_TILE * TOP_K + N_EXPERTS * GRANULE
RWKV_CHUNK = 64
RWKV_SEQS_PER_STEP = 2


def _cparams(semantics):
    return pltpu.CompilerParams(dimension_semantics=semantics, vmem_limit_bytes=VMEM_LIMIT)


def _mm(a, b):
    return jnp.dot(a.astype(BF16), b.astype(BF16), preferred_element_type=F32)


def _mm_nt(a, b):
    return lax.dot_general(a.astype(BF16), b.astype(BF16), (((1,), (1,)), ((), ())),
                           preferred_element_type=F32)


def _mm_tn(a, b):
    return lax.dot_general(a.astype(BF16), b.astype(BF16), (((0,), (0,)), ((), ())),
                           preferred_element_type=F32)


def _mm_f32(a, b):
    return jnp.dot(a, b, precision=HIGHEST, preferred_element_type=F32)


def _bf16_pieces(x, passes):
    pieces = []
    for _ in range(passes):
        piece = x.astype(BF16)
        pieces.append(piece)
        x = x - piece.astype(F32)
    return pieces


def _mm_split(m01, x, passes=3):
    return sum(jnp.dot(m01, p, preferred_element_type=F32) for p in _bf16_pieces(x, passes))


def _mm_split_r(x, m01, passes=2):
    return sum(jnp.dot(p, m01, preferred_element_type=F32) for p in _bf16_pieces(x, passes))


def _sigmoid(x):
    return 1.0 / (1.0 + jnp.exp(-x))


def _softplus(x):
    return jnp.maximum(x, 0.0) + jnp.log(1.0 + jnp.exp(-jnp.abs(x)))


def _silu(x):
    return x * _sigmoid(x)


def _pack_halves(x):
    w = x.shape[1] // 2
    bits = lambda t: lax.bitcast_convert_type(t.astype(jnp.bfloat16).astype(F32), jnp.uint32)
    return (bits(x[:, 0:w]) >> 16) | (bits(x[:, w:2 * w]) & jnp.uint32(0xFFFF0000))


def _unpack_halves(p):
    lo = lax.bitcast_convert_type(p << 16, F32)
    hi = lax.bitcast_convert_type(p & jnp.uint32(0xFFFF0000), F32)
    return lo, hi


def _shift_rows(z, carry_ref, first):
    rows = z.shape[0]
    prev_row = jnp.where(first, 0.0, carry_ref[0:1, :])
    prev = pltpu.roll(z, 1, 0)
    row0 = lax.broadcasted_iota(I32, (rows, 1), 0) == 0
    prev = jnp.where(row0, prev_row, prev)
    carry_ref[0:1, :] = z[rows - 1:rows, :]
    return prev


def _ada_kernel(c_ref, w_ref, b_ref, o_ref):
    o_ref[...] = _mm_f32(_silu(c_ref[...]), w_ref[...]) + b_ref[...]


def _ada(c, ada_w, ada_b):
    B, D = c.shape
    n_out = ada_w.shape[1]
    tn = 512
    return pl.pallas_call(
        _ada_kernel,
        grid=(n_out // tn,),
        in_specs=[pl.BlockSpec((B, D), lambda j: (0, 0)),
                  pl.BlockSpec((D, tn), lambda j: (0, j)),
                  pl.BlockSpec((1, tn), lambda j: (0, j))],
        out_specs=pl.BlockSpec((B, tn), lambda j: (0, j)),
        out_shape=jax.ShapeDtypeStruct((B, n_out), F32),
        compiler_params=_cparams(("arbitrary",)),
        name="ada",
    )(c, ada_w, ada_b.reshape(1, n_out))


def _inproj_kernel(x_ref, g_ref, sh_ref, sc_ref, w_ref, mub_ref, mus_ref, zm_ref, zs_ref,
                   carry_b, carry_s, *, tiles_per_seq):
    first = (pl.program_id(0) % tiles_per_seq) == 0
    x = x_ref[...]
    h = x * lax.rsqrt(jnp.mean(x * x, axis=-1, keepdims=True) + RMS_EPS) * g_ref[...]
    hb = (h * (1.0 + sc_ref[...]) + sh_ref[...]).astype(BF16)
    nf = 4 * HALF
    zm_ref[:, 0:nf] = jnp.dot(hb, w_ref[:, 0:nf], preferred_element_type=F32).astype(BF16)
    zr = jnp.dot(hb, w_ref[:, nf:Z_MAIN], preferred_element_type=F32)
    zr = zr + mub_ref[...] * (_shift_rows(zr, carry_b, first) - zr)
    zm_ref[:, nf:Z_MAIN] = zr.astype(BF16)
    zs = jnp.dot(hb, w_ref[:, Z_MAIN:Z_MAIN + Z_SMALL], preferred_element_type=F32)
    zs_ref[...] = zs + mus_ref[...] * (_shift_rows(zs, carry_s, first) - zs)


def _inproj(x2d, mod4, norm_g, w_all, mu_big, mu_small, T):
    N, D = x2d.shape
    tm = min(512, T)
    tps = T // tm
    vec = lambda j: pl.BlockSpec((None, None, 1, D), lambda i: (i // tps, j, 0, 0))
    return pl.pallas_call(
        functools.partial(_inproj_kernel, tiles_per_seq=tps),
        grid=(N // tm,),
        in_specs=[pl.BlockSpec((tm, D), lambda i: (i, 0)),
                  pl.BlockSpec((1, D), lambda i: (0, 0)),
                  vec(0), vec(1),
                  pl.BlockSpec((D, Z_MAIN + Z_SMALL), lambda i: (0, 0)),
                  pl.BlockSpec((1, 3 * HALF), lambda i: (0, 0)),
                  pl.BlockSpec((1, Z_SMALL), lambda i: (0, 0))],
        out_specs=[pl.BlockSpec((tm, Z_MAIN), lambda i: (i, 0)),
                   pl.BlockSpec((tm, Z_SMALL), lambda i: (i, 0))],
        out_shape=[jax.ShapeDtypeStruct((N, Z_MAIN), BF16),
                   jax.ShapeDtypeStruct((N, Z_SMALL), F32)],
        scratch_shapes=[pltpu.VMEM((8, 3 * HALF), F32), pltpu.VMEM((8, Z_SMALL), F32)],
        compiler_params=_cparams(("arbitrary",)),
        name="inproj",
    )(x2d, norm_g.reshape(1, D), mod4, mod4, w_all, mu_big, mu_small)


def _foxprep_kernel(q_ref, k_ref, v_ref, zs_ref, fb_ref, qg_ref, kg_ref, tri_ref, eexp_ref,
                    esum_ref, e8_ref, plq_ref, plk_ref, plv_ref, cq_ref, ck_ref, cv_ref,
                    qp_ref, kp_ref, vp_ref, carry_k, carry_v, carry_c, *, tiles_per_seq):
    first = (pl.program_id(0) % tiles_per_seq) == 0
    zs = zs_ref[...]
    logf = -_softplus(-(zs + fb_ref[...]))
    cum = _mm_split(tri_ref[...], logf) + jnp.where(first, 0.0, carry_c[0:1, :])
    carry_c[0:1, :] = cum[cum.shape[0] - 1:, :]
    cum2 = cum * LOG2E
    c_hi = cum2.astype(BF16)
    r1 = cum2 - c_hi.astype(F32)
    c_mid = r1.astype(BF16)
    c_lo = (r1 - c_mid.astype(F32)).astype(BF16)
    a_full = _mm_split_r(_sigmoid(zs), eexp_ref[...])
    k = k_ref[...].astype(F32)
    v = v_ref[...].astype(F32)
    a_k = a_full[:, 0:HALF]
    a_v = a_full[:, HALF:2 * HALF]
    k = a_k * _shift_rows(k, carry_k, first) + (1.0 - a_k) * k
    v = a_v * _shift_rows(v, carry_v, first) + (1.0 - a_v) * v
    q = q_ref[...].astype(F32)

    def head_rms(t, gain):
        ms = _mm_split_r(t * t, esum_ref[...]) * (1.0 / HEAD_DIM)
        inv = _mm_split_r(lax.rsqrt(ms + RMS_EPS), e8_ref[...])
        return t * inv * gain

    qn = head_rms(q, qg_ref[...]) * (HEAD_DIM ** -0.5 * LOG2E)
    kn = head_rms(k, kg_ref[...])
    aug = [c_hi, c_mid, c_lo]
    lhs_q = jnp.concatenate([qn.astype(BF16)] + aug, axis=1)
    lhs_k = jnp.concatenate([kn.astype(BF16)] + aug, axis=1)
    qp_ref[...] = (jnp.dot(lhs_q, plq_ref[...], preferred_element_type=F32) + cq_ref[...]).astype(BF16)
    kp_ref[...] = (jnp.dot(lhs_k, plk_ref[...], preferred_element_type=F32) + ck_ref[...]).astype(BF16)
    vp_ref[...] = (jnp.dot(v.astype(BF16), plv_ref[...], preferred_element_type=F32)
                   + cv_ref[...]).astype(BF16)


def _fox_constants(tm):
    hp = N_HEADS * LANES
    eexp = np.zeros((LANES, 2 * HALF), np.float32)
    esum = np.zeros((HALF, LANES), np.float32)
    e8 = np.zeros((LANES, HALF), np.float32)
    plq = np.zeros((HALF + 3 * LANES, hp), np.float32)
    plk = np.zeros((HALF + 3 * LANES, hp), np.float32)
    plv = np.zeros((HALF, hp), np.float32)
    cq = np.zeros((1, hp), np.float32)
    ck = np.zeros((1, hp), np.float32)
    cv = np.zeros((1, hp), np.float32)
    for h in range(N_HEADS):
        sl = slice(h * HEAD_DIM, (h + 1) * HEAD_DIM)
        eexp[8 + h, sl] = 1.0
        eexp[16 + h, HALF + h * HEAD_DIM:HALF + (h + 1) * HEAD_DIM] = 1.0
        esum[sl, h] = 1.0
        e8[h, sl] = 1.0
        base = h * LANES
        for d in range(HEAD_DIM):
            plq[h * HEAD_DIM + d, base + d] = 1.0
            plk[h * HEAD_DIM + d, base + d] = 1.0
            plv[h * HEAD_DIM + d, base + d] = 1.0
        for j in range(3):
            plq[HALF + j * LANES + h, base + HEAD_DIM + j] = 1.0
            plk[HALF + j * LANES + h, base + HEAD_DIM + 3 + j] = -1.0
            cq[0, base + HEAD_DIM + 3 + j] = 1.0
            ck[0, base + HEAD_DIM + j] = 1.0
        cv[0, base + HEAD_DIM] = 1.0
    tri = np.tril(np.ones((tm, tm), np.float32))
    bf = lambda a: jnp.asarray(a, BF16)
    return dict(tri=bf(tri), eexp=bf(eexp), esum=bf(esum), e8=bf(e8),
                plq=bf(plq), plk=bf(plk), plv=bf(plv), cq=jnp.asarray(cq), ck=jnp.asarray(ck),
                cv=jnp.asarray(cv))


def _foxprep(zm, zs, forget_b, qn_g, kn_g, T):
    N = zm.shape[0]
    tm = min(256, T)
    tps = T // tm
    cst = _fox_constants(tm)
    hp = N_HEADS * LANES
    fb = jnp.zeros((1, LANES), F32).at[0, :N_HEADS].set(forget_b)
    full = lambda a: pl.BlockSpec(a.shape, lambda i: (0, 0))
    consts = [fb, qn_g.reshape(1, HALF), kn_g.reshape(1, HALF), cst["tri"], cst["eexp"], cst["esum"],
              cst["e8"], cst["plq"], cst["plk"], cst["plv"], cst["cq"], cst["ck"], cst["cv"]]
    return pl.pallas_call(
        functools.partial(_foxprep_kernel, tiles_per_seq=tps),
        grid=(N // tm,),
        in_specs=[pl.BlockSpec((tm, HALF), lambda i: (i, 0)),
                  pl.BlockSpec((tm, HALF), lambda i: (i, 1)),
                  pl.BlockSpec((tm, HALF), lambda i: (i, 2)),
                  pl.BlockSpec((tm, LANES), lambda i: (i, 0))] + [full(a) for a in consts],
        out_specs=[pl.BlockSpec((tm, hp), lambda i: (i, 0))] * 3,
        out_shape=[jax.ShapeDtypeStruct((N, hp), BF16)] * 3,
        scratch_shapes=[pltpu.VMEM((8, HALF), F32), pltpu.VMEM((8, HALF), F32),
                        pltpu.VMEM((8, LANES), F32)],
        compiler_params=_cparams(("arbitrary",)),
        name="foxprep",
    )(zm, zm, zm, zs, *consts)


def _attn_kernel(q_ref, k_ref, v_ref, g_ref, ong_ref, o_ref, *, tq):
    i = pl.program_id(2)
    row = lax.broadcasted_iota(I32, (tq, tq), 0)
    col = lax.broadcasted_iota(I32, (tq, tq), 1)
    causal = col <= row
    heads = range(2)
    lanes = [slice(hh * LANES, (hh + 1) * LANES) for hh in heads]
    qs = [q_ref[:, lanes[hh]] for hh in heads]

    def scores(j):
        start = pl.multiple_of(j * tq, tq)
        return tuple(lax.dot_general(qs[hh], k_ref[pl.ds(start, tq), lanes[hh]], (((1,), (1,)), ((), ())),
                                     preferred_element_type=F32) for hh in heads)

    def update(j, s, m, acc, masked):
        start = pl.multiple_of(j * tq, tq)
        m_out, acc_out = [], []
        for hh in heads:
            sh = jnp.where(causal, s[hh], -jnp.inf) if masked else s[hh]
            m_new = jnp.maximum(m[hh], jnp.max(sh, axis=1, keepdims=True))
            p = jnp.exp2(sh - m_new)
            v = v_ref[pl.ds(start, tq), lanes[hh]]
            acc_out.append(jnp.exp2(m[hh] - m_new) * acc[hh]
                           + jnp.dot(p.astype(BF16), v, preferred_element_type=F32))
            m_out.append(m_new)
        return tuple(m_out), tuple(acc_out)

    def body(j, carry):
        s, m, acc = carry
        s_next = scores(j + 1)
        m, acc = update(j, s, m, acc, False)
        return s_next, m, acc

    init = (scores(0), tuple(jnp.full((tq, 1), -jnp.inf, F32) for _ in heads),
            tuple(jnp.zeros((tq, LANES), F32) for _ in heads))
    s, m, acc = lax.fori_loop(0, i, body, init)
    _, acc = update(i, s, m, acc, True)
    lane = lax.broadcasted_iota(I32, (tq, LANES), 1)
    outs = []
    for hh in heads:
        o = acc[hh] / acc[hh][:, HEAD_DIM:HEAD_DIM + 1]
        ms = jnp.sum(jnp.where(lane < HEAD_DIM, o * o, 0.0), axis=1, keepdims=True) * (1.0 / HEAD_DIM)
        outs.append(o * lax.rsqrt(ms + RMS_EPS))
    o = jnp.where(lane < HEAD_DIM, outs[0], pltpu.roll(outs[1], HEAD_DIM, 1))
    y = o * ong_ref[...] * _sigmoid(g_ref[...].astype(F32))
    o_ref[...] = y.astype(BF16)


def _attention(qp, kp, vp, zm, on_g, B, T):
    N = qp.shape[0]
    tq = min(256, T)
    nq = T // tq
    pairs = N_HEADS // 2
    g_col0 = 3 * HALF // LANES
    return pl.pallas_call(
        functools.partial(_attn_kernel, tq=tq),
        grid=(B, pairs, nq),
        in_specs=[pl.BlockSpec((tq, 2 * LANES), lambda b, p, i: (b * nq + i, p)),
                  pl.BlockSpec((T, 2 * LANES), lambda b, p, i: (b, p)),
                  pl.BlockSpec((T, 2 * LANES), lambda b, p, i: (b, p)),
                  pl.BlockSpec((tq, LANES), lambda b, p, i: (b * nq + i, g_col0 + p)),
                  pl.BlockSpec((None, 1, LANES), lambda b, p, i: (p, 0, 0))],
        out_specs=pl.BlockSpec((tq, LANES), lambda b, p, i: (b * nq + i, p)),
        out_shape=jax.ShapeDtypeStruct((N, HALF), BF16),
        compiler_params=_cparams(("arbitrary", "arbitrary", "arbitrary")),
        name="attn",
    )(qp, kp, vp, zm, on_g.reshape(pairs, 1, LANES))


def _rwkv_kernel(r_ref, k_ref, v_ref, zs_ref, w0_ref, dup_ref, a0_ref, iup_ref, gup_ref, kk_ref,
                 ka_ref, rk_ref, lng_ref, lnb_ref, tri_ref, o_ref, state, ybuf, *, chunk, nb):
    C = chunk
    c_idx = pl.program_id(1)

    @pl.when(c_idx == 0)
    def _():
        state[...] = jnp.zeros_like(state)

    R = nb * C
    r = r_ref[...].reshape(R, HALF).astype(F32)
    k = k_ref[...].reshape(R, HALF).astype(F32)
    v = v_ref[...].reshape(R, HALF).astype(F32)
    zs = zs_ref[...].reshape(R, Z_SMALL)
    wd = zs[:, LANES:2 * LANES]
    ad = zs[:, 2 * LANES:3 * LANES]
    gd = zs[:, 3 * LANES:5 * LANES]
    wl = w0_ref[...] + _mm(jnp.tanh(wd), dup_ref[...])
    lw = -jnp.exp(-_softplus(-wl) - 0.5)
    a = _sigmoid(a0_ref[...] + _mm(ad, iup_ref[...]))
    g = _mm(_sigmoid(gd), gup_ref[...])
    kk = k * kk_ref[...]
    k2 = k * (1.0 + (a - 1.0) * ka_ref[...])
    cl = _mm_split(tri_ref[...], lw)
    cl_end = jnp.concatenate(
        [jnp.broadcast_to(cl[(bi + 1) * C - 1:(bi + 1) * C, :], (C, HALF)) for bi in range(nb)], axis=0)
    e_neg = jnp.exp(-cl)
    e_tail = jnp.exp(cl_end - cl)
    pre = dict(rt=r * jnp.exp(cl), at=-kk * jnp.exp(cl - lw), kh=k2 * e_neg, bh=kk * a * e_neg,
               kb=k2 * e_tail, bb=kk * a * e_tail, v=v, kk=kk, pend=jnp.exp(cl_end),
               rkr=r * k2 * rk_ref[...], g=g)

    row = lax.broadcasted_iota(I32, (C, C), 0)
    col = lax.broadcasted_iota(I32, (C, C), 1)
    lower = col <= row
    strict = col < row
    rowk = lax.broadcasted_iota(I32, (HEAD_DIM, HEAD_DIM), 0)
    colk = lax.broadcasted_iota(I32, (HEAD_DIM, HEAD_DIM), 1)
    eye_k = rowk == colk
    n_sq = int(np.log2(C)) - 1
    units = [(bi, h) for bi in range(nb) for h in range(N_HEADS)]

    def part(name, bi, h):
        return pre[name][bi * C:(bi + 1) * C, h * HEAD_DIM:(h + 1) * HEAD_DIM]

    ops = []
    for bi, h in units:
        kk_h = part("kk", bi, h)
        inv = 1.0 / jnp.maximum(jnp.sqrt(jnp.sum(kk_h * kk_h, axis=1, keepdims=True)), 1e-12)
        ops.append(dict(rt=part("rt", bi, h), at=part("at", bi, h) * inv, kh=part("kh", bi, h),
                        bh=part("bh", bi, h) * inv, kb=part("kb", bi, h), bb=part("bb", bi, h) * inv,
                        v=part("v", bi, h)))
    ra = [jnp.concatenate([o["rt"], o["at"]], axis=0).astype(BF16) for o in ops]
    g1 = [_mm_nt(x, o["kh"]) for x, o in zip(ra, ops)]
    g2 = [_mm_nt(x, o["bh"]) for x, o in zip(ra, ops)]
    a_rb = [jnp.where(lower, t[0:C], 0.0).astype(BF16) for t in g2]
    pw = [jnp.where(strict, t[C:2 * C], 0.0) for t in g2]
    av = [_mm(jnp.concatenate([jnp.where(lower, t[0:C], 0.0), jnp.where(strict, t[C:2 * C], 0.0)], axis=0),
              o["v"]) for t, o in zip(g1, ops)]
    xs = [jnp.concatenate([o["at"], t[C:2 * C]], axis=1) for o, t in zip(ops, av)]
    for level in range(n_sq + 1):
        pb = [p.astype(BF16) for p in pw]
        xs = [x + _mm(p, x) for p, x in zip(pb, xs)]
        if level < n_sq:
            pw = [jnp.dot(p, p, preferred_element_type=F32) for p in pb]
    xb = [x.astype(BF16) for x in xs]
    rb = [jnp.dot(p, x, preferred_element_type=F32) for p, x in zip(a_rb, xb)]
    bx = [_mm_tn(o["bb"], x) for o, x in zip(ops, xb)]
    kv = [_mm_tn(o["kb"], o["v"]) for o in ops]
    for u, (bi, h) in enumerate(units):
        o = ops[u]
        sl = slice(h * HEAD_DIM, (h + 1) * HEAD_DIM)
        r2 = o["rt"] + rb[u][:, 0:HEAD_DIM]
        y0 = rb[u][:, HEAD_DIM:2 * HEAD_DIM] + av[u][0:C]
        m_mat = jnp.where(eye_k, part("pend", bi, h)[0:1, :], 0.0) + bx[u][:, 0:HEAD_DIM]
        g_mat = bx[u][:, HEAD_DIM:2 * HEAD_DIM] + kv[u]
        out = _mm(jnp.concatenate([r2, m_mat], axis=0), state[bi, h])
        state[bi, h] = out[C:C + HEAD_DIM] + g_mat
        y = out[0:C] + y0
        cen = y - jnp.sum(y, axis=1, keepdims=True) * (1.0 / HEAD_DIM)
        var = jnp.sum(cen * cen, axis=1, keepdims=True) * (1.0 / HEAD_DIM)
        bonus = jnp.sum(part("rkr", bi, h), axis=1, keepdims=True) * o["v"]
        y = cen * lax.rsqrt(var + LNX_EPS) * lng_ref[:, sl] + lnb_ref[:, sl] + bonus
        ybuf[bi, :, sl] = y * part("g", bi, h)
    o_ref[...] = ybuf[...].astype(BF16)


def _rwkv(zm, zs, p, B, T):
    N = zm.shape[0]
    C = min(RWKV_CHUNK, T)
    nc = T // C
    nb = RWKV_SEQS_PER_STEP if B % RWKV_SEQS_PER_STEP == 0 else 1
    tri = np.kron(np.eye(nb, dtype=np.float32), np.tril(np.ones((C, C), np.float32)))
    pad_rows = lambda w, rows: jnp.zeros((rows, HALF), F32).at[:w.shape[0]].set(w)
    row = lambda a: a.reshape(1, HALF)
    consts = [row(p["rw_w0"]), pad_rows(p["rw_decay_up"], LANES), row(p["rw_a0"]),
              pad_rows(p["rw_iclr_up"], LANES), pad_rows(p["rw_gate_up"], 2 * LANES),
              row(p["rw_k_k"]), row(p["rw_k_a"]), row(p["rw_r_k"]), row(p["rw_lnx_g"]),
              row(p["rw_lnx_b"]), jnp.asarray(tri, BF16)]
    full = lambda a: pl.BlockSpec(a.shape, lambda b, c: (0, 0))
    rcol = 4 * HALF // HALF
    zm3 = zm.reshape(B, T, Z_MAIN)
    out = pl.pallas_call(
        functools.partial(_rwkv_kernel, chunk=C, nb=nb),
        grid=(B // nb, nc),
        in_specs=[pl.BlockSpec((nb, C, HALF), lambda b, c: (b, c, rcol)),
                  pl.BlockSpec((nb, C, HALF), lambda b, c: (b, c, rcol + 1)),
                  pl.BlockSpec((nb, C, HALF), lambda b, c: (b, c, rcol + 2)),
                  pl.BlockSpec((nb, C, Z_SMALL), lambda b, c: (b, c, 0))] + [full(a) for a in consts],
        out_specs=pl.BlockSpec((nb, C, HALF), lambda b, c: (b, c, 0)),
        out_shape=jax.ShapeDtypeStruct((B, T, HALF), BF16),
        scratch_shapes=[pltpu.VMEM((nb, N_HEADS, HEAD_DIM, HEAD_DIM), F32), pltpu.VMEM((nb, C, HALF), F32)],
        compiler_params=_cparams(("arbitrary", "arbitrary")),
        name="rwkv",
    )(zm3, zm3, zm3, zs.reshape(B, T, Z_SMALL), *consts)
    return out.reshape(N, HALF)


def _first_index(mask, iota, big):
    return jnp.min(jnp.where(mask, iota, big), axis=0, keepdims=True)


def _outproj_kernel(x_ref, yf_ref, yr_ref, g1_ref, sh2_ref, sc2_ref, g2_ref, n2g_ref, wo_ref, rwt_ref,
                    rb_ref, swg_ref, swu_ref, swd_ref, ustrict_ref,
                    xs_ref, hp_ref, idx_ref, wts_ref, rank_ref, *, tm):
    d = (jnp.dot(yf_ref[...], wo_ref[0:HALF, :], preferred_element_type=F32)
         + jnp.dot(yr_ref[...], wo_ref[HALF:2 * HALF, :], preferred_element_type=F32))
    x2 = x_ref[...] + g1_ref[...] * d
    h = x2 * lax.rsqrt(jnp.mean(x2 * x2, axis=-1, keepdims=True) + RMS_EPS) * n2g_ref[...]
    h = h * (1.0 + sc2_ref[...]) + sh2_ref[...]
    hb = h.astype(BF16)
    act = _silu(jnp.dot(hb, swg_ref[...], preferred_element_type=F32)) * jnp.dot(
        hb, swu_ref[...], preferred_element_type=F32)
    shared = jnp.dot(act.astype(BF16), swd_ref[...], preferred_element_type=F32)
    xs_ref[...] = x2 + g2_ref[...] * shared
    half = D_MODEL // 2
    hp_ref[...] = _pack_halves(h)

    logits = lax.dot_general(rwt_ref[...], h, (((1,), (1,)), ((), ())), precision=HIGHEST,
                             preferred_element_type=F32)
    scores = _sigmoid(logits)
    sel = scores + rb_ref[:, 0:1]
    neg = -jnp.inf
    sel3 = sel.reshape(N_GROUPS, GROUP_SIZE, tm)
    io_in = lax.broadcasted_iota(I32, (N_GROUPS, GROUP_SIZE, tm), 1)
    m1 = jnp.max(sel3, axis=1, keepdims=True)
    f1 = jnp.min(jnp.where(sel3 == m1, io_in, GROUP_SIZE), axis=1, keepdims=True)
    m2 = jnp.max(jnp.where(io_in == f1, neg, sel3), axis=1, keepdims=True)
    gs = (m1 + m2).reshape(N_GROUPS, tm)
    io_g = lax.broadcasted_iota(I32, (N_GROUPS, tm), 0)
    gmask = jnp.zeros((N_GROUPS, tm), jnp.bool_)
    for _ in range(TOPK_GROUPS):
        mg = jnp.max(gs, axis=0, keepdims=True)
        fg = _first_index(gs == mg, io_g, N_GROUPS)
        pick = io_g == fg
        gmask = jnp.logical_or(gmask, pick)
        gs = jnp.where(pick, neg, gs)
    emask = jnp.broadcast_to(gmask.reshape(N_GROUPS, 1, tm), (N_GROUPS, GROUP_SIZE, tm)).reshape(N_EXPERTS, tm)
    cur = jnp.where(emask, sel, neg)
    io_e = lax.broadcasted_iota(I32, (N_EXPERTS, tm), 0)
    picks, idxs, wts = [], [], []
    for _ in range(TOP_K):
        me = jnp.max(cur, axis=0, keepdims=True)
        fe = _first_index(cur == me, io_e, N_EXPERTS)
        pick = io_e == fe
        picks.append(pick)
        idxs.append(fe)
        wts.append(jnp.sum(jnp.where(pick, scores, 0.0), axis=0, keepdims=True))
        cur = jnp.where(pick, neg, cur)
    wsum = wts[0]
    for w in wts[1:]:
        wsum = wsum + w
    zero_i = jnp.zeros((1, tm), I32)
    zero_f = jnp.zeros((1, tm), F32)
    idx_ref[...] = jnp.concatenate(idxs + [zero_i, zero_i], axis=0)
    wts_ref[...] = jnp.concatenate([w / wsum * ROUTED_SCALE for w in wts] + [zero_f, zero_f], axis=0)
    cnt = picks[0].astype(F32)
    for pk in picks[1:]:
        cnt = cnt + pk.astype(F32)
    excl = jnp.dot(cnt.astype(BF16), ustrict_ref[...], preferred_element_type=F32)
    ranks = [jnp.sum(jnp.where(pk, excl, 0.0), axis=0, keepdims=True).astype(I32) for pk in picks]
    rank_ref[...] = jnp.concatenate(ranks + [zero_i, zero_i], axis=0)


def _outproj(x2d, yf, yr, mod4, norm2_g, w_out, router_w, router_bias, swg, swu, swd, T):
    N, D = x2d.shape
    tm = min(512, T)
    tps = T // tm
    vec = lambda j: pl.BlockSpec((None, None, 1, D), lambda i: (i // tps, j, 0, 0))
    full = lambda a: pl.BlockSpec(a.shape, lambda i: (0, 0))
    ts = min(MOE_TILE, tm)
    ustrict = jnp.asarray(np.kron(np.eye(tm // ts, dtype=np.float32),
                                  np.triu(np.ones((ts, ts), np.float32), 1)), BF16)
    rb = jnp.broadcast_to(router_bias.reshape(N_EXPERTS, 1), (N_EXPERTS, LANES))
    consts = [norm2_g.reshape(1, D), w_out.astype(BF16), router_w.T, rb, swg.astype(BF16),
              swu.astype(BF16), swd.astype(BF16), ustrict]
    small = lambda dt: jax.ShapeDtypeStruct((8, N), dt)
    return pl.pallas_call(
        functools.partial(_outproj_kernel, tm=tm),
        grid=(N // tm,),
        in_specs=[pl.BlockSpec((tm, D), lambda i: (i, 0)),
                  pl.BlockSpec((tm, HALF), lambda i: (i, 0)),
                  pl.BlockSpec((tm, HALF), lambda i: (i, 0)),
                  vec(2), vec(3), vec(4), vec(5)] + [full(a) for a in consts],
        out_specs=[pl.BlockSpec((tm, D), lambda i: (i, 0)),
                   pl.BlockSpec((tm, D // 2), lambda i: (i, 0)),
                   pl.BlockSpec((8, tm), lambda i: (0, i)),
                   pl.BlockSpec((8, tm), lambda i: (0, i)),
                   pl.BlockSpec((8, tm), lambda i: (0, i))],
        out_shape=[jax.ShapeDtypeStruct((N, D), F32), jax.ShapeDtypeStruct((N, D // 2), jnp.uint32),
                   small(I32), small(F32), small(I32)],
        compiler_params=_cparams(("arbitrary",)),
        name="outproj",
    )(x2d, yf, yr, mod4, mod4, mod4, mod4, *consts)


def _for_each(count, fn, group):
    full = count // group

    def trip(t, c):
        for u in range(group):
            fn(t * group + u)
        return c

    def single(g, c):
        fn(g)
        return c

    lax.fori_loop(0, full, trip, 0)
    lax.fori_loop(full * group, count, single, 0)


def _wait_granules(count, wait_rows):
    full = count // WAIT_CHUNK

    def chunk(_, c):
        wait_rows(WAIT_CHUNK * GRANULE)
        return c

    def single(_, c):
        wait_rows(GRANULE)
        return c

    lax.fori_loop(0, full, chunk, 0)
    lax.fori_loop(full * WAIT_CHUNK, count, single, 0)


def _moe_plan(idx_t, rank_t):
    n = idx_t.shape[1]
    n_tiles = n // MOE_TILE
    experts = jnp.arange(N_EXPERTS, dtype=I32)
    hot = idx_t[:TOP_K, :, None] == experts
    cnt = jnp.sum(hot.reshape(TOP_K, n_tiles, MOE_TILE, N_EXPERTS).astype(I32), axis=(0, 2))
    gran = (cnt + GRANULE - 1) // GRANULE
    loc_end = jnp.cumsum(gran, axis=1)
    loc_off = loc_end - gran
    g_tile = loc_end[:, N_EXPERTS - 1]
    padded = (GRANULE * jnp.sum(gran, axis=0) + EXPERT_BLOCK - 1) // EXPERT_BLOCK * EXPERT_BLOCK
    pad_end = jnp.cumsum(padded)
    glob_off = (pad_end - padded)[None, :] + GRANULE * (jnp.cumsum(gran, axis=0) - gran)
    loc_tok = jnp.repeat(GRANULE * loc_off, MOE_TILE, axis=0)
    pos = rank_t[:TOP_K] + jnp.sum(jnp.where(hot, loc_tok[None], 0), axis=-1)
    pos = jnp.concatenate([pos, jnp.full((8 - TOP_K, n), -1, I32)], axis=0)
    g = jnp.arange(MOE_TILE, dtype=I32)
    e_of_g = jnp.minimum(jnp.sum((loc_end[:, None, :] <= g[None, :, None]).astype(I32), axis=-1), N_EXPERTS - 1)
    dst = jnp.sum(jnp.where(e_of_g[:, :, None] == experts,
                            glob_off[:, None, :] + GRANULE * (g[None, :, None] - loc_off[:, None, :]), 0), axis=-1)
    gtab = jnp.where(g[None, :] < g_tile[:, None], dst, 0).reshape(n_tiles, 1, MOE_TILE)
    n_blocks = -(-(n * TOP_K + GRANULE * N_EXPERTS * n_tiles) // EXPERT_BLOCK) + N_EXPERTS
    blk_start = jnp.arange(n_blocks, dtype=I32) * EXPERT_BLOCK
    blk_e = jnp.minimum(jnp.sum((pad_end[None, :] <= blk_start[:, None]).astype(I32), axis=1), N_EXPERTS - 1)
    plan = dict(pos=pos.astype(I32), gtab=gtab.astype(I32), g_tile=g_tile.astype(I32),
                pad_end=pad_end.astype(I32), padded=padded.astype(I32), blk_e=blk_e.astype(I32),
                n_used=(pad_end[N_EXPERTS - 1:] // EXPERT_BLOCK).astype(I32))
    return plan, n_blocks * EXPERT_BLOCK


def _dispatch_kernel(pend_ref, padded_ref, nu_ref, gt_ref, gtab_ref, pos_ref, hp_ref, xs_ref, zeros, sbuf,
                     sem, zsem, *, n_blocks, n_tiles):
    @pl.when(pl.program_id(0) == 0)
    def _():
        zeros[...] = jnp.zeros_like(zeros)

        def block_copy(start):
            return pltpu.make_async_copy(zeros, xs_ref.at[pl.ds(pl.multiple_of(start, EXPERT_BLOCK),
                                                                 EXPERT_BLOCK), :], zsem)

        def pad_loop(fn):
            def body(e, _):
                @pl.when(padded_ref[e] > 0)
                def _():
                    fn(block_copy(pend_ref[e] - EXPERT_BLOCK))
                return 0
            lax.fori_loop(0, N_EXPERTS, body, 0)

        def tail_loop(fn):
            def body(b, _):
                fn(block_copy(b * EXPERT_BLOCK))
                return 0
            lax.fori_loop(nu_ref[0], n_blocks, body, 0)

        pad_loop(lambda cp: cp.start())
        tail_loop(lambda cp: cp.start())
        pad_loop(lambda cp: cp.wait())
        tail_loop(lambda cp: cp.wait())

    i = pl.program_id(0)
    slot = i % 2

    def wait_granules(which, count):
        def wait_rows(rows):
            pltpu.make_async_copy(sbuf.at[which, pl.ds(0, rows), :], xs_ref.at[pl.ds(0, rows), :],
                                  sem.at[which]).wait()
        _wait_granules(count, wait_rows)

    @pl.when(i >= 2)
    def _():
        wait_granules(slot, gt_ref[jnp.maximum(i - 2, 0)])

    pos = pos_ref[...]
    r_iota = lax.broadcasted_iota(I32, (MOE_SORTED_ROWS, MOE_TILE), 0)
    hit = r_iota == pos[0:1, :]
    for k in range(1, TOP_K):
        hit = jnp.logical_or(hit, r_iota == pos[k:k + 1, :])
    perm = jnp.where(hit, 1.0, 0.0).astype(BF16)
    lo, hi = _unpack_halves(hp_ref[...])
    sbuf[slot] = _pack_halves(jnp.concatenate([_mm(perm, lo), _mm(perm, hi)], axis=1))

    def issue(g):
        pltpu.make_async_copy(sbuf.at[slot, pl.ds(pl.multiple_of(g * GRANULE, GRANULE), GRANULE), :],
                              xs_ref.at[pl.ds(pl.multiple_of(gtab_ref[0, 0, g], GRANULE), GRANULE), :],
                              sem.at[slot]).start()
    _for_each(gt_ref[i], issue, DMA_UNROLL)

    @pl.when(i == n_tiles - 1)
    def _():
        wait_granules(slot, gt_ref[i])
        if n_tiles > 1:
            wait_granules(1 - slot, gt_ref[jnp.maximum(i - 1, 0)])


def _dispatch(hp, plan, n_rows):
    N, W = hp.shape
    n_tiles = N // MOE_TILE
    n_blocks = n_rows // EXPERT_BLOCK
    return pl.pallas_call(
        functools.partial(_dispatch_kernel, n_blocks=n_blocks, n_tiles=n_tiles),
        grid_spec=pltpu.PrefetchScalarGridSpec(
            num_scalar_prefetch=4, grid=(n_tiles,),
            in_specs=[pl.BlockSpec((1, 1, MOE_TILE), lambda i, *_: (i, 0, 0), memory_space=pltpu.SMEM),
                      pl.BlockSpec((8, MOE_TILE), lambda i, *_: (0, i)),
                      pl.BlockSpec((MOE_TILE, W), lambda i, *_: (i, 0))],
            out_specs=pl.BlockSpec(memory_space=pl.ANY),
            scratch_shapes=[pltpu.VMEM((EXPERT_BLOCK, W), jnp.uint32),
                            pltpu.VMEM((2, MOE_SORTED_ROWS, W), jnp.uint32),
                            pltpu.SemaphoreType.DMA((2,)), pltpu.SemaphoreType.DMA]),
        out_shape=jax.ShapeDtypeStruct((n_rows, W), jnp.uint32),
        compiler_params=_cparams(("arbitrary",)),
        name="dispatch",
    )(plan["pad_end"], plan["padded"], plan["n_used"], plan["g_tile"], plan["gtab"], plan["pos"], hp)


def _experts_kernel(be_ref, nu_ref, xs_ref, wgf_ref, wuf_ref, wdf_ref, ys_ref, wg_ref, wu_ref, wd_ref):
    i = pl.program_id(0)
    live = i < nu_ref[0]
    new_expert = jnp.logical_or(i == 0, be_ref[i] != be_ref[jnp.maximum(i - 1, 0)])

    @pl.when(jnp.logical_not(live))
    def _():
        ys_ref[...] = jnp.zeros_like(ys_ref)

    @pl.when(jnp.logical_and(live, new_expert))
    def _():
        wg_ref[...] = wgf_ref[...].astype(BF16)
        wu_ref[...] = wuf_ref[...].astype(BF16)
        wd_ref[...] = wdf_ref[...].astype(BF16)

    @pl.when(live)
    def _():
        half = D_MODEL // 2
        lo, hi = _unpack_halves(xs_ref[...])
        lo = lo.astype(BF16)
        hi = hi.astype(BF16)
        gate = (jnp.dot(lo, wg_ref[0:half, :], preferred_element_type=F32)
                + jnp.dot(hi, wg_ref[half:D_MODEL, :], preferred_element_type=F32))
        up = (jnp.dot(lo, wu_ref[0:half, :], preferred_element_type=F32)
              + jnp.dot(hi, wu_ref[half:D_MODEL, :], preferred_element_type=F32))
        y = jnp.dot((_silu(gate) * up).astype(BF16), wd_ref[...], preferred_element_type=F32)
        ys_ref[...] = _pack_halves(y)


def _experts(xs, blk_e, n_used, wg, wu, wd):
    n_rows, W = xs.shape
    n_blocks = n_rows // EXPERT_BLOCK
    row_map = lambda i, be, nu: (jnp.minimum(i, nu[0] - 1), 0)
    return pl.pallas_call(
        _experts_kernel,
        grid_spec=pltpu.PrefetchScalarGridSpec(
            num_scalar_prefetch=2, grid=(n_blocks,),
            in_specs=[pl.BlockSpec((EXPERT_BLOCK, W), row_map),
                      pl.BlockSpec((None, D_MODEL, D_EXPERT), lambda i, be, nu: (be[i], 0, 0)),
                      pl.BlockSpec((None, D_MODEL, D_EXPERT), lambda i, be, nu: (be[i], 0, 0)),
                      pl.BlockSpec((None, D_EXPERT, D_MODEL), lambda i, be, nu: (be[i], 0, 0))],
            out_specs=pl.BlockSpec((EXPERT_BLOCK, W), lambda i, be, nu: (i, 0)),
            scratch_shapes=[pltpu.VMEM((D_MODEL, D_EXPERT), BF16), pltpu.VMEM((D_MODEL, D_EXPERT), BF16),
                            pltpu.VMEM((D_EXPERT, D_MODEL), BF16)]),
        out_shape=jax.ShapeDtypeStruct((n_rows, W), jnp.uint32),
        compiler_params=_cparams(("arbitrary",)),
        name="experts",
    )(blk_e, n_used, xs, wg, wu, wd)


def _combine_kernel(gt_ref, gtab_ref, gtab_next_ref, ys_ref, pos_ref, xs_ref, wts_ref, g2_ref, fg_ref, o_ref,
                    buf, sem, *, n_tiles):
    i = pl.program_id(0)
    slot = i % 2

    def fetch(which, table_ref, count):
        def body(g):
            pltpu.make_async_copy(
                ys_ref.at[pl.ds(pl.multiple_of(table_ref[0, 0, g], GRANULE), GRANULE), :],
                buf.at[which, pl.ds(pl.multiple_of(g * GRANULE, GRANULE), GRANULE), :], sem.at[which]).start()
        _for_each(count, body, DMA_UNROLL)

    @pl.when(i == 0)
    def _():
        buf[...] = jnp.zeros_like(buf)
        fetch(0, gtab_ref, gt_ref[0])

    @pl.when(i + 1 < n_tiles)
    def _():
        fetch(1 - slot, gtab_next_ref, gt_ref[jnp.minimum(i + 1, n_tiles - 1)])

    def wait_rows(rows):
        pltpu.make_async_copy(ys_ref.at[pl.ds(0, rows), :], buf.at[slot, pl.ds(0, rows), :],
                              sem.at[slot]).wait()
    _wait_granules(gt_ref[i], wait_rows)

    pos = pos_ref[...]
    w = wts_ref[...]
    c_iota = lax.broadcasted_iota(I32, (MOE_TILE, MOE_SORTED_ROWS), 1)
    wmat = jnp.where(c_iota == pos[:, 0:1], w[:, 0:1], 0.0)
    for k in range(1, TOP_K):
        wmat = wmat + jnp.where(c_iota == pos[:, k:k + 1], w[:, k:k + 1], 0.0)
    lo, hi = _unpack_halves(buf[slot])
    lo = lo.astype(BF16)
    hi = hi.astype(BF16)
    routed = jnp.concatenate([_mm_split_r(wmat, lo), _mm_split_r(wmat, hi)], axis=1)
    x3 = xs_ref[...] + g2_ref[...] * routed
    y = x3 * lax.rsqrt(jnp.mean(x3 * x3, axis=-1, keepdims=True) + RMS_EPS) * fg_ref[...]
    o_ref[...] = y


def _combine(ys, plan, xsr, wts_t, mod4, final_g, T):
    N, D = xsr.shape
    W = ys.shape[1]
    tm = MOE_TILE
    tps = T // tm
    n_tiles = N // tm
    tile_tab = lambda off: pl.BlockSpec((1, 1, MOE_TILE), lambda i, gt: (jnp.minimum(i + off, n_tiles - 1), 0, 0),
                                        memory_space=pltpu.SMEM)
    return pl.pallas_call(
        functools.partial(_combine_kernel, n_tiles=n_tiles),
        grid_spec=pltpu.PrefetchScalarGridSpec(
            num_scalar_prefetch=1, grid=(n_tiles,),
            in_specs=[tile_tab(0), tile_tab(1),
                      pl.BlockSpec(memory_space=pl.ANY),
                      pl.BlockSpec((tm, 8), lambda i, gt: (i, 0)),
                      pl.BlockSpec((tm, D), lambda i, gt: (i, 0)),
                      pl.BlockSpec((tm, 8), lambda i, gt: (i, 0)),
                      pl.BlockSpec((None, None, 1, D), lambda i, gt: (i // tps, 5, 0, 0)),
                      pl.BlockSpec((1, D), lambda i, gt: (0, 0))],
            out_specs=pl.BlockSpec((tm, D), lambda i, gt: (i, 0)),
            scratch_shapes=[pltpu.VMEM((2, MOE_SORTED_ROWS, W), jnp.uint32), pltpu.SemaphoreType.DMA((2,))]),
        out_shape=jax.ShapeDtypeStruct((N, D), F32),
        compiler_params=_cparams(("arbitrary",)),
        name="combine",
    )(plan["g_tile"], plan["gtab"], plan["gtab"], ys, plan["pos"].T, xsr, wts_t, mod4, final_g.reshape(1, D))


def _pack_w_in(w_in):
    D = w_in.shape[0]
    fox_cols = 4 * HALF + 3 * N_HEADS
    wf = w_in[:, :fox_cols]
    wr = w_in[:, fox_cols:]
    o = 3 * HALF
    pad = lambda a, n: jnp.concatenate([a, jnp.zeros((D, n - a.shape[1]), a.dtype)], axis=1)
    parts = [wf[:, :4 * HALF], wr[:, :o],
             pad(wf[:, 4 * HALF:], LANES),
             pad(wr[:, o:o + DECAY_LORA], LANES),
             pad(wr[:, o + DECAY_LORA:o + DECAY_LORA + ICLR_LORA], LANES),
             pad(wr[:, o + DECAY_LORA + ICLR_LORA:], 2 * LANES)]
    return jnp.concatenate(parts, axis=1).astype(BF16)


def _pack_mu(mu):
    o = 3 * HALF
    pad = lambda a, n: jnp.concatenate([a, jnp.zeros((n - a.shape[0],), a.dtype)])
    small = jnp.concatenate([jnp.zeros((LANES,), mu.dtype),
                             pad(mu[o:o + DECAY_LORA], LANES),
                             pad(mu[o + DECAY_LORA:o + DECAY_LORA + ICLR_LORA], LANES),
                             pad(mu[o + DECAY_LORA + ICLR_LORA:], 2 * LANES)])
    return mu[:o].reshape(1, o), small.reshape(1, Z_SMALL)


def kernel(x, c, norm1_g, norm2_g, ada_w, ada_b, w_in, w_out, fox_qn_g, fox_kn_g, fox_on_g, fox_forget_b,
           rw_mu, rw_w0, rw_decay_up, rw_a0, rw_iclr_up, rw_gate_up, rw_k_k, rw_k_a, rw_r_k, rw_lnx_g,
           rw_lnx_b, router_w, router_bias, exp_w_gate, exp_w_up, exp_w_down, sh_w_gate, sh_w_up,
           sh_w_down, final_g):
    B, T, D = x.shape
    N = B * T
    depth = norm1_g.shape[0]
    assert depth == 1, "the combine kernel fuses the final RMSNorm, so exactly one layer is supported"
    xf = x.reshape(N, D)
    for l in range(depth):
        mod4 = _ada(c, ada_w[l], ada_b[l]).reshape(B, 6, 1, D)
        mu_big, mu_small = _pack_mu(rw_mu[l])
        zm, zs = _inproj(xf, mod4, norm1_g[l], _pack_w_in(w_in[l]), mu_big, mu_small, T)
        qp, kp, vp = _foxprep(zm, zs, fox_forget_b[l], fox_qn_g[l], fox_kn_g[l], T)
        y_fox = _attention(qp, kp, vp, zm, fox_on_g[l], B, T)
        rw = dict(rw_w0=rw_w0[l], rw_decay_up=rw_decay_up[l], rw_a0=rw_a0[l], rw_iclr_up=rw_iclr_up[l],
                  rw_gate_up=rw_gate_up[l], rw_k_k=rw_k_k[l], rw_k_a=rw_k_a[l], rw_r_k=rw_r_k[l],
                  rw_lnx_g=rw_lnx_g[l], rw_lnx_b=rw_lnx_b[l])
        y_rwkv = _rwkv(zm, zs, rw, B, T)
        xsr, hp, idx_t, wts_t, rank_t = _outproj(
            xf, y_fox, y_rwkv, mod4, norm2_g[l], w_out[l], router_w[l], router_bias[l],
            sh_w_gate[l], sh_w_up[l], sh_w_down[l], T)
        plan, n_rows = _moe_plan(idx_t, rank_t)
        xs = _dispatch(hp, plan, n_rows)
        ys = _experts(xs, plan["blk_e"], plan["n_used"], exp_w_gate[l], exp_w_up[l], exp_w_down[l])
        xf = _combine(ys, plan, xsr, wts_t.T, mod4, final_g, T)
    return xf.reshape(B, T, D)
```

```python
import functools

import jax
import jax.numpy as jnp
import numpy as np
from jax import lax
from jax.experimental import pallas as pl
from jax.experimental.pallas import tpu as pltpu

F32 = jnp.float32
BF16 = jnp.bfloat16
I32 = jnp.int32
HIGHEST = lax.Precision.HIGHEST

D_MODEL = 1024
HEAD_DIM = 64
N_HEADS = 8
HALF = N_HEADS * HEAD_DIM
RMS_EPS = 1e-6
LNX_EPS = 64e-5
LOG2E = 1.4426950408889634
DECAY_LORA = 64
ICLR_LORA = 64
GATE_LORA = 160
N_EXPERTS = 64
N_GROUPS = 8
GROUP_SIZE = N_EXPERTS // N_GROUPS
TOPK_GROUPS = 4
TOP_K = 6
D_EXPERT = 256
D_SHARED = 256
ROUTED_SCALE = 2.5
EXPERT_BLOCK = 512

LANES = 128
Z_MAIN = 4 * HALF + 3 * HALF
Z_SMALL = 5 * LANES
VMEM_LIMIT = 56 * 1024 * 1024
ATTN_HEADS = 4
DMA_UNROLL = 4
WAIT_CHUNK = 16
MOE_TILE = 256
GRANULE = 8
MOE_SORTED_ROWS = MOE_TILE * TOP_K + N_EXPERTS * GRANULE
RWKV_CHUNK = 64
RWKV_SEQS_PER_STEP = 2


def _cparams(semantics):
    return pltpu.CompilerParams(dimension_semantics=semantics, vmem_limit_bytes=VMEM_LIMIT)


def _mm(a, b):
    return jnp.dot(a.astype(BF16), b.astype(BF16), preferred_element_type=F32)


def _mm_nt(a, b):
    return lax.dot_general(a.astype(BF16), b.astype(BF16), (((1,), (1,)), ((), ())),
                           preferred_element_type=F32)


def _mm_tn(a, b):
    return lax.dot_general(a.astype(BF16), b.astype(BF16), (((0,), (0,)), ((), ())),
                           preferred_element_type=F32)


def _mm_f32(a, b):
    return jnp.dot(a, b, precision=HIGHEST, preferred_element_type=F32)


def _bf16_pieces(x, passes):
    pieces = []
    for _ in range(passes):
        piece = x.astype(BF16)
        pieces.append(piece)
        x = x - piece.astype(F32)
    return pieces


def _mm_split(m01, x, passes=3):
    return sum(jnp.dot(m01, p, preferred_element_type=F32) for p in _bf16_pieces(x, passes))


def _mm_split_r(x, m01, passes=2):
    return sum(jnp.dot(p, m01, preferred_element_type=F32) for p in _bf16_pieces(x, passes))


def _sigmoid(x):
    return 1.0 / (1.0 + jnp.exp(-x))


def _softplus(x):
    return jnp.maximum(x, 0.0) + jnp.log(1.0 + jnp.exp(-jnp.abs(x)))


def _silu(x):
    return x * _sigmoid(x)


def _pack_halves(x):
    w = x.shape[1] // 2
    bits = lambda t: lax.bitcast_convert_type(t.astype(jnp.bfloat16).astype(F32), jnp.uint32)
    return (bits(x[:, 0:w]) >> 16) | (bits(x[:, w:2 * w]) & jnp.uint32(0xFFFF0000))


def _unpack_halves(p):
    lo = lax.bitcast_convert_type(p << 16, F32)
    hi = lax.bitcast_convert_type(p & jnp.uint32(0xFFFF0000), F32)
    return lo, hi


def _shift_rows(z, carry_ref, first):
    rows = z.shape[0]
    prev_row = jnp.where(first, 0.0, carry_ref[0:1, :])
    prev = pltpu.roll(z, 1, 0)
    row0 = lax.broadcasted_iota(I32, (rows, 1), 0) == 0
    prev = jnp.where(row0, prev_row, prev)
    carry_ref[0:1, :] = z[rows - 1:rows, :]
    return prev


def _ada_kernel(c_ref, w_ref, b_ref, o_ref):
    o_ref[...] = _mm_f32(_silu(c_ref[...]), w_ref[...]) + b_ref[...]


def _ada(c, ada_w, ada_b):
    B, D = c.shape
    n_out = ada_w.shape[1]
    tn = 512
    return pl.pallas_call(
        _ada_kernel,
        grid=(n_out // tn,),
        in_specs=[pl.BlockSpec((B, D), lambda j: (0, 0)),
                  pl.BlockSpec((D, tn), lambda j: (0, j)),
                  pl.BlockSpec((1, tn), lambda j: (0, j))],
        out_specs=pl.BlockSpec((B, tn), lambda j: (0, j)),
        out_shape=jax.ShapeDtypeStruct((B, n_out), F32),
        compiler_params=_cparams(("arbitrary",)),
        name="ada",
    )(c, ada_w, ada_b.reshape(1, n_out))


def _inproj_kernel(x_ref, g_ref, sh_ref, sc_ref, w_ref, mub_ref, mus_ref, zm_ref, zs_ref,
                   carry_b, carry_s, *, tiles_per_seq):
    first = (pl.program_id(0) % tiles_per_seq) == 0
    x = x_ref[...]
    h = x * lax.rsqrt(jnp.mean(x * x, axis=-1, keepdims=True) + RMS_EPS) * g_ref[...]
    hb = (h * (1.0 + sc_ref[...]) + sh_ref[...]).astype(BF16)
    nf = 4 * HALF
    zm_ref[:, 0:nf] = jnp.dot(hb, w_ref[:, 0:nf], preferred_element_type=F32).astype(BF16)
    zr = jnp.dot(hb, w_ref[:, nf:Z_MAIN], preferred_element_type=F32)
    zr = zr + mub_ref[...] * (_shift_rows(zr, carry_b, first) - zr)
    zm_ref[:, nf:Z_MAIN] = zr.astype(BF16)
    zs = jnp.dot(hb, w_ref[:, Z_MAIN:Z_MAIN + Z_SMALL], preferred_element_type=F32)
    zs_ref[...] = zs + mus_ref[...] * (_shift_rows(zs, carry_s, first) - zs)


def _inproj(x2d, mod4, norm_g, w_all, mu_big, mu_small, T):
    N, D = x2d.shape
    tm = min(512, T)
    tps = T // tm
    vec = lambda j: pl.BlockSpec((None, None, 1, D), lambda i: (i // tps, j, 0, 0))
    return pl.pallas_call(
        functools.partial(_inproj_kernel, tiles_per_seq=tps),
        grid=(N // tm,),
        in_specs=[pl.BlockSpec((tm, D), lambda i: (i, 0)),
                  pl.BlockSpec((1, D), lambda i: (0, 0)),
                  vec(0), vec(1),
                  pl.BlockSpec((D, Z_MAIN + Z_SMALL), lambda i: (0, 0)),
                  pl.BlockSpec((1, 3 * HALF), lambda i: (0, 0)),
                  pl.BlockSpec((1, Z_SMALL), lambda i: (0, 0))],
        out_specs=[pl.BlockSpec((tm, Z_MAIN), lambda i: (i, 0)),
                   pl.BlockSpec((tm, Z_SMALL), lambda i: (i, 0))],
        out_shape=[jax.ShapeDtypeStruct((N, Z_MAIN), BF16),
                   jax.ShapeDtypeStruct((N, Z_SMALL), F32)],
        scratch_shapes=[pltpu.VMEM((8, 3 * HALF), F32), pltpu.VMEM((8, Z_SMALL), F32)],
        compiler_params=_cparams(("arbitrary",)),
        name="inproj",
    )(x2d, norm_g.reshape(1, D), mod4, mod4, w_all, mu_big, mu_small)


def _foxprep_kernel(q_ref, k_ref, v_ref, zs_ref, fb_ref, qg_ref, kg_ref, tri_ref, eexp_ref,
                    esum_ref, e8_ref, plq_ref, plk_ref, plv_ref, cq_ref, ck_ref, cv_ref,
                    qp_ref, kp_ref, vp_ref, carry_k, carry_v, carry_c, *, tiles_per_seq):
    first = (pl.program_id(0) % tiles_per_seq) == 0
    zs = zs_ref[...]
    logf = -_softplus(-(zs + fb_ref[...]))
    cum = _mm_split(tri_ref[...], logf) + jnp.where(first, 0.0, carry_c[0:1, :])
    carry_c[0:1, :] = cum[cum.shape[0] - 1:, :]
    cum2 = cum * LOG2E
    c_hi = cum2.astype(BF16)
    r1 = cum2 - c_hi.astype(F32)
    c_mid = r1.astype(BF16)
    c_lo = (r1 - c_mid.astype(F32)).astype(BF16)
    a_full = _mm_split_r(_sigmoid(zs), eexp_ref[...])
    k = k_ref[...].astype(F32)
    v = v_ref[...].astype(F32)
    a_k = a_full[:, 0:HALF]
    a_v = a_full[:, HALF:2 * HALF]
    k = a_k * _shift_rows(k, carry_k, first) + (1.0 - a_k) * k
    v = a_v * _shift_rows(v, carry_v, first) + (1.0 - a_v) * v
    q = q_ref[...].astype(F32)

    def head_rms(t, gain):
        ms = _mm_split_r(t * t, esum_ref[...]) * (1.0 / HEAD_DIM)
        inv = _mm_split_r(lax.rsqrt(ms + RMS_EPS), e8_ref[...])
        return t * inv * gain

    qn = head_rms(q, qg_ref[...]) * (HEAD_DIM ** -0.5 * LOG2E)
    kn = head_rms(k, kg_ref[...])
    aug = [c_hi, c_mid, c_lo]
    lhs_q = jnp.concatenate([qn.astype(BF16)] + aug, axis=1)
    lhs_k = jnp.concatenate([kn.astype(BF16)] + aug, axis=1)
    qp_ref[...] = (jnp.dot(lhs_q, plq_ref[...], preferred_element_type=F32) + cq_ref[...]).astype(BF16)
    kp_ref[...] = (jnp.dot(lhs_k, plk_ref[...], preferred_element_type=F32) + ck_ref[...]).astype(BF16)
    vp_ref[...] = (jnp.dot(v.astype(BF16), plv_ref[...], preferred_element_type=F32)
                   + cv_ref[...]).astype(BF16)


def _fox_constants(tm):
    hp = N_HEADS * LANES
    eexp = np.zeros((LANES, 2 * HALF), np.float32)
    esum = np.zeros((HALF, LANES), np.float32)
    e8 = np.zeros((LANES, HALF), np.float32)
    plq = np.zeros((HALF + 3 * LANES, hp), np.float32)
    plk = np.zeros((HALF + 3 * LANES, hp), np.float32)
    plv = np.zeros((HALF, hp), np.float32)
    cq = np.zeros((1, hp), np.float32)
    ck = np.zeros((1, hp), np.float32)
    cv = np.zeros((1, hp), np.float32)
    for h in range(N_HEADS):
        sl = slice(h * HEAD_DIM, (h + 1) * HEAD_DIM)
        eexp[8 + h, sl] = 1.0
        eexp[16 + h, HALF + h * HEAD_DIM:HALF + (h + 1) * HEAD_DIM] = 1.0
        esum[sl, h] = 1.0
        e8[h, sl] = 1.0
        base = h * LANES
        for d in range(HEAD_DIM):
            plq[h * HEAD_DIM + d, base + d] = 1.0
            plk[h * HEAD_DIM + d, base + d] = 1.0
            plv[h * HEAD_DIM + d, base + d] = 1.0
        for j in range(3):
            plq[HALF + j * LANES + h, base + HEAD_DIM + j] = 1.0
            plk[HALF + j * LANES + h, base + HEAD_DIM + 3 + j] = -1.0
            cq[0, base + HEAD_DIM + 3 + j] = 1.0
            ck[0, base + HEAD_DIM + j] = 1.0
        cv[0, base + HEAD_DIM] = 1.0
    tri = np.tril(np.ones((tm, tm), np.float32))
    bf = lambda a: jnp.asarray(a, BF16)
    return dict(tri=bf(tri), eexp=bf(eexp), esum=bf(esum), e8=bf(e8),
                plq=bf(plq), plk=bf(plk), plv=bf(plv), cq=jnp.asarray(cq), ck=jnp.asarray(ck),
                cv=jnp.asarray(cv))


def _foxprep(zm, zs, forget_b, qn_g, kn_g, T):
    N = zm.shape[0]
    tm = min(256, T)
    tps = T // tm
    cst = _fox_constants(tm)
    hp = N_HEADS * LANES
    fb = jnp.zeros((1, LANES), F32).at[0, :N_HEADS].set(forget_b)
    full = lambda a: pl.BlockSpec(a.shape, lambda i: (0, 0))
    consts = [fb, qn_g.reshape(1, HALF), kn_g.reshape(1, HALF), cst["tri"], cst["eexp"], cst["esum"],
              cst["e8"], cst["plq"], cst["plk"], cst["plv"], cst["cq"], cst["ck"], cst["cv"]]
    return pl.pallas_call(
        functools.partial(_foxprep_kernel, tiles_per_seq=tps),
        grid=(N // tm,),
        in_specs=[pl.BlockSpec((tm, HALF), lambda i: (i, 0)),
                  pl.BlockSpec((tm, HALF), lambda i: (i, 1)),
                  pl.BlockSpec((tm, HALF), lambda i: (i, 2)),
                  pl.BlockSpec((tm, LANES), lambda i: (i, 0))] + [full(a) for a in consts],
        out_specs=[pl.BlockSpec((tm, hp), lambda i: (i, 0))] * 3,
        out_shape=[jax.ShapeDtypeStruct((N, hp), BF16)] * 3,
        scratch_shapes=[pltpu.VMEM((8, HALF), F32), pltpu.VMEM((8, HALF), F32),
                        pltpu.VMEM((8, LANES), F32)],
        compiler_params=_cparams(("arbitrary",)),
        name="foxprep",
    )(zm, zm, zm, zs, *consts)


def _attn_kernel(q_ref, k_ref, v_ref, g_ref, ong_ref, o_ref, *, tq):
    i = pl.program_id(2)
    row = lax.broadcasted_iota(I32, (tq, tq), 0)
    col = lax.broadcasted_iota(I32, (tq, tq), 1)
    causal = col <= row
    heads = range(ATTN_HEADS)
    lanes = [slice(hh * LANES, (hh + 1) * LANES) for hh in heads]
    qs = [q_ref[:, lanes[hh]] for hh in heads]

    def step(j, carry, masked):
        m, acc = carry
        start = pl.multiple_of(j * tq, tq)
        s = [lax.dot_general(qs[hh], k_ref[pl.ds(start, tq), lanes[hh]], (((1,), (1,)), ((), ())),
                             preferred_element_type=F32) for hh in heads]
        m_out, acc_out = [], []
        for hh in heads:
            sh = jnp.where(causal, s[hh], -jnp.inf) if masked else s[hh]
            m_new = jnp.maximum(m[hh], jnp.max(sh, axis=1, keepdims=True))
            p = jnp.exp2(sh - m_new)
            v = v_ref[pl.ds(start, tq), lanes[hh]]
            acc_out.append(jnp.exp2(m[hh] - m_new) * acc[hh]
                           + jnp.dot(p.astype(BF16), v, preferred_element_type=F32))
            m_out.append(m_new)
        return tuple(m_out), tuple(acc_out)

    init = (tuple(jnp.full((tq, 1), -jnp.inf, F32) for _ in heads),
            tuple(jnp.zeros((tq, LANES), F32) for _ in heads))
    carry = lax.fori_loop(0, i, functools.partial(step, masked=False), init)
    _, acc = step(i, carry, True)
    lane = lax.broadcasted_iota(I32, (tq, LANES), 1)
    outs = []
    for hh in heads:
        o = acc[hh] / acc[hh][:, HEAD_DIM:HEAD_DIM + 1]
        ms = jnp.sum(jnp.where(lane < HEAD_DIM, o * o, 0.0), axis=1, keepdims=True) * (1.0 / HEAD_DIM)
        outs.append(o * lax.rsqrt(ms + RMS_EPS))
    o = jnp.concatenate([jnp.where(lane < HEAD_DIM, outs[2 * p], pltpu.roll(outs[2 * p + 1], HEAD_DIM, 1))
                         for p in range(ATTN_HEADS // 2)], axis=1)
    y = o * ong_ref[...] * _sigmoid(g_ref[...].astype(F32))
    o_ref[...] = y.astype(BF16)


def _attention(qp, kp, vp, zm, on_g, B, T):
    N = qp.shape[0]
    tq = min(256, T)
    nq = T // tq
    groups = N_HEADS // ATTN_HEADS
    wp = ATTN_HEADS * LANES
    wo = ATTN_HEADS * HEAD_DIM
    g_col0 = 3 * HALF // wo
    return pl.pallas_call(
        functools.partial(_attn_kernel, tq=tq),
        grid=(B, groups, nq),
        in_specs=[pl.BlockSpec((tq, wp), lambda b, p, i: (b * nq + i, p)),
                  pl.BlockSpec((T, wp), lambda b, p, i: (b, p)),
                  pl.BlockSpec((T, wp), lambda b, p, i: (b, p)),
                  pl.BlockSpec((tq, wo), lambda b, p, i: (b * nq + i, g_col0 + p)),
                  pl.BlockSpec((None, 1, wo), lambda b, p, i: (p, 0, 0))],
        out_specs=pl.BlockSpec((tq, wo), lambda b, p, i: (b * nq + i, p)),
        out_shape=jax.ShapeDtypeStruct((N, HALF), BF16),
        compiler_params=_cparams(("arbitrary", "arbitrary", "arbitrary")),
        name="attn",
    )(qp, kp, vp, zm, on_g.reshape(groups, 1, wo))


def _rwkv_kernel(r_ref, k_ref, v_ref, zs_ref, w0_ref, dup_ref, a0_ref, iup_ref, gup_ref, kk_ref,
                 ka_ref, rk_ref, lng_ref, lnb_ref, tri_ref, o_ref, state, ybuf, *, chunk, nb):
    C = chunk
    c_idx = pl.program_id(1)

    @pl.when(c_idx == 0)
    def _():
        state[...] = jnp.zeros_like(state)

    R = nb * C
    r = r_ref[...].reshape(R, HALF).astype(F32)
    k = k_ref[...].reshape(R, HALF).astype(F32)
    v = v_ref[...].reshape(R, HALF).astype(F32)
    zs = zs_ref[...].reshape(R, Z_SMALL)
    wd = zs[:, LANES:2 * LANES]
    ad = zs[:, 2 * LANES:3 * LANES]
    gd = zs[:, 3 * LANES:5 * LANES]
    wl = w0_ref[...] + _mm(jnp.tanh(wd), dup_ref[...])
    lw = -jnp.exp(-_softplus(-wl) - 0.5)
    a = _sigmoid(a0_ref[...] + _mm(ad, iup_ref[...]))
    g = _mm(_sigmoid(gd), gup_ref[...])
    kk = k * kk_ref[...]
    k2 = k * (1.0 + (a - 1.0) * ka_ref[...])
    cl = _mm_split(tri_ref[...], lw)
    cl_end = jnp.concatenate(
        [jnp.broadcast_to(cl[(bi + 1) * C - 1:(bi + 1) * C, :], (C, HALF)) for bi in range(nb)], axis=0)
    e_neg = jnp.exp(-cl)
    e_tail = jnp.exp(cl_end - cl)
    pre = dict(rt=r * jnp.exp(cl), at=-kk * jnp.exp(cl - lw), kh=k2 * e_neg, bh=kk * a * e_neg,
               kb=k2 * e_tail, bb=kk * a * e_tail, v=v, kk=kk, pend=jnp.exp(cl_end),
               rkr=r * k2 * rk_ref[...], g=g)

    C2 = 2 * C
    row = lax.broadcasted_iota(I32, (C2, C2), 0)
    col = lax.broadcasted_iota(I32, (C2, C2), 1)
    lower = (col & (C - 1)) <= (row & (C - 1))
    strict = (col & (C - 1)) < (row & (C - 1))
    eye = row == col
    head0 = lax.broadcasted_iota(I32, (C, LANES), 1) < HEAD_DIM
    n_sq = int(np.log2(C)) - 1
    units = [(bi, p) for bi in range(nb) for p in range(N_HEADS // 2)]

    def part(name, bi, p):
        return pre[name][bi * C:(bi + 1) * C, p * LANES:(p + 1) * LANES]

    def stack(x):
        return jnp.concatenate([jnp.where(head0, x, 0.0), jnp.where(head0, 0.0, x)], axis=0)

    def unstack(x):
        return x[0:C] + x[C:C2]

    def head_sum(x):
        s0 = jnp.sum(jnp.where(head0, x, 0.0), axis=1, keepdims=True)
        s1 = jnp.sum(jnp.where(head0, 0.0, x), axis=1, keepdims=True)
        return jnp.where(head0, s0, s1)

    ops = []
    for bi, p in units:
        kk_p = part("kk", bi, p)
        inv = 1.0 / jnp.maximum(jnp.sqrt(head_sum(kk_p * kk_p)), 1e-12)
        ops.append(dict(rt=stack(part("rt", bi, p)), at=stack(part("at", bi, p) * inv),
                        kh=stack(part("kh", bi, p)).astype(BF16), bh=stack(part("bh", bi, p) * inv).astype(BF16),
                        kb=stack(part("kb", bi, p)).astype(BF16), bb=stack(part("bb", bi, p) * inv).astype(BF16),
                        v=stack(part("v", bi, p)).astype(BF16)))
    ra = [jnp.concatenate([o["rt"], o["at"]], axis=0).astype(BF16) for o in ops]
    g1 = [_mm_nt(x, o["kh"]) for x, o in zip(ra, ops)]
    g2 = [_mm_nt(x, o["bh"]) for x, o in zip(ra, ops)]
    a_rb = [jnp.where(lower, t[0:C2], 0.0).astype(BF16) for t in g2]
    pw = [jnp.where(strict, t[C2:2 * C2], 0.0) for t in g2]
    av = [_mm(jnp.concatenate([jnp.where(lower, t[0:C2], 0.0), jnp.where(strict, t[C2:2 * C2], 0.0)], axis=0),
              o["v"]) for t, o in zip(g1, ops)]
    xs = [jnp.concatenate([o["at"], t[C2:2 * C2]], axis=1) for o, t in zip(ops, av)]
    for level in range(n_sq + 1):
        pb = [p.astype(BF16) for p in pw]
        xs = [x + _mm(p, x) for p, x in zip(pb, xs)]
        if level < n_sq:
            pw = [jnp.dot(p, p, preferred_element_type=F32) for p in pb]
    xb = [x.astype(BF16) for x in xs]
    rb = [jnp.dot(p, x, preferred_element_type=F32) for p, x in zip(a_rb, xb)]
    bx = [_mm_tn(o["bb"], x) for o, x in zip(ops, xb)]
    kv = [_mm_tn(o["kb"], o["v"]) for o in ops]
    for u, (bi, p) in enumerate(units):
        o = ops[u]
        sl = slice(p * LANES, (p + 1) * LANES)
        r2 = unstack(o["rt"] + rb[u][:, 0:LANES])
        y0 = unstack(rb[u][:, LANES:2 * LANES] + av[u][0:C2])
        m_mat = jnp.where(eye, part("pend", bi, p)[0:1, :], 0.0) + bx[u][:, 0:LANES]
        g_mat = bx[u][:, LANES:2 * LANES] + kv[u]
        out = _mm(jnp.concatenate([r2, m_mat], axis=0), state[bi, p])
        state[bi, p] = out[C:C + LANES] + g_mat
        y = out[0:C] + y0
        cen = y - head_sum(y) * (1.0 / HEAD_DIM)
        var = head_sum(cen * cen) * (1.0 / HEAD_DIM)
        bonus = head_sum(part("rkr", bi, p)) * part("v", bi, p)
        y = cen * lax.rsqrt(var + LNX_EPS) * lng_ref[:, sl] + lnb_ref[:, sl] + bonus
        ybuf[bi, :, sl] = y * part("g", bi, p)
    o_ref[...] = ybuf[...].astype(BF16)


def _rwkv(zm, zs, p, B, T):
    N = zm.shape[0]
    C = RWKV_CHUNK
    assert 2 * C == LANES and T % C == 0, "a head pair's stacked chunk must fill one 128-row tile"
    nc = T // C
    nb = RWKV_SEQS_PER_STEP if B % RWKV_SEQS_PER_STEP == 0 else 1
    tri = np.kron(np.eye(nb, dtype=np.float32), np.tril(np.ones((C, C), np.float32)))
    pad_rows = lambda w, rows: jnp.zeros((rows, HALF), F32).at[:w.shape[0]].set(w)
    row = lambda a: a.reshape(1, HALF)
    consts = [row(p["rw_w0"]), pad_rows(p["rw_decay_up"], LANES), row(p["rw_a0"]),
              pad_rows(p["rw_iclr_up"], LANES), pad_rows(p["rw_gate_up"], 2 * LANES),
              row(p["rw_k_k"]), row(p["rw_k_a"]), row(p["rw_r_k"]), row(p["rw_lnx_g"]),
              row(p["rw_lnx_b"]), jnp.asarray(tri, BF16)]
    full = lambda a: pl.BlockSpec(a.shape, lambda b, c: (0, 0))
    rcol = 4 * HALF // HALF
    zm3 = zm.reshape(B, T, Z_MAIN)
    out = pl.pallas_call(
        functools.partial(_rwkv_kernel, chunk=C, nb=nb),
        grid=(B // nb, nc),
        in_specs=[pl.BlockSpec((nb, C, HALF), lambda b, c: (b, c, rcol)),
                  pl.BlockSpec((nb, C, HALF), lambda b, c: (b, c, rcol + 1)),
                  pl.BlockSpec((nb, C, HALF), lambda b, c: (b, c, rcol + 2)),
                  pl.BlockSpec((nb, C, Z_SMALL), lambda b, c: (b, c, 0))] + [full(a) for a in consts],
        out_specs=pl.BlockSpec((nb, C, HALF), lambda b, c: (b, c, 0)),
        out_shape=jax.ShapeDtypeStruct((B, T, HALF), BF16),
        scratch_shapes=[pltpu.VMEM((nb, N_HEADS // 2, LANES, LANES), F32), pltpu.VMEM((nb, C, HALF), F32)],
        compiler_params=_cparams(("arbitrary", "arbitrary")),
        name="rwkv",
    )(zm3, zm3, zm3, zs.reshape(B, T, Z_SMALL), *consts)
    return out.reshape(N, HALF)


def _first_index(mask, iota, big):
    return jnp.min(jnp.where(mask, iota, big), axis=0, keepdims=True)


def _outproj_kernel(x_ref, yf_ref, yr_ref, g1_ref, sh2_ref, sc2_ref, g2_ref, n2g_ref, wo_ref, rwt_ref,
                    rb_ref, swg_ref, swu_ref, swd_ref, ustrict_ref,
                    xs_ref, hp_ref, idx_ref, wts_ref, rank_ref, *, tm):
    d = (jnp.dot(yf_ref[...], wo_ref[0:HALF, :], preferred_element_type=F32)
         + jnp.dot(yr_ref[...], wo_ref[HALF:2 * HALF, :], preferred_element_type=F32))
    x2 = x_ref[...] + g1_ref[...] * d
    h = x2 * lax.rsqrt(jnp.mean(x2 * x2, axis=-1, keepdims=True) + RMS_EPS) * n2g_ref[...]
    h = h * (1.0 + sc2_ref[...]) + sh2_ref[...]
    hb = h.astype(BF16)
    act = _silu(jnp.dot(hb, swg_ref[...], preferred_element_type=F32)) * jnp.dot(
        hb, swu_ref[...], preferred_element_type=F32)
    shared = jnp.dot(act.astype(BF16), swd_ref[...], preferred_element_type=F32)
    xs_ref[...] = x2 + g2_ref[...] * shared
    half = D_MODEL // 2
    hp_ref[...] = _pack_halves(h)

    logits = lax.dot_general(rwt_ref[...], h, (((1,), (1,)), ((), ())), precision=HIGHEST,
                             preferred_element_type=F32)
    scores = _sigmoid(logits)
    sel = scores + rb_ref[:, 0:1]
    neg = -jnp.inf
    sel3 = sel.reshape(N_GROUPS, GROUP_SIZE, tm)
    io_in = lax.broadcasted_iota(I32, (N_GROUPS, GROUP_SIZE, tm), 1)
    m1 = jnp.max(sel3, axis=1, keepdims=True)
    f1 = jnp.min(jnp.where(sel3 == m1, io_in, GROUP_SIZE), axis=1, keepdims=True)
    m2 = jnp.max(jnp.where(io_in == f1, neg, sel3), axis=1, keepdims=True)
    gs = (m1 + m2).reshape(N_GROUPS, tm)
    io_g = lax.broadcasted_iota(I32, (N_GROUPS, tm), 0)
    gmask = jnp.zeros((N_GROUPS, tm), jnp.bool_)
    for _ in range(TOPK_GROUPS):
        mg = jnp.max(gs, axis=0, keepdims=True)
        fg = _first_index(gs == mg, io_g, N_GROUPS)
        pick = io_g == fg
        gmask = jnp.logical_or(gmask, pick)
        gs = jnp.where(pick, neg, gs)
    emask = jnp.broadcast_to(gmask.reshape(N_GROUPS, 1, tm), (N_GROUPS, GROUP_SIZE, tm)).reshape(N_EXPERTS, tm)
    cur = jnp.where(emask, sel, neg)
    io_e = lax.broadcasted_iota(I32, (N_EXPERTS, tm), 0)
    picks, idxs, wts = [], [], []
    for _ in range(TOP_K):
        me = jnp.max(cur, axis=0, keepdims=True)
        fe = _first_index(cur == me, io_e, N_EXPERTS)
        pick = io_e == fe
        picks.append(pick)
        idxs.append(fe)
        wts.append(jnp.sum(jnp.where(pick, scores, 0.0), axis=0, keepdims=True))
        cur = jnp.where(pick, neg, cur)
    wsum = wts[0]
    for w in wts[1:]:
        wsum = wsum + w
    zero_i = jnp.zeros((1, tm), I32)
    zero_f = jnp.zeros((1, tm), F32)
    idx_ref[...] = jnp.concatenate(idxs + [zero_i, zero_i], axis=0)
    wts_ref[...] = jnp.concatenate([w / wsum * ROUTED_SCALE for w in wts] + [zero_f, zero_f], axis=0)
    cnt = picks[0].astype(F32)
    for pk in picks[1:]:
        cnt = cnt + pk.astype(F32)
    excl = jnp.dot(cnt.astype(BF16), ustrict_ref[...], preferred_element_type=F32)
    ranks = [jnp.sum(jnp.where(pk, excl, 0.0), axis=0, keepdims=True).astype(I32) for pk in picks]
    rank_ref[...] = jnp.concatenate(ranks + [zero_i, zero_i], axis=0)


def _outproj(x2d, yf, yr, mod4, norm2_g, w_out, router_w, router_bias, swg, swu, swd, T):
    N, D = x2d.shape
    tm = min(512, T)
    tps = T // tm
    vec = lambda j: pl.BlockSpec((None, None, 1, D), lambda i: (i // tps, j, 0, 0))
    full = lambda a: pl.BlockSpec(a.shape, lambda i: (0, 0))
    ts = min(MOE_TILE, tm)
    ustrict = jnp.asarray(np.kron(np.eye(tm // ts, dtype=np.float32),
                                  np.triu(np.ones((ts, ts), np.float32), 1)), BF16)
    rb = jnp.broadcast_to(router_bias.reshape(N_EXPERTS, 1), (N_EXPERTS, LANES))
    consts = [norm2_g.reshape(1, D), w_out.astype(BF16), router_w.T, rb, swg.astype(BF16),
              swu.astype(BF16), swd.astype(BF16), ustrict]
    small = lambda dt: jax.ShapeDtypeStruct((8, N), dt)
    return pl.pallas_call(
        functools.partial(_outproj_kernel, tm=tm),
        grid=(N // tm,),
        in_specs=[pl.BlockSpec((tm, D), lambda i: (i, 0)),
                  pl.BlockSpec((tm, HALF), lambda i: (i, 0)),
                  pl.BlockSpec((tm, HALF), lambda i: (i, 0)),
                  vec(2), vec(3), vec(4), vec(5)] + [full(a) for a in consts],
        out_specs=[pl.BlockSpec((tm, D), lambda i: (i, 0)),
                   pl.BlockSpec((tm, D // 2), lambda i: (i, 0)),
                   pl.BlockSpec((8, tm), lambda i: (0, i)),
                   pl.BlockSpec((8, tm), lambda i: (0, i)),
                   pl.BlockSpec((8, tm), lambda i: (0, i))],
        out_shape=[jax.ShapeDtypeStruct((N, D), F32), jax.ShapeDtypeStruct((N, D // 2), jnp.uint32),
                   small(I32), small(F32), small(I32)],
        compiler_params=_cparams(("arbitrary",)),
        name="outproj",
    )(x2d, yf, yr, mod4, mod4, mod4, mod4, *consts)


def _for_each(count, fn, group):
    full = count // group

    def trip(t, c):
        for u in range(group):
            fn(t * group + u)
        return c

    def single(g, c):
        fn(g)
        return c

    lax.fori_loop(0, full, trip, 0)
    lax.fori_loop(full * group, count, single, 0)


def _wait_granules(count, wait_rows):
    full = count // WAIT_CHUNK

    def chunk(_, c):
        wait_rows(WAIT_CHUNK * GRANULE)
        return c

    def single(_, c):
        wait_rows(GRANULE)
        return c

    lax.fori_loop(0, full, chunk, 0)
    lax.fori_loop(full * WAIT_CHUNK, count, single, 0)


def _moe_plan(idx_t, rank_t):
    n = idx_t.shape[1]
    n_tiles = n // MOE_TILE
    experts = jnp.arange(N_EXPERTS, dtype=I32)
    hot = idx_t[:TOP_K, :, None] == experts
    cnt = jnp.sum(hot.reshape(TOP_K, n_tiles, MOE_TILE, N_EXPERTS).astype(I32), axis=(0, 2))
    gran = (cnt + GRANULE - 1) // GRANULE
    loc_end = jnp.cumsum(gran, axis=1)
    loc_off = loc_end - gran
    g_tile = loc_end[:, N_EXPERTS - 1]
    padded = (GRANULE * jnp.sum(gran, axis=0) + EXPERT_BLOCK - 1) // EXPERT_BLOCK * EXPERT_BLOCK
    pad_end = jnp.cumsum(padded)
    glob_off = (pad_end - padded)[None, :] + GRANULE * (jnp.cumsum(gran, axis=0) - gran)
    loc_tok = jnp.repeat(GRANULE * loc_off, MOE_TILE, axis=0)
    pos = rank_t[:TOP_K] + jnp.sum(jnp.where(hot, loc_tok[None], 0), axis=-1)
    pos = jnp.concatenate([pos, jnp.full((8 - TOP_K, n), -1, I32)], axis=0)
    g = jnp.arange(MOE_TILE, dtype=I32)
    e_of_g = jnp.minimum(jnp.sum((loc_end[:, None, :] <= g[None, :, None]).astype(I32), axis=-1), N_EXPERTS - 1)
    dst = jnp.sum(jnp.where(e_of_g[:, :, None] == experts,
                            glob_off[:, None, :] + GRANULE * (g[None, :, None] - loc_off[:, None, :]), 0), axis=-1)
    gtab = jnp.where(g[None, :] < g_tile[:, None], dst, 0).reshape(n_tiles, 1, MOE_TILE)
    n_blocks = -(-(n * TOP_K + GRANULE * N_EXPERTS * n_tiles) // EXPERT_BLOCK) + N_EXPERTS
    blk_start = jnp.arange(n_blocks, dtype=I32) * EXPERT_BLOCK
    blk_e = jnp.minimum(jnp.sum((pad_end[None, :] <= blk_start[:, None]).astype(I32), axis=1), N_EXPERTS - 1)
    plan = dict(pos=pos.astype(I32), gtab=gtab.astype(I32), g_tile=g_tile.astype(I32),
                pad_end=pad_end.astype(I32), padded=padded.astype(I32), blk_e=blk_e.astype(I32),
                n_used=(pad_end[N_EXPERTS - 1:] // EXPERT_BLOCK).astype(I32))
    return plan, n_blocks * EXPERT_BLOCK


def _dispatch_kernel(pend_ref, padded_ref, nu_ref, gt_ref, gtab_ref, pos_ref, hp_ref, xs_ref, zeros, sbuf,
                     sem, zsem, *, n_blocks, n_tiles):
    @pl.when(pl.program_id(0) == 0)
    def _():
        zeros[...] = jnp.zeros_like(zeros)

        def block_copy(start):
            return pltpu.make_async_copy(zeros, xs_ref.at[pl.ds(pl.multiple_of(start, EXPERT_BLOCK),
                                                                 EXPERT_BLOCK), :], zsem)

        def pad_loop(fn):
            def body(e, _):
                @pl.when(padded_ref[e] > 0)
                def _():
                    fn(block_copy(pend_ref[e] - EXPERT_BLOCK))
                return 0
            lax.fori_loop(0, N_EXPERTS, body, 0)

        def tail_loop(fn):
            def body(b, _):
                fn(block_copy(b * EXPERT_BLOCK))
                return 0
            lax.fori_loop(nu_ref[0], n_blocks, body, 0)

        pad_loop(lambda cp: cp.start())
        tail_loop(lambda cp: cp.start())
        pad_loop(lambda cp: cp.wait())
        tail_loop(lambda cp: cp.wait())

    i = pl.program_id(0)
    slot = i % 2

    def wait_granules(which, count):
        def wait_rows(rows):
            pltpu.make_async_copy(sbuf.at[which, pl.ds(0, rows), :], xs_ref.at[pl.ds(0, rows), :],
                                  sem.at[which]).wait()
        _wait_granules(count, wait_rows)

    @pl.when(i >= 2)
    def _():
        wait_granules(slot, gt_ref[jnp.maximum(i - 2, 0)])

    pos = pos_ref[...]
    r_iota = lax.broadcasted_iota(I32, (MOE_SORTED_ROWS, MOE_TILE), 0)
    hit = r_iota == pos[0:1, :]
    for k in range(1, TOP_K):
        hit = jnp.logical_or(hit, r_iota == pos[k:k + 1, :])
    perm = jnp.where(hit, 1.0, 0.0).astype(BF16)
    lo, hi = _unpack_halves(hp_ref[...])
    sbuf[slot] = _pack_halves(jnp.concatenate([_mm(perm, lo), _mm(perm, hi)], axis=1))

    def issue(g):
        pltpu.make_async_copy(sbuf.at[slot, pl.ds(pl.multiple_of(g * GRANULE, GRANULE), GRANULE), :],
                              xs_ref.at[pl.ds(pl.multiple_of(gtab_ref[0, 0, g], GRANULE), GRANULE), :],
                              sem.at[slot]).start()
    _for_each(gt_ref[i], issue, DMA_UNROLL)

    @pl.when(i == n_tiles - 1)
    def _():
        wait_granules(slot, gt_ref[i])
        if n_tiles > 1:
            wait_granules(1 - slot, gt_ref[jnp.maximum(i - 1, 0)])


def _dispatch(hp, plan, n_rows):
    N, W = hp.shape
    n_tiles = N // MOE_TILE
    n_blocks = n_rows // EXPERT_BLOCK
    return pl.pallas_call(
        functools.partial(_dispatch_kernel, n_blocks=n_blocks, n_tiles=n_tiles),
        grid_spec=pltpu.PrefetchScalarGridSpec(
            num_scalar_prefetch=4, grid=(n_tiles,),
            in_specs=[pl.BlockSpec((1, 1, MOE_TILE), lambda i, *_: (i, 0, 0), memory_space=pltpu.SMEM),
                      pl.BlockSpec((8, MOE_TILE), lambda i, *_: (0, i)),
                      pl.BlockSpec((MOE_TILE, W), lambda i, *_: (i, 0))],
            out_specs=pl.BlockSpec(memory_space=pl.ANY),
            scratch_shapes=[pltpu.VMEM((EXPERT_BLOCK, W), jnp.uint32),
                            pltpu.VMEM((2, MOE_SORTED_ROWS, W), jnp.uint32),
                            pltpu.SemaphoreType.DMA((2,)), pltpu.SemaphoreType.DMA]),
        out_shape=jax.ShapeDtypeStruct((n_rows, W), jnp.uint32),
        compiler_params=_cparams(("arbitrary",)),
        name="dispatch",
    )(plan["pad_end"], plan["padded"], plan["n_used"], plan["g_tile"], plan["gtab"], plan["pos"], hp)


def _experts_kernel(be_ref, nu_ref, xs_ref, wgf_ref, wuf_ref, wdf_ref, ys_ref, wg_ref, wu_ref, wd_ref):
    i = pl.program_id(0)
    live = i < nu_ref[0]
    new_expert = jnp.logical_or(i == 0, be_ref[i] != be_ref[jnp.maximum(i - 1, 0)])

    @pl.when(jnp.logical_not(live))
    def _():
        ys_ref[...] = jnp.zeros_like(ys_ref)

    @pl.when(jnp.logical_and(live, new_expert))
    def _():
        wg_ref[...] = wgf_ref[...].astype(BF16)
        wu_ref[...] = wuf_ref[...].astype(BF16)
        wd_ref[...] = wdf_ref[...].astype(BF16)

    @pl.when(live)
    def _():
        half = D_MODEL // 2
        lo, hi = _unpack_halves(xs_ref[...])
        lo = lo.astype(BF16)
        hi = hi.astype(BF16)
        gate = (jnp.dot(lo, wg_ref[0:half, :], preferred_element_type=F32)
                + jnp.dot(hi, wg_ref[half:D_MODEL, :], preferred_element_type=F32))
        up = (jnp.dot(lo, wu_ref[0:half, :], preferred_element_type=F32)
              + jnp.dot(hi, wu_ref[half:D_MODEL, :], preferred_element_type=F32))
        y = jnp.dot((_silu(gate) * up).astype(BF16), wd_ref[...], preferred_element_type=F32)
        ys_ref[...] = _pack_halves(y)


def _experts(xs, blk_e, n_used, wg, wu, wd):
    n_rows, W = xs.shape
    n_blocks = n_rows // EXPERT_BLOCK
    row_map = lambda i, be, nu: (jnp.minimum(i, nu[0] - 1), 0)
    return pl.pallas_call(
        _experts_kernel,
        grid_spec=pltpu.PrefetchScalarGridSpec(
            num_scalar_prefetch=2, grid=(n_blocks,),
            in_specs=[pl.BlockSpec((EXPERT_BLOCK, W), row_map),
                      pl.BlockSpec((None, D_MODEL, D_EXPERT), lambda i, be, nu: (be[i], 0, 0)),
                      pl.BlockSpec((None, D_MODEL, D_EXPERT), lambda i, be, nu: (be[i], 0, 0)),
                      pl.BlockSpec((None, D_EXPERT, D_MODEL), lambda i, be, nu: (be[i], 0, 0))],
            out_specs=pl.BlockSpec((EXPERT_BLOCK, W), lambda i, be, nu: (i, 0)),
            scratch_shapes=[pltpu.VMEM((D_MODEL, D_EXPERT), BF16), pltpu.VMEM((D_MODEL, D_EXPERT), BF16),
                            pltpu.VMEM((D_EXPERT, D_MODEL), BF16)]),
        out_shape=jax.ShapeDtypeStruct((n_rows, W), jnp.uint32),
        compiler_params=_cparams(("arbitrary",)),
        name="experts",
    )(blk_e, n_used, xs, wg, wu, wd)


def _combine_kernel(gt_ref, gtab_ref, gtab_next_ref, ys_ref, pos_ref, xs_ref, wts_ref, g2_ref, fg_ref, o_ref,
                    buf, sem, *, n_tiles):
    i = pl.program_id(0)
    slot = i % 2

    def fetch(which, table_ref, count):
        def body(g):
            pltpu.make_async_copy(
                ys_ref.at[pl.ds(pl.multiple_of(table_ref[0, 0, g], GRANULE), GRANULE), :],
                buf.at[which, pl.ds(pl.multiple_of(g * GRANULE, GRANULE), GRANULE), :], sem.at[which]).start()
        _for_each(count, body, DMA_UNROLL)

    @pl.when(i == 0)
    def _():
        buf[...] = jnp.zeros_like(buf)
        fetch(0, gtab_ref, gt_ref[0])

    @pl.when(i + 1 < n_tiles)
    def _():
        fetch(1 - slot, gtab_next_ref, gt_ref[jnp.minimum(i + 1, n_tiles - 1)])

    def wait_rows(rows):
        pltpu.make_async_copy(ys_ref.at[pl.ds(0, rows), :], buf.at[slot, pl.ds(0, rows), :],
                              sem.at[slot]).wait()
    _wait_granules(gt_ref[i], wait_rows)

    pos = pos_ref[...]
    w = wts_ref[...]
    c_iota = lax.broadcasted_iota(I32, (MOE_TILE, MOE_SORTED_ROWS), 1)
    wmat = jnp.where(c_iota == pos[:, 0:1], w[:, 0:1], 0.0)
    for k in range(1, TOP_K):
        wmat = wmat + jnp.where(c_iota == pos[:, k:k + 1], w[:, k:k + 1], 0.0)
    lo, hi = _unpack_halves(buf[slot])
    lo = lo.astype(BF16)
    hi = hi.astype(BF16)
    routed = jnp.concatenate([_mm_split_r(wmat, lo), _mm_split_r(wmat, hi)], axis=1)
    x3 = xs_ref[...] + g2_ref[...] * routed
    y = x3 * lax.rsqrt(jnp.mean(x3 * x3, axis=-1, keepdims=True) + RMS_EPS) * fg_ref[...]
    o_ref[...] = y


def _combine(ys, plan, xsr, wts_t, mod4, final_g, T):
    N, D = xsr.shape
    W = ys.shape[1]
    tm = MOE_TILE
    tps = T // tm
    n_tiles = N // tm
    tile_tab = lambda off: pl.BlockSpec((1, 1, MOE_TILE), lambda i, gt: (jnp.minimum(i + off, n_tiles - 1), 0, 0),
                                        memory_space=pltpu.SMEM)
    return pl.pallas_call(
        functools.partial(_combine_kernel, n_tiles=n_tiles),
        grid_spec=pltpu.PrefetchScalarGridSpec(
            num_scalar_prefetch=1, grid=(n_tiles,),
            in_specs=[tile_tab(0), tile_tab(1),
                      pl.BlockSpec(memory_space=pl.ANY),
                      pl.BlockSpec((tm, 8), lambda i, gt: (i, 0)),
                      pl.BlockSpec((tm, D), lambda i, gt: (i, 0)),
                      pl.BlockSpec((tm, 8), lambda i, gt: (i, 0)),
                      pl.BlockSpec((None, None, 1, D), lambda i, gt: (i // tps, 5, 0, 0)),
                      pl.BlockSpec((1, D), lambda i, gt: (0, 0))],
            out_specs=pl.BlockSpec((tm, D), lambda i, gt: (i, 0)),
            scratch_shapes=[pltpu.VMEM((2, MOE_SORTED_ROWS, W), jnp.uint32), pltpu.SemaphoreType.DMA((2,))]),
        out_shape=jax.ShapeDtypeStruct((N, D), F32),
        compiler_params=_cparams(("arbitrary",)),
        name="combine",
    )(plan["g_tile"], plan["gtab"], plan["gtab"], ys, plan["pos"].T, xsr, wts_t, mod4, final_g.reshape(1, D))


def _pack_w_in(w_in):
    D = w_in.shape[0]
    fox_cols = 4 * HALF + 3 * N_HEADS
    wf = w_in[:, :fox_cols]
    wr = w_in[:, fox_cols:]
    o = 3 * HALF
    pad = lambda a, n: jnp.concatenate([a, jnp.zeros((D, n - a.shape[1]), a.dtype)], axis=1)
    parts = [wf[:, :4 * HALF], wr[:, :o],
             pad(wf[:, 4 * HALF:], LANES),
             pad(wr[:, o:o + DECAY_LORA], LANES),
             pad(wr[:, o + DECAY_LORA:o + DECAY_LORA + ICLR_LORA], LANES),
             pad(wr[:, o + DECAY_LORA + ICLR_LORA:], 2 * LANES)]
    return jnp.concatenate(parts, axis=1).astype(BF16)


def _pack_mu(mu):
    o = 3 * HALF
    pad = lambda a, n: jnp.concatenate([a, jnp.zeros((n - a.shape[0],), a.dtype)])
    small = jnp.concatenate([jnp.zeros((LANES,), mu.dtype),
                             pad(mu[o:o + DECAY_LORA], LANES),
                             pad(mu[o + DECAY_LORA:o + DECAY_LORA + ICLR_LORA], LANES),
                             pad(mu[o + DECAY_LORA + ICLR_LORA:], 2 * LANES)])
    return mu[:o].reshape(1, o), small.reshape(1, Z_SMALL)


def kernel(x, c, norm1_g, norm2_g, ada_w, ada_b, w_in, w_out, fox_qn_g, fox_kn_g, fox_on_g, fox_forget_b,
           rw_mu, rw_w0, rw_decay_up, rw_a0, rw_iclr_up, rw_gate_up, rw_k_k, rw_k_a, rw_r_k, rw_lnx_g,
           rw_lnx_b, router_w, router_bias, exp_w_gate, exp_w_up, exp_w_down, sh_w_gate, sh_w_up,
           sh_w_down, final_g):
    B, T, D = x.shape
    N = B * T
    depth = norm1_g.shape[0]
    assert depth == 1, "the combine kernel fuses the final RMSNorm, so exactly one layer is supported"
    xf = x.reshape(N, D)
    for l in range(depth):
        mod4 = _ada(c, ada_w[l], ada_b[l]).reshape(B, 6, 1, D)
        mu_big, mu_small = _pack_mu(rw_mu[l])
        zm, zs = _inproj(xf, mod4, norm1_g[l], _pack_w_in(w_in[l]), mu_big, mu_small, T)
        qp, kp, vp = _foxprep(zm, zs, fox_forget_b[l], fox_qn_g[l], fox_kn_g[l], T)
        y_fox = _attention(qp, kp, vp, zm, fox_on_g[l], B, T)
        rw = dict(rw_w0=rw_w0[l], rw_decay_up=rw_decay_up[l], rw_a0=rw_a0[l], rw_iclr_up=rw_iclr_up[l],
                  rw_gate_up=rw_gate_up[l], rw_k_k=rw_k_k[l], rw_k_a=rw_k_a[l], rw_r_k=rw_r_k[l],
                  rw_lnx_g=rw_lnx_g[l], rw_lnx_b=rw_lnx_b[l])
        y_rwkv = _rwkv(zm, zs, rw, B, T)
        xsr, hp, idx_t, wts_t, rank_t = _outproj(
            xf, y_fox, y_rwkv, mod4, norm2_g[l], w_out[l], router_w[l], router_bias[l],
            sh_w_gate[l], sh_w_up[l], sh_w_down[l], T)
        plan, n_rows = _moe_plan(idx_t, rank_t)
        xs = _dispatch(hp, plan, n_rows)
        ys = _experts(xs, plan["blk_e"], plan["n_used"], exp_w_gate[l], exp_w_up[l], exp_w_down[l])
        xf = _combine(ys, plan, xsr, wts_t.T, mod4, final_g, T)
    return xf.reshape(B, T, D)
```

```python
import functools

import jax
import jax.numpy as jnp
import numpy as np
from jax import lax
from jax.experimental import pallas as pl
from jax.experimental.pallas import tpu as pltpu

F32 = jnp.float32
BF16 = jnp.bfloat16
I32 = jnp.int32
HIGHEST = lax.Precision.HIGHEST

D_MODEL = 1024
HEAD_DIM = 64
N_HEADS = 8
HALF = N_HEADS * HEAD_DIM
RMS_EPS = 1e-6
LNX_EPS = 64e-5
LOG2E = 1.4426950408889634
DECAY_LORA = 64
ICLR_LORA = 64
GATE_LORA = 160
N_EXPERTS = 64
N_GROUPS = 8
GROUP_SIZE = N_EXPERTS // N_GROUPS
TOPK_GROUPS = 4
TOP_K = 6
D_EXPERT = 256
D_SHARED = 256
ROUTED_SCALE = 2.5
EXPERT_BLOCK = 1024

LANES = 128
Z_MAIN = 4 * HALF + 3 * HALF
Z_SMALL = 5 * LANES
VMEM_LIMIT = 56 * 1024 * 1024
ATTN_HEADS = 4
DMA_UNROLL = 4
WAIT_CHUNK = 16
MOE_TILE = 256
GRANULE = 8
MOE_SORTED_ROWS = MOE_TILE * TOP_K + N_EXPERTS * GRANULE
RWKV_CHUNK = 64
RWKV_SEQS_PER_STEP = 2


def _cparams(semantics):
    return pltpu.CompilerParams(dimension_semantics=semantics, vmem_limit_bytes=VMEM_LIMIT)


def _mm(a, b):
    return jnp.dot(a.astype(BF16), b.astype(BF16), preferred_element_type=F32)


def _mm_nt(a, b):
    return lax.dot_general(a.astype(BF16), b.astype(BF16), (((1,), (1,)), ((), ())),
                           preferred_element_type=F32)


def _mm_tn(a, b):
    return lax.dot_general(a.astype(BF16), b.astype(BF16), (((0,), (0,)), ((), ())),
                           preferred_element_type=F32)


def _mm_f32(a, b):
    return jnp.dot(a, b, precision=HIGHEST, preferred_element_type=F32)


def _bf16_pieces(x, passes):
    pieces = []
    for _ in range(passes):
        piece = x.astype(BF16)
        pieces.append(piece)
        x = x - piece.astype(F32)
    return pieces


def _mm_split(m01, x, passes=3):
    return sum(jnp.dot(m01, p, preferred_element_type=F32) for p in _bf16_pieces(x, passes))


def _mm_split_r(x, m01, passes=2):
    return sum(jnp.dot(p, m01, preferred_element_type=F32) for p in _bf16_pieces(x, passes))


def _sigmoid(x):
    return 1.0 / (1.0 + jnp.exp(-x))


def _softplus(x):
    return jnp.maximum(x, 0.0) + jnp.log(1.0 + jnp.exp(-jnp.abs(x)))


def _silu(x):
    return x * _sigmoid(x)


def _pack_halves(x):
    w = x.shape[1] // 2
    bits = lambda t: lax.bitcast_convert_type(t.astype(jnp.bfloat16).astype(F32), jnp.uint32)
    return (bits(x[:, 0:w]) >> 16) | (bits(x[:, w:2 * w]) & jnp.uint32(0xFFFF0000))


def _pack_bf16_values(lo, hi):
    return (lax.bitcast_convert_type(lo, jnp.uint32) >> 16) | lax.bitcast_convert_type(hi, jnp.uint32)


def _unpack_halves(p):
    lo = lax.bitcast_convert_type(p << 16, F32)
    hi = lax.bitcast_convert_type(p & jnp.uint32(0xFFFF0000), F32)
    return lo, hi


def _shift_rows(z, carry_ref, first):
    rows = z.shape[0]
    prev_row = jnp.where(first, 0.0, carry_ref[0:1, :])
    prev = pltpu.roll(z, 1, 0)
    row0 = lax.broadcasted_iota(I32, (rows, 1), 0) == 0
    prev = jnp.where(row0, prev_row, prev)
    carry_ref[0:1, :] = z[rows - 1:rows, :]
    return prev


def _ada_kernel(c_ref, w_ref, b_ref, o_ref):
    o_ref[...] = _mm_f32(_silu(c_ref[...]), w_ref[...]) + b_ref[...]


def _ada(c, ada_w, ada_b):
    B, D = c.shape
    n_out = ada_w.shape[1]
    tn = 512
    return pl.pallas_call(
        _ada_kernel,
        grid=(n_out // tn,),
        in_specs=[pl.BlockSpec((B, D), lambda j: (0, 0)),
                  pl.BlockSpec((D, tn), lambda j: (0, j)),
                  pl.BlockSpec((1, tn), lambda j: (0, j))],
        out_specs=pl.BlockSpec((B, tn), lambda j: (0, j)),
        out_shape=jax.ShapeDtypeStruct((B, n_out), F32),
        compiler_params=_cparams(("arbitrary",)),
        name="ada",
    )(c, ada_w, ada_b.reshape(1, n_out))


def _inproj_kernel(x_ref, g_ref, sh_ref, sc_ref, w_ref, mub_ref, mus_ref, zm_ref, zs_ref,
                   carry_b, carry_s, *, tiles_per_seq):
    first = (pl.program_id(0) % tiles_per_seq) == 0
    x = x_ref[...]
    h = x * lax.rsqrt(jnp.mean(x * x, axis=-1, keepdims=True) + RMS_EPS) * g_ref[...]
    hb = (h * (1.0 + sc_ref[...]) + sh_ref[...]).astype(BF16)
    nf = 4 * HALF
    zm_ref[:, 0:nf] = jnp.dot(hb, w_ref[:, 0:nf], preferred_element_type=F32).astype(BF16)
    zr = jnp.dot(hb, w_ref[:, nf:Z_MAIN], preferred_element_type=F32)
    zr = zr + mub_ref[...] * (_shift_rows(zr, carry_b, first) - zr)
    zm_ref[:, nf:Z_MAIN] = zr.astype(BF16)
    zs = jnp.dot(hb, w_ref[:, Z_MAIN:Z_MAIN + Z_SMALL], preferred_element_type=F32)
    zs_ref[...] = zs + mus_ref[...] * (_shift_rows(zs, carry_s, first) - zs)


def _inproj(x2d, mod4, norm_g, w_all, mu_big, mu_small, T):
    N, D = x2d.shape
    tm = min(512, T)
    tps = T // tm
    vec = lambda j: pl.BlockSpec((None, None, 1, D), lambda i: (i // tps, j, 0, 0))
    return pl.pallas_call(
        functools.partial(_inproj_kernel, tiles_per_seq=tps),
        grid=(N // tm,),
        in_specs=[pl.BlockSpec((tm, D), lambda i: (i, 0)),
                  pl.BlockSpec((1, D), lambda i: (0, 0)),
                  vec(0), vec(1),
                  pl.BlockSpec((D, Z_MAIN + Z_SMALL), lambda i: (0, 0)),
                  pl.BlockSpec((1, 3 * HALF), lambda i: (0, 0)),
                  pl.BlockSpec((1, Z_SMALL), lambda i: (0, 0))],
        out_specs=[pl.BlockSpec((tm, Z_MAIN), lambda i: (i, 0)),
                   pl.BlockSpec((tm, Z_SMALL), lambda i: (i, 0))],
        out_shape=[jax.ShapeDtypeStruct((N, Z_MAIN), BF16),
                   jax.ShapeDtypeStruct((N, Z_SMALL), F32)],
        scratch_shapes=[pltpu.VMEM((8, 3 * HALF), F32), pltpu.VMEM((8, Z_SMALL), F32)],
        compiler_params=_cparams(("arbitrary",)),
        name="inproj",
    )(x2d, norm_g.reshape(1, D), mod4, mod4, w_all, mu_big, mu_small)


def _foxprep_kernel(q_ref, k_ref, v_ref, zs_ref, fb_ref, qg_ref, kg_ref, tri_ref, eexp_ref,
                    esum_ref, e8_ref, plq_ref, plk_ref, plv_ref, cq_ref, ck_ref, cv_ref,
                    qp_ref, kp_ref, vp_ref, carry_k, carry_v, carry_c, *, tiles_per_seq):
    first = (pl.program_id(0) % tiles_per_seq) == 0
    zs = zs_ref[...]
    logf = -_softplus(-(zs + fb_ref[...]))
    cum = _mm_split(tri_ref[...], logf) + jnp.where(first, 0.0, carry_c[0:1, :])
    carry_c[0:1, :] = cum[cum.shape[0] - 1:, :]
    cum2 = cum * LOG2E
    c_hi = cum2.astype(BF16)
    r1 = cum2 - c_hi.astype(F32)
    c_mid = r1.astype(BF16)
    c_lo = (r1 - c_mid.astype(F32)).astype(BF16)
    a_full = _mm_split_r(_sigmoid(zs), eexp_ref[...])
    k = k_ref[...].astype(F32)
    v = v_ref[...].astype(F32)
    a_k = a_full[:, 0:HALF]
    a_v = a_full[:, HALF:2 * HALF]
    k = a_k * _shift_rows(k, carry_k, first) + (1.0 - a_k) * k
    v = a_v * _shift_rows(v, carry_v, first) + (1.0 - a_v) * v
    q = q_ref[...].astype(F32)

    def head_rms(t, gain):
        ms = _mm_split_r(t * t, esum_ref[...]) * (1.0 / HEAD_DIM)
        inv = _mm_split_r(lax.rsqrt(ms + RMS_EPS), e8_ref[...])
        return t * inv * gain

    qn = head_rms(q, qg_ref[...]) * (HEAD_DIM ** -0.5 * LOG2E)
    kn = head_rms(k, kg_ref[...])
    aug = [c_hi, c_mid, c_lo]
    lhs_q = jnp.concatenate([qn.astype(BF16)] + aug, axis=1)
    lhs_k = jnp.concatenate([kn.astype(BF16)] + aug, axis=1)
    qp_ref[...] = (jnp.dot(lhs_q, plq_ref[...], preferred_element_type=F32) + cq_ref[...]).astype(BF16)
    kp_ref[...] = (jnp.dot(lhs_k, plk_ref[...], preferred_element_type=F32) + ck_ref[...]).astype(BF16)
    vp_ref[...] = (jnp.dot(v.astype(BF16), plv_ref[...], preferred_element_type=F32)
                   + cv_ref[...]).astype(BF16)


def _fox_constants(tm):
    hp = N_HEADS * LANES
    eexp = np.zeros((LANES, 2 * HALF), np.float32)
    esum = np.zeros((HALF, LANES), np.float32)
    e8 = np.zeros((LANES, HALF), np.float32)
    plq = np.zeros((HALF + 3 * LANES, hp), np.float32)
    plk = np.zeros((HALF + 3 * LANES, hp), np.float32)
    plv = np.zeros((HALF, hp), np.float32)
    cq = np.zeros((1, hp), np.float32)
    ck = np.zeros((1, hp), np.float32)
    cv = np.zeros((1, hp), np.float32)
    for h in range(N_HEADS):
        sl = slice(h * HEAD_DIM, (h + 1) * HEAD_DIM)
        eexp[8 + h, sl] = 1.0
        eexp[16 + h, HALF + h * HEAD_DIM:HALF + (h + 1) * HEAD_DIM] = 1.0
        esum[sl, h] = 1.0
        e8[h, sl] = 1.0
        base = h * LANES
        for d in range(HEAD_DIM):
            plq[h * HEAD_DIM + d, base + d] = 1.0
            plk[h * HEAD_DIM + d, base + d] = 1.0
            plv[h * HEAD_DIM + d, base + d] = 1.0
        for j in range(3):
            plq[HALF + j * LANES + h, base + HEAD_DIM + j] = 1.0
            plk[HALF + j * LANES + h, base + HEAD_DIM + 3 + j] = -1.0
            cq[0, base + HEAD_DIM + 3 + j] = 1.0
            ck[0, base + HEAD_DIM + j] = 1.0
        cv[0, base + HEAD_DIM] = 1.0
    tri = np.tril(np.ones((tm, tm), np.float32))
    bf = lambda a: jnp.asarray(a, BF16)
    return dict(tri=bf(tri), eexp=bf(eexp), esum=bf(esum), e8=bf(e8),
                plq=bf(plq), plk=bf(plk), plv=bf(plv), cq=jnp.asarray(cq), ck=jnp.asarray(ck),
                cv=jnp.asarray(cv))


def _foxprep(zm, zs, forget_b, qn_g, kn_g, T):
    N = zm.shape[0]
    tm = min(256, T)
    tps = T // tm
    cst = _fox_constants(tm)
    hp = N_HEADS * LANES
    fb = jnp.zeros((1, LANES), F32).at[0, :N_HEADS].set(forget_b)
    full = lambda a: pl.BlockSpec(a.shape, lambda i: (0, 0))
    consts = [fb, qn_g.reshape(1, HALF), kn_g.reshape(1, HALF), cst["tri"], cst["eexp"], cst["esum"],
              cst["e8"], cst["plq"], cst["plk"], cst["plv"], cst["cq"], cst["ck"], cst["cv"]]
    return pl.pallas_call(
        functools.partial(_foxprep_kernel, tiles_per_seq=tps),
        grid=(N // tm,),
        in_specs=[pl.BlockSpec((tm, HALF), lambda i: (i, 0)),
                  pl.BlockSpec((tm, HALF), lambda i: (i, 1)),
                  pl.BlockSpec((tm, HALF), lambda i: (i, 2)),
                  pl.BlockSpec((tm, LANES), lambda i: (i, 0))] + [full(a) for a in consts],
        out_specs=[pl.BlockSpec((tm, hp), lambda i: (i, 0))] * 3,
        out_shape=[jax.ShapeDtypeStruct((N, hp), BF16)] * 3,
        scratch_shapes=[pltpu.VMEM((8, HALF), F32), pltpu.VMEM((8, HALF), F32),
                        pltpu.VMEM((8, LANES), F32)],
        compiler_params=_cparams(("arbitrary",)),
        name="foxprep",
    )(zm, zm, zm, zs, *consts)


def _attn_kernel(q_ref, k_ref, v_ref, g_ref, ong_ref, o_ref, *, tq):
    i = pl.program_id(2)
    row = lax.broadcasted_iota(I32, (tq, tq), 0)
    col = lax.broadcasted_iota(I32, (tq, tq), 1)
    causal = col <= row
    heads = range(ATTN_HEADS)
    lanes = [slice(hh * LANES, (hh + 1) * LANES) for hh in heads]
    qs = [q_ref[:, lanes[hh]] for hh in heads]

    def step(j, carry, masked):
        m, acc = carry
        start = pl.multiple_of(j * tq, tq)
        s = [lax.dot_general(qs[hh], k_ref[pl.ds(start, tq), lanes[hh]], (((1,), (1,)), ((), ())),
                             preferred_element_type=F32) for hh in heads]
        m_out, acc_out = [], []
        for hh in heads:
            sh = jnp.where(causal, s[hh], -jnp.inf) if masked else s[hh]
            m_new = jnp.maximum(m[hh], jnp.max(sh, axis=1, keepdims=True))
            p = jnp.exp2(sh - m_new)
            v = v_ref[pl.ds(start, tq), lanes[hh]]
            acc_out.append(jnp.exp2(m[hh] - m_new) * acc[hh]
                           + jnp.dot(p.astype(BF16), v, preferred_element_type=F32))
            m_out.append(m_new)
        return tuple(m_out), tuple(acc_out)

    init = (tuple(jnp.full((tq, 1), -jnp.inf, F32) for _ in heads),
            tuple(jnp.zeros((tq, LANES), F32) for _ in heads))
    carry = lax.fori_loop(0, i, functools.partial(step, masked=False), init)
    _, acc = step(i, carry, True)
    lane = lax.broadcasted_iota(I32, (tq, LANES), 1)
    lane_w = jnp.where(lane < HEAD_DIM, 1.0 / HEAD_DIM, jnp.where(lane == HEAD_DIM, RMS_EPS, 0.0))
    outs = []
    for hh in heads:
        t = jnp.sum(acc[hh] * acc[hh] * lane_w, axis=1, keepdims=True)
        outs.append(acc[hh] * lax.rsqrt(t))
    o = jnp.concatenate([jnp.where(lane < HEAD_DIM, outs[2 * p], pltpu.roll(outs[2 * p + 1], HEAD_DIM, 1))
                         for p in range(ATTN_HEADS // 2)], axis=1)
    y = o * ong_ref[...] * _sigmoid(g_ref[...].astype(F32))
    o_ref[...] = y.astype(BF16)


def _attention(qp, kp, vp, zm, on_g, B, T):
    N = qp.shape[0]
    tq = min(256, T)
    nq = T // tq
    groups = N_HEADS // ATTN_HEADS
    wp = ATTN_HEADS * LANES
    wo = ATTN_HEADS * HEAD_DIM
    g_col0 = 3 * HALF // wo
    return pl.pallas_call(
        functools.partial(_attn_kernel, tq=tq),
        grid=(B, groups, nq),
        in_specs=[pl.BlockSpec((tq, wp), lambda b, p, i: (b * nq + i, p)),
                  pl.BlockSpec((T, wp), lambda b, p, i: (b, p)),
                  pl.BlockSpec((T, wp), lambda b, p, i: (b, p)),
                  pl.BlockSpec((tq, wo), lambda b, p, i: (b * nq + i, g_col0 + p)),
                  pl.BlockSpec((None, 1, wo), lambda b, p, i: (p, 0, 0))],
        out_specs=pl.BlockSpec((tq, wo), lambda b, p, i: (b * nq + i, p)),
        out_shape=jax.ShapeDtypeStruct((N, HALF), BF16),
        compiler_params=_cparams(("arbitrary", "arbitrary", "arbitrary")),
        name="attn",
    )(qp, kp, vp, zm, on_g.reshape(groups, 1, wo))


def _rwkv_kernel(r_ref, k_ref, v_ref, zs_ref, w0_ref, dup_ref, a0_ref, iup_ref, gup_ref, kk_ref,
                 ka_ref, rk_ref, lng_ref, lnb_ref, tri_ref, o_ref, state, ybuf, *, chunk, nb):
    C = chunk
    c_idx = pl.program_id(1)

    @pl.when(c_idx == 0)
    def _():
        state[...] = jnp.zeros_like(state)

    R = nb * C
    r = r_ref[...].reshape(R, HALF).astype(F32)
    k = k_ref[...].reshape(R, HALF).astype(F32)
    v = v_ref[...].reshape(R, HALF).astype(F32)
    zs = zs_ref[...].reshape(R, Z_SMALL)
    wd = zs[:, LANES:2 * LANES]
    ad = zs[:, 2 * LANES:3 * LANES]
    gd = zs[:, 3 * LANES:5 * LANES]
    wl = w0_ref[...] + _mm(jnp.tanh(wd), dup_ref[...])
    lw = -jnp.exp(-_softplus(-wl) - 0.5)
    a = _sigmoid(a0_ref[...] + _mm(ad, iup_ref[...]))
    g = _mm(_sigmoid(gd), gup_ref[...])
    kk = k * kk_ref[...]
    k2 = k * (1.0 + (a - 1.0) * ka_ref[...])
    cl = _mm_split(tri_ref[...], lw)
    cl_end = jnp.concatenate(
        [jnp.broadcast_to(cl[(bi + 1) * C - 1:(bi + 1) * C, :], (C, HALF)) for bi in range(nb)], axis=0)
    e_neg = jnp.exp(-cl)
    e_tail = jnp.exp(cl_end - cl)
    pre = dict(rt=r * jnp.exp(cl), at=-kk * jnp.exp(cl - lw), kh=k2 * e_neg, bh=kk * a * e_neg,
               kb=k2 * e_tail, bb=kk * a * e_tail, v=v, kk=kk, pend=jnp.exp(cl_end),
               rkr=r * k2 * rk_ref[...], g=g)

    C2 = 2 * C
    row = lax.broadcasted_iota(I32, (C2, C2), 0)
    col = lax.broadcasted_iota(I32, (C2, C2), 1)
    lower = (col & (C - 1)) <= (row & (C - 1))
    strict = (col & (C - 1)) < (row & (C - 1))
    eye = row == col
    head0 = lax.broadcasted_iota(I32, (C, LANES), 1) < HEAD_DIM
    n_sq = int(np.log2(C)) - 1
    units = [(bi, p) for bi in range(nb) for p in range(N_HEADS // 2)]

    def part(name, bi, p):
        return pre[name][bi * C:(bi + 1) * C, p * LANES:(p + 1) * LANES]

    def stack(x):
        return jnp.concatenate([jnp.where(head0, x, 0.0), jnp.where(head0, 0.0, x)], axis=0)

    def unstack(x):
        return x[0:C] + x[C:C2]

    def head_sum(x):
        s0 = jnp.sum(jnp.where(head0, x, 0.0), axis=1, keepdims=True)
        s1 = jnp.sum(jnp.where(head0, 0.0, x), axis=1, keepdims=True)
        return jnp.where(head0, s0, s1)

    ops = []
    for bi, p in units:
        kk_p = part("kk", bi, p)
        inv = 1.0 / jnp.maximum(jnp.sqrt(head_sum(kk_p * kk_p)), 1e-12)
        ops.append(dict(rt=stack(part("rt", bi, p)), at=stack(part("at", bi, p) * inv),
                        kh=stack(part("kh", bi, p)).astype(BF16), bh=stack(part("bh", bi, p) * inv).astype(BF16),
                        kb=stack(part("kb", bi, p)).astype(BF16), bb=stack(part("bb", bi, p) * inv).astype(BF16),
                        v=stack(part("v", bi, p)).astype(BF16)))
    ra = [jnp.concatenate([o["rt"], o["at"]], axis=0).astype(BF16) for o in ops]
    g1 = [_mm_nt(x, o["kh"]) for x, o in zip(ra, ops)]
    g2 = [_mm_nt(x, o["bh"]) for x, o in zip(ra, ops)]
    a_rb = [jnp.where(lower, t[0:C2], 0.0).astype(BF16) for t in g2]
    pw = [jnp.where(strict, t[C2:2 * C2], 0.0) for t in g2]
    av = [_mm(jnp.concatenate([jnp.where(lower, t[0:C2], 0.0), jnp.where(strict, t[C2:2 * C2], 0.0)], axis=0),
              o["v"]) for t, o in zip(g1, ops)]
    xs = [jnp.concatenate([o["at"], t[C2:2 * C2]], axis=1) for o, t in zip(ops, av)]
    for level in range(n_sq + 1):
        pb = [p.astype(BF16) for p in pw]
        xs = [x + _mm(p, x) for p, x in zip(pb, xs)]
        if level < n_sq:
            pw = [jnp.dot(p, p, preferred_element_type=F32) for p in pb]
    xb = [x.astype(BF16) for x in xs]
    rb = [jnp.dot(p, x, preferred_element_type=F32) for p, x in zip(a_rb, xb)]
    bx = [_mm_tn(o["bb"], x) for o, x in zip(ops, xb)]
    kv = [_mm_tn(o["kb"], o["v"]) for o in ops]
    for u, (bi, p) in enumerate(units):
        o = ops[u]
        sl = slice(p * LANES, (p + 1) * LANES)
        r2 = unstack(o["rt"] + rb[u][:, 0:LANES])
        y0 = unstack(rb[u][:, LANES:2 * LANES] + av[u][0:C2])
        m_mat = jnp.where(eye, part("pend", bi, p)[0:1, :], 0.0) + bx[u][:, 0:LANES]
        g_mat = bx[u][:, LANES:2 * LANES] + kv[u]
        out = _mm(jnp.concatenate([r2, m_mat], axis=0), state[bi, p])
        state[bi, p] = out[C:C + LANES] + g_mat
        y = out[0:C] + y0
        cen = y - head_sum(y) * (1.0 / HEAD_DIM)
        var = head_sum(cen * cen) * (1.0 / HEAD_DIM)
        bonus = head_sum(part("rkr", bi, p)) * part("v", bi, p)
        y = cen * lax.rsqrt(var + LNX_EPS) * lng_ref[:, sl] + lnb_ref[:, sl] + bonus
        ybuf[bi, :, sl] = y * part("g", bi, p)
    o_ref[...] = ybuf[...].astype(BF16)


def _rwkv(zm, zs, p, B, T):
    N = zm.shape[0]
    C = RWKV_CHUNK
    assert 2 * C == LANES and T % C == 0, "a head pair's stacked chunk must fill one 128-row tile"
    nc = T // C
    nb = RWKV_SEQS_PER_STEP if B % RWKV_SEQS_PER_STEP == 0 else 1
    tri = np.kron(np.eye(nb, dtype=np.float32), np.tril(np.ones((C, C), np.float32)))
    pad_rows = lambda w, rows: jnp.zeros((rows, HALF), F32).at[:w.shape[0]].set(w)
    row = lambda a: a.reshape(1, HALF)
    consts = [row(p["rw_w0"]), pad_rows(p["rw_decay_up"], LANES), row(p["rw_a0"]),
              pad_rows(p["rw_iclr_up"], LANES), pad_rows(p["rw_gate_up"], 2 * LANES),
              row(p["rw_k_k"]), row(p["rw_k_a"]), row(p["rw_r_k"]), row(p["rw_lnx_g"]),
              row(p["rw_lnx_b"]), jnp.asarray(tri, BF16)]
    full = lambda a: pl.BlockSpec(a.shape, lambda b, c: (0, 0))
    rcol = 4 * HALF // HALF
    zm3 = zm.reshape(B, T, Z_MAIN)
    out = pl.pallas_call(
        functools.partial(_rwkv_kernel, chunk=C, nb=nb),
        grid=(B // nb, nc),
        in_specs=[pl.BlockSpec((nb, C, HALF), lambda b, c: (b, c, rcol)),
                  pl.BlockSpec((nb, C, HALF), lambda b, c: (b, c, rcol + 1)),
                  pl.BlockSpec((nb, C, HALF), lambda b, c: (b, c, rcol + 2)),
                  pl.BlockSpec((nb, C, Z_SMALL), lambda b, c: (b, c, 0))] + [full(a) for a in consts],
        out_specs=pl.BlockSpec((nb, C, HALF), lambda b, c: (b, c, 0)),
        out_shape=jax.ShapeDtypeStruct((B, T, HALF), BF16),
        scratch_shapes=[pltpu.VMEM((nb, N_HEADS // 2, LANES, LANES), F32), pltpu.VMEM((nb, C, HALF), F32)],
        compiler_params=_cparams(("arbitrary", "arbitrary")),
        name="rwkv",
    )(zm3, zm3, zm3, zs.reshape(B, T, Z_SMALL), *consts)
    return out.reshape(N, HALF)


def _first_index(mask, iota, big):
    return jnp.min(jnp.where(mask, iota, big), axis=0, keepdims=True)


def _outproj_kernel(x_ref, yf_ref, yr_ref, g1_ref, sh2_ref, sc2_ref, g2_ref, n2g_ref, wo_ref, rwt_ref,
                    rb_ref, swg_ref, swu_ref, swd_ref, ustrict_ref,
                    xs_ref, hp_ref, idx_ref, wts_ref, rank_ref, *, tm):
    d = (jnp.dot(yf_ref[...], wo_ref[0:HALF, :], preferred_element_type=F32)
         + jnp.dot(yr_ref[...], wo_ref[HALF:2 * HALF, :], preferred_element_type=F32))
    x2 = x_ref[...] + g1_ref[...] * d
    h = x2 * lax.rsqrt(jnp.mean(x2 * x2, axis=-1, keepdims=True) + RMS_EPS) * n2g_ref[...]
    h = h * (1.0 + sc2_ref[...]) + sh2_ref[...]
    hb = h.astype(BF16)
    act = _silu(jnp.dot(hb, swg_ref[...], preferred_element_type=F32)) * jnp.dot(
        hb, swu_ref[...], preferred_element_type=F32)
    shared = jnp.dot(act.astype(BF16), swd_ref[...], preferred_element_type=F32)
    xs_ref[...] = x2 + g2_ref[...] * shared
    half = D_MODEL // 2
    hp_ref[...] = _pack_halves(h)

    logits = lax.dot_general(rwt_ref[...], h, (((1,), (1,)), ((), ())), precision=HIGHEST,
                             preferred_element_type=F32)
    scores = _sigmoid(logits)
    sel = scores + rb_ref[:, 0:1]
    neg = -jnp.inf
    sel3 = sel.reshape(N_GROUPS, GROUP_SIZE, tm)
    io_in = lax.broadcasted_iota(I32, (N_GROUPS, GROUP_SIZE, tm), 1)
    m1 = jnp.max(sel3, axis=1, keepdims=True)
    f1 = jnp.min(jnp.where(sel3 == m1, io_in, GROUP_SIZE), axis=1, keepdims=True)
    m2 = jnp.max(jnp.where(io_in == f1, neg, sel3), axis=1, keepdims=True)
    gs = (m1 + m2).reshape(N_GROUPS, tm)
    io_g = lax.broadcasted_iota(I32, (N_GROUPS, tm), 0)
    gmask = jnp.zeros((N_GROUPS, tm), jnp.bool_)
    for _ in range(TOPK_GROUPS):
        mg = jnp.max(gs, axis=0, keepdims=True)
        fg = _first_index(gs == mg, io_g, N_GROUPS)
        pick = io_g == fg
        gmask = jnp.logical_or(gmask, pick)
        gs = jnp.where(pick, neg, gs)
    emask = jnp.broadcast_to(gmask.reshape(N_GROUPS, 1, tm), (N_GROUPS, GROUP_SIZE, tm)).reshape(N_EXPERTS, tm)
    cur = jnp.where(emask, sel, neg)
    io_e = lax.broadcasted_iota(I32, (N_EXPERTS, tm), 0)
    picks, idxs, wts = [], [], []
    for _ in range(TOP_K):
        me = jnp.max(cur, axis=0, keepdims=True)
        fe = _first_index(cur == me, io_e, N_EXPERTS)
        pick = io_e == fe
        picks.append(pick)
        idxs.append(fe)
        wts.append(jnp.sum(jnp.where(pick, scores, 0.0), axis=0, keepdims=True))
        cur = jnp.where(pick, neg, cur)
    wsum = wts[0]
    for w in wts[1:]:
        wsum = wsum + w
    zero_i = jnp.zeros((1, tm), I32)
    zero_f = jnp.zeros((1, tm), F32)
    idx_ref[...] = jnp.concatenate(idxs + [zero_i, zero_i], axis=0)
    wts_ref[...] = jnp.concatenate([w / wsum * ROUTED_SCALE for w in wts] + [zero_f, zero_f], axis=0)
    cnt = picks[0].astype(F32)
    for pk in picks[1:]:
        cnt = cnt + pk.astype(F32)
    excl = jnp.dot(cnt.astype(BF16), ustrict_ref[...], preferred_element_type=F32)
    ranks = [jnp.sum(jnp.where(pk, excl, 0.0), axis=0, keepdims=True).astype(I32) for pk in picks]
    rank_ref[...] = jnp.concatenate(ranks + [zero_i, zero_i], axis=0)


def _outproj(x2d, yf, yr, mod4, norm2_g, w_out, router_w, router_bias, swg, swu, swd, T):
    N, D = x2d.shape
    tm = min(512, T)
    tps = T // tm
    vec = lambda j: pl.BlockSpec((None, None, 1, D), lambda i: (i // tps, j, 0, 0))
    full = lambda a: pl.BlockSpec(a.shape, lambda i: (0, 0))
    ts = min(MOE_TILE, tm)
    ustrict = jnp.asarray(np.kron(np.eye(tm // ts, dtype=np.float32),
                                  np.triu(np.ones((ts, ts), np.float32), 1)), BF16)
    rb = jnp.broadcast_to(router_bias.reshape(N_EXPERTS, 1), (N_EXPERTS, LANES))
    consts = [norm2_g.reshape(1, D), w_out.astype(BF16), router_w.T, rb, swg.astype(BF16),
              swu.astype(BF16), swd.astype(BF16), ustrict]
    small = lambda dt: jax.ShapeDtypeStruct((8, N), dt)
    return pl.pallas_call(
        functools.partial(_outproj_kernel, tm=tm),
        grid=(N // tm,),
        in_specs=[pl.BlockSpec((tm, D), lambda i: (i, 0)),
                  pl.BlockSpec((tm, HALF), lambda i: (i, 0)),
                  pl.BlockSpec((tm, HALF), lambda i: (i, 0)),
                  vec(2), vec(3), vec(4), vec(5)] + [full(a) for a in consts],
        out_specs=[pl.BlockSpec((tm, D), lambda i: (i, 0)),
                   pl.BlockSpec((tm, D // 2), lambda i: (i, 0)),
                   pl.BlockSpec((8, tm), lambda i: (0, i)),
                   pl.BlockSpec((8, tm), lambda i: (0, i)),
                   pl.BlockSpec((8, tm), lambda i: (0, i))],
        out_shape=[jax.ShapeDtypeStruct((N, D), F32), jax.ShapeDtypeStruct((N, D // 2), jnp.uint32),
                   small(I32), small(F32), small(I32)],
        compiler_params=_cparams(("arbitrary",)),
        name="outproj",
    )(x2d, yf, yr, mod4, mod4, mod4, mod4, *consts)


def _for_each(count, fn, group):
    full = count // group

    def trip(t, c):
        for u in range(group):
            fn(t * group + u)
        return c

    def single(g, c):
        fn(g)
        return c

    lax.fori_loop(0, full, trip, 0)
    lax.fori_loop(full * group, count, single, 0)


def _wait_granules(count, wait_rows):
    full = count // WAIT_CHUNK

    def chunk(_, c):
        wait_rows(WAIT_CHUNK * GRANULE)
        return c

    def single(_, c):
        wait_rows(GRANULE)
        return c

    lax.fori_loop(0, full, chunk, 0)
    lax.fori_loop(full * WAIT_CHUNK, count, single, 0)


def _moe_plan(idx_t, rank_t):
    n = idx_t.shape[1]
    n_tiles = n // MOE_TILE
    experts = jnp.arange(N_EXPERTS, dtype=I32)
    hot = idx_t[:TOP_K, :, None] == experts
    cnt = jnp.sum(hot.reshape(TOP_K, n_tiles, MOE_TILE, N_EXPERTS).astype(I32), axis=(0, 2))
    gran = (cnt + GRANULE - 1) // GRANULE
    loc_end = jnp.cumsum(gran, axis=1)
    loc_off = loc_end - gran
    g_tile = loc_end[:, N_EXPERTS - 1]
    padded = (GRANULE * jnp.sum(gran, axis=0) + EXPERT_BLOCK - 1) // EXPERT_BLOCK * EXPERT_BLOCK
    pad_end = jnp.cumsum(padded)
    glob_off = (pad_end - padded)[None, :] + GRANULE * (jnp.cumsum(gran, axis=0) - gran)
    loc_tok = jnp.repeat(GRANULE * loc_off, MOE_TILE, axis=0)
    pos = rank_t[:TOP_K] + jnp.sum(jnp.where(hot, loc_tok[None], 0), axis=-1)
    pos = jnp.concatenate([pos, jnp.full((8 - TOP_K, n), -1, I32)], axis=0)
    g = jnp.arange(MOE_TILE, dtype=I32)
    e_of_g = jnp.minimum(jnp.sum((loc_end[:, None, :] <= g[None, :, None]).astype(I32), axis=-1), N_EXPERTS - 1)
    dst = jnp.sum(jnp.where(e_of_g[:, :, None] == experts,
                            glob_off[:, None, :] + GRANULE * (g[None, :, None] - loc_off[:, None, :]), 0), axis=-1)
    gtab = jnp.where(g[None, :] < g_tile[:, None], dst, 0).reshape(n_tiles, 1, MOE_TILE)
    n_blocks = -(-(n * TOP_K + GRANULE * N_EXPERTS * n_tiles) // EXPERT_BLOCK) + N_EXPERTS
    blk_start = jnp.arange(n_blocks, dtype=I32) * EXPERT_BLOCK
    blk_e = jnp.minimum(jnp.sum((pad_end[None, :] <= blk_start[:, None]).astype(I32), axis=1), N_EXPERTS - 1)
    plan = dict(pos=pos.astype(I32), gtab=gtab.astype(I32), g_tile=g_tile.astype(I32),
                pad_end=pad_end.astype(I32), padded=padded.astype(I32), blk_e=blk_e.astype(I32),
                n_used=(pad_end[N_EXPERTS - 1:] // EXPERT_BLOCK).astype(I32))
    return plan, n_blocks * EXPERT_BLOCK


def _dispatch_kernel(pend_ref, padded_ref, nu_ref, gt_ref, gtab_ref, pos_ref, hp_ref, xs_ref, zeros, sbuf,
                     sem, zsem, *, n_blocks, n_tiles):
    @pl.when(pl.program_id(0) == 0)
    def _():
        zeros[...] = jnp.zeros_like(zeros)

        def block_copy(start):
            return pltpu.make_async_copy(zeros, xs_ref.at[pl.ds(pl.multiple_of(start, EXPERT_BLOCK),
                                                                 EXPERT_BLOCK), :], zsem)

        def pad_loop(fn):
            def body(e, _):
                @pl.when(padded_ref[e] > 0)
                def _():
                    fn(block_copy(pend_ref[e] - EXPERT_BLOCK))
                return 0
            lax.fori_loop(0, N_EXPERTS, body, 0)

        def tail_loop(fn):
            def body(b, _):
                fn(block_copy(b * EXPERT_BLOCK))
                return 0
            lax.fori_loop(nu_ref[0], n_blocks, body, 0)

        pad_loop(lambda cp: cp.start())
        tail_loop(lambda cp: cp.start())
        pad_loop(lambda cp: cp.wait())
        tail_loop(lambda cp: cp.wait())

    i = pl.program_id(0)
    slot = i % 2

    def wait_granules(which, count):
        def wait_rows(rows):
            pltpu.make_async_copy(sbuf.at[which, pl.ds(0, rows), :], xs_ref.at[pl.ds(0, rows), :],
                                  sem.at[which]).wait()
        _wait_granules(count, wait_rows)

    @pl.when(i >= 2)
    def _():
        wait_granules(slot, gt_ref[jnp.maximum(i - 2, 0)])

    pos = pos_ref[...]
    r_iota = lax.broadcasted_iota(I32, (MOE_SORTED_ROWS, MOE_TILE), 0)
    hit = r_iota == pos[0:1, :]
    for k in range(1, TOP_K):
        hit = jnp.logical_or(hit, r_iota == pos[k:k + 1, :])
    perm = jnp.where(hit, 1.0, 0.0).astype(BF16)
    lo, hi = _unpack_halves(hp_ref[...])
    sbuf[slot] = _pack_bf16_values(_mm(perm, lo), _mm(perm, hi))

    def issue(g):
        pltpu.make_async_copy(sbuf.at[slot, pl.ds(pl.multiple_of(g * GRANULE, GRANULE), GRANULE), :],
                              xs_ref.at[pl.ds(pl.multiple_of(gtab_ref[0, 0, g], GRANULE), GRANULE), :],
                              sem.at[slot]).start()
    _for_each(gt_ref[i], issue, DMA_UNROLL)

    @pl.when(i == n_tiles - 1)
    def _():
        wait_granules(slot, gt_ref[i])
        if n_tiles > 1:
            wait_granules(1 - slot, gt_ref[jnp.maximum(i - 1, 0)])


def _dispatch(hp, plan, n_rows):
    N, W = hp.shape
    n_tiles = N // MOE_TILE
    n_blocks = n_rows // EXPERT_BLOCK
    return pl.pallas_call(
        functools.partial(_dispatch_kernel, n_blocks=n_blocks, n_tiles=n_tiles),
        grid_spec=pltpu.PrefetchScalarGridSpec(
            num_scalar_prefetch=4, grid=(n_tiles,),
            in_specs=[pl.BlockSpec((1, 1, MOE_TILE), lambda i, *_: (i, 0, 0), memory_space=pltpu.SMEM),
                      pl.BlockSpec((8, MOE_TILE), lambda i, *_: (0, i)),
                      pl.BlockSpec((MOE_TILE, W), lambda i, *_: (i, 0))],
            out_specs=pl.BlockSpec(memory_space=pl.ANY),
            scratch_shapes=[pltpu.VMEM((EXPERT_BLOCK, W), jnp.uint32),
                            pltpu.VMEM((2, MOE_SORTED_ROWS, W), jnp.uint32),
                            pltpu.SemaphoreType.DMA((2,)), pltpu.SemaphoreType.DMA]),
        out_shape=jax.ShapeDtypeStruct((n_rows, W), jnp.uint32),
        compiler_params=_cparams(("arbitrary",)),
        name="dispatch",
    )(plan["pad_end"], plan["padded"], plan["n_used"], plan["g_tile"], plan["gtab"], plan["pos"], hp)


def _experts_kernel(be_ref, nu_ref, xs_ref, wgf_ref, wuf_ref, wdf_ref, ys_ref, wg_ref, wu_ref, wd_ref):
    i = pl.program_id(0)
    live = i < nu_ref[0]
    new_expert = jnp.logical_or(i == 0, be_ref[i] != be_ref[jnp.maximum(i - 1, 0)])

    @pl.when(jnp.logical_not(live))
    def _():
        ys_ref[...] = jnp.zeros_like(ys_ref)

    @pl.when(jnp.logical_and(live, new_expert))
    def _():
        wg_ref[...] = wgf_ref[...].astype(BF16)
        wu_ref[...] = wuf_ref[...].astype(BF16)
        wd_ref[...] = wdf_ref[...].astype(BF16)

    @pl.when(live)
    def _():
        half = D_MODEL // 2
        lo, hi = _unpack_halves(xs_ref[...])
        lo = lo.astype(BF16)
        hi = hi.astype(BF16)
        gate = (jnp.dot(lo, wg_ref[0:half, :], preferred_element_type=F32)
                + jnp.dot(hi, wg_ref[half:D_MODEL, :], preferred_element_type=F32))
        up = (jnp.dot(lo, wu_ref[0:half, :], preferred_element_type=F32)
              + jnp.dot(hi, wu_ref[half:D_MODEL, :], preferred_element_type=F32))
        y = jnp.dot((_silu(gate) * up).astype(BF16), wd_ref[...], preferred_element_type=F32)
        ys_ref[...] = _pack_halves(y)


def _experts(xs, blk_e, n_used, wg, wu, wd):
    n_rows, W = xs.shape
    n_blocks = n_rows // EXPERT_BLOCK
    row_map = lambda i, be, nu: (jnp.minimum(i, nu[0] - 1), 0)
    return pl.pallas_call(
        _experts_kernel,
        grid_spec=pltpu.PrefetchScalarGridSpec(
            num_scalar_prefetch=2, grid=(n_blocks,),
            in_specs=[pl.BlockSpec((EXPERT_BLOCK, W), row_map),
                      pl.BlockSpec((None, D_MODEL, D_EXPERT), lambda i, be, nu: (be[i], 0, 0)),
                      pl.BlockSpec((None, D_MODEL, D_EXPERT), lambda i, be, nu: (be[i], 0, 0)),
                      pl.BlockSpec((None, D_EXPERT, D_MODEL), lambda i, be, nu: (be[i], 0, 0))],
            out_specs=pl.BlockSpec((EXPERT_BLOCK, W), lambda i, be, nu: (i, 0)),
            scratch_shapes=[pltpu.VMEM((D_MODEL, D_EXPERT), BF16), pltpu.VMEM((D_MODEL, D_EXPERT), BF16),
                            pltpu.VMEM((D_EXPERT, D_MODEL), BF16)]),
        out_shape=jax.ShapeDtypeStruct((n_rows, W), jnp.uint32),
        compiler_params=_cparams(("arbitrary",)),
        name="experts",
    )(blk_e, n_used, xs, wg, wu, wd)


def _combine_kernel(gt_ref, gtab_ref, gtab_next_ref, ys_ref, pos_ref, xs_ref, wts_ref, g2_ref, fg_ref, o_ref,
                    buf, sem, *, n_tiles):
    i = pl.program_id(0)
    slot = i % 2

    def fetch(which, table_ref, count):
        def body(g):
            pltpu.make_async_copy(
                ys_ref.at[pl.ds(pl.multiple_of(table_ref[0, 0, g], GRANULE), GRANULE), :],
                buf.at[which, pl.ds(pl.multiple_of(g * GRANULE, GRANULE), GRANULE), :], sem.at[which]).start()
        _for_each(count, body, DMA_UNROLL)

    @pl.when(i == 0)
    def _():
        buf[...] = jnp.zeros_like(buf)
        fetch(0, gtab_ref, gt_ref[0])

    @pl.when(i + 1 < n_tiles)
    def _():
        fetch(1 - slot, gtab_next_ref, gt_ref[jnp.minimum(i + 1, n_tiles - 1)])

    def wait_rows(rows):
        pltpu.make_async_copy(ys_ref.at[pl.ds(0, rows), :], buf.at[slot, pl.ds(0, rows), :],
                              sem.at[slot]).wait()
    _wait_granules(gt_ref[i], wait_rows)

    pos = pos_ref[...]
    w = wts_ref[...]
    c_iota = lax.broadcasted_iota(I32, (MOE_TILE, MOE_SORTED_ROWS), 1)
    wmat = jnp.where(c_iota == pos[:, 0:1], w[:, 0:1], 0.0)
    for k in range(1, TOP_K):
        wmat = wmat + jnp.where(c_iota == pos[:, k:k + 1], w[:, k:k + 1], 0.0)
    lo, hi = _unpack_halves(buf[slot])
    lo = lo.astype(BF16)
    hi = hi.astype(BF16)
    wb = wmat.astype(BF16)
    routed = jnp.concatenate([jnp.dot(wb, lo, preferred_element_type=F32),
                              jnp.dot(wb, hi, preferred_element_type=F32)], axis=1)
    x3 = xs_ref[...] + g2_ref[...] * routed
    y = x3 * lax.rsqrt(jnp.mean(x3 * x3, axis=-1, keepdims=True) + RMS_EPS) * fg_ref[...]
    o_ref[...] = y


def _combine(ys, plan, xsr, wts_t, mod4, final_g, T):
    N, D = xsr.shape
    W = ys.shape[1]
    tm = MOE_TILE
    tps = T // tm
    n_tiles = N // tm
    tile_tab = lambda off: pl.BlockSpec((1, 1, MOE_TILE), lambda i, gt: (jnp.minimum(i + off, n_tiles - 1), 0, 0),
                                        memory_space=pltpu.SMEM)
    return pl.pallas_call(
        functools.partial(_combine_kernel, n_tiles=n_tiles),
        grid_spec=pltpu.PrefetchScalarGridSpec(
            num_scalar_prefetch=1, grid=(n_tiles,),
            in_specs=[tile_tab(0), tile_tab(1),
                      pl.BlockSpec(memory_space=pl.ANY),
                      pl.BlockSpec((tm, 8), lambda i, gt: (i, 0)),
                      pl.BlockSpec((tm, D), lambda i, gt: (i, 0)),
                      pl.BlockSpec((tm, 8), lambda i, gt: (i, 0)),
                      pl.BlockSpec((None, None, 1, D), lambda i, gt: (i // tps, 5, 0, 0)),
                      pl.BlockSpec((1, D), lambda i, gt: (0, 0))],
            out_specs=pl.BlockSpec((tm, D), lambda i, gt: (i, 0)),
            scratch_shapes=[pltpu.VMEM((2, MOE_SORTED_ROWS, W), jnp.uint32), pltpu.SemaphoreType.DMA((2,))]),
        out_shape=jax.ShapeDtypeStruct((N, D), F32),
        compiler_params=_cparams(("arbitrary",)),
        name="combine",
    )(plan["g_tile"], plan["gtab"], plan["gtab"], ys, plan["pos"].T, xsr, wts_t, mod4, final_g.reshape(1, D))


def _pack_w_in(w_in):
    D = w_in.shape[0]
    fox_cols = 4 * HALF + 3 * N_HEADS
    wf = w_in[:, :fox_cols]
    wr = w_in[:, fox_cols:]
    o = 3 * HALF
    pad = lambda a, n: jnp.concatenate([a, jnp.zeros((D, n - a.shape[1]), a.dtype)], axis=1)
    parts = [wf[:, :4 * HALF], wr[:, :o],
             pad(wf[:, 4 * HALF:], LANES),
             pad(wr[:, o:o + DECAY_LORA], LANES),
             pad(wr[:, o + DECAY_LORA:o + DECAY_LORA + ICLR_LORA], LANES),
             pad(wr[:, o + DECAY_LORA + ICLR_LORA:], 2 * LANES)]
    return jnp.concatenate(parts, axis=1).astype(BF16)


def _pack_mu(mu):
    o = 3 * HALF
    pad = lambda a, n: jnp.concatenate([a, jnp.zeros((n - a.shape[0],), a.dtype)])
    small = jnp.concatenate([jnp.zeros((LANES,), mu.dtype),
                             pad(mu[o:o + DECAY_LORA], LANES),
                             pad(mu[o + DECAY_LORA:o + DECAY_LORA + ICLR_LORA], LANES),
                             pad(mu[o + DECAY_LORA + ICLR_LORA:], 2 * LANES)])
    return mu[:o].reshape(1, o), small.reshape(1, Z_SMALL)


def kernel(x, c, norm1_g, norm2_g, ada_w, ada_b, w_in, w_out, fox_qn_g, fox_kn_g, fox_on_g, fox_forget_b,
           rw_mu, rw_w0, rw_decay_up, rw_a0, rw_iclr_up, rw_gate_up, rw_k_k, rw_k_a, rw_r_k, rw_lnx_g,
           rw_lnx_b, router_w, router_bias, exp_w_gate, exp_w_up, exp_w_down, sh_w_gate, sh_w_up,
           sh_w_down, final_g):
    B, T, D = x.shape
    N = B * T
    depth = norm1_g.shape[0]
    assert depth == 1, "the combine kernel fuses the final RMSNorm, so exactly one layer is supported"
    xf = x.reshape(N, D)
    for l in range(depth):
        mod4 = _ada(c, ada_w[l], ada_b[l]).reshape(B, 6, 1, D)
        mu_big, mu_small = _pack_mu(rw_mu[l])
        zm, zs = _inproj(xf, mod4, norm1_g[l], _pack_w_in(w_in[l]), mu_big, mu_small, T)
        qp, kp, vp = _foxprep(zm, zs, fox_forget_b[l], fox_qn_g[l], fox_kn_g[l], T)
        y_fox = _attention(qp, kp, vp, zm, fox_on_g[l], B, T)
        rw = dict(rw_w0=rw_w0[l], rw_decay_up=rw_decay_up[l], rw_a0=rw_a0[l], rw_iclr_up=rw_iclr_up[l],
                  rw_gate_up=rw_gate_up[l], rw_k_k=rw_k_k[l], rw_k_a=rw_k_a[l], rw_r_k=rw_r_k[l],
                  rw_lnx_g=rw_lnx_g[l], rw_lnx_b=rw_lnx_b[l])
        y_rwkv = _rwkv(zm, zs, rw, B, T)
        xsr, hp, idx_t, wts_t, rank_t = _outproj(
            xf, y_fox, y_rwkv, mod4, norm2_g[l], w_out[l], router_w[l], router_bias[l],
            sh_w_gate[l], sh_w_up[l], sh_w_down[l], T)
        plan, n_rows = _moe_plan(idx_t, rank_t)
        xs = _dispatch(hp, plan, n_rows)
        ys = _experts(xs, plan["blk_e"], plan["n_used"], exp_w_gate[l], exp_w_up[l], exp_w_down[l])
        xf = _combine(ys, plan, xsr, wts_t.T, mod4, final_g, T)
    return xf.reshape(B, T, D)
```

```python
import functools

import jax
import jax.numpy as jnp
import numpy as np
from jax import lax
from jax.experimental import pallas as pl
from jax.experimental.pallas import tpu as pltpu

F32 = jnp.float32
BF16 = jnp.bfloat16
I32 = jnp.int32
HIGHEST = lax.Precision.HIGHEST

D_MODEL = 1024
HEAD_DIM = 64
N_HEADS = 8
HALF = N_HEADS * HEAD_DIM
RMS_EPS = 1e-6
LNX_EPS = 64e-5
LOG2E = 1.4426950408889634
DECAY_LORA = 64
ICLR_LORA = 64
GATE_LORA = 160
N_EXPERTS = 64
N_GROUPS = 8
GROUP_SIZE = N_EXPERTS // N_GROUPS
TOPK_GROUPS = 4
TOP_K = 6
D_EXPERT = 256
D_SHARED = 256
ROUTED_SCALE = 2.5
EXPERT_BLOCK = 1024

LANES = 128
Z_MAIN = 4 * HALF + 3 * HALF
Z_SMALL = 5 * LANES
VMEM_LIMIT = 56 * 1024 * 1024
ATTN_BLOCK = 512
ATTN_HEADS = 4
DMA_UNROLL = 4
WAIT_CHUNK = 16
MOE_TILE = 256
GRANULE = 8
MOE_SORTED_ROWS = MOE_TILE * TOP_K + N_EXPERTS * GRANULE
RWKV_CHUNK = 64
RWKV_SEQS_PER_STEP = 2


def _cparams(semantics):
    return pltpu.CompilerParams(dimension_semantics=semantics, vmem_limit_bytes=VMEM_LIMIT)


def _mm(a, b):
    return jnp.dot(a.astype(BF16), b.astype(BF16), preferred_element_type=F32)


def _mm_nt(a, b):
    return lax.dot_general(a.astype(BF16), b.astype(BF16), (((1,), (1,)), ((), ())),
                           preferred_element_type=F32)


def _mm_tn(a, b):
    return lax.dot_general(a.astype(BF16), b.astype(BF16), (((0,), (0,)), ((), ())),
                           preferred_element_type=F32)


def _mm_f32(a, b):
    return jnp.dot(a, b, precision=HIGHEST, preferred_element_type=F32)


def _bf16_pieces(x, passes):
    pieces = []
    for _ in range(passes):
        piece = x.astype(BF16)
        pieces.append(piece)
        x = x - piece.astype(F32)
    return pieces


def _mm_split(m01, x, passes=3):
    return sum(jnp.dot(m01, p, preferred_element_type=F32) for p in _bf16_pieces(x, passes))


def _mm_split_r(x, m01, passes=2):
    return sum(jnp.dot(p, m01, preferred_element_type=F32) for p in _bf16_pieces(x, passes))


def _sigmoid(x):
    return 1.0 / (1.0 + jnp.exp(-x))


def _softplus(x):
    return jnp.maximum(x, 0.0) + jnp.log(1.0 + jnp.exp(-jnp.abs(x)))


def _silu(x):
    return x * _sigmoid(x)


def _pack_halves(x):
    w = x.shape[1] // 2
    bits = lambda t: lax.bitcast_convert_type(t.astype(jnp.bfloat16).astype(F32), jnp.uint32)
    return (bits(x[:, 0:w]) >> 16) | (bits(x[:, w:2 * w]) & jnp.uint32(0xFFFF0000))


def _pack_bf16_values(lo, hi):
    return (lax.bitcast_convert_type(lo, jnp.uint32) >> 16) | lax.bitcast_convert_type(hi, jnp.uint32)


def _unpack_halves(p):
    lo = lax.bitcast_convert_type(p << 16, F32)
    hi = lax.bitcast_convert_type(p & jnp.uint32(0xFFFF0000), F32)
    return lo, hi


def _shift_rows(z, carry_ref, first):
    rows = z.shape[0]
    prev_row = jnp.where(first, 0.0, carry_ref[0:1, :])
    prev = pltpu.roll(z, 1, 0)
    row0 = lax.broadcasted_iota(I32, (rows, 1), 0) == 0
    prev = jnp.where(row0, prev_row, prev)
    carry_ref[0:1, :] = z[rows - 1:rows, :]
    return prev


def _ada_kernel(c_ref, w_ref, b_ref, o_ref):
    o_ref[...] = _mm_f32(_silu(c_ref[...]), w_ref[...]) + b_ref[...]


def _ada(c, ada_w, ada_b):
    B, D = c.shape
    n_out = ada_w.shape[1]
    tn = 512
    return pl.pallas_call(
        _ada_kernel,
        grid=(n_out // tn,),
        in_specs=[pl.BlockSpec((B, D), lambda j: (0, 0)),
                  pl.BlockSpec((D, tn), lambda j: (0, j)),
                  pl.BlockSpec((1, tn), lambda j: (0, j))],
        out_specs=pl.BlockSpec((B, tn), lambda j: (0, j)),
        out_shape=jax.ShapeDtypeStruct((B, n_out), F32),
        compiler_params=_cparams(("arbitrary",)),
        name="ada",
    )(c, ada_w, ada_b.reshape(1, n_out))


def _inproj_kernel(x_ref, g_ref, sh_ref, sc_ref, w_ref, mub_ref, mus_ref, zm_ref, zs_ref,
                   carry_b, carry_s, *, tiles_per_seq):
    first = (pl.program_id(0) % tiles_per_seq) == 0
    x = x_ref[...]
    h = x * lax.rsqrt(jnp.mean(x * x, axis=-1, keepdims=True) + RMS_EPS) * g_ref[...]
    hb = (h * (1.0 + sc_ref[...]) + sh_ref[...]).astype(BF16)
    nf = 4 * HALF
    zm_ref[:, 0:nf] = jnp.dot(hb, w_ref[:, 0:nf], preferred_element_type=F32).astype(BF16)
    zr = jnp.dot(hb, w_ref[:, nf:Z_MAIN], preferred_element_type=F32)
    zr = zr + mub_ref[...] * (_shift_rows(zr, carry_b, first) - zr)
    zm_ref[:, nf:Z_MAIN] = zr.astype(BF16)
    zs = jnp.dot(hb, w_ref[:, Z_MAIN:Z_MAIN + Z_SMALL], preferred_element_type=F32)
    zs_ref[...] = zs + mus_ref[...] * (_shift_rows(zs, carry_s, first) - zs)


def _inproj(x2d, mod4, norm_g, w_all, mu_big, mu_small, T):
    N, D = x2d.shape
    tm = min(512, T)
    tps = T // tm
    vec = lambda j: pl.BlockSpec((None, None, 1, D), lambda i: (i // tps, j, 0, 0))
    return pl.pallas_call(
        functools.partial(_inproj_kernel, tiles_per_seq=tps),
        grid=(N // tm,),
        in_specs=[pl.BlockSpec((tm, D), lambda i: (i, 0)),
                  pl.BlockSpec((1, D), lambda i: (0, 0)),
                  vec(0), vec(1),
                  pl.BlockSpec((D, Z_MAIN + Z_SMALL), lambda i: (0, 0)),
                  pl.BlockSpec((1, 3 * HALF), lambda i: (0, 0)),
                  pl.BlockSpec((1, Z_SMALL), lambda i: (0, 0))],
        out_specs=[pl.BlockSpec((tm, Z_MAIN), lambda i: (i, 0)),
                   pl.BlockSpec((tm, Z_SMALL), lambda i: (i, 0))],
        out_shape=[jax.ShapeDtypeStruct((N, Z_MAIN), BF16),
                   jax.ShapeDtypeStruct((N, Z_SMALL), F32)],
        scratch_shapes=[pltpu.VMEM((8, 3 * HALF), F32), pltpu.VMEM((8, Z_SMALL), F32)],
        compiler_params=_cparams(("arbitrary",)),
        name="inproj",
    )(x2d, norm_g.reshape(1, D), mod4, mod4, w_all, mu_big, mu_small)


def _foxprep_kernel(q_ref, k_ref, v_ref, zs_ref, fb_ref, qg_ref, kg_ref, tri_ref, eexp_ref,
                    esum_ref, e8_ref, plq_ref, plk_ref, plv_ref, cq_ref, ck_ref, cv_ref,
                    qp_ref, kp_ref, vp_ref, carry_k, carry_v, carry_c, *, tiles_per_seq):
    first = (pl.program_id(0) % tiles_per_seq) == 0
    zs = zs_ref[...]
    logf = -_softplus(-(zs + fb_ref[...]))
    cum = _mm_split(tri_ref[...], logf) + jnp.where(first, 0.0, carry_c[0:1, :])
    carry_c[0:1, :] = cum[cum.shape[0] - 1:, :]
    cum2 = cum * LOG2E
    c_hi = cum2.astype(BF16)
    r1 = cum2 - c_hi.astype(F32)
    c_mid = r1.astype(BF16)
    c_lo = (r1 - c_mid.astype(F32)).astype(BF16)
    a_full = _mm_split_r(_sigmoid(zs), eexp_ref[...])
    k = k_ref[...].astype(F32)
    v = v_ref[...].astype(F32)
    a_k = a_full[:, 0:HALF]
    a_v = a_full[:, HALF:2 * HALF]
    k = a_k * _shift_rows(k, carry_k, first) + (1.0 - a_k) * k
    v = a_v * _shift_rows(v, carry_v, first) + (1.0 - a_v) * v
    q = q_ref[...].astype(F32)

    def head_rms(t, gain):
        ms = _mm_split_r(t * t, esum_ref[...]) * (1.0 / HEAD_DIM)
        inv = _mm_split_r(lax.rsqrt(ms + RMS_EPS), e8_ref[...])
        return t * inv * gain

    qn = head_rms(q, qg_ref[...]) * (HEAD_DIM ** -0.5 * LOG2E)
    kn = head_rms(k, kg_ref[...])
    aug = [c_hi, c_mid, c_lo]
    lhs_q = jnp.concatenate([qn.astype(BF16)] + aug, axis=1)
    lhs_k = jnp.concatenate([kn.astype(BF16)] + aug, axis=1)
    qp_ref[...] = (jnp.dot(lhs_q, plq_ref[...], preferred_element_type=F32) + cq_ref[...]).astype(BF16)
    kp_ref[...] = (jnp.dot(lhs_k, plk_ref[...], preferred_element_type=F32) + ck_ref[...]).astype(BF16)
    vp_ref[...] = (jnp.dot(v.astype(BF16), plv_ref[...], preferred_element_type=F32)
                   + cv_ref[...]).astype(BF16)


def _fox_constants(tm):
    hp = N_HEADS * LANES
    eexp = np.zeros((LANES, 2 * HALF), np.float32)
    esum = np.zeros((HALF, LANES), np.float32)
    e8 = np.zeros((LANES, HALF), np.float32)
    plq = np.zeros((HALF + 3 * LANES, hp), np.float32)
    plk = np.zeros((HALF + 3 * LANES, hp), np.float32)
    plv = np.zeros((HALF, hp), np.float32)
    cq = np.zeros((1, hp), np.float32)
    ck = np.zeros((1, hp), np.float32)
    cv = np.zeros((1, hp), np.float32)
    for h in range(N_HEADS):
        sl = slice(h * HEAD_DIM, (h + 1) * HEAD_DIM)
        eexp[8 + h, sl] = 1.0
        eexp[16 + h, HALF + h * HEAD_DIM:HALF + (h + 1) * HEAD_DIM] = 1.0
        esum[sl, h] = 1.0
        e8[h, sl] = 1.0
        base = h * LANES
        for d in range(HEAD_DIM):
            plq[h * HEAD_DIM + d, base + d] = 1.0
            plk[h * HEAD_DIM + d, base + d] = 1.0
            plv[h * HEAD_DIM + d, base + d] = 1.0
        for j in range(3):
            plq[HALF + j * LANES + h, base + HEAD_DIM + j] = 1.0
            plk[HALF + j * LANES + h, base + HEAD_DIM + 3 + j] = -1.0
            cq[0, base + HEAD_DIM + 3 + j] = 1.0
            ck[0, base + HEAD_DIM + j] = 1.0
        cv[0, base + HEAD_DIM] = 1.0
    tri = np.tril(np.ones((tm, tm), np.float32))
    bf = lambda a: jnp.asarray(a, BF16)
    return dict(tri=bf(tri), eexp=bf(eexp), esum=bf(esum), e8=bf(e8),
                plq=bf(plq), plk=bf(plk), plv=bf(plv), cq=jnp.asarray(cq), ck=jnp.asarray(ck),
                cv=jnp.asarray(cv))


def _foxprep(zm, zs, forget_b, qn_g, kn_g, T):
    N = zm.shape[0]
    tm = min(256, T)
    tps = T // tm
    cst = _fox_constants(tm)
    hp = N_HEADS * LANES
    fb = jnp.zeros((1, LANES), F32).at[0, :N_HEADS].set(forget_b)
    full = lambda a: pl.BlockSpec(a.shape, lambda i: (0, 0))
    consts = [fb, qn_g.reshape(1, HALF), kn_g.reshape(1, HALF), cst["tri"], cst["eexp"], cst["esum"],
              cst["e8"], cst["plq"], cst["plk"], cst["plv"], cst["cq"], cst["ck"], cst["cv"]]
    return pl.pallas_call(
        functools.partial(_foxprep_kernel, tiles_per_seq=tps),
        grid=(N // tm,),
        in_specs=[pl.BlockSpec((tm, HALF), lambda i: (i, 0)),
                  pl.BlockSpec((tm, HALF), lambda i: (i, 1)),
                  pl.BlockSpec((tm, HALF), lambda i: (i, 2)),
                  pl.BlockSpec((tm, LANES), lambda i: (i, 0))] + [full(a) for a in consts],
        out_specs=[pl.BlockSpec((tm, hp), lambda i: (i, 0))] * 3,
        out_shape=[jax.ShapeDtypeStruct((N, hp), BF16)] * 3,
        scratch_shapes=[pltpu.VMEM((8, HALF), F32), pltpu.VMEM((8, HALF), F32),
                        pltpu.VMEM((8, LANES), F32)],
        compiler_params=_cparams(("arbitrary",)),
        name="foxprep",
    )(zm, zm, zm, zs, *consts)


def _attn_kernel(q_ref, k_ref, v_ref, g_ref, ong_ref, o_ref, vt_ref, *, tq):
    i = pl.program_id(2)
    n_kv = vt_ref.shape[1]
    heads = range(ATTN_HEADS)
    lanes = [slice(hh * LANES, (hh + 1) * LANES) for hh in heads]

    @pl.when(i == 0)
    def _():
        for hh in heads:
            for c in range(n_kv):
                vt_ref[hh, c] = v_ref[c * tq:(c + 1) * tq, lanes[hh]].T

    key = lax.broadcasted_iota(I32, (tq, tq), 0)
    qry = lax.broadcasted_iota(I32, (tq, tq), 1)
    causal = key <= qry
    qs = [q_ref[:, lanes[hh]] for hh in heads]

    def step(j, carry, masked):
        m, acc = carry
        start = pl.multiple_of(j * tq, tq)
        s = [lax.dot_general(k_ref[pl.ds(start, tq), lanes[hh]], qs[hh], (((1,), (1,)), ((), ())),
                             preferred_element_type=F32) for hh in heads]
        m_out, acc_out = [], []
        for hh in heads:
            sh = jnp.where(causal, s[hh], -jnp.inf) if masked else s[hh]
            m_new = jnp.maximum(m[hh], jnp.max(sh, axis=0, keepdims=True))
            p = jnp.exp2(sh - m_new)
            acc_out.append(jnp.exp2(m[hh] - m_new) * acc[hh]
                           + jnp.dot(vt_ref[hh, j], p.astype(BF16), preferred_element_type=F32))
            m_out.append(m_new)
        return tuple(m_out), tuple(acc_out)

    init = (tuple(jnp.full((1, tq), -jnp.inf, F32) for _ in heads),
            tuple(jnp.zeros((LANES, tq), F32) for _ in heads))
    carry = lax.fori_loop(0, i, functools.partial(step, masked=False), init)
    _, acc = step(i, carry, True)
    chan = lax.broadcasted_iota(I32, (LANES, tq), 0)
    chan_w = jnp.where(chan < HEAD_DIM, 1.0 / HEAD_DIM, jnp.where(chan == HEAD_DIM, RMS_EPS, 0.0))
    lane = lax.broadcasted_iota(I32, (tq, LANES), 1)
    outs = []
    for hh in heads:
        t = jnp.sum(acc[hh] * acc[hh] * chan_w, axis=0, keepdims=True)
        outs.append((acc[hh] * lax.rsqrt(t)).T)
    o = jnp.concatenate([jnp.where(lane < HEAD_DIM, outs[2 * p], pltpu.roll(outs[2 * p + 1], HEAD_DIM, 1))
                         for p in range(ATTN_HEADS // 2)], axis=1)
    y = o * ong_ref[...] * _sigmoid(g_ref[...].astype(F32))
    o_ref[...] = y.astype(BF16)


def _attention(qp, kp, vp, zm, on_g, B, T):
    N = qp.shape[0]
    tq = min(ATTN_BLOCK, T)
    nq = T // tq
    groups = N_HEADS // ATTN_HEADS
    wp = ATTN_HEADS * LANES
    wo = ATTN_HEADS * HEAD_DIM
    g_col0 = 3 * HALF // wo
    return pl.pallas_call(
        functools.partial(_attn_kernel, tq=tq),
        grid=(B, groups, nq),
        in_specs=[pl.BlockSpec((tq, wp), lambda b, p, i: (b * nq + i, p)),
                  pl.BlockSpec((T, wp), lambda b, p, i: (b, p)),
                  pl.BlockSpec((T, wp), lambda b, p, i: (b, p)),
                  pl.BlockSpec((tq, wo), lambda b, p, i: (b * nq + i, g_col0 + p)),
                  pl.BlockSpec((None, 1, wo), lambda b, p, i: (p, 0, 0))],
        out_specs=pl.BlockSpec((tq, wo), lambda b, p, i: (b * nq + i, p)),
        out_shape=jax.ShapeDtypeStruct((N, HALF), BF16),
        scratch_shapes=[pltpu.VMEM((ATTN_HEADS, nq, LANES, tq), BF16)],
        compiler_params=_cparams(("arbitrary", "arbitrary", "arbitrary")),
        name="attn",
    )(qp, kp, vp, zm, on_g.reshape(groups, 1, wo))


def _rwkv_kernel(r_ref, k_ref, v_ref, zs_ref, w0_ref, dup_ref, a0_ref, iup_ref, gup_ref, kk_ref,
                 ka_ref, rk_ref, lng_ref, lnb_ref, tri_ref, o_ref, state, ybuf, *, chunk, nb):
    C = chunk
    c_idx = pl.program_id(1)

    @pl.when(c_idx == 0)
    def _():
        state[...] = jnp.zeros_like(state)

    R = nb * C
    r = r_ref[...].reshape(R, HALF).astype(F32)
    k = k_ref[...].reshape(R, HALF).astype(F32)
    v = v_ref[...].reshape(R, HALF).astype(F32)
    zs = zs_ref[...].reshape(R, Z_SMALL)
    wd = zs[:, LANES:2 * LANES]
    ad = zs[:, 2 * LANES:3 * LANES]
    gd = zs[:, 3 * LANES:5 * LANES]
    wl = w0_ref[...] + _mm(jnp.tanh(wd), dup_ref[...])
    lw = -jnp.exp(-_softplus(-wl) - 0.5)
    a = _sigmoid(a0_ref[...] + _mm(ad, iup_ref[...]))
    g = _mm(_sigmoid(gd), gup_ref[...])
    kk = k * kk_ref[...]
    k2 = k * (1.0 + (a - 1.0) * ka_ref[...])
    cl = _mm_split(tri_ref[...], lw)
    cl_end = jnp.concatenate(
        [jnp.broadcast_to(cl[(bi + 1) * C - 1:(bi + 1) * C, :], (C, HALF)) for bi in range(nb)], axis=0)
    e_neg = jnp.exp(-cl)
    e_tail = jnp.exp(cl_end - cl)
    pre = dict(rt=r * jnp.exp(cl), at=-kk * jnp.exp(cl - lw), kh=k2 * e_neg, bh=kk * a * e_neg,
               kb=k2 * e_tail, bb=kk * a * e_tail, v=v, kk=kk, pend=jnp.exp(cl_end),
               rkr=r * k2 * rk_ref[...], g=g)

    C2 = 2 * C
    row = lax.broadcasted_iota(I32, (C2, C2), 0)
    col = lax.broadcasted_iota(I32, (C2, C2), 1)
    lower = (col & (C - 1)) <= (row & (C - 1))
    strict = (col & (C - 1)) < (row & (C - 1))
    eye = row == col
    head0 = lax.broadcasted_iota(I32, (C, LANES), 1) < HEAD_DIM
    n_sq = int(np.log2(C)) - 1
    units = [(bi, p) for bi in range(nb) for p in range(N_HEADS // 2)]

    def part(name, bi, p):
        return pre[name][bi * C:(bi + 1) * C, p * LANES:(p + 1) * LANES]

    def stack(x):
        return jnp.concatenate([jnp.where(head0, x, 0.0), jnp.where(head0, 0.0, x)], axis=0)

    def unstack(x):
        return x[0:C] + x[C:C2]

    def head_sum(x):
        s0 = jnp.sum(jnp.where(head0, x, 0.0), axis=1, keepdims=True)
        s1 = jnp.sum(jnp.where(head0, 0.0, x), axis=1, keepdims=True)
        return jnp.where(head0, s0, s1)

    ops = []
    for bi, p in units:
        kk_p = part("kk", bi, p)
        inv = 1.0 / jnp.maximum(jnp.sqrt(head_sum(kk_p * kk_p)), 1e-12)
        ops.append(dict(rt=stack(part("rt", bi, p)), at=stack(part("at", bi, p) * inv),
                        kh=stack(part("kh", bi, p)).astype(BF16), bh=stack(part("bh", bi, p) * inv).astype(BF16),
                        kb=stack(part("kb", bi, p)).astype(BF16), bb=stack(part("bb", bi, p) * inv).astype(BF16),
                        v=stack(part("v", bi, p)).astype(BF16)))
    ra = [jnp.concatenate([o["rt"], o["at"]], axis=0).astype(BF16) for o in ops]
    g1 = [_mm_nt(x, o["kh"]) for x, o in zip(ra, ops)]
    g2 = [_mm_nt(x, o["bh"]) for x, o in zip(ra, ops)]
    a_rb = [jnp.where(lower, t[0:C2], 0.0).astype(BF16) for t in g2]
    pw = [jnp.where(strict, t[C2:2 * C2], 0.0) for t in g2]
    av = [_mm(jnp.concatenate([jnp.where(lower, t[0:C2], 0.0), jnp.where(strict, t[C2:2 * C2], 0.0)], axis=0),
              o["v"]) for t, o in zip(g1, ops)]
    xs = [jnp.concatenate([o["at"], t[C2:2 * C2]], axis=1) for o, t in zip(ops, av)]
    for level in range(n_sq + 1):
        pb = [p.astype(BF16) for p in pw]
        xs = [x + _mm(p, x) for p, x in zip(pb, xs)]
        if level < n_sq:
            pw = [jnp.dot(p, p, preferred_element_type=F32) for p in pb]
    xb = [x.astype(BF16) for x in xs]
    rb = [jnp.dot(p, x, preferred_element_type=F32) for p, x in zip(a_rb, xb)]
    bx = [_mm_tn(o["bb"], x) for o, x in zip(ops, xb)]
    kv = [_mm_tn(o["kb"], o["v"]) for o in ops]
    for u, (bi, p) in enumerate(units):
        o = ops[u]
        sl = slice(p * LANES, (p + 1) * LANES)
        r2 = unstack(o["rt"] + rb[u][:, 0:LANES])
        y0 = unstack(rb[u][:, LANES:2 * LANES] + av[u][0:C2])
        m_mat = jnp.where(eye, part("pend", bi, p)[0:1, :], 0.0) + bx[u][:, 0:LANES]
        g_mat = bx[u][:, LANES:2 * LANES] + kv[u]
        out = _mm(jnp.concatenate([r2, m_mat], axis=0), state[bi, p])
        state[bi, p] = out[C:C + LANES] + g_mat
        y = out[0:C] + y0
        cen = y - head_sum(y) * (1.0 / HEAD_DIM)
        var = head_sum(cen * cen) * (1.0 / HEAD_DIM)
        bonus = head_sum(part("rkr", bi, p)) * part("v", bi, p)
        y = cen * lax.rsqrt(var + LNX_EPS) * lng_ref[:, sl] + lnb_ref[:, sl] + bonus
        ybuf[bi, :, sl] = y * part("g", bi, p)
    o_ref[...] = ybuf[...].astype(BF16)


def _rwkv(zm, zs, p, B, T):
    N = zm.shape[0]
    C = RWKV_CHUNK
    assert 2 * C == LANES and T % C == 0, "a head pair's stacked chunk must fill one 128-row tile"
    nc = T // C
    nb = RWKV_SEQS_PER_STEP if B % RWKV_SEQS_PER_STEP == 0 else 1
    tri = np.kron(np.eye(nb, dtype=np.float32), np.tril(np.ones((C, C), np.float32)))
    pad_rows = lambda w, rows: jnp.zeros((rows, HALF), F32).at[:w.shape[0]].set(w)
    row = lambda a: a.reshape(1, HALF)
    consts = [row(p["rw_w0"]), pad_rows(p["rw_decay_up"], LANES), row(p["rw_a0"]),
              pad_rows(p["rw_iclr_up"], LANES), pad_rows(p["rw_gate_up"], 2 * LANES),
              row(p["rw_k_k"]), row(p["rw_k_a"]), row(p["rw_r_k"]), row(p["rw_lnx_g"]),
              row(p["rw_lnx_b"]), jnp.asarray(tri, BF16)]
    full = lambda a: pl.BlockSpec(a.shape, lambda b, c: (0, 0))
    rcol = 4 * HALF // HALF
    zm3 = zm.reshape(B, T, Z_MAIN)
    out = pl.pallas_call(
        functools.partial(_rwkv_kernel, chunk=C, nb=nb),
        grid=(B // nb, nc),
        in_specs=[pl.BlockSpec((nb, C, HALF), lambda b, c: (b, c, rcol)),
                  pl.BlockSpec((nb, C, HALF), lambda b, c: (b, c, rcol + 1)),
                  pl.BlockSpec((nb, C, HALF), lambda b, c: (b, c, rcol + 2)),
                  pl.BlockSpec((nb, C, Z_SMALL), lambda b, c: (b, c, 0))] + [full(a) for a in consts],
        out_specs=pl.BlockSpec((nb, C, HALF), lambda b, c: (b, c, 0)),
        out_shape=jax.ShapeDtypeStruct((B, T, HALF), BF16),
        scratch_shapes=[pltpu.VMEM((nb, N_HEADS // 2, LANES, LANES), F32), pltpu.VMEM((nb, C, HALF), F32)],
        compiler_params=_cparams(("arbitrary", "arbitrary")),
        name="rwkv",
    )(zm3, zm3, zm3, zs.reshape(B, T, Z_SMALL), *consts)
    return out.reshape(N, HALF)


def _first_index(mask, iota, big):
    return jnp.min(jnp.where(mask, iota, big), axis=0, keepdims=True)


def _outproj_kernel(x_ref, yf_ref, yr_ref, g1_ref, sh2_ref, sc2_ref, g2_ref, n2g_ref, wo_ref, rwt_ref,
                    rb_ref, swg_ref, swu_ref, swd_ref, ustrict_ref,
                    xs_ref, hp_ref, idx_ref, wts_ref, rank_ref, *, tm):
    d = (jnp.dot(yf_ref[...], wo_ref[0:HALF, :], preferred_element_type=F32)
         + jnp.dot(yr_ref[...], wo_ref[HALF:2 * HALF, :], preferred_element_type=F32))
    x2 = x_ref[...] + g1_ref[...] * d
    h = x2 * lax.rsqrt(jnp.mean(x2 * x2, axis=-1, keepdims=True) + RMS_EPS) * n2g_ref[...]
    h = h * (1.0 + sc2_ref[...]) + sh2_ref[...]
    hb = h.astype(BF16)
    act = _silu(jnp.dot(hb, swg_ref[...], preferred_element_type=F32)) * jnp.dot(
        hb, swu_ref[...], preferred_element_type=F32)
    shared = jnp.dot(act.astype(BF16), swd_ref[...], preferred_element_type=F32)
    xs_ref[...] = x2 + g2_ref[...] * shared
    half = D_MODEL // 2
    hp_ref[...] = _pack_halves(h)

    logits = lax.dot_general(rwt_ref[...], h, (((1,), (1,)), ((), ())), precision=HIGHEST,
                             preferred_element_type=F32)
    scores = _sigmoid(logits)
    sel = scores + rb_ref[:, 0:1]
    neg = -jnp.inf
    sel3 = sel.reshape(N_GROUPS, GROUP_SIZE, tm)
    io_in = lax.broadcasted_iota(I32, (N_GROUPS, GROUP_SIZE, tm), 1)
    m1 = jnp.max(sel3, axis=1, keepdims=True)
    f1 = jnp.min(jnp.where(sel3 == m1, io_in, GROUP_SIZE), axis=1, keepdims=True)
    m2 = jnp.max(jnp.where(io_in == f1, neg, sel3), axis=1, keepdims=True)
    gs = (m1 + m2).reshape(N_GROUPS, tm)
    io_g = lax.broadcasted_iota(I32, (N_GROUPS, tm), 0)
    gmask = jnp.zeros((N_GROUPS, tm), jnp.bool_)
    for _ in range(TOPK_GROUPS):
        mg = jnp.max(gs, axis=0, keepdims=True)
        fg = _first_index(gs == mg, io_g, N_GROUPS)
        pick = io_g == fg
        gmask = jnp.logical_or(gmask, pick)
        gs = jnp.where(pick, neg, gs)
    emask = jnp.broadcast_to(gmask.reshape(N_GROUPS, 1, tm), (N_GROUPS, GROUP_SIZE, tm)).reshape(N_EXPERTS, tm)
    cur = jnp.where(emask, sel, neg)
    io_e = lax.broadcasted_iota(I32, (N_EXPERTS, tm), 0)
    picks, idxs, wts = [], [], []
    for _ in range(TOP_K):
        me = jnp.max(cur, axis=0, keepdims=True)
        fe = _first_index(cur == me, io_e, N_EXPERTS)
        pick = io_e == fe
        picks.append(pick)
        idxs.append(fe)
        wts.append(jnp.sum(jnp.where(pick, scores, 0.0), axis=0, keepdims=True))
        cur = jnp.where(pick, neg, cur)
    wsum = wts[0]
    for w in wts[1:]:
        wsum = wsum + w
    zero_i = jnp.zeros((1, tm), I32)
    zero_f = jnp.zeros((1, tm), F32)
    idx_ref[...] = jnp.concatenate(idxs + [zero_i, zero_i], axis=0)
    wts_ref[...] = jnp.concatenate([w / wsum * ROUTED_SCALE for w in wts] + [zero_f, zero_f], axis=0)
    cnt = picks[0].astype(F32)
    for pk in picks[1:]:
        cnt = cnt + pk.astype(F32)
    excl = jnp.dot(cnt.astype(BF16), ustrict_ref[...], preferred_element_type=F32)
    ranks = [jnp.sum(jnp.where(pk, excl, 0.0), axis=0, keepdims=True).astype(I32) for pk in picks]
    rank_ref[...] = jnp.concatenate(ranks + [zero_i, zero_i], axis=0)


def _outproj(x2d, yf, yr, mod4, norm2_g, w_out, router_w, router_bias, swg, swu, swd, T):
    N, D = x2d.shape
    tm = min(512, T)
    tps = T // tm
    vec = lambda j: pl.BlockSpec((None, None, 1, D), lambda i: (i // tps, j, 0, 0))
    full = lambda a: pl.BlockSpec(a.shape, lambda i: (0, 0))
    ts = min(MOE_TILE, tm)
    ustrict = jnp.asarray(np.kron(np.eye(tm // ts, dtype=np.float32),
                                  np.triu(np.ones((ts, ts), np.float32), 1)), BF16)
    rb = jnp.broadcast_to(router_bias.reshape(N_EXPERTS, 1), (N_EXPERTS, LANES))
    consts = [norm2_g.reshape(1, D), w_out.astype(BF16), router_w.T, rb, swg.astype(BF16),
              swu.astype(BF16), swd.astype(BF16), ustrict]
    small = lambda dt: jax.ShapeDtypeStruct((8, N), dt)
    return pl.pallas_call(
        functools.partial(_outproj_kernel, tm=tm),
        grid=(N // tm,),
        in_specs=[pl.BlockSpec((tm, D), lambda i: (i, 0)),
                  pl.BlockSpec((tm, HALF), lambda i: (i, 0)),
                  pl.BlockSpec((tm, HALF), lambda i: (i, 0)),
                  vec(2), vec(3), vec(4), vec(5)] + [full(a) for a in consts],
        out_specs=[pl.BlockSpec((tm, D), lambda i: (i, 0)),
                   pl.BlockSpec((tm, D // 2), lambda i: (i, 0)),
                   pl.BlockSpec((8, tm), lambda i: (0, i)),
                   pl.BlockSpec((8, tm), lambda i: (0, i)),
                   pl.BlockSpec((8, tm), lambda i: (0, i))],
        out_shape=[jax.ShapeDtypeStruct((N, D), F32), jax.ShapeDtypeStruct((N, D // 2), jnp.uint32),
                   small(I32), small(F32), small(I32)],
        compiler_params=_cparams(("arbitrary",)),
        name="outproj",
    )(x2d, yf, yr, mod4, mod4, mod4, mod4, *consts)


def _for_each(count, fn, group):
    full = count // group

    def trip(t, c):
        for u in range(group):
            fn(t * group + u)
        return c

    def single(g, c):
        fn(g)
        return c

    lax.fori_loop(0, full, trip, 0)
    lax.fori_loop(full * group, count, single, 0)


def _wait_granules(count, wait_rows):
    full = count // WAIT_CHUNK

    def chunk(_, c):
        wait_rows(WAIT_CHUNK * GRANULE)
        return c

    def single(_, c):
        wait_rows(GRANULE)
        return c

    lax.fori_loop(0, full, chunk, 0)
    lax.fori_loop(full * WAIT_CHUNK, count, single, 0)


def _moe_plan(idx_t, rank_t):
    n = idx_t.shape[1]
    n_tiles = n // MOE_TILE
    experts = jnp.arange(N_EXPERTS, dtype=I32)
    hot = idx_t[:TOP_K, :, None] == experts
    cnt = jnp.sum(hot.reshape(TOP_K, n_tiles, MOE_TILE, N_EXPERTS).astype(I32), axis=(0, 2))
    gran = (cnt + GRANULE - 1) // GRANULE
    loc_end = jnp.cumsum(gran, axis=1)
    loc_off = loc_end - gran
    g_tile = loc_end[:, N_EXPERTS - 1]
    padded = (GRANULE * jnp.sum(gran, axis=0) + EXPERT_BLOCK - 1) // EXPERT_BLOCK * EXPERT_BLOCK
    pad_end = jnp.cumsum(padded)
    glob_off = (pad_end - padded)[None, :] + GRANULE * (jnp.cumsum(gran, axis=0) - gran)
    loc_tok = jnp.repeat(GRANULE * loc_off, MOE_TILE, axis=0)
    pos = rank_t[:TOP_K] + jnp.sum(jnp.where(hot, loc_tok[None], 0), axis=-1)
    pos = jnp.concatenate([pos, jnp.full((8 - TOP_K, n), -1, I32)], axis=0)
    g = jnp.arange(MOE_TILE, dtype=I32)
    e_of_g = jnp.minimum(jnp.sum((loc_end[:, None, :] <= g[None, :, None]).astype(I32), axis=-1), N_EXPERTS - 1)
    dst = jnp.sum(jnp.where(e_of_g[:, :, None] == experts,
                            glob_off[:, None, :] + GRANULE * (g[None, :, None] - loc_off[:, None, :]), 0), axis=-1)
    gtab = jnp.where(g[None, :] < g_tile[:, None], dst, 0).reshape(n_tiles, 1, MOE_TILE)
    n_blocks = -(-(n * TOP_K + GRANULE * N_EXPERTS * n_tiles) // EXPERT_BLOCK) + N_EXPERTS
    blk_start = jnp.arange(n_blocks, dtype=I32) * EXPERT_BLOCK
    blk_e = jnp.minimum(jnp.sum((pad_end[None, :] <= blk_start[:, None]).astype(I32), axis=1), N_EXPERTS - 1)
    plan = dict(pos=pos.astype(I32), gtab=gtab.astype(I32), g_tile=g_tile.astype(I32),
                pad_end=pad_end.astype(I32), padded=padded.astype(I32), blk_e=blk_e.astype(I32),
                n_used=(pad_end[N_EXPERTS - 1:] // EXPERT_BLOCK).astype(I32))
    return plan, n_blocks * EXPERT_BLOCK


def _dispatch_kernel(pend_ref, padded_ref, nu_ref, gt_ref, gtab_ref, pos_ref, hp_ref, xs_ref, zeros, sbuf,
                     sem, zsem, *, n_blocks, n_tiles):
    @pl.when(pl.program_id(0) == 0)
    def _():
        zeros[...] = jnp.zeros_like(zeros)

        def block_copy(start):
            return pltpu.make_async_copy(zeros, xs_ref.at[pl.ds(pl.multiple_of(start, EXPERT_BLOCK),
                                                                 EXPERT_BLOCK), :], zsem)

        def pad_loop(fn):
            def body(e, _):
                @pl.when(padded_ref[e] > 0)
                def _():
                    fn(block_copy(pend_ref[e] - EXPERT_BLOCK))
                return 0
            lax.fori_loop(0, N_EXPERTS, body, 0)

        def tail_loop(fn):
            def body(b, _):
                fn(block_copy(b * EXPERT_BLOCK))
                return 0
            lax.fori_loop(nu_ref[0], n_blocks, body, 0)

        pad_loop(lambda cp: cp.start())
        tail_loop(lambda cp: cp.start())
        pad_loop(lambda cp: cp.wait())
        tail_loop(lambda cp: cp.wait())

    i = pl.program_id(0)
    slot = i % 2

    def wait_granules(which, count):
        def wait_rows(rows):
            pltpu.make_async_copy(sbuf.at[which, pl.ds(0, rows), :], xs_ref.at[pl.ds(0, rows), :],
                                  sem.at[which]).wait()
        _wait_granules(count, wait_rows)

    @pl.when(i >= 2)
    def _():
        wait_granules(slot, gt_ref[jnp.maximum(i - 2, 0)])

    pos = pos_ref[...]
    r_iota = lax.broadcasted_iota(I32, (MOE_SORTED_ROWS, MOE_TILE), 0)
    perm = jnp.zeros((MOE_SORTED_ROWS, MOE_TILE), F32)
    for k in range(TOP_K):
        perm = jnp.where(r_iota == pos[k:k + 1, :], 1.0, perm)
    perm = perm.astype(BF16)
    lo, hi = _unpack_halves(hp_ref[...])
    sbuf[slot] = _pack_bf16_values(_mm(perm, lo), _mm(perm, hi))

    def issue(g):
        pltpu.make_async_copy(sbuf.at[slot, pl.ds(pl.multiple_of(g * GRANULE, GRANULE), GRANULE), :],
                              xs_ref.at[pl.ds(pl.multiple_of(gtab_ref[0, 0, g], GRANULE), GRANULE), :],
                              sem.at[slot]).start()
    _for_each(gt_ref[i], issue, DMA_UNROLL)

    @pl.when(i == n_tiles - 1)
    def _():
        wait_granules(slot, gt_ref[i])
        if n_tiles > 1:
            wait_granules(1 - slot, gt_ref[jnp.maximum(i - 1, 0)])


def _dispatch(hp, plan, n_rows):
    N, W = hp.shape
    n_tiles = N // MOE_TILE
    n_blocks = n_rows // EXPERT_BLOCK
    return pl.pallas_call(
        functools.partial(_dispatch_kernel, n_blocks=n_blocks, n_tiles=n_tiles),
        grid_spec=pltpu.PrefetchScalarGridSpec(
            num_scalar_prefetch=4, grid=(n_tiles,),
            in_specs=[pl.BlockSpec((1, 1, MOE_TILE), lambda i, *_: (i, 0, 0), memory_space=pltpu.SMEM),
                      pl.BlockSpec((8, MOE_TILE), lambda i, *_: (0, i)),
                      pl.BlockSpec((MOE_TILE, W), lambda i, *_: (i, 0))],
            out_specs=pl.BlockSpec(memory_space=pl.ANY),
            scratch_shapes=[pltpu.VMEM((EXPERT_BLOCK, W), jnp.uint32),
                            pltpu.VMEM((2, MOE_SORTED_ROWS, W), jnp.uint32),
                            pltpu.SemaphoreType.DMA((2,)), pltpu.SemaphoreType.DMA]),
        out_shape=jax.ShapeDtypeStruct((n_rows, W), jnp.uint32),
        compiler_params=_cparams(("arbitrary",)),
        name="dispatch",
    )(plan["pad_end"], plan["padded"], plan["n_used"], plan["g_tile"], plan["gtab"], plan["pos"], hp)


def _experts_kernel(be_ref, nu_ref, xs_ref, wgf_ref, wuf_ref, wdf_ref, ys_ref, wg_ref, wu_ref, wd_ref):
    i = pl.program_id(0)
    live = i < nu_ref[0]
    new_expert = jnp.logical_or(i == 0, be_ref[i] != be_ref[jnp.maximum(i - 1, 0)])

    @pl.when(jnp.logical_not(live))
    def _():
        ys_ref[...] = jnp.zeros_like(ys_ref)

    @pl.when(jnp.logical_and(live, new_expert))
    def _():
        wg_ref[...] = wgf_ref[...].astype(BF16)
        wu_ref[...] = wuf_ref[...].astype(BF16)
        wd_ref[...] = wdf_ref[...].astype(BF16)

    @pl.when(live)
    def _():
        half = D_MODEL // 2
        lo, hi = _unpack_halves(xs_ref[...])
        lo = lo.astype(BF16)
        hi = hi.astype(BF16)
        gate = (jnp.dot(lo, wg_ref[0:half, :], preferred_element_type=F32)
                + jnp.dot(hi, wg_ref[half:D_MODEL, :], preferred_element_type=F32))
        up = (jnp.dot(lo, wu_ref[0:half, :], preferred_element_type=F32)
              + jnp.dot(hi, wu_ref[half:D_MODEL, :], preferred_element_type=F32))
        y = jnp.dot((_silu(gate) * up).astype(BF16), wd_ref[...], preferred_element_type=F32)
        ys_ref[...] = _pack_halves(y)


def _experts(xs, blk_e, n_used, wg, wu, wd):
    n_rows, W = xs.shape
    n_blocks = n_rows // EXPERT_BLOCK
    row_map = lambda i, be, nu: (jnp.minimum(i, nu[0] - 1), 0)
    return pl.pallas_call(
        _experts_kernel,
        grid_spec=pltpu.PrefetchScalarGridSpec(
            num_scalar_prefetch=2, grid=(n_blocks,),
            in_specs=[pl.BlockSpec((EXPERT_BLOCK, W), row_map),
                      pl.BlockSpec((None, D_MODEL, D_EXPERT), lambda i, be, nu: (be[i], 0, 0)),
                      pl.BlockSpec((None, D_MODEL, D_EXPERT), lambda i, be, nu: (be[i], 0, 0)),
                      pl.BlockSpec((None, D_EXPERT, D_MODEL), lambda i, be, nu: (be[i], 0, 0))],
            out_specs=pl.BlockSpec((EXPERT_BLOCK, W), lambda i, be, nu: (i, 0)),
            scratch_shapes=[pltpu.VMEM((D_MODEL, D_EXPERT), BF16), pltpu.VMEM((D_MODEL, D_EXPERT), BF16),
                            pltpu.VMEM((D_EXPERT, D_MODEL), BF16)]),
        out_shape=jax.ShapeDtypeStruct((n_rows, W), jnp.uint32),
        compiler_params=_cparams(("arbitrary",)),
        name="experts",
    )(blk_e, n_used, xs, wg, wu, wd)


def _combine_kernel(gt_ref, gtab_ref, gtab_next_ref, ys_ref, pos_ref, xs_ref, wts_ref, g2_ref, fg_ref, o_ref,
                    buf, sem, *, n_tiles):
    i = pl.program_id(0)
    slot = i % 2

    def fetch(which, table_ref, count):
        def body(g):
            pltpu.make_async_copy(
                ys_ref.at[pl.ds(pl.multiple_of(table_ref[0, 0, g], GRANULE), GRANULE), :],
                buf.at[which, pl.ds(pl.multiple_of(g * GRANULE, GRANULE), GRANULE), :], sem.at[which]).start()
        _for_each(count, body, DMA_UNROLL)

    @pl.when(i == 0)
    def _():
        buf[...] = jnp.zeros_like(buf)
        fetch(0, gtab_ref, gt_ref[0])

    @pl.when(i + 1 < n_tiles)
    def _():
        fetch(1 - slot, gtab_next_ref, gt_ref[jnp.minimum(i + 1, n_tiles - 1)])

    def wait_rows(rows):
        pltpu.make_async_copy(ys_ref.at[pl.ds(0, rows), :], buf.at[slot, pl.ds(0, rows), :],
                              sem.at[slot]).wait()
    _wait_granules(gt_ref[i], wait_rows)

    pos = pos_ref[...]
    w = wts_ref[...]
    c_iota = lax.broadcasted_iota(I32, (MOE_TILE, MOE_SORTED_ROWS), 1)
    wmat = jnp.zeros((MOE_TILE, MOE_SORTED_ROWS), F32)
    for k in range(TOP_K):
        wmat = jnp.where(c_iota == pos[:, k:k + 1], w[:, k:k + 1], wmat)
    lo, hi = _unpack_halves(buf[slot])
    lo = lo.astype(BF16)
    hi = hi.astype(BF16)
    wb = wmat.astype(BF16)
    routed = jnp.concatenate([jnp.dot(wb, lo, preferred_element_type=F32),
                              jnp.dot(wb, hi, preferred_element_type=F32)], axis=1)
    x3 = xs_ref[...] + g2_ref[...] * routed
    y = x3 * lax.rsqrt(jnp.mean(x3 * x3, axis=-1, keepdims=True) + RMS_EPS) * fg_ref[...]
    o_ref[...] = y


def _combine(ys, plan, xsr, wts_t, mod4, final_g, T):
    N, D = xsr.shape
    W = ys.shape[1]
    tm = MOE_TILE
    tps = T // tm
    n_tiles = N // tm
    tile_tab = lambda off: pl.BlockSpec((1, 1, MOE_TILE), lambda i, gt: (jnp.minimum(i + off, n_tiles - 1), 0, 0),
                                        memory_space=pltpu.SMEM)
    return pl.pallas_call(
        functools.partial(_combine_kernel, n_tiles=n_tiles),
        grid_spec=pltpu.PrefetchScalarGridSpec(
            num_scalar_prefetch=1, grid=(n_tiles,),
            in_specs=[tile_tab(0), tile_tab(1),
                      pl.BlockSpec(memory_space=pl.ANY),
                      pl.BlockSpec((tm, 8), lambda i, gt: (i, 0)),
                      pl.BlockSpec((tm, D), lambda i, gt: (i, 0)),
                      pl.BlockSpec((tm, 8), lambda i, gt: (i, 0)),
                      pl.BlockSpec((None, None, 1, D), lambda i, gt: (i // tps, 5, 0, 0)),
                      pl.BlockSpec((1, D), lambda i, gt: (0, 0))],
            out_specs=pl.BlockSpec((tm, D), lambda i, gt: (i, 0)),
            scratch_shapes=[pltpu.VMEM((2, MOE_SORTED_ROWS, W), jnp.uint32), pltpu.SemaphoreType.DMA((2,))]),
        out_shape=jax.ShapeDtypeStruct((N, D), F32),
        compiler_params=_cparams(("arbitrary",)),
        name="combine",
    )(plan["g_tile"], plan["gtab"], plan["gtab"], ys, plan["pos"].T, xsr, wts_t, mod4, final_g.reshape(1, D))


def _pack_w_in(w_in):
    D = w_in.shape[0]
    fox_cols = 4 * HALF + 3 * N_HEADS
    wf = w_in[:, :fox_cols]
    wr = w_in[:, fox_cols:]
    o = 3 * HALF
    pad = lambda a, n: jnp.concatenate([a, jnp.zeros((D, n - a.shape[1]), a.dtype)], axis=1)
    parts = [wf[:, :4 * HALF], wr[:, :o],
             pad(wf[:, 4 * HALF:], LANES),
             pad(wr[:, o:o + DECAY_LORA], LANES),
             pad(wr[:, o + DECAY_LORA:o + DECAY_LORA + ICLR_LORA], LANES),
             pad(wr[:, o + DECAY_LORA + ICLR_LORA:], 2 * LANES)]
    return jnp.concatenate(parts, axis=1).astype(BF16)


def _pack_mu(mu):
    o = 3 * HALF
    pad = lambda a, n: jnp.concatenate([a, jnp.zeros((n - a.shape[0],), a.dtype)])
    small = jnp.concatenate([jnp.zeros((LANES,), mu.dtype),
                             pad(mu[o:o + DECAY_LORA], LANES),
                             pad(mu[o + DECAY_LORA:o + DECAY_LORA + ICLR_LORA], LANES),
                             pad(mu[o + DECAY_LORA + ICLR_LORA:], 2 * LANES)])
    return mu[:o].reshape(1, o), small.reshape(1, Z_SMALL)


def kernel(x, c, norm1_g, norm2_g, ada_w, ada_b, w_in, w_out, fox_qn_g, fox_kn_g, fox_on_g, fox_forget_b,
           rw_mu, rw_w0, rw_decay_up, rw_a0, rw_iclr_up, rw_gate_up, rw_k_k, rw_k_a, rw_r_k, rw_lnx_g,
           rw_lnx_b, router_w, router_bias, exp_w_gate, exp_w_up, exp_w_down, sh_w_gate, sh_w_up,
           sh_w_down, final_g):
    B, T, D = x.shape
    N = B * T
    depth = norm1_g.shape[0]
    assert depth == 1, "the combine kernel fuses the final RMSNorm, so exactly one layer is supported"
    xf = x.reshape(N, D)
    for l in range(depth):
        mod4 = _ada(c, ada_w[l], ada_b[l]).reshape(B, 6, 1, D)
        mu_big, mu_small = _pack_mu(rw_mu[l])
        zm, zs = _inproj(xf, mod4, norm1_g[l], _pack_w_in(w_in[l]), mu_big, mu_small, T)
        qp, kp, vp = _foxprep(zm, zs, fox_forget_b[l], fox_qn_g[l], fox_kn_g[l], T)
        y_fox = _attention(qp, kp, vp, zm, fox_on_g[l], B, T)
        rw = dict(rw_w0=rw_w0[l], rw_decay_up=rw_decay_up[l], rw_a0=rw_a0[l], rw_iclr_up=rw_iclr_up[l],
                  rw_gate_up=rw_gate_up[l], rw_k_k=rw_k_k[l], rw_k_a=rw_k_a[l], rw_r_k=rw_r_k[l],
                  rw_lnx_g=rw_lnx_g[l], rw_lnx_b=rw_lnx_b[l])
        y_rwkv = _rwkv(zm, zs, rw, B, T)
        xsr, hp, idx_t, wts_t, rank_t = _outproj(
            xf, y_fox, y_rwkv, mod4, norm2_g[l], w_out[l], router_w[l], router_bias[l],
            sh_w_gate[l], sh_w_up[l], sh_w_down[l], T)
        plan, n_rows = _moe_plan(idx_t, rank_t)
        xs = _dispatch(hp, plan, n_rows)
        ys = _experts(xs, plan["blk_e"], plan["n_used"], exp_w_gate[l], exp_w_up[l], exp_w_down[l])
        xf = _combine(ys, plan, xsr, wts_t.T, mod4, final_g, T)
    return xf.reshape(B, T, D)
```

```python
import functools

import jax
import jax.numpy as jnp
import numpy as np
from jax import lax
from jax.experimental import pallas as pl
from jax.experimental.pallas import tpu as pltpu

F32 = jnp.float32
BF16 = jnp.bfloat16
I32 = jnp.int32
HIGHEST = lax.Precision.HIGHEST

D_MODEL = 1024
HEAD_DIM = 64
N_HEADS = 8
HALF = N_HEADS * HEAD_DIM
RMS_EPS = 1e-6
LNX_EPS = 64e-5
LOG2E = 1.4426950408889634
DECAY_LORA = 64
ICLR_LORA = 64
GATE_LORA = 160
N_EXPERTS = 64
N_GROUPS = 8
GROUP_SIZE = N_EXPERTS // N_GROUPS
TOPK_GROUPS = 4
TOP_K = 6
D_EXPERT = 256
D_SHARED = 256
ROUTED_SCALE = 2.5
EXPERT_BLOCK = 1024

LANES = 128
Z_MAIN = 4 * HALF + 3 * HALF
Z_SMALL = 5 * LANES
VMEM_LIMIT = 56 * 1024 * 1024
ATTN_BLOCK = 512
ATTN_HEADS = 4
DMA_UNROLL = 4
WAIT_CHUNK = 16
MOE_TILE = 256
GRANULE = 8
MOE_SORTED_ROWS = MOE_TILE * TOP_K + N_EXPERTS * GRANULE
MOE_ROW_CHUNK = 256
RWKV_CHUNK = 64
RWKV_SEQS_PER_STEP = 4


def _cparams(semantics):
    return pltpu.CompilerParams(dimension_semantics=semantics, vmem_limit_bytes=VMEM_LIMIT)


def _mm(a, b):
    return jnp.dot(a.astype(BF16), b.astype(BF16), preferred_element_type=F32)


def _mm_nt(a, b):
    return lax.dot_general(a.astype(BF16), b.astype(BF16), (((1,), (1,)), ((), ())),
                           preferred_element_type=F32)


def _mm_tn(a, b):
    return lax.dot_general(a.astype(BF16), b.astype(BF16), (((0,), (0,)), ((), ())),
                           preferred_element_type=F32)


def _mm_f32(a, b):
    return jnp.dot(a, b, precision=HIGHEST, preferred_element_type=F32)


def _bf16_pieces(x, passes):
    pieces = []
    for _ in range(passes):
        piece = x.astype(BF16)
        pieces.append(piece)
        x = x - piece.astype(F32)
    return pieces


def _mm_split(m01, x, passes=3):
    return sum(jnp.dot(m01, p, preferred_element_type=F32) for p in _bf16_pieces(x, passes))


def _mm_split_r(x, m01, passes=2):
    return sum(jnp.dot(p, m01, preferred_element_type=F32) for p in _bf16_pieces(x, passes))


def _sigmoid(x):
    return 1.0 / (1.0 + jnp.exp(-x))


def _softplus(x):
    return jnp.maximum(x, 0.0) + jnp.log(1.0 + jnp.exp(-jnp.abs(x)))


def _silu(x):
    return x * _sigmoid(x)


def _pack_halves(x):
    w = x.shape[1] // 2
    bits = lambda t: lax.bitcast_convert_type(t.astype(jnp.bfloat16).astype(F32), jnp.uint32)
    return (bits(x[:, 0:w]) >> 16) | (bits(x[:, w:2 * w]) & jnp.uint32(0xFFFF0000))


def _pack_bf16_values(lo, hi):
    return (lax.bitcast_convert_type(lo, jnp.uint32) >> 16) | lax.bitcast_convert_type(hi, jnp.uint32)


def _unpack_halves(p):
    lo = lax.bitcast_convert_type(p << 16, F32)
    hi = lax.bitcast_convert_type(p & jnp.uint32(0xFFFF0000), F32)
    return lo, hi


def _shift_rows(z, carry_ref, first):
    rows = z.shape[0]
    prev_row = jnp.where(first, 0.0, carry_ref[0:1, :])
    prev = pltpu.roll(z, 1, 0)
    row0 = lax.broadcasted_iota(I32, (rows, 1), 0) == 0
    prev = jnp.where(row0, prev_row, prev)
    carry_ref[0:1, :] = z[rows - 1:rows, :]
    return prev


def _ada_kernel(c_ref, w_ref, b_ref, o_ref):
    o_ref[...] = _mm_f32(_silu(c_ref[...]), w_ref[...]) + b_ref[...]


def _ada(c, ada_w, ada_b):
    B, D = c.shape
    n_out = ada_w.shape[1]
    tn = 512
    return pl.pallas_call(
        _ada_kernel,
        grid=(n_out // tn,),
        in_specs=[pl.BlockSpec((B, D), lambda j: (0, 0)),
                  pl.BlockSpec((D, tn), lambda j: (0, j)),
                  pl.BlockSpec((1, tn), lambda j: (0, j))],
        out_specs=pl.BlockSpec((B, tn), lambda j: (0, j)),
        out_shape=jax.ShapeDtypeStruct((B, n_out), F32),
        compiler_params=_cparams(("arbitrary",)),
        name="ada",
    )(c, ada_w, ada_b.reshape(1, n_out))


def _inproj_kernel(x_ref, g_ref, sh_ref, sc_ref, w_ref, mub_ref, mus_ref, zm_ref, zs_ref,
                   carry_b, carry_s, *, tiles_per_seq):
    first = (pl.program_id(0) % tiles_per_seq) == 0
    x = x_ref[...]
    h = x * lax.rsqrt(jnp.mean(x * x, axis=-1, keepdims=True) + RMS_EPS) * g_ref[...]
    hb = (h * (1.0 + sc_ref[...]) + sh_ref[...]).astype(BF16)
    nf = 4 * HALF
    zm_ref[:, 0:nf] = jnp.dot(hb, w_ref[:, 0:nf], preferred_element_type=F32).astype(BF16)
    zr = jnp.dot(hb, w_ref[:, nf:Z_MAIN], preferred_element_type=F32)
    zr = zr + mub_ref[...] * (_shift_rows(zr, carry_b, first) - zr)
    zm_ref[:, nf:Z_MAIN] = zr.astype(BF16)
    zs = jnp.dot(hb, w_ref[:, Z_MAIN:Z_MAIN + Z_SMALL], preferred_element_type=F32)
    zs_ref[...] = zs + mus_ref[...] * (_shift_rows(zs, carry_s, first) - zs)


def _inproj(x2d, mod4, norm_g, w_all, mu_big, mu_small, T):
    N, D = x2d.shape
    tm = min(512, T)
    tps = T // tm
    vec = lambda j: pl.BlockSpec((None, None, 1, D), lambda i: (i // tps, j, 0, 0))
    return pl.pallas_call(
        functools.partial(_inproj_kernel, tiles_per_seq=tps),
        grid=(N // tm,),
        in_specs=[pl.BlockSpec((tm, D), lambda i: (i, 0)),
                  pl.BlockSpec((1, D), lambda i: (0, 0)),
                  vec(0), vec(1),
                  pl.BlockSpec((D, Z_MAIN + Z_SMALL), lambda i: (0, 0)),
                  pl.BlockSpec((1, 3 * HALF), lambda i: (0, 0)),
                  pl.BlockSpec((1, Z_SMALL), lambda i: (0, 0))],
        out_specs=[pl.BlockSpec((tm, Z_MAIN), lambda i: (i, 0)),
                   pl.BlockSpec((tm, Z_SMALL), lambda i: (i, 0))],
        out_shape=[jax.ShapeDtypeStruct((N, Z_MAIN), BF16),
                   jax.ShapeDtypeStruct((N, Z_SMALL), F32)],
        scratch_shapes=[pltpu.VMEM((8, 3 * HALF), F32), pltpu.VMEM((8, Z_SMALL), F32)],
        compiler_params=_cparams(("arbitrary",)),
        name="inproj",
    )(x2d, norm_g.reshape(1, D), mod4, mod4, w_all, mu_big, mu_small)


def _foxprep_kernel(q_ref, k_ref, v_ref, zs_ref, fb_ref, qg_ref, kg_ref, tri_ref, eexp_ref,
                    esum_ref, e8_ref, plq_ref, plk_ref, plv_ref, cq_ref, ck_ref, cv_ref,
                    qp_ref, kp_ref, vp_ref, carry_k, carry_v, carry_c, *, tiles_per_seq):
    first = (pl.program_id(0) % tiles_per_seq) == 0
    zs = zs_ref[...]
    logf = -_softplus(-(zs + fb_ref[...]))
    cum = _mm_split(tri_ref[...], logf) + jnp.where(first, 0.0, carry_c[0:1, :])
    carry_c[0:1, :] = cum[cum.shape[0] - 1:, :]
    cum2 = cum * LOG2E
    c_hi = cum2.astype(BF16)
    r1 = cum2 - c_hi.astype(F32)
    c_mid = r1.astype(BF16)
    c_lo = (r1 - c_mid.astype(F32)).astype(BF16)
    a_full = _mm_split_r(_sigmoid(zs), eexp_ref[...])
    k = k_ref[...].astype(F32)
    v = v_ref[...].astype(F32)
    a_k = a_full[:, 0:HALF]
    a_v = a_full[:, HALF:2 * HALF]
    k = a_k * _shift_rows(k, carry_k, first) + (1.0 - a_k) * k
    v = a_v * _shift_rows(v, carry_v, first) + (1.0 - a_v) * v
    q = q_ref[...].astype(F32)

    def head_rms(t, gain):
        ms = _mm_split_r(t * t, esum_ref[...]) * (1.0 / HEAD_DIM)
        inv = _mm_split_r(lax.rsqrt(ms + RMS_EPS), e8_ref[...])
        return t * inv * gain

    qn = head_rms(q, qg_ref[...]) * (HEAD_DIM ** -0.5 * LOG2E)
    kn = head_rms(k, kg_ref[...])
    aug = [c_hi, c_mid, c_lo]
    lhs_q = jnp.concatenate([qn.astype(BF16)] + aug, axis=1)
    lhs_k = jnp.concatenate([kn.astype(BF16)] + aug, axis=1)
    qp_ref[...] = (jnp.dot(lhs_q, plq_ref[...], preferred_element_type=F32) + cq_ref[...]).astype(BF16)
    kp_ref[...] = (jnp.dot(lhs_k, plk_ref[...], preferred_element_type=F32) + ck_ref[...]).astype(BF16)
    vp_ref[...] = (jnp.dot(v.astype(BF16), plv_ref[...], preferred_element_type=F32)
                   + cv_ref[...]).astype(BF16)


def _fox_constants(tm):
    hp = N_HEADS * LANES
    eexp = np.zeros((LANES, 2 * HALF), np.float32)
    esum = np.zeros((HALF, LANES), np.float32)
    e8 = np.zeros((LANES, HALF), np.float32)
    plq = np.zeros((HALF + 3 * LANES, hp), np.float32)
    plk = np.zeros((HALF + 3 * LANES, hp), np.float32)
    plv = np.zeros((HALF, hp), np.float32)
    cq = np.zeros((1, hp), np.float32)
    ck = np.zeros((1, hp), np.float32)
    cv = np.zeros((1, hp), np.float32)
    for h in range(N_HEADS):
        sl = slice(h * HEAD_DIM, (h + 1) * HEAD_DIM)
        eexp[8 + h, sl] = 1.0
        eexp[16 + h, HALF + h * HEAD_DIM:HALF + (h + 1) * HEAD_DIM] = 1.0
        esum[sl, h] = 1.0
        e8[h, sl] = 1.0
        base = h * LANES
        for d in range(HEAD_DIM):
            plq[h * HEAD_DIM + d, base + d] = 1.0
            plk[h * HEAD_DIM + d, base + d] = 1.0
            plv[h * HEAD_DIM + d, base + d] = 1.0
        for j in range(3):
            plq[HALF + j * LANES + h, base + HEAD_DIM + j] = 1.0
            plk[HALF + j * LANES + h, base + HEAD_DIM + 3 + j] = -1.0
            cq[0, base + HEAD_DIM + 3 + j] = 1.0
            ck[0, base + HEAD_DIM + j] = 1.0
        cv[0, base + HEAD_DIM] = 1.0
    tri = np.tril(np.ones((tm, tm), np.float32))
    bf = lambda a: jnp.asarray(a, BF16)
    return dict(tri=bf(tri), eexp=bf(eexp), esum=bf(esum), e8=bf(e8),
                plq=bf(plq), plk=bf(plk), plv=bf(plv), cq=jnp.asarray(cq), ck=jnp.asarray(ck),
                cv=jnp.asarray(cv))


def _foxprep(zm, zs, forget_b, qn_g, kn_g, T):
    N = zm.shape[0]
    tm = min(256, T)
    tps = T // tm
    cst = _fox_constants(tm)
    hp = N_HEADS * LANES
    fb = jnp.zeros((1, LANES), F32).at[0, :N_HEADS].set(forget_b)
    full = lambda a: pl.BlockSpec(a.shape, lambda i: (0, 0))
    consts = [fb, qn_g.reshape(1, HALF), kn_g.reshape(1, HALF), cst["tri"], cst["eexp"], cst["esum"],
              cst["e8"], cst["plq"], cst["plk"], cst["plv"], cst["cq"], cst["ck"], cst["cv"]]
    return pl.pallas_call(
        functools.partial(_foxprep_kernel, tiles_per_seq=tps),
        grid=(N // tm,),
        in_specs=[pl.BlockSpec((tm, HALF), lambda i: (i, 0)),
                  pl.BlockSpec((tm, HALF), lambda i: (i, 1)),
                  pl.BlockSpec((tm, HALF), lambda i: (i, 2)),
                  pl.BlockSpec((tm, LANES), lambda i: (i, 0))] + [full(a) for a in consts],
        out_specs=[pl.BlockSpec((tm, hp), lambda i: (i, 0))] * 3,
        out_shape=[jax.ShapeDtypeStruct((N, hp), BF16)] * 3,
        scratch_shapes=[pltpu.VMEM((8, HALF), F32), pltpu.VMEM((8, HALF), F32),
                        pltpu.VMEM((8, LANES), F32)],
        compiler_params=_cparams(("arbitrary",)),
        name="foxprep",
    )(zm, zm, zm, zs, *consts)


def _attn_kernel(q_ref, k_ref, v_ref, g_ref, ong_ref, o_ref, vt_ref, *, tq):
    i = pl.program_id(2)
    n_kv = vt_ref.shape[1]
    heads = range(ATTN_HEADS)
    lanes = [slice(hh * LANES, (hh + 1) * LANES) for hh in heads]

    @pl.when(i == 0)
    def _():
        for hh in heads:
            for c in range(n_kv):
                vt_ref[hh, c] = v_ref[c * tq:(c + 1) * tq, lanes[hh]].T

    key = lax.broadcasted_iota(I32, (tq, tq), 0)
    qry = lax.broadcasted_iota(I32, (tq, tq), 1)
    causal = key <= qry
    qs = [q_ref[:, lanes[hh]] for hh in heads]

    def step(j, carry, masked):
        m, acc = carry
        start = pl.multiple_of(j * tq, tq)
        s = [lax.dot_general(k_ref[pl.ds(start, tq), lanes[hh]], qs[hh], (((1,), (1,)), ((), ())),
                             preferred_element_type=F32) for hh in heads]
        m_out, acc_out = [], []
        for hh in heads:
            sh = jnp.where(causal, s[hh], -jnp.inf) if masked else s[hh]
            m_new = jnp.maximum(m[hh], jnp.max(sh, axis=0, keepdims=True))
            p = jnp.exp2(sh - m_new)
            acc_out.append(jnp.exp2(m[hh] - m_new) * acc[hh]
                           + jnp.dot(vt_ref[hh, j], p.astype(BF16), preferred_element_type=F32))
            m_out.append(m_new)
        return tuple(m_out), tuple(acc_out)

    init = (tuple(jnp.full((1, tq), -jnp.inf, F32) for _ in heads),
            tuple(jnp.zeros((LANES, tq), F32) for _ in heads))
    carry = lax.fori_loop(0, i, functools.partial(step, masked=False), init)
    _, acc = step(i, carry, True)
    chan = lax.broadcasted_iota(I32, (LANES, tq), 0)
    chan_w = jnp.where(chan < HEAD_DIM, 1.0 / HEAD_DIM, jnp.where(chan == HEAD_DIM, RMS_EPS, 0.0))
    lane = lax.broadcasted_iota(I32, (tq, LANES), 1)
    outs = []
    for hh in heads:
        t = jnp.sum(acc[hh] * acc[hh] * chan_w, axis=0, keepdims=True)
        outs.append((acc[hh] * lax.rsqrt(t)).T)
    o = jnp.concatenate([jnp.where(lane < HEAD_DIM, outs[2 * p], pltpu.roll(outs[2 * p + 1], HEAD_DIM, 1))
                         for p in range(ATTN_HEADS // 2)], axis=1)
    y = o * ong_ref[...] * _sigmoid(g_ref[...].astype(F32))
    o_ref[...] = y.astype(BF16)


def _attention(qp, kp, vp, zm, on_g, B, T):
    N = qp.shape[0]
    tq = min(ATTN_BLOCK, T)
    nq = T // tq
    groups = N_HEADS // ATTN_HEADS
    wp = ATTN_HEADS * LANES
    wo = ATTN_HEADS * HEAD_DIM
    g_col0 = 3 * HALF // wo
    return pl.pallas_call(
        functools.partial(_attn_kernel, tq=tq),
        grid=(B, groups, nq),
        in_specs=[pl.BlockSpec((tq, wp), lambda b, p, i: (b * nq + i, p)),
                  pl.BlockSpec((T, wp), lambda b, p, i: (b, p)),
                  pl.BlockSpec((T, wp), lambda b, p, i: (b, p)),
                  pl.BlockSpec((tq, wo), lambda b, p, i: (b * nq + i, g_col0 + p)),
                  pl.BlockSpec((None, 1, wo), lambda b, p, i: (p, 0, 0))],
        out_specs=pl.BlockSpec((tq, wo), lambda b, p, i: (b * nq + i, p)),
        out_shape=jax.ShapeDtypeStruct((N, HALF), BF16),
        scratch_shapes=[pltpu.VMEM((ATTN_HEADS, nq, LANES, tq), BF16)],
        compiler_params=_cparams(("arbitrary", "arbitrary", "arbitrary")),
        name="attn",
    )(qp, kp, vp, zm, on_g.reshape(groups, 1, wo))


def _rwkv_kernel(r_ref, k_ref, v_ref, zs_ref, w0_ref, dup_ref, a0_ref, iup_ref, gup_ref, kk_ref,
                 ka_ref, rk_ref, lng_ref, lnb_ref, tri_ref, o_ref, state, ybuf, *, chunk, nb):
    C = chunk
    c_idx = pl.program_id(1)

    @pl.when(c_idx == 0)
    def _():
        state[...] = jnp.zeros_like(state)

    R = nb * C
    r = r_ref[...].reshape(R, HALF).astype(F32)
    k = k_ref[...].reshape(R, HALF).astype(F32)
    v = v_ref[...].reshape(R, HALF).astype(F32)
    zs = zs_ref[...].reshape(R, Z_SMALL)
    wd = zs[:, LANES:2 * LANES]
    ad = zs[:, 2 * LANES:3 * LANES]
    gd = zs[:, 3 * LANES:5 * LANES]
    wl = w0_ref[...] + _mm(jnp.tanh(wd), dup_ref[...])
    lw = -jnp.exp(-_softplus(-wl) - 0.5)
    a = _sigmoid(a0_ref[...] + _mm(ad, iup_ref[...]))
    g = _mm(_sigmoid(gd), gup_ref[...])
    kk = k * kk_ref[...]
    k2 = k * (1.0 + (a - 1.0) * ka_ref[...])
    cl = _mm_split(tri_ref[...], lw)
    cl_end = jnp.concatenate(
        [jnp.broadcast_to(cl[(bi + 1) * C - 1:(bi + 1) * C, :], (C, HALF)) for bi in range(nb)], axis=0)
    e_neg = jnp.exp(-cl)
    e_tail = jnp.exp(cl_end - cl)
    pre = dict(rt=r * jnp.exp(cl), at=-kk * jnp.exp(cl - lw), kh=k2 * e_neg, bh=kk * a * e_neg,
               kb=k2 * e_tail, bb=kk * a * e_tail, v=v, kk=kk, pend=jnp.exp(cl_end),
               rkr=r * k2 * rk_ref[...], g=g)

    C2 = 2 * C
    row = lax.broadcasted_iota(I32, (C2, C2), 0)
    col = lax.broadcasted_iota(I32, (C2, C2), 1)
    lower = (col & (C - 1)) <= (row & (C - 1))
    strict = (col & (C - 1)) < (row & (C - 1))
    eye = row == col
    head0 = lax.broadcasted_iota(I32, (C, LANES), 1) < HEAD_DIM
    n_sq = int(np.log2(C)) - 1
    units = [(bi, p) for bi in range(nb) for p in range(N_HEADS // 2)]

    def part(name, bi, p):
        return pre[name][bi * C:(bi + 1) * C, p * LANES:(p + 1) * LANES]

    def stack(x):
        return jnp.concatenate([jnp.where(head0, x, 0.0), jnp.where(head0, 0.0, x)], axis=0)

    def unstack(x):
        return x[0:C] + x[C:C2]

    def head_sum(x):
        s0 = jnp.sum(jnp.where(head0, x, 0.0), axis=1, keepdims=True)
        s1 = jnp.sum(jnp.where(head0, 0.0, x), axis=1, keepdims=True)
        return jnp.where(head0, s0, s1)

    ops = []
    for bi, p in units:
        kk_p = part("kk", bi, p)
        inv = 1.0 / jnp.maximum(jnp.sqrt(head_sum(kk_p * kk_p)), 1e-12)
        ops.append(dict(rt=stack(part("rt", bi, p)), at=stack(part("at", bi, p) * inv),
                        kh=stack(part("kh", bi, p)).astype(BF16), bh=stack(part("bh", bi, p) * inv).astype(BF16),
                        kb=stack(part("kb", bi, p)).astype(BF16), bb=stack(part("bb", bi, p) * inv).astype(BF16),
                        v=stack(part("v", bi, p)).astype(BF16)))
    ra = [jnp.concatenate([o["rt"], o["at"]], axis=0).astype(BF16) for o in ops]
    gg = [_mm_nt(x, jnp.concatenate([o["kh"], o["bh"]], axis=0)) for x, o in zip(ra, ops)]
    g1 = [t[:, 0:C2] for t in gg]
    g2 = [t[:, C2:2 * C2] for t in gg]
    a_rb = [jnp.where(lower, t[0:C2], 0.0).astype(BF16) for t in g2]
    pw = [jnp.where(strict, t[C2:2 * C2], 0.0) for t in g2]
    av = [_mm(jnp.concatenate([jnp.where(lower, t[0:C2], 0.0), jnp.where(strict, t[C2:2 * C2], 0.0)], axis=0),
              o["v"]) for t, o in zip(g1, ops)]
    xs = [jnp.concatenate([o["at"], t[C2:2 * C2]], axis=1) for o, t in zip(ops, av)]
    for level in range(n_sq + 1):
        pb = [p.astype(BF16) for p in pw]
        xs = [x + _mm(p, x) for p, x in zip(pb, xs)]
        if level < n_sq:
            pw = [jnp.dot(p, p, preferred_element_type=F32) for p in pb]
    xb = [x.astype(BF16) for x in xs]
    rb = [jnp.dot(p, x, preferred_element_type=F32) for p, x in zip(a_rb, xb)]
    bx = [_mm_tn(o["bb"], x) for o, x in zip(ops, xb)]
    kv = [_mm_tn(o["kb"], o["v"]) for o in ops]
    for u, (bi, p) in enumerate(units):
        o = ops[u]
        sl = slice(p * LANES, (p + 1) * LANES)
        r2 = unstack(o["rt"] + rb[u][:, 0:LANES])
        y0 = unstack(rb[u][:, LANES:2 * LANES] + av[u][0:C2])
        m_mat = jnp.where(eye, part("pend", bi, p)[0:1, :], 0.0) + bx[u][:, 0:LANES]
        g_mat = bx[u][:, LANES:2 * LANES] + kv[u]
        out = _mm(jnp.concatenate([r2, m_mat], axis=0), state[bi, p])
        state[bi, p] = out[C:C + LANES] + g_mat
        y = out[0:C] + y0
        cen = y - head_sum(y) * (1.0 / HEAD_DIM)
        var = head_sum(cen * cen) * (1.0 / HEAD_DIM)
        bonus = head_sum(part("rkr", bi, p)) * part("v", bi, p)
        y = cen * lax.rsqrt(var + LNX_EPS) * lng_ref[:, sl] + lnb_ref[:, sl] + bonus
        ybuf[bi, :, sl] = y * part("g", bi, p)
    o_ref[...] = ybuf[...].astype(BF16)


def _rwkv(zm, zs, p, B, T):
    N = zm.shape[0]
    C = RWKV_CHUNK
    assert 2 * C == LANES and T % C == 0, "a head pair's stacked chunk must fill one 128-row tile"
    nc = T // C
    nb = RWKV_SEQS_PER_STEP if B % RWKV_SEQS_PER_STEP == 0 else 1
    tri = np.kron(np.eye(nb, dtype=np.float32), np.tril(np.ones((C, C), np.float32)))
    pad_rows = lambda w, rows: jnp.zeros((rows, HALF), F32).at[:w.shape[0]].set(w)
    row = lambda a: a.reshape(1, HALF)
    consts = [row(p["rw_w0"]), pad_rows(p["rw_decay_up"], LANES), row(p["rw_a0"]),
              pad_rows(p["rw_iclr_up"], LANES), pad_rows(p["rw_gate_up"], 2 * LANES),
              row(p["rw_k_k"]), row(p["rw_k_a"]), row(p["rw_r_k"]), row(p["rw_lnx_g"]),
              row(p["rw_lnx_b"]), jnp.asarray(tri, BF16)]
    full = lambda a: pl.BlockSpec(a.shape, lambda b, c: (0, 0))
    rcol = 4 * HALF // HALF
    zm3 = zm.reshape(B, T, Z_MAIN)
    out = pl.pallas_call(
        functools.partial(_rwkv_kernel, chunk=C, nb=nb),
        grid=(B // nb, nc),
        in_specs=[pl.BlockSpec((nb, C, HALF), lambda b, c: (b, c, rcol)),
                  pl.BlockSpec((nb, C, HALF), lambda b, c: (b, c, rcol + 1)),
                  pl.BlockSpec((nb, C, HALF), lambda b, c: (b, c, rcol + 2)),
                  pl.BlockSpec((nb, C, Z_SMALL), lambda b, c: (b, c, 0))] + [full(a) for a in consts],
        out_specs=pl.BlockSpec((nb, C, HALF), lambda b, c: (b, c, 0)),
        out_shape=jax.ShapeDtypeStruct((B, T, HALF), BF16),
        scratch_shapes=[pltpu.VMEM((nb, N_HEADS // 2, LANES, LANES), F32), pltpu.VMEM((nb, C, HALF), F32)],
        compiler_params=_cparams(("arbitrary", "arbitrary")),
        name="rwkv",
    )(zm3, zm3, zm3, zs.reshape(B, T, Z_SMALL), *consts)
    return out.reshape(N, HALF)


def _first_index(mask, iota, big):
    return jnp.min(jnp.where(mask, iota, big), axis=0, keepdims=True)


def _outproj_kernel(x_ref, yf_ref, yr_ref, g1_ref, sh2_ref, sc2_ref, g2_ref, n2g_ref, wo_ref, rwt_ref,
                    rb_ref, swg_ref, swu_ref, swd_ref, ustrict_ref,
                    xs_ref, hp_ref, idx_ref, wts_ref, rank_ref, *, tm):
    d = (jnp.dot(yf_ref[...], wo_ref[0:HALF, :], preferred_element_type=F32)
         + jnp.dot(yr_ref[...], wo_ref[HALF:2 * HALF, :], preferred_element_type=F32))
    x2 = x_ref[...] + g1_ref[...] * d
    h = x2 * lax.rsqrt(jnp.mean(x2 * x2, axis=-1, keepdims=True) + RMS_EPS) * n2g_ref[...]
    h = h * (1.0 + sc2_ref[...]) + sh2_ref[...]
    hb = h.astype(BF16)
    act = _silu(jnp.dot(hb, swg_ref[...], preferred_element_type=F32)) * jnp.dot(
        hb, swu_ref[...], preferred_element_type=F32)
    shared = jnp.dot(act.astype(BF16), swd_ref[...], preferred_element_type=F32)
    xs_ref[...] = x2 + g2_ref[...] * shared
    half = D_MODEL // 2
    hp_ref[...] = _pack_halves(h)

    logits = lax.dot_general(rwt_ref[...], h, (((1,), (1,)), ((), ())), precision=HIGHEST,
                             preferred_element_type=F32)
    scores = _sigmoid(logits)
    sel = scores + rb_ref[:, 0:1]
    neg = -jnp.inf
    sel3 = sel.reshape(N_GROUPS, GROUP_SIZE, tm)
    io_in = lax.broadcasted_iota(I32, (N_GROUPS, GROUP_SIZE, tm), 1)
    m1 = jnp.max(sel3, axis=1, keepdims=True)
    f1 = jnp.min(jnp.where(sel3 == m1, io_in, GROUP_SIZE), axis=1, keepdims=True)
    m2 = jnp.max(jnp.where(io_in == f1, neg, sel3), axis=1, keepdims=True)
    gs = (m1 + m2).reshape(N_GROUPS, tm)
    io_g = lax.broadcasted_iota(I32, (N_GROUPS, tm), 0)
    gmask = jnp.zeros((N_GROUPS, tm), jnp.bool_)
    for _ in range(TOPK_GROUPS):
        mg = jnp.max(gs, axis=0, keepdims=True)
        fg = _first_index(gs == mg, io_g, N_GROUPS)
        pick = io_g == fg
        gmask = jnp.logical_or(gmask, pick)
        gs = jnp.where(pick, neg, gs)
    emask = jnp.broadcast_to(gmask.reshape(N_GROUPS, 1, tm), (N_GROUPS, GROUP_SIZE, tm)).reshape(N_EXPERTS, tm)
    cur = jnp.where(emask, sel, neg)
    io_e = lax.broadcasted_iota(I32, (N_EXPERTS, tm), 0)
    picks, idxs, wts = [], [], []
    for _ in range(TOP_K):
        me = jnp.max(cur, axis=0, keepdims=True)
        fe = _first_index(cur == me, io_e, N_EXPERTS)
        pick = io_e == fe
        picks.append(pick)
        idxs.append(fe)
        wts.append(jnp.sum(jnp.where(pick, scores, 0.0), axis=0, keepdims=True))
        cur = jnp.where(pick, neg, cur)
    wsum = wts[0]
    for w in wts[1:]:
        wsum = wsum + w
    zero_i = jnp.zeros((1, tm), I32)
    zero_f = jnp.zeros((1, tm), F32)
    idx_ref[...] = jnp.concatenate(idxs + [zero_i, zero_i], axis=0)
    wts_ref[...] = jnp.concatenate([w / wsum * ROUTED_SCALE for w in wts] + [zero_f, zero_f], axis=0)
    cnt = picks[0].astype(F32)
    for pk in picks[1:]:
        cnt = cnt + pk.astype(F32)
    excl = jnp.dot(cnt.astype(BF16), ustrict_ref[...], preferred_element_type=F32)
    ranks = [jnp.sum(jnp.where(pk, excl, 0.0), axis=0, keepdims=True).astype(I32) for pk in picks]
    rank_ref[...] = jnp.concatenate(ranks + [zero_i, zero_i], axis=0)


def _outproj(x2d, yf, yr, mod4, norm2_g, w_out, router_w, router_bias, swg, swu, swd, T):
    N, D = x2d.shape
    tm = min(512, T)
    tps = T // tm
    vec = lambda j: pl.BlockSpec((None, None, 1, D), lambda i: (i // tps, j, 0, 0))
    full = lambda a: pl.BlockSpec(a.shape, lambda i: (0, 0))
    ts = min(MOE_TILE, tm)
    ustrict = jnp.asarray(np.kron(np.eye(tm // ts, dtype=np.float32),
                                  np.triu(np.ones((ts, ts), np.float32), 1)), BF16)
    rb = jnp.broadcast_to(router_bias.reshape(N_EXPERTS, 1), (N_EXPERTS, LANES))
    consts = [norm2_g.reshape(1, D), w_out.astype(BF16), router_w.T, rb, swg.astype(BF16),
              swu.astype(BF16), swd.astype(BF16), ustrict]
    small = lambda dt: jax.ShapeDtypeStruct((8, N), dt)
    return pl.pallas_call(
        functools.partial(_outproj_kernel, tm=tm),
        grid=(N // tm,),
        in_specs=[pl.BlockSpec((tm, D), lambda i: (i, 0)),
                  pl.BlockSpec((tm, HALF), lambda i: (i, 0)),
                  pl.BlockSpec((tm, HALF), lambda i: (i, 0)),
                  vec(2), vec(3), vec(4), vec(5)] + [full(a) for a in consts],
        out_specs=[pl.BlockSpec((tm, D), lambda i: (i, 0)),
                   pl.BlockSpec((tm, D // 2), lambda i: (i, 0)),
                   pl.BlockSpec((8, tm), lambda i: (0, i)),
                   pl.BlockSpec((8, tm), lambda i: (0, i)),
                   pl.BlockSpec((8, tm), lambda i: (0, i))],
        out_shape=[jax.ShapeDtypeStruct((N, D), F32), jax.ShapeDtypeStruct((N, D // 2), jnp.uint32),
                   small(I32), small(F32), small(I32)],
        compiler_params=_cparams(("arbitrary",)),
        name="outproj",
    )(x2d, yf, yr, mod4, mod4, mod4, mod4, *consts)


def _for_each(count, fn, group):
    full = count // group

    def trip(t, c):
        for u in range(group):
            fn(t * group + u)
        return c

    def single(g, c):
        fn(g)
        return c

    lax.fori_loop(0, full, trip, 0)
    lax.fori_loop(full * group, count, single, 0)


def _wait_granules(count, wait_rows):
    full = count // WAIT_CHUNK

    def chunk(_, c):
        wait_rows(WAIT_CHUNK * GRANULE)
        return c

    def single(_, c):
        wait_rows(GRANULE)
        return c

    lax.fori_loop(0, full, chunk, 0)
    lax.fori_loop(full * WAIT_CHUNK, count, single, 0)


def _moe_plan(idx_t, rank_t):
    n = idx_t.shape[1]
    n_tiles = n // MOE_TILE
    experts = jnp.arange(N_EXPERTS, dtype=I32)
    hot = idx_t[:TOP_K, :, None] == experts
    cnt = jnp.sum(hot.reshape(TOP_K, n_tiles, MOE_TILE, N_EXPERTS).astype(I32), axis=(0, 2))
    gran = (cnt + GRANULE - 1) // GRANULE
    loc_end = jnp.cumsum(gran, axis=1)
    loc_off = loc_end - gran
    g_tile = loc_end[:, N_EXPERTS - 1]
    padded = (GRANULE * jnp.sum(gran, axis=0) + EXPERT_BLOCK - 1) // EXPERT_BLOCK * EXPERT_BLOCK
    pad_end = jnp.cumsum(padded)
    glob_off = (pad_end - padded)[None, :] + GRANULE * (jnp.cumsum(gran, axis=0) - gran)
    loc_tok = jnp.repeat(GRANULE * loc_off, MOE_TILE, axis=0)
    pos = rank_t[:TOP_K] + jnp.sum(jnp.where(hot, loc_tok[None], 0), axis=-1)
    pos = jnp.concatenate([pos, jnp.full((8 - TOP_K, n), -1, I32)], axis=0)
    g = jnp.arange(MOE_TILE, dtype=I32)
    e_of_g = jnp.minimum(jnp.sum((loc_end[:, None, :] <= g[None, :, None]).astype(I32), axis=-1), N_EXPERTS - 1)
    dst = jnp.sum(jnp.where(e_of_g[:, :, None] == experts,
                            glob_off[:, None, :] + GRANULE * (g[None, :, None] - loc_off[:, None, :]), 0), axis=-1)
    gtab = jnp.where(g[None, :] < g_tile[:, None], dst, 0).reshape(n_tiles, 1, MOE_TILE)
    n_blocks = -(-(n * TOP_K + GRANULE * N_EXPERTS * n_tiles) // EXPERT_BLOCK) + N_EXPERTS
    blk_start = jnp.arange(n_blocks, dtype=I32) * EXPERT_BLOCK
    blk_e = jnp.minimum(jnp.sum((pad_end[None, :] <= blk_start[:, None]).astype(I32), axis=1), N_EXPERTS - 1)
    plan = dict(pos=pos.astype(I32), gtab=gtab.astype(I32), g_tile=g_tile.astype(I32),
                pad_end=pad_end.astype(I32), padded=padded.astype(I32), blk_e=blk_e.astype(I32),
                n_used=(pad_end[N_EXPERTS - 1:] // EXPERT_BLOCK).astype(I32))
    return plan, n_blocks * EXPERT_BLOCK


def _dispatch_kernel(pend_ref, padded_ref, nu_ref, gt_ref, gtab_ref, pos_ref, hp_ref, xs_ref, zeros, sbuf,
                     sem, zsem, *, n_blocks, n_tiles):
    @pl.when(pl.program_id(0) == 0)
    def _():
        zeros[...] = jnp.zeros_like(zeros)

        def block_copy(start):
            return pltpu.make_async_copy(zeros, xs_ref.at[pl.ds(pl.multiple_of(start, EXPERT_BLOCK),
                                                                 EXPERT_BLOCK), :], zsem)

        def pad_loop(fn):
            def body(e, _):
                @pl.when(padded_ref[e] > 0)
                def _():
                    fn(block_copy(pend_ref[e] - EXPERT_BLOCK))
                return 0
            lax.fori_loop(0, N_EXPERTS, body, 0)

        def tail_loop(fn):
            def body(b, _):
                fn(block_copy(b * EXPERT_BLOCK))
                return 0
            lax.fori_loop(nu_ref[0], n_blocks, body, 0)

        pad_loop(lambda cp: cp.start())
        tail_loop(lambda cp: cp.start())
        pad_loop(lambda cp: cp.wait())
        tail_loop(lambda cp: cp.wait())

    i = pl.program_id(0)
    slot = i % 2

    def wait_granules(which, count):
        def wait_rows(rows):
            pltpu.make_async_copy(sbuf.at[which, pl.ds(0, rows), :], xs_ref.at[pl.ds(0, rows), :],
                                  sem.at[which]).wait()
        _wait_granules(count, wait_rows)

    @pl.when(i >= 2)
    def _():
        wait_granules(slot, gt_ref[jnp.maximum(i - 2, 0)])

    pos = pos_ref[...]
    lo, hi = _unpack_halves(hp_ref[...])
    lo = lo.astype(BF16)
    hi = hi.astype(BF16)
    r_iota = lax.broadcasted_iota(I32, (MOE_ROW_CHUNK, MOE_TILE), 0).astype(F32).astype(BF16)
    one = jnp.ones((), BF16)
    for c in range(MOE_SORTED_ROWS // MOE_ROW_CHUNK):
        rel = pos - c * MOE_ROW_CHUNK
        rel = jnp.where(jnp.logical_and(rel >= 0, rel < MOE_ROW_CHUNK), rel, -1).astype(F32)
        perm = jnp.zeros((MOE_ROW_CHUNK, MOE_TILE), BF16)
        for k in range(TOP_K):
            perm = jnp.where(r_iota == rel[k:k + 1, :].astype(BF16), one, perm)
        sbuf[slot, c * MOE_ROW_CHUNK:(c + 1) * MOE_ROW_CHUNK, :] = _pack_bf16_values(
            jnp.dot(perm, lo, preferred_element_type=F32), jnp.dot(perm, hi, preferred_element_type=F32))

    def issue(g):
        pltpu.make_async_copy(sbuf.at[slot, pl.ds(pl.multiple_of(g * GRANULE, GRANULE), GRANULE), :],
                              xs_ref.at[pl.ds(pl.multiple_of(gtab_ref[0, 0, g], GRANULE), GRANULE), :],
                              sem.at[slot]).start()
    _for_each(gt_ref[i], issue, DMA_UNROLL)

    @pl.when(i == n_tiles - 1)
    def _():
        wait_granules(slot, gt_ref[i])
        if n_tiles > 1:
            wait_granules(1 - slot, gt_ref[jnp.maximum(i - 1, 0)])


def _dispatch(hp, plan, n_rows):
    N, W = hp.shape
    n_tiles = N // MOE_TILE
    n_blocks = n_rows // EXPERT_BLOCK
    return pl.pallas_call(
        functools.partial(_dispatch_kernel, n_blocks=n_blocks, n_tiles=n_tiles),
        grid_spec=pltpu.PrefetchScalarGridSpec(
            num_scalar_prefetch=4, grid=(n_tiles,),
            in_specs=[pl.BlockSpec((1, 1, MOE_TILE), lambda i, *_: (i, 0, 0), memory_space=pltpu.SMEM),
                      pl.BlockSpec((8, MOE_TILE), lambda i, *_: (0, i)),
                      pl.BlockSpec((MOE_TILE, W), lambda i, *_: (i, 0))],
            out_specs=pl.BlockSpec(memory_space=pl.ANY),
            scratch_shapes=[pltpu.VMEM((EXPERT_BLOCK, W), jnp.uint32),
                            pltpu.VMEM((2, MOE_SORTED_ROWS, W), jnp.uint32),
                            pltpu.SemaphoreType.DMA((2,)), pltpu.SemaphoreType.DMA]),
        out_shape=jax.ShapeDtypeStruct((n_rows, W), jnp.uint32),
        compiler_params=_cparams(("arbitrary",)),
        name="dispatch",
    )(plan["pad_end"], plan["padded"], plan["n_used"], plan["g_tile"], plan["gtab"], plan["pos"], hp)


def _experts_kernel(be_ref, nu_ref, xs_ref, wgf_ref, wuf_ref, wdf_ref, ys_ref, wg_ref, wu_ref, wd_ref):
    i = pl.program_id(0)
    live = i < nu_ref[0]
    new_expert = jnp.logical_or(i == 0, be_ref[i] != be_ref[jnp.maximum(i - 1, 0)])

    @pl.when(jnp.logical_not(live))
    def _():
        ys_ref[...] = jnp.zeros_like(ys_ref)

    @pl.when(jnp.logical_and(live, new_expert))
    def _():
        wg_ref[...] = wgf_ref[...].astype(BF16)
        wu_ref[...] = wuf_ref[...].astype(BF16)
        wd_ref[...] = wdf_ref[...].astype(BF16)

    @pl.when(live)
    def _():
        half = D_MODEL // 2
        lo, hi = _unpack_halves(xs_ref[...])
        lo = lo.astype(BF16)
        hi = hi.astype(BF16)
        gate = (jnp.dot(lo, wg_ref[0:half, :], preferred_element_type=F32)
                + jnp.dot(hi, wg_ref[half:D_MODEL, :], preferred_element_type=F32))
        up = (jnp.dot(lo, wu_ref[0:half, :], preferred_element_type=F32)
              + jnp.dot(hi, wu_ref[half:D_MODEL, :], preferred_element_type=F32))
        y = jnp.dot((_silu(gate) * up).astype(BF16), wd_ref[...], preferred_element_type=F32)
        ys_ref[...] = _pack_halves(y)


def _experts(xs, blk_e, n_used, wg, wu, wd):
    n_rows, W = xs.shape
    n_blocks = n_rows // EXPERT_BLOCK
    row_map = lambda i, be, nu: (jnp.minimum(i, nu[0] - 1), 0)
    return pl.pallas_call(
        _experts_kernel,
        grid_spec=pltpu.PrefetchScalarGridSpec(
            num_scalar_prefetch=2, grid=(n_blocks,),
            in_specs=[pl.BlockSpec((EXPERT_BLOCK, W), row_map),
                      pl.BlockSpec((None, D_MODEL, D_EXPERT), lambda i, be, nu: (be[i], 0, 0)),
                      pl.BlockSpec((None, D_MODEL, D_EXPERT), lambda i, be, nu: (be[i], 0, 0)),
                      pl.BlockSpec((None, D_EXPERT, D_MODEL), lambda i, be, nu: (be[i], 0, 0))],
            out_specs=pl.BlockSpec((EXPERT_BLOCK, W), lambda i, be, nu: (i, 0)),
            scratch_shapes=[pltpu.VMEM((D_MODEL, D_EXPERT), BF16), pltpu.VMEM((D_MODEL, D_EXPERT), BF16),
                            pltpu.VMEM((D_EXPERT, D_MODEL), BF16)]),
        out_shape=jax.ShapeDtypeStruct((n_rows, W), jnp.uint32),
        compiler_params=_cparams(("arbitrary",)),
        name="experts",
    )(blk_e, n_used, xs, wg, wu, wd)


def _combine_kernel(gt_ref, gtab_ref, gtab_next_ref, ys_ref, pos_ref, xs_ref, wts_ref, g2_ref, fg_ref, o_ref,
                    buf, sem, *, n_tiles):
    i = pl.program_id(0)
    slot = i % 2

    def fetch(which, table_ref, count):
        def body(g):
            pltpu.make_async_copy(
                ys_ref.at[pl.ds(pl.multiple_of(table_ref[0, 0, g], GRANULE), GRANULE), :],
                buf.at[which, pl.ds(pl.multiple_of(g * GRANULE, GRANULE), GRANULE), :], sem.at[which]).start()
        _for_each(count, body, DMA_UNROLL)

    @pl.when(i == 0)
    def _():
        buf[...] = jnp.zeros_like(buf)
        fetch(0, gtab_ref, gt_ref[0])

    @pl.when(i + 1 < n_tiles)
    def _():
        fetch(1 - slot, gtab_next_ref, gt_ref[jnp.minimum(i + 1, n_tiles - 1)])

    def wait_rows(rows):
        pltpu.make_async_copy(ys_ref.at[pl.ds(0, rows), :], buf.at[slot, pl.ds(0, rows), :],
                              sem.at[slot]).wait()
    _wait_granules(gt_ref[i], wait_rows)

    pos = pos_ref[...]
    w = wts_ref[...]
    c_iota = lax.broadcasted_iota(I32, (MOE_TILE, MOE_ROW_CHUNK), 1).astype(F32).astype(BF16)
    half = D_MODEL // 2
    r_lo = jnp.zeros((MOE_TILE, half), F32)
    r_hi = jnp.zeros((MOE_TILE, half), F32)
    for c in range(MOE_SORTED_ROWS // MOE_ROW_CHUNK):
        rel = pos - c * MOE_ROW_CHUNK
        rel = jnp.where(jnp.logical_and(rel >= 0, rel < MOE_ROW_CHUNK), rel, -1).astype(F32)
        wb = jnp.zeros((MOE_TILE, MOE_ROW_CHUNK), BF16)
        for k in range(TOP_K):
            wb = jnp.where(c_iota == rel[:, k:k + 1].astype(BF16), w[:, k:k + 1].astype(BF16), wb)
        lo, hi = _unpack_halves(buf[slot, c * MOE_ROW_CHUNK:(c + 1) * MOE_ROW_CHUNK, :])
        r_lo = r_lo + jnp.dot(wb, lo.astype(BF16), preferred_element_type=F32)
        r_hi = r_hi + jnp.dot(wb, hi.astype(BF16), preferred_element_type=F32)
    x3 = xs_ref[...] + g2_ref[...] * jnp.concatenate([r_lo, r_hi], axis=1)
    y = x3 * lax.rsqrt(jnp.mean(x3 * x3, axis=-1, keepdims=True) + RMS_EPS) * fg_ref[...]
    o_ref[...] = y


def _combine(ys, plan, xsr, wts_t, mod4, final_g, T):
    N, D = xsr.shape
    W = ys.shape[1]
    tm = MOE_TILE
    tps = T // tm
    n_tiles = N // tm
    tile_tab = lambda off: pl.BlockSpec((1, 1, MOE_TILE), lambda i, gt: (jnp.minimum(i + off, n_tiles - 1), 0, 0),
                                        memory_space=pltpu.SMEM)
    return pl.pallas_call(
        functools.partial(_combine_kernel, n_tiles=n_tiles),
        grid_spec=pltpu.PrefetchScalarGridSpec(
            num_scalar_prefetch=1, grid=(n_tiles,),
            in_specs=[tile_tab(0), tile_tab(1),
                      pl.BlockSpec(memory_space=pl.ANY),
                      pl.BlockSpec((tm, 8), lambda i, gt: (i, 0)),
                      pl.BlockSpec((tm, D), lambda i, gt: (i, 0)),
                      pl.BlockSpec((tm, 8), lambda i, gt: (i, 0)),
                      pl.BlockSpec((None, None, 1, D), lambda i, gt: (i // tps, 5, 0, 0)),
                      pl.BlockSpec((1, D), lambda i, gt: (0, 0))],
            out_specs=pl.BlockSpec((tm, D), lambda i, gt: (i, 0)),
            scratch_shapes=[pltpu.VMEM((2, MOE_SORTED_ROWS, W), jnp.uint32), pltpu.SemaphoreType.DMA((2,))]),
        out_shape=jax.ShapeDtypeStruct((N, D), F32),
        compiler_params=_cparams(("arbitrary",)),
        name="combine",
    )(plan["g_tile"], plan["gtab"], plan["gtab"], ys, plan["pos"].T, xsr, wts_t, mod4, final_g.reshape(1, D))


def _pack_w_in(w_in):
    D = w_in.shape[0]
    fox_cols = 4 * HALF + 3 * N_HEADS
    wf = w_in[:, :fox_cols]
    wr = w_in[:, fox_cols:]
    o = 3 * HALF
    pad = lambda a, n: jnp.concatenate([a, jnp.zeros((D, n - a.shape[1]), a.dtype)], axis=1)
    parts = [wf[:, :4 * HALF], wr[:, :o],
             pad(wf[:, 4 * HALF:], LANES),
             pad(wr[:, o:o + DECAY_LORA], LANES),
             pad(wr[:, o + DECAY_LORA:o + DECAY_LORA + ICLR_LORA], LANES),
             pad(wr[:, o + DECAY_LORA + ICLR_LORA:], 2 * LANES)]
    return jnp.concatenate(parts, axis=1).astype(BF16)


def _pack_mu(mu):
    o = 3 * HALF
    pad = lambda a, n: jnp.concatenate([a, jnp.zeros((n - a.shape[0],), a.dtype)])
    small = jnp.concatenate([jnp.zeros((LANES,), mu.dtype),
                             pad(mu[o:o + DECAY_LORA], LANES),
                             pad(mu[o + DECAY_LORA:o + DECAY_LORA + ICLR_LORA], LANES),
                             pad(mu[o + DECAY_LORA + ICLR_LORA:], 2 * LANES)])
    return mu[:o].reshape(1, o), small.reshape(1, Z_SMALL)


def kernel(x, c, norm1_g, norm2_g, ada_w, ada_b, w_in, w_out, fox_qn_g, fox_kn_g, fox_on_g, fox_forget_b,
           rw_mu, rw_w0, rw_decay_up, rw_a0, rw_iclr_up, rw_gate_up, rw_k_k, rw_k_a, rw_r_k, rw_lnx_g,
           rw_lnx_b, router_w, router_bias, exp_w_gate, exp_w_up, exp_w_down, sh_w_gate, sh_w_up,
           sh_w_down, final_g):
    B, T, D = x.shape
    N = B * T
    depth = norm1_g.shape[0]
    assert depth == 1, "the combine kernel fuses the final RMSNorm, so exactly one layer is supported"
    xf = x.reshape(N, D)
    for l in range(depth):
        mod4 = _ada(c, ada_w[l], ada_b[l]).reshape(B, 6, 1, D)
        mu_big, mu_small = _pack_mu(rw_mu[l])
        zm, zs = _inproj(xf, mod4, norm1_g[l], _pack_w_in(w_in[l]), mu_big, mu_small, T)
        qp, kp, vp = _foxprep(zm, zs, fox_forget_b[l], fox_qn_g[l], fox_kn_g[l], T)
        y_fox = _attention(qp, kp, vp, zm, fox_on_g[l], B, T)
        rw = dict(rw_w0=rw_w0[l], rw_decay_up=rw_decay_up[l], rw_a0=rw_a0[l], rw_iclr_up=rw_iclr_up[l],
                  rw_gate_up=rw_gate_up[l], rw_k_k=rw_k_k[l], rw_k_a=rw_k_a[l], rw_r_k=rw_r_k[l],
                  rw_lnx_g=rw_lnx_g[l], rw_lnx_b=rw_lnx_b[l])
        y_rwkv = _rwkv(zm, zs, rw, B, T)
        xsr, hp, idx_t, wts_t, rank_t = _outproj(
            xf, y_fox, y_rwkv, mod4, norm2_g[l], w_out[l], router_w[l], router_bias[l],
            sh_w_gate[l], sh_w_up[l], sh_w_down[l], T)
        plan, n_rows = _moe_plan(idx_t, rank_t)
        xs = _dispatch(hp, plan, n_rows)
        ys = _experts(xs, plan["blk_e"], plan["n_used"], exp_w_gate[l], exp_w_up[l], exp_w_down[l])
        xf = _combine(ys, plan, xsr, wts_t.T, mod4, final_g, T)
    return xf.reshape(B, T, D)
```

```python
import functools

import jax
import jax.numpy as jnp
import numpy as np
from jax import lax
from jax.experimental import pallas as pl
from jax.experimental.pallas import tpu as pltpu

F32 = jnp.float32
BF16 = jnp.bfloat16
I32 = jnp.int32
HIGHEST = lax.Precision.HIGHEST

D_MODEL = 1024
HEAD_DIM = 64
N_HEADS = 8
HALF = N_HEADS * HEAD_DIM
RMS_EPS = 1e-6
LNX_EPS = 64e-5
LOG2E = 1.4426950408889634
DECAY_LORA = 64
ICLR_LORA = 64
GATE_LORA = 160
N_EXPERTS = 64
N_GROUPS = 8
GROUP_SIZE = N_EXPERTS // N_GROUPS
TOPK_GROUPS = 4
TOP_K = 6
D_EXPERT = 256
D_SHARED = 256
ROUTED_SCALE = 2.5
EXPERT_BLOCK = 1024

LANES = 128
Z_MAIN = 4 * HALF + 3 * HALF
Z_SMALL = 5 * LANES
VMEM_LIMIT = 56 * 1024 * 1024
ATTN_BLOCK = 512
ATTN_HEADS = 4
DMA_UNROLL = 4
WAIT_CHUNK = 16
MOE_TILE = 256
GRANULE = 8
MOE_SORTED_ROWS = MOE_TILE * TOP_K + N_EXPERTS * GRANULE
MOE_ROW_CHUNK = 256
RWKV_CHUNK = 64
RWKV_SEQS_PER_STEP = 4


def _cparams(semantics):
    return pltpu.CompilerParams(dimension_semantics=semantics, vmem_limit_bytes=VMEM_LIMIT)


def _mm(a, b):
    return jnp.dot(a.astype(BF16), b.astype(BF16), preferred_element_type=F32)


def _mm_nt(a, b):
    return lax.dot_general(a.astype(BF16), b.astype(BF16), (((1,), (1,)), ((), ())),
                           preferred_element_type=F32)


def _mm_tn(a, b):
    return lax.dot_general(a.astype(BF16), b.astype(BF16), (((0,), (0,)), ((), ())),
                           preferred_element_type=F32)


def _mm_f32(a, b):
    return jnp.dot(a, b, precision=HIGHEST, preferred_element_type=F32)


def _bf16_pieces(x, passes):
    pieces = []
    for _ in range(passes):
        piece = x.astype(BF16)
        pieces.append(piece)
        x = x - piece.astype(F32)
    return pieces


def _mm_split(m01, x, passes=3):
    return sum(jnp.dot(m01, p, preferred_element_type=F32) for p in _bf16_pieces(x, passes))


def _mm_split_r(x, m01, passes=2):
    return sum(jnp.dot(p, m01, preferred_element_type=F32) for p in _bf16_pieces(x, passes))


def _sigmoid(x):
    return 1.0 / (1.0 + jnp.exp(-x))


def _softplus(x):
    return jnp.maximum(x, 0.0) + jnp.log(1.0 + jnp.exp(-jnp.abs(x)))


def _silu(x):
    return x * _sigmoid(x)


def _pack_halves(x):
    w = x.shape[1] // 2
    bits = lambda t: lax.bitcast_convert_type(t.astype(jnp.bfloat16).astype(F32), jnp.uint32)
    return (bits(x[:, 0:w]) >> 16) | (bits(x[:, w:2 * w]) & jnp.uint32(0xFFFF0000))


def _pack_bf16_values(lo, hi):
    return (lax.bitcast_convert_type(lo, jnp.uint32) >> 16) | lax.bitcast_convert_type(hi, jnp.uint32)


def _unpack_halves(p):
    lo = lax.bitcast_convert_type(p << 16, F32)
    hi = lax.bitcast_convert_type(p & jnp.uint32(0xFFFF0000), F32)
    return lo, hi


def _shift_rows(z, carry_ref, first):
    rows = z.shape[0]
    prev_row = jnp.where(first, 0.0, carry_ref[0:1, :])
    prev = pltpu.roll(z, 1, 0)
    row0 = lax.broadcasted_iota(I32, (rows, 1), 0) == 0
    prev = jnp.where(row0, prev_row, prev)
    carry_ref[0:1, :] = z[rows - 1:rows, :]
    return prev


def _ada_kernel(c_ref, w_ref, b_ref, o_ref):
    o_ref[...] = _mm_f32(_silu(c_ref[...]), w_ref[...]) + b_ref[...]


def _ada(c, ada_w, ada_b):
    B, D = c.shape
    n_out = ada_w.shape[1]
    tn = 512
    return pl.pallas_call(
        _ada_kernel,
        grid=(n_out // tn,),
        in_specs=[pl.BlockSpec((B, D), lambda j: (0, 0)),
                  pl.BlockSpec((D, tn), lambda j: (0, j)),
                  pl.BlockSpec((1, tn), lambda j: (0, j))],
        out_specs=pl.BlockSpec((B, tn), lambda j: (0, j)),
        out_shape=jax.ShapeDtypeStruct((B, n_out), F32),
        compiler_params=_cparams(("arbitrary",)),
        name="ada",
    )(c, ada_w, ada_b.reshape(1, n_out))


def _inproj_kernel(x_ref, g_ref, sh_ref, sc_ref, w_ref, mub_ref, mus_ref, zm_ref, zs_ref,
                   carry_b, carry_s, *, tiles_per_seq):
    first = (pl.program_id(0) % tiles_per_seq) == 0
    x = x_ref[...]
    h = x * lax.rsqrt(jnp.mean(x * x, axis=-1, keepdims=True) + RMS_EPS) * g_ref[...]
    hb = (h * (1.0 + sc_ref[...]) + sh_ref[...]).astype(BF16)
    nf = 4 * HALF
    zm_ref[:, 0:nf] = jnp.dot(hb, w_ref[:, 0:nf], preferred_element_type=F32).astype(BF16)
    zr = jnp.dot(hb, w_ref[:, nf:Z_MAIN], preferred_element_type=F32)
    zr = zr + mub_ref[...] * (_shift_rows(zr, carry_b, first) - zr)
    zm_ref[:, nf:Z_MAIN] = zr.astype(BF16)
    zs = jnp.dot(hb, w_ref[:, Z_MAIN:Z_MAIN + Z_SMALL], preferred_element_type=F32)
    zs_ref[...] = zs + mus_ref[...] * (_shift_rows(zs, carry_s, first) - zs)


def _inproj(x2d, mod4, norm_g, w_all, mu_big, mu_small, T):
    N, D = x2d.shape
    tm = min(512, T)
    tps = T // tm
    vec = lambda j: pl.BlockSpec((None, None, 1, D), lambda i: (i // tps, j, 0, 0))
    return pl.pallas_call(
        functools.partial(_inproj_kernel, tiles_per_seq=tps),
        grid=(N // tm,),
        in_specs=[pl.BlockSpec((tm, D), lambda i: (i, 0)),
                  pl.BlockSpec((1, D), lambda i: (0, 0)),
                  vec(0), vec(1),
                  pl.BlockSpec((D, Z_MAIN + Z_SMALL), lambda i: (0, 0)),
                  pl.BlockSpec((1, 3 * HALF), lambda i: (0, 0)),
                  pl.BlockSpec((1, Z_SMALL), lambda i: (0, 0))],
        out_specs=[pl.BlockSpec((tm, Z_MAIN), lambda i: (i, 0)),
                   pl.BlockSpec((tm, Z_SMALL), lambda i: (i, 0))],
        out_shape=[jax.ShapeDtypeStruct((N, Z_MAIN), BF16),
                   jax.ShapeDtypeStruct((N, Z_SMALL), F32)],
        scratch_shapes=[pltpu.VMEM((8, 3 * HALF), F32), pltpu.VMEM((8, Z_SMALL), F32)],
        compiler_params=_cparams(("arbitrary",)),
        name="inproj",
    )(x2d, norm_g.reshape(1, D), mod4, mod4, w_all, mu_big, mu_small)


def _foxprep_kernel(q_ref, k_ref, v_ref, zs_ref, fb_ref, qg_ref, kg_ref, tri_ref, eexp_ref,
                    esum_ref, e8_ref, plq_ref, plk_ref, plv_ref, cq_ref, ck_ref, cv_ref,
                    qp_ref, kp_ref, vp_ref, carry_k, carry_v, carry_c, *, tiles_per_seq):
    first = (pl.program_id(0) % tiles_per_seq) == 0
    zs = zs_ref[...]
    logf = -_softplus(-(zs + fb_ref[...]))
    cum = _mm_split(tri_ref[...], logf) + jnp.where(first, 0.0, carry_c[0:1, :])
    carry_c[0:1, :] = cum[cum.shape[0] - 1:, :]
    cum2 = cum * LOG2E
    c_hi = cum2.astype(BF16)
    r1 = cum2 - c_hi.astype(F32)
    c_mid = r1.astype(BF16)
    c_lo = (r1 - c_mid.astype(F32)).astype(BF16)
    a_full = _mm_split_r(_sigmoid(zs), eexp_ref[...])
    k = k_ref[...].astype(F32)
    v = v_ref[...].astype(F32)
    a_k = a_full[:, 0:HALF]
    a_v = a_full[:, HALF:2 * HALF]
    k = a_k * _shift_rows(k, carry_k, first) + (1.0 - a_k) * k
    v = a_v * _shift_rows(v, carry_v, first) + (1.0 - a_v) * v
    q = q_ref[...].astype(F32)

    def head_rms(t, gain):
        ms = _mm_split_r(t * t, esum_ref[...]) * (1.0 / HEAD_DIM)
        inv = _mm_split_r(lax.rsqrt(ms + RMS_EPS), e8_ref[...])
        return t * inv * gain

    qn = head_rms(q, qg_ref[...]) * (HEAD_DIM ** -0.5 * LOG2E)
    kn = head_rms(k, kg_ref[...])
    aug = [c_hi, c_mid, c_lo]
    lhs_q = jnp.concatenate([qn.astype(BF16)] + aug, axis=1)
    lhs_k = jnp.concatenate([kn.astype(BF16)] + aug, axis=1)
    qp_ref[...] = (jnp.dot(lhs_q, plq_ref[...], preferred_element_type=F32) + cq_ref[...]).astype(BF16)
    kp_ref[...] = (jnp.dot(lhs_k, plk_ref[...], preferred_element_type=F32) + ck_ref[...]).astype(BF16)
    vp_ref[...] = (jnp.dot(v.astype(BF16), plv_ref[...], preferred_element_type=F32)
                   + cv_ref[...]).astype(BF16)


def _fox_constants(tm):
    hp = N_HEADS * LANES
    eexp = np.zeros((LANES, 2 * HALF), np.float32)
    esum = np.zeros((HALF, LANES), np.float32)
    e8 = np.zeros((LANES, HALF), np.float32)
    plq = np.zeros((HALF + 3 * LANES, hp), np.float32)
    plk = np.zeros((HALF + 3 * LANES, hp), np.float32)
    plv = np.zeros((HALF, hp), np.float32)
    cq = np.zeros((1, hp), np.float32)
    ck = np.zeros((1, hp), np.float32)
    cv = np.zeros((1, hp), np.float32)
    for h in range(N_HEADS):
        sl = slice(h * HEAD_DIM, (h + 1) * HEAD_DIM)
        eexp[8 + h, sl] = 1.0
        eexp[16 + h, HALF + h * HEAD_DIM:HALF + (h + 1) * HEAD_DIM] = 1.0
        esum[sl, h] = 1.0
        e8[h, sl] = 1.0
        base = h * LANES
        for d in range(HEAD_DIM):
            plq[h * HEAD_DIM + d, base + d] = 1.0
            plk[h * HEAD_DIM + d, base + d] = 1.0
            plv[h * HEAD_DIM + d, base + d] = 1.0
        for j in range(3):
            plq[HALF + j * LANES + h, base + HEAD_DIM + j] = 1.0
            plk[HALF + j * LANES + h, base + HEAD_DIM + 3 + j] = -1.0
            cq[0, base + HEAD_DIM + 3 + j] = 1.0
            ck[0, base + HEAD_DIM + j] = 1.0
        cv[0, base + HEAD_DIM] = 1.0
    tri = np.tril(np.ones((tm, tm), np.float32))
    bf = lambda a: jnp.asarray(a, BF16)
    return dict(tri=bf(tri), eexp=bf(eexp), esum=bf(esum), e8=bf(e8),
                plq=bf(plq), plk=bf(plk), plv=bf(plv), cq=jnp.asarray(cq), ck=jnp.asarray(ck),
                cv=jnp.asarray(cv))


def _foxprep(zm, zs, forget_b, qn_g, kn_g, T):
    N = zm.shape[0]
    tm = min(512, T)
    tps = T // tm
    cst = _fox_constants(tm)
    hp = N_HEADS * LANES
    fb = jnp.zeros((1, LANES), F32).at[0, :N_HEADS].set(forget_b)
    full = lambda a: pl.BlockSpec(a.shape, lambda i: (0, 0))
    consts = [fb, qn_g.reshape(1, HALF), kn_g.reshape(1, HALF), cst["tri"], cst["eexp"], cst["esum"],
              cst["e8"], cst["plq"], cst["plk"], cst["plv"], cst["cq"], cst["ck"], cst["cv"]]
    return pl.pallas_call(
        functools.partial(_foxprep_kernel, tiles_per_seq=tps),
        grid=(N // tm,),
        in_specs=[pl.BlockSpec((tm, HALF), lambda i: (i, 0)),
                  pl.BlockSpec((tm, HALF), lambda i: (i, 1)),
                  pl.BlockSpec((tm, HALF), lambda i: (i, 2)),
                  pl.BlockSpec((tm, LANES), lambda i: (i, 0))] + [full(a) for a in consts],
        out_specs=[pl.BlockSpec((tm, hp), lambda i: (i, 0))] * 3,
        out_shape=[jax.ShapeDtypeStruct((N, hp), BF16)] * 3,
        scratch_shapes=[pltpu.VMEM((8, HALF), F32), pltpu.VMEM((8, HALF), F32),
                        pltpu.VMEM((8, LANES), F32)],
        compiler_params=_cparams(("arbitrary",)),
        name="foxprep",
    )(zm, zm, zm, zs, *consts)


def _attn_kernel(q_ref, k_ref, v_ref, g_ref, ong_ref, o_ref, vt_ref, *, tq):
    i = pl.program_id(2)
    n_kv = vt_ref.shape[1]
    heads = range(ATTN_HEADS)
    lanes = [slice(hh * LANES, (hh + 1) * LANES) for hh in heads]

    @pl.when(i == 0)
    def _():
        for hh in heads:
            for c in range(n_kv):
                vt_ref[hh, c] = v_ref[c * tq:(c + 1) * tq, lanes[hh]].T

    key = lax.broadcasted_iota(I32, (tq, tq), 0)
    qry = lax.broadcasted_iota(I32, (tq, tq), 1)
    causal = key <= qry
    qs = [q_ref[:, lanes[hh]] for hh in heads]

    def step(j, carry, masked):
        m, acc = carry
        start = pl.multiple_of(j * tq, tq)
        s = [lax.dot_general(k_ref[pl.ds(start, tq), lanes[hh]], qs[hh], (((1,), (1,)), ((), ())),
                             preferred_element_type=F32) for hh in heads]
        m_out, acc_out = [], []
        for hh in heads:
            sh = jnp.where(causal, s[hh], -jnp.inf) if masked else s[hh]
            m_new = jnp.maximum(m[hh], jnp.max(sh, axis=0, keepdims=True))
            p = jnp.exp2(sh - m_new)
            acc_out.append(jnp.exp2(m[hh] - m_new) * acc[hh]
                           + jnp.dot(vt_ref[hh, j], p.astype(BF16), preferred_element_type=F32))
            m_out.append(m_new)
        return tuple(m_out), tuple(acc_out)

    init = (tuple(jnp.full((1, tq), -jnp.inf, F32) for _ in heads),
            tuple(jnp.zeros((LANES, tq), F32) for _ in heads))
    carry = lax.fori_loop(0, i, functools.partial(step, masked=False), init)
    _, acc = step(i, carry, True)
    chan = lax.broadcasted_iota(I32, (LANES, tq), 0)
    chan_w = jnp.where(chan < HEAD_DIM, 1.0 / HEAD_DIM, jnp.where(chan == HEAD_DIM, RMS_EPS, 0.0))
    lane = lax.broadcasted_iota(I32, (tq, LANES), 1)
    outs = []
    for hh in heads:
        t = jnp.sum(acc[hh] * acc[hh] * chan_w, axis=0, keepdims=True)
        outs.append((acc[hh] * lax.rsqrt(t)).T)
    o = jnp.concatenate([jnp.where(lane < HEAD_DIM, outs[2 * p], pltpu.roll(outs[2 * p + 1], HEAD_DIM, 1))
                         for p in range(ATTN_HEADS // 2)], axis=1)
    y = o * ong_ref[...] * _sigmoid(g_ref[...].astype(F32))
    o_ref[...] = y.astype(BF16)


def _attention(qp, kp, vp, zm, on_g, B, T):
    N = qp.shape[0]
    tq = min(ATTN_BLOCK, T)
    nq = T // tq
    groups = N_HEADS // ATTN_HEADS
    wp = ATTN_HEADS * LANES
    wo = ATTN_HEADS * HEAD_DIM
    g_col0 = 3 * HALF // wo
    return pl.pallas_call(
        functools.partial(_attn_kernel, tq=tq),
        grid=(B, groups, nq),
        in_specs=[pl.BlockSpec((tq, wp), lambda b, p, i: (b * nq + i, p)),
                  pl.BlockSpec((T, wp), lambda b, p, i: (b, p)),
                  pl.BlockSpec((T, wp), lambda b, p, i: (b, p)),
                  pl.BlockSpec((tq, wo), lambda b, p, i: (b * nq + i, g_col0 + p)),
                  pl.BlockSpec((None, 1, wo), lambda b, p, i: (p, 0, 0))],
        out_specs=pl.BlockSpec((tq, wo), lambda b, p, i: (b * nq + i, p)),
        out_shape=jax.ShapeDtypeStruct((N, HALF), BF16),
        scratch_shapes=[pltpu.VMEM((ATTN_HEADS, nq, LANES, tq), BF16)],
        compiler_params=_cparams(("arbitrary", "arbitrary", "arbitrary")),
        name="attn",
    )(qp, kp, vp, zm, on_g.reshape(groups, 1, wo))


def _rwkv_kernel(r_ref, k_ref, v_ref, zs_ref, w0_ref, dup_ref, a0_ref, iup_ref, gup_ref, kk_ref,
                 ka_ref, rk_ref, lng_ref, lnb_ref, tri_ref, o_ref, state, ybuf, *, chunk, nb):
    C = chunk
    c_idx = pl.program_id(1)

    @pl.when(c_idx == 0)
    def _():
        state[...] = jnp.zeros_like(state)

    R = nb * C
    r = r_ref[...].reshape(R, HALF).astype(F32)
    k = k_ref[...].reshape(R, HALF).astype(F32)
    v = v_ref[...].reshape(R, HALF).astype(F32)
    zs = zs_ref[...].reshape(R, Z_SMALL)
    wd = zs[:, LANES:2 * LANES]
    ad = zs[:, 2 * LANES:3 * LANES]
    gd = zs[:, 3 * LANES:5 * LANES]
    wl = w0_ref[...] + _mm(jnp.tanh(wd), dup_ref[...])
    lw = -jnp.exp(-_softplus(-wl) - 0.5)
    a = _sigmoid(a0_ref[...] + _mm(ad, iup_ref[...]))
    g = _mm(_sigmoid(gd), gup_ref[...])
    kk = k * kk_ref[...]
    k2 = k * (1.0 + (a - 1.0) * ka_ref[...])
    cl = _mm_split(tri_ref[...], lw)
    cl_end = jnp.concatenate(
        [jnp.broadcast_to(cl[(bi + 1) * C - 1:(bi + 1) * C, :], (C, HALF)) for bi in range(nb)], axis=0)
    e_neg = jnp.exp(-cl)
    e_tail = jnp.exp(cl_end - cl)
    pre = dict(rt=r * jnp.exp(cl), at=-kk * jnp.exp(cl - lw), kh=k2 * e_neg, bh=kk * a * e_neg,
               kb=k2 * e_tail, bb=kk * a * e_tail, v=v, kk=kk, pend=jnp.exp(cl_end),
               rkr=r * k2 * rk_ref[...], g=g)

    C2 = 2 * C
    row = lax.broadcasted_iota(I32, (C2, C2), 0)
    col = lax.broadcasted_iota(I32, (C2, C2), 1)
    lower = (col & (C - 1)) <= (row & (C - 1))
    strict = (col & (C - 1)) < (row & (C - 1))
    eye = row == col
    head0 = lax.broadcasted_iota(I32, (C, LANES), 1) < HEAD_DIM
    n_sq = int(np.log2(C)) - 1
    units = [(bi, p) for bi in range(nb) for p in range(N_HEADS // 2)]

    def part(name, bi, p):
        return pre[name][bi * C:(bi + 1) * C, p * LANES:(p + 1) * LANES]

    def stack(x):
        return jnp.concatenate([jnp.where(head0, x, 0.0), jnp.where(head0, 0.0, x)], axis=0)

    def unstack(x):
        return x[0:C] + x[C:C2]

    def head_sum(x):
        s0 = jnp.sum(jnp.where(head0, x, 0.0), axis=1, keepdims=True)
        s1 = jnp.sum(jnp.where(head0, 0.0, x), axis=1, keepdims=True)
        return jnp.where(head0, s0, s1)

    ops = []
    for bi, p in units:
        kk_p = part("kk", bi, p)
        inv = 1.0 / jnp.maximum(jnp.sqrt(head_sum(kk_p * kk_p)), 1e-12)
        ops.append(dict(rt=stack(part("rt", bi, p)), at=stack(part("at", bi, p) * inv),
                        kh=stack(part("kh", bi, p)).astype(BF16), bh=stack(part("bh", bi, p) * inv).astype(BF16),
                        kb=stack(part("kb", bi, p)).astype(BF16), bb=stack(part("bb", bi, p) * inv).astype(BF16),
                        v=stack(part("v", bi, p)).astype(BF16)))
    ra = [jnp.concatenate([o["rt"], o["at"]], axis=0).astype(BF16) for o in ops]
    gg = [_mm_nt(x, jnp.concatenate([o["kh"], o["bh"]], axis=0)) for x, o in zip(ra, ops)]
    g1 = [t[:, 0:C2] for t in gg]
    g2 = [t[:, C2:2 * C2] for t in gg]
    a_rb = [jnp.where(lower, t[0:C2], 0.0).astype(BF16) for t in g2]
    pw = [jnp.where(strict, t[C2:2 * C2], 0.0) for t in g2]
    av = [_mm(jnp.concatenate([jnp.where(lower, t[0:C2], 0.0).astype(BF16),
                               jnp.where(strict, t[C2:2 * C2], 0.0).astype(BF16), o["kb"].T], axis=0),
              o["v"]) for t, o in zip(g1, ops)]
    swap = lambda t: pltpu.roll(t, HEAD_DIM, 1)
    own = (row < C) == (col < HEAD_DIM)
    xs = [o["at"] + swap(t[C2:2 * C2]) for o, t in zip(ops, av)]
    for level in range(n_sq + 1):
        pb = [p.astype(BF16) for p in pw]
        xs = [x + _mm(p, x) for p, x in zip(pb, xs)]
        if level < n_sq:
            pw = [jnp.dot(p, p, preferred_element_type=F32) for p in pb]
    rbx = [_mm(jnp.concatenate([p, o["bb"].T], axis=0), x) for p, o, x in zip(a_rb, ops, xs)]
    for u, (bi, p) in enumerate(units):
        o = ops[u]
        sl = slice(p * LANES, (p + 1) * LANES)
        rb, bx, kv = rbx[u][0:C2], rbx[u][C2:2 * C2], av[u][2 * C2:3 * C2]
        r2 = unstack(o["rt"] + jnp.where(own, rb, 0.0))
        y0 = swap(unstack(jnp.where(own, 0.0, rb))) + unstack(av[u][0:C2])
        m_mat = jnp.where(eye, part("pend", bi, p)[0:1, :], 0.0) + jnp.where(own, bx, 0.0)
        g_mat = swap(jnp.where(own, 0.0, bx)) + kv
        out = _mm(jnp.concatenate([r2, m_mat], axis=0), state[bi, p])
        state[bi, p] = out[C:C + LANES] + g_mat
        y = out[0:C] + y0
        cen = y - head_sum(y) * (1.0 / HEAD_DIM)
        var = head_sum(cen * cen) * (1.0 / HEAD_DIM)
        bonus = head_sum(part("rkr", bi, p)) * part("v", bi, p)
        y = cen * lax.rsqrt(var + LNX_EPS) * lng_ref[:, sl] + lnb_ref[:, sl] + bonus
        ybuf[bi, :, sl] = y * part("g", bi, p)
    o_ref[...] = ybuf[...].astype(BF16)


def _rwkv(zm, zs, p, B, T):
    N = zm.shape[0]
    C = RWKV_CHUNK
    assert 2 * C == LANES and T % C == 0, "a head pair's stacked chunk must fill one 128-row tile"
    nc = T // C
    nb = RWKV_SEQS_PER_STEP if B % RWKV_SEQS_PER_STEP == 0 else 1
    tri = np.kron(np.eye(nb, dtype=np.float32), np.tril(np.ones((C, C), np.float32)))
    pad_rows = lambda w, rows: jnp.zeros((rows, HALF), F32).at[:w.shape[0]].set(w)
    row = lambda a: a.reshape(1, HALF)
    consts = [row(p["rw_w0"]), pad_rows(p["rw_decay_up"], LANES), row(p["rw_a0"]),
              pad_rows(p["rw_iclr_up"], LANES), pad_rows(p["rw_gate_up"], 2 * LANES),
              row(p["rw_k_k"]), row(p["rw_k_a"]), row(p["rw_r_k"]), row(p["rw_lnx_g"]),
              row(p["rw_lnx_b"]), jnp.asarray(tri, BF16)]
    full = lambda a: pl.BlockSpec(a.shape, lambda b, c: (0, 0))
    rcol = 4 * HALF // HALF
    zm3 = zm.reshape(B, T, Z_MAIN)
    out = pl.pallas_call(
        functools.partial(_rwkv_kernel, chunk=C, nb=nb),
        grid=(B // nb, nc),
        in_specs=[pl.BlockSpec((nb, C, HALF), lambda b, c: (b, c, rcol)),
                  pl.BlockSpec((nb, C, HALF), lambda b, c: (b, c, rcol + 1)),
                  pl.BlockSpec((nb, C, HALF), lambda b, c: (b, c, rcol + 2)),
                  pl.BlockSpec((nb, C, Z_SMALL), lambda b, c: (b, c, 0))] + [full(a) for a in consts],
        out_specs=pl.BlockSpec((nb, C, HALF), lambda b, c: (b, c, 0)),
        out_shape=jax.ShapeDtypeStruct((B, T, HALF), BF16),
        scratch_shapes=[pltpu.VMEM((nb, N_HEADS // 2, LANES, LANES), F32), pltpu.VMEM((nb, C, HALF), F32)],
        compiler_params=_cparams(("arbitrary", "arbitrary")),
        name="rwkv",
    )(zm3, zm3, zm3, zs.reshape(B, T, Z_SMALL), *consts)
    return out.reshape(N, HALF)


def _first_index(mask, iota, big):
    return jnp.min(jnp.where(mask, iota, big), axis=0, keepdims=True)


def _outproj_kernel(x_ref, yf_ref, yr_ref, g1_ref, sh2_ref, sc2_ref, g2_ref, n2g_ref, wo_ref, rwt_ref,
                    rb_ref, swg_ref, swu_ref, swd_ref, ustrict_ref,
                    xs_ref, hp_ref, idx_ref, wts_ref, rank_ref, *, tm):
    d = (jnp.dot(yf_ref[...], wo_ref[0:HALF, :], preferred_element_type=F32)
         + jnp.dot(yr_ref[...], wo_ref[HALF:2 * HALF, :], preferred_element_type=F32))
    x2 = x_ref[...] + g1_ref[...] * d
    h = x2 * lax.rsqrt(jnp.mean(x2 * x2, axis=-1, keepdims=True) + RMS_EPS) * n2g_ref[...]
    h = h * (1.0 + sc2_ref[...]) + sh2_ref[...]
    hb = h.astype(BF16)
    act = _silu(jnp.dot(hb, swg_ref[...], preferred_element_type=F32)) * jnp.dot(
        hb, swu_ref[...], preferred_element_type=F32)
    shared = jnp.dot(act.astype(BF16), swd_ref[...], preferred_element_type=F32)
    xs_ref[...] = x2 + g2_ref[...] * shared
    half = D_MODEL // 2
    hp_ref[...] = _pack_halves(h)

    logits = lax.dot_general(rwt_ref[...], h, (((1,), (1,)), ((), ())), precision=HIGHEST,
                             preferred_element_type=F32)
    scores = _sigmoid(logits)
    sel = scores + rb_ref[:, 0:1]
    neg = -jnp.inf
    sel3 = sel.reshape(N_GROUPS, GROUP_SIZE, tm)
    io_in = lax.broadcasted_iota(I32, (N_GROUPS, GROUP_SIZE, tm), 1)
    m1 = jnp.max(sel3, axis=1, keepdims=True)
    f1 = jnp.min(jnp.where(sel3 == m1, io_in, GROUP_SIZE), axis=1, keepdims=True)
    m2 = jnp.max(jnp.where(io_in == f1, neg, sel3), axis=1, keepdims=True)
    gs = (m1 + m2).reshape(N_GROUPS, tm)
    io_g = lax.broadcasted_iota(I32, (N_GROUPS, tm), 0)
    gmask = jnp.zeros((N_GROUPS, tm), jnp.bool_)
    for _ in range(TOPK_GROUPS):
        mg = jnp.max(gs, axis=0, keepdims=True)
        fg = _first_index(gs == mg, io_g, N_GROUPS)
        pick = io_g == fg
        gmask = jnp.logical_or(gmask, pick)
        gs = jnp.where(pick, neg, gs)
    emask = jnp.broadcast_to(gmask.reshape(N_GROUPS, 1, tm), (N_GROUPS, GROUP_SIZE, tm)).reshape(N_EXPERTS, tm)
    cur = jnp.where(emask, sel, neg)
    io_e = lax.broadcasted_iota(I32, (N_EXPERTS, tm), 0)
    picks, idxs, wts = [], [], []
    for _ in range(TOP_K):
        me = jnp.max(cur, axis=0, keepdims=True)
        fe = _first_index(cur == me, io_e, N_EXPERTS)
        pick = io_e == fe
        picks.append(pick)
        idxs.append(fe)
        wts.append(jnp.sum(jnp.where(pick, scores, 0.0), axis=0, keepdims=True))
        cur = jnp.where(pick, neg, cur)
    wsum = wts[0]
    for w in wts[1:]:
        wsum = wsum + w
    zero_i = jnp.zeros((1, tm), I32)
    zero_f = jnp.zeros((1, tm), F32)
    idx_ref[...] = jnp.concatenate(idxs + [zero_i, zero_i], axis=0)
    wts_ref[...] = jnp.concatenate([w / wsum * ROUTED_SCALE for w in wts] + [zero_f, zero_f], axis=0)
    cnt = picks[0].astype(F32)
    for pk in picks[1:]:
        cnt = cnt + pk.astype(F32)
    excl = jnp.dot(cnt.astype(BF16), ustrict_ref[...], preferred_element_type=F32)
    ranks = [jnp.sum(jnp.where(pk, excl, 0.0), axis=0, keepdims=True).astype(I32) for pk in picks]
    rank_ref[...] = jnp.concatenate(ranks + [zero_i, zero_i], axis=0)


def _outproj(x2d, yf, yr, mod4, norm2_g, w_out, router_w, router_bias, swg, swu, swd, T):
    N, D = x2d.shape
    tm = min(512, T)
    tps = T // tm
    vec = lambda j: pl.BlockSpec((None, None, 1, D), lambda i: (i // tps, j, 0, 0))
    full = lambda a: pl.BlockSpec(a.shape, lambda i: (0, 0))
    ts = min(MOE_TILE, tm)
    ustrict = jnp.asarray(np.kron(np.eye(tm // ts, dtype=np.float32),
                                  np.triu(np.ones((ts, ts), np.float32), 1)), BF16)
    rb = jnp.broadcast_to(router_bias.reshape(N_EXPERTS, 1), (N_EXPERTS, LANES))
    consts = [norm2_g.reshape(1, D), w_out.astype(BF16), router_w.T, rb, swg.astype(BF16),
              swu.astype(BF16), swd.astype(BF16), ustrict]
    small = lambda dt: jax.ShapeDtypeStruct((8, N), dt)
    return pl.pallas_call(
        functools.partial(_outproj_kernel, tm=tm),
        grid=(N // tm,),
        in_specs=[pl.BlockSpec((tm, D), lambda i: (i, 0)),
                  pl.BlockSpec((tm, HALF), lambda i: (i, 0)),
                  pl.BlockSpec((tm, HALF), lambda i: (i, 0)),
                  vec(2), vec(3), vec(4), vec(5)] + [full(a) for a in consts],
        out_specs=[pl.BlockSpec((tm, D), lambda i: (i, 0)),
                   pl.BlockSpec((tm, D // 2), lambda i: (i, 0)),
                   pl.BlockSpec((8, tm), lambda i: (0, i)),
                   pl.BlockSpec((8, tm), lambda i: (0, i)),
                   pl.BlockSpec((8, tm), lambda i: (0, i))],
        out_shape=[jax.ShapeDtypeStruct((N, D), F32), jax.ShapeDtypeStruct((N, D // 2), jnp.uint32),
                   small(I32), small(F32), small(I32)],
        compiler_params=_cparams(("arbitrary",)),
        name="outproj",
    )(x2d, yf, yr, mod4, mod4, mod4, mod4, *consts)


def _for_each(count, fn, group):
    full = count // group

    def trip(t, c):
        for u in range(group):
            fn(t * group + u)
        return c

    def single(g, c):
        fn(g)
        return c

    lax.fori_loop(0, full, trip, 0)
    lax.fori_loop(full * group, count, single, 0)


def _wait_granules(count, wait_rows):
    full = count // WAIT_CHUNK

    def chunk(_, c):
        wait_rows(WAIT_CHUNK * GRANULE)
        return c

    def single(_, c):
        wait_rows(GRANULE)
        return c

    lax.fori_loop(0, full, chunk, 0)
    lax.fori_loop(full * WAIT_CHUNK, count, single, 0)


def _moe_plan(idx_t, rank_t):
    n = idx_t.shape[1]
    n_tiles = n // MOE_TILE
    experts = jnp.arange(N_EXPERTS, dtype=I32)
    hot = idx_t[:TOP_K, :, None] == experts
    cnt = jnp.sum(hot.reshape(TOP_K, n_tiles, MOE_TILE, N_EXPERTS).astype(I32), axis=(0, 2))
    gran = (cnt + GRANULE - 1) // GRANULE
    loc_end = jnp.cumsum(gran, axis=1)
    loc_off = loc_end - gran
    g_tile = loc_end[:, N_EXPERTS - 1]
    padded = (GRANULE * jnp.sum(gran, axis=0) + EXPERT_BLOCK - 1) // EXPERT_BLOCK * EXPERT_BLOCK
    pad_end = jnp.cumsum(padded)
    glob_off = (pad_end - padded)[None, :] + GRANULE * (jnp.cumsum(gran, axis=0) - gran)
    loc_tok = jnp.repeat(GRANULE * loc_off, MOE_TILE, axis=0)
    pos = rank_t[:TOP_K] + jnp.sum(jnp.where(hot, loc_tok[None], 0), axis=-1)
    pos = jnp.concatenate([pos, jnp.full((8 - TOP_K, n), -1, I32)], axis=0)
    g = jnp.arange(MOE_TILE, dtype=I32)
    e_of_g = jnp.minimum(jnp.sum((loc_end[:, None, :] <= g[None, :, None]).astype(I32), axis=-1), N_EXPERTS - 1)
    dst = jnp.sum(jnp.where(e_of_g[:, :, None] == experts,
                            glob_off[:, None, :] + GRANULE * (g[None, :, None] - loc_off[:, None, :]), 0), axis=-1)
    gtab = jnp.where(g[None, :] < g_tile[:, None], dst, 0).reshape(n_tiles, 1, MOE_TILE)
    n_blocks = -(-(n * TOP_K + GRANULE * N_EXPERTS * n_tiles) // EXPERT_BLOCK) + N_EXPERTS
    blk_start = jnp.arange(n_blocks, dtype=I32) * EXPERT_BLOCK
    blk_e = jnp.minimum(jnp.sum((pad_end[None, :] <= blk_start[:, None]).astype(I32), axis=1), N_EXPERTS - 1)
    plan = dict(pos=pos.astype(I32), gtab=gtab.astype(I32), g_tile=g_tile.astype(I32),
                pad_end=pad_end.astype(I32), padded=padded.astype(I32), blk_e=blk_e.astype(I32),
                n_used=(pad_end[N_EXPERTS - 1:] // EXPERT_BLOCK).astype(I32))
    return plan, n_blocks * EXPERT_BLOCK


def _dispatch_kernel(pend_ref, padded_ref, nu_ref, gt_ref, gtab_ref, pos_ref, hp_ref, xs_ref, zeros, sbuf,
                     sem, zsem, *, n_blocks, n_tiles):
    @pl.when(pl.program_id(0) == 0)
    def _():
        zeros[...] = jnp.zeros_like(zeros)

        def block_copy(start):
            return pltpu.make_async_copy(zeros, xs_ref.at[pl.ds(pl.multiple_of(start, EXPERT_BLOCK),
                                                                 EXPERT_BLOCK), :], zsem)

        def pad_loop(fn):
            def body(e, _):
                @pl.when(padded_ref[e] > 0)
                def _():
                    fn(block_copy(pend_ref[e] - EXPERT_BLOCK))
                return 0
            lax.fori_loop(0, N_EXPERTS, body, 0)

        def tail_loop(fn):
            def body(b, _):
                fn(block_copy(b * EXPERT_BLOCK))
                return 0
            lax.fori_loop(nu_ref[0], n_blocks, body, 0)

        pad_loop(lambda cp: cp.start())
        tail_loop(lambda cp: cp.start())
        pad_loop(lambda cp: cp.wait())
        tail_loop(lambda cp: cp.wait())

    i = pl.program_id(0)
    slot = i % 2

    def wait_granules(which, count):
        def wait_rows(rows):
            pltpu.make_async_copy(sbuf.at[which, pl.ds(0, rows), :], xs_ref.at[pl.ds(0, rows), :],
                                  sem.at[which]).wait()
        _wait_granules(count, wait_rows)

    @pl.when(i >= 2)
    def _():
        wait_granules(slot, gt_ref[jnp.maximum(i - 2, 0)])

    pos = pos_ref[...]
    lo, hi = _unpack_halves(hp_ref[...])
    lo = lo.astype(BF16)
    hi = hi.astype(BF16)
    r_iota = lax.broadcasted_iota(I32, (MOE_ROW_CHUNK, MOE_TILE), 0).astype(F32).astype(BF16)
    one = jnp.ones((), BF16)
    for c in range(MOE_SORTED_ROWS // MOE_ROW_CHUNK):
        rel = pos - c * MOE_ROW_CHUNK
        rel = jnp.where(jnp.logical_and(rel >= 0, rel < MOE_ROW_CHUNK), rel, -1).astype(F32)
        perm = jnp.zeros((MOE_ROW_CHUNK, MOE_TILE), BF16)
        for k in range(TOP_K):
            perm = jnp.where(r_iota == rel[k:k + 1, :].astype(BF16), one, perm)
        sbuf[slot, c * MOE_ROW_CHUNK:(c + 1) * MOE_ROW_CHUNK, :] = _pack_bf16_values(
            jnp.dot(perm, lo, preferred_element_type=F32), jnp.dot(perm, hi, preferred_element_type=F32))

    def issue(g):
        pltpu.make_async_copy(sbuf.at[slot, pl.ds(pl.multiple_of(g * GRANULE, GRANULE), GRANULE), :],
                              xs_ref.at[pl.ds(pl.multiple_of(gtab_ref[0, 0, g], GRANULE), GRANULE), :],
                              sem.at[slot]).start()
    _for_each(gt_ref[i], issue, DMA_UNROLL)

    @pl.when(i == n_tiles - 1)
    def _():
        wait_granules(slot, gt_ref[i])
        if n_tiles > 1:
            wait_granules(1 - slot, gt_ref[jnp.maximum(i - 1, 0)])


def _dispatch(hp, plan, n_rows):
    N, W = hp.shape
    n_tiles = N // MOE_TILE
    n_blocks = n_rows // EXPERT_BLOCK
    return pl.pallas_call(
        functools.partial(_dispatch_kernel, n_blocks=n_blocks, n_tiles=n_tiles),
        grid_spec=pltpu.PrefetchScalarGridSpec(
            num_scalar_prefetch=4, grid=(n_tiles,),
            in_specs=[pl.BlockSpec((1, 1, MOE_TILE), lambda i, *_: (i, 0, 0), memory_space=pltpu.SMEM),
                      pl.BlockSpec((8, MOE_TILE), lambda i, *_: (0, i)),
                      pl.BlockSpec((MOE_TILE, W), lambda i, *_: (i, 0))],
            out_specs=pl.BlockSpec(memory_space=pl.ANY),
            scratch_shapes=[pltpu.VMEM((EXPERT_BLOCK, W), jnp.uint32),
                            pltpu.VMEM((2, MOE_SORTED_ROWS, W), jnp.uint32),
                            pltpu.SemaphoreType.DMA((2,)), pltpu.SemaphoreType.DMA]),
        out_shape=jax.ShapeDtypeStruct((n_rows, W), jnp.uint32),
        compiler_params=_cparams(("arbitrary",)),
        name="dispatch",
    )(plan["pad_end"], plan["padded"], plan["n_used"], plan["g_tile"], plan["gtab"], plan["pos"], hp)


def _experts_kernel(be_ref, nu_ref, xs_ref, wgf_ref, wuf_ref, wdf_ref, ys_ref, wg_ref, wu_ref, wd_ref):
    i = pl.program_id(0)
    live = i < nu_ref[0]
    new_expert = jnp.logical_or(i == 0, be_ref[i] != be_ref[jnp.maximum(i - 1, 0)])

    @pl.when(jnp.logical_not(live))
    def _():
        ys_ref[...] = jnp.zeros_like(ys_ref)

    @pl.when(jnp.logical_and(live, new_expert))
    def _():
        wg_ref[...] = wgf_ref[...].astype(BF16)
        wu_ref[...] = wuf_ref[...].astype(BF16)
        wd_ref[...] = wdf_ref[...].astype(BF16)

    @pl.when(live)
    def _():
        half = D_MODEL // 2
        lo, hi = _unpack_halves(xs_ref[...])
        lo = lo.astype(BF16)
        hi = hi.astype(BF16)
        gate = (jnp.dot(lo, wg_ref[0:half, :], preferred_element_type=F32)
                + jnp.dot(hi, wg_ref[half:D_MODEL, :], preferred_element_type=F32))
        up = (jnp.dot(lo, wu_ref[0:half, :], preferred_element_type=F32)
              + jnp.dot(hi, wu_ref[half:D_MODEL, :], preferred_element_type=F32))
        y = jnp.dot((_silu(gate) * up).astype(BF16), wd_ref[...], preferred_element_type=F32)
        ys_ref[...] = _pack_halves(y)


def _experts(xs, blk_e, n_used, wg, wu, wd):
    n_rows, W = xs.shape
    n_blocks = n_rows // EXPERT_BLOCK
    row_map = lambda i, be, nu: (jnp.minimum(i, nu[0] - 1), 0)
    return pl.pallas_call(
        _experts_kernel,
        grid_spec=pltpu.PrefetchScalarGridSpec(
            num_scalar_prefetch=2, grid=(n_blocks,),
            in_specs=[pl.BlockSpec((EXPERT_BLOCK, W), row_map),
                      pl.BlockSpec((None, D_MODEL, D_EXPERT), lambda i, be, nu: (be[i], 0, 0)),
                      pl.BlockSpec((None, D_MODEL, D_EXPERT), lambda i, be, nu: (be[i], 0, 0)),
                      pl.BlockSpec((None, D_EXPERT, D_MODEL), lambda i, be, nu: (be[i], 0, 0))],
            out_specs=pl.BlockSpec((EXPERT_BLOCK, W), lambda i, be, nu: (i, 0)),
            scratch_shapes=[pltpu.VMEM((D_MODEL, D_EXPERT), BF16), pltpu.VMEM((D_MODEL, D_EXPERT), BF16),
                            pltpu.VMEM((D_EXPERT, D_MODEL), BF16)]),
        out_shape=jax.ShapeDtypeStruct((n_rows, W), jnp.uint32),
        compiler_params=_cparams(("arbitrary",)),
        name="experts",
    )(blk_e, n_used, xs, wg, wu, wd)


def _combine_kernel(gt_ref, gtab_ref, gtab_next_ref, ys_ref, pos_ref, xs_ref, wts_ref, g2_ref, fg_ref, o_ref,
                    buf, sem, *, n_tiles):
    i = pl.program_id(0)
    slot = i % 2

    def fetch(which, table_ref, count):
        def body(g):
            pltpu.make_async_copy(
                ys_ref.at[pl.ds(pl.multiple_of(table_ref[0, 0, g], GRANULE), GRANULE), :],
                buf.at[which, pl.ds(pl.multiple_of(g * GRANULE, GRANULE), GRANULE), :], sem.at[which]).start()
        _for_each(count, body, DMA_UNROLL)

    @pl.when(i == 0)
    def _():
        buf[...] = jnp.zeros_like(buf)
        fetch(0, gtab_ref, gt_ref[0])

    @pl.when(i + 1 < n_tiles)
    def _():
        fetch(1 - slot, gtab_next_ref, gt_ref[jnp.minimum(i + 1, n_tiles - 1)])

    def wait_rows(rows):
        pltpu.make_async_copy(ys_ref.at[pl.ds(0, rows), :], buf.at[slot, pl.ds(0, rows), :],
                              sem.at[slot]).wait()
    _wait_granules(gt_ref[i], wait_rows)

    pos = pos_ref[...]
    w = wts_ref[...]
    c_iota = lax.broadcasted_iota(I32, (MOE_TILE, MOE_ROW_CHUNK), 1).astype(F32).astype(BF16)
    half = D_MODEL // 2
    r_lo = jnp.zeros((MOE_TILE, half), F32)
    r_hi = jnp.zeros((MOE_TILE, half), F32)
    for c in range(MOE_SORTED_ROWS // MOE_ROW_CHUNK):
        rel = pos - c * MOE_ROW_CHUNK
        rel = jnp.where(jnp.logical_and(rel >= 0, rel < MOE_ROW_CHUNK), rel, -1).astype(F32)
        wb = jnp.zeros((MOE_TILE, MOE_ROW_CHUNK), BF16)
        for k in range(TOP_K):
            wb = jnp.where(c_iota == rel[:, k:k + 1].astype(BF16), w[:, k:k + 1].astype(BF16), wb)
        lo, hi = _unpack_halves(buf[slot, c * MOE_ROW_CHUNK:(c + 1) * MOE_ROW_CHUNK, :])
        r_lo = r_lo + jnp.dot(wb, lo.astype(BF16), preferred_element_type=F32)
        r_hi = r_hi + jnp.dot(wb, hi.astype(BF16), preferred_element_type=F32)
    x3 = xs_ref[...] + g2_ref[...] * jnp.concatenate([r_lo, r_hi], axis=1)
    y = x3 * lax.rsqrt(jnp.mean(x3 * x3, axis=-1, keepdims=True) + RMS_EPS) * fg_ref[...]
    o_ref[...] = y


def _combine(ys, plan, xsr, wts_t, mod4, final_g, T):
    N, D = xsr.shape
    W = ys.shape[1]
    tm = MOE_TILE
    tps = T // tm
    n_tiles = N // tm
    tile_tab = lambda off: pl.BlockSpec((1, 1, MOE_TILE), lambda i, gt: (jnp.minimum(i + off, n_tiles - 1), 0, 0),
                                        memory_space=pltpu.SMEM)
    return pl.pallas_call(
        functools.partial(_combine_kernel, n_tiles=n_tiles),
        grid_spec=pltpu.PrefetchScalarGridSpec(
            num_scalar_prefetch=1, grid=(n_tiles,),
            in_specs=[tile_tab(0), tile_tab(1),
                      pl.BlockSpec(memory_space=pl.ANY),
                      pl.BlockSpec((tm, 8), lambda i, gt: (i, 0)),
                      pl.BlockSpec((tm, D), lambda i, gt: (i, 0)),
                      pl.BlockSpec((tm, 8), lambda i, gt: (i, 0)),
                      pl.BlockSpec((None, None, 1, D), lambda i, gt: (i // tps, 5, 0, 0)),
                      pl.BlockSpec((1, D), lambda i, gt: (0, 0))],
            out_specs=pl.BlockSpec((tm, D), lambda i, gt: (i, 0)),
            scratch_shapes=[pltpu.VMEM((2, MOE_SORTED_ROWS, W), jnp.uint32), pltpu.SemaphoreType.DMA((2,))]),
        out_shape=jax.ShapeDtypeStruct((N, D), F32),
        compiler_params=_cparams(("arbitrary",)),
        name="combine",
    )(plan["g_tile"], plan["gtab"], plan["gtab"], ys, plan["pos"].T, xsr, wts_t, mod4, final_g.reshape(1, D))


def _pack_w_in(w_in):
    D = w_in.shape[0]
    fox_cols = 4 * HALF + 3 * N_HEADS
    wf = w_in[:, :fox_cols]
    wr = w_in[:, fox_cols:]
    o = 3 * HALF
    pad = lambda a, n: jnp.concatenate([a, jnp.zeros((D, n - a.shape[1]), a.dtype)], axis=1)
    parts = [wf[:, :4 * HALF], wr[:, :o],
             pad(wf[:, 4 * HALF:], LANES),
             pad(wr[:, o:o + DECAY_LORA], LANES),
             pad(wr[:, o + DECAY_LORA:o + DECAY_LORA + ICLR_LORA], LANES),
             pad(wr[:, o + DECAY_LORA + ICLR_LORA:], 2 * LANES)]
    return jnp.concatenate(parts, axis=1).astype(BF16)


def _pack_mu(mu):
    o = 3 * HALF
    pad = lambda a, n: jnp.concatenate([a, jnp.zeros((n - a.shape[0],), a.dtype)])
    small = jnp.concatenate([jnp.zeros((LANES,), mu.dtype),
                             pad(mu[o:o + DECAY_LORA], LANES),
                             pad(mu[o + DECAY_LORA:o + DECAY_LORA + ICLR_LORA], LANES),
                             pad(mu[o + DECAY_LORA + ICLR_LORA:], 2 * LANES)])
    return mu[:o].reshape(1, o), small.reshape(1, Z_SMALL)


def kernel(x, c, norm1_g, norm2_g, ada_w, ada_b, w_in, w_out, fox_qn_g, fox_kn_g, fox_on_g, fox_forget_b,
           rw_mu, rw_w0, rw_decay_up, rw_a0, rw_iclr_up, rw_gate_up, rw_k_k, rw_k_a, rw_r_k, rw_lnx_g,
           rw_lnx_b, router_w, router_bias, exp_w_gate, exp_w_up, exp_w_down, sh_w_gate, sh_w_up,
           sh_w_down, final_g):
    B, T, D = x.shape
    N = B * T
    depth = norm1_g.shape[0]
    assert depth == 1, "the combine kernel fuses the final RMSNorm, so exactly one layer is supported"
    xf = x.reshape(N, D)
    for l in range(depth):
        mod4 = _ada(c, ada_w[l], ada_b[l]).reshape(B, 6, 1, D)
        mu_big, mu_small = _pack_mu(rw_mu[l])
        zm, zs = _inproj(xf, mod4, norm1_g[l], _pack_w_in(w_in[l]), mu_big, mu_small, T)
        qp, kp, vp = _foxprep(zm, zs, fox_forget_b[l], fox_qn_g[l], fox_kn_g[l], T)
        y_fox = _attention(qp, kp, vp, zm, fox_on_g[l], B, T)
        rw = dict(rw_w0=rw_w0[l], rw_decay_up=rw_decay_up[l], rw_a0=rw_a0[l], rw_iclr_up=rw_iclr_up[l],
                  rw_gate_up=rw_gate_up[l], rw_k_k=rw_k_k[l], rw_k_a=rw_k_a[l], rw_r_k=rw_r_k[l],
                  rw_lnx_g=rw_lnx_g[l], rw_lnx_b=rw_lnx_b[l])
        y_rwkv = _rwkv(zm, zs, rw, B, T)
        xsr, hp, idx_t, wts_t, rank_t = _outproj(
            xf, y_fox, y_rwkv, mod4, norm2_g[l], w_out[l], router_w[l], router_bias[l],
            sh_w_gate[l], sh_w_up[l], sh_w_down[l], T)
        plan, n_rows = _moe_plan(idx_t, rank_t)
        xs = _dispatch(hp, plan, n_rows)
        ys = _experts(xs, plan["blk_e"], plan["n_used"], exp_w_gate[l], exp_w_up[l], exp_w_down[l])
        xf = _combine(ys, plan, xsr, wts_t.T, mod4, final_g, T)
    return xf.reshape(B, T, D)
```

```python
import functools

import jax
import jax.numpy as jnp
import numpy as np
from jax import lax
from jax.experimental import pallas as pl
from jax.experimental.pallas import tpu as pltpu

F32 = jnp.float32
BF16 = jnp.bfloat16
I32 = jnp.int32
HIGHEST = lax.Precision.HIGHEST

D_MODEL = 1024
HEAD_DIM = 64
N_HEADS = 8
HALF = N_HEADS * HEAD_DIM
RMS_EPS = 1e-6
LNX_EPS = 64e-5
LOG2E = 1.4426950408889634
DECAY_LORA = 64
ICLR_LORA = 64
GATE_LORA = 160
N_EXPERTS = 64
N_GROUPS = 8
GROUP_SIZE = N_EXPERTS // N_GROUPS
TOPK_GROUPS = 4
TOP_K = 6
D_EXPERT = 256
D_SHARED = 256
ROUTED_SCALE = 2.5
EXPERT_BLOCK = 1024

LANES = 128
Z_MAIN = 4 * HALF + 3 * HALF
Z_SMALL = 5 * LANES
VMEM_LIMIT = 56 * 1024 * 1024
ATTN_BLOCK = 512
ATTN_HEADS = 4
DMA_UNROLL = 4
WAIT_CHUNK = 16
MOE_TILE = 256
GRANULE = 8
MOE_SORTED_ROWS = MOE_TILE * TOP_K + N_EXPERTS * GRANULE
MOE_SPARE_ROWS = 2 * MOE_TILE * GRANULE
MOE_ROW_CHUNK = 256
RWKV_CHUNK = 64
RWKV_SEQS_PER_STEP = 4


def _cparams(semantics):
    return pltpu.CompilerParams(dimension_semantics=semantics, vmem_limit_bytes=VMEM_LIMIT)


def _mm(a, b):
    return jnp.dot(a.astype(BF16), b.astype(BF16), preferred_element_type=F32)


def _mm_nt(a, b):
    return lax.dot_general(a.astype(BF16), b.astype(BF16), (((1,), (1,)), ((), ())),
                           preferred_element_type=F32)


def _mm_tn(a, b):
    return lax.dot_general(a.astype(BF16), b.astype(BF16), (((0,), (0,)), ((), ())),
                           preferred_element_type=F32)


def _mm_f32(a, b):
    return jnp.dot(a, b, precision=HIGHEST, preferred_element_type=F32)


def _bf16_pieces(x, passes):
    pieces = []
    for _ in range(passes):
        piece = x.astype(BF16)
        pieces.append(piece)
        x = x - piece.astype(F32)
    return pieces


def _mm_split(m01, x, passes=3):
    return sum(jnp.dot(m01, p, preferred_element_type=F32) for p in _bf16_pieces(x, passes))


def _mm_split_r(x, m01, passes=2):
    return sum(jnp.dot(p, m01, preferred_element_type=F32) for p in _bf16_pieces(x, passes))


def _sigmoid(x):
    return 1.0 / (1.0 + jnp.exp(-x))


def _softplus(x):
    return jnp.maximum(x, 0.0) + jnp.log(1.0 + jnp.exp(-jnp.abs(x)))


def _silu(x):
    return x * _sigmoid(x)


def _pack_halves(x):
    w = x.shape[1] // 2
    bits = lambda t: lax.bitcast_convert_type(t.astype(jnp.bfloat16).astype(F32), jnp.uint32)
    return (bits(x[:, 0:w]) >> 16) | (bits(x[:, w:2 * w]) & jnp.uint32(0xFFFF0000))


def _pack_bf16_values(lo, hi):
    return (lax.bitcast_convert_type(lo, jnp.uint32) >> 16) | lax.bitcast_convert_type(hi, jnp.uint32)


def _unpack_halves(p):
    lo = lax.bitcast_convert_type(p << 16, F32)
    hi = lax.bitcast_convert_type(p & jnp.uint32(0xFFFF0000), F32)
    return lo, hi


def _shift_rows(z, carry_ref, first):
    rows = z.shape[0]
    prev_row = jnp.where(first, 0.0, carry_ref[0:1, :])
    prev = pltpu.roll(z, 1, 0)
    row0 = lax.broadcasted_iota(I32, (rows, 1), 0) == 0
    prev = jnp.where(row0, prev_row, prev)
    carry_ref[0:1, :] = z[rows - 1:rows, :]
    return prev


def _ada_kernel(c_ref, w_ref, b_ref, o_ref):
    o_ref[...] = _mm_f32(_silu(c_ref[...]), w_ref[...]) + b_ref[...]


def _ada(c, ada_w, ada_b):
    B, D = c.shape
    n_out = ada_w.shape[1]
    tn = 512
    return pl.pallas_call(
        _ada_kernel,
        grid=(n_out // tn,),
        in_specs=[pl.BlockSpec((B, D), lambda j: (0, 0)),
                  pl.BlockSpec((D, tn), lambda j: (0, j)),
                  pl.BlockSpec((1, tn), lambda j: (0, j))],
        out_specs=pl.BlockSpec((B, tn), lambda j: (0, j)),
        out_shape=jax.ShapeDtypeStruct((B, n_out), F32),
        compiler_params=_cparams(("arbitrary",)),
        name="ada",
    )(c, ada_w, ada_b.reshape(1, n_out))


def _inproj_kernel(x_ref, g_ref, sh_ref, sc_ref, w_ref, mub_ref, mus_ref, zm_ref, zs_ref,
                   carry_b, carry_s, *, tiles_per_seq):
    first = (pl.program_id(0) % tiles_per_seq) == 0
    x = x_ref[...]
    h = x * lax.rsqrt(jnp.mean(x * x, axis=-1, keepdims=True) + RMS_EPS) * g_ref[...]
    hb = (h * (1.0 + sc_ref[...]) + sh_ref[...]).astype(BF16)
    nf = 4 * HALF
    zm_ref[:, 0:nf] = jnp.dot(hb, w_ref[:, 0:nf], preferred_element_type=F32).astype(BF16)
    zr = jnp.dot(hb, w_ref[:, nf:Z_MAIN], preferred_element_type=F32)
    zr = zr + mub_ref[...] * (_shift_rows(zr, carry_b, first) - zr)
    zm_ref[:, nf:Z_MAIN] = zr.astype(BF16)
    zs = jnp.dot(hb, w_ref[:, Z_MAIN:Z_MAIN + Z_SMALL], preferred_element_type=F32)
    zs_ref[...] = zs + mus_ref[...] * (_shift_rows(zs, carry_s, first) - zs)


def _inproj(x2d, mod4, norm_g, w_all, mu_big, mu_small, T):
    N, D = x2d.shape
    tm = min(512, T)
    tps = T // tm
    vec = lambda j: pl.BlockSpec((None, None, 1, D), lambda i: (i // tps, j, 0, 0))
    return pl.pallas_call(
        functools.partial(_inproj_kernel, tiles_per_seq=tps),
        grid=(N // tm,),
        in_specs=[pl.BlockSpec((tm, D), lambda i: (i, 0)),
                  pl.BlockSpec((1, D), lambda i: (0, 0)),
                  vec(0), vec(1),
                  pl.BlockSpec((D, Z_MAIN + Z_SMALL), lambda i: (0, 0)),
                  pl.BlockSpec((1, 3 * HALF), lambda i: (0, 0)),
                  pl.BlockSpec((1, Z_SMALL), lambda i: (0, 0))],
        out_specs=[pl.BlockSpec((tm, Z_MAIN), lambda i: (i, 0)),
                   pl.BlockSpec((tm, Z_SMALL), lambda i: (i, 0))],
        out_shape=[jax.ShapeDtypeStruct((N, Z_MAIN), BF16),
                   jax.ShapeDtypeStruct((N, Z_SMALL), F32)],
        scratch_shapes=[pltpu.VMEM((8, 3 * HALF), F32), pltpu.VMEM((8, Z_SMALL), F32)],
        compiler_params=_cparams(("arbitrary",)),
        name="inproj",
    )(x2d, norm_g.reshape(1, D), mod4, mod4, w_all, mu_big, mu_small)


def _foxprep_kernel(q_ref, k_ref, v_ref, zs_ref, fb_ref, qg_ref, kg_ref, tri_ref, eexp_ref,
                    esum_ref, e8_ref, plq_ref, plk_ref, plv_ref, cq_ref, ck_ref, cv_ref,
                    qp_ref, kp_ref, vp_ref, carry_k, carry_v, carry_c, *, tiles_per_seq):
    first = (pl.program_id(0) % tiles_per_seq) == 0
    zs = zs_ref[...]
    logf = -_softplus(-(zs + fb_ref[...]))
    cum = _mm_split(tri_ref[...], logf) + jnp.where(first, 0.0, carry_c[0:1, :])
    carry_c[0:1, :] = cum[cum.shape[0] - 1:, :]
    cum2 = cum * LOG2E
    c_hi = cum2.astype(BF16)
    r1 = cum2 - c_hi.astype(F32)
    c_mid = r1.astype(BF16)
    c_lo = (r1 - c_mid.astype(F32)).astype(BF16)
    a_full = _mm_split_r(_sigmoid(zs), eexp_ref[...])
    k = k_ref[...].astype(F32)
    v = v_ref[...].astype(F32)
    a_k = a_full[:, 0:HALF]
    a_v = a_full[:, HALF:2 * HALF]
    k = a_k * _shift_rows(k, carry_k, first) + (1.0 - a_k) * k
    v = a_v * _shift_rows(v, carry_v, first) + (1.0 - a_v) * v
    q = q_ref[...].astype(F32)

    def head_rms(t, gain):
        ms = _mm_split_r(t * t, esum_ref[...]) * (1.0 / HEAD_DIM)
        inv = _mm_split_r(lax.rsqrt(ms + RMS_EPS), e8_ref[...])
        return t * inv * gain

    qn = head_rms(q, qg_ref[...]) * (HEAD_DIM ** -0.5 * LOG2E)
    kn = head_rms(k, kg_ref[...])
    aug = [c_hi, c_mid, c_lo]
    lhs_q = jnp.concatenate([qn.astype(BF16)] + aug, axis=1)
    lhs_k = jnp.concatenate([kn.astype(BF16)] + aug, axis=1)
    qp_ref[...] = (jnp.dot(lhs_q, plq_ref[...], preferred_element_type=F32) + cq_ref[...]).astype(BF16)
    kp_ref[...] = (jnp.dot(lhs_k, plk_ref[...], preferred_element_type=F32) + ck_ref[...]).astype(BF16)
    vp_ref[...] = (jnp.dot(v.astype(BF16), plv_ref[...], preferred_element_type=F32)
                   + cv_ref[...]).astype(BF16)


def _fox_constants(tm):
    hp = N_HEADS * LANES
    eexp = np.zeros((LANES, 2 * HALF), np.float32)
    esum = np.zeros((HALF, LANES), np.float32)
    e8 = np.zeros((LANES, HALF), np.float32)
    plq = np.zeros((HALF + 3 * LANES, hp), np.float32)
    plk = np.zeros((HALF + 3 * LANES, hp), np.float32)
    plv = np.zeros((HALF, hp), np.float32)
    cq = np.zeros((1, hp), np.float32)
    ck = np.zeros((1, hp), np.float32)
    cv = np.zeros((1, hp), np.float32)
    for h in range(N_HEADS):
        sl = slice(h * HEAD_DIM, (h + 1) * HEAD_DIM)
        eexp[8 + h, sl] = 1.0
        eexp[16 + h, HALF + h * HEAD_DIM:HALF + (h + 1) * HEAD_DIM] = 1.0
        esum[sl, h] = 1.0
        e8[h, sl] = 1.0
        base = h * LANES
        for d in range(HEAD_DIM):
            plq[h * HEAD_DIM + d, base + d] = 1.0
            plk[h * HEAD_DIM + d, base + d] = 1.0
            plv[h * HEAD_DIM + d, base + d] = 1.0
        for j in range(3):
            plq[HALF + j * LANES + h, base + HEAD_DIM + j] = 1.0
            plk[HALF + j * LANES + h, base + HEAD_DIM + 3 + j] = -1.0
            cq[0, base + HEAD_DIM + 3 + j] = 1.0
            ck[0, base + HEAD_DIM + j] = 1.0
        cv[0, base + HEAD_DIM] = 1.0
    tri = np.tril(np.ones((tm, tm), np.float32))
    bf = lambda a: jnp.asarray(a, BF16)
    return dict(tri=bf(tri), eexp=bf(eexp), esum=bf(esum), e8=bf(e8),
                plq=bf(plq), plk=bf(plk), plv=bf(plv), cq=jnp.asarray(cq), ck=jnp.asarray(ck),
                cv=jnp.asarray(cv))


def _foxprep(zm, zs, forget_b, qn_g, kn_g, T):
    N = zm.shape[0]
    tm = min(512, T)
    tps = T // tm
    cst = _fox_constants(tm)
    hp = N_HEADS * LANES
    fb = jnp.zeros((1, LANES), F32).at[0, :N_HEADS].set(forget_b)
    full = lambda a: pl.BlockSpec(a.shape, lambda i: (0, 0))
    consts = [fb, qn_g.reshape(1, HALF), kn_g.reshape(1, HALF), cst["tri"], cst["eexp"], cst["esum"],
              cst["e8"], cst["plq"], cst["plk"], cst["plv"], cst["cq"], cst["ck"], cst["cv"]]
    return pl.pallas_call(
        functools.partial(_foxprep_kernel, tiles_per_seq=tps),
        grid=(N // tm,),
        in_specs=[pl.BlockSpec((tm, HALF), lambda i: (i, 0)),
                  pl.BlockSpec((tm, HALF), lambda i: (i, 1)),
                  pl.BlockSpec((tm, HALF), lambda i: (i, 2)),
                  pl.BlockSpec((tm, LANES), lambda i: (i, 0))] + [full(a) for a in consts],
        out_specs=[pl.BlockSpec((tm, hp), lambda i: (i, 0))] * 3,
        out_shape=[jax.ShapeDtypeStruct((N, hp), BF16)] * 3,
        scratch_shapes=[pltpu.VMEM((8, HALF), F32), pltpu.VMEM((8, HALF), F32),
                        pltpu.VMEM((8, LANES), F32)],
        compiler_params=_cparams(("arbitrary",)),
        name="foxprep",
    )(zm, zm, zm, zs, *consts)


def _attn_kernel(q_ref, k_ref, v_ref, g_ref, ong_ref, o_ref, vt_ref, *, tq):
    i = pl.program_id(2)
    n_kv = vt_ref.shape[1]
    heads = range(ATTN_HEADS)
    lanes = [slice(hh * LANES, (hh + 1) * LANES) for hh in heads]

    @pl.when(i == 0)
    def _():
        for hh in heads:
            for c in range(n_kv):
                vt_ref[hh, c] = v_ref[c * tq:(c + 1) * tq, lanes[hh]].T

    key = lax.broadcasted_iota(I32, (tq, tq), 0)
    qry = lax.broadcasted_iota(I32, (tq, tq), 1)
    causal = key <= qry
    qs = [q_ref[:, lanes[hh]] for hh in heads]

    def step(j, carry, masked):
        m, acc = carry
        start = pl.multiple_of(j * tq, tq)
        s = [lax.dot_general(k_ref[pl.ds(start, tq), lanes[hh]], qs[hh], (((1,), (1,)), ((), ())),
                             preferred_element_type=F32) for hh in heads]
        m_out, acc_out = [], []
        for hh in heads:
            sh = jnp.where(causal, s[hh], -jnp.inf) if masked else s[hh]
            m_new = jnp.maximum(m[hh], jnp.max(sh, axis=0, keepdims=True))
            p = jnp.exp2(sh - m_new)
            acc_out.append(jnp.exp2(m[hh] - m_new) * acc[hh]
                           + jnp.dot(vt_ref[hh, j], p.astype(BF16), preferred_element_type=F32))
            m_out.append(m_new)
        return tuple(m_out), tuple(acc_out)

    init = (tuple(jnp.full((1, tq), -jnp.inf, F32) for _ in heads),
            tuple(jnp.zeros((LANES, tq), F32) for _ in heads))
    carry = lax.fori_loop(0, i, functools.partial(step, masked=False), init)
    _, acc = step(i, carry, True)
    chan = lax.broadcasted_iota(I32, (LANES, tq), 0)
    chan_w = jnp.where(chan < HEAD_DIM, 1.0 / HEAD_DIM, jnp.where(chan == HEAD_DIM, RMS_EPS, 0.0))
    lane = lax.broadcasted_iota(I32, (tq, LANES), 1)
    outs = []
    for hh in heads:
        t = jnp.sum(acc[hh] * acc[hh] * chan_w, axis=0, keepdims=True)
        outs.append((acc[hh] * lax.rsqrt(t)).T)
    o = jnp.concatenate([jnp.where(lane < HEAD_DIM, outs[2 * p], pltpu.roll(outs[2 * p + 1], HEAD_DIM, 1))
                         for p in range(ATTN_HEADS // 2)], axis=1)
    y = o * ong_ref[...] * _sigmoid(g_ref[...].astype(F32))
    o_ref[...] = y.astype(BF16)


def _attention(qp, kp, vp, zm, on_g, B, T):
    N = qp.shape[0]
    tq = min(ATTN_BLOCK, T)
    nq = T // tq
    groups = N_HEADS // ATTN_HEADS
    wp = ATTN_HEADS * LANES
    wo = ATTN_HEADS * HEAD_DIM
    g_col0 = 3 * HALF // wo
    return pl.pallas_call(
        functools.partial(_attn_kernel, tq=tq),
        grid=(B, groups, nq),
        in_specs=[pl.BlockSpec((tq, wp), lambda b, p, i: (b * nq + i, p)),
                  pl.BlockSpec((T, wp), lambda b, p, i: (b, p)),
                  pl.BlockSpec((T, wp), lambda b, p, i: (b, p)),
                  pl.BlockSpec((tq, wo), lambda b, p, i: (b * nq + i, g_col0 + p)),
                  pl.BlockSpec((None, 1, wo), lambda b, p, i: (p, 0, 0))],
        out_specs=pl.BlockSpec((tq, wo), lambda b, p, i: (b * nq + i, p)),
        out_shape=jax.ShapeDtypeStruct((N, HALF), BF16),
        scratch_shapes=[pltpu.VMEM((ATTN_HEADS, nq, LANES, tq), BF16)],
        compiler_params=_cparams(("arbitrary", "arbitrary", "arbitrary")),
        name="attn",
    )(qp, kp, vp, zm, on_g.reshape(groups, 1, wo))


def _rwkv_kernel(r_ref, k_ref, v_ref, zs_ref, w0_ref, dup_ref, a0_ref, iup_ref, gup_ref, kk_ref,
                 ka_ref, rk_ref, lng_ref, lnb_ref, tri_ref, o_ref, state, ybuf, *, chunk, nb):
    C = chunk
    c_idx = pl.program_id(1)

    @pl.when(c_idx == 0)
    def _():
        state[...] = jnp.zeros_like(state)

    R = nb * C
    r = r_ref[...].reshape(R, HALF).astype(F32)
    k = k_ref[...].reshape(R, HALF).astype(F32)
    v = v_ref[...].reshape(R, HALF).astype(F32)
    zs = zs_ref[...].reshape(R, Z_SMALL)
    wd = zs[:, LANES:2 * LANES]
    ad = zs[:, 2 * LANES:3 * LANES]
    gd = zs[:, 3 * LANES:5 * LANES]
    wl = w0_ref[...] + _mm(jnp.tanh(wd), dup_ref[...])
    lw = -jnp.exp(-_softplus(-wl) - 0.5)
    a = _sigmoid(a0_ref[...] + _mm(ad, iup_ref[...]))
    g = _mm(_sigmoid(gd), gup_ref[...])
    kk = k * kk_ref[...]
    k2 = k * (1.0 + (a - 1.0) * ka_ref[...])
    cl = _mm_split(tri_ref[...], lw)
    cl_end = jnp.concatenate(
        [jnp.broadcast_to(cl[(bi + 1) * C - 1:(bi + 1) * C, :], (C, HALF)) for bi in range(nb)], axis=0)
    e_neg = jnp.exp(-cl)
    e_tail = jnp.exp(cl_end - cl)
    pre = dict(rt=r * jnp.exp(cl), at=-kk * jnp.exp(cl - lw), kh=k2 * e_neg, bh=kk * a * e_neg,
               kb=k2 * e_tail, bb=kk * a * e_tail, v=v, kk=kk, pend=jnp.exp(cl_end),
               rkr=r * k2 * rk_ref[...], g=g)

    C2 = 2 * C
    row = lax.broadcasted_iota(I32, (C2, C2), 0)
    col = lax.broadcasted_iota(I32, (C2, C2), 1)
    lower = (col & (C - 1)) <= (row & (C - 1))
    strict = (col & (C - 1)) < (row & (C - 1))
    eye = row == col
    head0 = lax.broadcasted_iota(I32, (C, LANES), 1) < HEAD_DIM
    n_sq = int(np.log2(C)) - 1
    units = [(bi, p) for bi in range(nb) for p in range(N_HEADS // 2)]

    def part(name, bi, p):
        return pre[name][bi * C:(bi + 1) * C, p * LANES:(p + 1) * LANES]

    def stack(x):
        return jnp.concatenate([jnp.where(head0, x, 0.0), jnp.where(head0, 0.0, x)], axis=0)

    def unstack(x):
        return x[0:C] + x[C:C2]

    def head_sum(x):
        s0 = jnp.sum(jnp.where(head0, x, 0.0), axis=1, keepdims=True)
        s1 = jnp.sum(jnp.where(head0, 0.0, x), axis=1, keepdims=True)
        return jnp.where(head0, s0, s1)

    ops = []
    for bi, p in units:
        kk_p = part("kk", bi, p)
        inv = 1.0 / jnp.maximum(jnp.sqrt(head_sum(kk_p * kk_p)), 1e-12)
        ops.append(dict(rt=stack(part("rt", bi, p)), at=stack(part("at", bi, p) * inv),
                        kh=stack(part("kh", bi, p)).astype(BF16), bh=stack(part("bh", bi, p) * inv).astype(BF16),
                        kb=stack(part("kb", bi, p)).astype(BF16), bb=stack(part("bb", bi, p) * inv).astype(BF16),
                        v=stack(part("v", bi, p)).astype(BF16)))
    ra = [jnp.concatenate([o["rt"], o["at"]], axis=0).astype(BF16) for o in ops]
    gg = [_mm_nt(x, jnp.concatenate([o["kh"], o["bh"]], axis=0)) for x, o in zip(ra, ops)]
    g1 = [t[:, 0:C2] for t in gg]
    g2 = [t[:, C2:2 * C2] for t in gg]
    a_rb = [jnp.where(lower, t[0:C2], 0.0).astype(BF16) for t in g2]
    pw = [jnp.where(strict, t[C2:2 * C2], 0.0) for t in g2]
    av = [_mm(jnp.concatenate([jnp.where(lower, t[0:C2], 0.0).astype(BF16),
                               jnp.where(strict, t[C2:2 * C2], 0.0).astype(BF16), o["kb"].T], axis=0),
              o["v"]) for t, o in zip(g1, ops)]
    swap = lambda t: pltpu.roll(t, HEAD_DIM, 1)
    own = (row < C) == (col < HEAD_DIM)
    xs = [o["at"] + swap(t[C2:2 * C2]) for o, t in zip(ops, av)]
    for level in range(n_sq + 1):
        pb = [p.astype(BF16) for p in pw]
        xs = [x + _mm(p, x) for p, x in zip(pb, xs)]
        if level < n_sq:
            pw = [jnp.dot(p, p, preferred_element_type=F32) for p in pb]
    rbx = [_mm(jnp.concatenate([p, o["bb"].T], axis=0), x) for p, o, x in zip(a_rb, ops, xs)]
    for u, (bi, p) in enumerate(units):
        o = ops[u]
        sl = slice(p * LANES, (p + 1) * LANES)
        rb, bx, kv = rbx[u][0:C2], rbx[u][C2:2 * C2], av[u][2 * C2:3 * C2]
        r2 = unstack(o["rt"] + jnp.where(own, rb, 0.0))
        y0 = swap(unstack(jnp.where(own, 0.0, rb))) + unstack(av[u][0:C2])
        m_mat = jnp.where(eye, part("pend", bi, p)[0:1, :], 0.0) + jnp.where(own, bx, 0.0)
        g_mat = swap(jnp.where(own, 0.0, bx)) + kv
        out = _mm(jnp.concatenate([r2, m_mat], axis=0), state[bi, p])
        state[bi, p] = out[C:C + LANES] + g_mat
        y = out[0:C] + y0
        cen = y - head_sum(y) * (1.0 / HEAD_DIM)
        var = head_sum(cen * cen) * (1.0 / HEAD_DIM)
        bonus = head_sum(part("rkr", bi, p)) * part("v", bi, p)
        y = cen * lax.rsqrt(var + LNX_EPS) * lng_ref[:, sl] + lnb_ref[:, sl] + bonus
        ybuf[bi, :, sl] = y * part("g", bi, p)
    o_ref[...] = ybuf[...].astype(BF16)


def _rwkv(zm, zs, p, B, T):
    N = zm.shape[0]
    C = RWKV_CHUNK
    assert 2 * C == LANES and T % C == 0, "a head pair's stacked chunk must fill one 128-row tile"
    nc = T // C
    nb = RWKV_SEQS_PER_STEP if B % RWKV_SEQS_PER_STEP == 0 else 1
    tri = np.kron(np.eye(nb, dtype=np.float32), np.tril(np.ones((C, C), np.float32)))
    pad_rows = lambda w, rows: jnp.zeros((rows, HALF), F32).at[:w.shape[0]].set(w)
    row = lambda a: a.reshape(1, HALF)
    consts = [row(p["rw_w0"]), pad_rows(p["rw_decay_up"], LANES), row(p["rw_a0"]),
              pad_rows(p["rw_iclr_up"], LANES), pad_rows(p["rw_gate_up"], 2 * LANES),
              row(p["rw_k_k"]), row(p["rw_k_a"]), row(p["rw_r_k"]), row(p["rw_lnx_g"]),
              row(p["rw_lnx_b"]), jnp.asarray(tri, BF16)]
    full = lambda a: pl.BlockSpec(a.shape, lambda b, c: (0, 0))
    rcol = 4 * HALF // HALF
    zm3 = zm.reshape(B, T, Z_MAIN)
    out = pl.pallas_call(
        functools.partial(_rwkv_kernel, chunk=C, nb=nb),
        grid=(B // nb, nc),
        in_specs=[pl.BlockSpec((nb, C, HALF), lambda b, c: (b, c, rcol)),
                  pl.BlockSpec((nb, C, HALF), lambda b, c: (b, c, rcol + 1)),
                  pl.BlockSpec((nb, C, HALF), lambda b, c: (b, c, rcol + 2)),
                  pl.BlockSpec((nb, C, Z_SMALL), lambda b, c: (b, c, 0))] + [full(a) for a in consts],
        out_specs=pl.BlockSpec((nb, C, HALF), lambda b, c: (b, c, 0)),
        out_shape=jax.ShapeDtypeStruct((B, T, HALF), BF16),
        scratch_shapes=[pltpu.VMEM((nb, N_HEADS // 2, LANES, LANES), F32), pltpu.VMEM((nb, C, HALF), F32)],
        compiler_params=_cparams(("arbitrary", "arbitrary")),
        name="rwkv",
    )(zm3, zm3, zm3, zs.reshape(B, T, Z_SMALL), *consts)
    return out.reshape(N, HALF)


def _first_index(mask, iota, big):
    return jnp.min(jnp.where(mask, iota, big), axis=0, keepdims=True)


def _outproj_kernel(x_ref, yf_ref, yr_ref, g1_ref, sh2_ref, sc2_ref, g2_ref, n2g_ref, wo_ref, rwt_ref,
                    rb_ref, swg_ref, swu_ref, swd_ref, ustrict_ref,
                    xs_ref, hp_ref, idx_ref, wts_ref, rank_ref, *, tm):
    d = (jnp.dot(yf_ref[...], wo_ref[0:HALF, :], preferred_element_type=F32)
         + jnp.dot(yr_ref[...], wo_ref[HALF:2 * HALF, :], preferred_element_type=F32))
    x2 = x_ref[...] + g1_ref[...] * d
    h = x2 * lax.rsqrt(jnp.mean(x2 * x2, axis=-1, keepdims=True) + RMS_EPS) * n2g_ref[...]
    h = h * (1.0 + sc2_ref[...]) + sh2_ref[...]
    hb = h.astype(BF16)
    act = _silu(jnp.dot(hb, swg_ref[...], preferred_element_type=F32)) * jnp.dot(
        hb, swu_ref[...], preferred_element_type=F32)
    shared = jnp.dot(act.astype(BF16), swd_ref[...], preferred_element_type=F32)
    xs_ref[...] = x2 + g2_ref[...] * shared
    half = D_MODEL // 2
    hp_ref[...] = _pack_halves(h)

    logits = lax.dot_general(rwt_ref[...], h, (((1,), (1,)), ((), ())), precision=HIGHEST,
                             preferred_element_type=F32)
    scores = _sigmoid(logits)
    sel = scores + rb_ref[:, 0:1]
    neg = -jnp.inf
    sel3 = sel.reshape(N_GROUPS, GROUP_SIZE, tm)
    io_in = lax.broadcasted_iota(I32, (N_GROUPS, GROUP_SIZE, tm), 1)
    m1 = jnp.max(sel3, axis=1, keepdims=True)
    f1 = jnp.min(jnp.where(sel3 == m1, io_in, GROUP_SIZE), axis=1, keepdims=True)
    m2 = jnp.max(jnp.where(io_in == f1, neg, sel3), axis=1, keepdims=True)
    gs = (m1 + m2).reshape(N_GROUPS, tm)
    io_g = lax.broadcasted_iota(I32, (N_GROUPS, tm), 0)
    gmask = jnp.zeros((N_GROUPS, tm), jnp.bool_)
    for _ in range(TOPK_GROUPS):
        mg = jnp.max(gs, axis=0, keepdims=True)
        fg = _first_index(gs == mg, io_g, N_GROUPS)
        pick = io_g == fg
        gmask = jnp.logical_or(gmask, pick)
        gs = jnp.where(pick, neg, gs)
    emask = jnp.broadcast_to(gmask.reshape(N_GROUPS, 1, tm), (N_GROUPS, GROUP_SIZE, tm)).reshape(N_EXPERTS, tm)
    cur = jnp.where(emask, sel, neg)
    io_e = lax.broadcasted_iota(I32, (N_EXPERTS, tm), 0)
    picks, idxs, wts = [], [], []
    for _ in range(TOP_K):
        me = jnp.max(cur, axis=0, keepdims=True)
        fe = _first_index(cur == me, io_e, N_EXPERTS)
        pick = io_e == fe
        picks.append(pick)
        idxs.append(fe)
        wts.append(jnp.sum(jnp.where(pick, scores, 0.0), axis=0, keepdims=True))
        cur = jnp.where(pick, neg, cur)
    wsum = wts[0]
    for w in wts[1:]:
        wsum = wsum + w
    zero_i = jnp.zeros((1, tm), I32)
    zero_f = jnp.zeros((1, tm), F32)
    idx_ref[...] = jnp.concatenate(idxs + [zero_i, zero_i], axis=0)
    wts_ref[...] = jnp.concatenate([w / wsum * ROUTED_SCALE for w in wts] + [zero_f, zero_f], axis=0)
    cnt = picks[0].astype(F32)
    for pk in picks[1:]:
        cnt = cnt + pk.astype(F32)
    excl = jnp.dot(cnt.astype(BF16), ustrict_ref[...], preferred_element_type=F32)
    ranks = [jnp.sum(jnp.where(pk, excl, 0.0), axis=0, keepdims=True).astype(I32) for pk in picks]
    rank_ref[...] = jnp.concatenate(ranks + [zero_i, zero_i], axis=0)


def _outproj(x2d, yf, yr, mod4, norm2_g, w_out, router_w, router_bias, swg, swu, swd, T):
    N, D = x2d.shape
    tm = min(512, T)
    tps = T // tm
    vec = lambda j: pl.BlockSpec((None, None, 1, D), lambda i: (i // tps, j, 0, 0))
    full = lambda a: pl.BlockSpec(a.shape, lambda i: (0, 0))
    ts = min(MOE_TILE, tm)
    ustrict = jnp.asarray(np.kron(np.eye(tm // ts, dtype=np.float32),
                                  np.triu(np.ones((ts, ts), np.float32), 1)), BF16)
    rb = jnp.broadcast_to(router_bias.reshape(N_EXPERTS, 1), (N_EXPERTS, LANES))
    consts = [norm2_g.reshape(1, D), w_out.astype(BF16), router_w.T, rb, swg.astype(BF16),
              swu.astype(BF16), swd.astype(BF16), ustrict]
    small = lambda dt: jax.ShapeDtypeStruct((8, N), dt)
    return pl.pallas_call(
        functools.partial(_outproj_kernel, tm=tm),
        grid=(N // tm,),
        in_specs=[pl.BlockSpec((tm, D), lambda i: (i, 0)),
                  pl.BlockSpec((tm, HALF), lambda i: (i, 0)),
                  pl.BlockSpec((tm, HALF), lambda i: (i, 0)),
                  vec(2), vec(3), vec(4), vec(5)] + [full(a) for a in consts],
        out_specs=[pl.BlockSpec((tm, D), lambda i: (i, 0)),
                   pl.BlockSpec((tm, D // 2), lambda i: (i, 0)),
                   pl.BlockSpec((8, tm), lambda i: (0, i)),
                   pl.BlockSpec((8, tm), lambda i: (0, i)),
                   pl.BlockSpec((8, tm), lambda i: (0, i))],
        out_shape=[jax.ShapeDtypeStruct((N, D), F32), jax.ShapeDtypeStruct((N, D // 2), jnp.uint32),
                   small(I32), small(F32), small(I32)],
        compiler_params=_cparams(("arbitrary",)),
        name="outproj",
    )(x2d, yf, yr, mod4, mod4, mod4, mod4, *consts)


def _for_each(count, fn, group):
    full = count // group

    def trip(t, c):
        for u in range(group):
            fn(t * group + u)
        return c

    def single(g, c):
        fn(g)
        return c

    lax.fori_loop(0, full, trip, 0)
    lax.fori_loop(full * group, count, single, 0)


def _wait_granules(count, wait_rows):
    full = count // WAIT_CHUNK

    def chunk(_, c):
        wait_rows(WAIT_CHUNK * GRANULE)
        return c

    def single(_, c):
        wait_rows(GRANULE)
        return c

    lax.fori_loop(0, full, chunk, 0)
    lax.fori_loop(full * WAIT_CHUNK, count, single, 0)


def _moe_plan(idx_t, rank_t):
    n = idx_t.shape[1]
    n_tiles = n // MOE_TILE
    experts = jnp.arange(N_EXPERTS, dtype=I32)
    hot = idx_t[:TOP_K, :, None] == experts
    cnt = jnp.sum(hot.reshape(TOP_K, n_tiles, MOE_TILE, N_EXPERTS).astype(I32), axis=(0, 2))
    gran = (cnt + GRANULE - 1) // GRANULE
    loc_end = jnp.cumsum(gran, axis=1)
    loc_off = loc_end - gran
    g_tile = loc_end[:, N_EXPERTS - 1]
    padded = (GRANULE * jnp.sum(gran, axis=0) + EXPERT_BLOCK - 1) // EXPERT_BLOCK * EXPERT_BLOCK
    pad_end = jnp.cumsum(padded)
    glob_off = (pad_end - padded)[None, :] + GRANULE * (jnp.cumsum(gran, axis=0) - gran)
    loc_tok = jnp.repeat(GRANULE * loc_off, MOE_TILE, axis=0)
    pos = rank_t[:TOP_K] + jnp.sum(jnp.where(hot, loc_tok[None], 0), axis=-1)
    pos = jnp.concatenate([pos, jnp.full((8 - TOP_K, n), -1, I32)], axis=0)
    g = jnp.arange(MOE_TILE, dtype=I32)
    e_of_g = jnp.minimum(jnp.sum((loc_end[:, None, :] <= g[None, :, None]).astype(I32), axis=-1), N_EXPERTS - 1)
    dst = jnp.sum(jnp.where(e_of_g[:, :, None] == experts,
                            glob_off[:, None, :] + GRANULE * (g[None, :, None] - loc_off[:, None, :]), 0), axis=-1)
    n_blocks = -(-(n * TOP_K + GRANULE * N_EXPERTS * n_tiles) // EXPERT_BLOCK) + N_EXPERTS
    n_rows = n_blocks * EXPERT_BLOCK
    live = g[None, :] < g_tile[:, None]
    spare = n_rows + (jnp.arange(n_tiles, dtype=I32)[:, None] % 2) * (MOE_TILE * GRANULE) + g[None, :] * GRANULE
    gtab_in = jnp.where(live, dst, 0).reshape(n_tiles, 1, MOE_TILE)
    gtab_out = jnp.where(live, dst, spare).reshape(n_tiles, 1, MOE_TILE)
    blk_start = jnp.arange(n_blocks, dtype=I32) * EXPERT_BLOCK
    blk_e = jnp.minimum(jnp.sum((pad_end[None, :] <= blk_start[:, None]).astype(I32), axis=1), N_EXPERTS - 1)
    plan = dict(pos=pos.astype(I32), gtab_in=gtab_in.astype(I32), gtab_out=gtab_out.astype(I32),
                pad_end=pad_end.astype(I32), padded=padded.astype(I32), blk_e=blk_e.astype(I32),
                n_used=(pad_end[N_EXPERTS - 1:] // EXPERT_BLOCK).astype(I32))
    return plan, n_rows


def _dispatch_kernel(pend_ref, padded_ref, nu_ref, gtab_ref, pos_ref, hp_ref, xs_ref, zeros, sbuf,
                     sem, zsem, *, n_blocks, n_tiles):
    @pl.when(pl.program_id(0) == 0)
    def _():
        zeros[...] = jnp.zeros_like(zeros)

        def block_copy(start):
            return pltpu.make_async_copy(zeros, xs_ref.at[pl.ds(pl.multiple_of(start, EXPERT_BLOCK),
                                                                 EXPERT_BLOCK), :], zsem)

        def pad_loop(fn):
            def body(e, _):
                @pl.when(padded_ref[e] > 0)
                def _():
                    fn(block_copy(pend_ref[e] - EXPERT_BLOCK))
                return 0
            lax.fori_loop(0, N_EXPERTS, body, 0)

        def tail_loop(fn):
            def body(b, _):
                fn(block_copy(b * EXPERT_BLOCK))
                return 0
            lax.fori_loop(nu_ref[0], n_blocks + MOE_SPARE_ROWS // EXPERT_BLOCK, body, 0)

        pad_loop(lambda cp: cp.start())
        tail_loop(lambda cp: cp.start())
        pad_loop(lambda cp: cp.wait())
        tail_loop(lambda cp: cp.wait())

    i = pl.program_id(0)
    slot = i % 2

    def wait_tile(which):
        pltpu.make_async_copy(sbuf.at[which], xs_ref.at[pl.ds(0, MOE_SORTED_ROWS), :], sem.at[which]).wait()

    @pl.when(i >= 2)
    def _():
        wait_tile(slot)

    pos = pos_ref[...]
    lo, hi = _unpack_halves(hp_ref[...])
    lo = lo.astype(BF16)
    hi = hi.astype(BF16)
    r_iota = lax.broadcasted_iota(I32, (MOE_ROW_CHUNK, MOE_TILE), 0).astype(F32).astype(BF16)
    one = jnp.ones((), BF16)
    for c in range(MOE_SORTED_ROWS // MOE_ROW_CHUNK):
        rel = pos - c * MOE_ROW_CHUNK
        rel = jnp.where(jnp.logical_and(rel >= 0, rel < MOE_ROW_CHUNK), rel, -1).astype(F32)
        perm = jnp.zeros((MOE_ROW_CHUNK, MOE_TILE), BF16)
        for k in range(TOP_K):
            perm = jnp.where(r_iota == rel[k:k + 1, :].astype(BF16), one, perm)
        sbuf[slot, c * MOE_ROW_CHUNK:(c + 1) * MOE_ROW_CHUNK, :] = _pack_bf16_values(
            jnp.dot(perm, lo, preferred_element_type=F32), jnp.dot(perm, hi, preferred_element_type=F32))
        for g in range(c * MOE_ROW_CHUNK // GRANULE, (c + 1) * MOE_ROW_CHUNK // GRANULE):
            pltpu.make_async_copy(sbuf.at[slot, pl.ds(g * GRANULE, GRANULE), :],
                                  xs_ref.at[pl.ds(pl.multiple_of(gtab_ref[0, 0, g], GRANULE), GRANULE), :],
                                  sem.at[slot]).start()

    @pl.when(i == n_tiles - 1)
    def _():
        wait_tile(slot)
        if n_tiles > 1:
            wait_tile(1 - slot)


def _dispatch(hp, plan, n_rows):
    N, W = hp.shape
    n_tiles = N // MOE_TILE
    n_blocks = n_rows // EXPERT_BLOCK
    return pl.pallas_call(
        functools.partial(_dispatch_kernel, n_blocks=n_blocks, n_tiles=n_tiles),
        grid_spec=pltpu.PrefetchScalarGridSpec(
            num_scalar_prefetch=3, grid=(n_tiles,),
            in_specs=[pl.BlockSpec((1, 1, MOE_TILE), lambda i, *_: (i, 0, 0), memory_space=pltpu.SMEM),
                      pl.BlockSpec((8, MOE_TILE), lambda i, *_: (0, i)),
                      pl.BlockSpec((MOE_TILE, W), lambda i, *_: (i, 0))],
            out_specs=pl.BlockSpec(memory_space=pl.ANY),
            scratch_shapes=[pltpu.VMEM((EXPERT_BLOCK, W), jnp.uint32),
                            pltpu.VMEM((2, MOE_SORTED_ROWS, W), jnp.uint32),
                            pltpu.SemaphoreType.DMA((2,)), pltpu.SemaphoreType.DMA]),
        out_shape=jax.ShapeDtypeStruct((n_rows + MOE_SPARE_ROWS, W), jnp.uint32),
        compiler_params=_cparams(("arbitrary",)),
        name="dispatch",
    )(plan["pad_end"], plan["padded"], plan["n_used"], plan["gtab_out"], plan["pos"], hp)


def _experts_kernel(be_ref, nu_ref, xs_ref, wgf_ref, wuf_ref, wdf_ref, ys_ref, wg_ref, wu_ref, wd_ref):
    i = pl.program_id(0)
    live = i < nu_ref[0]
    new_expert = jnp.logical_or(i == 0, be_ref[i] != be_ref[jnp.maximum(i - 1, 0)])

    @pl.when(jnp.logical_not(live))
    def _():
        ys_ref[...] = jnp.zeros_like(ys_ref)

    @pl.when(jnp.logical_and(live, new_expert))
    def _():
        wg_ref[...] = wgf_ref[...].astype(BF16)
        wu_ref[...] = wuf_ref[...].astype(BF16)
        wd_ref[...] = wdf_ref[...].astype(BF16)

    @pl.when(live)
    def _():
        half = D_MODEL // 2
        lo, hi = _unpack_halves(xs_ref[...])
        lo = lo.astype(BF16)
        hi = hi.astype(BF16)
        gate = (jnp.dot(lo, wg_ref[0:half, :], preferred_element_type=F32)
                + jnp.dot(hi, wg_ref[half:D_MODEL, :], preferred_element_type=F32))
        up = (jnp.dot(lo, wu_ref[0:half, :], preferred_element_type=F32)
              + jnp.dot(hi, wu_ref[half:D_MODEL, :], preferred_element_type=F32))
        y = jnp.dot((_silu(gate) * up).astype(BF16), wd_ref[...], preferred_element_type=F32)
        ys_ref[...] = _pack_halves(y)


def _experts(xs, n_rows, blk_e, n_used, wg, wu, wd):
    W = xs.shape[1]
    n_blocks = n_rows // EXPERT_BLOCK
    row_map = lambda i, be, nu: (jnp.minimum(i, nu[0] - 1), 0)
    return pl.pallas_call(
        _experts_kernel,
        grid_spec=pltpu.PrefetchScalarGridSpec(
            num_scalar_prefetch=2, grid=(n_blocks,),
            in_specs=[pl.BlockSpec((EXPERT_BLOCK, W), row_map),
                      pl.BlockSpec((None, D_MODEL, D_EXPERT), lambda i, be, nu: (be[i], 0, 0)),
                      pl.BlockSpec((None, D_MODEL, D_EXPERT), lambda i, be, nu: (be[i], 0, 0)),
                      pl.BlockSpec((None, D_EXPERT, D_MODEL), lambda i, be, nu: (be[i], 0, 0))],
            out_specs=pl.BlockSpec((EXPERT_BLOCK, W), lambda i, be, nu: (i, 0)),
            scratch_shapes=[pltpu.VMEM((D_MODEL, D_EXPERT), BF16), pltpu.VMEM((D_MODEL, D_EXPERT), BF16),
                            pltpu.VMEM((D_EXPERT, D_MODEL), BF16)]),
        out_shape=jax.ShapeDtypeStruct((n_rows, W), jnp.uint32),
        compiler_params=_cparams(("arbitrary",)),
        name="experts",
    )(blk_e, n_used, xs, wg, wu, wd)


def _combine_kernel(gtab_ref, gtab_next_ref, ys_ref, pos_ref, xs_ref, wts_ref, g2_ref, fg_ref, o_ref,
                    buf, sem, *, n_tiles):
    i = pl.program_id(0)
    slot = i % 2
    gran_per_chunk = MOE_ROW_CHUNK // GRANULE

    def fetch(which, table_ref, g):
        pltpu.make_async_copy(ys_ref.at[pl.ds(pl.multiple_of(table_ref[0, 0, g], GRANULE), GRANULE), :],
                              buf.at[which, pl.ds(g * GRANULE, GRANULE), :], sem.at[which]).start()

    def wait_tile(which):
        pltpu.make_async_copy(ys_ref.at[pl.ds(0, MOE_SORTED_ROWS), :], buf.at[which], sem.at[which]).wait()

    @pl.when(i == 0)
    def _():
        for g in range(MOE_TILE):
            fetch(0, gtab_ref, g)

    wait_tile(slot)

    pos = pos_ref[...]
    w = wts_ref[...]
    c_iota = lax.broadcasted_iota(I32, (MOE_TILE, MOE_ROW_CHUNK), 1).astype(F32).astype(BF16)
    half = D_MODEL // 2
    r_lo = jnp.zeros((MOE_TILE, half), F32)
    r_hi = jnp.zeros((MOE_TILE, half), F32)
    for c in range(MOE_SORTED_ROWS // MOE_ROW_CHUNK):
        rel = pos - c * MOE_ROW_CHUNK
        rel = jnp.where(jnp.logical_and(rel >= 0, rel < MOE_ROW_CHUNK), rel, -1).astype(F32)
        wb = jnp.zeros((MOE_TILE, MOE_ROW_CHUNK), BF16)
        for k in range(TOP_K):
            wb = jnp.where(c_iota == rel[:, k:k + 1].astype(BF16), w[:, k:k + 1].astype(BF16), wb)
        lo, hi = _unpack_halves(buf[slot, c * MOE_ROW_CHUNK:(c + 1) * MOE_ROW_CHUNK, :])
        r_lo = r_lo + jnp.dot(wb, lo.astype(BF16), preferred_element_type=F32)
        r_hi = r_hi + jnp.dot(wb, hi.astype(BF16), preferred_element_type=F32)
        for g in range(c * gran_per_chunk, (c + 1) * gran_per_chunk):
            fetch(1 - slot, gtab_next_ref, g)
    x3 = xs_ref[...] + g2_ref[...] * jnp.concatenate([r_lo, r_hi], axis=1)
    y = x3 * lax.rsqrt(jnp.mean(x3 * x3, axis=-1, keepdims=True) + RMS_EPS) * fg_ref[...]
    o_ref[...] = y

    @pl.when(i == n_tiles - 1)
    def _():
        wait_tile(1 - slot)


def _combine(ys, plan, xsr, wts_t, mod4, final_g, T):
    N, D = xsr.shape
    W = ys.shape[1]
    tm = MOE_TILE
    tps = T // tm
    n_tiles = N // tm
    tile_tab = lambda off: pl.BlockSpec((1, 1, MOE_TILE), lambda i: (jnp.minimum(i + off, n_tiles - 1), 0, 0),
                                        memory_space=pltpu.SMEM)
    return pl.pallas_call(
        functools.partial(_combine_kernel, n_tiles=n_tiles),
        grid=(n_tiles,),
        in_specs=[tile_tab(0), tile_tab(1),
                  pl.BlockSpec(memory_space=pl.ANY),
                  pl.BlockSpec((tm, 8), lambda i: (i, 0)),
                  pl.BlockSpec((tm, D), lambda i: (i, 0)),
                  pl.BlockSpec((tm, 8), lambda i: (i, 0)),
                  pl.BlockSpec((None, None, 1, D), lambda i: (i // tps, 5, 0, 0)),
                  pl.BlockSpec((1, D), lambda i: (0, 0))],
        out_specs=pl.BlockSpec((tm, D), lambda i: (i, 0)),
        out_shape=jax.ShapeDtypeStruct((N, D), F32),
        scratch_shapes=[pltpu.VMEM((2, MOE_SORTED_ROWS, W), jnp.uint32), pltpu.SemaphoreType.DMA((2,))],
        compiler_params=_cparams(("arbitrary",)),
        name="combine",
    )(plan["gtab_in"], plan["gtab_in"], ys, plan["pos"].T, xsr, wts_t, mod4, final_g.reshape(1, D))


def _pack_w_in(w_in):
    D = w_in.shape[0]
    fox_cols = 4 * HALF + 3 * N_HEADS
    wf = w_in[:, :fox_cols]
    wr = w_in[:, fox_cols:]
    o = 3 * HALF
    pad = lambda a, n: jnp.concatenate([a, jnp.zeros((D, n - a.shape[1]), a.dtype)], axis=1)
    parts = [wf[:, :4 * HALF], wr[:, :o],
             pad(wf[:, 4 * HALF:], LANES),
             pad(wr[:, o:o + DECAY_LORA], LANES),
             pad(wr[:, o + DECAY_LORA:o + DECAY_LORA + ICLR_LORA], LANES),
             pad(wr[:, o + DECAY_LORA + ICLR_LORA:], 2 * LANES)]
    return jnp.concatenate(parts, axis=1).astype(BF16)


def _pack_mu(mu):
    o = 3 * HALF
    pad = lambda a, n: jnp.concatenate([a, jnp.zeros((n - a.shape[0],), a.dtype)])
    small = jnp.concatenate([jnp.zeros((LANES,), mu.dtype),
                             pad(mu[o:o + DECAY_LORA], LANES),
                             pad(mu[o + DECAY_LORA:o + DECAY_LORA + ICLR_LORA], LANES),
                             pad(mu[o + DECAY_LORA + ICLR_LORA:], 2 * LANES)])
    return mu[:o].reshape(1, o), small.reshape(1, Z_SMALL)


def kernel(x, c, norm1_g, norm2_g, ada_w, ada_b, w_in, w_out, fox_qn_g, fox_kn_g, fox_on_g, fox_forget_b,
           rw_mu, rw_w0, rw_decay_up, rw_a0, rw_iclr_up, rw_gate_up, rw_k_k, rw_k_a, rw_r_k, rw_lnx_g,
           rw_lnx_b, router_w, router_bias, exp_w_gate, exp_w_up, exp_w_down, sh_w_gate, sh_w_up,
           sh_w_down, final_g):
    B, T, D = x.shape
    N = B * T
    depth = norm1_g.shape[0]
    assert depth == 1, "the combine kernel fuses the final RMSNorm, so exactly one layer is supported"
    xf = x.reshape(N, D)
    for l in range(depth):
        mod4 = _ada(c, ada_w[l], ada_b[l]).reshape(B, 6, 1, D)
        mu_big, mu_small = _pack_mu(rw_mu[l])
        zm, zs = _inproj(xf, mod4, norm1_g[l], _pack_w_in(w_in[l]), mu_big, mu_small, T)
        qp, kp, vp = _foxprep(zm, zs, fox_forget_b[l], fox_qn_g[l], fox_kn_g[l], T)
        y_fox = _attention(qp, kp, vp, zm, fox_on_g[l], B, T)
        rw = dict(rw_w0=rw_w0[l], rw_decay_up=rw_decay_up[l], rw_a0=rw_a0[l], rw_iclr_up=rw_iclr_up[l],
                  rw_gate_up=rw_gate_up[l], rw_k_k=rw_k_k[l], rw_k_a=rw_k_a[l], rw_r_k=rw_r_k[l],
                  rw_lnx_g=rw_lnx_g[l], rw_lnx_b=rw_lnx_b[l])
        y_rwkv = _rwkv(zm, zs, rw, B, T)
        xsr, hp, idx_t, wts_t, rank_t = _outproj(
            xf, y_fox, y_rwkv, mod4, norm2_g[l], w_out[l], router_w[l], router_bias[l],
            sh_w_gate[l], sh_w_up[l], sh_w_down[l], T)
        plan, n_rows = _moe_plan(idx_t, rank_t)
        xs = _dispatch(hp, plan, n_rows)
        ys = _experts(xs, n_rows, plan["blk_e"], plan["n_used"], exp_w_gate[l], exp_w_up[l], exp_w_down[l])
        xf = _combine(ys, plan, xsr, wts_t.T, mod4, final_g, T)
    return xf.reshape(B, T, D)
```

```python
import functools

import jax
import jax.numpy as jnp
import numpy as np
from jax import lax
from jax.experimental import pallas as pl
from jax.experimental.pallas import tpu as pltpu

F32 = jnp.float32
BF16 = jnp.bfloat16
I32 = jnp.int32
HIGHEST = lax.Precision.HIGHEST

D_MODEL = 1024
HEAD_DIM = 64
N_HEADS = 8
HALF = N_HEADS * HEAD_DIM
RMS_EPS = 1e-6
LNX_EPS = 64e-5
LOG2E = 1.4426950408889634
DECAY_LORA = 64
ICLR_LORA = 64
GATE_LORA = 160
N_EXPERTS = 64
N_GROUPS = 8
GROUP_SIZE = N_EXPERTS // N_GROUPS
TOPK_GROUPS = 4
TOP_K = 6
D_EXPERT = 256
D_SHARED = 256
ROUTED_SCALE = 2.5
EXPERT_BLOCK = 1024

LANES = 128
Z_MAIN = 4 * HALF + 3 * HALF
Z_SMALL = 5 * LANES
VMEM_LIMIT = 56 * 1024 * 1024
ATTN_BLOCK = 512
ATTN_HEADS = 4
DMA_UNROLL = 4
WAIT_CHUNK = 16
MOE_TILE = 256
GRANULE = 8
MOE_SORTED_ROWS = MOE_TILE * TOP_K + N_EXPERTS * GRANULE
MOE_SPARE_ROWS = 2 * MOE_TILE * GRANULE
FETCH_PER_CHUNK = 96
MOE_ROW_CHUNK = 256
RWKV_CHUNK = 64
RWKV_SEQS_PER_STEP = 4


def _cparams(semantics):
    return pltpu.CompilerParams(dimension_semantics=semantics, vmem_limit_bytes=VMEM_LIMIT)


def _mm(a, b):
    return jnp.dot(a.astype(BF16), b.astype(BF16), preferred_element_type=F32)


def _mm_nt(a, b):
    return lax.dot_general(a.astype(BF16), b.astype(BF16), (((1,), (1,)), ((), ())),
                           preferred_element_type=F32)


def _mm_tn(a, b):
    return lax.dot_general(a.astype(BF16), b.astype(BF16), (((0,), (0,)), ((), ())),
                           preferred_element_type=F32)


def _mm_f32(a, b):
    return jnp.dot(a, b, precision=HIGHEST, preferred_element_type=F32)


def _bf16_pieces(x, passes):
    pieces = []
    for _ in range(passes):
        piece = x.astype(BF16)
        pieces.append(piece)
        x = x - piece.astype(F32)
    return pieces


def _mm_split(m01, x, passes=3):
    return sum(jnp.dot(m01, p, preferred_element_type=F32) for p in _bf16_pieces(x, passes))


def _mm_split_r(x, m01, passes=2):
    return sum(jnp.dot(p, m01, preferred_element_type=F32) for p in _bf16_pieces(x, passes))


def _sigmoid(x):
    return 1.0 / (1.0 + jnp.exp(-x))


def _softplus(x):
    return jnp.maximum(x, 0.0) + jnp.log(1.0 + jnp.exp(-jnp.abs(x)))


def _silu(x):
    return x * _sigmoid(x)


def _pack_halves(x):
    w = x.shape[1] // 2
    bits = lambda t: lax.bitcast_convert_type(t.astype(jnp.bfloat16).astype(F32), jnp.uint32)
    return (bits(x[:, 0:w]) >> 16) | (bits(x[:, w:2 * w]) & jnp.uint32(0xFFFF0000))


def _pack_bf16_values(lo, hi):
    return (lax.bitcast_convert_type(lo, jnp.uint32) >> 16) | lax.bitcast_convert_type(hi, jnp.uint32)


def _unpack_halves(p):
    lo = lax.bitcast_convert_type(p << 16, F32)
    hi = lax.bitcast_convert_type(p & jnp.uint32(0xFFFF0000), F32)
    return lo, hi


def _shift_rows(z, carry_ref, first):
    rows = z.shape[0]
    prev_row = jnp.where(first, 0.0, carry_ref[0:1, :])
    prev = pltpu.roll(z, 1, 0)
    row0 = lax.broadcasted_iota(I32, (rows, 1), 0) == 0
    prev = jnp.where(row0, prev_row, prev)
    carry_ref[0:1, :] = z[rows - 1:rows, :]
    return prev


def _ada_kernel(c_ref, w_ref, b_ref, o_ref):
    o_ref[...] = _mm_f32(_silu(c_ref[...]), w_ref[...]) + b_ref[...]


def _ada(c, ada_w, ada_b):
    B, D = c.shape
    n_out = ada_w.shape[1]
    tn = 512
    return pl.pallas_call(
        _ada_kernel,
        grid=(n_out // tn,),
        in_specs=[pl.BlockSpec((B, D), lambda j: (0, 0)),
                  pl.BlockSpec((D, tn), lambda j: (0, j)),
                  pl.BlockSpec((1, tn), lambda j: (0, j))],
        out_specs=pl.BlockSpec((B, tn), lambda j: (0, j)),
        out_shape=jax.ShapeDtypeStruct((B, n_out), F32),
        compiler_params=_cparams(("arbitrary",)),
        name="ada",
    )(c, ada_w, ada_b.reshape(1, n_out))


def _inproj_kernel(x_ref, g_ref, sh_ref, sc_ref, w_ref, mub_ref, mus_ref, zm_ref, zs_ref,
                   carry_b, carry_s, *, tiles_per_seq):
    first = (pl.program_id(0) % tiles_per_seq) == 0
    x = x_ref[...]
    h = x * lax.rsqrt(jnp.mean(x * x, axis=-1, keepdims=True) + RMS_EPS) * g_ref[...]
    hb = (h * (1.0 + sc_ref[...]) + sh_ref[...]).astype(BF16)
    nf = 4 * HALF
    zm_ref[:, 0:nf] = jnp.dot(hb, w_ref[:, 0:nf], preferred_element_type=F32).astype(BF16)
    zr = jnp.dot(hb, w_ref[:, nf:Z_MAIN], preferred_element_type=F32)
    zr = zr + mub_ref[...] * (_shift_rows(zr, carry_b, first) - zr)
    zm_ref[:, nf:Z_MAIN] = zr.astype(BF16)
    zs = jnp.dot(hb, w_ref[:, Z_MAIN:Z_MAIN + Z_SMALL], preferred_element_type=F32)
    zs_ref[...] = zs + mus_ref[...] * (_shift_rows(zs, carry_s, first) - zs)


def _inproj(x2d, mod4, norm_g, w_all, mu_big, mu_small, T):
    N, D = x2d.shape
    tm = min(512, T)
    tps = T // tm
    vec = lambda j: pl.BlockSpec((None, None, 1, D), lambda i: (i // tps, j, 0, 0))
    return pl.pallas_call(
        functools.partial(_inproj_kernel, tiles_per_seq=tps),
        grid=(N // tm,),
        in_specs=[pl.BlockSpec((tm, D), lambda i: (i, 0)),
                  pl.BlockSpec((1, D), lambda i: (0, 0)),
                  vec(0), vec(1),
                  pl.BlockSpec((D, Z_MAIN + Z_SMALL), lambda i: (0, 0)),
                  pl.BlockSpec((1, 3 * HALF), lambda i: (0, 0)),
                  pl.BlockSpec((1, Z_SMALL), lambda i: (0, 0))],
        out_specs=[pl.BlockSpec((tm, Z_MAIN), lambda i: (i, 0)),
                   pl.BlockSpec((tm, Z_SMALL), lambda i: (i, 0))],
        out_shape=[jax.ShapeDtypeStruct((N, Z_MAIN), BF16),
                   jax.ShapeDtypeStruct((N, Z_SMALL), F32)],
        scratch_shapes=[pltpu.VMEM((8, 3 * HALF), F32), pltpu.VMEM((8, Z_SMALL), F32)],
        compiler_params=_cparams(("arbitrary",)),
        name="inproj",
    )(x2d, norm_g.reshape(1, D), mod4, mod4, w_all, mu_big, mu_small)


def _foxprep_kernel(q_ref, k_ref, v_ref, zs_ref, fb_ref, qg_ref, kg_ref, tri_ref, eexp_ref,
                    esum_ref, e8_ref, plq_ref, plk_ref, plv_ref, cq_ref, ck_ref, cv_ref,
                    qp_ref, kp_ref, vp_ref, carry_k, carry_v, carry_c, *, tiles_per_seq):
    first = (pl.program_id(0) % tiles_per_seq) == 0
    zs = zs_ref[...]
    logf = -_softplus(-(zs + fb_ref[...]))
    cum = _mm_split(tri_ref[...], logf) + jnp.where(first, 0.0, carry_c[0:1, :])
    carry_c[0:1, :] = cum[cum.shape[0] - 1:, :]
    cum2 = cum * LOG2E
    c_hi = cum2.astype(BF16)
    r1 = cum2 - c_hi.astype(F32)
    c_mid = r1.astype(BF16)
    c_lo = (r1 - c_mid.astype(F32)).astype(BF16)
    a_full = _mm_split_r(_sigmoid(zs), eexp_ref[...])
    k = k_ref[...].astype(F32)
    v = v_ref[...].astype(F32)
    a_k = a_full[:, 0:HALF]
    a_v = a_full[:, HALF:2 * HALF]
    k = a_k * _shift_rows(k, carry_k, first) + (1.0 - a_k) * k
    v = a_v * _shift_rows(v, carry_v, first) + (1.0 - a_v) * v
    q = q_ref[...].astype(F32)

    def head_rms(t, gain):
        ms = _mm_split_r(t * t, esum_ref[...]) * (1.0 / HEAD_DIM)
        inv = _mm_split_r(lax.rsqrt(ms + RMS_EPS), e8_ref[...])
        return t * inv * gain

    qn = head_rms(q, qg_ref[...]) * (HEAD_DIM ** -0.5 * LOG2E)
    kn = head_rms(k, kg_ref[...])
    aug = [c_hi, c_mid, c_lo]
    lhs_q = jnp.concatenate([qn.astype(BF16)] + aug, axis=1)
    lhs_k = jnp.concatenate([kn.astype(BF16)] + aug, axis=1)
    qp_ref[...] = (jnp.dot(lhs_q, plq_ref[...], preferred_element_type=F32) + cq_ref[...]).astype(BF16)
    kp_ref[...] = (jnp.dot(lhs_k, plk_ref[...], preferred_element_type=F32) + ck_ref[...]).astype(BF16)
    vp_ref[...] = (jnp.dot(v.astype(BF16), plv_ref[...], preferred_element_type=F32)
                   + cv_ref[...]).astype(BF16)


def _fox_constants(tm):
    hp = N_HEADS * LANES
    eexp = np.zeros((LANES, 2 * HALF), np.float32)
    esum = np.zeros((HALF, LANES), np.float32)
    e8 = np.zeros((LANES, HALF), np.float32)
    plq = np.zeros((HALF + 3 * LANES, hp), np.float32)
    plk = np.zeros((HALF + 3 * LANES, hp), np.float32)
    plv = np.zeros((HALF, hp), np.float32)
    cq = np.zeros((1, hp), np.float32)
    ck = np.zeros((1, hp), np.float32)
    cv = np.zeros((1, hp), np.float32)
    for h in range(N_HEADS):
        sl = slice(h * HEAD_DIM, (h + 1) * HEAD_DIM)
        eexp[8 + h, sl] = 1.0
        eexp[16 + h, HALF + h * HEAD_DIM:HALF + (h + 1) * HEAD_DIM] = 1.0
        esum[sl, h] = 1.0
        e8[h, sl] = 1.0
        base = h * LANES
        for d in range(HEAD_DIM):
            plq[h * HEAD_DIM + d, base + d] = 1.0
            plk[h * HEAD_DIM + d, base + d] = 1.0
            plv[h * HEAD_DIM + d, base + d] = 1.0
        for j in range(3):
            plq[HALF + j * LANES + h, base + HEAD_DIM + j] = 1.0
            plk[HALF + j * LANES + h, base + HEAD_DIM + 3 + j] = -1.0
            cq[0, base + HEAD_DIM + 3 + j] = 1.0
            ck[0, base + HEAD_DIM + j] = 1.0
        cv[0, base + HEAD_DIM] = 1.0
    tri = np.tril(np.ones((tm, tm), np.float32))
    bf = lambda a: jnp.asarray(a, BF16)
    return dict(tri=bf(tri), eexp=bf(eexp), esum=bf(esum), e8=bf(e8),
                plq=bf(plq), plk=bf(plk), plv=bf(plv), cq=jnp.asarray(cq), ck=jnp.asarray(ck),
                cv=jnp.asarray(cv))


def _foxprep(zm, zs, forget_b, qn_g, kn_g, T):
    N = zm.shape[0]
    tm = min(512, T)
    tps = T // tm
    cst = _fox_constants(tm)
    hp = N_HEADS * LANES
    fb = jnp.zeros((1, LANES), F32).at[0, :N_HEADS].set(forget_b)
    full = lambda a: pl.BlockSpec(a.shape, lambda i: (0, 0))
    consts = [fb, qn_g.reshape(1, HALF), kn_g.reshape(1, HALF), cst["tri"], cst["eexp"], cst["esum"],
              cst["e8"], cst["plq"], cst["plk"], cst["plv"], cst["cq"], cst["ck"], cst["cv"]]
    return pl.pallas_call(
        functools.partial(_foxprep_kernel, tiles_per_seq=tps),
        grid=(N // tm,),
        in_specs=[pl.BlockSpec((tm, HALF), lambda i: (i, 0)),
                  pl.BlockSpec((tm, HALF), lambda i: (i, 1)),
                  pl.BlockSpec((tm, HALF), lambda i: (i, 2)),
                  pl.BlockSpec((tm, LANES), lambda i: (i, 0))] + [full(a) for a in consts],
        out_specs=[pl.BlockSpec((tm, hp), lambda i: (i, 0))] * 3,
        out_shape=[jax.ShapeDtypeStruct((N, hp), BF16)] * 3,
        scratch_shapes=[pltpu.VMEM((8, HALF), F32), pltpu.VMEM((8, HALF), F32),
                        pltpu.VMEM((8, LANES), F32)],
        compiler_params=_cparams(("arbitrary",)),
        name="foxprep",
    )(zm, zm, zm, zs, *consts)


def _attn_kernel(q_ref, k_ref, v_ref, g_ref, ong_ref, o_ref, vt_ref, *, tq):
    i = pl.program_id(2)
    n_kv = vt_ref.shape[1]
    heads = range(ATTN_HEADS)
    lanes = [slice(hh * LANES, (hh + 1) * LANES) for hh in heads]

    @pl.when(i == 0)
    def _():
        for hh in heads:
            for c in range(n_kv):
                vt_ref[hh, c] = v_ref[c * tq:(c + 1) * tq, lanes[hh]].T

    key = lax.broadcasted_iota(I32, (tq, tq), 0)
    qry = lax.broadcasted_iota(I32, (tq, tq), 1)
    causal = key <= qry
    qs = [q_ref[:, lanes[hh]] for hh in heads]

    def step(j, carry, masked):
        m, acc = carry
        start = pl.multiple_of(j * tq, tq)
        s = [lax.dot_general(k_ref[pl.ds(start, tq), lanes[hh]], qs[hh], (((1,), (1,)), ((), ())),
                             preferred_element_type=F32) for hh in heads]
        m_out, acc_out = [], []
        for hh in heads:
            sh = jnp.where(causal, s[hh], -jnp.inf) if masked else s[hh]
            m_new = jnp.maximum(m[hh], jnp.max(sh, axis=0, keepdims=True))
            p = jnp.exp2(sh - m_new)
            acc_out.append(jnp.exp2(m[hh] - m_new) * acc[hh]
                           + jnp.dot(vt_ref[hh, j], p.astype(BF16), preferred_element_type=F32))
            m_out.append(m_new)
        return tuple(m_out), tuple(acc_out)

    init = (tuple(jnp.full((1, tq), -jnp.inf, F32) for _ in heads),
            tuple(jnp.zeros((LANES, tq), F32) for _ in heads))
    carry = lax.fori_loop(0, i, functools.partial(step, masked=False), init)
    _, acc = step(i, carry, True)
    chan = lax.broadcasted_iota(I32, (LANES, tq), 0)
    chan_w = jnp.where(chan < HEAD_DIM, 1.0 / HEAD_DIM, jnp.where(chan == HEAD_DIM, RMS_EPS, 0.0))
    lane = lax.broadcasted_iota(I32, (tq, LANES), 1)
    outs = []
    for hh in heads:
        t = jnp.sum(acc[hh] * acc[hh] * chan_w, axis=0, keepdims=True)
        outs.append((acc[hh] * lax.rsqrt(t)).T)
    o = jnp.concatenate([jnp.where(lane < HEAD_DIM, outs[2 * p], pltpu.roll(outs[2 * p + 1], HEAD_DIM, 1))
                         for p in range(ATTN_HEADS // 2)], axis=1)
    y = o * ong_ref[...] * _sigmoid(g_ref[...].astype(F32))
    o_ref[...] = y.astype(BF16)


def _attention(qp, kp, vp, zm, on_g, B, T):
    N = qp.shape[0]
    tq = min(ATTN_BLOCK, T)
    nq = T // tq
    groups = N_HEADS // ATTN_HEADS
    wp = ATTN_HEADS * LANES
    wo = ATTN_HEADS * HEAD_DIM
    g_col0 = 3 * HALF // wo
    return pl.pallas_call(
        functools.partial(_attn_kernel, tq=tq),
        grid=(B, groups, nq),
        in_specs=[pl.BlockSpec((tq, wp), lambda b, p, i: (b * nq + i, p)),
                  pl.BlockSpec((T, wp), lambda b, p, i: (b, p)),
                  pl.BlockSpec((T, wp), lambda b, p, i: (b, p)),
                  pl.BlockSpec((tq, wo), lambda b, p, i: (b * nq + i, g_col0 + p)),
                  pl.BlockSpec((None, 1, wo), lambda b, p, i: (p, 0, 0))],
        out_specs=pl.BlockSpec((tq, wo), lambda b, p, i: (b * nq + i, p)),
        out_shape=jax.ShapeDtypeStruct((N, HALF), BF16),
        scratch_shapes=[pltpu.VMEM((ATTN_HEADS, nq, LANES, tq), BF16)],
        compiler_params=_cparams(("arbitrary", "arbitrary", "arbitrary")),
        name="attn",
    )(qp, kp, vp, zm, on_g.reshape(groups, 1, wo))


def _rwkv_kernel(r_ref, k_ref, v_ref, zs_ref, w0_ref, dup_ref, a0_ref, iup_ref, gup_ref, kk_ref,
                 ka_ref, rk_ref, lng_ref, lnb_ref, tri_ref, o_ref, state, ybuf, *, chunk, nb):
    C = chunk
    c_idx = pl.program_id(1)

    @pl.when(c_idx == 0)
    def _():
        state[...] = jnp.zeros_like(state)

    R = nb * C
    r = r_ref[...].reshape(R, HALF).astype(F32)
    k = k_ref[...].reshape(R, HALF).astype(F32)
    v = v_ref[...].reshape(R, HALF).astype(F32)
    zs = zs_ref[...].reshape(R, Z_SMALL)
    wd = zs[:, LANES:2 * LANES]
    ad = zs[:, 2 * LANES:3 * LANES]
    gd = zs[:, 3 * LANES:5 * LANES]
    wl = w0_ref[...] + _mm(jnp.tanh(wd), dup_ref[...])
    lw = -jnp.exp(-_softplus(-wl) - 0.5)
    a = _sigmoid(a0_ref[...] + _mm(ad, iup_ref[...]))
    g = _mm(_sigmoid(gd), gup_ref[...])
    kk = k * kk_ref[...]
    k2 = k * (1.0 + (a - 1.0) * ka_ref[...])
    cl = _mm_split(tri_ref[...], lw)
    cl_end = jnp.concatenate(
        [jnp.broadcast_to(cl[(bi + 1) * C - 1:(bi + 1) * C, :], (C, HALF)) for bi in range(nb)], axis=0)
    e_neg = jnp.exp(-cl)
    e_tail = jnp.exp(cl_end - cl)
    pre = dict(rt=r * jnp.exp(cl), at=-kk * jnp.exp(cl - lw), kh=k2 * e_neg, bh=kk * a * e_neg,
               kb=k2 * e_tail, bb=kk * a * e_tail, v=v, kk=kk, pend=jnp.exp(cl_end),
               rkr=r * k2 * rk_ref[...], g=g)

    C2 = 2 * C
    row = lax.broadcasted_iota(I32, (C2, C2), 0)
    col = lax.broadcasted_iota(I32, (C2, C2), 1)
    lower = (col & (C - 1)) <= (row & (C - 1))
    strict = (col & (C - 1)) < (row & (C - 1))
    eye = row == col
    head0 = lax.broadcasted_iota(I32, (C, LANES), 1) < HEAD_DIM
    n_sq = int(np.log2(C)) - 1
    units = [(bi, p) for bi in range(nb) for p in range(N_HEADS // 2)]

    def part(name, bi, p):
        return pre[name][bi * C:(bi + 1) * C, p * LANES:(p + 1) * LANES]

    def stack(x):
        return jnp.concatenate([jnp.where(head0, x, 0.0), jnp.where(head0, 0.0, x)], axis=0)

    def unstack(x):
        return x[0:C] + x[C:C2]

    def head_sum(x):
        s0 = jnp.sum(jnp.where(head0, x, 0.0), axis=1, keepdims=True)
        s1 = jnp.sum(jnp.where(head0, 0.0, x), axis=1, keepdims=True)
        return jnp.where(head0, s0, s1)

    ops = []
    for bi, p in units:
        kk_p = part("kk", bi, p)
        inv = 1.0 / jnp.maximum(jnp.sqrt(head_sum(kk_p * kk_p)), 1e-12)
        ops.append(dict(rt=stack(part("rt", bi, p)), at=stack(part("at", bi, p) * inv),
                        kh=stack(part("kh", bi, p)).astype(BF16), bh=stack(part("bh", bi, p) * inv).astype(BF16),
                        kb=stack(part("kb", bi, p)).astype(BF16), bb=stack(part("bb", bi, p) * inv).astype(BF16),
                        v=stack(part("v", bi, p)).astype(BF16)))
    ra = [jnp.concatenate([o["rt"], o["at"]], axis=0).astype(BF16) for o in ops]
    gg = [_mm_nt(x, jnp.concatenate([o["kh"], o["bh"]], axis=0)) for x, o in zip(ra, ops)]
    g1 = [t[:, 0:C2] for t in gg]
    g2 = [t[:, C2:2 * C2] for t in gg]
    a_rb = [jnp.where(lower, t[0:C2], 0.0).astype(BF16) for t in g2]
    pw = [jnp.where(strict, t[C2:2 * C2], 0.0) for t in g2]
    av = [_mm(jnp.concatenate([jnp.where(lower, t[0:C2], 0.0).astype(BF16),
                               jnp.where(strict, t[C2:2 * C2], 0.0).astype(BF16), o["kb"].T], axis=0),
              o["v"]) for t, o in zip(g1, ops)]
    swap = lambda t: pltpu.roll(t, HEAD_DIM, 1)
    own = (row < C) == (col < HEAD_DIM)
    xs = [o["at"] + swap(t[C2:2 * C2]) for o, t in zip(ops, av)]
    for level in range(n_sq + 1):
        pb = [p.astype(BF16) for p in pw]
        xs = [x + _mm(p, x) for p, x in zip(pb, xs)]
        if level < n_sq:
            pw = [jnp.dot(p, p, preferred_element_type=F32) for p in pb]
    rbx = [_mm(jnp.concatenate([p, o["bb"].T], axis=0), x) for p, o, x in zip(a_rb, ops, xs)]
    for u, (bi, p) in enumerate(units):
        o = ops[u]
        sl = slice(p * LANES, (p + 1) * LANES)
        rb, bx, kv = rbx[u][0:C2], rbx[u][C2:2 * C2], av[u][2 * C2:3 * C2]
        r2 = unstack(o["rt"] + jnp.where(own, rb, 0.0))
        y0 = swap(unstack(jnp.where(own, 0.0, rb))) + unstack(av[u][0:C2])
        m_mat = jnp.where(eye, part("pend", bi, p)[0:1, :], 0.0) + jnp.where(own, bx, 0.0)
        g_mat = swap(jnp.where(own, 0.0, bx)) + kv
        out = _mm(jnp.concatenate([r2, m_mat], axis=0), state[bi, p])
        state[bi, p] = out[C:C + LANES] + g_mat
        y = out[0:C] + y0
        cen = y - head_sum(y) * (1.0 / HEAD_DIM)
        var = head_sum(cen * cen) * (1.0 / HEAD_DIM)
        bonus = head_sum(part("rkr", bi, p)) * part("v", bi, p)
        y = cen * lax.rsqrt(var + LNX_EPS) * lng_ref[:, sl] + lnb_ref[:, sl] + bonus
        ybuf[bi, :, sl] = y * part("g", bi, p)
    o_ref[...] = ybuf[...].astype(BF16)


def _rwkv(zm, zs, p, B, T):
    N = zm.shape[0]
    C = RWKV_CHUNK
    assert 2 * C == LANES and T % C == 0, "a head pair's stacked chunk must fill one 128-row tile"
    nc = T // C
    nb = RWKV_SEQS_PER_STEP if B % RWKV_SEQS_PER_STEP == 0 else 1
    tri = np.kron(np.eye(nb, dtype=np.float32), np.tril(np.ones((C, C), np.float32)))
    pad_rows = lambda w, rows: jnp.zeros((rows, HALF), F32).at[:w.shape[0]].set(w)
    row = lambda a: a.reshape(1, HALF)
    consts = [row(p["rw_w0"]), pad_rows(p["rw_decay_up"], LANES), row(p["rw_a0"]),
              pad_rows(p["rw_iclr_up"], LANES), pad_rows(p["rw_gate_up"], 2 * LANES),
              row(p["rw_k_k"]), row(p["rw_k_a"]), row(p["rw_r_k"]), row(p["rw_lnx_g"]),
              row(p["rw_lnx_b"]), jnp.asarray(tri, BF16)]
    full = lambda a: pl.BlockSpec(a.shape, lambda b, c: (0, 0))
    rcol = 4 * HALF // HALF
    zm3 = zm.reshape(B, T, Z_MAIN)
    out = pl.pallas_call(
        functools.partial(_rwkv_kernel, chunk=C, nb=nb),
        grid=(B // nb, nc),
        in_specs=[pl.BlockSpec((nb, C, HALF), lambda b, c: (b, c, rcol)),
                  pl.BlockSpec((nb, C, HALF), lambda b, c: (b, c, rcol + 1)),
                  pl.BlockSpec((nb, C, HALF), lambda b, c: (b, c, rcol + 2)),
                  pl.BlockSpec((nb, C, Z_SMALL), lambda b, c: (b, c, 0))] + [full(a) for a in consts],
        out_specs=pl.BlockSpec((nb, C, HALF), lambda b, c: (b, c, 0)),
        out_shape=jax.ShapeDtypeStruct((B, T, HALF), BF16),
        scratch_shapes=[pltpu.VMEM((nb, N_HEADS // 2, LANES, LANES), F32), pltpu.VMEM((nb, C, HALF), F32)],
        compiler_params=_cparams(("arbitrary", "arbitrary")),
        name="rwkv",
    )(zm3, zm3, zm3, zs.reshape(B, T, Z_SMALL), *consts)
    return out.reshape(N, HALF)


def _first_index(mask, iota, big):
    return jnp.min(jnp.where(mask, iota, big), axis=0, keepdims=True)


def _outproj_kernel(x_ref, yf_ref, yr_ref, g1_ref, sh2_ref, sc2_ref, g2_ref, n2g_ref, wo_ref, rwt_ref,
                    rb_ref, swg_ref, swu_ref, swd_ref, ustrict_ref,
                    xs_ref, hp_ref, idx_ref, wts_ref, rank_ref, *, tm):
    d = (jnp.dot(yf_ref[...], wo_ref[0:HALF, :], preferred_element_type=F32)
         + jnp.dot(yr_ref[...], wo_ref[HALF:2 * HALF, :], preferred_element_type=F32))
    x2 = x_ref[...] + g1_ref[...] * d
    h = x2 * lax.rsqrt(jnp.mean(x2 * x2, axis=-1, keepdims=True) + RMS_EPS) * n2g_ref[...]
    h = h * (1.0 + sc2_ref[...]) + sh2_ref[...]
    hb = h.astype(BF16)
    act = _silu(jnp.dot(hb, swg_ref[...], preferred_element_type=F32)) * jnp.dot(
        hb, swu_ref[...], preferred_element_type=F32)
    shared = jnp.dot(act.astype(BF16), swd_ref[...], preferred_element_type=F32)
    xs_ref[...] = x2 + g2_ref[...] * shared
    half = D_MODEL // 2
    hp_ref[...] = _pack_halves(h)

    logits = lax.dot_general(rwt_ref[...], h, (((1,), (1,)), ((), ())), precision=HIGHEST,
                             preferred_element_type=F32)
    scores = _sigmoid(logits)
    sel = scores + rb_ref[:, 0:1]
    neg = -jnp.inf
    sel3 = sel.reshape(N_GROUPS, GROUP_SIZE, tm)
    io_in = lax.broadcasted_iota(I32, (N_GROUPS, GROUP_SIZE, tm), 1)
    m1 = jnp.max(sel3, axis=1, keepdims=True)
    f1 = jnp.min(jnp.where(sel3 == m1, io_in, GROUP_SIZE), axis=1, keepdims=True)
    m2 = jnp.max(jnp.where(io_in == f1, neg, sel3), axis=1, keepdims=True)
    gs = (m1 + m2).reshape(N_GROUPS, tm)
    io_g = lax.broadcasted_iota(I32, (N_GROUPS, tm), 0)
    gmask = jnp.zeros((N_GROUPS, tm), jnp.bool_)
    for _ in range(TOPK_GROUPS):
        mg = jnp.max(gs, axis=0, keepdims=True)
        fg = _first_index(gs == mg, io_g, N_GROUPS)
        pick = io_g == fg
        gmask = jnp.logical_or(gmask, pick)
        gs = jnp.where(pick, neg, gs)
    emask = jnp.broadcast_to(gmask.reshape(N_GROUPS, 1, tm), (N_GROUPS, GROUP_SIZE, tm)).reshape(N_EXPERTS, tm)
    cur = jnp.where(emask, sel, neg)
    io_e = lax.broadcasted_iota(I32, (N_EXPERTS, tm), 0)
    picks, idxs, wts = [], [], []
    for _ in range(TOP_K):
        me = jnp.max(cur, axis=0, keepdims=True)
        fe = _first_index(cur == me, io_e, N_EXPERTS)
        pick = io_e == fe
        picks.append(pick)
        idxs.append(fe)
        wts.append(jnp.sum(jnp.where(pick, scores, 0.0), axis=0, keepdims=True))
        cur = jnp.where(pick, neg, cur)
    wsum = wts[0]
    for w in wts[1:]:
        wsum = wsum + w
    zero_i = jnp.zeros((1, tm), I32)
    zero_f = jnp.zeros((1, tm), F32)
    idx_ref[...] = jnp.concatenate(idxs + [zero_i, zero_i], axis=0)
    wts_ref[...] = jnp.concatenate([w / wsum * ROUTED_SCALE for w in wts] + [zero_f, zero_f], axis=0)
    cnt = picks[0].astype(F32)
    for pk in picks[1:]:
        cnt = cnt + pk.astype(F32)
    excl = jnp.dot(cnt.astype(BF16), ustrict_ref[...], preferred_element_type=F32)
    ranks = [jnp.sum(jnp.where(pk, excl, 0.0), axis=0, keepdims=True).astype(I32) for pk in picks]
    rank_ref[...] = jnp.concatenate(ranks + [zero_i, zero_i], axis=0)


def _outproj(x2d, yf, yr, mod4, norm2_g, w_out, router_w, router_bias, swg, swu, swd, T):
    N, D = x2d.shape
    tm = min(512, T)
    tps = T // tm
    vec = lambda j: pl.BlockSpec((None, None, 1, D), lambda i: (i // tps, j, 0, 0))
    full = lambda a: pl.BlockSpec(a.shape, lambda i: (0, 0))
    ts = min(MOE_TILE, tm)
    ustrict = jnp.asarray(np.kron(np.eye(tm // ts, dtype=np.float32),
                                  np.triu(np.ones((ts, ts), np.float32), 1)), BF16)
    rb = jnp.broadcast_to(router_bias.reshape(N_EXPERTS, 1), (N_EXPERTS, LANES))
    consts = [norm2_g.reshape(1, D), w_out.astype(BF16), router_w.T, rb, swg.astype(BF16),
              swu.astype(BF16), swd.astype(BF16), ustrict]
    small = lambda dt: jax.ShapeDtypeStruct((8, N), dt)
    return pl.pallas_call(
        functools.partial(_outproj_kernel, tm=tm),
        grid=(N // tm,),
        in_specs=[pl.BlockSpec((tm, D), lambda i: (i, 0)),
                  pl.BlockSpec((tm, HALF), lambda i: (i, 0)),
                  pl.BlockSpec((tm, HALF), lambda i: (i, 0)),
                  vec(2), vec(3), vec(4), vec(5)] + [full(a) for a in consts],
        out_specs=[pl.BlockSpec((tm, D), lambda i: (i, 0)),
                   pl.BlockSpec((tm, D // 2), lambda i: (i, 0)),
                   pl.BlockSpec((8, tm), lambda i: (0, i)),
                   pl.BlockSpec((8, tm), lambda i: (0, i)),
                   pl.BlockSpec((8, tm), lambda i: (0, i))],
        out_shape=[jax.ShapeDtypeStruct((N, D), F32), jax.ShapeDtypeStruct((N, D // 2), jnp.uint32),
                   small(I32), small(F32), small(I32)],
        compiler_params=_cparams(("arbitrary",)),
        name="outproj",
    )(x2d, yf, yr, mod4, mod4, mod4, mod4, *consts)


def _for_each(count, fn, group):
    full = count // group

    def trip(t, c):
        for u in range(group):
            fn(t * group + u)
        return c

    def single(g, c):
        fn(g)
        return c

    lax.fori_loop(0, full, trip, 0)
    lax.fori_loop(full * group, count, single, 0)


def _wait_granules(count, wait_rows):
    full = count // WAIT_CHUNK

    def chunk(_, c):
        wait_rows(WAIT_CHUNK * GRANULE)
        return c

    def single(_, c):
        wait_rows(GRANULE)
        return c

    lax.fori_loop(0, full, chunk, 0)
    lax.fori_loop(full * WAIT_CHUNK, count, single, 0)


def _moe_plan(idx_t, rank_t):
    n = idx_t.shape[1]
    n_tiles = n // MOE_TILE
    experts = jnp.arange(N_EXPERTS, dtype=I32)
    hot = idx_t[:TOP_K, :, None] == experts
    cnt = jnp.sum(hot.reshape(TOP_K, n_tiles, MOE_TILE, N_EXPERTS).astype(I32), axis=(0, 2))
    gran = (cnt + GRANULE - 1) // GRANULE
    loc_end = jnp.cumsum(gran, axis=1)
    loc_off = loc_end - gran
    g_tile = loc_end[:, N_EXPERTS - 1]
    padded = (GRANULE * jnp.sum(gran, axis=0) + EXPERT_BLOCK - 1) // EXPERT_BLOCK * EXPERT_BLOCK
    pad_end = jnp.cumsum(padded)
    glob_off = (pad_end - padded)[None, :] + GRANULE * (jnp.cumsum(gran, axis=0) - gran)
    loc_tok = jnp.repeat(GRANULE * loc_off, MOE_TILE, axis=0)
    pos = rank_t[:TOP_K] + jnp.sum(jnp.where(hot, loc_tok[None], 0), axis=-1)
    pos = jnp.concatenate([pos, jnp.full((8 - TOP_K, n), -1, I32)], axis=0)
    g = jnp.arange(MOE_TILE, dtype=I32)
    e_of_g = jnp.minimum(jnp.sum((loc_end[:, None, :] <= g[None, :, None]).astype(I32), axis=-1), N_EXPERTS - 1)
    dst = jnp.sum(jnp.where(e_of_g[:, :, None] == experts,
                            glob_off[:, None, :] + GRANULE * (g[None, :, None] - loc_off[:, None, :]), 0), axis=-1)
    n_blocks = -(-(n * TOP_K + GRANULE * N_EXPERTS * n_tiles) // EXPERT_BLOCK) + N_EXPERTS
    n_rows = n_blocks * EXPERT_BLOCK
    live = g[None, :] < g_tile[:, None]
    spare = n_rows + (jnp.arange(n_tiles, dtype=I32)[:, None] % 2) * (MOE_TILE * GRANULE) + g[None, :] * GRANULE
    gtab_in = jnp.where(live, dst, 0).reshape(n_tiles, 1, MOE_TILE)
    gtab_out = jnp.where(live, dst, spare).reshape(n_tiles, 1, MOE_TILE)
    blk_start = jnp.arange(n_blocks, dtype=I32) * EXPERT_BLOCK
    blk_e = jnp.minimum(jnp.sum((pad_end[None, :] <= blk_start[:, None]).astype(I32), axis=1), N_EXPERTS - 1)
    plan = dict(pos=pos.astype(I32), gtab_in=gtab_in.astype(I32), gtab_out=gtab_out.astype(I32),
                pad_end=pad_end.astype(I32), padded=padded.astype(I32), blk_e=blk_e.astype(I32),
                n_used=(pad_end[N_EXPERTS - 1:] // EXPERT_BLOCK).astype(I32))
    return plan, n_rows


def _dispatch_kernel(pend_ref, padded_ref, nu_ref, gtab_ref, pos_ref, hp_ref, xs_ref, zeros, sbuf,
                     sem, zsem, *, n_blocks, n_tiles):
    @pl.when(pl.program_id(0) == 0)
    def _():
        zeros[...] = jnp.zeros_like(zeros)

        def block_copy(start):
            return pltpu.make_async_copy(zeros, xs_ref.at[pl.ds(pl.multiple_of(start, EXPERT_BLOCK),
                                                                 EXPERT_BLOCK), :], zsem)

        def pad_loop(fn):
            def body(e, _):
                @pl.when(padded_ref[e] > 0)
                def _():
                    fn(block_copy(pend_ref[e] - EXPERT_BLOCK))
                return 0
            lax.fori_loop(0, N_EXPERTS, body, 0)

        def tail_loop(fn):
            def body(b, _):
                fn(block_copy(b * EXPERT_BLOCK))
                return 0
            lax.fori_loop(nu_ref[0], n_blocks + MOE_SPARE_ROWS // EXPERT_BLOCK, body, 0)

        pad_loop(lambda cp: cp.start())
        tail_loop(lambda cp: cp.start())
        pad_loop(lambda cp: cp.wait())
        tail_loop(lambda cp: cp.wait())

    i = pl.program_id(0)
    slot = i % 2

    def wait_tile(which):
        pltpu.make_async_copy(sbuf.at[which], xs_ref.at[pl.ds(0, MOE_SORTED_ROWS), :], sem.at[which]).wait()

    @pl.when(i >= 2)
    def _():
        wait_tile(slot)

    pos = pos_ref[...]
    lo, hi = _unpack_halves(hp_ref[...])
    lo = lo.astype(BF16)
    hi = hi.astype(BF16)
    r_iota = lax.broadcasted_iota(I32, (MOE_ROW_CHUNK, MOE_TILE), 0).astype(F32).astype(BF16)
    one = jnp.ones((), BF16)
    for c in range(MOE_SORTED_ROWS // MOE_ROW_CHUNK):
        rel = pos - c * MOE_ROW_CHUNK
        rel = jnp.where(jnp.logical_and(rel >= 0, rel < MOE_ROW_CHUNK), rel, -1).astype(F32)
        perm = jnp.zeros((MOE_ROW_CHUNK, MOE_TILE), BF16)
        for k in range(TOP_K):
            perm = jnp.where(r_iota == rel[k:k + 1, :].astype(BF16), one, perm)
        sbuf[slot, c * MOE_ROW_CHUNK:(c + 1) * MOE_ROW_CHUNK, :] = _pack_bf16_values(
            jnp.dot(perm, lo, preferred_element_type=F32), jnp.dot(perm, hi, preferred_element_type=F32))
        for g in range(c * MOE_ROW_CHUNK // GRANULE, (c + 1) * MOE_ROW_CHUNK // GRANULE):
            pltpu.make_async_copy(sbuf.at[slot, pl.ds(g * GRANULE, GRANULE), :],
                                  xs_ref.at[pl.ds(pl.multiple_of(gtab_ref[0, 0, g], GRANULE), GRANULE), :],
                                  sem.at[slot]).start()

    @pl.when(i == n_tiles - 1)
    def _():
        wait_tile(slot)
        if n_tiles > 1:
            wait_tile(1 - slot)


def _dispatch(hp, plan, n_rows):
    N, W = hp.shape
    n_tiles = N // MOE_TILE
    n_blocks = n_rows // EXPERT_BLOCK
    return pl.pallas_call(
        functools.partial(_dispatch_kernel, n_blocks=n_blocks, n_tiles=n_tiles),
        grid_spec=pltpu.PrefetchScalarGridSpec(
            num_scalar_prefetch=3, grid=(n_tiles,),
            in_specs=[pl.BlockSpec((1, 1, MOE_TILE), lambda i, *_: (i, 0, 0), memory_space=pltpu.SMEM),
                      pl.BlockSpec((8, MOE_TILE), lambda i, *_: (0, i)),
                      pl.BlockSpec((MOE_TILE, W), lambda i, *_: (i, 0))],
            out_specs=pl.BlockSpec(memory_space=pl.ANY),
            scratch_shapes=[pltpu.VMEM((EXPERT_BLOCK, W), jnp.uint32),
                            pltpu.VMEM((2, MOE_SORTED_ROWS, W), jnp.uint32),
                            pltpu.SemaphoreType.DMA((2,)), pltpu.SemaphoreType.DMA]),
        out_shape=jax.ShapeDtypeStruct((n_rows + MOE_SPARE_ROWS, W), jnp.uint32),
        compiler_params=_cparams(("arbitrary",)),
        name="dispatch",
    )(plan["pad_end"], plan["padded"], plan["n_used"], plan["gtab_out"], plan["pos"], hp)


def _experts_kernel(be_ref, nu_ref, xs_ref, wgf_ref, wuf_ref, wdf_ref, ys_ref, wg_ref, wu_ref, wd_ref):
    i = pl.program_id(0)
    live = i < nu_ref[0]
    new_expert = jnp.logical_or(i == 0, be_ref[i] != be_ref[jnp.maximum(i - 1, 0)])

    @pl.when(jnp.logical_not(live))
    def _():
        ys_ref[...] = jnp.zeros_like(ys_ref)

    @pl.when(jnp.logical_and(live, new_expert))
    def _():
        wg_ref[...] = wgf_ref[...].astype(BF16)
        wu_ref[...] = wuf_ref[...].astype(BF16)
        wd_ref[...] = wdf_ref[...].astype(BF16)

    @pl.when(live)
    def _():
        half = D_MODEL // 2
        lo, hi = _unpack_halves(xs_ref[...])
        lo = lo.astype(BF16)
        hi = hi.astype(BF16)
        gate = (jnp.dot(lo, wg_ref[0:half, :], preferred_element_type=F32)
                + jnp.dot(hi, wg_ref[half:D_MODEL, :], preferred_element_type=F32))
        up = (jnp.dot(lo, wu_ref[0:half, :], preferred_element_type=F32)
              + jnp.dot(hi, wu_ref[half:D_MODEL, :], preferred_element_type=F32))
        y = jnp.dot((_silu(gate) * up).astype(BF16), wd_ref[...], preferred_element_type=F32)
        ys_ref[...] = _pack_halves(y)


def _experts(xs, n_rows, blk_e, n_used, wg, wu, wd):
    W = xs.shape[1]
    n_blocks = n_rows // EXPERT_BLOCK
    row_map = lambda i, be, nu: (jnp.minimum(i, nu[0] - 1), 0)
    return pl.pallas_call(
        _experts_kernel,
        grid_spec=pltpu.PrefetchScalarGridSpec(
            num_scalar_prefetch=2, grid=(n_blocks,),
            in_specs=[pl.BlockSpec((EXPERT_BLOCK, W), row_map),
                      pl.BlockSpec((None, D_MODEL, D_EXPERT), lambda i, be, nu: (be[i], 0, 0)),
                      pl.BlockSpec((None, D_MODEL, D_EXPERT), lambda i, be, nu: (be[i], 0, 0)),
                      pl.BlockSpec((None, D_EXPERT, D_MODEL), lambda i, be, nu: (be[i], 0, 0))],
            out_specs=pl.BlockSpec((EXPERT_BLOCK, W), lambda i, be, nu: (i, 0)),
            scratch_shapes=[pltpu.VMEM((D_MODEL, D_EXPERT), BF16), pltpu.VMEM((D_MODEL, D_EXPERT), BF16),
                            pltpu.VMEM((D_EXPERT, D_MODEL), BF16)]),
        out_shape=jax.ShapeDtypeStruct((n_rows, W), jnp.uint32),
        compiler_params=_cparams(("arbitrary",)),
        name="experts",
    )(blk_e, n_used, xs, wg, wu, wd)


def _combine_kernel(gtab_ref, gtab_next_ref, ys_ref, pos_ref, xs_ref, wts_ref, g2_ref, fg_ref, o_ref,
                    buf, sem, *, n_tiles):
    i = pl.program_id(0)
    slot = i % 2

    def fetch(which, table_ref, g):
        pltpu.make_async_copy(ys_ref.at[pl.ds(pl.multiple_of(table_ref[0, 0, g], GRANULE), GRANULE), :],
                              buf.at[which, pl.ds(g * GRANULE, GRANULE), :], sem.at[which]).start()

    def wait_tile(which):
        pltpu.make_async_copy(ys_ref.at[pl.ds(0, MOE_SORTED_ROWS), :], buf.at[which], sem.at[which]).wait()

    @pl.when(i == 0)
    def _():
        for g in range(MOE_TILE):
            fetch(0, gtab_ref, g)

    wait_tile(slot)

    pos = pos_ref[...]
    w = wts_ref[...]
    c_iota = lax.broadcasted_iota(I32, (MOE_TILE, MOE_ROW_CHUNK), 1).astype(F32).astype(BF16)
    half = D_MODEL // 2
    r_lo = jnp.zeros((MOE_TILE, half), F32)
    r_hi = jnp.zeros((MOE_TILE, half), F32)
    for c in range(MOE_SORTED_ROWS // MOE_ROW_CHUNK):
        rel = pos - c * MOE_ROW_CHUNK
        rel = jnp.where(jnp.logical_and(rel >= 0, rel < MOE_ROW_CHUNK), rel, -1).astype(F32)
        wb = jnp.zeros((MOE_TILE, MOE_ROW_CHUNK), BF16)
        for k in range(TOP_K):
            wb = jnp.where(c_iota == rel[:, k:k + 1].astype(BF16), w[:, k:k + 1].astype(BF16), wb)
        lo, hi = _unpack_halves(buf[slot, c * MOE_ROW_CHUNK:(c + 1) * MOE_ROW_CHUNK, :])
        r_lo = r_lo + jnp.dot(wb, lo.astype(BF16), preferred_element_type=F32)
        r_hi = r_hi + jnp.dot(wb, hi.astype(BF16), preferred_element_type=F32)
        for g in range(min(c * FETCH_PER_CHUNK, MOE_TILE), min((c + 1) * FETCH_PER_CHUNK, MOE_TILE)):
            fetch(1 - slot, gtab_next_ref, g)
    x3 = xs_ref[...] + g2_ref[...] * jnp.concatenate([r_lo, r_hi], axis=1)
    y = x3 * lax.rsqrt(jnp.mean(x3 * x3, axis=-1, keepdims=True) + RMS_EPS) * fg_ref[...]
    o_ref[...] = y

    @pl.when(i == n_tiles - 1)
    def _():
        wait_tile(1 - slot)


def _combine(ys, plan, xsr, wts_t, mod4, final_g, T):
    N, D = xsr.shape
    W = ys.shape[1]
    tm = MOE_TILE
    tps = T // tm
    n_tiles = N // tm
    tile_tab = lambda off: pl.BlockSpec((1, 1, MOE_TILE), lambda i: (jnp.minimum(i + off, n_tiles - 1), 0, 0),
                                        memory_space=pltpu.SMEM)
    return pl.pallas_call(
        functools.partial(_combine_kernel, n_tiles=n_tiles),
        grid=(n_tiles,),
        in_specs=[tile_tab(0), tile_tab(1),
                  pl.BlockSpec(memory_space=pl.ANY),
                  pl.BlockSpec((tm, 8), lambda i: (i, 0)),
                  pl.BlockSpec((tm, D), lambda i: (i, 0)),
                  pl.BlockSpec((tm, 8), lambda i: (i, 0)),
                  pl.BlockSpec((None, None, 1, D), lambda i: (i // tps, 5, 0, 0)),
                  pl.BlockSpec((1, D), lambda i: (0, 0))],
        out_specs=pl.BlockSpec((tm, D), lambda i: (i, 0)),
        out_shape=jax.ShapeDtypeStruct((N, D), F32),
        scratch_shapes=[pltpu.VMEM((2, MOE_SORTED_ROWS, W), jnp.uint32), pltpu.SemaphoreType.DMA((2,))],
        compiler_params=_cparams(("arbitrary",)),
        name="combine",
    )(plan["gtab_in"], plan["gtab_in"], ys, plan["pos"].T, xsr, wts_t, mod4, final_g.reshape(1, D))


def _pack_w_in(w_in):
    D = w_in.shape[0]
    fox_cols = 4 * HALF + 3 * N_HEADS
    wf = w_in[:, :fox_cols]
    wr = w_in[:, fox_cols:]
    o = 3 * HALF
    pad = lambda a, n: jnp.concatenate([a, jnp.zeros((D, n - a.shape[1]), a.dtype)], axis=1)
    parts = [wf[:, :4 * HALF], wr[:, :o],
             pad(wf[:, 4 * HALF:], LANES),
             pad(wr[:, o:o + DECAY_LORA], LANES),
             pad(wr[:, o + DECAY_LORA:o + DECAY_LORA + ICLR_LORA], LANES),
             pad(wr[:, o + DECAY_LORA + ICLR_LORA:], 2 * LANES)]
    return jnp.concatenate(parts, axis=1).astype(BF16)


def _pack_mu(mu):
    o = 3 * HALF
    pad = lambda a, n: jnp.concatenate([a, jnp.zeros((n - a.shape[0],), a.dtype)])
    small = jnp.concatenate([jnp.zeros((LANES,), mu.dtype),
                             pad(mu[o:o + DECAY_LORA], LANES),
                             pad(mu[o + DECAY_LORA:o + DECAY_LORA + ICLR_LORA], LANES),
                             pad(mu[o + DECAY_LORA + ICLR_LORA:], 2 * LANES)])
    return mu[:o].reshape(1, o), small.reshape(1, Z_SMALL)


def kernel(x, c, norm1_g, norm2_g, ada_w, ada_b, w_in, w_out, fox_qn_g, fox_kn_g, fox_on_g, fox_forget_b,
           rw_mu, rw_w0, rw_decay_up, rw_a0, rw_iclr_up, rw_gate_up, rw_k_k, rw_k_a, rw_r_k, rw_lnx_g,
           rw_lnx_b, router_w, router_bias, exp_w_gate, exp_w_up, exp_w_down, sh_w_gate, sh_w_up,
           sh_w_down, final_g):
    B, T, D = x.shape
    N = B * T
    depth = norm1_g.shape[0]
    assert depth == 1, "the combine kernel fuses the final RMSNorm, so exactly one layer is supported"
    xf = x.reshape(N, D)
    for l in range(depth):
        mod4 = _ada(c, ada_w[l], ada_b[l]).reshape(B, 6, 1, D)
        mu_big, mu_small = _pack_mu(rw_mu[l])
        zm, zs = _inproj(xf, mod4, norm1_g[l], _pack_w_in(w_in[l]), mu_big, mu_small, T)
        qp, kp, vp = _foxprep(zm, zs, fox_forget_b[l], fox_qn_g[l], fox_kn_g[l], T)
        y_fox = _attention(qp, kp, vp, zm, fox_on_g[l], B, T)
        rw = dict(rw_w0=rw_w0[l], rw_decay_up=rw_decay_up[l], rw_a0=rw_a0[l], rw_iclr_up=rw_iclr_up[l],
                  rw_gate_up=rw_gate_up[l], rw_k_k=rw_k_k[l], rw_k_a=rw_k_a[l], rw_r_k=rw_r_k[l],
                  rw_lnx_g=rw_lnx_g[l], rw_lnx_b=rw_lnx_b[l])
        y_rwkv = _rwkv(zm, zs, rw, B, T)
        xsr, hp, idx_t, wts_t, rank_t = _outproj(
            xf, y_fox, y_rwkv, mod4, norm2_g[l], w_out[l], router_w[l], router_bias[l],
            sh_w_gate[l], sh_w_up[l], sh_w_down[l], T)
        plan, n_rows = _moe_plan(idx_t, rank_t)
        xs = _dispatch(hp, plan, n_rows)
        ys = _experts(xs, n_rows, plan["blk_e"], plan["n_used"], exp_w_gate[l], exp_w_up[l], exp_w_down[l])
        xf = _combine(ys, plan, xsr, wts_t.T, mod4, final_g, T)
    return xf.reshape(B, T, D)
```

```python
import functools

import jax
import jax.numpy as jnp
import numpy as np
from jax import lax
from jax.experimental import pallas as pl
from jax.experimental.pallas import tpu as pltpu

F32 = jnp.float32
BF16 = jnp.bfloat16
I32 = jnp.int32
HIGHEST = lax.Precision.HIGHEST

D_MODEL = 1024
HEAD_DIM = 64
N_HEADS = 8
HALF = N_HEADS * HEAD_DIM
RMS_EPS = 1e-6
LNX_EPS = 64e-5
LOG2E = 1.4426950408889634
DECAY_LORA = 64
ICLR_LORA = 64
GATE_LORA = 160
N_EXPERTS = 64
N_GROUPS = 8
GROUP_SIZE = N_EXPERTS // N_GROUPS
TOPK_GROUPS = 4
TOP_K = 6
D_EXPERT = 256
D_SHARED = 256
ROUTED_SCALE = 2.5
EXPERT_BLOCK = 1024

LANES = 128
Z_MAIN = 4 * HALF + 3 * HALF
Z_SMALL = 5 * LANES
VMEM_LIMIT = 56 * 1024 * 1024
ATTN_BLOCK = 512
ATTN_HEADS = 4
DMA_UNROLL = 4
WAIT_CHUNK = 16
MOE_TILE = 256
GRANULE = 8
MOE_SORTED_ROWS = MOE_TILE * TOP_K + N_EXPERTS * GRANULE
MOE_SPARE_ROWS = 2 * MOE_TILE * GRANULE
FETCH_PER_CHUNK = 96
MOE_ROW_CHUNK = 256
RWKV_CHUNK = 64
RWKV_SEQS_PER_STEP = 4


def _cparams(semantics):
    return pltpu.CompilerParams(dimension_semantics=semantics, vmem_limit_bytes=VMEM_LIMIT)


def _mm(a, b):
    return jnp.dot(a.astype(BF16), b.astype(BF16), preferred_element_type=F32)


def _mm_nt(a, b):
    return lax.dot_general(a.astype(BF16), b.astype(BF16), (((1,), (1,)), ((), ())),
                           preferred_element_type=F32)


def _mm_tn(a, b):
    return lax.dot_general(a.astype(BF16), b.astype(BF16), (((0,), (0,)), ((), ())),
                           preferred_element_type=F32)


def _mm_f32(a, b):
    return jnp.dot(a, b, precision=HIGHEST, preferred_element_type=F32)


def _bf16_pieces(x, passes):
    pieces = []
    for _ in range(passes):
        piece = x.astype(BF16)
        pieces.append(piece)
        x = x - piece.astype(F32)
    return pieces


def _mm_split(m01, x, passes=3):
    return sum(jnp.dot(m01, p, preferred_element_type=F32) for p in _bf16_pieces(x, passes))


def _mm_split_r(x, m01, passes=2):
    return sum(jnp.dot(p, m01, preferred_element_type=F32) for p in _bf16_pieces(x, passes))


def _sigmoid(x):
    return 1.0 / (1.0 + jnp.exp(-x))


def _softplus(x):
    return jnp.maximum(x, 0.0) + jnp.log(1.0 + jnp.exp(-jnp.abs(x)))


def _silu(x):
    return x * _sigmoid(x)


def _pack_halves(x):
    w = x.shape[1] // 2
    bits = lambda t: lax.bitcast_convert_type(t.astype(jnp.bfloat16).astype(F32), jnp.uint32)
    return (bits(x[:, 0:w]) >> 16) | (bits(x[:, w:2 * w]) & jnp.uint32(0xFFFF0000))


def _pack_bf16_values(lo, hi):
    return (lax.bitcast_convert_type(lo, jnp.uint32) >> 16) | lax.bitcast_convert_type(hi, jnp.uint32)


def _unpack_halves(p):
    lo = lax.bitcast_convert_type(p << 16, F32)
    hi = lax.bitcast_convert_type(p & jnp.uint32(0xFFFF0000), F32)
    return lo, hi


def _shift_rows(z, carry_ref, first):
    rows = z.shape[0]
    prev_row = jnp.where(first, 0.0, carry_ref[0:1, :])
    prev = pltpu.roll(z, 1, 0)
    row0 = lax.broadcasted_iota(I32, (rows, 1), 0) == 0
    prev = jnp.where(row0, prev_row, prev)
    carry_ref[0:1, :] = z[rows - 1:rows, :]
    return prev


def _ada_kernel(c_ref, w_ref, b_ref, o_ref):
    o_ref[...] = _mm_f32(_silu(c_ref[...]), w_ref[...]) + b_ref[...]


def _ada(c, ada_w, ada_b):
    B, D = c.shape
    n_out = ada_w.shape[1]
    tn = 512
    return pl.pallas_call(
        _ada_kernel,
        grid=(n_out // tn,),
        in_specs=[pl.BlockSpec((B, D), lambda j: (0, 0)),
                  pl.BlockSpec((D, tn), lambda j: (0, j)),
                  pl.BlockSpec((1, tn), lambda j: (0, j))],
        out_specs=pl.BlockSpec((B, tn), lambda j: (0, j)),
        out_shape=jax.ShapeDtypeStruct((B, n_out), F32),
        compiler_params=_cparams(("arbitrary",)),
        name="ada",
    )(c, ada_w, ada_b.reshape(1, n_out))


def _inproj_kernel(x_ref, g_ref, sh_ref, sc_ref, w_ref, mub_ref, mus_ref, zm_ref, zs_ref,
                   carry_b, carry_s, *, tiles_per_seq):
    first = (pl.program_id(0) % tiles_per_seq) == 0
    x = x_ref[...]
    h = x * lax.rsqrt(jnp.mean(x * x, axis=-1, keepdims=True) + RMS_EPS) * g_ref[...]
    hb = (h * (1.0 + sc_ref[...]) + sh_ref[...]).astype(BF16)
    nf = 4 * HALF
    zm_ref[:, 0:nf] = jnp.dot(hb, w_ref[:, 0:nf], preferred_element_type=F32).astype(BF16)
    zr = jnp.dot(hb, w_ref[:, nf:Z_MAIN], preferred_element_type=F32)
    zr = zr + mub_ref[...] * (_shift_rows(zr, carry_b, first) - zr)
    zm_ref[:, nf:Z_MAIN] = zr.astype(BF16)
    zs = jnp.dot(hb, w_ref[:, Z_MAIN:Z_MAIN + Z_SMALL], preferred_element_type=F32)
    zs_ref[...] = zs + mus_ref[...] * (_shift_rows(zs, carry_s, first) - zs)


def _inproj(x2d, mod4, norm_g, w_all, mu_big, mu_small, T):
    N, D = x2d.shape
    tm = min(512, T)
    tps = T // tm
    vec = lambda j: pl.BlockSpec((None, None, 1, D), lambda i: (i // tps, j, 0, 0))
    return pl.pallas_call(
        functools.partial(_inproj_kernel, tiles_per_seq=tps),
        grid=(N // tm,),
        in_specs=[pl.BlockSpec((tm, D), lambda i: (i, 0)),
                  pl.BlockSpec((1, D), lambda i: (0, 0)),
                  vec(0), vec(1),
                  pl.BlockSpec((D, Z_MAIN + Z_SMALL), lambda i: (0, 0)),
                  pl.BlockSpec((1, 3 * HALF), lambda i: (0, 0)),
                  pl.BlockSpec((1, Z_SMALL), lambda i: (0, 0))],
        out_specs=[pl.BlockSpec((tm, Z_MAIN), lambda i: (i, 0)),
                   pl.BlockSpec((tm, Z_SMALL), lambda i: (i, 0))],
        out_shape=[jax.ShapeDtypeStruct((N, Z_MAIN), BF16),
                   jax.ShapeDtypeStruct((N, Z_SMALL), F32)],
        scratch_shapes=[pltpu.VMEM((8, 3 * HALF), F32), pltpu.VMEM((8, Z_SMALL), F32)],
        compiler_params=_cparams(("arbitrary",)),
        name="inproj",
    )(x2d, norm_g.reshape(1, D), mod4, mod4, w_all, mu_big, mu_small)


def _foxprep_kernel(q_ref, k_ref, v_ref, zs_ref, fb_ref, qg_ref, kg_ref, tri_ref, eexp_ref,
                    esum_ref, e8_ref, plq_ref, plk_ref, plv_ref, cq_ref, ck_ref, cv_ref,
                    qp_ref, kp_ref, vp_ref, carry_k, carry_v, carry_c, *, tiles_per_seq):
    first = (pl.program_id(0) % tiles_per_seq) == 0
    zs = zs_ref[...]
    logf = -_softplus(-(zs + fb_ref[...]))
    cum = _mm_split(tri_ref[...], logf) + jnp.where(first, 0.0, carry_c[0:1, :])
    carry_c[0:1, :] = cum[cum.shape[0] - 1:, :]
    cum2 = cum * LOG2E
    c_hi = cum2.astype(BF16)
    r1 = cum2 - c_hi.astype(F32)
    c_mid = r1.astype(BF16)
    c_lo = (r1 - c_mid.astype(F32)).astype(BF16)
    a_full = _mm_split_r(_sigmoid(zs), eexp_ref[...])
    k = k_ref[...].astype(F32)
    v = v_ref[...].astype(F32)
    a_k = a_full[:, 0:HALF]
    a_v = a_full[:, HALF:2 * HALF]
    k = a_k * _shift_rows(k, carry_k, first) + (1.0 - a_k) * k
    v = a_v * _shift_rows(v, carry_v, first) + (1.0 - a_v) * v
    q = q_ref[...].astype(F32)

    def head_rms(t, gain):
        ms = _mm_split_r(t * t, esum_ref[...]) * (1.0 / HEAD_DIM)
        inv = _mm_split_r(lax.rsqrt(ms + RMS_EPS), e8_ref[...])
        return t * inv * gain

    qn = head_rms(q, qg_ref[...]) * (HEAD_DIM ** -0.5 * LOG2E)
    kn = head_rms(k, kg_ref[...])
    aug = [c_hi, c_mid, c_lo]
    lhs_q = jnp.concatenate([qn.astype(BF16)] + aug, axis=1)
    lhs_k = jnp.concatenate([kn.astype(BF16)] + aug, axis=1)
    qp_ref[...] = (jnp.dot(lhs_q, plq_ref[...], preferred_element_type=F32) + cq_ref[...]).astype(BF16)
    kp_ref[...] = (jnp.dot(lhs_k, plk_ref[...], preferred_element_type=F32) + ck_ref[...]).astype(BF16)
    vp_ref[...] = (jnp.dot(v.astype(BF16), plv_ref[...], preferred_element_type=F32)
                   + cv_ref[...]).astype(BF16)


def _fox_constants(tm):
    hp = N_HEADS * LANES
    eexp = np.zeros((LANES, 2 * HALF), np.float32)
    esum = np.zeros((HALF, LANES), np.float32)
    e8 = np.zeros((LANES, HALF), np.float32)
    plq = np.zeros((HALF + 3 * LANES, hp), np.float32)
    plk = np.zeros((HALF + 3 * LANES, hp), np.float32)
    plv = np.zeros((HALF, hp), np.float32)
    cq = np.zeros((1, hp), np.float32)
    ck = np.zeros((1, hp), np.float32)
    cv = np.zeros((1, hp), np.float32)
    for h in range(N_HEADS):
        sl = slice(h * HEAD_DIM, (h + 1) * HEAD_DIM)
        eexp[8 + h, sl] = 1.0
        eexp[16 + h, HALF + h * HEAD_DIM:HALF + (h + 1) * HEAD_DIM] = 1.0
        esum[sl, h] = 1.0
        e8[h, sl] = 1.0
        base = h * LANES
        for d in range(HEAD_DIM):
            plq[h * HEAD_DIM + d, base + d] = 1.0
            plk[h * HEAD_DIM + d, base + d] = 1.0
            plv[h * HEAD_DIM + d, base + d] = 1.0
        for j in range(3):
            plq[HALF + j * LANES + h, base + HEAD_DIM + j] = 1.0
            plk[HALF + j * LANES + h, base + HEAD_DIM + 3 + j] = -1.0
            cq[0, base + HEAD_DIM + 3 + j] = 1.0
            ck[0, base + HEAD_DIM + j] = 1.0
        cv[0, base + HEAD_DIM] = 1.0
    tri = np.tril(np.ones((tm, tm), np.float32))
    bf = lambda a: jnp.asarray(a, BF16)
    return dict(tri=bf(tri), eexp=bf(eexp), esum=bf(esum), e8=bf(e8),
                plq=bf(plq), plk=bf(plk), plv=bf(plv), cq=jnp.asarray(cq), ck=jnp.asarray(ck),
                cv=jnp.asarray(cv))


def _foxprep(zm, zs, forget_b, qn_g, kn_g, T):
    N = zm.shape[0]
    tm = min(512, T)
    tps = T // tm
    cst = _fox_constants(tm)
    hp = N_HEADS * LANES
    fb = jnp.zeros((1, LANES), F32).at[0, :N_HEADS].set(forget_b)
    full = lambda a: pl.BlockSpec(a.shape, lambda i: (0, 0))
    consts = [fb, qn_g.reshape(1, HALF), kn_g.reshape(1, HALF), cst["tri"], cst["eexp"], cst["esum"],
              cst["e8"], cst["plq"], cst["plk"], cst["plv"], cst["cq"], cst["ck"], cst["cv"]]
    return pl.pallas_call(
        functools.partial(_foxprep_kernel, tiles_per_seq=tps),
        grid=(N // tm,),
        in_specs=[pl.BlockSpec((tm, HALF), lambda i: (i, 0)),
                  pl.BlockSpec((tm, HALF), lambda i: (i, 1)),
                  pl.BlockSpec((tm, HALF), lambda i: (i, 2)),
                  pl.BlockSpec((tm, LANES), lambda i: (i, 0))] + [full(a) for a in consts],
        out_specs=[pl.BlockSpec((tm, hp), lambda i: (i, 0))] * 3,
        out_shape=[jax.ShapeDtypeStruct((N, hp), BF16)] * 3,
        scratch_shapes=[pltpu.VMEM((8, HALF), F32), pltpu.VMEM((8, HALF), F32),
                        pltpu.VMEM((8, LANES), F32)],
        compiler_params=_cparams(("arbitrary",)),
        name="foxprep",
    )(zm, zm, zm, zs, *consts)


def _attn_kernel(q_ref, k_ref, v_ref, g_ref, ong_ref, o_ref, vt_ref, *, tq):
    i = pl.program_id(2)
    n_kv = vt_ref.shape[1]
    heads = range(ATTN_HEADS)
    lanes = [slice(hh * LANES, (hh + 1) * LANES) for hh in heads]

    @pl.when(i == 0)
    def _():
        for hh in heads:
            for c in range(n_kv):
                vt_ref[hh, c] = v_ref[c * tq:(c + 1) * tq, lanes[hh]].T

    key = lax.broadcasted_iota(I32, (tq, tq), 0)
    qry = lax.broadcasted_iota(I32, (tq, tq), 1)
    causal = key <= qry
    qs = [q_ref[:, lanes[hh]] for hh in heads]

    def step(j, carry, masked):
        m, acc = carry
        start = pl.multiple_of(j * tq, tq)
        s = [lax.dot_general(k_ref[pl.ds(start, tq), lanes[hh]], qs[hh], (((1,), (1,)), ((), ())),
                             preferred_element_type=F32) for hh in heads]
        m_out, acc_out = [], []
        for hh in heads:
            sh = jnp.where(causal, s[hh], -jnp.inf) if masked else s[hh]
            m_new = jnp.maximum(m[hh], jnp.max(sh, axis=0, keepdims=True))
            p = jnp.exp2(sh - m_new)
            acc_out.append(jnp.exp2(m[hh] - m_new) * acc[hh]
                           + jnp.dot(vt_ref[hh, j], p.astype(BF16), preferred_element_type=F32))
            m_out.append(m_new)
        return tuple(m_out), tuple(acc_out)

    init = (tuple(jnp.full((1, tq), -jnp.inf, F32) for _ in heads),
            tuple(jnp.zeros((LANES, tq), F32) for _ in heads))
    carry = lax.fori_loop(0, i, functools.partial(step, masked=False), init)
    _, acc = step(i, carry, True)
    chan = lax.broadcasted_iota(I32, (LANES, tq), 0)
    chan_w = jnp.where(chan < HEAD_DIM, 1.0 / HEAD_DIM, jnp.where(chan == HEAD_DIM, RMS_EPS, 0.0))
    lane = lax.broadcasted_iota(I32, (tq, LANES), 1)
    outs = []
    for hh in heads:
        t = jnp.sum(acc[hh] * acc[hh] * chan_w, axis=0, keepdims=True)
        outs.append((acc[hh] * lax.rsqrt(t)).T)
    o = jnp.concatenate([jnp.where(lane < HEAD_DIM, outs[2 * p], pltpu.roll(outs[2 * p + 1], HEAD_DIM, 1))
                         for p in range(ATTN_HEADS // 2)], axis=1)
    y = o * ong_ref[...] * _sigmoid(g_ref[...].astype(F32))
    o_ref[...] = y.astype(BF16)


def _attention(qp, kp, vp, zm, on_g, B, T):
    N = qp.shape[0]
    tq = min(ATTN_BLOCK, T)
    nq = T // tq
    groups = N_HEADS // ATTN_HEADS
    wp = ATTN_HEADS * LANES
    wo = ATTN_HEADS * HEAD_DIM
    g_col0 = 3 * HALF // wo
    return pl.pallas_call(
        functools.partial(_attn_kernel, tq=tq),
        grid=(B, groups, nq),
        in_specs=[pl.BlockSpec((tq, wp), lambda b, p, i: (b * nq + i, p)),
                  pl.BlockSpec((T, wp), lambda b, p, i: (b, p)),
                  pl.BlockSpec((T, wp), lambda b, p, i: (b, p)),
                  pl.BlockSpec((tq, wo), lambda b, p, i: (b * nq + i, g_col0 + p)),
                  pl.BlockSpec((None, 1, wo), lambda b, p, i: (p, 0, 0))],
        out_specs=pl.BlockSpec((tq, wo), lambda b, p, i: (b * nq + i, p)),
        out_shape=jax.ShapeDtypeStruct((N, HALF), BF16),
        scratch_shapes=[pltpu.VMEM((ATTN_HEADS, nq, LANES, tq), BF16)],
        compiler_params=_cparams(("arbitrary", "arbitrary", "arbitrary")),
        name="attn",
    )(qp, kp, vp, zm, on_g.reshape(groups, 1, wo))


def _rwkv_kernel(r_ref, k_ref, v_ref, zs_ref, w0_ref, dup_ref, a0_ref, iup_ref, gup_ref, kk_ref,
                 ka_ref, rk_ref, lng_ref, lnb_ref, tri_ref, o_ref, state, ybuf, *, chunk, nb):
    C = chunk
    c_idx = pl.program_id(1)

    @pl.when(c_idx == 0)
    def _():
        state[...] = jnp.zeros_like(state)

    R = nb * C
    r = r_ref[...].reshape(R, HALF).astype(F32)
    k = k_ref[...].reshape(R, HALF).astype(F32)
    v = v_ref[...].reshape(R, HALF).astype(F32)
    zs = zs_ref[...].reshape(R, Z_SMALL)
    wd = zs[:, LANES:2 * LANES]
    ad = zs[:, 2 * LANES:3 * LANES]
    gd = zs[:, 3 * LANES:5 * LANES]
    wl = w0_ref[...] + _mm(jnp.tanh(wd), dup_ref[...])
    lw = -jnp.exp(-_softplus(-wl) - 0.5)
    a = _sigmoid(a0_ref[...] + _mm(ad, iup_ref[...]))
    g = _mm(_sigmoid(gd), gup_ref[...])
    kk = k * kk_ref[...]
    k2 = k * (1.0 + (a - 1.0) * ka_ref[...])
    cl = _mm_split(tri_ref[...], lw)
    cl_end = jnp.concatenate(
        [jnp.broadcast_to(cl[(bi + 1) * C - 1:(bi + 1) * C, :], (C, HALF)) for bi in range(nb)], axis=0)
    e_neg = jnp.exp(-cl)
    e_tail = jnp.exp(cl_end - cl)
    pre = dict(rt=r * jnp.exp(cl), at=-kk * jnp.exp(cl - lw), kh=k2 * e_neg, bh=kk * a * e_neg,
               kb=k2 * e_tail, bb=kk * a * e_tail, v=v, kk=kk, pend=jnp.exp(cl_end),
               rkr=r * k2 * rk_ref[...], g=g)

    C2 = 2 * C
    row = lax.broadcasted_iota(I32, (C2, C2), 0)
    col = lax.broadcasted_iota(I32, (C2, C2), 1)
    lower = (col & (C - 1)) <= (row & (C - 1))
    strict = (col & (C - 1)) < (row & (C - 1))
    eye = row == col
    head0 = lax.broadcasted_iota(I32, (C, LANES), 1) < HEAD_DIM
    n_sq = int(np.log2(C)) - 1
    units = [(bi, p) for bi in range(nb) for p in range(N_HEADS // 2)]

    def part(name, bi, p):
        return pre[name][bi * C:(bi + 1) * C, p * LANES:(p + 1) * LANES]

    def stack(x):
        return jnp.concatenate([jnp.where(head0, x, 0.0), jnp.where(head0, 0.0, x)], axis=0)

    def unstack(x):
        return x[0:C] + x[C:C2]

    def head_sum(x):
        s0 = jnp.sum(jnp.where(head0, x, 0.0), axis=1, keepdims=True)
        s1 = jnp.sum(jnp.where(head0, 0.0, x), axis=1, keepdims=True)
        return jnp.where(head0, s0, s1)

    ops = []
    for bi, p in units:
        kk_p = part("kk", bi, p)
        inv = 1.0 / jnp.maximum(jnp.sqrt(head_sum(kk_p * kk_p)), 1e-12)
        ops.append(dict(rt=stack(part("rt", bi, p)), at=stack(part("at", bi, p) * inv),
                        kh=stack(part("kh", bi, p)).astype(BF16), bh=stack(part("bh", bi, p) * inv).astype(BF16),
                        kb=stack(part("kb", bi, p)).astype(BF16), bb=stack(part("bb", bi, p) * inv).astype(BF16),
                        v=stack(part("v", bi, p)).astype(BF16)))
    ra = [jnp.concatenate([o["rt"], o["at"]], axis=0).astype(BF16) for o in ops]
    gg = [_mm_nt(x, jnp.concatenate([o["kh"], o["bh"]], axis=0)) for x, o in zip(ra, ops)]
    g1 = [t[:, 0:C2] for t in gg]
    g2 = [t[:, C2:2 * C2] for t in gg]
    a_rb = [jnp.where(lower, t[0:C2], 0.0).astype(BF16) for t in g2]
    pw = [jnp.where(strict, t[C2:2 * C2], 0.0) for t in g2]
    av = [_mm(jnp.concatenate([jnp.where(lower, t[0:C2], 0.0).astype(BF16),
                               jnp.where(strict, t[C2:2 * C2], 0.0).astype(BF16), o["kb"].T], axis=0),
              o["v"]) for t, o in zip(g1, ops)]
    swap = lambda t: pltpu.roll(t, HEAD_DIM, 1)
    own = (row < C) == (col < HEAD_DIM)
    xs = [o["at"] + swap(t[C2:2 * C2]) for o, t in zip(ops, av)]
    for level in range(n_sq + 1):
        pb = [p.astype(BF16) for p in pw]
        xs = [x + _mm(p, x) for p, x in zip(pb, xs)]
        if level < n_sq:
            pw = [jnp.dot(p, p, preferred_element_type=F32) for p in pb]
    rbx = [_mm(jnp.concatenate([p, o["bb"].T], axis=0), x) for p, o, x in zip(a_rb, ops, xs)]
    for u, (bi, p) in enumerate(units):
        o = ops[u]
        sl = slice(p * LANES, (p + 1) * LANES)
        rb, bx, kv = rbx[u][0:C2], rbx[u][C2:2 * C2], av[u][2 * C2:3 * C2]
        r2 = unstack(o["rt"] + jnp.where(own, rb, 0.0))
        y0 = swap(unstack(jnp.where(own, 0.0, rb))) + unstack(av[u][0:C2])
        m_mat = jnp.where(eye, part("pend", bi, p)[0:1, :], 0.0) + jnp.where(own, bx, 0.0)
        g_mat = swap(jnp.where(own, 0.0, bx)) + kv
        out = _mm(jnp.concatenate([r2, m_mat], axis=0), state[bi, p])
        state[bi, p] = out[C:C + LANES] + g_mat
        y = out[0:C] + y0
        cen = y - head_sum(y) * (1.0 / HEAD_DIM)
        var = head_sum(cen * cen) * (1.0 / HEAD_DIM)
        bonus = head_sum(part("rkr", bi, p)) * part("v", bi, p)
        y = cen * lax.rsqrt(var + LNX_EPS) * lng_ref[:, sl] + lnb_ref[:, sl] + bonus
        ybuf[bi, :, sl] = y * part("g", bi, p)
    o_ref[...] = ybuf[...].astype(BF16)


def _rwkv(zm, zs, p, B, T):
    N = zm.shape[0]
    C = RWKV_CHUNK
    assert 2 * C == LANES and T % C == 0, "a head pair's stacked chunk must fill one 128-row tile"
    nc = T // C
    nb = RWKV_SEQS_PER_STEP if B % RWKV_SEQS_PER_STEP == 0 else 1
    tri = np.kron(np.eye(nb, dtype=np.float32), np.tril(np.ones((C, C), np.float32)))
    pad_rows = lambda w, rows: jnp.zeros((rows, HALF), F32).at[:w.shape[0]].set(w)
    row = lambda a: a.reshape(1, HALF)
    consts = [row(p["rw_w0"]), pad_rows(p["rw_decay_up"], LANES), row(p["rw_a0"]),
              pad_rows(p["rw_iclr_up"], LANES), pad_rows(p["rw_gate_up"], 2 * LANES),
              row(p["rw_k_k"]), row(p["rw_k_a"]), row(p["rw_r_k"]), row(p["rw_lnx_g"]),
              row(p["rw_lnx_b"]), jnp.asarray(tri, BF16)]
    full = lambda a: pl.BlockSpec(a.shape, lambda b, c: (0, 0))
    rcol = 4 * HALF // HALF
    zm3 = zm.reshape(B, T, Z_MAIN)
    out = pl.pallas_call(
        functools.partial(_rwkv_kernel, chunk=C, nb=nb),
        grid=(B // nb, nc),
        in_specs=[pl.BlockSpec((nb, C, HALF), lambda b, c: (b, c, rcol)),
                  pl.BlockSpec((nb, C, HALF), lambda b, c: (b, c, rcol + 1)),
                  pl.BlockSpec((nb, C, HALF), lambda b, c: (b, c, rcol + 2)),
                  pl.BlockSpec((nb, C, Z_SMALL), lambda b, c: (b, c, 0))] + [full(a) for a in consts],
        out_specs=pl.BlockSpec((nb, C, HALF), lambda b, c: (b, c, 0)),
        out_shape=jax.ShapeDtypeStruct((B, T, HALF), BF16),
        scratch_shapes=[pltpu.VMEM((nb, N_HEADS // 2, LANES, LANES), F32), pltpu.VMEM((nb, C, HALF), F32)],
        compiler_params=_cparams(("arbitrary", "arbitrary")),
        name="rwkv",
    )(zm3, zm3, zm3, zs.reshape(B, T, Z_SMALL), *consts)
    return out.reshape(N, HALF)


def _first_index(mask, iota, big):
    return jnp.min(jnp.where(mask, iota, big), axis=0, keepdims=True)


def _outproj_kernel(x_ref, yf_ref, yr_ref, g1_ref, sh2_ref, sc2_ref, g2_ref, n2g_ref, wo_ref, rwt_ref,
                    rb_ref, swg_ref, swu_ref, swd_ref, ustrict_ref,
                    xs_ref, hp_ref, idx_ref, wts_ref, rank_ref, *, tm):
    d = (jnp.dot(yf_ref[...], wo_ref[0:HALF, :], preferred_element_type=F32)
         + jnp.dot(yr_ref[...], wo_ref[HALF:2 * HALF, :], preferred_element_type=F32))
    x2 = x_ref[...] + g1_ref[...] * d
    h = x2 * lax.rsqrt(jnp.mean(x2 * x2, axis=-1, keepdims=True) + RMS_EPS) * n2g_ref[...]
    h = h * (1.0 + sc2_ref[...]) + sh2_ref[...]
    hb = h.astype(BF16)
    act = _silu(jnp.dot(hb, swg_ref[...], preferred_element_type=F32)) * jnp.dot(
        hb, swu_ref[...], preferred_element_type=F32)
    shared = jnp.dot(act.astype(BF16), swd_ref[...], preferred_element_type=F32)
    xs_ref[...] = x2 + g2_ref[...] * shared
    half = D_MODEL // 2
    hp_ref[...] = _pack_halves(h)

    logits = lax.dot_general(rwt_ref[...], h, (((1,), (1,)), ((), ())), precision=HIGHEST,
                             preferred_element_type=F32)
    scores = _sigmoid(logits)
    sel = scores + rb_ref[:, 0:1]
    neg = -jnp.inf
    sel3 = sel.reshape(N_GROUPS, GROUP_SIZE, tm)
    io_in = lax.broadcasted_iota(I32, (N_GROUPS, GROUP_SIZE, tm), 1)
    m1 = jnp.max(sel3, axis=1, keepdims=True)
    f1 = jnp.min(jnp.where(sel3 == m1, io_in, GROUP_SIZE), axis=1, keepdims=True)
    m2 = jnp.max(jnp.where(io_in == f1, neg, sel3), axis=1, keepdims=True)
    gs = (m1 + m2).reshape(N_GROUPS, tm)
    io_g = lax.broadcasted_iota(I32, (N_GROUPS, tm), 0)
    gmask = jnp.zeros((N_GROUPS, tm), jnp.bool_)
    for _ in range(TOPK_GROUPS):
        mg = jnp.max(gs, axis=0, keepdims=True)
        fg = _first_index(gs == mg, io_g, N_GROUPS)
        pick = io_g == fg
        gmask = jnp.logical_or(gmask, pick)
        gs = jnp.where(pick, neg, gs)
    emask = jnp.broadcast_to(gmask.reshape(N_GROUPS, 1, tm), (N_GROUPS, GROUP_SIZE, tm)).reshape(N_EXPERTS, tm)
    cur = jnp.where(emask, sel, neg)
    io_e = lax.broadcasted_iota(I32, (N_EXPERTS, tm), 0)
    picks, idxs, wts = [], [], []
    for _ in range(TOP_K):
        me = jnp.max(cur, axis=0, keepdims=True)
        fe = _first_index(cur == me, io_e, N_EXPERTS)
        pick = io_e == fe
        picks.append(pick)
        idxs.append(fe)
        wts.append(jnp.sum(jnp.where(pick, scores, 0.0), axis=0, keepdims=True))
        cur = jnp.where(pick, neg, cur)
    wsum = wts[0]
    for w in wts[1:]:
        wsum = wsum + w
    zero_i = jnp.zeros((1, tm), I32)
    zero_f = jnp.zeros((1, tm), F32)
    idx_ref[...] = jnp.concatenate(idxs + [zero_i, zero_i], axis=0)
    wts_ref[...] = jnp.concatenate([w / wsum * ROUTED_SCALE for w in wts] + [zero_f, zero_f], axis=0)
    cnt = picks[0].astype(F32)
    for pk in picks[1:]:
        cnt = cnt + pk.astype(F32)
    excl = jnp.dot(cnt.astype(BF16), ustrict_ref[...], preferred_element_type=F32)
    ranks = [jnp.sum(jnp.where(pk, excl, 0.0), axis=0, keepdims=True).astype(I32) for pk in picks]
    rank_ref[...] = jnp.concatenate(ranks + [zero_i, zero_i], axis=0)


def _outproj(x2d, yf, yr, mod4, norm2_g, w_out, router_w, router_bias, swg, swu, swd, T):
    N, D = x2d.shape
    tm = min(512, T)
    tps = T // tm
    vec = lambda j: pl.BlockSpec((None, None, 1, D), lambda i: (i // tps, j, 0, 0))
    full = lambda a: pl.BlockSpec(a.shape, lambda i: (0, 0))
    ts = min(MOE_TILE, tm)
    ustrict = jnp.asarray(np.kron(np.eye(tm // ts, dtype=np.float32),
                                  np.triu(np.ones((ts, ts), np.float32), 1)), BF16)
    rb = jnp.broadcast_to(router_bias.reshape(N_EXPERTS, 1), (N_EXPERTS, LANES))
    consts = [norm2_g.reshape(1, D), w_out.astype(BF16), router_w.T, rb, swg.astype(BF16),
              swu.astype(BF16), swd.astype(BF16), ustrict]
    small = lambda dt: jax.ShapeDtypeStruct((8, N), dt)
    return pl.pallas_call(
        functools.partial(_outproj_kernel, tm=tm),
        grid=(N // tm,),
        in_specs=[pl.BlockSpec((tm, D), lambda i: (i, 0)),
                  pl.BlockSpec((tm, HALF), lambda i: (i, 0)),
                  pl.BlockSpec((tm, HALF), lambda i: (i, 0)),
                  vec(2), vec(3), vec(4), vec(5)] + [full(a) for a in consts],
        out_specs=[pl.BlockSpec((tm, D), lambda i: (i, 0)),
                   pl.BlockSpec((tm, D // 2), lambda i: (i, 0)),
                   pl.BlockSpec((8, tm), lambda i: (0, i)),
                   pl.BlockSpec((8, tm), lambda i: (0, i)),
                   pl.BlockSpec((8, tm), lambda i: (0, i))],
        out_shape=[jax.ShapeDtypeStruct((N, D), F32), jax.ShapeDtypeStruct((N, D // 2), jnp.uint32),
                   small(I32), small(F32), small(I32)],
        compiler_params=_cparams(("arbitrary",)),
        name="outproj",
    )(x2d, yf, yr, mod4, mod4, mod4, mod4, *consts)


def _for_each(count, fn, group):
    full = count // group

    def trip(t, c):
        for u in range(group):
            fn(t * group + u)
        return c

    def single(g, c):
        fn(g)
        return c

    lax.fori_loop(0, full, trip, 0)
    lax.fori_loop(full * group, count, single, 0)


def _wait_granules(count, wait_rows):
    full = count // WAIT_CHUNK

    def chunk(_, c):
        wait_rows(WAIT_CHUNK * GRANULE)
        return c

    def single(_, c):
        wait_rows(GRANULE)
        return c

    lax.fori_loop(0, full, chunk, 0)
    lax.fori_loop(full * WAIT_CHUNK, count, single, 0)


def _moe_plan(idx_t, rank_t):
    n = idx_t.shape[1]
    n_tiles = n // MOE_TILE
    experts = jnp.arange(N_EXPERTS, dtype=I32)
    hot = idx_t[:TOP_K, :, None] == experts
    cnt = jnp.sum(hot.reshape(TOP_K, n_tiles, MOE_TILE, N_EXPERTS).astype(I32), axis=(0, 2))
    gran = (cnt + GRANULE - 1) // GRANULE
    loc_end = jnp.cumsum(gran, axis=1)
    loc_off = loc_end - gran
    g_tile = loc_end[:, N_EXPERTS - 1]
    padded = (GRANULE * jnp.sum(gran, axis=0) + EXPERT_BLOCK - 1) // EXPERT_BLOCK * EXPERT_BLOCK
    pad_end = jnp.cumsum(padded)
    glob_off = (pad_end - padded)[None, :] + GRANULE * (jnp.cumsum(gran, axis=0) - gran)
    loc_tok = jnp.repeat(GRANULE * loc_off, MOE_TILE, axis=0)
    pos = rank_t[:TOP_K] + jnp.sum(jnp.where(hot, loc_tok[None], 0), axis=-1)
    pos = jnp.concatenate([pos, jnp.full((8 - TOP_K, n), -1, I32)], axis=0)
    g = jnp.arange(MOE_TILE, dtype=I32)
    e_of_g = jnp.minimum(jnp.sum((loc_end[:, None, :] <= g[None, :, None]).astype(I32), axis=-1), N_EXPERTS - 1)
    dst = jnp.sum(jnp.where(e_of_g[:, :, None] == experts,
                            glob_off[:, None, :] + GRANULE * (g[None, :, None] - loc_off[:, None, :]), 0), axis=-1)
    n_blocks = -(-(n * TOP_K + GRANULE * N_EXPERTS * n_tiles) // EXPERT_BLOCK) + N_EXPERTS
    n_rows = n_blocks * EXPERT_BLOCK
    live = g[None, :] < g_tile[:, None]
    spare = n_rows + (jnp.arange(n_tiles, dtype=I32)[:, None] % 2) * (MOE_TILE * GRANULE) + g[None, :] * GRANULE
    gtab_in = jnp.where(live, dst, 0).reshape(n_tiles, 1, MOE_TILE)
    gtab_out = jnp.where(live, dst, spare).reshape(n_tiles, 1, MOE_TILE)
    blk_start = jnp.arange(n_blocks, dtype=I32) * EXPERT_BLOCK
    blk_e = jnp.minimum(jnp.sum((pad_end[None, :] <= blk_start[:, None]).astype(I32), axis=1), N_EXPERTS - 1)
    plan = dict(pos=pos.astype(I32), gtab_in=gtab_in.astype(I32), gtab_out=gtab_out.astype(I32),
                pad_end=pad_end.astype(I32), padded=padded.astype(I32), blk_e=blk_e.astype(I32),
                n_used=(pad_end[N_EXPERTS - 1:] // EXPERT_BLOCK).astype(I32))
    return plan, n_rows


def _dispatch_kernel(pend_ref, padded_ref, nu_ref, gtab_ref, pos_ref, hp_ref, xs_ref, zeros, sbuf,
                     sem, zsem, *, n_blocks, n_tiles):
    @pl.when(pl.program_id(0) == 0)
    def _():
        zeros[...] = jnp.zeros_like(zeros)

        def block_copy(start):
            return pltpu.make_async_copy(zeros, xs_ref.at[pl.ds(pl.multiple_of(start, EXPERT_BLOCK),
                                                                 EXPERT_BLOCK), :], zsem)

        def pad_loop(fn):
            def body(e, _):
                @pl.when(padded_ref[e] > 0)
                def _():
                    fn(block_copy(pend_ref[e] - EXPERT_BLOCK))
                return 0
            lax.fori_loop(0, N_EXPERTS, body, 0)

        def tail_loop(fn):
            def body(b, _):
                fn(block_copy(b * EXPERT_BLOCK))
                return 0
            lax.fori_loop(nu_ref[0], n_blocks + MOE_SPARE_ROWS // EXPERT_BLOCK, body, 0)

        pad_loop(lambda cp: cp.start())
        tail_loop(lambda cp: cp.start())
        pad_loop(lambda cp: cp.wait())
        tail_loop(lambda cp: cp.wait())

    i = pl.program_id(0)
    slot = i % 2

    def wait_tile(which):
        pltpu.make_async_copy(sbuf.at[which], xs_ref.at[pl.ds(0, MOE_SORTED_ROWS), :], sem.at[which]).wait()

    @pl.when(i >= 2)
    def _():
        wait_tile(slot)

    pos = pos_ref[...]
    lo, hi = _unpack_halves(hp_ref[...])
    lo = lo.astype(BF16)
    hi = hi.astype(BF16)
    r_iota = lax.broadcasted_iota(I32, (MOE_ROW_CHUNK, MOE_TILE), 0).astype(F32).astype(BF16)
    one = jnp.ones((), BF16)
    for c in range(MOE_SORTED_ROWS // MOE_ROW_CHUNK):
        rel = pos - c * MOE_ROW_CHUNK
        rel = jnp.where(jnp.logical_and(rel >= 0, rel < MOE_ROW_CHUNK), rel, -1).astype(F32)
        perm = jnp.zeros((MOE_ROW_CHUNK, MOE_TILE), BF16)
        for k in range(TOP_K):
            perm = jnp.where(r_iota == rel[k:k + 1, :].astype(BF16), one, perm)
        sbuf[slot, c * MOE_ROW_CHUNK:(c + 1) * MOE_ROW_CHUNK, :] = _pack_bf16_values(
            jnp.dot(perm, lo, preferred_element_type=F32), jnp.dot(perm, hi, preferred_element_type=F32))
        for g in range(c * MOE_ROW_CHUNK // GRANULE, (c + 1) * MOE_ROW_CHUNK // GRANULE):
            pltpu.make_async_copy(sbuf.at[slot, pl.ds(g * GRANULE, GRANULE), :],
                                  xs_ref.at[pl.ds(pl.multiple_of(gtab_ref[0, 0, g], GRANULE), GRANULE), :],
                                  sem.at[slot]).start(priority=g % 2)

    @pl.when(i == n_tiles - 1)
    def _():
        wait_tile(slot)
        if n_tiles > 1:
            wait_tile(1 - slot)


def _dispatch(hp, plan, n_rows):
    N, W = hp.shape
    n_tiles = N // MOE_TILE
    n_blocks = n_rows // EXPERT_BLOCK
    return pl.pallas_call(
        functools.partial(_dispatch_kernel, n_blocks=n_blocks, n_tiles=n_tiles),
        grid_spec=pltpu.PrefetchScalarGridSpec(
            num_scalar_prefetch=3, grid=(n_tiles,),
            in_specs=[pl.BlockSpec((1, 1, MOE_TILE), lambda i, *_: (i, 0, 0), memory_space=pltpu.SMEM),
                      pl.BlockSpec((8, MOE_TILE), lambda i, *_: (0, i)),
                      pl.BlockSpec((MOE_TILE, W), lambda i, *_: (i, 0))],
            out_specs=pl.BlockSpec(memory_space=pl.ANY),
            scratch_shapes=[pltpu.VMEM((EXPERT_BLOCK, W), jnp.uint32),
                            pltpu.VMEM((2, MOE_SORTED_ROWS, W), jnp.uint32),
                            pltpu.SemaphoreType.DMA((2,)), pltpu.SemaphoreType.DMA]),
        out_shape=jax.ShapeDtypeStruct((n_rows + MOE_SPARE_ROWS, W), jnp.uint32),
        compiler_params=_cparams(("arbitrary",)),
        name="dispatch",
    )(plan["pad_end"], plan["padded"], plan["n_used"], plan["gtab_out"], plan["pos"], hp)


def _experts_kernel(be_ref, nu_ref, xs_ref, wgf_ref, wuf_ref, wdf_ref, ys_ref, wg_ref, wu_ref, wd_ref):
    i = pl.program_id(0)
    live = i < nu_ref[0]
    new_expert = jnp.logical_or(i == 0, be_ref[i] != be_ref[jnp.maximum(i - 1, 0)])

    @pl.when(jnp.logical_not(live))
    def _():
        ys_ref[...] = jnp.zeros_like(ys_ref)

    @pl.when(jnp.logical_and(live, new_expert))
    def _():
        wg_ref[...] = wgf_ref[...].astype(BF16)
        wu_ref[...] = wuf_ref[...].astype(BF16)
        wd_ref[...] = wdf_ref[...].astype(BF16)

    @pl.when(live)
    def _():
        half = D_MODEL // 2
        lo, hi = _unpack_halves(xs_ref[...])
        lo = lo.astype(BF16)
        hi = hi.astype(BF16)
        gate = (jnp.dot(lo, wg_ref[0:half, :], preferred_element_type=F32)
                + jnp.dot(hi, wg_ref[half:D_MODEL, :], preferred_element_type=F32))
        up = (jnp.dot(lo, wu_ref[0:half, :], preferred_element_type=F32)
              + jnp.dot(hi, wu_ref[half:D_MODEL, :], preferred_element_type=F32))
        y = jnp.dot((_silu(gate) * up).astype(BF16), wd_ref[...], preferred_element_type=F32)
        ys_ref[...] = _pack_halves(y)


def _experts(xs, n_rows, blk_e, n_used, wg, wu, wd):
    W = xs.shape[1]
    n_blocks = n_rows // EXPERT_BLOCK
    row_map = lambda i, be, nu: (jnp.minimum(i, nu[0] - 1), 0)
    return pl.pallas_call(
        _experts_kernel,
        grid_spec=pltpu.PrefetchScalarGridSpec(
            num_scalar_prefetch=2, grid=(n_blocks,),
            in_specs=[pl.BlockSpec((EXPERT_BLOCK, W), row_map),
                      pl.BlockSpec((None, D_MODEL, D_EXPERT), lambda i, be, nu: (be[i], 0, 0)),
                      pl.BlockSpec((None, D_MODEL, D_EXPERT), lambda i, be, nu: (be[i], 0, 0)),
                      pl.BlockSpec((None, D_EXPERT, D_MODEL), lambda i, be, nu: (be[i], 0, 0))],
            out_specs=pl.BlockSpec((EXPERT_BLOCK, W), lambda i, be, nu: (i, 0)),
            scratch_shapes=[pltpu.VMEM((D_MODEL, D_EXPERT), BF16), pltpu.VMEM((D_MODEL, D_EXPERT), BF16),
                            pltpu.VMEM((D_EXPERT, D_MODEL), BF16)]),
        out_shape=jax.ShapeDtypeStruct((n_rows, W), jnp.uint32),
        compiler_params=_cparams(("arbitrary",)),
        name="experts",
    )(blk_e, n_used, xs, wg, wu, wd)


def _combine_kernel(gtab_ref, gtab_next_ref, ys_ref, pos_ref, xs_ref, wts_ref, g2_ref, fg_ref, o_ref,
                    buf, sem, *, n_tiles):
    i = pl.program_id(0)
    slot = i % 2

    def fetch(which, table_ref, g):
        pltpu.make_async_copy(ys_ref.at[pl.ds(pl.multiple_of(table_ref[0, 0, g], GRANULE), GRANULE), :],
                              buf.at[which, pl.ds(g * GRANULE, GRANULE), :],
                              sem.at[which]).start(priority=g % 2)

    def wait_tile(which):
        pltpu.make_async_copy(ys_ref.at[pl.ds(0, MOE_SORTED_ROWS), :], buf.at[which], sem.at[which]).wait()

    @pl.when(i == 0)
    def _():
        for g in range(MOE_TILE):
            fetch(0, gtab_ref, g)

    wait_tile(slot)

    pos = pos_ref[...]
    w = wts_ref[...]
    c_iota = lax.broadcasted_iota(I32, (MOE_TILE, MOE_ROW_CHUNK), 1).astype(F32).astype(BF16)
    half = D_MODEL // 2
    r_lo = jnp.zeros((MOE_TILE, half), F32)
    r_hi = jnp.zeros((MOE_TILE, half), F32)
    for c in range(MOE_SORTED_ROWS // MOE_ROW_CHUNK):
        rel = pos - c * MOE_ROW_CHUNK
        rel = jnp.where(jnp.logical_and(rel >= 0, rel < MOE_ROW_CHUNK), rel, -1).astype(F32)
        wb = jnp.zeros((MOE_TILE, MOE_ROW_CHUNK), BF16)
        for k in range(TOP_K):
            wb = jnp.where(c_iota == rel[:, k:k + 1].astype(BF16), w[:, k:k + 1].astype(BF16), wb)
        lo, hi = _unpack_halves(buf[slot, c * MOE_ROW_CHUNK:(c + 1) * MOE_ROW_CHUNK, :])
        r_lo = r_lo + jnp.dot(wb, lo.astype(BF16), preferred_element_type=F32)
        r_hi = r_hi + jnp.dot(wb, hi.astype(BF16), preferred_element_type=F32)
        for g in range(min(c * FETCH_PER_CHUNK, MOE_TILE), min((c + 1) * FETCH_PER_CHUNK, MOE_TILE)):
            fetch(1 - slot, gtab_next_ref, g)
    x3 = xs_ref[...] + g2_ref[...] * jnp.concatenate([r_lo, r_hi], axis=1)
    y = x3 * lax.rsqrt(jnp.mean(x3 * x3, axis=-1, keepdims=True) + RMS_EPS) * fg_ref[...]
    o_ref[...] = y

    @pl.when(i == n_tiles - 1)
    def _():
        wait_tile(1 - slot)


def _combine(ys, plan, xsr, wts_t, mod4, final_g, T):
    N, D = xsr.shape
    W = ys.shape[1]
    tm = MOE_TILE
    tps = T // tm
    n_tiles = N // tm
    tile_tab = lambda off: pl.BlockSpec((1, 1, MOE_TILE), lambda i: (jnp.minimum(i + off, n_tiles - 1), 0, 0),
                                        memory_space=pltpu.SMEM)
    return pl.pallas_call(
        functools.partial(_combine_kernel, n_tiles=n_tiles),
        grid=(n_tiles,),
        in_specs=[tile_tab(0), tile_tab(1),
                  pl.BlockSpec(memory_space=pl.ANY),
                  pl.BlockSpec((tm, 8), lambda i: (i, 0)),
                  pl.BlockSpec((tm, D), lambda i: (i, 0)),
                  pl.BlockSpec((tm, 8), lambda i: (i, 0)),
                  pl.BlockSpec((None, None, 1, D), lambda i: (i // tps, 5, 0, 0)),
                  pl.BlockSpec((1, D), lambda i: (0, 0))],
        out_specs=pl.BlockSpec((tm, D), lambda i: (i, 0)),
        out_shape=jax.ShapeDtypeStruct((N, D), F32),
        scratch_shapes=[pltpu.VMEM((2, MOE_SORTED_ROWS, W), jnp.uint32), pltpu.SemaphoreType.DMA((2,))],
        compiler_params=_cparams(("arbitrary",)),
        name="combine",
    )(plan["gtab_in"], plan["gtab_in"], ys, plan["pos"].T, xsr, wts_t, mod4, final_g.reshape(1, D))


def _pack_w_in(w_in):
    D = w_in.shape[0]
    fox_cols = 4 * HALF + 3 * N_HEADS
    wf = w_in[:, :fox_cols]
    wr = w_in[:, fox_cols:]
    o = 3 * HALF
    pad = lambda a, n: jnp.concatenate([a, jnp.zeros((D, n - a.shape[1]), a.dtype)], axis=1)
    parts = [wf[:, :4 * HALF], wr[:, :o],
             pad(wf[:, 4 * HALF:], LANES),
             pad(wr[:, o:o + DECAY_LORA], LANES),
             pad(wr[:, o + DECAY_LORA:o + DECAY_LORA + ICLR_LORA], LANES),
             pad(wr[:, o + DECAY_LORA + ICLR_LORA:], 2 * LANES)]
    return jnp.concatenate(parts, axis=1).astype(BF16)


def _pack_mu(mu):
    o = 3 * HALF
    pad = lambda a, n: jnp.concatenate([a, jnp.zeros((n - a.shape[0],), a.dtype)])
    small = jnp.concatenate([jnp.zeros((LANES,), mu.dtype),
                             pad(mu[o:o + DECAY_LORA], LANES),
                             pad(mu[o + DECAY_LORA:o + DECAY_LORA + ICLR_LORA], LANES),
                             pad(mu[o + DECAY_LORA + ICLR_LORA:], 2 * LANES)])
    return mu[:o].reshape(1, o), small.reshape(1, Z_SMALL)


def kernel(x, c, norm1_g, norm2_g, ada_w, ada_b, w_in, w_out, fox_qn_g, fox_kn_g, fox_on_g, fox_forget_b,
           rw_mu, rw_w0, rw_decay_up, rw_a0, rw_iclr_up, rw_gate_up, rw_k_k, rw_k_a, rw_r_k, rw_lnx_g,
           rw_lnx_b, router_w, router_bias, exp_w_gate, exp_w_up, exp_w_down, sh_w_gate, sh_w_up,
           sh_w_down, final_g):
    B, T, D = x.shape
    N = B * T
    depth = norm1_g.shape[0]
    assert depth == 1, "the combine kernel fuses the final RMSNorm, so exactly one layer is supported"
    xf = x.reshape(N, D)
    for l in range(depth):
        mod4 = _ada(c, ada_w[l], ada_b[l]).reshape(B, 6, 1, D)
        mu_big, mu_small = _pack_mu(rw_mu[l])
        zm, zs = _inproj(xf, mod4, norm1_g[l], _pack_w_in(w_in[l]), mu_big, mu_small, T)
        qp, kp, vp = _foxprep(zm, zs, fox_forget_b[l], fox_qn_g[l], fox_kn_g[l], T)
        y_fox = _attention(qp, kp, vp, zm, fox_on_g[l], B, T)
        rw = dict(rw_w0=rw_w0[l], rw_decay_up=rw_decay_up[l], rw_a0=rw_a0[l], rw_iclr_up=rw_iclr_up[l],
                  rw_gate_up=rw_gate_up[l], rw_k_k=rw_k_k[l], rw_k_a=rw_k_a[l], rw_r_k=rw_r_k[l],
                  rw_lnx_g=rw_lnx_g[l], rw_lnx_b=rw_lnx_b[l])
        y_rwkv = _rwkv(zm, zs, rw, B, T)
        xsr, hp, idx_t, wts_t, rank_t = _outproj(
            xf, y_fox, y_rwkv, mod4, norm2_g[l], w_out[l], router_w[l], router_bias[l],
            sh_w_gate[l], sh_w_up[l], sh_w_down[l], T)
        plan, n_rows = _moe_plan(idx_t, rank_t)
        xs = _dispatch(hp, plan, n_rows)
        ys = _experts(xs, n_rows, plan["blk_e"], plan["n_used"], exp_w_gate[l], exp_w_up[l], exp_w_down[l])
        xf = _combine(ys, plan, xsr, wts_t.T, mod4, final_g, T)
    return xf.reshape(B, T, D)
```

```python
import functools

import jax
import jax.numpy as jnp
import numpy as np
from jax import lax
from jax.experimental import pallas as pl
from jax.experimental.pallas import tpu as pltpu

F32 = jnp.float32
BF16 = jnp.bfloat16
I32 = jnp.int32
HIGHEST = lax.Precision.HIGHEST

D_MODEL = 1024
HEAD_DIM = 64
N_HEADS = 8
HALF = N_HEADS * HEAD_DIM
RMS_EPS = 1e-6
LNX_EPS = 64e-5
LOG2E = 1.4426950408889634
DECAY_LORA = 64
ICLR_LORA = 64
GATE_LORA = 160
N_EXPERTS = 64
N_GROUPS = 8
GROUP_SIZE = N_EXPERTS // N_GROUPS
TOPK_GROUPS = 4
TOP_K = 6
D_EXPERT = 256
D_SHARED = 256
ROUTED_SCALE = 2.5
EXPERT_BLOCK = 1024

LANES = 128
Z_MAIN = 4 * HALF + 3 * HALF
Z_SMALL = 5 * LANES
VMEM_LIMIT = 56 * 1024 * 1024
ATTN_BLOCK = 512
ATTN_HEADS = 4
DMA_UNROLL = 4
WAIT_CHUNK = 16
MOE_TILE = 256
GRANULE = 8
MOE_SORTED_ROWS = MOE_TILE * TOP_K + N_EXPERTS * GRANULE
MOE_SPARE_ROWS = 2 * MOE_TILE * GRANULE
MIN_GRANULES = MOE_TILE * TOP_K // GRANULE
FETCH_PER_CHUNK = 96
MOE_ROW_CHUNK = 256
RWKV_CHUNK = 64
RWKV_SEQS_PER_STEP = 4


def _cparams(semantics):
    return pltpu.CompilerParams(dimension_semantics=semantics, vmem_limit_bytes=VMEM_LIMIT)


def _mm(a, b):
    return jnp.dot(a.astype(BF16), b.astype(BF16), preferred_element_type=F32)


def _mm_nt(a, b):
    return lax.dot_general(a.astype(BF16), b.astype(BF16), (((1,), (1,)), ((), ())),
                           preferred_element_type=F32)


def _mm_tn(a, b):
    return lax.dot_general(a.astype(BF16), b.astype(BF16), (((0,), (0,)), ((), ())),
                           preferred_element_type=F32)


def _mm_f32(a, b):
    return jnp.dot(a, b, precision=HIGHEST, preferred_element_type=F32)


def _bf16_pieces(x, passes):
    pieces = []
    for _ in range(passes):
        piece = x.astype(BF16)
        pieces.append(piece)
        x = x - piece.astype(F32)
    return pieces


def _mm_split(m01, x, passes=3):
    return sum(jnp.dot(m01, p, preferred_element_type=F32) for p in _bf16_pieces(x, passes))


def _mm_split_r(x, m01, passes=2):
    return sum(jnp.dot(p, m01, preferred_element_type=F32) for p in _bf16_pieces(x, passes))


def _sigmoid(x):
    return 1.0 / (1.0 + jnp.exp(-x))


def _softplus(x):
    return jnp.maximum(x, 0.0) + jnp.log(1.0 + jnp.exp(-jnp.abs(x)))


def _silu(x):
    return x * _sigmoid(x)


def _pack_halves(x):
    w = x.shape[1] // 2
    bits = lambda t: lax.bitcast_convert_type(t.astype(jnp.bfloat16).astype(F32), jnp.uint32)
    return (bits(x[:, 0:w]) >> 16) | (bits(x[:, w:2 * w]) & jnp.uint32(0xFFFF0000))


def _pack_bf16_values(lo, hi):
    return (lax.bitcast_convert_type(lo, jnp.uint32) >> 16) | lax.bitcast_convert_type(hi, jnp.uint32)


def _unpack_halves(p):
    lo = lax.bitcast_convert_type(p << 16, F32)
    hi = lax.bitcast_convert_type(p & jnp.uint32(0xFFFF0000), F32)
    return lo, hi


def _shift_rows(z, carry_ref, first):
    rows = z.shape[0]
    prev_row = jnp.where(first, 0.0, carry_ref[0:1, :])
    prev = pltpu.roll(z, 1, 0)
    row0 = lax.broadcasted_iota(I32, (rows, 1), 0) == 0
    prev = jnp.where(row0, prev_row, prev)
    carry_ref[0:1, :] = z[rows - 1:rows, :]
    return prev


def _ada_kernel(c_ref, w_ref, b_ref, o_ref):
    o_ref[...] = _mm_f32(_silu(c_ref[...]), w_ref[...]) + b_ref[...]


def _ada(c, ada_w, ada_b):
    B, D = c.shape
    n_out = ada_w.shape[1]
    tn = 512
    return pl.pallas_call(
        _ada_kernel,
        grid=(n_out // tn,),
        in_specs=[pl.BlockSpec((B, D), lambda j: (0, 0)),
                  pl.BlockSpec((D, tn), lambda j: (0, j)),
                  pl.BlockSpec((1, tn), lambda j: (0, j))],
        out_specs=pl.BlockSpec((B, tn), lambda j: (0, j)),
        out_shape=jax.ShapeDtypeStruct((B, n_out), F32),
        compiler_params=_cparams(("arbitrary",)),
        name="ada",
    )(c, ada_w, ada_b.reshape(1, n_out))


def _inproj_kernel(x_ref, g_ref, sh_ref, sc_ref, w_ref, mub_ref, mus_ref, zm_ref, zs_ref,
                   carry_b, carry_s, *, tiles_per_seq):
    first = (pl.program_id(0) % tiles_per_seq) == 0
    x = x_ref[...]
    h = x * lax.rsqrt(jnp.mean(x * x, axis=-1, keepdims=True) + RMS_EPS) * g_ref[...]
    hb = (h * (1.0 + sc_ref[...]) + sh_ref[...]).astype(BF16)
    nf = 4 * HALF
    zm_ref[:, 0:nf] = jnp.dot(hb, w_ref[:, 0:nf], preferred_element_type=F32).astype(BF16)
    zr = jnp.dot(hb, w_ref[:, nf:Z_MAIN], preferred_element_type=F32)
    zr = zr + mub_ref[...] * (_shift_rows(zr, carry_b, first) - zr)
    zm_ref[:, nf:Z_MAIN] = zr.astype(BF16)
    zs = jnp.dot(hb, w_ref[:, Z_MAIN:Z_MAIN + Z_SMALL], preferred_element_type=F32)
    zs_ref[...] = zs + mus_ref[...] * (_shift_rows(zs, carry_s, first) - zs)


def _inproj(x2d, mod4, norm_g, w_all, mu_big, mu_small, T):
    N, D = x2d.shape
    tm = min(512, T)
    tps = T // tm
    vec = lambda j: pl.BlockSpec((None, None, 1, D), lambda i: (i // tps, j, 0, 0))
    return pl.pallas_call(
        functools.partial(_inproj_kernel, tiles_per_seq=tps),
        grid=(N // tm,),
        in_specs=[pl.BlockSpec((tm, D), lambda i: (i, 0)),
                  pl.BlockSpec((1, D), lambda i: (0, 0)),
                  vec(0), vec(1),
                  pl.BlockSpec((D, Z_MAIN + Z_SMALL), lambda i: (0, 0)),
                  pl.BlockSpec((1, 3 * HALF), lambda i: (0, 0)),
                  pl.BlockSpec((1, Z_SMALL), lambda i: (0, 0))],
        out_specs=[pl.BlockSpec((tm, Z_MAIN), lambda i: (i, 0)),
                   pl.BlockSpec((tm, Z_SMALL), lambda i: (i, 0))],
        out_shape=[jax.ShapeDtypeStruct((N, Z_MAIN), BF16),
                   jax.ShapeDtypeStruct((N, Z_SMALL), F32)],
        scratch_shapes=[pltpu.VMEM((8, 3 * HALF), F32), pltpu.VMEM((8, Z_SMALL), F32)],
        compiler_params=_cparams(("arbitrary",)),
        name="inproj",
    )(x2d, norm_g.reshape(1, D), mod4, mod4, w_all, mu_big, mu_small)


def _foxprep_kernel(q_ref, k_ref, v_ref, zs_ref, fb_ref, qg_ref, kg_ref, tri_ref, eexp_ref,
                    esum_ref, e8_ref, plq_ref, plk_ref, plv_ref, cq_ref, ck_ref, cv_ref,
                    qp_ref, kp_ref, vp_ref, carry_k, carry_v, carry_c, *, tiles_per_seq):
    first = (pl.program_id(0) % tiles_per_seq) == 0
    zs = zs_ref[...]
    logf = -_softplus(-(zs + fb_ref[...]))
    cum = _mm_split(tri_ref[...], logf) + jnp.where(first, 0.0, carry_c[0:1, :])
    carry_c[0:1, :] = cum[cum.shape[0] - 1:, :]
    cum2 = cum * LOG2E
    c_hi = cum2.astype(BF16)
    r1 = cum2 - c_hi.astype(F32)
    c_mid = r1.astype(BF16)
    c_lo = (r1 - c_mid.astype(F32)).astype(BF16)
    a_full = _mm_split_r(_sigmoid(zs), eexp_ref[...])
    k = k_ref[...].astype(F32)
    v = v_ref[...].astype(F32)
    a_k = a_full[:, 0:HALF]
    a_v = a_full[:, HALF:2 * HALF]
    k = a_k * _shift_rows(k, carry_k, first) + (1.0 - a_k) * k
    v = a_v * _shift_rows(v, carry_v, first) + (1.0 - a_v) * v
    q = q_ref[...].astype(F32)

    def head_rms(t, gain):
        ms = _mm_split_r(t * t, esum_ref[...]) * (1.0 / HEAD_DIM)
        inv = _mm_split_r(lax.rsqrt(ms + RMS_EPS), e8_ref[...])
        return t * inv * gain

    qn = head_rms(q, qg_ref[...]) * (HEAD_DIM ** -0.5 * LOG2E)
    kn = head_rms(k, kg_ref[...])
    aug = [c_hi, c_mid, c_lo]
    lhs_q = jnp.concatenate([qn.astype(BF16)] + aug, axis=1)
    lhs_k = jnp.concatenate([kn.astype(BF16)] + aug, axis=1)
    qp_ref[...] = (jnp.dot(lhs_q, plq_ref[...], preferred_element_type=F32) + cq_ref[...]).astype(BF16)
    kp_ref[...] = (jnp.dot(lhs_k, plk_ref[...], preferred_element_type=F32) + ck_ref[...]).astype(BF16)
    vp_ref[...] = (jnp.dot(v.astype(BF16), plv_ref[...], preferred_element_type=F32)
                   + cv_ref[...]).astype(BF16)


def _fox_constants(tm):
    hp = N_HEADS * LANES
    eexp = np.zeros((LANES, 2 * HALF), np.float32)
    esum = np.zeros((HALF, LANES), np.float32)
    e8 = np.zeros((LANES, HALF), np.float32)
    plq = np.zeros((HALF + 3 * LANES, hp), np.float32)
    plk = np.zeros((HALF + 3 * LANES, hp), np.float32)
    plv = np.zeros((HALF, hp), np.float32)
    cq = np.zeros((1, hp), np.float32)
    ck = np.zeros((1, hp), np.float32)
    cv = np.zeros((1, hp), np.float32)
    for h in range(N_HEADS):
        sl = slice(h * HEAD_DIM, (h + 1) * HEAD_DIM)
        eexp[8 + h, sl] = 1.0
        eexp[16 + h, HALF + h * HEAD_DIM:HALF + (h + 1) * HEAD_DIM] = 1.0
        esum[sl, h] = 1.0
        e8[h, sl] = 1.0
        base = h * LANES
        for d in range(HEAD_DIM):
            plq[h * HEAD_DIM + d, base + d] = 1.0
            plk[h * HEAD_DIM + d, base + d] = 1.0
            plv[h * HEAD_DIM + d, base + d] = 1.0
        for j in range(3):
            plq[HALF + j * LANES + h, base + HEAD_DIM + j] = 1.0
            plk[HALF + j * LANES + h, base + HEAD_DIM + 3 + j] = -1.0
            cq[0, base + HEAD_DIM + 3 + j] = 1.0
            ck[0, base + HEAD_DIM + j] = 1.0
        cv[0, base + HEAD_DIM] = 1.0
    tri = np.tril(np.ones((tm, tm), np.float32))
    bf = lambda a: jnp.asarray(a, BF16)
    return dict(tri=bf(tri), eexp=bf(eexp), esum=bf(esum), e8=bf(e8),
                plq=bf(plq), plk=bf(plk), plv=bf(plv), cq=jnp.asarray(cq), ck=jnp.asarray(ck),
                cv=jnp.asarray(cv))


def _foxprep(zm, zs, forget_b, qn_g, kn_g, T):
    N = zm.shape[0]
    tm = min(512, T)
    tps = T // tm
    cst = _fox_constants(tm)
    hp = N_HEADS * LANES
    fb = jnp.zeros((1, LANES), F32).at[0, :N_HEADS].set(forget_b)
    full = lambda a: pl.BlockSpec(a.shape, lambda i: (0, 0))
    consts = [fb, qn_g.reshape(1, HALF), kn_g.reshape(1, HALF), cst["tri"], cst["eexp"], cst["esum"],
              cst["e8"], cst["plq"], cst["plk"], cst["plv"], cst["cq"], cst["ck"], cst["cv"]]
    return pl.pallas_call(
        functools.partial(_foxprep_kernel, tiles_per_seq=tps),
        grid=(N // tm,),
        in_specs=[pl.BlockSpec((tm, HALF), lambda i: (i, 0)),
                  pl.BlockSpec((tm, HALF), lambda i: (i, 1)),
                  pl.BlockSpec((tm, HALF), lambda i: (i, 2)),
                  pl.BlockSpec((tm, LANES), lambda i: (i, 0))] + [full(a) for a in consts],
        out_specs=[pl.BlockSpec((tm, hp), lambda i: (i, 0))] * 3,
        out_shape=[jax.ShapeDtypeStruct((N, hp), BF16)] * 3,
        scratch_shapes=[pltpu.VMEM((8, HALF), F32), pltpu.VMEM((8, HALF), F32),
                        pltpu.VMEM((8, LANES), F32)],
        compiler_params=_cparams(("arbitrary",)),
        name="foxprep",
    )(zm, zm, zm, zs, *consts)


def _attn_kernel(q_ref, k_ref, v_ref, g_ref, ong_ref, o_ref, vt_ref, *, tq):
    i = pl.program_id(2)
    n_kv = vt_ref.shape[1]
    heads = range(ATTN_HEADS)
    lanes = [slice(hh * LANES, (hh + 1) * LANES) for hh in heads]

    @pl.when(i == 0)
    def _():
        for hh in heads:
            for c in range(n_kv):
                vt_ref[hh, c] = v_ref[c * tq:(c + 1) * tq, lanes[hh]].T

    key = lax.broadcasted_iota(I32, (tq, tq), 0)
    qry = lax.broadcasted_iota(I32, (tq, tq), 1)
    causal = key <= qry
    qs = [q_ref[:, lanes[hh]] for hh in heads]

    def step(j, carry, masked):
        m, acc = carry
        start = pl.multiple_of(j * tq, tq)
        s = [lax.dot_general(k_ref[pl.ds(start, tq), lanes[hh]], qs[hh], (((1,), (1,)), ((), ())),
                             preferred_element_type=F32) for hh in heads]
        m_out, acc_out = [], []
        for hh in heads:
            sh = jnp.where(causal, s[hh], -jnp.inf) if masked else s[hh]
            m_new = jnp.maximum(m[hh], jnp.max(sh, axis=0, keepdims=True))
            p = jnp.exp2(sh - m_new)
            acc_out.append(jnp.exp2(m[hh] - m_new) * acc[hh]
                           + jnp.dot(vt_ref[hh, j], p.astype(BF16), preferred_element_type=F32))
            m_out.append(m_new)
        return tuple(m_out), tuple(acc_out)

    init = (tuple(jnp.full((1, tq), -jnp.inf, F32) for _ in heads),
            tuple(jnp.zeros((LANES, tq), F32) for _ in heads))
    carry = lax.fori_loop(0, i, functools.partial(step, masked=False), init)
    _, acc = step(i, carry, True)
    chan = lax.broadcasted_iota(I32, (LANES, tq), 0)
    chan_w = jnp.where(chan < HEAD_DIM, 1.0 / HEAD_DIM, jnp.where(chan == HEAD_DIM, RMS_EPS, 0.0))
    lane = lax.broadcasted_iota(I32, (tq, LANES), 1)
    outs = []
    for hh in heads:
        t = jnp.sum(acc[hh] * acc[hh] * chan_w, axis=0, keepdims=True)
        outs.append((acc[hh] * lax.rsqrt(t)).T)
    o = jnp.concatenate([jnp.where(lane < HEAD_DIM, outs[2 * p], pltpu.roll(outs[2 * p + 1], HEAD_DIM, 1))
                         for p in range(ATTN_HEADS // 2)], axis=1)
    y = o * ong_ref[...] * _sigmoid(g_ref[...].astype(F32))
    o_ref[...] = y.astype(BF16)


def _attention(qp, kp, vp, zm, on_g, B, T):
    N = qp.shape[0]
    tq = min(ATTN_BLOCK, T)
    nq = T // tq
    groups = N_HEADS // ATTN_HEADS
    wp = ATTN_HEADS * LANES
    wo = ATTN_HEADS * HEAD_DIM
    g_col0 = 3 * HALF // wo
    return pl.pallas_call(
        functools.partial(_attn_kernel, tq=tq),
        grid=(B, groups, nq),
        in_specs=[pl.BlockSpec((tq, wp), lambda b, p, i: (b * nq + i, p)),
                  pl.BlockSpec((T, wp), lambda b, p, i: (b, p)),
                  pl.BlockSpec((T, wp), lambda b, p, i: (b, p)),
                  pl.BlockSpec((tq, wo), lambda b, p, i: (b * nq + i, g_col0 + p)),
                  pl.BlockSpec((None, 1, wo), lambda b, p, i: (p, 0, 0))],
        out_specs=pl.BlockSpec((tq, wo), lambda b, p, i: (b * nq + i, p)),
        out_shape=jax.ShapeDtypeStruct((N, HALF), BF16),
        scratch_shapes=[pltpu.VMEM((ATTN_HEADS, nq, LANES, tq), BF16)],
        compiler_params=_cparams(("arbitrary", "arbitrary", "arbitrary")),
        name="attn",
    )(qp, kp, vp, zm, on_g.reshape(groups, 1, wo))


def _rwkv_kernel(r_ref, k_ref, v_ref, zs_ref, w0_ref, dup_ref, a0_ref, iup_ref, gup_ref, kk_ref,
                 ka_ref, rk_ref, lng_ref, lnb_ref, tri_ref, o_ref, state, ybuf, *, chunk, nb):
    C = chunk
    c_idx = pl.program_id(1)

    @pl.when(c_idx == 0)
    def _():
        state[...] = jnp.zeros_like(state)

    R = nb * C
    r = r_ref[...].reshape(R, HALF).astype(F32)
    k = k_ref[...].reshape(R, HALF).astype(F32)
    v = v_ref[...].reshape(R, HALF).astype(F32)
    zs = zs_ref[...].reshape(R, Z_SMALL)
    wd = zs[:, LANES:2 * LANES]
    ad = zs[:, 2 * LANES:3 * LANES]
    gd = zs[:, 3 * LANES:5 * LANES]
    wl = w0_ref[...] + _mm(jnp.tanh(wd), dup_ref[...])
    lw = -jnp.exp(-_softplus(-wl) - 0.5)
    a = _sigmoid(a0_ref[...] + _mm(ad, iup_ref[...]))
    g = _mm(_sigmoid(gd), gup_ref[...])
    kk = k * kk_ref[...]
    k2 = k * (1.0 + (a - 1.0) * ka_ref[...])
    cl = _mm_split(tri_ref[...], lw)
    cl_end = jnp.concatenate(
        [jnp.broadcast_to(cl[(bi + 1) * C - 1:(bi + 1) * C, :], (C, HALF)) for bi in range(nb)], axis=0)
    e_neg = jnp.exp(-cl)
    e_tail = jnp.exp(cl_end - cl)
    pre = dict(rt=r * jnp.exp(cl), at=-kk * jnp.exp(cl - lw), kh=k2 * e_neg, bh=kk * a * e_neg,
               kb=k2 * e_tail, bb=kk * a * e_tail, v=v, kk=kk, pend=jnp.exp(cl_end),
               rkr=r * k2 * rk_ref[...], g=g)

    C2 = 2 * C
    row = lax.broadcasted_iota(I32, (C2, C2), 0)
    col = lax.broadcasted_iota(I32, (C2, C2), 1)
    lower = (col & (C - 1)) <= (row & (C - 1))
    strict = (col & (C - 1)) < (row & (C - 1))
    eye = row == col
    head0 = lax.broadcasted_iota(I32, (C, LANES), 1) < HEAD_DIM
    n_sq = int(np.log2(C)) - 1
    units = [(bi, p) for bi in range(nb) for p in range(N_HEADS // 2)]

    def part(name, bi, p):
        return pre[name][bi * C:(bi + 1) * C, p * LANES:(p + 1) * LANES]

    def stack(x):
        return jnp.concatenate([jnp.where(head0, x, 0.0), jnp.where(head0, 0.0, x)], axis=0)

    def unstack(x):
        return x[0:C] + x[C:C2]

    def head_sum(x):
        s0 = jnp.sum(jnp.where(head0, x, 0.0), axis=1, keepdims=True)
        s1 = jnp.sum(jnp.where(head0, 0.0, x), axis=1, keepdims=True)
        return jnp.where(head0, s0, s1)

    ops = []
    for bi, p in units:
        kk_p = part("kk", bi, p)
        inv = 1.0 / jnp.maximum(jnp.sqrt(head_sum(kk_p * kk_p)), 1e-12)
        ops.append(dict(rt=stack(part("rt", bi, p)), at=stack(part("at", bi, p) * inv),
                        kh=stack(part("kh", bi, p)).astype(BF16), bh=stack(part("bh", bi, p) * inv).astype(BF16),
                        kb=stack(part("kb", bi, p)).astype(BF16), bb=stack(part("bb", bi, p) * inv).astype(BF16),
                        v=stack(part("v", bi, p)).astype(BF16)))
    ra = [jnp.concatenate([o["rt"], o["at"]], axis=0).astype(BF16) for o in ops]
    gg = [_mm_nt(x, jnp.concatenate([o["kh"], o["bh"]], axis=0)) for x, o in zip(ra, ops)]
    g1 = [t[:, 0:C2] for t in gg]
    g2 = [t[:, C2:2 * C2] for t in gg]
    a_rb = [jnp.where(lower, t[0:C2], 0.0).astype(BF16) for t in g2]
    pw = [jnp.where(strict, t[C2:2 * C2], 0.0) for t in g2]
    av = [_mm(jnp.concatenate([jnp.where(lower, t[0:C2], 0.0).astype(BF16),
                               jnp.where(strict, t[C2:2 * C2], 0.0).astype(BF16), o["kb"].T], axis=0),
              o["v"]) for t, o in zip(g1, ops)]
    swap = lambda t: pltpu.roll(t, HEAD_DIM, 1)
    own = (row < C) == (col < HEAD_DIM)
    xs = [o["at"] + swap(t[C2:2 * C2]) for o, t in zip(ops, av)]
    for level in range(n_sq + 1):
        pb = [p.astype(BF16) for p in pw]
        xs = [x + _mm(p, x) for p, x in zip(pb, xs)]
        if level < n_sq:
            pw = [jnp.dot(p, p, preferred_element_type=F32) for p in pb]
    rbx = [_mm(jnp.concatenate([p, o["bb"].T], axis=0), x) for p, o, x in zip(a_rb, ops, xs)]
    for u, (bi, p) in enumerate(units):
        o = ops[u]
        sl = slice(p * LANES, (p + 1) * LANES)
        rb, bx, kv = rbx[u][0:C2], rbx[u][C2:2 * C2], av[u][2 * C2:3 * C2]
        r2 = unstack(o["rt"] + jnp.where(own, rb, 0.0))
        y0 = swap(unstack(jnp.where(own, 0.0, rb))) + unstack(av[u][0:C2])
        m_mat = jnp.where(eye, part("pend", bi, p)[0:1, :], 0.0) + jnp.where(own, bx, 0.0)
        g_mat = swap(jnp.where(own, 0.0, bx)) + kv
        out = _mm(jnp.concatenate([r2, m_mat], axis=0), state[bi, p])
        state[bi, p] = out[C:C + LANES] + g_mat
        y = out[0:C] + y0
        cen = y - head_sum(y) * (1.0 / HEAD_DIM)
        var = head_sum(cen * cen) * (1.0 / HEAD_DIM)
        bonus = head_sum(part("rkr", bi, p)) * part("v", bi, p)
        y = cen * lax.rsqrt(var + LNX_EPS) * lng_ref[:, sl] + lnb_ref[:, sl] + bonus
        ybuf[bi, :, sl] = y * part("g", bi, p)
    o_ref[...] = ybuf[...].astype(BF16)


def _rwkv(zm, zs, p, B, T):
    N = zm.shape[0]
    C = RWKV_CHUNK
    assert 2 * C == LANES and T % C == 0, "a head pair's stacked chunk must fill one 128-row tile"
    nc = T // C
    nb = RWKV_SEQS_PER_STEP if B % RWKV_SEQS_PER_STEP == 0 else 1
    tri = np.kron(np.eye(nb, dtype=np.float32), np.tril(np.ones((C, C), np.float32)))
    pad_rows = lambda w, rows: jnp.zeros((rows, HALF), F32).at[:w.shape[0]].set(w)
    row = lambda a: a.reshape(1, HALF)
    consts = [row(p["rw_w0"]), pad_rows(p["rw_decay_up"], LANES), row(p["rw_a0"]),
              pad_rows(p["rw_iclr_up"], LANES), pad_rows(p["rw_gate_up"], 2 * LANES),
              row(p["rw_k_k"]), row(p["rw_k_a"]), row(p["rw_r_k"]), row(p["rw_lnx_g"]),
              row(p["rw_lnx_b"]), jnp.asarray(tri, BF16)]
    full = lambda a: pl.BlockSpec(a.shape, lambda b, c: (0, 0))
    rcol = 4 * HALF // HALF
    zm3 = zm.reshape(B, T, Z_MAIN)
    out = pl.pallas_call(
        functools.partial(_rwkv_kernel, chunk=C, nb=nb),
        grid=(B // nb, nc),
        in_specs=[pl.BlockSpec((nb, C, HALF), lambda b, c: (b, c, rcol)),
                  pl.BlockSpec((nb, C, HALF), lambda b, c: (b, c, rcol + 1)),
                  pl.BlockSpec((nb, C, HALF), lambda b, c: (b, c, rcol + 2)),
                  pl.BlockSpec((nb, C, Z_SMALL), lambda b, c: (b, c, 0))] + [full(a) for a in consts],
        out_specs=pl.BlockSpec((nb, C, HALF), lambda b, c: (b, c, 0)),
        out_shape=jax.ShapeDtypeStruct((B, T, HALF), BF16),
        scratch_shapes=[pltpu.VMEM((nb, N_HEADS // 2, LANES, LANES), F32), pltpu.VMEM((nb, C, HALF), F32)],
        compiler_params=_cparams(("arbitrary", "arbitrary")),
        name="rwkv",
    )(zm3, zm3, zm3, zs.reshape(B, T, Z_SMALL), *consts)
    return out.reshape(N, HALF)


def _first_index(mask, iota, big):
    return jnp.min(jnp.where(mask, iota, big), axis=0, keepdims=True)


def _outproj_kernel(x_ref, yf_ref, yr_ref, g1_ref, sh2_ref, sc2_ref, g2_ref, n2g_ref, wo_ref, rwt_ref,
                    rb_ref, swg_ref, swu_ref, swd_ref, ustrict_ref,
                    xs_ref, hp_ref, idx_ref, wts_ref, rank_ref, *, tm):
    d = (jnp.dot(yf_ref[...], wo_ref[0:HALF, :], preferred_element_type=F32)
         + jnp.dot(yr_ref[...], wo_ref[HALF:2 * HALF, :], preferred_element_type=F32))
    x2 = x_ref[...] + g1_ref[...] * d
    h = x2 * lax.rsqrt(jnp.mean(x2 * x2, axis=-1, keepdims=True) + RMS_EPS) * n2g_ref[...]
    h = h * (1.0 + sc2_ref[...]) + sh2_ref[...]
    hb = h.astype(BF16)
    act = _silu(jnp.dot(hb, swg_ref[...], preferred_element_type=F32)) * jnp.dot(
        hb, swu_ref[...], preferred_element_type=F32)
    shared = jnp.dot(act.astype(BF16), swd_ref[...], preferred_element_type=F32)
    xs_ref[...] = x2 + g2_ref[...] * shared
    half = D_MODEL // 2
    hp_ref[...] = _pack_halves(h)

    logits = lax.dot_general(rwt_ref[...], h, (((1,), (1,)), ((), ())), precision=HIGHEST,
                             preferred_element_type=F32)
    scores = _sigmoid(logits)
    sel = scores + rb_ref[:, 0:1]
    neg = -jnp.inf
    sel3 = sel.reshape(N_GROUPS, GROUP_SIZE, tm)
    io_in = lax.broadcasted_iota(I32, (N_GROUPS, GROUP_SIZE, tm), 1)
    m1 = jnp.max(sel3, axis=1, keepdims=True)
    f1 = jnp.min(jnp.where(sel3 == m1, io_in, GROUP_SIZE), axis=1, keepdims=True)
    m2 = jnp.max(jnp.where(io_in == f1, neg, sel3), axis=1, keepdims=True)
    gs = (m1 + m2).reshape(N_GROUPS, tm)
    io_g = lax.broadcasted_iota(I32, (N_GROUPS, tm), 0)
    gmask = jnp.zeros((N_GROUPS, tm), jnp.bool_)
    for _ in range(TOPK_GROUPS):
        mg = jnp.max(gs, axis=0, keepdims=True)
        fg = _first_index(gs == mg, io_g, N_GROUPS)
        pick = io_g == fg
        gmask = jnp.logical_or(gmask, pick)
        gs = jnp.where(pick, neg, gs)
    emask = jnp.broadcast_to(gmask.reshape(N_GROUPS, 1, tm), (N_GROUPS, GROUP_SIZE, tm)).reshape(N_EXPERTS, tm)
    cur = jnp.where(emask, sel, neg)
    io_e = lax.broadcasted_iota(I32, (N_EXPERTS, tm), 0)
    picks, idxs, wts = [], [], []
    for _ in range(TOP_K):
        me = jnp.max(cur, axis=0, keepdims=True)
        fe = _first_index(cur == me, io_e, N_EXPERTS)
        pick = io_e == fe
        picks.append(pick)
        idxs.append(fe)
        wts.append(jnp.sum(jnp.where(pick, scores, 0.0), axis=0, keepdims=True))
        cur = jnp.where(pick, neg, cur)
    wsum = wts[0]
    for w in wts[1:]:
        wsum = wsum + w
    zero_i = jnp.zeros((1, tm), I32)
    zero_f = jnp.zeros((1, tm), F32)
    idx_ref[...] = jnp.concatenate(idxs + [zero_i, zero_i], axis=0)
    wts_ref[...] = jnp.concatenate([w / wsum * ROUTED_SCALE for w in wts] + [zero_f, zero_f], axis=0)
    cnt = picks[0].astype(F32)
    for pk in picks[1:]:
        cnt = cnt + pk.astype(F32)
    excl = jnp.dot(cnt.astype(BF16), ustrict_ref[...], preferred_element_type=F32)
    ranks = [jnp.sum(jnp.where(pk, excl, 0.0), axis=0, keepdims=True).astype(I32) for pk in picks]
    rank_ref[...] = jnp.concatenate(ranks + [zero_i, zero_i], axis=0)


def _outproj(x2d, yf, yr, mod4, norm2_g, w_out, router_w, router_bias, swg, swu, swd, T):
    N, D = x2d.shape
    tm = min(512, T)
    tps = T // tm
    vec = lambda j: pl.BlockSpec((None, None, 1, D), lambda i: (i // tps, j, 0, 0))
    full = lambda a: pl.BlockSpec(a.shape, lambda i: (0, 0))
    ts = min(MOE_TILE, tm)
    ustrict = jnp.asarray(np.kron(np.eye(tm // ts, dtype=np.float32),
                                  np.triu(np.ones((ts, ts), np.float32), 1)), BF16)
    rb = jnp.broadcast_to(router_bias.reshape(N_EXPERTS, 1), (N_EXPERTS, LANES))
    consts = [norm2_g.reshape(1, D), w_out.astype(BF16), router_w.T, rb, swg.astype(BF16),
              swu.astype(BF16), swd.astype(BF16), ustrict]
    small = lambda dt: jax.ShapeDtypeStruct((8, N), dt)
    return pl.pallas_call(
        functools.partial(_outproj_kernel, tm=tm),
        grid=(N // tm,),
        in_specs=[pl.BlockSpec((tm, D), lambda i: (i, 0)),
                  pl.BlockSpec((tm, HALF), lambda i: (i, 0)),
                  pl.BlockSpec((tm, HALF), lambda i: (i, 0)),
                  vec(2), vec(3), vec(4), vec(5)] + [full(a) for a in consts],
        out_specs=[pl.BlockSpec((tm, D), lambda i: (i, 0)),
                   pl.BlockSpec((tm, D // 2), lambda i: (i, 0)),
                   pl.BlockSpec((8, tm), lambda i: (0, i)),
                   pl.BlockSpec((8, tm), lambda i: (0, i)),
                   pl.BlockSpec((8, tm), lambda i: (0, i))],
        out_shape=[jax.ShapeDtypeStruct((N, D), F32), jax.ShapeDtypeStruct((N, D // 2), jnp.uint32),
                   small(I32), small(F32), small(I32)],
        compiler_params=_cparams(("arbitrary",)),
        name="outproj",
    )(x2d, yf, yr, mod4, mod4, mod4, mod4, *consts)


def _for_each(count, fn, group):
    full = count // group

    def trip(t, c):
        for u in range(group):
            fn(t * group + u)
        return c

    def single(g, c):
        fn(g)
        return c

    lax.fori_loop(0, full, trip, 0)
    lax.fori_loop(full * group, count, single, 0)


def _wait_granules(count, wait_rows):
    full = count // WAIT_CHUNK

    def chunk(_, c):
        wait_rows(WAIT_CHUNK * GRANULE)
        return c

    def single(_, c):
        wait_rows(GRANULE)
        return c

    lax.fori_loop(0, full, chunk, 0)
    lax.fori_loop(full * WAIT_CHUNK, count, single, 0)


def _moe_plan(idx_t, rank_t):
    n = idx_t.shape[1]
    n_tiles = n // MOE_TILE
    experts = jnp.arange(N_EXPERTS, dtype=I32)
    hot = idx_t[:TOP_K, :, None] == experts
    cnt = jnp.sum(hot.reshape(TOP_K, n_tiles, MOE_TILE, N_EXPERTS).astype(I32), axis=(0, 2))
    gran = (cnt + GRANULE - 1) // GRANULE
    loc_end = jnp.cumsum(gran, axis=1)
    loc_off = loc_end - gran
    g_tile = loc_end[:, N_EXPERTS - 1]
    padded = (GRANULE * jnp.sum(gran, axis=0) + EXPERT_BLOCK - 1) // EXPERT_BLOCK * EXPERT_BLOCK
    pad_end = jnp.cumsum(padded)
    glob_off = (pad_end - padded)[None, :] + GRANULE * (jnp.cumsum(gran, axis=0) - gran)
    loc_tok = jnp.repeat(GRANULE * loc_off, MOE_TILE, axis=0)
    pos = rank_t[:TOP_K] + jnp.sum(jnp.where(hot, loc_tok[None], 0), axis=-1)
    pos = jnp.concatenate([pos, jnp.full((8 - TOP_K, n), -1, I32)], axis=0)
    g = jnp.arange(MOE_TILE, dtype=I32)
    e_of_g = jnp.minimum(jnp.sum((loc_end[:, None, :] <= g[None, :, None]).astype(I32), axis=-1), N_EXPERTS - 1)
    dst = jnp.sum(jnp.where(e_of_g[:, :, None] == experts,
                            glob_off[:, None, :] + GRANULE * (g[None, :, None] - loc_off[:, None, :]), 0), axis=-1)
    n_blocks = -(-(n * TOP_K + GRANULE * N_EXPERTS * n_tiles) // EXPERT_BLOCK) + N_EXPERTS
    n_rows = n_blocks * EXPERT_BLOCK
    live = g[None, :] < g_tile[:, None]
    spare = n_rows + (jnp.arange(n_tiles, dtype=I32)[:, None] % 2) * (MOE_TILE * GRANULE) + g[None, :] * GRANULE
    gtab_in = jnp.where(live, dst, 0).reshape(n_tiles, 1, MOE_TILE)
    gtab_out = jnp.where(live, dst, spare).reshape(n_tiles, 1, MOE_TILE)
    blk_start = jnp.arange(n_blocks, dtype=I32) * EXPERT_BLOCK
    blk_e = jnp.minimum(jnp.sum((pad_end[None, :] <= blk_start[:, None]).astype(I32), axis=1), N_EXPERTS - 1)
    plan = dict(pos=pos.astype(I32), gtab_in=gtab_in.astype(I32), gtab_out=gtab_out.astype(I32),
                g_tile=g_tile.astype(I32),
                pad_end=pad_end.astype(I32), padded=padded.astype(I32), blk_e=blk_e.astype(I32),
                n_used=(pad_end[N_EXPERTS - 1:] // EXPERT_BLOCK).astype(I32))
    return plan, n_rows


def _dispatch_kernel(pend_ref, padded_ref, nu_ref, gtab_ref, pos_ref, hp_ref, xs_ref, zeros, sbuf,
                     sem, zsem, *, n_blocks, n_tiles):
    @pl.when(pl.program_id(0) == 0)
    def _():
        zeros[...] = jnp.zeros_like(zeros)

        def block_copy(start):
            return pltpu.make_async_copy(zeros, xs_ref.at[pl.ds(pl.multiple_of(start, EXPERT_BLOCK),
                                                                 EXPERT_BLOCK), :], zsem)

        def pad_loop(fn):
            def body(e, _):
                @pl.when(padded_ref[e] > 0)
                def _():
                    fn(block_copy(pend_ref[e] - EXPERT_BLOCK))
                return 0
            lax.fori_loop(0, N_EXPERTS, body, 0)

        def tail_loop(fn):
            def body(b, _):
                fn(block_copy(b * EXPERT_BLOCK))
                return 0
            lax.fori_loop(nu_ref[0], n_blocks + MOE_SPARE_ROWS // EXPERT_BLOCK, body, 0)

        pad_loop(lambda cp: cp.start())
        tail_loop(lambda cp: cp.start())
        pad_loop(lambda cp: cp.wait())
        tail_loop(lambda cp: cp.wait())

    i = pl.program_id(0)
    slot = i % 2

    def wait_tile(which):
        pltpu.make_async_copy(sbuf.at[which], xs_ref.at[pl.ds(0, MOE_SORTED_ROWS), :], sem.at[which]).wait()

    @pl.when(i >= 2)
    def _():
        wait_tile(slot)

    pos = pos_ref[...]
    lo, hi = _unpack_halves(hp_ref[...])
    lo = lo.astype(BF16)
    hi = hi.astype(BF16)
    r_iota = lax.broadcasted_iota(I32, (MOE_ROW_CHUNK, MOE_TILE), 0).astype(F32).astype(BF16)
    one = jnp.ones((), BF16)
    for c in range(MOE_SORTED_ROWS // MOE_ROW_CHUNK):
        rel = pos - c * MOE_ROW_CHUNK
        rel = jnp.where(jnp.logical_and(rel >= 0, rel < MOE_ROW_CHUNK), rel, -1).astype(F32)
        perm = jnp.zeros((MOE_ROW_CHUNK, MOE_TILE), BF16)
        for k in range(TOP_K):
            perm = jnp.where(r_iota == rel[k:k + 1, :].astype(BF16), one, perm)
        sbuf[slot, c * MOE_ROW_CHUNK:(c + 1) * MOE_ROW_CHUNK, :] = _pack_bf16_values(
            jnp.dot(perm, lo, preferred_element_type=F32), jnp.dot(perm, hi, preferred_element_type=F32))
        for g in range(c * MOE_ROW_CHUNK // GRANULE, (c + 1) * MOE_ROW_CHUNK // GRANULE):
            pltpu.make_async_copy(sbuf.at[slot, pl.ds(g * GRANULE, GRANULE), :],
                                  xs_ref.at[pl.ds(pl.multiple_of(gtab_ref[0, 0, g], GRANULE), GRANULE), :],
                                  sem.at[slot]).start(priority=g % 2)

    @pl.when(i == n_tiles - 1)
    def _():
        wait_tile(slot)
        if n_tiles > 1:
            wait_tile(1 - slot)


def _dispatch(hp, plan, n_rows):
    N, W = hp.shape
    n_tiles = N // MOE_TILE
    n_blocks = n_rows // EXPERT_BLOCK
    return pl.pallas_call(
        functools.partial(_dispatch_kernel, n_blocks=n_blocks, n_tiles=n_tiles),
        grid_spec=pltpu.PrefetchScalarGridSpec(
            num_scalar_prefetch=3, grid=(n_tiles,),
            in_specs=[pl.BlockSpec((1, 1, MOE_TILE), lambda i, *_: (i, 0, 0), memory_space=pltpu.SMEM),
                      pl.BlockSpec((8, MOE_TILE), lambda i, *_: (0, i)),
                      pl.BlockSpec((MOE_TILE, W), lambda i, *_: (i, 0))],
            out_specs=pl.BlockSpec(memory_space=pl.ANY),
            scratch_shapes=[pltpu.VMEM((EXPERT_BLOCK, W), jnp.uint32),
                            pltpu.VMEM((2, MOE_SORTED_ROWS, W), jnp.uint32),
                            pltpu.SemaphoreType.DMA((2,)), pltpu.SemaphoreType.DMA]),
        out_shape=jax.ShapeDtypeStruct((n_rows + MOE_SPARE_ROWS, W), jnp.uint32),
        compiler_params=_cparams(("arbitrary",)),
        name="dispatch",
    )(plan["pad_end"], plan["padded"], plan["n_used"], plan["gtab_out"], plan["pos"], hp)


def _experts_kernel(be_ref, nu_ref, xs_ref, wgf_ref, wuf_ref, wdf_ref, ys_ref, wg_ref, wu_ref, wd_ref):
    i = pl.program_id(0)
    live = i < nu_ref[0]
    new_expert = jnp.logical_or(i == 0, be_ref[i] != be_ref[jnp.maximum(i - 1, 0)])

    @pl.when(jnp.logical_not(live))
    def _():
        ys_ref[...] = jnp.zeros_like(ys_ref)

    @pl.when(jnp.logical_and(live, new_expert))
    def _():
        wg_ref[...] = wgf_ref[...].astype(BF16)
        wu_ref[...] = wuf_ref[...].astype(BF16)
        wd_ref[...] = wdf_ref[...].astype(BF16)

    @pl.when(live)
    def _():
        half = D_MODEL // 2
        lo, hi = _unpack_halves(xs_ref[...])
        lo = lo.astype(BF16)
        hi = hi.astype(BF16)
        gate = (jnp.dot(lo, wg_ref[0:half, :], preferred_element_type=F32)
                + jnp.dot(hi, wg_ref[half:D_MODEL, :], preferred_element_type=F32))
        up = (jnp.dot(lo, wu_ref[0:half, :], preferred_element_type=F32)
              + jnp.dot(hi, wu_ref[half:D_MODEL, :], preferred_element_type=F32))
        y = jnp.dot((_silu(gate) * up).astype(BF16), wd_ref[...], preferred_element_type=F32)
        ys_ref[...] = _pack_halves(y)


def _experts(xs, n_rows, blk_e, n_used, wg, wu, wd):
    W = xs.shape[1]
    n_blocks = n_rows // EXPERT_BLOCK
    row_map = lambda i, be, nu: (jnp.minimum(i, nu[0] - 1), 0)
    return pl.pallas_call(
        _experts_kernel,
        grid_spec=pltpu.PrefetchScalarGridSpec(
            num_scalar_prefetch=2, grid=(n_blocks,),
            in_specs=[pl.BlockSpec((EXPERT_BLOCK, W), row_map),
                      pl.BlockSpec((None, D_MODEL, D_EXPERT), lambda i, be, nu: (be[i], 0, 0)),
                      pl.BlockSpec((None, D_MODEL, D_EXPERT), lambda i, be, nu: (be[i], 0, 0)),
                      pl.BlockSpec((None, D_EXPERT, D_MODEL), lambda i, be, nu: (be[i], 0, 0))],
            out_specs=pl.BlockSpec((EXPERT_BLOCK, W), lambda i, be, nu: (i, 0)),
            scratch_shapes=[pltpu.VMEM((D_MODEL, D_EXPERT), BF16), pltpu.VMEM((D_MODEL, D_EXPERT), BF16),
                            pltpu.VMEM((D_EXPERT, D_MODEL), BF16)]),
        out_shape=jax.ShapeDtypeStruct((n_rows, W), jnp.uint32),
        compiler_params=_cparams(("arbitrary",)),
        name="experts",
    )(blk_e, n_used, xs, wg, wu, wd)


def _combine_kernel(gt_ref, gtab_ref, gtab_next_ref, ys_ref, pos_ref, xs_ref, wts_ref, g2_ref, fg_ref, o_ref,
                    buf, sem, *, n_tiles):
    i = pl.program_id(0)
    slot = i % 2
    nxt = jnp.minimum(i + 1, n_tiles - 1)

    def fetch(which, table_ref, g):
        pltpu.make_async_copy(ys_ref.at[pl.ds(pl.multiple_of(table_ref[0, 0, g], GRANULE), GRANULE), :],
                              buf.at[which, pl.ds(pl.multiple_of(g * GRANULE, GRANULE), GRANULE), :],
                              sem.at[which]).start()

    def fetch_rest(which, table_ref, count):
        def body(g, c):
            fetch(which, table_ref, g)
            return c
        lax.fori_loop(MIN_GRANULES, count, body, 0)

    def wait_tile(which, count):
        def wait_rows(rows):
            pltpu.make_async_copy(ys_ref.at[pl.ds(0, rows), :], buf.at[which, pl.ds(0, rows), :],
                                  sem.at[which]).wait()
        wait_rows(MIN_GRANULES * GRANULE)
        _wait_granules(count - MIN_GRANULES, wait_rows)

    @pl.when(i == 0)
    def _():
        buf[...] = jnp.zeros_like(buf)
        for g in range(MIN_GRANULES):
            fetch(0, gtab_ref, g)
        fetch_rest(0, gtab_ref, gt_ref[0])

    wait_tile(slot, gt_ref[i])

    pos = pos_ref[...]
    w = wts_ref[...]
    c_iota = lax.broadcasted_iota(I32, (MOE_TILE, MOE_ROW_CHUNK), 1).astype(F32).astype(BF16)
    half = D_MODEL // 2
    r_lo = jnp.zeros((MOE_TILE, half), F32)
    r_hi = jnp.zeros((MOE_TILE, half), F32)
    for c in range(MOE_SORTED_ROWS // MOE_ROW_CHUNK):
        rel = pos - c * MOE_ROW_CHUNK
        rel = jnp.where(jnp.logical_and(rel >= 0, rel < MOE_ROW_CHUNK), rel, -1).astype(F32)
        wb = jnp.zeros((MOE_TILE, MOE_ROW_CHUNK), BF16)
        for k in range(TOP_K):
            wb = jnp.where(c_iota == rel[:, k:k + 1].astype(BF16), w[:, k:k + 1].astype(BF16), wb)
        lo, hi = _unpack_halves(buf[slot, c * MOE_ROW_CHUNK:(c + 1) * MOE_ROW_CHUNK, :])
        r_lo = r_lo + jnp.dot(wb, lo.astype(BF16), preferred_element_type=F32)
        r_hi = r_hi + jnp.dot(wb, hi.astype(BF16), preferred_element_type=F32)
        for g in range(min(c * FETCH_PER_CHUNK, MIN_GRANULES), min((c + 1) * FETCH_PER_CHUNK, MIN_GRANULES)):
            fetch(1 - slot, gtab_next_ref, g)
        if c == MIN_GRANULES // FETCH_PER_CHUNK:
            fetch_rest(1 - slot, gtab_next_ref, gt_ref[nxt])
    x3 = xs_ref[...] + g2_ref[...] * jnp.concatenate([r_lo, r_hi], axis=1)
    y = x3 * lax.rsqrt(jnp.mean(x3 * x3, axis=-1, keepdims=True) + RMS_EPS) * fg_ref[...]
    o_ref[...] = y

    @pl.when(i == n_tiles - 1)
    def _():
        wait_tile(1 - slot, gt_ref[nxt])


def _combine(ys, plan, xsr, wts_t, mod4, final_g, T):
    N, D = xsr.shape
    W = ys.shape[1]
    tm = MOE_TILE
    tps = T // tm
    n_tiles = N // tm
    tile_tab = lambda off: pl.BlockSpec((1, 1, MOE_TILE), lambda i, gt: (jnp.minimum(i + off, n_tiles - 1), 0, 0),
                                        memory_space=pltpu.SMEM)
    return pl.pallas_call(
        functools.partial(_combine_kernel, n_tiles=n_tiles),
        grid_spec=pltpu.PrefetchScalarGridSpec(
            num_scalar_prefetch=1, grid=(n_tiles,),
            in_specs=[tile_tab(0), tile_tab(1),
                      pl.BlockSpec(memory_space=pl.ANY),
                      pl.BlockSpec((tm, 8), lambda i, gt: (i, 0)),
                      pl.BlockSpec((tm, D), lambda i, gt: (i, 0)),
                      pl.BlockSpec((tm, 8), lambda i, gt: (i, 0)),
                      pl.BlockSpec((None, None, 1, D), lambda i, gt: (i // tps, 5, 0, 0)),
                      pl.BlockSpec((1, D), lambda i, gt: (0, 0))],
            out_specs=pl.BlockSpec((tm, D), lambda i, gt: (i, 0)),
            scratch_shapes=[pltpu.VMEM((2, MOE_SORTED_ROWS, W), jnp.uint32), pltpu.SemaphoreType.DMA((2,))]),
        out_shape=jax.ShapeDtypeStruct((N, D), F32),
        compiler_params=_cparams(("arbitrary",)),
        name="combine",
    )(plan["g_tile"], plan["gtab_in"], plan["gtab_in"], ys, plan["pos"].T, xsr, wts_t, mod4,
      final_g.reshape(1, D))


def _pack_w_in(w_in):
    D = w_in.shape[0]
    fox_cols = 4 * HALF + 3 * N_HEADS
    wf = w_in[:, :fox_cols]
    wr = w_in[:, fox_cols:]
    o = 3 * HALF
    pad = lambda a, n: jnp.concatenate([a, jnp.zeros((D, n - a.shape[1]), a.dtype)], axis=1)
    parts = [wf[:, :4 * HALF], wr[:, :o],
             pad(wf[:, 4 * HALF:], LANES),
             pad(wr[:, o:o + DECAY_LORA], LANES),
             pad(wr[:, o + DECAY_LORA:o + DECAY_LORA + ICLR_LORA], LANES),
             pad(wr[:, o + DECAY_LORA + ICLR_LORA:], 2 * LANES)]
    return jnp.concatenate(parts, axis=1).astype(BF16)


def _pack_mu(mu):
    o = 3 * HALF
    pad = lambda a, n: jnp.concatenate([a, jnp.zeros((n - a.shape[0],), a.dtype)])
    small = jnp.concatenate([jnp.zeros((LANES,), mu.dtype),
                             pad(mu[o:o + DECAY_LORA], LANES),
                             pad(mu[o + DECAY_LORA:o + DECAY_LORA + ICLR_LORA], LANES),
                             pad(mu[o + DECAY_LORA + ICLR_LORA:], 2 * LANES)])
    return mu[:o].reshape(1, o), small.reshape(1, Z_SMALL)


def kernel(x, c, norm1_g, norm2_g, ada_w, ada_b, w_in, w_out, fox_qn_g, fox_kn_g, fox_on_g, fox_forget_b,
           rw_mu, rw_w0, rw_decay_up, rw_a0, rw_iclr_up, rw_gate_up, rw_k_k, rw_k_a, rw_r_k, rw_lnx_g,
           rw_lnx_b, router_w, router_bias, exp_w_gate, exp_w_up, exp_w_down, sh_w_gate, sh_w_up,
           sh_w_down, final_g):
    B, T, D = x.shape
    N = B * T
    depth = norm1_g.shape[0]
    assert depth == 1, "the combine kernel fuses the final RMSNorm, so exactly one layer is supported"
    xf = x.reshape(N, D)
    for l in range(depth):
        mod4 = _ada(c, ada_w[l], ada_b[l]).reshape(B, 6, 1, D)
        mu_big, mu_small = _pack_mu(rw_mu[l])
        zm, zs = _inproj(xf, mod4, norm1_g[l], _pack_w_in(w_in[l]), mu_big, mu_small, T)
        qp, kp, vp = _foxprep(zm, zs, fox_forget_b[l], fox_qn_g[l], fox_kn_g[l], T)
        y_fox = _attention(qp, kp, vp, zm, fox_on_g[l], B, T)
        rw = dict(rw_w0=rw_w0[l], rw_decay_up=rw_decay_up[l], rw_a0=rw_a0[l], rw_iclr_up=rw_iclr_up[l],
                  rw_gate_up=rw_gate_up[l], rw_k_k=rw_k_k[l], rw_k_a=rw_k_a[l], rw_r_k=rw_r_k[l],
                  rw_lnx_g=rw_lnx_g[l], rw_lnx_b=rw_lnx_b[l])
        y_rwkv = _rwkv(zm, zs, rw, B, T)
        xsr, hp, idx_t, wts_t, rank_t = _outproj(
            xf, y_fox, y_rwkv, mod4, norm2_g[l], w_out[l], router_w[l], router_bias[l],
            sh_w_gate[l], sh_w_up[l], sh_w_down[l], T)
        plan, n_rows = _moe_plan(idx_t, rank_t)
        xs = _dispatch(hp, plan, n_rows)
        ys = _experts(xs, n_rows, plan["blk_e"], plan["n_used"], exp_w_gate[l], exp_w_up[l], exp_w_down[l])
        xf = _combine(ys, plan, xsr, wts_t.T, mod4, final_g, T)
    return xf.reshape(B, T, D)
```

```python
import functools

import jax
import jax.numpy as jnp
import numpy as np
from jax import lax
from jax.experimental import pallas as pl
from jax.experimental.pallas import tpu as pltpu

F32 = jnp.float32
BF16 = jnp.bfloat16
I32 = jnp.int32
HIGHEST = lax.Precision.HIGHEST

D_MODEL = 1024
HEAD_DIM = 64
N_HEADS = 8
HALF = N_HEADS * HEAD_DIM
RMS_EPS = 1e-6
LNX_EPS = 64e-5
LOG2E = 1.4426950408889634
DECAY_LORA = 64
ICLR_LORA = 64
GATE_LORA = 160
N_EXPERTS = 64
N_GROUPS = 8
GROUP_SIZE = N_EXPERTS // N_GROUPS
TOPK_GROUPS = 4
TOP_K = 6
D_EXPERT = 256
D_SHARED = 256
ROUTED_SCALE = 2.5
EXPERT_BLOCK = 1024

LANES = 128
Z_MAIN = 4 * HALF + 3 * HALF
Z_SMALL = 5 * LANES
VMEM_LIMIT = 56 * 1024 * 1024
ATTN_BLOCK = 512
ATTN_HEADS = 4
WAIT_CHUNK = 16
MOE_TILE = 256
GRANULE = 8
MOE_SORTED_ROWS = MOE_TILE * TOP_K + N_EXPERTS * GRANULE
MIN_GRANULES = MOE_TILE * TOP_K // GRANULE
FETCH_PER_CHUNK = 96
MOE_ROW_CHUNK = 256
RWKV_CHUNK = 64
RWKV_SEQS_PER_STEP = 4


def _cparams(semantics):
    return pltpu.CompilerParams(dimension_semantics=semantics, vmem_limit_bytes=VMEM_LIMIT)


def _mm(a, b):
    return jnp.dot(a.astype(BF16), b.astype(BF16), preferred_element_type=F32)


def _mm_nt(a, b):
    return lax.dot_general(a.astype(BF16), b.astype(BF16), (((1,), (1,)), ((), ())),
                           preferred_element_type=F32)


def _mm_tn(a, b):
    return lax.dot_general(a.astype(BF16), b.astype(BF16), (((0,), (0,)), ((), ())),
                           preferred_element_type=F32)


def _mm_f32(a, b):
    return jnp.dot(a, b, precision=HIGHEST, preferred_element_type=F32)


def _bf16_pieces(x, passes):
    pieces = []
    for _ in range(passes):
        piece = x.astype(BF16)
        pieces.append(piece)
        x = x - piece.astype(F32)
    return pieces


def _mm_split(m01, x, passes=3):
    return sum(jnp.dot(m01, p, preferred_element_type=F32) for p in _bf16_pieces(x, passes))


def _mm_split_r(x, m01, passes=2):
    return sum(jnp.dot(p, m01, preferred_element_type=F32) for p in _bf16_pieces(x, passes))


def _sigmoid(x):
    return 1.0 / (1.0 + jnp.exp(-x))


def _softplus(x):
    return jnp.maximum(x, 0.0) + jnp.log(1.0 + jnp.exp(-jnp.abs(x)))


def _silu(x):
    return x * _sigmoid(x)


def _pack_halves(x):
    w = x.shape[1] // 2
    bits = lambda t: lax.bitcast_convert_type(t.astype(jnp.bfloat16).astype(F32), jnp.uint32)
    return (bits(x[:, 0:w]) >> 16) | (bits(x[:, w:2 * w]) & jnp.uint32(0xFFFF0000))


def _pack_bf16_values(lo, hi):
    return (lax.bitcast_convert_type(lo, jnp.uint32) >> 16) | lax.bitcast_convert_type(hi, jnp.uint32)


def _unpack_halves(p):
    lo = lax.bitcast_convert_type(p << 16, F32)
    hi = lax.bitcast_convert_type(p & jnp.uint32(0xFFFF0000), F32)
    return lo, hi


def _shift_rows(z, carry_ref, first):
    rows = z.shape[0]
    prev_row = jnp.where(first, 0.0, carry_ref[0:1, :])
    prev = pltpu.roll(z, 1, 0)
    row0 = lax.broadcasted_iota(I32, (rows, 1), 0) == 0
    prev = jnp.where(row0, prev_row, prev)
    carry_ref[0:1, :] = z[rows - 1:rows, :]
    return prev


def _ada_kernel(c_ref, w_ref, b_ref, o_ref):
    o_ref[...] = _mm_f32(_silu(c_ref[...]), w_ref[...]) + b_ref[...]


def _ada(c, ada_w, ada_b):
    B, D = c.shape
    n_out = ada_w.shape[1]
    tn = 512
    return pl.pallas_call(
        _ada_kernel,
        grid=(n_out // tn,),
        in_specs=[pl.BlockSpec((B, D), lambda j: (0, 0)),
                  pl.BlockSpec((D, tn), lambda j: (0, j)),
                  pl.BlockSpec((1, tn), lambda j: (0, j))],
        out_specs=pl.BlockSpec((B, tn), lambda j: (0, j)),
        out_shape=jax.ShapeDtypeStruct((B, n_out), F32),
        compiler_params=_cparams(("arbitrary",)),
        name="ada",
    )(c, ada_w, ada_b.reshape(1, n_out))


def _inproj_kernel(x_ref, g_ref, sh_ref, sc_ref, w_ref, mub_ref, mus_ref, zm_ref, zs_ref,
                   carry_b, carry_s, *, tiles_per_seq):
    first = (pl.program_id(0) % tiles_per_seq) == 0
    x = x_ref[...]
    h = x * lax.rsqrt(jnp.mean(x * x, axis=-1, keepdims=True) + RMS_EPS) * g_ref[...]
    hb = (h * (1.0 + sc_ref[...]) + sh_ref[...]).astype(BF16)
    nf = 4 * HALF
    zm_ref[:, 0:nf] = jnp.dot(hb, w_ref[:, 0:nf], preferred_element_type=F32).astype(BF16)
    zr = jnp.dot(hb, w_ref[:, nf:Z_MAIN], preferred_element_type=F32)
    zr = zr + mub_ref[...] * (_shift_rows(zr, carry_b, first) - zr)
    zm_ref[:, nf:Z_MAIN] = zr.astype(BF16)
    zs = jnp.dot(hb, w_ref[:, Z_MAIN:Z_MAIN + Z_SMALL], preferred_element_type=F32)
    zs_ref[...] = zs + mus_ref[...] * (_shift_rows(zs, carry_s, first) - zs)


def _inproj(x2d, mod4, norm_g, w_all, mu_big, mu_small, T):
    N, D = x2d.shape
    tm = min(512, T)
    tps = T // tm
    vec = lambda j: pl.BlockSpec((None, None, 1, D), lambda i: (i // tps, j, 0, 0))
    return pl.pallas_call(
        functools.partial(_inproj_kernel, tiles_per_seq=tps),
        grid=(N // tm,),
        in_specs=[pl.BlockSpec((tm, D), lambda i: (i, 0)),
                  pl.BlockSpec((1, D), lambda i: (0, 0)),
                  vec(0), vec(1),
                  pl.BlockSpec((D, Z_MAIN + Z_SMALL), lambda i: (0, 0)),
                  pl.BlockSpec((1, 3 * HALF), lambda i: (0, 0)),
                  pl.BlockSpec((1, Z_SMALL), lambda i: (0, 0))],
        out_specs=[pl.BlockSpec((tm, Z_MAIN), lambda i: (i, 0)),
                   pl.BlockSpec((tm, Z_SMALL), lambda i: (i, 0))],
        out_shape=[jax.ShapeDtypeStruct((N, Z_MAIN), BF16),
                   jax.ShapeDtypeStruct((N, Z_SMALL), F32)],
        scratch_shapes=[pltpu.VMEM((8, 3 * HALF), F32), pltpu.VMEM((8, Z_SMALL), F32)],
        compiler_params=_cparams(("arbitrary",)),
        name="inproj",
    )(x2d, norm_g.reshape(1, D), mod4, mod4, w_all, mu_big, mu_small)


def _foxprep_kernel(q_ref, k_ref, v_ref, zs_ref, fb_ref, qg_ref, kg_ref, tri_ref, eexp_ref,
                    esum_ref, e8_ref, plq_ref, plk_ref, plv_ref, cq_ref, ck_ref, cv_ref,
                    qp_ref, kp_ref, vp_ref, carry_k, carry_v, carry_c, *, tiles_per_seq):
    first = (pl.program_id(0) % tiles_per_seq) == 0
    zs = zs_ref[...]
    logf = -_softplus(-(zs + fb_ref[...]))
    cum = _mm_split(tri_ref[...], logf) + jnp.where(first, 0.0, carry_c[0:1, :])
    carry_c[0:1, :] = cum[cum.shape[0] - 1:, :]
    cum2 = cum * LOG2E
    c_hi = cum2.astype(BF16)
    r1 = cum2 - c_hi.astype(F32)
    c_mid = r1.astype(BF16)
    c_lo = (r1 - c_mid.astype(F32)).astype(BF16)
    a_full = _mm_split_r(_sigmoid(zs), eexp_ref[...])
    k = k_ref[...].astype(F32)
    v = v_ref[...].astype(F32)
    a_k = a_full[:, 0:HALF]
    a_v = a_full[:, HALF:2 * HALF]
    k = a_k * _shift_rows(k, carry_k, first) + (1.0 - a_k) * k
    v = a_v * _shift_rows(v, carry_v, first) + (1.0 - a_v) * v
    q = q_ref[...].astype(F32)

    def head_rms(t, gain):
        ms = _mm_split_r(t * t, esum_ref[...]) * (1.0 / HEAD_DIM)
        inv = _mm_split_r(lax.rsqrt(ms + RMS_EPS), e8_ref[...])
        return t * inv * gain

    qn = head_rms(q, qg_ref[...]) * (HEAD_DIM ** -0.5 * LOG2E)
    kn = head_rms(k, kg_ref[...])
    aug = [c_hi, c_mid, c_lo]
    lhs_q = jnp.concatenate([qn.astype(BF16)] + aug, axis=1)
    lhs_k = jnp.concatenate([kn.astype(BF16)] + aug, axis=1)
    qp_ref[...] = (jnp.dot(lhs_q, plq_ref[...], preferred_element_type=F32) + cq_ref[...]).astype(BF16)
    kp_ref[...] = (jnp.dot(lhs_k, plk_ref[...], preferred_element_type=F32) + ck_ref[...]).astype(BF16)
    vp_ref[...] = (jnp.dot(v.astype(BF16), plv_ref[...], preferred_element_type=F32)
                   + cv_ref[...]).astype(BF16)


def _fox_constants(tm):
    hp = N_HEADS * LANES
    eexp = np.zeros((LANES, 2 * HALF), np.float32)
    esum = np.zeros((HALF, LANES), np.float32)
    e8 = np.zeros((LANES, HALF), np.float32)
    plq = np.zeros((HALF + 3 * LANES, hp), np.float32)
    plk = np.zeros((HALF + 3 * LANES, hp), np.float32)
    plv = np.zeros((HALF, hp), np.float32)
    cq = np.zeros((1, hp), np.float32)
    ck = np.zeros((1, hp), np.float32)
    cv = np.zeros((1, hp), np.float32)
    for h in range(N_HEADS):
        sl = slice(h * HEAD_DIM, (h + 1) * HEAD_DIM)
        eexp[8 + h, sl] = 1.0
        eexp[16 + h, HALF + h * HEAD_DIM:HALF + (h + 1) * HEAD_DIM] = 1.0
        esum[sl, h] = 1.0
        e8[h, sl] = 1.0
        base = h * LANES
        for d in range(HEAD_DIM):
            plq[h * HEAD_DIM + d, base + d] = 1.0
            plk[h * HEAD_DIM + d, base + d] = 1.0
            plv[h * HEAD_DIM + d, base + d] = 1.0
        for j in range(3):
            plq[HALF + j * LANES + h, base + HEAD_DIM + j] = 1.0
            plk[HALF + j * LANES + h, base + HEAD_DIM + 3 + j] = -1.0
            cq[0, base + HEAD_DIM + 3 + j] = 1.0
            ck[0, base + HEAD_DIM + j] = 1.0
        cv[0, base + HEAD_DIM] = 1.0
    tri = np.tril(np.ones((tm, tm), np.float32))
    bf = lambda a: jnp.asarray(a, BF16)
    return dict(tri=bf(tri), eexp=bf(eexp), esum=bf(esum), e8=bf(e8),
                plq=bf(plq), plk=bf(plk), plv=bf(plv), cq=jnp.asarray(cq), ck=jnp.asarray(ck),
                cv=jnp.asarray(cv))


def _foxprep(zm, zs, forget_b, qn_g, kn_g, T):
    N = zm.shape[0]
    tm = min(512, T)
    tps = T // tm
    cst = _fox_constants(tm)
    hp = N_HEADS * LANES
    fb = jnp.zeros((1, LANES), F32).at[0, :N_HEADS].set(forget_b)
    full = lambda a: pl.BlockSpec(a.shape, lambda i: (0, 0))
    consts = [fb, qn_g.reshape(1, HALF), kn_g.reshape(1, HALF), cst["tri"], cst["eexp"], cst["esum"],
              cst["e8"], cst["plq"], cst["plk"], cst["plv"], cst["cq"], cst["ck"], cst["cv"]]
    return pl.pallas_call(
        functools.partial(_foxprep_kernel, tiles_per_seq=tps),
        grid=(N // tm,),
        in_specs=[pl.BlockSpec((tm, HALF), lambda i: (i, 0)),
                  pl.BlockSpec((tm, HALF), lambda i: (i, 1)),
                  pl.BlockSpec((tm, HALF), lambda i: (i, 2)),
                  pl.BlockSpec((tm, LANES), lambda i: (i, 0))] + [full(a) for a in consts],
        out_specs=[pl.BlockSpec((tm, hp), lambda i: (i, 0))] * 3,
        out_shape=[jax.ShapeDtypeStruct((N, hp), BF16)] * 3,
        scratch_shapes=[pltpu.VMEM((8, HALF), F32), pltpu.VMEM((8, HALF), F32),
                        pltpu.VMEM((8, LANES), F32)],
        compiler_params=_cparams(("arbitrary",)),
        name="foxprep",
    )(zm, zm, zm, zs, *consts)


def _attn_kernel(q_ref, k_ref, v_ref, g_ref, ong_ref, o_ref, vt_ref, *, tq):
    i = pl.program_id(2)
    n_kv = vt_ref.shape[1]
    heads = range(ATTN_HEADS)
    lanes = [slice(hh * LANES, (hh + 1) * LANES) for hh in heads]

    @pl.when(i == 0)
    def _():
        for hh in heads:
            for c in range(n_kv):
                vt_ref[hh, c] = v_ref[c * tq:(c + 1) * tq, lanes[hh]].T

    key = lax.broadcasted_iota(I32, (tq, tq), 0)
    qry = lax.broadcasted_iota(I32, (tq, tq), 1)
    causal = key <= qry
    qs = [q_ref[:, lanes[hh]] for hh in heads]

    def step(j, carry, masked):
        m, acc = carry
        start = pl.multiple_of(j * tq, tq)
        s = [lax.dot_general(k_ref[pl.ds(start, tq), lanes[hh]], qs[hh], (((1,), (1,)), ((), ())),
                             preferred_element_type=F32) for hh in heads]
        m_out, acc_out = [], []
        for hh in heads:
            sh = jnp.where(causal, s[hh], -jnp.inf) if masked else s[hh]
            m_new = jnp.maximum(m[hh], jnp.max(sh, axis=0, keepdims=True))
            p = jnp.exp2(sh - m_new)
            acc_out.append(jnp.exp2(m[hh] - m_new) * acc[hh]
                           + jnp.dot(vt_ref[hh, j], p.astype(BF16), preferred_element_type=F32))
            m_out.append(m_new)
        return tuple(m_out), tuple(acc_out)

    init = (tuple(jnp.full((1, tq), -jnp.inf, F32) for _ in heads),
            tuple(jnp.zeros((LANES, tq), F32) for _ in heads))
    carry = lax.fori_loop(0, i, functools.partial(step, masked=False), init)
    _, acc = step(i, carry, True)
    chan = lax.broadcasted_iota(I32, (LANES, tq), 0)
    chan_w = jnp.where(chan < HEAD_DIM, 1.0 / HEAD_DIM, jnp.where(chan == HEAD_DIM, RMS_EPS, 0.0))
    lane = lax.broadcasted_iota(I32, (tq, LANES), 1)
    outs = []
    for hh in heads:
        t = jnp.sum(acc[hh] * acc[hh] * chan_w, axis=0, keepdims=True)
        outs.append((acc[hh] * lax.rsqrt(t)).T)
    o = jnp.concatenate([jnp.where(lane < HEAD_DIM, outs[2 * p], pltpu.roll(outs[2 * p + 1], HEAD_DIM, 1))
                         for p in range(ATTN_HEADS // 2)], axis=1)
    y = o * ong_ref[...] * _sigmoid(g_ref[...].astype(F32))
    o_ref[...] = y.astype(BF16)


def _attention(qp, kp, vp, zm, on_g, B, T):
    N = qp.shape[0]
    tq = min(ATTN_BLOCK, T)
    nq = T // tq
    groups = N_HEADS // ATTN_HEADS
    wp = ATTN_HEADS * LANES
    wo = ATTN_HEADS * HEAD_DIM
    g_col0 = 3 * HALF // wo
    return pl.pallas_call(
        functools.partial(_attn_kernel, tq=tq),
        grid=(B, groups, nq),
        in_specs=[pl.BlockSpec((tq, wp), lambda b, p, i: (b * nq + i, p)),
                  pl.BlockSpec((T, wp), lambda b, p, i: (b, p)),
                  pl.BlockSpec((T, wp), lambda b, p, i: (b, p)),
                  pl.BlockSpec((tq, wo), lambda b, p, i: (b * nq + i, g_col0 + p)),
                  pl.BlockSpec((None, 1, wo), lambda b, p, i: (p, 0, 0))],
        out_specs=pl.BlockSpec((tq, wo), lambda b, p, i: (b * nq + i, p)),
        out_shape=jax.ShapeDtypeStruct((N, HALF), BF16),
        scratch_shapes=[pltpu.VMEM((ATTN_HEADS, nq, LANES, tq), BF16)],
        compiler_params=_cparams(("arbitrary", "arbitrary", "arbitrary")),
        name="attn",
    )(qp, kp, vp, zm, on_g.reshape(groups, 1, wo))


def _rwkv_kernel(r_ref, k_ref, v_ref, zs_ref, w0_ref, dup_ref, a0_ref, iup_ref, gup_ref, kk_ref,
                 ka_ref, rk_ref, lng_ref, lnb_ref, tri_ref, o_ref, state, ybuf, *, chunk, nb):
    C = chunk
    c_idx = pl.program_id(1)

    @pl.when(c_idx == 0)
    def _():
        state[...] = jnp.zeros_like(state)

    R = nb * C
    r = r_ref[...].reshape(R, HALF).astype(F32)
    k = k_ref[...].reshape(R, HALF).astype(F32)
    v = v_ref[...].reshape(R, HALF).astype(F32)
    zs = zs_ref[...].reshape(R, Z_SMALL)
    wd = zs[:, LANES:2 * LANES]
    ad = zs[:, 2 * LANES:3 * LANES]
    gd = zs[:, 3 * LANES:5 * LANES]
    wl = w0_ref[...] + _mm(jnp.tanh(wd), dup_ref[...])
    lw = -jnp.exp(-_softplus(-wl) - 0.5)
    a = _sigmoid(a0_ref[...] + _mm(ad, iup_ref[...]))
    g = _mm(_sigmoid(gd), gup_ref[...])
    kk = k * kk_ref[...]
    k2 = k * (1.0 + (a - 1.0) * ka_ref[...])
    cl = _mm_split(tri_ref[...], lw)
    cl_end = jnp.concatenate(
        [jnp.broadcast_to(cl[(bi + 1) * C - 1:(bi + 1) * C, :], (C, HALF)) for bi in range(nb)], axis=0)
    e_neg = jnp.exp(-cl)
    e_tail = jnp.exp(cl_end - cl)
    pre = dict(rt=r * jnp.exp(cl), at=-kk * jnp.exp(cl - lw), kh=k2 * e_neg, bh=kk * a * e_neg,
               kb=k2 * e_tail, bb=kk * a * e_tail, v=v, kk=kk, pend=jnp.exp(cl_end),
               rkr=r * k2 * rk_ref[...], g=g)

    C2 = 2 * C
    row = lax.broadcasted_iota(I32, (C2, C2), 0)
    col = lax.broadcasted_iota(I32, (C2, C2), 1)
    lower = (col & (C - 1)) <= (row & (C - 1))
    strict = (col & (C - 1)) < (row & (C - 1))
    eye = row == col
    head0 = lax.broadcasted_iota(I32, (C, LANES), 1) < HEAD_DIM
    n_sq = int(np.log2(C)) - 1
    units = [(bi, p) for bi in range(nb) for p in range(N_HEADS // 2)]

    def part(name, bi, p):
        return pre[name][bi * C:(bi + 1) * C, p * LANES:(p + 1) * LANES]

    def stack(x):
        return jnp.concatenate([jnp.where(head0, x, 0.0), jnp.where(head0, 0.0, x)], axis=0)

    def unstack(x):
        return x[0:C] + x[C:C2]

    def head_sum(x):
        s0 = jnp.sum(jnp.where(head0, x, 0.0), axis=1, keepdims=True)
        s1 = jnp.sum(jnp.where(head0, 0.0, x), axis=1, keepdims=True)
        return jnp.where(head0, s0, s1)

    ops = []
    for bi, p in units:
        kk_p = part("kk", bi, p)
        inv = 1.0 / jnp.maximum(jnp.sqrt(head_sum(kk_p * kk_p)), 1e-12)
        ops.append(dict(rt=stack(part("rt", bi, p)), at=stack(part("at", bi, p) * inv),
                        kh=stack(part("kh", bi, p)).astype(BF16), bh=stack(part("bh", bi, p) * inv).astype(BF16),
                        kb=stack(part("kb", bi, p)).astype(BF16), bb=stack(part("bb", bi, p) * inv).astype(BF16),
                        v=stack(part("v", bi, p)).astype(BF16)))
    ra = [jnp.concatenate([o["rt"], o["at"]], axis=0).astype(BF16) for o in ops]
    gg = [_mm_nt(x, jnp.concatenate([o["kh"], o["bh"]], axis=0)) for x, o in zip(ra, ops)]
    g1 = [t[:, 0:C2] for t in gg]
    g2 = [t[:, C2:2 * C2] for t in gg]
    a_rb = [jnp.where(lower, t[0:C2], 0.0).astype(BF16) for t in g2]
    pw = [jnp.where(strict, t[C2:2 * C2], 0.0) for t in g2]
    av = [_mm(jnp.concatenate([jnp.where(lower, t[0:C2], 0.0).astype(BF16),
                               jnp.where(strict, t[C2:2 * C2], 0.0).astype(BF16), o["kb"].T], axis=0),
              o["v"]) for t, o in zip(g1, ops)]
    swap = lambda t: pltpu.roll(t, HEAD_DIM, 1)
    own = (row < C) == (col < HEAD_DIM)
    xs = [o["at"] + swap(t[C2:2 * C2]) for o, t in zip(ops, av)]
    for level in range(n_sq + 1):
        pb = [p.astype(BF16) for p in pw]
        xs = [x + _mm(p, x) for p, x in zip(pb, xs)]
        if level < n_sq:
            pw = [jnp.dot(p, p, preferred_element_type=F32) for p in pb]
    rbx = [_mm(jnp.concatenate([p, o["bb"].T], axis=0), x) for p, o, x in zip(a_rb, ops, xs)]
    for u, (bi, p) in enumerate(units):
        o = ops[u]
        sl = slice(p * LANES, (p + 1) * LANES)
        rb, bx, kv = rbx[u][0:C2], rbx[u][C2:2 * C2], av[u][2 * C2:3 * C2]
        r2 = unstack(o["rt"] + jnp.where(own, rb, 0.0))
        y0 = swap(unstack(jnp.where(own, 0.0, rb))) + unstack(av[u][0:C2])
        m_mat = jnp.where(eye, part("pend", bi, p)[0:1, :], 0.0) + jnp.where(own, bx, 0.0)
        g_mat = swap(jnp.where(own, 0.0, bx)) + kv
        out = _mm(jnp.concatenate([r2, m_mat], axis=0), state[bi, p])
        state[bi, p] = out[C:C + LANES] + g_mat
        y = out[0:C] + y0
        cen = y - head_sum(y) * (1.0 / HEAD_DIM)
        var = head_sum(cen * cen) * (1.0 / HEAD_DIM)
        bonus = head_sum(part("rkr", bi, p)) * part("v", bi, p)
        y = cen * lax.rsqrt(var + LNX_EPS) * lng_ref[:, sl] + lnb_ref[:, sl] + bonus
        ybuf[bi, :, sl] = y * part("g", bi, p)
    o_ref[...] = ybuf[...].astype(BF16)


def _rwkv(zm, zs, p, B, T):
    N = zm.shape[0]
    C = RWKV_CHUNK
    assert 2 * C == LANES and T % C == 0, "a head pair's stacked chunk must fill one 128-row tile"
    nc = T // C
    nb = RWKV_SEQS_PER_STEP if B % RWKV_SEQS_PER_STEP == 0 else 1
    tri = np.kron(np.eye(nb, dtype=np.float32), np.tril(np.ones((C, C), np.float32)))
    pad_rows = lambda w, rows: jnp.zeros((rows, HALF), F32).at[:w.shape[0]].set(w)
    row = lambda a: a.reshape(1, HALF)
    consts = [row(p["rw_w0"]), pad_rows(p["rw_decay_up"], LANES), row(p["rw_a0"]),
              pad_rows(p["rw_iclr_up"], LANES), pad_rows(p["rw_gate_up"], 2 * LANES),
              row(p["rw_k_k"]), row(p["rw_k_a"]), row(p["rw_r_k"]), row(p["rw_lnx_g"]),
              row(p["rw_lnx_b"]), jnp.asarray(tri, BF16)]
    full = lambda a: pl.BlockSpec(a.shape, lambda b, c: (0, 0))
    rcol = 4 * HALF // HALF
    zm3 = zm.reshape(B, T, Z_MAIN)
    out = pl.pallas_call(
        functools.partial(_rwkv_kernel, chunk=C, nb=nb),
        grid=(B // nb, nc),
        in_specs=[pl.BlockSpec((nb, C, HALF), lambda b, c: (b, c, rcol)),
                  pl.BlockSpec((nb, C, HALF), lambda b, c: (b, c, rcol + 1)),
                  pl.BlockSpec((nb, C, HALF), lambda b, c: (b, c, rcol + 2)),
                  pl.BlockSpec((nb, C, Z_SMALL), lambda b, c: (b, c, 0))] + [full(a) for a in consts],
        out_specs=pl.BlockSpec((nb, C, HALF), lambda b, c: (b, c, 0)),
        out_shape=jax.ShapeDtypeStruct((B, T, HALF), BF16),
        scratch_shapes=[pltpu.VMEM((nb, N_HEADS // 2, LANES, LANES), F32), pltpu.VMEM((nb, C, HALF), F32)],
        compiler_params=_cparams(("arbitrary", "arbitrary")),
        name="rwkv",
    )(zm3, zm3, zm3, zs.reshape(B, T, Z_SMALL), *consts)
    return out.reshape(N, HALF)


def _first_index(mask, iota, big):
    return jnp.min(jnp.where(mask, iota, big), axis=0, keepdims=True)


def _outproj_kernel(x_ref, yf_ref, yr_ref, g1_ref, sh2_ref, sc2_ref, g2_ref, n2g_ref, wo_ref, rwt_ref,
                    rb_ref, swg_ref, swu_ref, swd_ref, ustrict_ref,
                    xs_ref, hp_ref, idx_ref, wts_ref, rank_ref, *, tm):
    d = (jnp.dot(yf_ref[...], wo_ref[0:HALF, :], preferred_element_type=F32)
         + jnp.dot(yr_ref[...], wo_ref[HALF:2 * HALF, :], preferred_element_type=F32))
    x2 = x_ref[...] + g1_ref[...] * d
    h = x2 * lax.rsqrt(jnp.mean(x2 * x2, axis=-1, keepdims=True) + RMS_EPS) * n2g_ref[...]
    h = h * (1.0 + sc2_ref[...]) + sh2_ref[...]
    hb = h.astype(BF16)
    act = _silu(jnp.dot(hb, swg_ref[...], preferred_element_type=F32)) * jnp.dot(
        hb, swu_ref[...], preferred_element_type=F32)
    shared = jnp.dot(act.astype(BF16), swd_ref[...], preferred_element_type=F32)
    xs_ref[...] = x2 + g2_ref[...] * shared
    half = D_MODEL // 2
    hp_ref[...] = _pack_halves(h)

    logits = lax.dot_general(rwt_ref[...], h, (((1,), (1,)), ((), ())), precision=HIGHEST,
                             preferred_element_type=F32)
    scores = _sigmoid(logits)
    sel = scores + rb_ref[:, 0:1]
    neg = -jnp.inf
    sel3 = sel.reshape(N_GROUPS, GROUP_SIZE, tm)
    io_in = lax.broadcasted_iota(I32, (N_GROUPS, GROUP_SIZE, tm), 1)
    m1 = jnp.max(sel3, axis=1, keepdims=True)
    f1 = jnp.min(jnp.where(sel3 == m1, io_in, GROUP_SIZE), axis=1, keepdims=True)
    m2 = jnp.max(jnp.where(io_in == f1, neg, sel3), axis=1, keepdims=True)
    gs = (m1 + m2).reshape(N_GROUPS, tm)
    io_g = lax.broadcasted_iota(I32, (N_GROUPS, tm), 0)
    gmask = jnp.zeros((N_GROUPS, tm), jnp.bool_)
    for _ in range(TOPK_GROUPS):
        mg = jnp.max(gs, axis=0, keepdims=True)
        fg = _first_index(gs == mg, io_g, N_GROUPS)
        pick = io_g == fg
        gmask = jnp.logical_or(gmask, pick)
        gs = jnp.where(pick, neg, gs)
    emask = jnp.broadcast_to(gmask.reshape(N_GROUPS, 1, tm), (N_GROUPS, GROUP_SIZE, tm)).reshape(N_EXPERTS, tm)
    cur = jnp.where(emask, sel, neg)
    io_e = lax.broadcasted_iota(I32, (N_EXPERTS, tm), 0)
    picks, idxs, wts = [], [], []
    for _ in range(TOP_K):
        me = jnp.max(cur, axis=0, keepdims=True)
        fe = _first_index(cur == me, io_e, N_EXPERTS)
        pick = io_e == fe
        picks.append(pick)
        idxs.append(fe)
        wts.append(jnp.sum(jnp.where(pick, scores, 0.0), axis=0, keepdims=True))
        cur = jnp.where(pick, neg, cur)
    wsum = wts[0]
    for w in wts[1:]:
        wsum = wsum + w
    zero_i = jnp.zeros((1, tm), I32)
    zero_f = jnp.zeros((1, tm), F32)
    idx_ref[...] = jnp.concatenate(idxs + [zero_i, zero_i], axis=0)
    wts_ref[...] = jnp.concatenate([w / wsum * ROUTED_SCALE for w in wts] + [zero_f, zero_f], axis=0)
    cnt = picks[0].astype(F32)
    for pk in picks[1:]:
        cnt = cnt + pk.astype(F32)
    excl = jnp.dot(cnt.astype(BF16), ustrict_ref[...], preferred_element_type=F32)
    ranks = [jnp.sum(jnp.where(pk, excl, 0.0), axis=0, keepdims=True).astype(I32) for pk in picks]
    rank_ref[...] = jnp.concatenate(ranks + [zero_i, zero_i], axis=0)


def _outproj(x2d, yf, yr, mod4, norm2_g, w_out, router_w, router_bias, swg, swu, swd, T):
    N, D = x2d.shape
    tm = min(512, T)
    tps = T // tm
    vec = lambda j: pl.BlockSpec((None, None, 1, D), lambda i: (i // tps, j, 0, 0))
    full = lambda a: pl.BlockSpec(a.shape, lambda i: (0, 0))
    ts = min(MOE_TILE, tm)
    ustrict = jnp.asarray(np.kron(np.eye(tm // ts, dtype=np.float32),
                                  np.triu(np.ones((ts, ts), np.float32), 1)), BF16)
    rb = jnp.broadcast_to(router_bias.reshape(N_EXPERTS, 1), (N_EXPERTS, LANES))
    consts = [norm2_g.reshape(1, D), w_out.astype(BF16), router_w.T, rb, swg.astype(BF16),
              swu.astype(BF16), swd.astype(BF16), ustrict]
    small = lambda dt: jax.ShapeDtypeStruct((8, N), dt)
    return pl.pallas_call(
        functools.partial(_outproj_kernel, tm=tm),
        grid=(N // tm,),
        in_specs=[pl.BlockSpec((tm, D), lambda i: (i, 0)),
                  pl.BlockSpec((tm, HALF), lambda i: (i, 0)),
                  pl.BlockSpec((tm, HALF), lambda i: (i, 0)),
                  vec(2), vec(3), vec(4), vec(5)] + [full(a) for a in consts],
        out_specs=[pl.BlockSpec((tm, D), lambda i: (i, 0)),
                   pl.BlockSpec((tm, D // 2), lambda i: (i, 0)),
                   pl.BlockSpec((8, tm), lambda i: (0, i)),
                   pl.BlockSpec((8, tm), lambda i: (0, i)),
                   pl.BlockSpec((8, tm), lambda i: (0, i))],
        out_shape=[jax.ShapeDtypeStruct((N, D), F32), jax.ShapeDtypeStruct((N, D // 2), jnp.uint32),
                   small(I32), small(F32), small(I32)],
        compiler_params=_cparams(("arbitrary",)),
        name="outproj",
    )(x2d, yf, yr, mod4, mod4, mod4, mod4, *consts)


def _wait_granules(count, wait_rows):
    full = count // WAIT_CHUNK

    def chunk(_, c):
        wait_rows(WAIT_CHUNK * GRANULE)
        return c

    def single(_, c):
        wait_rows(GRANULE)
        return c

    lax.fori_loop(0, full, chunk, 0)
    lax.fori_loop(full * WAIT_CHUNK, count, single, 0)


def _moe_plan(idx_t, rank_t):
    n = idx_t.shape[1]
    n_tiles = n // MOE_TILE
    experts = jnp.arange(N_EXPERTS, dtype=I32)
    hot = idx_t[:TOP_K, :, None] == experts
    cnt = jnp.sum(hot.reshape(TOP_K, n_tiles, MOE_TILE, N_EXPERTS).astype(I32), axis=(0, 2))
    gran = (cnt + GRANULE - 1) // GRANULE
    loc_end = jnp.cumsum(gran, axis=1)
    loc_off = loc_end - gran
    g_tile = loc_end[:, N_EXPERTS - 1]
    padded = (GRANULE * jnp.sum(gran, axis=0) + EXPERT_BLOCK - 1) // EXPERT_BLOCK * EXPERT_BLOCK
    pad_end = jnp.cumsum(padded)
    glob_off = (pad_end - padded)[None, :] + GRANULE * (jnp.cumsum(gran, axis=0) - gran)
    loc_tok = jnp.repeat(GRANULE * loc_off, MOE_TILE, axis=0)
    pos = rank_t[:TOP_K] + jnp.sum(jnp.where(hot, loc_tok[None], 0), axis=-1)
    pos = jnp.concatenate([pos, jnp.full((8 - TOP_K, n), -1, I32)], axis=0)
    g = jnp.arange(MOE_TILE, dtype=I32)
    e_of_g = jnp.minimum(jnp.sum((loc_end[:, None, :] <= g[None, :, None]).astype(I32), axis=-1), N_EXPERTS - 1)
    dst = jnp.sum(jnp.where(e_of_g[:, :, None] == experts,
                            glob_off[:, None, :] + GRANULE * (g[None, :, None] - loc_off[:, None, :]), 0), axis=-1)
    n_blocks = -(-(n * TOP_K + GRANULE * N_EXPERTS * n_tiles) // EXPERT_BLOCK) + N_EXPERTS
    n_rows = n_blocks * EXPERT_BLOCK
    gtab = jnp.where(g[None, :] < g_tile[:, None], dst, 0).reshape(n_tiles, 1, MOE_TILE)
    blk_start = jnp.arange(n_blocks, dtype=I32) * EXPERT_BLOCK
    blk_e = jnp.minimum(jnp.sum((pad_end[None, :] <= blk_start[:, None]).astype(I32), axis=1), N_EXPERTS - 1)
    plan = dict(pos=pos.astype(I32), gtab=gtab.astype(I32), g_tile=g_tile.astype(I32),
                pad_end=pad_end.astype(I32), padded=padded.astype(I32), blk_e=blk_e.astype(I32),
                n_used=(pad_end[N_EXPERTS - 1:] // EXPERT_BLOCK).astype(I32))
    return plan, n_rows


def _dispatch_kernel(pend_ref, padded_ref, nu_ref, gt_ref, gtab_ref, pos_ref, hp_ref, xs_ref, zeros, sbuf,
                     sem, zsem, *, n_blocks, n_tiles):
    @pl.when(pl.program_id(0) == 0)
    def _():
        zeros[...] = jnp.zeros_like(zeros)

        def block_copy(start):
            return pltpu.make_async_copy(zeros, xs_ref.at[pl.ds(pl.multiple_of(start, EXPERT_BLOCK),
                                                                 EXPERT_BLOCK), :], zsem)

        def pad_loop(fn):
            def body(e, _):
                @pl.when(padded_ref[e] > 0)
                def _():
                    fn(block_copy(pend_ref[e] - EXPERT_BLOCK))
                return 0
            lax.fori_loop(0, N_EXPERTS, body, 0)

        def tail_loop(fn):
            def body(b, _):
                fn(block_copy(b * EXPERT_BLOCK))
                return 0
            lax.fori_loop(nu_ref[0], n_blocks, body, 0)

        pad_loop(lambda cp: cp.start())
        tail_loop(lambda cp: cp.start())
        pad_loop(lambda cp: cp.wait())
        tail_loop(lambda cp: cp.wait())

    i = pl.program_id(0)
    slot = i % 2

    def store(g):
        pltpu.make_async_copy(sbuf.at[slot, pl.ds(pl.multiple_of(g * GRANULE, GRANULE), GRANULE), :],
                              xs_ref.at[pl.ds(pl.multiple_of(gtab_ref[0, 0, g], GRANULE), GRANULE), :],
                              sem.at[slot]).start()

    def wait_tile(which, count):
        def wait_rows(rows):
            pltpu.make_async_copy(sbuf.at[which, pl.ds(0, rows), :], xs_ref.at[pl.ds(0, rows), :],
                                  sem.at[which]).wait()
        wait_rows(MIN_GRANULES * GRANULE)
        _wait_granules(count - MIN_GRANULES, wait_rows)

    @pl.when(i >= 2)
    def _():
        wait_tile(slot, gt_ref[jnp.maximum(i - 2, 0)])

    pos = pos_ref[...]
    lo, hi = _unpack_halves(hp_ref[...])
    lo = lo.astype(BF16)
    hi = hi.astype(BF16)
    r_iota = lax.broadcasted_iota(I32, (MOE_ROW_CHUNK, MOE_TILE), 0).astype(F32).astype(BF16)
    one = jnp.ones((), BF16)
    for c in range(MOE_SORTED_ROWS // MOE_ROW_CHUNK):
        rel = pos - c * MOE_ROW_CHUNK
        rel = jnp.where(jnp.logical_and(rel >= 0, rel < MOE_ROW_CHUNK), rel, -1).astype(F32)
        perm = jnp.zeros((MOE_ROW_CHUNK, MOE_TILE), BF16)
        for k in range(TOP_K):
            perm = jnp.where(r_iota == rel[k:k + 1, :].astype(BF16), one, perm)
        sbuf[slot, c * MOE_ROW_CHUNK:(c + 1) * MOE_ROW_CHUNK, :] = _pack_bf16_values(
            jnp.dot(perm, lo, preferred_element_type=F32), jnp.dot(perm, hi, preferred_element_type=F32))
        g_lo, g_hi = c * MOE_ROW_CHUNK // GRANULE, (c + 1) * MOE_ROW_CHUNK // GRANULE
        for g in range(g_lo, min(g_hi, MIN_GRANULES)):
            store(g)
        if g_hi > MIN_GRANULES:
            def body(g, carry):
                store(g)
                return carry
            lax.fori_loop(max(g_lo, MIN_GRANULES), jnp.clip(gt_ref[i], max(g_lo, MIN_GRANULES), g_hi), body, 0)

    @pl.when(i == n_tiles - 1)
    def _():
        wait_tile(slot, gt_ref[i])
        if n_tiles > 1:
            wait_tile(1 - slot, gt_ref[jnp.maximum(i - 1, 0)])


def _dispatch(hp, plan, n_rows):
    N, W = hp.shape
    n_tiles = N // MOE_TILE
    n_blocks = n_rows // EXPERT_BLOCK
    return pl.pallas_call(
        functools.partial(_dispatch_kernel, n_blocks=n_blocks, n_tiles=n_tiles),
        grid_spec=pltpu.PrefetchScalarGridSpec(
            num_scalar_prefetch=4, grid=(n_tiles,),
            in_specs=[pl.BlockSpec((1, 1, MOE_TILE), lambda i, *_: (i, 0, 0), memory_space=pltpu.SMEM),
                      pl.BlockSpec((8, MOE_TILE), lambda i, *_: (0, i)),
                      pl.BlockSpec((MOE_TILE, W), lambda i, *_: (i, 0))],
            out_specs=pl.BlockSpec(memory_space=pl.ANY),
            scratch_shapes=[pltpu.VMEM((EXPERT_BLOCK, W), jnp.uint32),
                            pltpu.VMEM((2, MOE_SORTED_ROWS, W), jnp.uint32),
                            pltpu.SemaphoreType.DMA((2,)), pltpu.SemaphoreType.DMA]),
        out_shape=jax.ShapeDtypeStruct((n_rows, W), jnp.uint32),
        compiler_params=_cparams(("arbitrary",)),
        name="dispatch",
    )(plan["pad_end"], plan["padded"], plan["n_used"], plan["g_tile"], plan["gtab"], plan["pos"], hp)


def _experts_kernel(be_ref, nu_ref, xs_ref, wgf_ref, wuf_ref, wdf_ref, ys_ref, wg_ref, wu_ref, wd_ref):
    i = pl.program_id(0)
    live = i < nu_ref[0]
    new_expert = jnp.logical_or(i == 0, be_ref[i] != be_ref[jnp.maximum(i - 1, 0)])

    @pl.when(jnp.logical_not(live))
    def _():
        ys_ref[...] = jnp.zeros_like(ys_ref)

    @pl.when(jnp.logical_and(live, new_expert))
    def _():
        wg_ref[...] = wgf_ref[...].astype(BF16)
        wu_ref[...] = wuf_ref[...].astype(BF16)
        wd_ref[...] = wdf_ref[...].astype(BF16)

    @pl.when(live)
    def _():
        half = D_MODEL // 2
        lo, hi = _unpack_halves(xs_ref[...])
        lo = lo.astype(BF16)
        hi = hi.astype(BF16)
        gate = (jnp.dot(lo, wg_ref[0:half, :], preferred_element_type=F32)
                + jnp.dot(hi, wg_ref[half:D_MODEL, :], preferred_element_type=F32))
        up = (jnp.dot(lo, wu_ref[0:half, :], preferred_element_type=F32)
              + jnp.dot(hi, wu_ref[half:D_MODEL, :], preferred_element_type=F32))
        y = jnp.dot((_silu(gate) * up).astype(BF16), wd_ref[...], preferred_element_type=F32)
        ys_ref[...] = _pack_halves(y)


def _experts(xs, n_rows, blk_e, n_used, wg, wu, wd):
    W = xs.shape[1]
    n_blocks = n_rows // EXPERT_BLOCK
    row_map = lambda i, be, nu: (jnp.minimum(i, nu[0] - 1), 0)
    return pl.pallas_call(
        _experts_kernel,
        grid_spec=pltpu.PrefetchScalarGridSpec(
            num_scalar_prefetch=2, grid=(n_blocks,),
            in_specs=[pl.BlockSpec((EXPERT_BLOCK, W), row_map),
                      pl.BlockSpec((None, D_MODEL, D_EXPERT), lambda i, be, nu: (be[i], 0, 0)),
                      pl.BlockSpec((None, D_MODEL, D_EXPERT), lambda i, be, nu: (be[i], 0, 0)),
                      pl.BlockSpec((None, D_EXPERT, D_MODEL), lambda i, be, nu: (be[i], 0, 0))],
            out_specs=pl.BlockSpec((EXPERT_BLOCK, W), lambda i, be, nu: (i, 0)),
            scratch_shapes=[pltpu.VMEM((D_MODEL, D_EXPERT), BF16), pltpu.VMEM((D_MODEL, D_EXPERT), BF16),
                            pltpu.VMEM((D_EXPERT, D_MODEL), BF16)]),
        out_shape=jax.ShapeDtypeStruct((n_rows, W), jnp.uint32),
        compiler_params=_cparams(("arbitrary",)),
        name="experts",
    )(blk_e, n_used, xs, wg, wu, wd)


def _combine_kernel(gt_ref, gtab_ref, gtab_next_ref, ys_ref, pos_ref, xs_ref, wts_ref, g2_ref, fg_ref, o_ref,
                    buf, sem, *, n_tiles):
    i = pl.program_id(0)
    slot = i % 2
    nxt = jnp.minimum(i + 1, n_tiles - 1)

    def fetch(which, table_ref, g):
        pltpu.make_async_copy(ys_ref.at[pl.ds(pl.multiple_of(table_ref[0, 0, g], GRANULE), GRANULE), :],
                              buf.at[which, pl.ds(pl.multiple_of(g * GRANULE, GRANULE), GRANULE), :],
                              sem.at[which]).start()

    def fetch_rest(which, table_ref, count):
        def body(g, c):
            fetch(which, table_ref, g)
            return c
        lax.fori_loop(MIN_GRANULES, count, body, 0)

    def wait_tile(which, count):
        def wait_rows(rows):
            pltpu.make_async_copy(ys_ref.at[pl.ds(0, rows), :], buf.at[which, pl.ds(0, rows), :],
                                  sem.at[which]).wait()
        wait_rows(MIN_GRANULES * GRANULE)
        _wait_granules(count - MIN_GRANULES, wait_rows)

    @pl.when(i == 0)
    def _():
        buf[...] = jnp.zeros_like(buf)
        for g in range(MIN_GRANULES):
            fetch(0, gtab_ref, g)
        fetch_rest(0, gtab_ref, gt_ref[0])

    wait_tile(slot, gt_ref[i])

    pos = pos_ref[...]
    w = wts_ref[...]
    c_iota = lax.broadcasted_iota(I32, (MOE_TILE, MOE_ROW_CHUNK), 1).astype(F32).astype(BF16)
    half = D_MODEL // 2
    r_lo = jnp.zeros((MOE_TILE, half), F32)
    r_hi = jnp.zeros((MOE_TILE, half), F32)
    for c in range(MOE_SORTED_ROWS // MOE_ROW_CHUNK):
        rel = pos - c * MOE_ROW_CHUNK
        rel = jnp.where(jnp.logical_and(rel >= 0, rel < MOE_ROW_CHUNK), rel, -1).astype(F32)
        wb = jnp.zeros((MOE_TILE, MOE_ROW_CHUNK), BF16)
        for k in range(TOP_K):
            wb = jnp.where(c_iota == rel[:, k:k + 1].astype(BF16), w[:, k:k + 1].astype(BF16), wb)
        lo, hi = _unpack_halves(buf[slot, c * MOE_ROW_CHUNK:(c + 1) * MOE_ROW_CHUNK, :])
        r_lo = r_lo + jnp.dot(wb, lo.astype(BF16), preferred_element_type=F32)
        r_hi = r_hi + jnp.dot(wb, hi.astype(BF16), preferred_element_type=F32)
        for g in range(min(c * FETCH_PER_CHUNK, MIN_GRANULES), min((c + 1) * FETCH_PER_CHUNK, MIN_GRANULES)):
            fetch(1 - slot, gtab_next_ref, g)
        if c == MIN_GRANULES // FETCH_PER_CHUNK:
            fetch_rest(1 - slot, gtab_next_ref, gt_ref[nxt])
    x3 = xs_ref[...] + g2_ref[...] * jnp.concatenate([r_lo, r_hi], axis=1)
    y = x3 * lax.rsqrt(jnp.mean(x3 * x3, axis=-1, keepdims=True) + RMS_EPS) * fg_ref[...]
    o_ref[...] = y

    @pl.when(i == n_tiles - 1)
    def _():
        wait_tile(1 - slot, gt_ref[nxt])


def _combine(ys, plan, xsr, wts_t, mod4, final_g, T):
    N, D = xsr.shape
    W = ys.shape[1]
    tm = MOE_TILE
    tps = T // tm
    n_tiles = N // tm
    tile_tab = lambda off: pl.BlockSpec((1, 1, MOE_TILE), lambda i, gt: (jnp.minimum(i + off, n_tiles - 1), 0, 0),
                                        memory_space=pltpu.SMEM)
    return pl.pallas_call(
        functools.partial(_combine_kernel, n_tiles=n_tiles),
        grid_spec=pltpu.PrefetchScalarGridSpec(
            num_scalar_prefetch=1, grid=(n_tiles,),
            in_specs=[tile_tab(0), tile_tab(1),
                      pl.BlockSpec(memory_space=pl.ANY),
                      pl.BlockSpec((tm, 8), lambda i, gt: (i, 0)),
                      pl.BlockSpec((tm, D), lambda i, gt: (i, 0)),
                      pl.BlockSpec((tm, 8), lambda i, gt: (i, 0)),
                      pl.BlockSpec((None, None, 1, D), lambda i, gt: (i // tps, 5, 0, 0)),
                      pl.BlockSpec((1, D), lambda i, gt: (0, 0))],
            out_specs=pl.BlockSpec((tm, D), lambda i, gt: (i, 0)),
            scratch_shapes=[pltpu.VMEM((2, MOE_SORTED_ROWS, W), jnp.uint32), pltpu.SemaphoreType.DMA((2,))]),
        out_shape=jax.ShapeDtypeStruct((N, D), F32),
        compiler_params=_cparams(("arbitrary",)),
        name="combine",
    )(plan["g_tile"], plan["gtab"], plan["gtab"], ys, plan["pos"].T, xsr, wts_t, mod4,
      final_g.reshape(1, D))


def _pack_w_in(w_in):
    D = w_in.shape[0]
    fox_cols = 4 * HALF + 3 * N_HEADS
    wf = w_in[:, :fox_cols]
    wr = w_in[:, fox_cols:]
    o = 3 * HALF
    pad = lambda a, n: jnp.concatenate([a, jnp.zeros((D, n - a.shape[1]), a.dtype)], axis=1)
    parts = [wf[:, :4 * HALF], wr[:, :o],
             pad(wf[:, 4 * HALF:], LANES),
             pad(wr[:, o:o + DECAY_LORA], LANES),
             pad(wr[:, o + DECAY_LORA:o + DECAY_LORA + ICLR_LORA], LANES),
             pad(wr[:, o + DECAY_LORA + ICLR_LORA:], 2 * LANES)]
    return jnp.concatenate(parts, axis=1).astype(BF16)


def _pack_mu(mu):
    o = 3 * HALF
    pad = lambda a, n: jnp.concatenate([a, jnp.zeros((n - a.shape[0],), a.dtype)])
    small = jnp.concatenate([jnp.zeros((LANES,), mu.dtype),
                             pad(mu[o:o + DECAY_LORA], LANES),
                             pad(mu[o + DECAY_LORA:o + DECAY_LORA + ICLR_LORA], LANES),
                             pad(mu[o + DECAY_LORA + ICLR_LORA:], 2 * LANES)])
    return mu[:o].reshape(1, o), small.reshape(1, Z_SMALL)


def kernel(x, c, norm1_g, norm2_g, ada_w, ada_b, w_in, w_out, fox_qn_g, fox_kn_g, fox_on_g, fox_forget_b,
           rw_mu, rw_w0, rw_decay_up, rw_a0, rw_iclr_up, rw_gate_up, rw_k_k, rw_k_a, rw_r_k, rw_lnx_g,
           rw_lnx_b, router_w, router_bias, exp_w_gate, exp_w_up, exp_w_down, sh_w_gate, sh_w_up,
           sh_w_down, final_g):
    B, T, D = x.shape
    N = B * T
    depth = norm1_g.shape[0]
    assert depth == 1, "the combine kernel fuses the final RMSNorm, so exactly one layer is supported"
    xf = x.reshape(N, D)
    for l in range(depth):
        mod4 = _ada(c, ada_w[l], ada_b[l]).reshape(B, 6, 1, D)
        mu_big, mu_small = _pack_mu(rw_mu[l])
        zm, zs = _inproj(xf, mod4, norm1_g[l], _pack_w_in(w_in[l]), mu_big, mu_small, T)
        qp, kp, vp = _foxprep(zm, zs, fox_forget_b[l], fox_qn_g[l], fox_kn_g[l], T)
        y_fox = _attention(qp, kp, vp, zm, fox_on_g[l], B, T)
        rw = dict(rw_w0=rw_w0[l], rw_decay_up=rw_decay_up[l], rw_a0=rw_a0[l], rw_iclr_up=rw_iclr_up[l],
                  rw_gate_up=rw_gate_up[l], rw_k_k=rw_k_k[l], rw_k_a=rw_k_a[l], rw_r_k=rw_r_k[l],
                  rw_lnx_g=rw_lnx_g[l], rw_lnx_b=rw_lnx_b[l])
        y_rwkv = _rwkv(zm, zs, rw, B, T)
        xsr, hp, idx_t, wts_t, rank_t = _outproj(
            xf, y_fox, y_rwkv, mod4, norm2_g[l], w_out[l], router_w[l], router_bias[l],
            sh_w_gate[l], sh_w_up[l], sh_w_down[l], T)
        plan, n_rows = _moe_plan(idx_t, rank_t)
        xs = _dispatch(hp, plan, n_rows)
        ys = _experts(xs, n_rows, plan["blk_e"], plan["n_used"], exp_w_gate[l], exp_w_up[l], exp_w_down[l])
        xf = _combine(ys, plan, xsr, wts_t.T, mod4, final_g, T)
    return xf.reshape(B, T, D)
```

```python
import functools

import jax
import jax.numpy as jnp
import numpy as np
from jax import lax
from jax.experimental import pallas as pl
from jax.experimental.pallas import tpu as pltpu

F32 = jnp.float32
BF16 = jnp.bfloat16
I32 = jnp.int32
HIGHEST = lax.Precision.HIGHEST

D_MODEL = 1024
HEAD_DIM = 64
N_HEADS = 8
HALF = N_HEADS * HEAD_DIM
RMS_EPS = 1e-6
LNX_EPS = 64e-5
LOG2E = 1.4426950408889634
DECAY_LORA = 64
ICLR_LORA = 64
GATE_LORA = 160
N_EXPERTS = 64
N_GROUPS = 8
GROUP_SIZE = N_EXPERTS // N_GROUPS
TOPK_GROUPS = 4
TOP_K = 6
D_EXPERT = 256
D_SHARED = 256
ROUTED_SCALE = 2.5
EXPERT_BLOCK = 1024

LANES = 128
Z_MAIN = 4 * HALF + 3 * HALF
Z_SMALL = 5 * LANES
VMEM_LIMIT = 56 * 1024 * 1024
ATTN_BLOCK = 512
ATTN_HEADS = 4
WAIT_CHUNK = 16
MOE_TILE = 256
GRANULE = 8
MOE_SORTED_ROWS = MOE_TILE * TOP_K + N_EXPERTS * GRANULE
MIN_GRANULES = MOE_TILE * TOP_K // GRANULE
FETCH_PER_CHUNK = 96
MOE_ROW_CHUNK = 256
RWKV_CHUNK = 64
RWKV_SEQS_PER_STEP = 4


def _cparams(semantics):
    return pltpu.CompilerParams(dimension_semantics=semantics, vmem_limit_bytes=VMEM_LIMIT)


def _mm(a, b):
    return jnp.dot(a.astype(BF16), b.astype(BF16), preferred_element_type=F32)


def _mm_nt(a, b):
    return lax.dot_general(a.astype(BF16), b.astype(BF16), (((1,), (1,)), ((), ())),
                           preferred_element_type=F32)


def _mm_tn(a, b):
    return lax.dot_general(a.astype(BF16), b.astype(BF16), (((0,), (0,)), ((), ())),
                           preferred_element_type=F32)


def _mm_f32(a, b):
    return jnp.dot(a, b, precision=HIGHEST, preferred_element_type=F32)


def _bf16_pieces(x, passes):
    pieces = []
    for _ in range(passes):
        piece = x.astype(BF16)
        pieces.append(piece)
        x = x - piece.astype(F32)
    return pieces


def _mm_split(m01, x, passes=3):
    return sum(jnp.dot(m01, p, preferred_element_type=F32) for p in _bf16_pieces(x, passes))


def _mm_split_r(x, m01, passes=2):
    return sum(jnp.dot(p, m01, preferred_element_type=F32) for p in _bf16_pieces(x, passes))


def _sigmoid(x):
    return 1.0 / (1.0 + jnp.exp(-x))


def _softplus(x):
    return jnp.maximum(x, 0.0) + jnp.log(1.0 + jnp.exp(-jnp.abs(x)))


def _silu(x):
    return x * _sigmoid(x)


def _pack_halves(x):
    w = x.shape[1] // 2
    bits = lambda t: lax.bitcast_convert_type(t.astype(jnp.bfloat16).astype(F32), jnp.uint32)
    return (bits(x[:, 0:w]) >> 16) | (bits(x[:, w:2 * w]) & jnp.uint32(0xFFFF0000))


def _pack_bf16_values(lo, hi):
    return (lax.bitcast_convert_type(lo, jnp.uint32) >> 16) | lax.bitcast_convert_type(hi, jnp.uint32)


def _unpack_halves(p):
    lo = lax.bitcast_convert_type(p << 16, F32)
    hi = lax.bitcast_convert_type(p & jnp.uint32(0xFFFF0000), F32)
    return lo, hi


def _shift_rows(z, carry_ref, first):
    rows = z.shape[0]
    prev_row = jnp.where(first, 0.0, carry_ref[0:1, :])
    prev = pltpu.roll(z, 1, 0)
    row0 = lax.broadcasted_iota(I32, (rows, 1), 0) == 0
    prev = jnp.where(row0, prev_row, prev)
    carry_ref[0:1, :] = z[rows - 1:rows, :]
    return prev


def _ada_kernel(c_ref, w_ref, b_ref, o_ref):
    o_ref[...] = _mm_f32(_silu(c_ref[...]), w_ref[...]) + b_ref[...]


def _ada(c, ada_w, ada_b):
    B, D = c.shape
    n_out = ada_w.shape[1]
    tn = 512
    return pl.pallas_call(
        _ada_kernel,
        grid=(n_out // tn,),
        in_specs=[pl.BlockSpec((B, D), lambda j: (0, 0)),
                  pl.BlockSpec((D, tn), lambda j: (0, j)),
                  pl.BlockSpec((1, tn), lambda j: (0, j))],
        out_specs=pl.BlockSpec((B, tn), lambda j: (0, j)),
        out_shape=jax.ShapeDtypeStruct((B, n_out), F32),
        compiler_params=_cparams(("arbitrary",)),
        name="ada",
    )(c, ada_w, ada_b.reshape(1, n_out))


def _inproj_kernel(x_ref, g_ref, sh_ref, sc_ref, w_ref, mub_ref, mus_ref, zm_ref, zs_ref,
                   carry_b, carry_s, *, tiles_per_seq):
    first = (pl.program_id(0) % tiles_per_seq) == 0
    x = x_ref[...]
    h = x * lax.rsqrt(jnp.mean(x * x, axis=-1, keepdims=True) + RMS_EPS) * g_ref[...]
    hb = (h * (1.0 + sc_ref[...]) + sh_ref[...]).astype(BF16)
    nf = 4 * HALF
    zm_ref[:, 0:nf] = jnp.dot(hb, w_ref[:, 0:nf], preferred_element_type=F32).astype(BF16)
    zr = jnp.dot(hb, w_ref[:, nf:Z_MAIN], preferred_element_type=F32)
    zr = zr + mub_ref[...] * (_shift_rows(zr, carry_b, first) - zr)
    zm_ref[:, nf:Z_MAIN] = zr.astype(BF16)
    zs = jnp.dot(hb, w_ref[:, Z_MAIN:Z_MAIN + Z_SMALL], preferred_element_type=F32)
    zs_ref[...] = zs + mus_ref[...] * (_shift_rows(zs, carry_s, first) - zs)


def _inproj(x2d, mod4, norm_g, w_all, mu_big, mu_small, T):
    N, D = x2d.shape
    tm = min(512, T)
    tps = T // tm
    vec = lambda j: pl.BlockSpec((None, None, 1, D), lambda i: (i // tps, j, 0, 0))
    return pl.pallas_call(
        functools.partial(_inproj_kernel, tiles_per_seq=tps),
        grid=(N // tm,),
        in_specs=[pl.BlockSpec((tm, D), lambda i: (i, 0)),
                  pl.BlockSpec((1, D), lambda i: (0, 0)),
                  vec(0), vec(1),
                  pl.BlockSpec((D, Z_MAIN + Z_SMALL), lambda i: (0, 0)),
                  pl.BlockSpec((1, 3 * HALF), lambda i: (0, 0)),
                  pl.BlockSpec((1, Z_SMALL), lambda i: (0, 0))],
        out_specs=[pl.BlockSpec((tm, Z_MAIN), lambda i: (i, 0)),
                   pl.BlockSpec((tm, Z_SMALL), lambda i: (i, 0))],
        out_shape=[jax.ShapeDtypeStruct((N, Z_MAIN), BF16),
                   jax.ShapeDtypeStruct((N, Z_SMALL), F32)],
        scratch_shapes=[pltpu.VMEM((8, 3 * HALF), F32), pltpu.VMEM((8, Z_SMALL), F32)],
        compiler_params=_cparams(("arbitrary",)),
        name="inproj",
    )(x2d, norm_g.reshape(1, D), mod4, mod4, w_all, mu_big, mu_small)


def _foxprep_kernel(q_ref, k_ref, v_ref, zs_ref, fb_ref, qg_ref, kg_ref, tri_ref, eexp_ref,
                    esum_ref, e8_ref, plq_ref, plk_ref, plv_ref, cq_ref, ck_ref, cv_ref,
                    qp_ref, kp_ref, vp_ref, carry_k, carry_v, carry_c, *, tiles_per_seq):
    first = (pl.program_id(0) % tiles_per_seq) == 0
    zs = zs_ref[...]
    logf = -_softplus(-(zs + fb_ref[...]))
    cum = _mm_split(tri_ref[...], logf) + jnp.where(first, 0.0, carry_c[0:1, :])
    carry_c[0:1, :] = cum[cum.shape[0] - 1:, :]
    cum2 = cum * LOG2E
    head_lane = lax.broadcasted_iota(I32, cum2.shape, 1) < N_HEADS
    c_hi, c_mid, c_lo = [jnp.where(head_lane, p.astype(F32), 0.0) for p in _bf16_pieces(cum2, 3)]
    aug = (c_hi + pltpu.roll(c_mid, N_HEADS, 1) + pltpu.roll(c_lo, 2 * N_HEADS, 1)).astype(BF16)
    a_full = _mm_split_r(_sigmoid(zs), eexp_ref[...])
    k = k_ref[...].astype(F32)
    v = v_ref[...].astype(F32)
    a_k = a_full[:, 0:HALF]
    a_v = a_full[:, HALF:2 * HALF]
    k = a_k * _shift_rows(k, carry_k, first) + (1.0 - a_k) * k
    v = a_v * _shift_rows(v, carry_v, first) + (1.0 - a_v) * v
    q = q_ref[...].astype(F32)

    def head_rms(t, gain):
        ms = _mm_split_r(t * t, esum_ref[...]) * (1.0 / HEAD_DIM)
        inv = _mm_split_r(lax.rsqrt(ms + RMS_EPS), e8_ref[...])
        return t * inv * gain

    qn = head_rms(q, qg_ref[...]) * (HEAD_DIM ** -0.5 * LOG2E)
    kn = head_rms(k, kg_ref[...])
    lhs_q = jnp.concatenate([qn.astype(BF16), aug], axis=1)
    lhs_k = jnp.concatenate([kn.astype(BF16), aug], axis=1)
    qp_ref[...] = (jnp.dot(lhs_q, plq_ref[...], preferred_element_type=F32) + cq_ref[...]).astype(BF16)
    kp_ref[...] = (jnp.dot(lhs_k, plk_ref[...], preferred_element_type=F32) + ck_ref[...]).astype(BF16)
    vp_ref[...] = (jnp.dot(v.astype(BF16), plv_ref[...], preferred_element_type=F32)
                   + cv_ref[...]).astype(BF16)


def _fox_constants(tm):
    hp = N_HEADS * LANES
    eexp = np.zeros((LANES, 2 * HALF), np.float32)
    esum = np.zeros((HALF, LANES), np.float32)
    e8 = np.zeros((LANES, HALF), np.float32)
    plq = np.zeros((HALF + LANES, hp), np.float32)
    plk = np.zeros((HALF + LANES, hp), np.float32)
    plv = np.zeros((HALF, hp), np.float32)
    cq = np.zeros((1, hp), np.float32)
    ck = np.zeros((1, hp), np.float32)
    cv = np.zeros((1, hp), np.float32)
    for h in range(N_HEADS):
        sl = slice(h * HEAD_DIM, (h + 1) * HEAD_DIM)
        eexp[8 + h, sl] = 1.0
        eexp[16 + h, HALF + h * HEAD_DIM:HALF + (h + 1) * HEAD_DIM] = 1.0
        esum[sl, h] = 1.0
        e8[h, sl] = 1.0
        base = h * LANES
        for d in range(HEAD_DIM):
            plq[h * HEAD_DIM + d, base + d] = 1.0
            plk[h * HEAD_DIM + d, base + d] = 1.0
            plv[h * HEAD_DIM + d, base + d] = 1.0
        for j in range(3):
            plq[HALF + j * N_HEADS + h, base + HEAD_DIM + j] = 1.0
            plk[HALF + j * N_HEADS + h, base + HEAD_DIM + 3 + j] = -1.0
            cq[0, base + HEAD_DIM + 3 + j] = 1.0
            ck[0, base + HEAD_DIM + j] = 1.0
        cv[0, base + HEAD_DIM] = 1.0
    tri = np.tril(np.ones((tm, tm), np.float32))
    bf = lambda a: jnp.asarray(a, BF16)
    return dict(tri=bf(tri), eexp=bf(eexp), esum=bf(esum), e8=bf(e8),
                plq=bf(plq), plk=bf(plk), plv=bf(plv), cq=jnp.asarray(cq), ck=jnp.asarray(ck),
                cv=jnp.asarray(cv))


def _foxprep(zm, zs, forget_b, qn_g, kn_g, T):
    N = zm.shape[0]
    tm = min(512, T)
    tps = T // tm
    cst = _fox_constants(tm)
    hp = N_HEADS * LANES
    fb = jnp.zeros((1, LANES), F32).at[0, :N_HEADS].set(forget_b)
    full = lambda a: pl.BlockSpec(a.shape, lambda i: (0, 0))
    consts = [fb, qn_g.reshape(1, HALF), kn_g.reshape(1, HALF), cst["tri"], cst["eexp"], cst["esum"],
              cst["e8"], cst["plq"], cst["plk"], cst["plv"], cst["cq"], cst["ck"], cst["cv"]]
    return pl.pallas_call(
        functools.partial(_foxprep_kernel, tiles_per_seq=tps),
        grid=(N // tm,),
        in_specs=[pl.BlockSpec((tm, HALF), lambda i: (i, 0)),
                  pl.BlockSpec((tm, HALF), lambda i: (i, 1)),
                  pl.BlockSpec((tm, HALF), lambda i: (i, 2)),
                  pl.BlockSpec((tm, LANES), lambda i: (i, 0))] + [full(a) for a in consts],
        out_specs=[pl.BlockSpec((tm, hp), lambda i: (i, 0))] * 3,
        out_shape=[jax.ShapeDtypeStruct((N, hp), BF16)] * 3,
        scratch_shapes=[pltpu.VMEM((8, HALF), F32), pltpu.VMEM((8, HALF), F32),
                        pltpu.VMEM((8, LANES), F32)],
        compiler_params=_cparams(("arbitrary",)),
        name="foxprep",
    )(zm, zm, zm, zs, *consts)


def _attn_kernel(q_ref, k_ref, v_ref, g_ref, ong_ref, o_ref, vt_ref, *, tq):
    i = pl.program_id(2)
    n_kv = vt_ref.shape[1]
    heads = range(ATTN_HEADS)
    lanes = [slice(hh * LANES, (hh + 1) * LANES) for hh in heads]

    @pl.when(i == 0)
    def _():
        for hh in heads:
            for c in range(n_kv):
                vt_ref[hh, c] = v_ref[c * tq:(c + 1) * tq, lanes[hh]].T

    key = lax.broadcasted_iota(I32, (tq, tq), 0)
    qry = lax.broadcasted_iota(I32, (tq, tq), 1)
    causal = key <= qry
    qs = [q_ref[:, lanes[hh]] for hh in heads]

    def step(j, carry, masked):
        m, acc = carry
        start = pl.multiple_of(j * tq, tq)
        s = [lax.dot_general(k_ref[pl.ds(start, tq), lanes[hh]], qs[hh], (((1,), (1,)), ((), ())),
                             preferred_element_type=F32) for hh in heads]
        m_out, acc_out = [], []
        for hh in heads:
            sh = jnp.where(causal, s[hh], -jnp.inf) if masked else s[hh]
            m_new = jnp.maximum(m[hh], jnp.max(sh, axis=0, keepdims=True))
            p = jnp.exp2(sh - m_new)
            acc_out.append(jnp.exp2(m[hh] - m_new) * acc[hh]
                           + jnp.dot(vt_ref[hh, j], p.astype(BF16), preferred_element_type=F32))
            m_out.append(m_new)
        return tuple(m_out), tuple(acc_out)

    init = (tuple(jnp.full((1, tq), -jnp.inf, F32) for _ in heads),
            tuple(jnp.zeros((LANES, tq), F32) for _ in heads))
    carry = lax.fori_loop(0, i, functools.partial(step, masked=False), init)
    _, acc = step(i, carry, True)
    chan = lax.broadcasted_iota(I32, (LANES, tq), 0)
    chan_w = jnp.where(chan < HEAD_DIM, 1.0 / HEAD_DIM, jnp.where(chan == HEAD_DIM, RMS_EPS, 0.0))
    lane = lax.broadcasted_iota(I32, (tq, LANES), 1)
    outs = []
    for hh in heads:
        t = jnp.sum(acc[hh] * acc[hh] * chan_w, axis=0, keepdims=True)
        outs.append((acc[hh] * lax.rsqrt(t)).T)
    o = jnp.concatenate([jnp.where(lane < HEAD_DIM, outs[2 * p], pltpu.roll(outs[2 * p + 1], HEAD_DIM, 1))
                         for p in range(ATTN_HEADS // 2)], axis=1)
    y = o * ong_ref[...] * _sigmoid(g_ref[...].astype(F32))
    o_ref[...] = y.astype(BF16)


def _attention(qp, kp, vp, zm, on_g, B, T):
    N = qp.shape[0]
    tq = min(ATTN_BLOCK, T)
    nq = T // tq
    groups = N_HEADS // ATTN_HEADS
    wp = ATTN_HEADS * LANES
    wo = ATTN_HEADS * HEAD_DIM
    g_col0 = 3 * HALF // wo
    return pl.pallas_call(
        functools.partial(_attn_kernel, tq=tq),
        grid=(B, groups, nq),
        in_specs=[pl.BlockSpec((tq, wp), lambda b, p, i: (b * nq + i, p)),
                  pl.BlockSpec((T, wp), lambda b, p, i: (b, p)),
                  pl.BlockSpec((T, wp), lambda b, p, i: (b, p)),
                  pl.BlockSpec((tq, wo), lambda b, p, i: (b * nq + i, g_col0 + p)),
                  pl.BlockSpec((None, 1, wo), lambda b, p, i: (p, 0, 0))],
        out_specs=pl.BlockSpec((tq, wo), lambda b, p, i: (b * nq + i, p)),
        out_shape=jax.ShapeDtypeStruct((N, HALF), BF16),
        scratch_shapes=[pltpu.VMEM((ATTN_HEADS, nq, LANES, tq), BF16)],
        compiler_params=_cparams(("arbitrary", "arbitrary", "arbitrary")),
        name="attn",
    )(qp, kp, vp, zm, on_g.reshape(groups, 1, wo))


def _rwkv_kernel(r_ref, k_ref, v_ref, zs_ref, w0_ref, dup_ref, a0_ref, iup_ref, gup_ref, kk_ref,
                 ka_ref, rk_ref, lng_ref, lnb_ref, tri_ref, o_ref, state, ybuf, *, chunk, nb):
    C = chunk
    c_idx = pl.program_id(1)

    @pl.when(c_idx == 0)
    def _():
        state[...] = jnp.zeros_like(state)

    R = nb * C
    r = r_ref[...].reshape(R, HALF).astype(F32)
    k = k_ref[...].reshape(R, HALF).astype(F32)
    v = v_ref[...].reshape(R, HALF).astype(F32)
    zs = zs_ref[...].reshape(R, Z_SMALL)
    wd = zs[:, LANES:2 * LANES]
    ad = zs[:, 2 * LANES:3 * LANES]
    gd = zs[:, 3 * LANES:5 * LANES]
    wl = w0_ref[...] + _mm(jnp.tanh(wd), dup_ref[...])
    lw = -jnp.exp(-_softplus(-wl) - 0.5)
    a = _sigmoid(a0_ref[...] + _mm(ad, iup_ref[...]))
    g = _mm(_sigmoid(gd), gup_ref[...])
    kk = k * kk_ref[...]
    k2 = k * (1.0 + (a - 1.0) * ka_ref[...])
    cl = _mm_split(tri_ref[...], lw)
    cl_end = jnp.concatenate(
        [jnp.broadcast_to(cl[(bi + 1) * C - 1:(bi + 1) * C, :], (C, HALF)) for bi in range(nb)], axis=0)
    e_neg = jnp.exp(-cl)
    e_tail = jnp.exp(cl_end - cl)
    pre = dict(rt=r * jnp.exp(cl), at=-kk * jnp.exp(cl - lw), kh=k2 * e_neg, bh=kk * a * e_neg,
               kb=k2 * e_tail, bb=kk * a * e_tail, v=v, kk=kk, pend=jnp.exp(cl_end),
               rkr=r * k2 * rk_ref[...], g=g)

    C2 = 2 * C
    row = lax.broadcasted_iota(I32, (C2, C2), 0)
    col = lax.broadcasted_iota(I32, (C2, C2), 1)
    lower = (col & (C - 1)) <= (row & (C - 1))
    strict = (col & (C - 1)) < (row & (C - 1))
    eye = row == col
    head0 = lax.broadcasted_iota(I32, (C, LANES), 1) < HEAD_DIM
    n_sq = int(np.log2(C)) - 1
    units = [(bi, p) for bi in range(nb) for p in range(N_HEADS // 2)]

    def part(name, bi, p):
        return pre[name][bi * C:(bi + 1) * C, p * LANES:(p + 1) * LANES]

    def stack(x):
        return jnp.concatenate([jnp.where(head0, x, 0.0), jnp.where(head0, 0.0, x)], axis=0)

    def unstack(x):
        return x[0:C] + x[C:C2]

    def head_sum(x):
        s0 = jnp.sum(jnp.where(head0, x, 0.0), axis=1, keepdims=True)
        s1 = jnp.sum(jnp.where(head0, 0.0, x), axis=1, keepdims=True)
        return jnp.where(head0, s0, s1)

    ops = []
    for bi, p in units:
        kk_p = part("kk", bi, p)
        inv = 1.0 / jnp.maximum(jnp.sqrt(head_sum(kk_p * kk_p)), 1e-12)
        ops.append(dict(rt=stack(part("rt", bi, p)), at=stack(part("at", bi, p) * inv),
                        kh=stack(part("kh", bi, p)).astype(BF16), bh=stack(part("bh", bi, p) * inv).astype(BF16),
                        kb=stack(part("kb", bi, p)).astype(BF16), bb=stack(part("bb", bi, p) * inv).astype(BF16),
                        v=stack(part("v", bi, p)).astype(BF16)))
    ra = [jnp.concatenate([o["rt"], o["at"]], axis=0).astype(BF16) for o in ops]
    gg = [_mm_nt(x, jnp.concatenate([o["kh"], o["bh"]], axis=0)) for x, o in zip(ra, ops)]
    g1 = [t[:, 0:C2] for t in gg]
    g2 = [t[:, C2:2 * C2] for t in gg]
    a_rb = [jnp.where(lower, t[0:C2], 0.0).astype(BF16) for t in g2]
    pw = [jnp.where(strict, t[C2:2 * C2], 0.0) for t in g2]
    av = [_mm(jnp.concatenate([jnp.where(lower, t[0:C2], 0.0).astype(BF16),
                               jnp.where(strict, t[C2:2 * C2], 0.0).astype(BF16), o["kb"].T], axis=0),
              o["v"]) for t, o in zip(g1, ops)]
    swap = lambda t: pltpu.roll(t, HEAD_DIM, 1)
    own = (row < C) == (col < HEAD_DIM)
    xs = [o["at"] + swap(t[C2:2 * C2]) for o, t in zip(ops, av)]
    for level in range(n_sq + 1):
        pb = [p.astype(BF16) for p in pw]
        xs = [x + _mm(p, x) for p, x in zip(pb, xs)]
        if level < n_sq:
            pw = [jnp.dot(p, p, preferred_element_type=F32) for p in pb]
    rbx = [_mm(jnp.concatenate([p, o["bb"].T], axis=0), x) for p, o, x in zip(a_rb, ops, xs)]
    for u, (bi, p) in enumerate(units):
        o = ops[u]
        sl = slice(p * LANES, (p + 1) * LANES)
        rb, bx, kv = rbx[u][0:C2], rbx[u][C2:2 * C2], av[u][2 * C2:3 * C2]
        r2 = unstack(o["rt"] + jnp.where(own, rb, 0.0))
        y0 = swap(unstack(jnp.where(own, 0.0, rb))) + unstack(av[u][0:C2])
        m_mat = jnp.where(eye, part("pend", bi, p)[0:1, :], 0.0) + jnp.where(own, bx, 0.0)
        g_mat = swap(jnp.where(own, 0.0, bx)) + kv
        out = _mm(jnp.concatenate([r2, m_mat], axis=0), state[bi, p])
        state[bi, p] = out[C:C + LANES] + g_mat
        y = out[0:C] + y0
        cen = y - head_sum(y) * (1.0 / HEAD_DIM)
        var = head_sum(cen * cen) * (1.0 / HEAD_DIM)
        bonus = head_sum(part("rkr", bi, p)) * part("v", bi, p)
        y = cen * lax.rsqrt(var + LNX_EPS) * lng_ref[:, sl] + lnb_ref[:, sl] + bonus
        ybuf[bi, :, sl] = y * part("g", bi, p)
    o_ref[...] = ybuf[...].astype(BF16)


def _rwkv(zm, zs, p, B, T):
    N = zm.shape[0]
    C = RWKV_CHUNK
    assert 2 * C == LANES and T % C == 0, "a head pair's stacked chunk must fill one 128-row tile"
    nc = T // C
    nb = RWKV_SEQS_PER_STEP if B % RWKV_SEQS_PER_STEP == 0 else 1
    tri = np.kron(np.eye(nb, dtype=np.float32), np.tril(np.ones((C, C), np.float32)))
    pad_rows = lambda w, rows: jnp.zeros((rows, HALF), F32).at[:w.shape[0]].set(w)
    row = lambda a: a.reshape(1, HALF)
    consts = [row(p["rw_w0"]), pad_rows(p["rw_decay_up"], LANES), row(p["rw_a0"]),
              pad_rows(p["rw_iclr_up"], LANES), pad_rows(p["rw_gate_up"], 2 * LANES),
              row(p["rw_k_k"]), row(p["rw_k_a"]), row(p["rw_r_k"]), row(p["rw_lnx_g"]),
              row(p["rw_lnx_b"]), jnp.asarray(tri, BF16)]
    full = lambda a: pl.BlockSpec(a.shape, lambda b, c: (0, 0))
    rcol = 4 * HALF // HALF
    zm3 = zm.reshape(B, T, Z_MAIN)
    out = pl.pallas_call(
        functools.partial(_rwkv_kernel, chunk=C, nb=nb),
        grid=(B // nb, nc),
        in_specs=[pl.BlockSpec((nb, C, HALF), lambda b, c: (b, c, rcol)),
                  pl.BlockSpec((nb, C, HALF), lambda b, c: (b, c, rcol + 1)),
                  pl.BlockSpec((nb, C, HALF), lambda b, c: (b, c, rcol + 2)),
                  pl.BlockSpec((nb, C, Z_SMALL), lambda b, c: (b, c, 0))] + [full(a) for a in consts],
        out_specs=pl.BlockSpec((nb, C, HALF), lambda b, c: (b, c, 0)),
        out_shape=jax.ShapeDtypeStruct((B, T, HALF), BF16),
        scratch_shapes=[pltpu.VMEM((nb, N_HEADS // 2, LANES, LANES), F32), pltpu.VMEM((nb, C, HALF), F32)],
        compiler_params=_cparams(("arbitrary", "arbitrary")),
        name="rwkv",
    )(zm3, zm3, zm3, zs.reshape(B, T, Z_SMALL), *consts)
    return out.reshape(N, HALF)


def _first_index(mask, iota, big):
    return jnp.min(jnp.where(mask, iota, big), axis=0, keepdims=True)


def _outproj_kernel(x_ref, yf_ref, yr_ref, g1_ref, sh2_ref, sc2_ref, g2_ref, n2g_ref, wo_ref, rwt_ref,
                    rb_ref, swg_ref, swu_ref, swd_ref, ustrict_ref,
                    xs_ref, hp_ref, idx_ref, wts_ref, rank_ref, *, tm):
    d = (jnp.dot(yf_ref[...], wo_ref[0:HALF, :], preferred_element_type=F32)
         + jnp.dot(yr_ref[...], wo_ref[HALF:2 * HALF, :], preferred_element_type=F32))
    x2 = x_ref[...] + g1_ref[...] * d
    h = x2 * lax.rsqrt(jnp.mean(x2 * x2, axis=-1, keepdims=True) + RMS_EPS) * n2g_ref[...]
    h = h * (1.0 + sc2_ref[...]) + sh2_ref[...]
    hb = h.astype(BF16)
    act = _silu(jnp.dot(hb, swg_ref[...], preferred_element_type=F32)) * jnp.dot(
        hb, swu_ref[...], preferred_element_type=F32)
    shared = jnp.dot(act.astype(BF16), swd_ref[...], preferred_element_type=F32)
    xs_ref[...] = x2 + g2_ref[...] * shared
    half = D_MODEL // 2
    hp_ref[...] = _pack_halves(h)

    r_hi, r_lo = _bf16_pieces(rwt_ref[...], 2)
    h_lo = (h - hb.astype(F32)).astype(BF16)
    nt = lambda a, b: lax.dot_general(a, b, (((1,), (1,)), ((), ())), preferred_element_type=F32)
    logits = nt(r_hi, hb) + nt(r_hi, h_lo) + nt(r_lo, hb)
    scores = _sigmoid(logits)
    sel = scores + rb_ref[:, 0:1]
    neg = -jnp.inf
    sel3 = sel.reshape(N_GROUPS, GROUP_SIZE, tm)
    io_in = lax.broadcasted_iota(I32, (N_GROUPS, GROUP_SIZE, tm), 1)
    m1 = jnp.max(sel3, axis=1, keepdims=True)
    f1 = jnp.min(jnp.where(sel3 == m1, io_in, GROUP_SIZE), axis=1, keepdims=True)
    m2 = jnp.max(jnp.where(io_in == f1, neg, sel3), axis=1, keepdims=True)
    gs = (m1 + m2).reshape(N_GROUPS, tm)
    io_g = lax.broadcasted_iota(I32, (N_GROUPS, tm), 0)
    gmask = jnp.zeros((N_GROUPS, tm), jnp.bool_)
    for _ in range(TOPK_GROUPS):
        mg = jnp.max(gs, axis=0, keepdims=True)
        fg = _first_index(gs == mg, io_g, N_GROUPS)
        pick = io_g == fg
        gmask = jnp.logical_or(gmask, pick)
        gs = jnp.where(pick, neg, gs)
    emask = jnp.broadcast_to(gmask.reshape(N_GROUPS, 1, tm), (N_GROUPS, GROUP_SIZE, tm)).reshape(N_EXPERTS, tm)
    cur = jnp.where(emask, sel, neg)
    io_e = lax.broadcasted_iota(I32, (N_EXPERTS, tm), 0)
    picks, idxs, wts = [], [], []
    for _ in range(TOP_K):
        me = jnp.max(cur, axis=0, keepdims=True)
        fe = _first_index(cur == me, io_e, N_EXPERTS)
        pick = io_e == fe
        picks.append(pick)
        idxs.append(fe)
        wts.append(jnp.sum(jnp.where(pick, scores, 0.0), axis=0, keepdims=True))
        cur = jnp.where(pick, neg, cur)
    wsum = wts[0]
    for w in wts[1:]:
        wsum = wsum + w
    zero_i = jnp.zeros((1, tm), I32)
    zero_f = jnp.zeros((1, tm), F32)
    idx_ref[...] = jnp.concatenate(idxs + [zero_i, zero_i], axis=0)
    wts_ref[...] = jnp.concatenate([w / wsum * ROUTED_SCALE for w in wts] + [zero_f, zero_f], axis=0)
    cnt = picks[0].astype(F32)
    for pk in picks[1:]:
        cnt = cnt + pk.astype(F32)
    excl = jnp.dot(cnt.astype(BF16), ustrict_ref[...], preferred_element_type=F32)
    ranks = [jnp.sum(jnp.where(pk, excl, 0.0), axis=0, keepdims=True).astype(I32) for pk in picks]
    rank_ref[...] = jnp.concatenate(ranks + [zero_i, zero_i], axis=0)


def _outproj(x2d, yf, yr, mod4, norm2_g, w_out, router_w, router_bias, swg, swu, swd, T):
    N, D = x2d.shape
    tm = min(512, T)
    tps = T // tm
    vec = lambda j: pl.BlockSpec((None, None, 1, D), lambda i: (i // tps, j, 0, 0))
    full = lambda a: pl.BlockSpec(a.shape, lambda i: (0, 0))
    ts = min(MOE_TILE, tm)
    ustrict = jnp.asarray(np.kron(np.eye(tm // ts, dtype=np.float32),
                                  np.triu(np.ones((ts, ts), np.float32), 1)), BF16)
    rb = jnp.broadcast_to(router_bias.reshape(N_EXPERTS, 1), (N_EXPERTS, LANES))
    consts = [norm2_g.reshape(1, D), w_out.astype(BF16), router_w.T, rb, swg.astype(BF16),
              swu.astype(BF16), swd.astype(BF16), ustrict]
    small = lambda dt: jax.ShapeDtypeStruct((8, N), dt)
    return pl.pallas_call(
        functools.partial(_outproj_kernel, tm=tm),
        grid=(N // tm,),
        in_specs=[pl.BlockSpec((tm, D), lambda i: (i, 0)),
                  pl.BlockSpec((tm, HALF), lambda i: (i, 0)),
                  pl.BlockSpec((tm, HALF), lambda i: (i, 0)),
                  vec(2), vec(3), vec(4), vec(5)] + [full(a) for a in consts],
        out_specs=[pl.BlockSpec((tm, D), lambda i: (i, 0)),
                   pl.BlockSpec((tm, D // 2), lambda i: (i, 0)),
                   pl.BlockSpec((8, tm), lambda i: (0, i)),
                   pl.BlockSpec((8, tm), lambda i: (0, i)),
                   pl.BlockSpec((8, tm), lambda i: (0, i))],
        out_shape=[jax.ShapeDtypeStruct((N, D), F32), jax.ShapeDtypeStruct((N, D // 2), jnp.uint32),
                   small(I32), small(F32), small(I32)],
        compiler_params=_cparams(("arbitrary",)),
        name="outproj",
    )(x2d, yf, yr, mod4, mod4, mod4, mod4, *consts)


def _wait_granules(count, wait_rows):
    full = count // WAIT_CHUNK

    def chunk(_, c):
        wait_rows(WAIT_CHUNK * GRANULE)
        return c

    def single(_, c):
        wait_rows(GRANULE)
        return c

    lax.fori_loop(0, full, chunk, 0)
    lax.fori_loop(full * WAIT_CHUNK, count, single, 0)


def _moe_plan(idx_t, rank_t):
    n = idx_t.shape[1]
    n_tiles = n // MOE_TILE
    experts = jnp.arange(N_EXPERTS, dtype=I32)
    hot = idx_t[:TOP_K, :, None] == experts
    cnt = jnp.sum(hot.reshape(TOP_K, n_tiles, MOE_TILE, N_EXPERTS).astype(I32), axis=(0, 2))
    gran = (cnt + GRANULE - 1) // GRANULE
    loc_end = jnp.cumsum(gran, axis=1)
    loc_off = loc_end - gran
    g_tile = loc_end[:, N_EXPERTS - 1]
    padded = (GRANULE * jnp.sum(gran, axis=0) + EXPERT_BLOCK - 1) // EXPERT_BLOCK * EXPERT_BLOCK
    pad_end = jnp.cumsum(padded)
    glob_off = (pad_end - padded)[None, :] + GRANULE * (jnp.cumsum(gran, axis=0) - gran)
    loc_tok = jnp.repeat(GRANULE * loc_off, MOE_TILE, axis=0)
    pos = rank_t[:TOP_K] + jnp.sum(jnp.where(hot, loc_tok[None], 0), axis=-1)
    pos = jnp.concatenate([pos, jnp.full((8 - TOP_K, n), -1, I32)], axis=0)
    g = jnp.arange(MOE_TILE, dtype=I32)
    e_of_g = jnp.minimum(jnp.sum((loc_end[:, None, :] <= g[None, :, None]).astype(I32), axis=-1), N_EXPERTS - 1)
    dst = jnp.sum(jnp.where(e_of_g[:, :, None] == experts,
                            glob_off[:, None, :] + GRANULE * (g[None, :, None] - loc_off[:, None, :]), 0), axis=-1)
    n_blocks = -(-(n * TOP_K + GRANULE * N_EXPERTS * n_tiles) // EXPERT_BLOCK) + N_EXPERTS
    n_rows = n_blocks * EXPERT_BLOCK
    gtab = jnp.where(g[None, :] < g_tile[:, None], dst, 0).reshape(n_tiles, 1, MOE_TILE)
    blk_start = jnp.arange(n_blocks, dtype=I32) * EXPERT_BLOCK
    blk_e = jnp.minimum(jnp.sum((pad_end[None, :] <= blk_start[:, None]).astype(I32), axis=1), N_EXPERTS - 1)
    plan = dict(pos=pos.astype(I32), gtab=gtab.astype(I32), g_tile=g_tile.astype(I32),
                pad_end=pad_end.astype(I32), padded=padded.astype(I32), blk_e=blk_e.astype(I32),
                n_used=(pad_end[N_EXPERTS - 1:] // EXPERT_BLOCK).astype(I32))
    return plan, n_rows


def _dispatch_kernel(pend_ref, padded_ref, nu_ref, gt_ref, gtab_ref, pos_ref, hp_ref, xs_ref, zeros, sbuf,
                     sem, zsem, *, n_blocks, n_tiles):
    @pl.when(pl.program_id(0) == 0)
    def _():
        zeros[...] = jnp.zeros_like(zeros)

        def block_copy(start):
            return pltpu.make_async_copy(zeros, xs_ref.at[pl.ds(pl.multiple_of(start, EXPERT_BLOCK),
                                                                 EXPERT_BLOCK), :], zsem)

        def pad_loop(fn):
            def body(e, _):
                @pl.when(padded_ref[e] > 0)
                def _():
                    fn(block_copy(pend_ref[e] - EXPERT_BLOCK))
                return 0
            lax.fori_loop(0, N_EXPERTS, body, 0)

        def tail_loop(fn):
            def body(b, _):
                fn(block_copy(b * EXPERT_BLOCK))
                return 0
            lax.fori_loop(nu_ref[0], n_blocks, body, 0)

        pad_loop(lambda cp: cp.start())
        tail_loop(lambda cp: cp.start())
        pad_loop(lambda cp: cp.wait())
        tail_loop(lambda cp: cp.wait())

    i = pl.program_id(0)
    slot = i % 2

    def store(g):
        pltpu.make_async_copy(sbuf.at[slot, pl.ds(pl.multiple_of(g * GRANULE, GRANULE), GRANULE), :],
                              xs_ref.at[pl.ds(pl.multiple_of(gtab_ref[0, 0, g], GRANULE), GRANULE), :],
                              sem.at[slot]).start()

    def wait_tile(which, count):
        def wait_rows(rows):
            pltpu.make_async_copy(sbuf.at[which, pl.ds(0, rows), :], xs_ref.at[pl.ds(0, rows), :],
                                  sem.at[which]).wait()
        wait_rows(MIN_GRANULES * GRANULE)
        _wait_granules(count - MIN_GRANULES, wait_rows)

    @pl.when(i >= 2)
    def _():
        wait_tile(slot, gt_ref[jnp.maximum(i - 2, 0)])

    pos = pos_ref[...]
    lo, hi = _unpack_halves(hp_ref[...])
    lo = lo.astype(BF16)
    hi = hi.astype(BF16)
    r_iota = lax.broadcasted_iota(I32, (MOE_ROW_CHUNK, MOE_TILE), 0).astype(F32).astype(BF16)
    one = jnp.ones((), BF16)
    for c in range(MOE_SORTED_ROWS // MOE_ROW_CHUNK):
        rel = pos - c * MOE_ROW_CHUNK
        rel = jnp.where(jnp.logical_and(rel >= 0, rel < MOE_ROW_CHUNK), rel, -1).astype(F32)
        perm = jnp.zeros((MOE_ROW_CHUNK, MOE_TILE), BF16)
        for k in range(TOP_K):
            perm = jnp.where(r_iota == rel[k:k + 1, :].astype(BF16), one, perm)
        sbuf[slot, c * MOE_ROW_CHUNK:(c + 1) * MOE_ROW_CHUNK, :] = _pack_bf16_values(
            jnp.dot(perm, lo, preferred_element_type=F32), jnp.dot(perm, hi, preferred_element_type=F32))
        g_lo, g_hi = c * MOE_ROW_CHUNK // GRANULE, (c + 1) * MOE_ROW_CHUNK // GRANULE
        for g in range(g_lo, min(g_hi, MIN_GRANULES)):
            store(g)
        if g_hi > MIN_GRANULES:
            def body(g, carry):
                store(g)
                return carry
            lax.fori_loop(max(g_lo, MIN_GRANULES), jnp.clip(gt_ref[i], max(g_lo, MIN_GRANULES), g_hi), body, 0)

    @pl.when(i == n_tiles - 1)
    def _():
        wait_tile(slot, gt_ref[i])
        if n_tiles > 1:
            wait_tile(1 - slot, gt_ref[jnp.maximum(i - 1, 0)])


def _dispatch(hp, plan, n_rows):
    N, W = hp.shape
    n_tiles = N // MOE_TILE
    n_blocks = n_rows // EXPERT_BLOCK
    return pl.pallas_call(
        functools.partial(_dispatch_kernel, n_blocks=n_blocks, n_tiles=n_tiles),
        grid_spec=pltpu.PrefetchScalarGridSpec(
            num_scalar_prefetch=4, grid=(n_tiles,),
            in_specs=[pl.BlockSpec((1, 1, MOE_TILE), lambda i, *_: (i, 0, 0), memory_space=pltpu.SMEM),
                      pl.BlockSpec((8, MOE_TILE), lambda i, *_: (0, i)),
                      pl.BlockSpec((MOE_TILE, W), lambda i, *_: (i, 0))],
            out_specs=pl.BlockSpec(memory_space=pl.ANY),
            scratch_shapes=[pltpu.VMEM((EXPERT_BLOCK, W), jnp.uint32),
                            pltpu.VMEM((2, MOE_SORTED_ROWS, W), jnp.uint32),
                            pltpu.SemaphoreType.DMA((2,)), pltpu.SemaphoreType.DMA]),
        out_shape=jax.ShapeDtypeStruct((n_rows, W), jnp.uint32),
        compiler_params=_cparams(("arbitrary",)),
        name="dispatch",
    )(plan["pad_end"], plan["padded"], plan["n_used"], plan["g_tile"], plan["gtab"], plan["pos"], hp)


def _experts_kernel(be_ref, nu_ref, xs_ref, wgf_ref, wuf_ref, wdf_ref, ys_ref, wg_ref, wu_ref, wd_ref):
    i = pl.program_id(0)
    live = i < nu_ref[0]
    new_expert = jnp.logical_or(i == 0, be_ref[i] != be_ref[jnp.maximum(i - 1, 0)])

    @pl.when(jnp.logical_not(live))
    def _():
        ys_ref[...] = jnp.zeros_like(ys_ref)

    @pl.when(jnp.logical_and(live, new_expert))
    def _():
        wg_ref[...] = wgf_ref[...].astype(BF16)
        wu_ref[...] = wuf_ref[...].astype(BF16)
        wd_ref[...] = wdf_ref[...].astype(BF16)

    @pl.when(live)
    def _():
        half = D_MODEL // 2
        lo, hi = _unpack_halves(xs_ref[...])
        lo = lo.astype(BF16)
        hi = hi.astype(BF16)
        gate = (jnp.dot(lo, wg_ref[0:half, :], preferred_element_type=F32)
                + jnp.dot(hi, wg_ref[half:D_MODEL, :], preferred_element_type=F32))
        up = (jnp.dot(lo, wu_ref[0:half, :], preferred_element_type=F32)
              + jnp.dot(hi, wu_ref[half:D_MODEL, :], preferred_element_type=F32))
        y = jnp.dot((_silu(gate) * up).astype(BF16), wd_ref[...], preferred_element_type=F32)
        ys_ref[...] = _pack_halves(y)


def _experts(xs, n_rows, blk_e, n_used, wg, wu, wd):
    W = xs.shape[1]
    n_blocks = n_rows // EXPERT_BLOCK
    row_map = lambda i, be, nu: (jnp.minimum(i, nu[0] - 1), 0)
    return pl.pallas_call(
        _experts_kernel,
        grid_spec=pltpu.PrefetchScalarGridSpec(
            num_scalar_prefetch=2, grid=(n_blocks,),
            in_specs=[pl.BlockSpec((EXPERT_BLOCK, W), row_map),
                      pl.BlockSpec((None, D_MODEL, D_EXPERT), lambda i, be, nu: (be[i], 0, 0)),
                      pl.BlockSpec((None, D_MODEL, D_EXPERT), lambda i, be, nu: (be[i], 0, 0)),
                      pl.BlockSpec((None, D_EXPERT, D_MODEL), lambda i, be, nu: (be[i], 0, 0))],
            out_specs=pl.BlockSpec((EXPERT_BLOCK, W), lambda i, be, nu: (i, 0)),
            scratch_shapes=[pltpu.VMEM((D_MODEL, D_EXPERT), BF16), pltpu.VMEM((D_MODEL, D_EXPERT), BF16),
                            pltpu.VMEM((D_EXPERT, D_MODEL), BF16)]),
        out_shape=jax.ShapeDtypeStruct((n_rows, W), jnp.uint32),
        compiler_params=_cparams(("arbitrary",)),
        name="experts",
    )(blk_e, n_used, xs, wg, wu, wd)


def _combine_kernel(gt_ref, gtab_ref, gtab_next_ref, ys_ref, pos_ref, xs_ref, wts_ref, g2_ref, fg_ref, o_ref,
                    buf, sem, *, n_tiles):
    i = pl.program_id(0)
    slot = i % 2
    nxt = jnp.minimum(i + 1, n_tiles - 1)

    def fetch(which, table_ref, g):
        pltpu.make_async_copy(ys_ref.at[pl.ds(pl.multiple_of(table_ref[0, 0, g], GRANULE), GRANULE), :],
                              buf.at[which, pl.ds(pl.multiple_of(g * GRANULE, GRANULE), GRANULE), :],
                              sem.at[which]).start()

    def fetch_rest(which, table_ref, count):
        def body(g, c):
            fetch(which, table_ref, g)
            return c
        lax.fori_loop(MIN_GRANULES, count, body, 0)

    def wait_tile(which, count):
        def wait_rows(rows):
            pltpu.make_async_copy(ys_ref.at[pl.ds(0, rows), :], buf.at[which, pl.ds(0, rows), :],
                                  sem.at[which]).wait()
        wait_rows(MIN_GRANULES * GRANULE)
        _wait_granules(count - MIN_GRANULES, wait_rows)

    @pl.when(i == 0)
    def _():
        buf[...] = jnp.zeros_like(buf)
        for g in range(MIN_GRANULES):
            fetch(0, gtab_ref, g)
        fetch_rest(0, gtab_ref, gt_ref[0])

    wait_tile(slot, gt_ref[i])

    pos = pos_ref[...]
    w = wts_ref[...]
    c_iota = lax.broadcasted_iota(I32, (MOE_TILE, MOE_ROW_CHUNK), 1).astype(F32).astype(BF16)
    half = D_MODEL // 2
    r_lo = jnp.zeros((MOE_TILE, half), F32)
    r_hi = jnp.zeros((MOE_TILE, half), F32)
    for c in range(MOE_SORTED_ROWS // MOE_ROW_CHUNK):
        rel = pos - c * MOE_ROW_CHUNK
        rel = jnp.where(jnp.logical_and(rel >= 0, rel < MOE_ROW_CHUNK), rel, -1).astype(F32)
        wb = jnp.zeros((MOE_TILE, MOE_ROW_CHUNK), BF16)
        for k in range(TOP_K):
            wb = jnp.where(c_iota == rel[:, k:k + 1].astype(BF16), w[:, k:k + 1].astype(BF16), wb)
        lo, hi = _unpack_halves(buf[slot, c * MOE_ROW_CHUNK:(c + 1) * MOE_ROW_CHUNK, :])
        r_lo = r_lo + jnp.dot(wb, lo.astype(BF16), preferred_element_type=F32)
        r_hi = r_hi + jnp.dot(wb, hi.astype(BF16), preferred_element_type=F32)
        for g in range(min(c * FETCH_PER_CHUNK, MIN_GRANULES), min((c + 1) * FETCH_PER_CHUNK, MIN_GRANULES)):
            fetch(1 - slot, gtab_next_ref, g)
        if c == MIN_GRANULES // FETCH_PER_CHUNK:
            fetch_rest(1 - slot, gtab_next_ref, gt_ref[nxt])
    x3 = xs_ref[...] + g2_ref[...] * jnp.concatenate([r_lo, r_hi], axis=1)
    y = x3 * lax.rsqrt(jnp.mean(x3 * x3, axis=-1, keepdims=True) + RMS_EPS) * fg_ref[...]
    o_ref[...] = y

    @pl.when(i == n_tiles - 1)
    def _():
        wait_tile(1 - slot, gt_ref[nxt])


def _combine(ys, plan, xsr, wts_t, mod4, final_g, T):
    N, D = xsr.shape
    W = ys.shape[1]
    tm = MOE_TILE
    tps = T // tm
    n_tiles = N // tm
    tile_tab = lambda off: pl.BlockSpec((1, 1, MOE_TILE), lambda i, gt: (jnp.minimum(i + off, n_tiles - 1), 0, 0),
                                        memory_space=pltpu.SMEM)
    return pl.pallas_call(
        functools.partial(_combine_kernel, n_tiles=n_tiles),
        grid_spec=pltpu.PrefetchScalarGridSpec(
            num_scalar_prefetch=1, grid=(n_tiles,),
            in_specs=[tile_tab(0), tile_tab(1),
                      pl.BlockSpec(memory_space=pl.ANY),
                      pl.BlockSpec((tm, 8), lambda i, gt: (i, 0)),
                      pl.BlockSpec((tm, D), lambda i, gt: (i, 0)),
                      pl.BlockSpec((tm, 8), lambda i, gt: (i, 0)),
                      pl.BlockSpec((None, None, 1, D), lambda i, gt: (i // tps, 5, 0, 0)),
                      pl.BlockSpec((1, D), lambda i, gt: (0, 0))],
            out_specs=pl.BlockSpec((tm, D), lambda i, gt: (i, 0)),
            scratch_shapes=[pltpu.VMEM((2, MOE_SORTED_ROWS, W), jnp.uint32), pltpu.SemaphoreType.DMA((2,))]),
        out_shape=jax.ShapeDtypeStruct((N, D), F32),
        compiler_params=_cparams(("arbitrary",)),
        name="combine",
    )(plan["g_tile"], plan["gtab"], plan["gtab"], ys, plan["pos"].T, xsr, wts_t, mod4,
      final_g.reshape(1, D))


def _pack_w_in(w_in):
    D = w_in.shape[0]
    fox_cols = 4 * HALF + 3 * N_HEADS
    wf = w_in[:, :fox_cols]
    wr = w_in[:, fox_cols:]
    o = 3 * HALF
    pad = lambda a, n: jnp.concatenate([a, jnp.zeros((D, n - a.shape[1]), a.dtype)], axis=1)
    parts = [wf[:, :4 * HALF], wr[:, :o],
             pad(wf[:, 4 * HALF:], LANES),
             pad(wr[:, o:o + DECAY_LORA], LANES),
             pad(wr[:, o + DECAY_LORA:o + DECAY_LORA + ICLR_LORA], LANES),
             pad(wr[:, o + DECAY_LORA + ICLR_LORA:], 2 * LANES)]
    return jnp.concatenate(parts, axis=1).astype(BF16)


def _pack_mu(mu):
    o = 3 * HALF
    pad = lambda a, n: jnp.concatenate([a, jnp.zeros((n - a.shape[0],), a.dtype)])
    small = jnp.concatenate([jnp.zeros((LANES,), mu.dtype),
                             pad(mu[o:o + DECAY_LORA], LANES),
                             pad(mu[o + DECAY_LORA:o + DECAY_LORA + ICLR_LORA], LANES),
                             pad(mu[o + DECAY_LORA + ICLR_LORA:], 2 * LANES)])
    return mu[:o].reshape(1, o), small.reshape(1, Z_SMALL)


def kernel(x, c, norm1_g, norm2_g, ada_w, ada_b, w_in, w_out, fox_qn_g, fox_kn_g, fox_on_g, fox_forget_b,
           rw_mu, rw_w0, rw_decay_up, rw_a0, rw_iclr_up, rw_gate_up, rw_k_k, rw_k_a, rw_r_k, rw_lnx_g,
           rw_lnx_b, router_w, router_bias, exp_w_gate, exp_w_up, exp_w_down, sh_w_gate, sh_w_up,
           sh_w_down, final_g):
    B, T, D = x.shape
    N = B * T
    depth = norm1_g.shape[0]
    assert depth == 1, "the combine kernel fuses the final RMSNorm, so exactly one layer is supported"
    xf = x.reshape(N, D)
    for l in range(depth):
        mod4 = _ada(c, ada_w[l], ada_b[l]).reshape(B, 6, 1, D)
        mu_big, mu_small = _pack_mu(rw_mu[l])
        zm, zs = _inproj(xf, mod4, norm1_g[l], _pack_w_in(w_in[l]), mu_big, mu_small, T)
        qp, kp, vp = _foxprep(zm, zs, fox_forget_b[l], fox_qn_g[l], fox_kn_g[l], T)
        y_fox = _attention(qp, kp, vp, zm, fox_on_g[l], B, T)
        rw = dict(rw_w0=rw_w0[l], rw_decay_up=rw_decay_up[l], rw_a0=rw_a0[l], rw_iclr_up=rw_iclr_up[l],
                  rw_gate_up=rw_gate_up[l], rw_k_k=rw_k_k[l], rw_k_a=rw_k_a[l], rw_r_k=rw_r_k[l],
                  rw_lnx_g=rw_lnx_g[l], rw_lnx_b=rw_lnx_b[l])
        y_rwkv = _rwkv(zm, zs, rw, B, T)
        xsr, hp, idx_t, wts_t, rank_t = _outproj(
            xf, y_fox, y_rwkv, mod4, norm2_g[l], w_out[l], router_w[l], router_bias[l],
            sh_w_gate[l], sh_w_up[l], sh_w_down[l], T)
        plan, n_rows = _moe_plan(idx_t, rank_t)
        xs = _dispatch(hp, plan, n_rows)
        ys = _experts(xs, n_rows, plan["blk_e"], plan["n_used"], exp_w_gate[l], exp_w_up[l], exp_w_down[l])
        xf = _combine(ys, plan, xsr, wts_t.T, mod4, final_g, T)
    return xf.reshape(B, T, D)
```

```python
import functools

import jax
import jax.numpy as jnp
import numpy as np
from jax import lax
from jax.experimental import pallas as pl
from jax.experimental.pallas import tpu as pltpu

F32 = jnp.float32
BF16 = jnp.bfloat16
I32 = jnp.int32
HIGHEST = lax.Precision.HIGHEST

D_MODEL = 1024
HEAD_DIM = 64
N_HEADS = 8
HALF = N_HEADS * HEAD_DIM
RMS_EPS = 1e-6
LNX_EPS = 64e-5
LOG2E = 1.4426950408889634
DECAY_LORA = 64
ICLR_LORA = 64
N_EXPERTS = 64
N_GROUPS = 8
GROUP_SIZE = N_EXPERTS // N_GROUPS
TOPK_GROUPS = 4
TOP_K = 6
D_EXPERT = 256
ROUTED_SCALE = 2.5
EXPERT_BLOCK = 1024

LANES = 128
Z_MAIN = 4 * HALF + 3 * HALF
Z_SMALL = 5 * LANES
VMEM_LIMIT = 56 * 1024 * 1024
ATTN_BLOCK = 512
ATTN_HEADS = 4
WAIT_CHUNK = 16
MOE_TILE = 256
GRANULE = 8
MOE_SORTED_ROWS = MOE_TILE * TOP_K + N_EXPERTS * GRANULE
MIN_GRANULES = MOE_TILE * TOP_K // GRANULE
FETCH_PER_CHUNK = 96
MOE_ROW_CHUNK = 256
RWKV_CHUNK = 64
RWKV_SEQS_PER_STEP = 4


def _cparams(semantics):
    return pltpu.CompilerParams(dimension_semantics=semantics, vmem_limit_bytes=VMEM_LIMIT)


def _mm(a, b):
    return jnp.dot(a.astype(BF16), b.astype(BF16), preferred_element_type=F32)


def _mm_nt(a, b):
    return lax.dot_general(a.astype(BF16), b.astype(BF16), (((1,), (1,)), ((), ())),
                           preferred_element_type=F32)


def _mm_f32(a, b):
    return jnp.dot(a, b, precision=HIGHEST, preferred_element_type=F32)


def _bf16_pieces(x, passes):
    pieces = []
    for _ in range(passes):
        piece = x.astype(BF16)
        pieces.append(piece)
        x = x - piece.astype(F32)
    return pieces


def _mm_split(m01, x, passes=3):
    return sum(jnp.dot(m01, p, preferred_element_type=F32) for p in _bf16_pieces(x, passes))


def _mm_split_r(x, m01, passes=2):
    return sum(jnp.dot(p, m01, preferred_element_type=F32) for p in _bf16_pieces(x, passes))


def _sigmoid(x):
    return 1.0 / (1.0 + jnp.exp(-x))


def _softplus(x):
    return jnp.maximum(x, 0.0) + jnp.log(1.0 + jnp.exp(-jnp.abs(x)))


def _silu(x):
    return x * _sigmoid(x)


def _pack_halves(x):
    w = x.shape[1] // 2
    bits = lambda t: lax.bitcast_convert_type(t.astype(jnp.bfloat16).astype(F32), jnp.uint32)
    return (bits(x[:, 0:w]) >> 16) | (bits(x[:, w:2 * w]) & jnp.uint32(0xFFFF0000))


def _pack_bf16_values(lo, hi):
    return (lax.bitcast_convert_type(lo, jnp.uint32) >> 16) | lax.bitcast_convert_type(hi, jnp.uint32)


def _unpack_halves(p):
    lo = lax.bitcast_convert_type(p << 16, F32)
    hi = lax.bitcast_convert_type(p & jnp.uint32(0xFFFF0000), F32)
    return lo, hi


def _shift_rows(z, carry_ref, first):
    rows = z.shape[0]
    prev_row = jnp.where(first, 0.0, carry_ref[0:1, :])
    prev = pltpu.roll(z, 1, 0)
    row0 = lax.broadcasted_iota(I32, (rows, 1), 0) == 0
    prev = jnp.where(row0, prev_row, prev)
    carry_ref[0:1, :] = z[rows - 1:rows, :]
    return prev


def _ada_kernel(c_ref, w_ref, b_ref, o_ref):
    o_ref[...] = _mm_f32(_silu(c_ref[...]), w_ref[...]) + b_ref[...]


def _ada(c, ada_w, ada_b):
    B, D = c.shape
    n_out = ada_w.shape[1]
    tn = 512
    return pl.pallas_call(
        _ada_kernel,
        grid=(n_out // tn,),
        in_specs=[pl.BlockSpec((B, D), lambda j: (0, 0)),
                  pl.BlockSpec((D, tn), lambda j: (0, j)),
                  pl.BlockSpec((1, tn), lambda j: (0, j))],
        out_specs=pl.BlockSpec((B, tn), lambda j: (0, j)),
        out_shape=jax.ShapeDtypeStruct((B, n_out), F32),
        compiler_params=_cparams(("arbitrary",)),
        name="ada",
    )(c, ada_w, ada_b.reshape(1, n_out))


def _inproj_kernel(x_ref, g_ref, sh_ref, sc_ref, w_ref, mub_ref, mus_ref, zm_ref, zs_ref,
                   carry_b, carry_s, *, tiles_per_seq):
    first = (pl.program_id(0) % tiles_per_seq) == 0
    x = x_ref[...]
    h = x * lax.rsqrt(jnp.mean(x * x, axis=-1, keepdims=True) + RMS_EPS) * g_ref[...]
    hb = (h * (1.0 + sc_ref[...]) + sh_ref[...]).astype(BF16)
    nf = 4 * HALF
    zm_ref[:, 0:nf] = jnp.dot(hb, w_ref[:, 0:nf], preferred_element_type=F32).astype(BF16)
    zr = jnp.dot(hb, w_ref[:, nf:Z_MAIN], preferred_element_type=F32)
    zr = zr + mub_ref[...] * (_shift_rows(zr, carry_b, first) - zr)
    zm_ref[:, nf:Z_MAIN] = zr.astype(BF16)
    zs = jnp.dot(hb, w_ref[:, Z_MAIN:Z_MAIN + Z_SMALL], preferred_element_type=F32)
    zs_ref[...] = zs + mus_ref[...] * (_shift_rows(zs, carry_s, first) - zs)


def _inproj(x2d, mod4, norm_g, w_all, mu_big, mu_small, T):
    N, D = x2d.shape
    tm = min(512, T)
    tps = T // tm
    vec = lambda j: pl.BlockSpec((None, None, 1, D), lambda i: (i // tps, j, 0, 0))
    return pl.pallas_call(
        functools.partial(_inproj_kernel, tiles_per_seq=tps),
        grid=(N // tm,),
        in_specs=[pl.BlockSpec((tm, D), lambda i: (i, 0)),
                  pl.BlockSpec((1, D), lambda i: (0, 0)),
                  vec(0), vec(1),
                  pl.BlockSpec((D, Z_MAIN + Z_SMALL), lambda i: (0, 0)),
                  pl.BlockSpec((1, 3 * HALF), lambda i: (0, 0)),
                  pl.BlockSpec((1, Z_SMALL), lambda i: (0, 0))],
        out_specs=[pl.BlockSpec((tm, Z_MAIN), lambda i: (i, 0)),
                   pl.BlockSpec((tm, Z_SMALL), lambda i: (i, 0))],
        out_shape=[jax.ShapeDtypeStruct((N, Z_MAIN), BF16),
                   jax.ShapeDtypeStruct((N, Z_SMALL), F32)],
        scratch_shapes=[pltpu.VMEM((8, 3 * HALF), F32), pltpu.VMEM((8, Z_SMALL), F32)],
        compiler_params=_cparams(("arbitrary",)),
        name="inproj",
    )(x2d, norm_g.reshape(1, D), mod4, mod4, w_all, mu_big, mu_small)


def _foxprep_kernel(q_ref, k_ref, v_ref, zs_ref, fb_ref, qg_ref, kg_ref, tri_ref, eexp_ref,
                    esum_ref, e8_ref, plq_ref, plk_ref, plv_ref, cq_ref, ck_ref, cv_ref,
                    qp_ref, kp_ref, vp_ref, carry_k, carry_v, carry_c, *, tiles_per_seq):
    first = (pl.program_id(0) % tiles_per_seq) == 0
    zs = zs_ref[...]
    logf = -_softplus(-(zs + fb_ref[...]))
    cum = _mm_split(tri_ref[...], logf) + jnp.where(first, 0.0, carry_c[0:1, :])
    carry_c[0:1, :] = cum[cum.shape[0] - 1:, :]
    cum2 = cum * LOG2E
    head_lane = lax.broadcasted_iota(I32, cum2.shape, 1) < N_HEADS
    c_hi, c_mid, c_lo = [jnp.where(head_lane, p.astype(F32), 0.0) for p in _bf16_pieces(cum2, 3)]
    aug = (c_hi + pltpu.roll(c_mid, N_HEADS, 1) + pltpu.roll(c_lo, 2 * N_HEADS, 1)).astype(BF16)
    a_full = _mm_split_r(_sigmoid(zs), eexp_ref[...])
    k = k_ref[...].astype(F32)
    v = v_ref[...].astype(F32)
    a_k = a_full[:, 0:HALF]
    a_v = a_full[:, HALF:2 * HALF]
    k = a_k * _shift_rows(k, carry_k, first) + (1.0 - a_k) * k
    v = a_v * _shift_rows(v, carry_v, first) + (1.0 - a_v) * v
    q = q_ref[...].astype(F32)

    def head_rms(t, gain):
        ms = _mm_split_r(t * t, esum_ref[...]) * (1.0 / HEAD_DIM)
        inv = _mm_split_r(lax.rsqrt(ms + RMS_EPS), e8_ref[...])
        return t * inv * gain

    qn = head_rms(q, qg_ref[...]) * (HEAD_DIM ** -0.5 * LOG2E)
    kn = head_rms(k, kg_ref[...])
    lhs_q = jnp.concatenate([qn.astype(BF16), aug], axis=1)
    lhs_k = jnp.concatenate([kn.astype(BF16), aug], axis=1)
    qp_ref[...] = (jnp.dot(lhs_q, plq_ref[...], preferred_element_type=F32) + cq_ref[...]).astype(BF16)
    kp_ref[...] = (jnp.dot(lhs_k, plk_ref[...], preferred_element_type=F32) + ck_ref[...]).astype(BF16)
    vp_ref[...] = (jnp.dot(v.astype(BF16), plv_ref[...], preferred_element_type=F32)
                   + cv_ref[...]).astype(BF16)


def _fox_constants(tm):
    hp = N_HEADS * LANES
    eexp = np.zeros((LANES, 2 * HALF), np.float32)
    esum = np.zeros((HALF, LANES), np.float32)
    e8 = np.zeros((LANES, HALF), np.float32)
    plq = np.zeros((HALF + LANES, hp), np.float32)
    plk = np.zeros((HALF + LANES, hp), np.float32)
    plv = np.zeros((HALF, hp), np.float32)
    cq = np.zeros((1, hp), np.float32)
    ck = np.zeros((1, hp), np.float32)
    cv = np.zeros((1, hp), np.float32)
    for h in range(N_HEADS):
        sl = slice(h * HEAD_DIM, (h + 1) * HEAD_DIM)
        eexp[8 + h, sl] = 1.0
        eexp[16 + h, HALF + h * HEAD_DIM:HALF + (h + 1) * HEAD_DIM] = 1.0
        esum[sl, h] = 1.0
        e8[h, sl] = 1.0
        base = h * LANES
        for d in range(HEAD_DIM):
            plq[h * HEAD_DIM + d, base + d] = 1.0
            plk[h * HEAD_DIM + d, base + d] = 1.0
            plv[h * HEAD_DIM + d, base + d] = 1.0
        for j in range(3):
            plq[HALF + j * N_HEADS + h, base + HEAD_DIM + j] = 1.0
            plk[HALF + j * N_HEADS + h, base + HEAD_DIM + 3 + j] = -1.0
            cq[0, base + HEAD_DIM + 3 + j] = 1.0
            ck[0, base + HEAD_DIM + j] = 1.0
        cv[0, base + HEAD_DIM] = 1.0
    tri = np.tril(np.ones((tm, tm), np.float32))
    bf = lambda a: jnp.asarray(a, BF16)
    return dict(tri=bf(tri), eexp=bf(eexp), esum=bf(esum), e8=bf(e8),
                plq=bf(plq), plk=bf(plk), plv=bf(plv), cq=jnp.asarray(cq), ck=jnp.asarray(ck),
                cv=jnp.asarray(cv))


def _foxprep(zm, zs, forget_b, qn_g, kn_g, T):
    N = zm.shape[0]
    tm = min(512, T)
    tps = T // tm
    cst = _fox_constants(tm)
    hp = N_HEADS * LANES
    fb = jnp.zeros((1, LANES), F32).at[0, :N_HEADS].set(forget_b)
    full = lambda a: pl.BlockSpec(a.shape, lambda i: (0, 0))
    consts = [fb, qn_g.reshape(1, HALF), kn_g.reshape(1, HALF), cst["tri"], cst["eexp"], cst["esum"],
              cst["e8"], cst["plq"], cst["plk"], cst["plv"], cst["cq"], cst["ck"], cst["cv"]]
    return pl.pallas_call(
        functools.partial(_foxprep_kernel, tiles_per_seq=tps),
        grid=(N // tm,),
        in_specs=[pl.BlockSpec((tm, HALF), lambda i: (i, 0)),
                  pl.BlockSpec((tm, HALF), lambda i: (i, 1)),
                  pl.BlockSpec((tm, HALF), lambda i: (i, 2)),
                  pl.BlockSpec((tm, LANES), lambda i: (i, 0))] + [full(a) for a in consts],
        out_specs=[pl.BlockSpec((tm, hp), lambda i: (i, 0))] * 3,
        out_shape=[jax.ShapeDtypeStruct((N, hp), BF16)] * 3,
        scratch_shapes=[pltpu.VMEM((8, HALF), F32), pltpu.VMEM((8, HALF), F32),
                        pltpu.VMEM((8, LANES), F32)],
        compiler_params=_cparams(("arbitrary",)),
        name="foxprep",
    )(zm, zm, zm, zs, *consts)


def _attn_kernel(q_ref, k_ref, v_ref, g_ref, ong_ref, o_ref, vt_ref, *, tq):
    i = pl.program_id(2)
    n_kv = vt_ref.shape[1]
    heads = range(ATTN_HEADS)
    lanes = [slice(hh * LANES, (hh + 1) * LANES) for hh in heads]

    @pl.when(i == 0)
    def _():
        for hh in heads:
            for c in range(n_kv):
                vt_ref[hh, c] = v_ref[c * tq:(c + 1) * tq, lanes[hh]].T

    key = lax.broadcasted_iota(I32, (tq, tq), 0)
    qry = lax.broadcasted_iota(I32, (tq, tq), 1)
    causal = key <= qry
    qs = [q_ref[:, lanes[hh]] for hh in heads]

    def step(j, carry, masked):
        m, acc = carry
        start = pl.multiple_of(j * tq, tq)
        s = [lax.dot_general(k_ref[pl.ds(start, tq), lanes[hh]], qs[hh], (((1,), (1,)), ((), ())),
                             preferred_element_type=F32) for hh in heads]
        m_out, acc_out = [], []
        for hh in heads:
            sh = jnp.where(causal, s[hh], -jnp.inf) if masked else s[hh]
            m_new = jnp.maximum(m[hh], jnp.max(sh, axis=0, keepdims=True))
            p = jnp.exp2(sh - m_new)
            acc_out.append(jnp.exp2(m[hh] - m_new) * acc[hh]
                           + jnp.dot(vt_ref[hh, j], p.astype(BF16), preferred_element_type=F32))
            m_out.append(m_new)
        return tuple(m_out), tuple(acc_out)

    init = (tuple(jnp.full((1, tq), -jnp.inf, F32) for _ in heads),
            tuple(jnp.zeros((LANES, tq), F32) for _ in heads))
    carry = lax.fori_loop(0, i, functools.partial(step, masked=False), init)
    _, acc = step(i, carry, True)
    chan = lax.broadcasted_iota(I32, (LANES, tq), 0)
    chan_w = jnp.where(chan < HEAD_DIM, 1.0 / HEAD_DIM, jnp.where(chan == HEAD_DIM, RMS_EPS, 0.0))
    lane = lax.broadcasted_iota(I32, (tq, LANES), 1)
    outs = []
    for hh in heads:
        t = jnp.sum(acc[hh] * acc[hh] * chan_w, axis=0, keepdims=True)
        outs.append((acc[hh] * lax.rsqrt(t)).T)
    o = jnp.concatenate([jnp.where(lane < HEAD_DIM, outs[2 * p], pltpu.roll(outs[2 * p + 1], HEAD_DIM, 1))
                         for p in range(ATTN_HEADS // 2)], axis=1)
    y = o * ong_ref[...] * _sigmoid(g_ref[...].astype(F32))
    o_ref[...] = y.astype(BF16)


def _attention(qp, kp, vp, zm, on_g, B, T):
    N = qp.shape[0]
    tq = min(ATTN_BLOCK, T)
    nq = T // tq
    groups = N_HEADS // ATTN_HEADS
    wp = ATTN_HEADS * LANES
    wo = ATTN_HEADS * HEAD_DIM
    g_col0 = 3 * HALF // wo
    return pl.pallas_call(
        functools.partial(_attn_kernel, tq=tq),
        grid=(B, groups, nq),
        in_specs=[pl.BlockSpec((tq, wp), lambda b, p, i: (b * nq + i, p)),
                  pl.BlockSpec((T, wp), lambda b, p, i: (b, p)),
                  pl.BlockSpec((T, wp), lambda b, p, i: (b, p)),
                  pl.BlockSpec((tq, wo), lambda b, p, i: (b * nq + i, g_col0 + p)),
                  pl.BlockSpec((None, 1, wo), lambda b, p, i: (p, 0, 0))],
        out_specs=pl.BlockSpec((tq, wo), lambda b, p, i: (b * nq + i, p)),
        out_shape=jax.ShapeDtypeStruct((N, HALF), BF16),
        scratch_shapes=[pltpu.VMEM((ATTN_HEADS, nq, LANES, tq), BF16)],
        compiler_params=_cparams(("arbitrary", "arbitrary", "arbitrary")),
        name="attn",
    )(qp, kp, vp, zm, on_g.reshape(groups, 1, wo))


def _rwkv_kernel(r_ref, k_ref, v_ref, zs_ref, w0_ref, dup_ref, a0_ref, iup_ref, gup_ref, kk_ref,
                 ka_ref, rk_ref, lng_ref, lnb_ref, tri_ref, o_ref, state, ybuf, *, chunk, nb):
    C = chunk
    c_idx = pl.program_id(1)

    @pl.when(c_idx == 0)
    def _():
        state[...] = jnp.zeros_like(state)

    R = nb * C
    r = r_ref[...].reshape(R, HALF).astype(F32)
    k = k_ref[...].reshape(R, HALF).astype(F32)
    v = v_ref[...].reshape(R, HALF).astype(F32)
    zs = zs_ref[...].reshape(R, Z_SMALL)
    wd = zs[:, LANES:2 * LANES]
    ad = zs[:, 2 * LANES:3 * LANES]
    gd = zs[:, 3 * LANES:5 * LANES]
    wl = w0_ref[...] + _mm(jnp.tanh(wd), dup_ref[...])
    lw = -jnp.exp(-_softplus(-wl) - 0.5)
    a = _sigmoid(a0_ref[...] + _mm(ad, iup_ref[...]))
    g = _mm(_sigmoid(gd), gup_ref[...])
    kk = k * kk_ref[...]
    k2 = k * (1.0 + (a - 1.0) * ka_ref[...])
    cl = _mm_split(tri_ref[...], lw)
    cl_end = jnp.concatenate(
        [jnp.broadcast_to(cl[(bi + 1) * C - 1:(bi + 1) * C, :], (C, HALF)) for bi in range(nb)], axis=0)
    e_neg = jnp.exp(-cl)
    e_tail = jnp.exp(cl_end - cl)
    pre = dict(rt=r * jnp.exp(cl), at=-kk * jnp.exp(cl - lw), kh=k2 * e_neg, bh=kk * a * e_neg,
               kb=k2 * e_tail, bb=kk * a * e_tail, v=v, kk=kk, pend=jnp.exp(cl_end),
               rkr=r * k2 * rk_ref[...], g=g)

    C2 = 2 * C
    row = lax.broadcasted_iota(I32, (C2, C2), 0)
    col = lax.broadcasted_iota(I32, (C2, C2), 1)
    lower = (col & (C - 1)) <= (row & (C - 1))
    strict = (col & (C - 1)) < (row & (C - 1))
    eye = row == col
    head0 = lax.broadcasted_iota(I32, (C, LANES), 1) < HEAD_DIM
    n_sq = int(np.log2(C)) - 1
    units = [(bi, p) for bi in range(nb) for p in range(N_HEADS // 2)]

    def part(name, bi, p):
        return pre[name][bi * C:(bi + 1) * C, p * LANES:(p + 1) * LANES]

    def stack(x):
        return jnp.concatenate([jnp.where(head0, x, 0.0), jnp.where(head0, 0.0, x)], axis=0)

    def unstack(x):
        return x[0:C] + x[C:C2]

    def head_sum(x):
        s0 = jnp.sum(jnp.where(head0, x, 0.0), axis=1, keepdims=True)
        s1 = jnp.sum(jnp.where(head0, 0.0, x), axis=1, keepdims=True)
        return jnp.where(head0, s0, s1)

    ops = []
    for bi, p in units:
        kk_p = part("kk", bi, p)
        inv = 1.0 / jnp.maximum(jnp.sqrt(head_sum(kk_p * kk_p)), 1e-12)
        ops.append(dict(rt=stack(part("rt", bi, p)), at=stack(part("at", bi, p) * inv),
                        kh=stack(part("kh", bi, p)).astype(BF16), bh=stack(part("bh", bi, p) * inv).astype(BF16),
                        kb=stack(part("kb", bi, p)).astype(BF16), bb=stack(part("bb", bi, p) * inv).astype(BF16),
                        v=stack(part("v", bi, p)).astype(BF16)))
    ra = [jnp.concatenate([o["rt"], o["at"]], axis=0).astype(BF16) for o in ops]
    gg = [_mm_nt(x, jnp.concatenate([o["kh"], o["bh"]], axis=0)) for x, o in zip(ra, ops)]
    g1 = [t[:, 0:C2] for t in gg]
    g2 = [t[:, C2:2 * C2] for t in gg]
    a_rb = [jnp.where(lower, t[0:C2], 0.0).astype(BF16) for t in g2]
    pw = [jnp.where(strict, t[C2:2 * C2], 0.0) for t in g2]
    av = [_mm(jnp.concatenate([jnp.where(lower, t[0:C2], 0.0).astype(BF16),
                               jnp.where(strict, t[C2:2 * C2], 0.0).astype(BF16), o["kb"].T], axis=0),
              o["v"]) for t, o in zip(g1, ops)]
    swap = lambda t: pltpu.roll(t, HEAD_DIM, 1)
    own = (row < C) == (col < HEAD_DIM)
    xs = [o["at"] + swap(t[C2:2 * C2]) for o, t in zip(ops, av)]
    for level in range(n_sq + 1):
        pb = [p.astype(BF16) for p in pw]
        xs = [x + _mm(p, x) for p, x in zip(pb, xs)]
        if level < n_sq:
            pw = [jnp.dot(p, p, preferred_element_type=F32) for p in pb]
    rbx = [_mm(jnp.concatenate([p, o["bb"].T], axis=0), x) for p, o, x in zip(a_rb, ops, xs)]
    for u, (bi, p) in enumerate(units):
        o = ops[u]
        sl = slice(p * LANES, (p + 1) * LANES)
        rb, bx, kv = rbx[u][0:C2], rbx[u][C2:2 * C2], av[u][2 * C2:3 * C2]
        r2 = unstack(o["rt"] + jnp.where(own, rb, 0.0))
        y0 = swap(unstack(jnp.where(own, 0.0, rb))) + unstack(av[u][0:C2])
        m_mat = jnp.where(eye, part("pend", bi, p)[0:1, :], 0.0) + jnp.where(own, bx, 0.0)
        g_mat = swap(jnp.where(own, 0.0, bx)) + kv
        out = _mm(jnp.concatenate([r2, m_mat], axis=0), state[bi, p])
        state[bi, p] = out[C:C + LANES] + g_mat
        y = out[0:C] + y0
        cen = y - head_sum(y) * (1.0 / HEAD_DIM)
        var = head_sum(cen * cen) * (1.0 / HEAD_DIM)
        bonus = head_sum(part("rkr", bi, p)) * part("v", bi, p)
        y = cen * lax.rsqrt(var + LNX_EPS) * lng_ref[:, sl] + lnb_ref[:, sl] + bonus
        ybuf[bi, :, sl] = y * part("g", bi, p)
    o_ref[...] = ybuf[...].astype(BF16)


def _rwkv(zm, zs, p, B, T):
    N = zm.shape[0]
    C = RWKV_CHUNK
    assert 2 * C == LANES and T % C == 0, "a head pair's stacked chunk must fill one 128-row tile"
    nc = T // C
    nb = RWKV_SEQS_PER_STEP if B % RWKV_SEQS_PER_STEP == 0 else 1
    tri = np.kron(np.eye(nb, dtype=np.float32), np.tril(np.ones((C, C), np.float32)))
    pad_rows = lambda w, rows: jnp.zeros((rows, HALF), F32).at[:w.shape[0]].set(w)
    row = lambda a: a.reshape(1, HALF)
    consts = [row(p["rw_w0"]), pad_rows(p["rw_decay_up"], LANES), row(p["rw_a0"]),
              pad_rows(p["rw_iclr_up"], LANES), pad_rows(p["rw_gate_up"], 2 * LANES),
              row(p["rw_k_k"]), row(p["rw_k_a"]), row(p["rw_r_k"]), row(p["rw_lnx_g"]),
              row(p["rw_lnx_b"]), jnp.asarray(tri, BF16)]
    full = lambda a: pl.BlockSpec(a.shape, lambda b, c: (0, 0))
    rcol = 4 * HALF // HALF
    zm3 = zm.reshape(B, T, Z_MAIN)
    out = pl.pallas_call(
        functools.partial(_rwkv_kernel, chunk=C, nb=nb),
        grid=(B // nb, nc),
        in_specs=[pl.BlockSpec((nb, C, HALF), lambda b, c: (b, c, rcol)),
                  pl.BlockSpec((nb, C, HALF), lambda b, c: (b, c, rcol + 1)),
                  pl.BlockSpec((nb, C, HALF), lambda b, c: (b, c, rcol + 2)),
                  pl.BlockSpec((nb, C, Z_SMALL), lambda b, c: (b, c, 0))] + [full(a) for a in consts],
        out_specs=pl.BlockSpec((nb, C, HALF), lambda b, c: (b, c, 0)),
        out_shape=jax.ShapeDtypeStruct((B, T, HALF), BF16),
        scratch_shapes=[pltpu.VMEM((nb, N_HEADS // 2, LANES, LANES), F32), pltpu.VMEM((nb, C, HALF), F32)],
        compiler_params=_cparams(("arbitrary", "arbitrary")),
        name="rwkv",
    )(zm3, zm3, zm3, zs.reshape(B, T, Z_SMALL), *consts)
    return out.reshape(N, HALF)


def _first_index(mask, iota, big):
    return jnp.min(jnp.where(mask, iota, big), axis=0, keepdims=True)


def _outproj_kernel(x_ref, yf_ref, yr_ref, g1_ref, sh2_ref, sc2_ref, g2_ref, n2g_ref, wo_ref, rwt_ref,
                    rb_ref, swg_ref, swu_ref, swd_ref, ustrict_ref,
                    xs_ref, hp_ref, idx_ref, wts_ref, rank_ref, *, tm):
    d = (jnp.dot(yf_ref[...], wo_ref[0:HALF, :], preferred_element_type=F32)
         + jnp.dot(yr_ref[...], wo_ref[HALF:2 * HALF, :], preferred_element_type=F32))
    x2 = x_ref[...] + g1_ref[...] * d
    h = x2 * lax.rsqrt(jnp.mean(x2 * x2, axis=-1, keepdims=True) + RMS_EPS) * n2g_ref[...]
    h = h * (1.0 + sc2_ref[...]) + sh2_ref[...]
    hb = h.astype(BF16)
    act = _silu(jnp.dot(hb, swg_ref[...], preferred_element_type=F32)) * jnp.dot(
        hb, swu_ref[...], preferred_element_type=F32)
    shared = jnp.dot(act.astype(BF16), swd_ref[...], preferred_element_type=F32)
    xs_ref[...] = x2 + g2_ref[...] * shared
    half = D_MODEL // 2
    hp_ref[...] = _pack_halves(h)

    r_hi, r_lo = _bf16_pieces(rwt_ref[...], 2)
    h_lo = (h - hb.astype(F32)).astype(BF16)
    nt = lambda a, b: lax.dot_general(a, b, (((1,), (1,)), ((), ())), preferred_element_type=F32)
    logits = nt(r_hi, hb) + nt(r_hi, h_lo) + nt(r_lo, hb)
    scores = _sigmoid(logits)
    sel = scores + rb_ref[:, 0:1]
    neg = -jnp.inf
    sel3 = sel.reshape(N_GROUPS, GROUP_SIZE, tm)
    io_in = lax.broadcasted_iota(I32, (N_GROUPS, GROUP_SIZE, tm), 1)
    m1 = jnp.max(sel3, axis=1, keepdims=True)
    f1 = jnp.min(jnp.where(sel3 == m1, io_in, GROUP_SIZE), axis=1, keepdims=True)
    m2 = jnp.max(jnp.where(io_in == f1, neg, sel3), axis=1, keepdims=True)
    gs = (m1 + m2).reshape(N_GROUPS, tm)
    io_g = lax.broadcasted_iota(I32, (N_GROUPS, tm), 0)
    gmask = jnp.zeros((N_GROUPS, tm), jnp.bool_)
    for _ in range(TOPK_GROUPS):
        mg = jnp.max(gs, axis=0, keepdims=True)
        fg = _first_index(gs == mg, io_g, N_GROUPS)
        pick = io_g == fg
        gmask = jnp.logical_or(gmask, pick)
        gs = jnp.where(pick, neg, gs)
    emask = jnp.broadcast_to(gmask.reshape(N_GROUPS, 1, tm), (N_GROUPS, GROUP_SIZE, tm)).reshape(N_EXPERTS, tm)
    cur = jnp.where(emask, sel, neg)
    io_e = lax.broadcasted_iota(I32, (N_EXPERTS, tm), 0)
    picks, idxs, wts = [], [], []
    for _ in range(TOP_K):
        me = jnp.max(cur, axis=0, keepdims=True)
        fe = _first_index(cur == me, io_e, N_EXPERTS)
        pick = io_e == fe
        picks.append(pick)
        idxs.append(fe)
        wts.append(jnp.sum(jnp.where(pick, scores, 0.0), axis=0, keepdims=True))
        cur = jnp.where(pick, neg, cur)
    wsum = wts[0]
    for w in wts[1:]:
        wsum = wsum + w
    zero_i = jnp.zeros((1, tm), I32)
    zero_f = jnp.zeros((1, tm), F32)
    idx_ref[...] = jnp.concatenate(idxs + [zero_i, zero_i], axis=0)
    wts_ref[...] = jnp.concatenate([w / wsum * ROUTED_SCALE for w in wts] + [zero_f, zero_f], axis=0)
    cnt = picks[0].astype(F32)
    for pk in picks[1:]:
        cnt = cnt + pk.astype(F32)
    excl = jnp.dot(cnt.astype(BF16), ustrict_ref[...], preferred_element_type=F32)
    ranks = [jnp.sum(jnp.where(pk, excl, 0.0), axis=0, keepdims=True).astype(I32) for pk in picks]
    rank_ref[...] = jnp.concatenate(ranks + [zero_i, zero_i], axis=0)


def _outproj(x2d, yf, yr, mod4, norm2_g, w_out, router_w, router_bias, swg, swu, swd, T):
    N, D = x2d.shape
    tm = min(512, T)
    tps = T // tm
    vec = lambda j: pl.BlockSpec((None, None, 1, D), lambda i: (i // tps, j, 0, 0))
    full = lambda a: pl.BlockSpec(a.shape, lambda i: (0, 0))
    ts = min(MOE_TILE, tm)
    ustrict = jnp.asarray(np.kron(np.eye(tm // ts, dtype=np.float32),
                                  np.triu(np.ones((ts, ts), np.float32), 1)), BF16)
    rb = jnp.broadcast_to(router_bias.reshape(N_EXPERTS, 1), (N_EXPERTS, LANES))
    consts = [norm2_g.reshape(1, D), w_out.astype(BF16), router_w.T, rb, swg.astype(BF16),
              swu.astype(BF16), swd.astype(BF16), ustrict]
    small = lambda dt: jax.ShapeDtypeStruct((8, N), dt)
    return pl.pallas_call(
        functools.partial(_outproj_kernel, tm=tm),
        grid=(N // tm,),
        in_specs=[pl.BlockSpec((tm, D), lambda i: (i, 0)),
                  pl.BlockSpec((tm, HALF), lambda i: (i, 0)),
                  pl.BlockSpec((tm, HALF), lambda i: (i, 0)),
                  vec(2), vec(3), vec(4), vec(5)] + [full(a) for a in consts],
        out_specs=[pl.BlockSpec((tm, D), lambda i: (i, 0)),
                   pl.BlockSpec((tm, D // 2), lambda i: (i, 0)),
                   pl.BlockSpec((8, tm), lambda i: (0, i)),
                   pl.BlockSpec((8, tm), lambda i: (0, i)),
                   pl.BlockSpec((8, tm), lambda i: (0, i))],
        out_shape=[jax.ShapeDtypeStruct((N, D), F32), jax.ShapeDtypeStruct((N, D // 2), jnp.uint32),
                   small(I32), small(F32), small(I32)],
        compiler_params=_cparams(("arbitrary",)),
        name="outproj",
    )(x2d, yf, yr, mod4, mod4, mod4, mod4, *consts)


def _wait_granules(count, wait_rows):
    full = count // WAIT_CHUNK

    def chunk(_, c):
        wait_rows(WAIT_CHUNK * GRANULE)
        return c

    def single(_, c):
        wait_rows(GRANULE)
        return c

    lax.fori_loop(0, full, chunk, 0)
    lax.fori_loop(full * WAIT_CHUNK, count, single, 0)


def _moe_plan(idx_t, rank_t):
    n = idx_t.shape[1]
    n_tiles = n // MOE_TILE
    experts = jnp.arange(N_EXPERTS, dtype=I32)
    hot = idx_t[:TOP_K, :, None] == experts
    cnt = jnp.sum(hot.reshape(TOP_K, n_tiles, MOE_TILE, N_EXPERTS).astype(I32), axis=(0, 2))
    gran = (cnt + GRANULE - 1) // GRANULE
    loc_end = jnp.cumsum(gran, axis=1)
    loc_off = loc_end - gran
    g_tile = loc_end[:, N_EXPERTS - 1]
    padded = (GRANULE * jnp.sum(gran, axis=0) + EXPERT_BLOCK - 1) // EXPERT_BLOCK * EXPERT_BLOCK
    pad_end = jnp.cumsum(padded)
    glob_off = (pad_end - padded)[None, :] + GRANULE * (jnp.cumsum(gran, axis=0) - gran)
    loc_tok = jnp.repeat(GRANULE * loc_off, MOE_TILE, axis=0)
    pos = rank_t[:TOP_K] + jnp.sum(jnp.where(hot, loc_tok[None], 0), axis=-1)
    pos = jnp.concatenate([pos, jnp.full((8 - TOP_K, n), -1, I32)], axis=0)
    g = jnp.arange(MOE_TILE, dtype=I32)
    e_of_g = jnp.minimum(jnp.sum((loc_end[:, None, :] <= g[None, :, None]).astype(I32), axis=-1), N_EXPERTS - 1)
    dst = jnp.sum(jnp.where(e_of_g[:, :, None] == experts,
                            glob_off[:, None, :] + GRANULE * (g[None, :, None] - loc_off[:, None, :]), 0), axis=-1)
    n_blocks = -(-(n * TOP_K + GRANULE * N_EXPERTS * n_tiles) // EXPERT_BLOCK) + N_EXPERTS
    n_rows = n_blocks * EXPERT_BLOCK
    gtab = jnp.where(g[None, :] < g_tile[:, None], dst, 0).reshape(n_tiles, 1, MOE_TILE)
    blk_start = jnp.arange(n_blocks, dtype=I32) * EXPERT_BLOCK
    blk_e = jnp.minimum(jnp.sum((pad_end[None, :] <= blk_start[:, None]).astype(I32), axis=1), N_EXPERTS - 1)
    plan = dict(pos=pos.astype(I32), gtab=gtab.astype(I32), g_tile=g_tile.astype(I32),
                pad_end=pad_end.astype(I32), padded=padded.astype(I32), blk_e=blk_e.astype(I32),
                n_used=(pad_end[N_EXPERTS - 1:] // EXPERT_BLOCK).astype(I32))
    return plan, n_rows


def _dispatch_kernel(pend_ref, padded_ref, nu_ref, gt_ref, gtab_ref, pos_ref, hp_ref, xs_ref, zeros, sbuf,
                     sem, zsem, *, n_blocks, n_tiles):
    @pl.when(pl.program_id(0) == 0)
    def _():
        zeros[...] = jnp.zeros_like(zeros)

        def block_copy(start):
            return pltpu.make_async_copy(zeros, xs_ref.at[pl.ds(pl.multiple_of(start, EXPERT_BLOCK),
                                                                 EXPERT_BLOCK), :], zsem)

        def pad_loop(fn):
            def body(e, _):
                @pl.when(padded_ref[e] > 0)
                def _():
                    fn(block_copy(pend_ref[e] - EXPERT_BLOCK))
                return 0
            lax.fori_loop(0, N_EXPERTS, body, 0)

        def tail_loop(fn):
            def body(b, _):
                fn(block_copy(b * EXPERT_BLOCK))
                return 0
            lax.fori_loop(nu_ref[0], n_blocks, body, 0)

        pad_loop(lambda cp: cp.start())
        tail_loop(lambda cp: cp.start())
        pad_loop(lambda cp: cp.wait())
        tail_loop(lambda cp: cp.wait())

    i = pl.program_id(0)
    slot = i % 2

    def store(g):
        pltpu.make_async_copy(sbuf.at[slot, pl.ds(pl.multiple_of(g * GRANULE, GRANULE), GRANULE), :],
                              xs_ref.at[pl.ds(pl.multiple_of(gtab_ref[0, 0, g], GRANULE), GRANULE), :],
                              sem.at[slot]).start()

    def wait_tile(which, count):
        def wait_rows(rows):
            pltpu.make_async_copy(sbuf.at[which, pl.ds(0, rows), :], xs_ref.at[pl.ds(0, rows), :],
                                  sem.at[which]).wait()
        wait_rows(MIN_GRANULES * GRANULE)
        _wait_granules(count - MIN_GRANULES, wait_rows)

    @pl.when(i >= 2)
    def _():
        wait_tile(slot, gt_ref[jnp.maximum(i - 2, 0)])

    pos = pos_ref[...]
    lo, hi = _unpack_halves(hp_ref[...])
    lo = lo.astype(BF16)
    hi = hi.astype(BF16)
    r_iota = lax.broadcasted_iota(I32, (MOE_ROW_CHUNK, MOE_TILE), 0).astype(F32).astype(BF16)
    one = jnp.ones((), BF16)
    for c in range(MOE_SORTED_ROWS // MOE_ROW_CHUNK):
        rel = pos - c * MOE_ROW_CHUNK
        rel = jnp.where(jnp.logical_and(rel >= 0, rel < MOE_ROW_CHUNK), rel, -1).astype(F32)
        perm = jnp.zeros((MOE_ROW_CHUNK, MOE_TILE), BF16)
        for k in range(TOP_K):
            perm = jnp.where(r_iota == rel[k:k + 1, :].astype(BF16), one, perm)
        sbuf[slot, c * MOE_ROW_CHUNK:(c + 1) * MOE_ROW_CHUNK, :] = _pack_bf16_values(
            jnp.dot(perm, lo, preferred_element_type=F32), jnp.dot(perm, hi, preferred_element_type=F32))
        g_lo, g_hi = c * MOE_ROW_CHUNK // GRANULE, (c + 1) * MOE_ROW_CHUNK // GRANULE
        for g in range(g_lo, min(g_hi, MIN_GRANULES)):
            store(g)
        if g_hi > MIN_GRANULES:
            def body(g, carry):
                store(g)
                return carry
            lax.fori_loop(max(g_lo, MIN_GRANULES), jnp.clip(gt_ref[i], max(g_lo, MIN_GRANULES), g_hi), body, 0)

    @pl.when(i == n_tiles - 1)
    def _():
        wait_tile(slot, gt_ref[i])
        if n_tiles > 1:
            wait_tile(1 - slot, gt_ref[jnp.maximum(i - 1, 0)])


def _dispatch(hp, plan, n_rows):
    N, W = hp.shape
    n_tiles = N // MOE_TILE
    n_blocks = n_rows // EXPERT_BLOCK
    return pl.pallas_call(
        functools.partial(_dispatch_kernel, n_blocks=n_blocks, n_tiles=n_tiles),
        grid_spec=pltpu.PrefetchScalarGridSpec(
            num_scalar_prefetch=4, grid=(n_tiles,),
            in_specs=[pl.BlockSpec((1, 1, MOE_TILE), lambda i, *_: (i, 0, 0), memory_space=pltpu.SMEM),
                      pl.BlockSpec((8, MOE_TILE), lambda i, *_: (0, i)),
                      pl.BlockSpec((MOE_TILE, W), lambda i, *_: (i, 0))],
            out_specs=pl.BlockSpec(memory_space=pl.ANY),
            scratch_shapes=[pltpu.VMEM((EXPERT_BLOCK, W), jnp.uint32),
                            pltpu.VMEM((2, MOE_SORTED_ROWS, W), jnp.uint32),
                            pltpu.SemaphoreType.DMA((2,)), pltpu.SemaphoreType.DMA]),
        out_shape=jax.ShapeDtypeStruct((n_rows, W), jnp.uint32),
        compiler_params=_cparams(("arbitrary",)),
        name="dispatch",
    )(plan["pad_end"], plan["padded"], plan["n_used"], plan["g_tile"], plan["gtab"], plan["pos"], hp)


def _experts_kernel(be_ref, nu_ref, xs_ref, wgf_ref, wuf_ref, wdf_ref, ys_ref, wg_ref, wu_ref, wd_ref):
    i = pl.program_id(0)
    live = i < nu_ref[0]
    new_expert = jnp.logical_or(i == 0, be_ref[i] != be_ref[jnp.maximum(i - 1, 0)])

    @pl.when(jnp.logical_not(live))
    def _():
        ys_ref[...] = jnp.zeros_like(ys_ref)

    @pl.when(jnp.logical_and(live, new_expert))
    def _():
        wg_ref[...] = wgf_ref[...].astype(BF16)
        wu_ref[...] = wuf_ref[...].astype(BF16)
        wd_ref[...] = wdf_ref[...].astype(BF16)

    @pl.when(live)
    def _():
        half = D_MODEL // 2
        lo, hi = _unpack_halves(xs_ref[...])
        lo = lo.astype(BF16)
        hi = hi.astype(BF16)
        gate = (jnp.dot(lo, wg_ref[0:half, :], preferred_element_type=F32)
                + jnp.dot(hi, wg_ref[half:D_MODEL, :], preferred_element_type=F32))
        up = (jnp.dot(lo, wu_ref[0:half, :], preferred_element_type=F32)
              + jnp.dot(hi, wu_ref[half:D_MODEL, :], preferred_element_type=F32))
        y = jnp.dot((_silu(gate) * up).astype(BF16), wd_ref[...], preferred_element_type=F32)
        ys_ref[...] = _pack_halves(y)


def _experts(xs, n_rows, blk_e, n_used, wg, wu, wd):
    W = xs.shape[1]
    n_blocks = n_rows // EXPERT_BLOCK
    row_map = lambda i, be, nu: (jnp.minimum(i, nu[0] - 1), 0)
    return pl.pallas_call(
        _experts_kernel,
        grid_spec=pltpu.PrefetchScalarGridSpec(
            num_scalar_prefetch=2, grid=(n_blocks,),
            in_specs=[pl.BlockSpec((EXPERT_BLOCK, W), row_map),
                      pl.BlockSpec((None, D_MODEL, D_EXPERT), lambda i, be, nu: (be[i], 0, 0)),
                      pl.BlockSpec((None, D_MODEL, D_EXPERT), lambda i, be, nu: (be[i], 0, 0)),
                      pl.BlockSpec((None, D_EXPERT, D_MODEL), lambda i, be, nu: (be[i], 0, 0))],
            out_specs=pl.BlockSpec((EXPERT_BLOCK, W), lambda i, be, nu: (i, 0)),
            scratch_shapes=[pltpu.VMEM((D_MODEL, D_EXPERT), BF16), pltpu.VMEM((D_MODEL, D_EXPERT), BF16),
                            pltpu.VMEM((D_EXPERT, D_MODEL), BF16)]),
        out_shape=jax.ShapeDtypeStruct((n_rows, W), jnp.uint32),
        compiler_params=_cparams(("arbitrary",)),
        name="experts",
    )(blk_e, n_used, xs, wg, wu, wd)


def _combine_kernel(gt_ref, gtab_ref, gtab_next_ref, ys_ref, pos_ref, xs_ref, wts_ref, g2_ref, fg_ref, o_ref,
                    buf, sem, *, n_tiles):
    i = pl.program_id(0)
    slot = i % 2
    nxt = jnp.minimum(i + 1, n_tiles - 1)

    def fetch(which, table_ref, g):
        pltpu.make_async_copy(ys_ref.at[pl.ds(pl.multiple_of(table_ref[0, 0, g], GRANULE), GRANULE), :],
                              buf.at[which, pl.ds(pl.multiple_of(g * GRANULE, GRANULE), GRANULE), :],
                              sem.at[which]).start()

    def fetch_rest(which, table_ref, count):
        def body(g, c):
            fetch(which, table_ref, g)
            return c
        lax.fori_loop(MIN_GRANULES, count, body, 0)

    def wait_tile(which, count):
        def wait_rows(rows):
            pltpu.make_async_copy(ys_ref.at[pl.ds(0, rows), :], buf.at[which, pl.ds(0, rows), :],
                                  sem.at[which]).wait()
        wait_rows(MIN_GRANULES * GRANULE)
        _wait_granules(count - MIN_GRANULES, wait_rows)

    @pl.when(i == 0)
    def _():
        buf[...] = jnp.zeros_like(buf)
        for g in range(MIN_GRANULES):
            fetch(0, gtab_ref, g)
        fetch_rest(0, gtab_ref, gt_ref[0])

    wait_tile(slot, gt_ref[i])

    pos = pos_ref[...]
    w = wts_ref[...]
    c_iota = lax.broadcasted_iota(I32, (MOE_TILE, MOE_ROW_CHUNK), 1).astype(F32).astype(BF16)
    half = D_MODEL // 2
    r_lo = jnp.zeros((MOE_TILE, half), F32)
    r_hi = jnp.zeros((MOE_TILE, half), F32)
    for c in range(MOE_SORTED_ROWS // MOE_ROW_CHUNK):
        rel = pos - c * MOE_ROW_CHUNK
        rel = jnp.where(jnp.logical_and(rel >= 0, rel < MOE_ROW_CHUNK), rel, -1).astype(F32)
        wb = jnp.zeros((MOE_TILE, MOE_ROW_CHUNK), BF16)
        for k in range(TOP_K):
            wb = jnp.where(c_iota == rel[:, k:k + 1].astype(BF16), w[:, k:k + 1].astype(BF16), wb)
        lo, hi = _unpack_halves(buf[slot, c * MOE_ROW_CHUNK:(c + 1) * MOE_ROW_CHUNK, :])
        r_lo = r_lo + jnp.dot(wb, lo.astype(BF16), preferred_element_type=F32)
        r_hi = r_hi + jnp.dot(wb, hi.astype(BF16), preferred_element_type=F32)
        for g in range(min(c * FETCH_PER_CHUNK, MIN_GRANULES), min((c + 1) * FETCH_PER_CHUNK, MIN_GRANULES)):
            fetch(1 - slot, gtab_next_ref, g)
        if c == MIN_GRANULES // FETCH_PER_CHUNK:
            fetch_rest(1 - slot, gtab_next_ref, gt_ref[nxt])
    x3 = xs_ref[...] + g2_ref[...] * jnp.concatenate([r_lo, r_hi], axis=1)
    y = x3 * lax.rsqrt(jnp.mean(x3 * x3, axis=-1, keepdims=True) + RMS_EPS) * fg_ref[...]
    o_ref[...] = y

    @pl.when(i == n_tiles - 1)
    def _():
        wait_tile(1 - slot, gt_ref[nxt])


def _combine(ys, plan, xsr, wts_t, mod4, final_g, T):
    N, D = xsr.shape
    W = ys.shape[1]
    tm = MOE_TILE
    tps = T // tm
    n_tiles = N // tm
    tile_tab = lambda off: pl.BlockSpec((1, 1, MOE_TILE), lambda i, gt: (jnp.minimum(i + off, n_tiles - 1), 0, 0),
                                        memory_space=pltpu.SMEM)
    return pl.pallas_call(
        functools.partial(_combine_kernel, n_tiles=n_tiles),
        grid_spec=pltpu.PrefetchScalarGridSpec(
            num_scalar_prefetch=1, grid=(n_tiles,),
            in_specs=[tile_tab(0), tile_tab(1),
                      pl.BlockSpec(memory_space=pl.ANY),
                      pl.BlockSpec((tm, 8), lambda i, gt: (i, 0)),
                      pl.BlockSpec((tm, D), lambda i, gt: (i, 0)),
                      pl.BlockSpec((tm, 8), lambda i, gt: (i, 0)),
                      pl.BlockSpec((None, None, 1, D), lambda i, gt: (i // tps, 5, 0, 0)),
                      pl.BlockSpec((1, D), lambda i, gt: (0, 0))],
            out_specs=pl.BlockSpec((tm, D), lambda i, gt: (i, 0)),
            scratch_shapes=[pltpu.VMEM((2, MOE_SORTED_ROWS, W), jnp.uint32), pltpu.SemaphoreType.DMA((2,))]),
        out_shape=jax.ShapeDtypeStruct((N, D), F32),
        compiler_params=_cparams(("arbitrary",)),
        name="combine",
    )(plan["g_tile"], plan["gtab"], plan["gtab"], ys, plan["pos"].T, xsr, wts_t, mod4,
      final_g.reshape(1, D))


def _pack_w_in(w_in):
    D = w_in.shape[0]
    fox_cols = 4 * HALF + 3 * N_HEADS
    wf = w_in[:, :fox_cols]
    wr = w_in[:, fox_cols:]
    o = 3 * HALF
    pad = lambda a, n: jnp.concatenate([a, jnp.zeros((D, n - a.shape[1]), a.dtype)], axis=1)
    parts = [wf[:, :4 * HALF], wr[:, :o],
             pad(wf[:, 4 * HALF:], LANES),
             pad(wr[:, o:o + DECAY_LORA], LANES),
             pad(wr[:, o + DECAY_LORA:o + DECAY_LORA + ICLR_LORA], LANES),
             pad(wr[:, o + DECAY_LORA + ICLR_LORA:], 2 * LANES)]
    return jnp.concatenate(parts, axis=1).astype(BF16)


def _pack_mu(mu):
    o = 3 * HALF
    pad = lambda a, n: jnp.concatenate([a, jnp.zeros((n - a.shape[0],), a.dtype)])
    small = jnp.concatenate([jnp.zeros((LANES,), mu.dtype),
                             pad(mu[o:o + DECAY_LORA], LANES),
                             pad(mu[o + DECAY_LORA:o + DECAY_LORA + ICLR_LORA], LANES),
                             pad(mu[o + DECAY_LORA + ICLR_LORA:], 2 * LANES)])
    return mu[:o].reshape(1, o), small.reshape(1, Z_SMALL)


def kernel(x, c, norm1_g, norm2_g, ada_w, ada_b, w_in, w_out, fox_qn_g, fox_kn_g, fox_on_g, fox_forget_b,
           rw_mu, rw_w0, rw_decay_up, rw_a0, rw_iclr_up, rw_gate_up, rw_k_k, rw_k_a, rw_r_k, rw_lnx_g,
           rw_lnx_b, router_w, router_bias, exp_w_gate, exp_w_up, exp_w_down, sh_w_gate, sh_w_up,
           sh_w_down, final_g):
    B, T, D = x.shape
    N = B * T
    depth = norm1_g.shape[0]
    assert depth == 1, "the combine kernel fuses the final RMSNorm, so exactly one layer is supported"
    xf = x.reshape(N, D)
    for l in range(depth):
        mod4 = _ada(c, ada_w[l], ada_b[l]).reshape(B, 6, 1, D)
        mu_big, mu_small = _pack_mu(rw_mu[l])
        zm, zs = _inproj(xf, mod4, norm1_g[l], _pack_w_in(w_in[l]), mu_big, mu_small, T)
        qp, kp, vp = _foxprep(zm, zs, fox_forget_b[l], fox_qn_g[l], fox_kn_g[l], T)
        y_fox = _attention(qp, kp, vp, zm, fox_on_g[l], B, T)
        rw = dict(rw_w0=rw_w0[l], rw_decay_up=rw_decay_up[l], rw_a0=rw_a0[l], rw_iclr_up=rw_iclr_up[l],
                  rw_gate_up=rw_gate_up[l], rw_k_k=rw_k_k[l], rw_k_a=rw_k_a[l], rw_r_k=rw_r_k[l],
                  rw_lnx_g=rw_lnx_g[l], rw_lnx_b=rw_lnx_b[l])
        y_rwkv = _rwkv(zm, zs, rw, B, T)
        xsr, hp, idx_t, wts_t, rank_t = _outproj(
            xf, y_fox, y_rwkv, mod4, norm2_g[l], w_out[l], router_w[l], router_bias[l],
            sh_w_gate[l], sh_w_up[l], sh_w_down[l], T)
        plan, n_rows = _moe_plan(idx_t, rank_t)
        xs = _dispatch(hp, plan, n_rows)
        ys = _experts(xs, n_rows, plan["blk_e"], plan["n_used"], exp_w_gate[l], exp_w_up[l], exp_w_down[l])
        xf = _combine(ys, plan, xsr, wts_t.T, mod4, final_g, T)
    return xf.reshape(B, T, D)
```

```python
import functools

import jax
import jax.numpy as jnp
import numpy as np
from jax import lax
from jax.experimental import pallas as pl
from jax.experimental.pallas import tpu as pltpu

F32 = jnp.float32
BF16 = jnp.bfloat16
I32 = jnp.int32
HIGHEST = lax.Precision.HIGHEST

D_MODEL = 1024
HEAD_DIM = 64
N_HEADS = 8
HALF = N_HEADS * HEAD_DIM
RMS_EPS = 1e-6
LNX_EPS = 64e-5
LOG2E = 1.4426950408889634
DECAY_LORA = 64
ICLR_LORA = 64
N_EXPERTS = 64
N_GROUPS = 8
GROUP_SIZE = N_EXPERTS // N_GROUPS
TOPK_GROUPS = 4
TOP_K = 6
D_EXPERT = 256
ROUTED_SCALE = 2.5
EXPERT_BLOCK = 1024
EXPERT_RING = 3

LANES = 128
Z_MAIN = 4 * HALF + 3 * HALF
Z_SMALL = 5 * LANES
VMEM_LIMIT = 56 * 1024 * 1024
ATTN_BLOCK = 512
ATTN_HEADS = 4
WAIT_CHUNK = 16
MOE_TILE = 256
GRANULE = 8
MOE_SORTED_ROWS = MOE_TILE * TOP_K + N_EXPERTS * GRANULE
MIN_GRANULES = MOE_TILE * TOP_K // GRANULE
FETCH_PER_CHUNK = 96
MOE_ROW_CHUNK = 256
RWKV_CHUNK = 64
RWKV_SEQS_PER_STEP = 4


def _cparams(semantics):
    return pltpu.CompilerParams(dimension_semantics=semantics, vmem_limit_bytes=VMEM_LIMIT)


def _mm(a, b):
    return jnp.dot(a.astype(BF16), b.astype(BF16), preferred_element_type=F32)


def _mm_nt(a, b):
    return lax.dot_general(a.astype(BF16), b.astype(BF16), (((1,), (1,)), ((), ())),
                           preferred_element_type=F32)


def _mm_f32(a, b):
    return jnp.dot(a, b, precision=HIGHEST, preferred_element_type=F32)


def _bf16_pieces(x, passes):
    pieces = []
    for _ in range(passes):
        piece = x.astype(BF16)
        pieces.append(piece)
        x = x - piece.astype(F32)
    return pieces


def _mm_split(m01, x, passes=3):
    return sum(jnp.dot(m01, p, preferred_element_type=F32) for p in _bf16_pieces(x, passes))


def _mm_split_r(x, m01, passes=2):
    return sum(jnp.dot(p, m01, preferred_element_type=F32) for p in _bf16_pieces(x, passes))


def _sigmoid(x):
    return 1.0 / (1.0 + jnp.exp(-x))


def _softplus(x):
    return jnp.maximum(x, 0.0) + jnp.log(1.0 + jnp.exp(-jnp.abs(x)))


def _silu(x):
    return x * _sigmoid(x)


def _pack_halves(x):
    w = x.shape[1] // 2
    bits = lambda t: lax.bitcast_convert_type(t.astype(jnp.bfloat16).astype(F32), jnp.uint32)
    return (bits(x[:, 0:w]) >> 16) | (bits(x[:, w:2 * w]) & jnp.uint32(0xFFFF0000))


def _pack_bf16_values(lo, hi):
    return (lax.bitcast_convert_type(lo, jnp.uint32) >> 16) | lax.bitcast_convert_type(hi, jnp.uint32)


def _unpack_halves(p):
    lo = lax.bitcast_convert_type(p << 16, F32)
    hi = lax.bitcast_convert_type(p & jnp.uint32(0xFFFF0000), F32)
    return lo, hi


def _shift_rows(z, carry_ref, first):
    rows = z.shape[0]
    prev_row = jnp.where(first, 0.0, carry_ref[0:1, :])
    prev = pltpu.roll(z, 1, 0)
    row0 = lax.broadcasted_iota(I32, (rows, 1), 0) == 0
    prev = jnp.where(row0, prev_row, prev)
    carry_ref[0:1, :] = z[rows - 1:rows, :]
    return prev


def _ada_kernel(c_ref, w_ref, b_ref, o_ref):
    o_ref[...] = _mm_f32(_silu(c_ref[...]), w_ref[...]) + b_ref[...]


def _ada(c, ada_w, ada_b):
    B, D = c.shape
    n_out = ada_w.shape[1]
    tn = 512
    return pl.pallas_call(
        _ada_kernel,
        grid=(n_out // tn,),
        in_specs=[pl.BlockSpec((B, D), lambda j: (0, 0)),
                  pl.BlockSpec((D, tn), lambda j: (0, j)),
                  pl.BlockSpec((1, tn), lambda j: (0, j))],
        out_specs=pl.BlockSpec((B, tn), lambda j: (0, j)),
        out_shape=jax.ShapeDtypeStruct((B, n_out), F32),
        compiler_params=_cparams(("arbitrary",)),
        name="ada",
    )(c, ada_w, ada_b.reshape(1, n_out))


def _inproj_kernel(x_ref, g_ref, sh_ref, sc_ref, w_ref, mub_ref, mus_ref, zm_ref, zs_ref,
                   carry_b, carry_s, *, tiles_per_seq):
    first = (pl.program_id(0) % tiles_per_seq) == 0
    x = x_ref[...]
    h = x * lax.rsqrt(jnp.mean(x * x, axis=-1, keepdims=True) + RMS_EPS) * g_ref[...]
    hb = (h * (1.0 + sc_ref[...]) + sh_ref[...]).astype(BF16)
    nf = 4 * HALF
    zm_ref[:, 0:nf] = jnp.dot(hb, w_ref[:, 0:nf], preferred_element_type=F32).astype(BF16)
    zr = jnp.dot(hb, w_ref[:, nf:Z_MAIN], preferred_element_type=F32)
    zr = zr + mub_ref[...] * (_shift_rows(zr, carry_b, first) - zr)
    zm_ref[:, nf:Z_MAIN] = zr.astype(BF16)
    zs = jnp.dot(hb, w_ref[:, Z_MAIN:Z_MAIN + Z_SMALL], preferred_element_type=F32)
    zs_ref[...] = zs + mus_ref[...] * (_shift_rows(zs, carry_s, first) - zs)


def _inproj(x2d, mod4, norm_g, w_all, mu_big, mu_small, T):
    N, D = x2d.shape
    tm = min(512, T)
    tps = T // tm
    vec = lambda j: pl.BlockSpec((None, None, 1, D), lambda i: (i // tps, j, 0, 0))
    return pl.pallas_call(
        functools.partial(_inproj_kernel, tiles_per_seq=tps),
        grid=(N // tm,),
        in_specs=[pl.BlockSpec((tm, D), lambda i: (i, 0)),
                  pl.BlockSpec((1, D), lambda i: (0, 0)),
                  vec(0), vec(1),
                  pl.BlockSpec((D, Z_MAIN + Z_SMALL), lambda i: (0, 0)),
                  pl.BlockSpec((1, 3 * HALF), lambda i: (0, 0)),
                  pl.BlockSpec((1, Z_SMALL), lambda i: (0, 0))],
        out_specs=[pl.BlockSpec((tm, Z_MAIN), lambda i: (i, 0)),
                   pl.BlockSpec((tm, Z_SMALL), lambda i: (i, 0))],
        out_shape=[jax.ShapeDtypeStruct((N, Z_MAIN), BF16),
                   jax.ShapeDtypeStruct((N, Z_SMALL), F32)],
        scratch_shapes=[pltpu.VMEM((8, 3 * HALF), F32), pltpu.VMEM((8, Z_SMALL), F32)],
        compiler_params=_cparams(("arbitrary",)),
        name="inproj",
    )(x2d, norm_g.reshape(1, D), mod4, mod4, w_all, mu_big, mu_small)


def _foxprep_kernel(q_ref, k_ref, v_ref, zs_ref, fb_ref, qg_ref, kg_ref, tri_ref, eexp_ref,
                    esum_ref, e8_ref, plq_ref, plk_ref, plv_ref, cq_ref, ck_ref, cv_ref,
                    qp_ref, kp_ref, vp_ref, carry_k, carry_v, carry_c, *, tiles_per_seq):
    first = (pl.program_id(0) % tiles_per_seq) == 0
    zs = zs_ref[...]
    logf = -_softplus(-(zs + fb_ref[...]))
    cum = _mm_split(tri_ref[...], logf) + jnp.where(first, 0.0, carry_c[0:1, :])
    carry_c[0:1, :] = cum[cum.shape[0] - 1:, :]
    cum2 = cum * LOG2E
    head_lane = lax.broadcasted_iota(I32, cum2.shape, 1) < N_HEADS
    c_hi, c_mid, c_lo = [jnp.where(head_lane, p.astype(F32), 0.0) for p in _bf16_pieces(cum2, 3)]
    aug = (c_hi + pltpu.roll(c_mid, N_HEADS, 1) + pltpu.roll(c_lo, 2 * N_HEADS, 1)).astype(BF16)
    a_full = _mm_split_r(_sigmoid(zs), eexp_ref[...])
    k = k_ref[...].astype(F32)
    v = v_ref[...].astype(F32)
    a_k = a_full[:, 0:HALF]
    a_v = a_full[:, HALF:2 * HALF]
    k = a_k * _shift_rows(k, carry_k, first) + (1.0 - a_k) * k
    v = a_v * _shift_rows(v, carry_v, first) + (1.0 - a_v) * v
    q = q_ref[...].astype(F32)

    def head_rms(t, gain):
        ms = _mm_split_r(t * t, esum_ref[...]) * (1.0 / HEAD_DIM)
        inv = _mm_split_r(lax.rsqrt(ms + RMS_EPS), e8_ref[...])
        return t * inv * gain

    qn = head_rms(q, qg_ref[...]) * (HEAD_DIM ** -0.5 * LOG2E)
    kn = head_rms(k, kg_ref[...])
    lhs_q = jnp.concatenate([qn.astype(BF16), aug], axis=1)
    lhs_k = jnp.concatenate([kn.astype(BF16), aug], axis=1)
    qp_ref[...] = (jnp.dot(lhs_q, plq_ref[...], preferred_element_type=F32) + cq_ref[...]).astype(BF16)
    kp_ref[...] = (jnp.dot(lhs_k, plk_ref[...], preferred_element_type=F32) + ck_ref[...]).astype(BF16)
    vp_ref[...] = (jnp.dot(v.astype(BF16), plv_ref[...], preferred_element_type=F32)
                   + cv_ref[...]).astype(BF16)


def _fox_constants(tm):
    hp = N_HEADS * LANES
    eexp = np.zeros((LANES, 2 * HALF), np.float32)
    esum = np.zeros((HALF, LANES), np.float32)
    e8 = np.zeros((LANES, HALF), np.float32)
    plq = np.zeros((HALF + LANES, hp), np.float32)
    plk = np.zeros((HALF + LANES, hp), np.float32)
    plv = np.zeros((HALF, hp), np.float32)
    cq = np.zeros((1, hp), np.float32)
    ck = np.zeros((1, hp), np.float32)
    cv = np.zeros((1, hp), np.float32)
    for h in range(N_HEADS):
        sl = slice(h * HEAD_DIM, (h + 1) * HEAD_DIM)
        eexp[8 + h, sl] = 1.0
        eexp[16 + h, HALF + h * HEAD_DIM:HALF + (h + 1) * HEAD_DIM] = 1.0
        esum[sl, h] = 1.0
        e8[h, sl] = 1.0
        base = h * LANES
        for d in range(HEAD_DIM):
            plq[h * HEAD_DIM + d, base + d] = 1.0
            plk[h * HEAD_DIM + d, base + d] = 1.0
            plv[h * HEAD_DIM + d, base + d] = 1.0
        for j in range(3):
            plq[HALF + j * N_HEADS + h, base + HEAD_DIM + j] = 1.0
            plk[HALF + j * N_HEADS + h, base + HEAD_DIM + 3 + j] = -1.0
            cq[0, base + HEAD_DIM + 3 + j] = 1.0
            ck[0, base + HEAD_DIM + j] = 1.0
        cv[0, base + HEAD_DIM] = 1.0
    tri = np.tril(np.ones((tm, tm), np.float32))
    bf = lambda a: jnp.asarray(a, BF16)
    return dict(tri=bf(tri), eexp=bf(eexp), esum=bf(esum), e8=bf(e8),
                plq=bf(plq), plk=bf(plk), plv=bf(plv), cq=jnp.asarray(cq), ck=jnp.asarray(ck),
                cv=jnp.asarray(cv))


def _foxprep(zm, zs, forget_b, qn_g, kn_g, T):
    N = zm.shape[0]
    tm = min(512, T)
    tps = T // tm
    cst = _fox_constants(tm)
    hp = N_HEADS * LANES
    fb = jnp.zeros((1, LANES), F32).at[0, :N_HEADS].set(forget_b)
    full = lambda a: pl.BlockSpec(a.shape, lambda i: (0, 0))
    consts = [fb, qn_g.reshape(1, HALF), kn_g.reshape(1, HALF), cst["tri"], cst["eexp"], cst["esum"],
              cst["e8"], cst["plq"], cst["plk"], cst["plv"], cst["cq"], cst["ck"], cst["cv"]]
    return pl.pallas_call(
        functools.partial(_foxprep_kernel, tiles_per_seq=tps),
        grid=(N // tm,),
        in_specs=[pl.BlockSpec((tm, HALF), lambda i: (i, 0)),
                  pl.BlockSpec((tm, HALF), lambda i: (i, 1)),
                  pl.BlockSpec((tm, HALF), lambda i: (i, 2)),
                  pl.BlockSpec((tm, LANES), lambda i: (i, 0))] + [full(a) for a in consts],
        out_specs=[pl.BlockSpec((tm, hp), lambda i: (i, 0))] * 3,
        out_shape=[jax.ShapeDtypeStruct((N, hp), BF16)] * 3,
        scratch_shapes=[pltpu.VMEM((8, HALF), F32), pltpu.VMEM((8, HALF), F32),
                        pltpu.VMEM((8, LANES), F32)],
        compiler_params=_cparams(("arbitrary",)),
        name="foxprep",
    )(zm, zm, zm, zs, *consts)


def _attn_kernel(q_ref, k_ref, v_ref, g_ref, ong_ref, o_ref, vt_ref, *, tq):
    i = pl.program_id(2)
    n_kv = vt_ref.shape[1]
    heads = range(ATTN_HEADS)
    lanes = [slice(hh * LANES, (hh + 1) * LANES) for hh in heads]

    @pl.when(i == 0)
    def _():
        for hh in heads:
            for c in range(n_kv):
                vt_ref[hh, c] = v_ref[c * tq:(c + 1) * tq, lanes[hh]].T

    key = lax.broadcasted_iota(I32, (tq, tq), 0)
    qry = lax.broadcasted_iota(I32, (tq, tq), 1)
    causal = key <= qry
    qs = [q_ref[:, lanes[hh]] for hh in heads]

    def step(j, carry, masked):
        m, acc = carry
        start = pl.multiple_of(j * tq, tq)
        s = [lax.dot_general(k_ref[pl.ds(start, tq), lanes[hh]], qs[hh], (((1,), (1,)), ((), ())),
                             preferred_element_type=F32) for hh in heads]
        m_out, acc_out = [], []
        for hh in heads:
            sh = jnp.where(causal, s[hh], -jnp.inf) if masked else s[hh]
            m_new = jnp.maximum(m[hh], jnp.max(sh, axis=0, keepdims=True))
            p = jnp.exp2(sh - m_new)
            acc_out.append(jnp.exp2(m[hh] - m_new) * acc[hh]
                           + jnp.dot(vt_ref[hh, j], p.astype(BF16), preferred_element_type=F32))
            m_out.append(m_new)
        return tuple(m_out), tuple(acc_out)

    init = (tuple(jnp.full((1, tq), -jnp.inf, F32) for _ in heads),
            tuple(jnp.zeros((LANES, tq), F32) for _ in heads))
    carry = lax.fori_loop(0, i, functools.partial(step, masked=False), init)
    _, acc = step(i, carry, True)
    chan = lax.broadcasted_iota(I32, (LANES, tq), 0)
    chan_w = jnp.where(chan < HEAD_DIM, 1.0 / HEAD_DIM, jnp.where(chan == HEAD_DIM, RMS_EPS, 0.0))
    lane = lax.broadcasted_iota(I32, (tq, LANES), 1)
    outs = []
    for hh in heads:
        t = jnp.sum(acc[hh] * acc[hh] * chan_w, axis=0, keepdims=True)
        outs.append((acc[hh] * lax.rsqrt(t)).T)
    o = jnp.concatenate([jnp.where(lane < HEAD_DIM, outs[2 * p], pltpu.roll(outs[2 * p + 1], HEAD_DIM, 1))
                         for p in range(ATTN_HEADS // 2)], axis=1)
    y = o * ong_ref[...] * _sigmoid(g_ref[...].astype(F32))
    o_ref[...] = y.astype(BF16)


def _attention(qp, kp, vp, zm, on_g, B, T):
    N = qp.shape[0]
    tq = min(ATTN_BLOCK, T)
    nq = T // tq
    groups = N_HEADS // ATTN_HEADS
    wp = ATTN_HEADS * LANES
    wo = ATTN_HEADS * HEAD_DIM
    g_col0 = 3 * HALF // wo
    return pl.pallas_call(
        functools.partial(_attn_kernel, tq=tq),
        grid=(B, groups, nq),
        in_specs=[pl.BlockSpec((tq, wp), lambda b, p, i: (b * nq + i, p)),
                  pl.BlockSpec((T, wp), lambda b, p, i: (b, p)),
                  pl.BlockSpec((T, wp), lambda b, p, i: (b, p)),
                  pl.BlockSpec((tq, wo), lambda b, p, i: (b * nq + i, g_col0 + p)),
                  pl.BlockSpec((None, 1, wo), lambda b, p, i: (p, 0, 0))],
        out_specs=pl.BlockSpec((tq, wo), lambda b, p, i: (b * nq + i, p)),
        out_shape=jax.ShapeDtypeStruct((N, HALF), BF16),
        scratch_shapes=[pltpu.VMEM((ATTN_HEADS, nq, LANES, tq), BF16)],
        compiler_params=_cparams(("arbitrary", "arbitrary", "arbitrary")),
        name="attn",
    )(qp, kp, vp, zm, on_g.reshape(groups, 1, wo))


def _rwkv_kernel(r_ref, k_ref, v_ref, zs_ref, w0_ref, dup_ref, a0_ref, iup_ref, gup_ref, kk_ref,
                 ka_ref, rk_ref, lng_ref, lnb_ref, tri_ref, o_ref, state, ybuf, *, chunk, nb):
    C = chunk
    c_idx = pl.program_id(1)

    @pl.when(c_idx == 0)
    def _():
        state[...] = jnp.zeros_like(state)

    R = nb * C
    r = r_ref[...].reshape(R, HALF).astype(F32)
    k = k_ref[...].reshape(R, HALF).astype(F32)
    v = v_ref[...].reshape(R, HALF).astype(F32)
    zs = zs_ref[...].reshape(R, Z_SMALL)
    wd = zs[:, LANES:2 * LANES]
    ad = zs[:, 2 * LANES:3 * LANES]
    gd = zs[:, 3 * LANES:5 * LANES]
    wl = w0_ref[...] + _mm(jnp.tanh(wd), dup_ref[...])
    lw = -jnp.exp(-_softplus(-wl) - 0.5)
    a = _sigmoid(a0_ref[...] + _mm(ad, iup_ref[...]))
    g = _mm(_sigmoid(gd), gup_ref[...])
    kk = k * kk_ref[...]
    k2 = k * (1.0 + (a - 1.0) * ka_ref[...])
    cl = _mm_split(tri_ref[...], lw)
    cl_end = jnp.concatenate(
        [jnp.broadcast_to(cl[(bi + 1) * C - 1:(bi + 1) * C, :], (C, HALF)) for bi in range(nb)], axis=0)
    e_neg = jnp.exp(-cl)
    e_tail = jnp.exp(cl_end - cl)
    pre = dict(rt=r * jnp.exp(cl), at=-kk * jnp.exp(cl - lw), kh=k2 * e_neg, bh=kk * a * e_neg,
               kb=k2 * e_tail, bb=kk * a * e_tail, v=v, kk=kk, pend=jnp.exp(cl_end),
               rkr=r * k2 * rk_ref[...], g=g)

    C2 = 2 * C
    row = lax.broadcasted_iota(I32, (C2, C2), 0)
    col = lax.broadcasted_iota(I32, (C2, C2), 1)
    lower = (col & (C - 1)) <= (row & (C - 1))
    strict = (col & (C - 1)) < (row & (C - 1))
    eye = row == col
    head0 = lax.broadcasted_iota(I32, (C, LANES), 1) < HEAD_DIM
    n_sq = int(np.log2(C)) - 1
    units = [(bi, p) for bi in range(nb) for p in range(N_HEADS // 2)]

    def part(name, bi, p):
        return pre[name][bi * C:(bi + 1) * C, p * LANES:(p + 1) * LANES]

    def stack(x):
        return jnp.concatenate([jnp.where(head0, x, 0.0), jnp.where(head0, 0.0, x)], axis=0)

    def unstack(x):
        return x[0:C] + x[C:C2]

    def head_sum(x):
        s0 = jnp.sum(jnp.where(head0, x, 0.0), axis=1, keepdims=True)
        s1 = jnp.sum(jnp.where(head0, 0.0, x), axis=1, keepdims=True)
        return jnp.where(head0, s0, s1)

    ops = []
    for bi, p in units:
        kk_p = part("kk", bi, p)
        inv = 1.0 / jnp.maximum(jnp.sqrt(head_sum(kk_p * kk_p)), 1e-12)
        ops.append(dict(rt=stack(part("rt", bi, p)), at=stack(part("at", bi, p) * inv),
                        kh=stack(part("kh", bi, p)).astype(BF16), bh=stack(part("bh", bi, p) * inv).astype(BF16),
                        kb=stack(part("kb", bi, p)).astype(BF16), bb=stack(part("bb", bi, p) * inv).astype(BF16),
                        v=stack(part("v", bi, p)).astype(BF16)))
    ra = [jnp.concatenate([o["rt"], o["at"]], axis=0).astype(BF16) for o in ops]
    gg = [_mm_nt(x, jnp.concatenate([o["kh"], o["bh"]], axis=0)) for x, o in zip(ra, ops)]
    g1 = [t[:, 0:C2] for t in gg]
    g2 = [t[:, C2:2 * C2] for t in gg]
    a_rb = [jnp.where(lower, t[0:C2], 0.0).astype(BF16) for t in g2]
    pw = [jnp.where(strict, t[C2:2 * C2], 0.0) for t in g2]
    av = [_mm(jnp.concatenate([jnp.where(lower, t[0:C2], 0.0).astype(BF16),
                               jnp.where(strict, t[C2:2 * C2], 0.0).astype(BF16), o["kb"].T], axis=0),
              o["v"]) for t, o in zip(g1, ops)]
    swap = lambda t: pltpu.roll(t, HEAD_DIM, 1)
    own = (row < C) == (col < HEAD_DIM)
    xs = [o["at"] + swap(t[C2:2 * C2]) for o, t in zip(ops, av)]
    for level in range(n_sq + 1):
        pb = [p.astype(BF16) for p in pw]
        xs = [x + _mm(p, x) for p, x in zip(pb, xs)]
        if level < n_sq:
            pw = [jnp.dot(p, p, preferred_element_type=F32) for p in pb]
    rbx = [_mm(jnp.concatenate([p, o["bb"].T], axis=0), x) for p, o, x in zip(a_rb, ops, xs)]
    for u, (bi, p) in enumerate(units):
        o = ops[u]
        sl = slice(p * LANES, (p + 1) * LANES)
        rb, bx, kv = rbx[u][0:C2], rbx[u][C2:2 * C2], av[u][2 * C2:3 * C2]
        r2 = unstack(o["rt"] + jnp.where(own, rb, 0.0))
        y0 = swap(unstack(jnp.where(own, 0.0, rb))) + unstack(av[u][0:C2])
        m_mat = jnp.where(eye, part("pend", bi, p)[0:1, :], 0.0) + jnp.where(own, bx, 0.0)
        g_mat = swap(jnp.where(own, 0.0, bx)) + kv
        out = _mm(jnp.concatenate([r2, m_mat], axis=0), state[bi, p])
        state[bi, p] = out[C:C + LANES] + g_mat
        y = out[0:C] + y0
        cen = y - head_sum(y) * (1.0 / HEAD_DIM)
        var = head_sum(cen * cen) * (1.0 / HEAD_DIM)
        bonus = head_sum(part("rkr", bi, p)) * part("v", bi, p)
        y = cen * lax.rsqrt(var + LNX_EPS) * lng_ref[:, sl] + lnb_ref[:, sl] + bonus
        ybuf[bi, :, sl] = y * part("g", bi, p)
    o_ref[...] = ybuf[...].astype(BF16)


def _rwkv(zm, zs, p, B, T):
    N = zm.shape[0]
    C = RWKV_CHUNK
    assert 2 * C == LANES and T % C == 0, "a head pair's stacked chunk must fill one 128-row tile"
    nc = T // C
    nb = RWKV_SEQS_PER_STEP if B % RWKV_SEQS_PER_STEP == 0 else 1
    tri = np.kron(np.eye(nb, dtype=np.float32), np.tril(np.ones((C, C), np.float32)))
    pad_rows = lambda w, rows: jnp.zeros((rows, HALF), F32).at[:w.shape[0]].set(w)
    row = lambda a: a.reshape(1, HALF)
    consts = [row(p["rw_w0"]), pad_rows(p["rw_decay_up"], LANES), row(p["rw_a0"]),
              pad_rows(p["rw_iclr_up"], LANES), pad_rows(p["rw_gate_up"], 2 * LANES),
              row(p["rw_k_k"]), row(p["rw_k_a"]), row(p["rw_r_k"]), row(p["rw_lnx_g"]),
              row(p["rw_lnx_b"]), jnp.asarray(tri, BF16)]
    full = lambda a: pl.BlockSpec(a.shape, lambda b, c: (0, 0))
    rcol = 4 * HALF // HALF
    zm3 = zm.reshape(B, T, Z_MAIN)
    out = pl.pallas_call(
        functools.partial(_rwkv_kernel, chunk=C, nb=nb),
        grid=(B // nb, nc),
        in_specs=[pl.BlockSpec((nb, C, HALF), lambda b, c: (b, c, rcol)),
                  pl.BlockSpec((nb, C, HALF), lambda b, c: (b, c, rcol + 1)),
                  pl.BlockSpec((nb, C, HALF), lambda b, c: (b, c, rcol + 2)),
                  pl.BlockSpec((nb, C, Z_SMALL), lambda b, c: (b, c, 0))] + [full(a) for a in consts],
        out_specs=pl.BlockSpec((nb, C, HALF), lambda b, c: (b, c, 0)),
        out_shape=jax.ShapeDtypeStruct((B, T, HALF), BF16),
        scratch_shapes=[pltpu.VMEM((nb, N_HEADS // 2, LANES, LANES), F32), pltpu.VMEM((nb, C, HALF), F32)],
        compiler_params=_cparams(("arbitrary", "arbitrary")),
        name="rwkv",
    )(zm3, zm3, zm3, zs.reshape(B, T, Z_SMALL), *consts)
    return out.reshape(N, HALF)


def _first_index(mask, iota, big):
    return jnp.min(jnp.where(mask, iota, big), axis=0, keepdims=True)


def _outproj_kernel(x_ref, yf_ref, yr_ref, g1_ref, sh2_ref, sc2_ref, g2_ref, n2g_ref, wo_ref, rwt_ref,
                    rb_ref, swg_ref, swu_ref, swd_ref, ustrict_ref,
                    xs_ref, hp_ref, idx_ref, wts_ref, rank_ref, *, tm):
    d = (jnp.dot(yf_ref[...], wo_ref[0:HALF, :], preferred_element_type=F32)
         + jnp.dot(yr_ref[...], wo_ref[HALF:2 * HALF, :], preferred_element_type=F32))
    x2 = x_ref[...] + g1_ref[...] * d
    h = x2 * lax.rsqrt(jnp.mean(x2 * x2, axis=-1, keepdims=True) + RMS_EPS) * n2g_ref[...]
    h = h * (1.0 + sc2_ref[...]) + sh2_ref[...]
    hb = h.astype(BF16)
    act = _silu(jnp.dot(hb, swg_ref[...], preferred_element_type=F32)) * jnp.dot(
        hb, swu_ref[...], preferred_element_type=F32)
    shared = jnp.dot(act.astype(BF16), swd_ref[...], preferred_element_type=F32)
    xs_ref[...] = x2 + g2_ref[...] * shared
    half = D_MODEL // 2
    hp_ref[...] = _pack_halves(h)

    r_hi, r_lo = _bf16_pieces(rwt_ref[...], 2)
    h_lo = (h - hb.astype(F32)).astype(BF16)
    nt = lambda a, b: lax.dot_general(a, b, (((1,), (1,)), ((), ())), preferred_element_type=F32)
    logits = nt(r_hi, hb) + nt(r_hi, h_lo) + nt(r_lo, hb)
    scores = _sigmoid(logits)
    sel = scores + rb_ref[:, 0:1]
    neg = -jnp.inf
    sel3 = sel.reshape(N_GROUPS, GROUP_SIZE, tm)
    io_in = lax.broadcasted_iota(I32, (N_GROUPS, GROUP_SIZE, tm), 1)
    m1 = jnp.max(sel3, axis=1, keepdims=True)
    f1 = jnp.min(jnp.where(sel3 == m1, io_in, GROUP_SIZE), axis=1, keepdims=True)
    m2 = jnp.max(jnp.where(io_in == f1, neg, sel3), axis=1, keepdims=True)
    gs = (m1 + m2).reshape(N_GROUPS, tm)
    io_g = lax.broadcasted_iota(I32, (N_GROUPS, tm), 0)
    gmask = jnp.zeros((N_GROUPS, tm), jnp.bool_)
    for _ in range(TOPK_GROUPS):
        mg = jnp.max(gs, axis=0, keepdims=True)
        fg = _first_index(gs == mg, io_g, N_GROUPS)
        pick = io_g == fg
        gmask = jnp.logical_or(gmask, pick)
        gs = jnp.where(pick, neg, gs)
    emask = jnp.broadcast_to(gmask.reshape(N_GROUPS, 1, tm), (N_GROUPS, GROUP_SIZE, tm)).reshape(N_EXPERTS, tm)
    cur = jnp.where(emask, sel, neg)
    io_e = lax.broadcasted_iota(I32, (N_EXPERTS, tm), 0)
    picks, idxs, wts = [], [], []
    for _ in range(TOP_K):
        me = jnp.max(cur, axis=0, keepdims=True)
        fe = _first_index(cur == me, io_e, N_EXPERTS)
        pick = io_e == fe
        picks.append(pick)
        idxs.append(fe)
        wts.append(jnp.sum(jnp.where(pick, scores, 0.0), axis=0, keepdims=True))
        cur = jnp.where(pick, neg, cur)
    wsum = wts[0]
    for w in wts[1:]:
        wsum = wsum + w
    zero_i = jnp.zeros((1, tm), I32)
    zero_f = jnp.zeros((1, tm), F32)
    idx_ref[...] = jnp.concatenate(idxs + [zero_i, zero_i], axis=0)
    wts_ref[...] = jnp.concatenate([w / wsum * ROUTED_SCALE for w in wts] + [zero_f, zero_f], axis=0)
    cnt = picks[0].astype(F32)
    for pk in picks[1:]:
        cnt = cnt + pk.astype(F32)
    excl = jnp.dot(cnt.astype(BF16), ustrict_ref[...], preferred_element_type=F32)
    ranks = [jnp.sum(jnp.where(pk, excl, 0.0), axis=0, keepdims=True).astype(I32) for pk in picks]
    rank_ref[...] = jnp.concatenate(ranks + [zero_i, zero_i], axis=0)


def _outproj(x2d, yf, yr, mod4, norm2_g, w_out, router_w, router_bias, swg, swu, swd, T):
    N, D = x2d.shape
    tm = min(512, T)
    tps = T // tm
    vec = lambda j: pl.BlockSpec((None, None, 1, D), lambda i: (i // tps, j, 0, 0))
    full = lambda a: pl.BlockSpec(a.shape, lambda i: (0, 0))
    ts = min(MOE_TILE, tm)
    ustrict = jnp.asarray(np.kron(np.eye(tm // ts, dtype=np.float32),
                                  np.triu(np.ones((ts, ts), np.float32), 1)), BF16)
    rb = jnp.broadcast_to(router_bias.reshape(N_EXPERTS, 1), (N_EXPERTS, LANES))
    consts = [norm2_g.reshape(1, D), w_out.astype(BF16), router_w.T, rb, swg.astype(BF16),
              swu.astype(BF16), swd.astype(BF16), ustrict]
    small = lambda dt: jax.ShapeDtypeStruct((8, N), dt)
    return pl.pallas_call(
        functools.partial(_outproj_kernel, tm=tm),
        grid=(N // tm,),
        in_specs=[pl.BlockSpec((tm, D), lambda i: (i, 0)),
                  pl.BlockSpec((tm, HALF), lambda i: (i, 0)),
                  pl.BlockSpec((tm, HALF), lambda i: (i, 0)),
                  vec(2), vec(3), vec(4), vec(5)] + [full(a) for a in consts],
        out_specs=[pl.BlockSpec((tm, D), lambda i: (i, 0)),
                   pl.BlockSpec((tm, D // 2), lambda i: (i, 0)),
                   pl.BlockSpec((8, tm), lambda i: (0, i)),
                   pl.BlockSpec((8, tm), lambda i: (0, i)),
                   pl.BlockSpec((8, tm), lambda i: (0, i))],
        out_shape=[jax.ShapeDtypeStruct((N, D), F32), jax.ShapeDtypeStruct((N, D // 2), jnp.uint32),
                   small(I32), small(F32), small(I32)],
        compiler_params=_cparams(("arbitrary",)),
        name="outproj",
    )(x2d, yf, yr, mod4, mod4, mod4, mod4, *consts)


def _wait_granules(count, wait_rows):
    full = count // WAIT_CHUNK

    def chunk(_, c):
        wait_rows(WAIT_CHUNK * GRANULE)
        return c

    def single(_, c):
        wait_rows(GRANULE)
        return c

    lax.fori_loop(0, full, chunk, 0)
    lax.fori_loop(full * WAIT_CHUNK, count, single, 0)


def _moe_plan(idx_t, rank_t):
    n = idx_t.shape[1]
    n_tiles = n // MOE_TILE
    experts = jnp.arange(N_EXPERTS, dtype=I32)
    hot = idx_t[:TOP_K, :, None] == experts
    cnt = jnp.sum(hot.reshape(TOP_K, n_tiles, MOE_TILE, N_EXPERTS).astype(I32), axis=(0, 2))
    gran = (cnt + GRANULE - 1) // GRANULE
    loc_end = jnp.cumsum(gran, axis=1)
    loc_off = loc_end - gran
    g_tile = loc_end[:, N_EXPERTS - 1]
    padded = (GRANULE * jnp.sum(gran, axis=0) + EXPERT_BLOCK - 1) // EXPERT_BLOCK * EXPERT_BLOCK
    pad_end = jnp.cumsum(padded)
    glob_off = (pad_end - padded)[None, :] + GRANULE * (jnp.cumsum(gran, axis=0) - gran)
    loc_tok = jnp.repeat(GRANULE * loc_off, MOE_TILE, axis=0)
    pos = rank_t[:TOP_K] + jnp.sum(jnp.where(hot, loc_tok[None], 0), axis=-1)
    pos = jnp.concatenate([pos, jnp.full((8 - TOP_K, n), -1, I32)], axis=0)
    g = jnp.arange(MOE_TILE, dtype=I32)
    e_of_g = jnp.minimum(jnp.sum((loc_end[:, None, :] <= g[None, :, None]).astype(I32), axis=-1), N_EXPERTS - 1)
    dst = jnp.sum(jnp.where(e_of_g[:, :, None] == experts,
                            glob_off[:, None, :] + GRANULE * (g[None, :, None] - loc_off[:, None, :]), 0), axis=-1)
    n_blocks = -(-(n * TOP_K + GRANULE * N_EXPERTS * n_tiles) // EXPERT_BLOCK) + N_EXPERTS
    n_rows = n_blocks * EXPERT_BLOCK
    gtab = jnp.where(g[None, :] < g_tile[:, None], dst, 0).reshape(n_tiles, 1, MOE_TILE)
    blk_start = jnp.arange(n_blocks, dtype=I32) * EXPERT_BLOCK
    blk_e = jnp.minimum(jnp.sum((pad_end[None, :] <= blk_start[:, None]).astype(I32), axis=1), N_EXPERTS - 1)
    plan = dict(pos=pos.astype(I32), gtab=gtab.astype(I32), g_tile=g_tile.astype(I32),
                pad_end=pad_end.astype(I32), padded=padded.astype(I32), blk_e=blk_e.astype(I32),
                n_used=(pad_end[N_EXPERTS - 1:] // EXPERT_BLOCK).astype(I32))
    return plan, n_rows


def _dispatch_kernel(pend_ref, padded_ref, nu_ref, gt_ref, gtab_ref, pos_ref, hp_ref, xs_ref, zeros, sbuf,
                     sem, zsem, *, n_blocks, n_tiles):
    @pl.when(pl.program_id(0) == 0)
    def _():
        zeros[...] = jnp.zeros_like(zeros)

        def block_copy(start):
            return pltpu.make_async_copy(zeros, xs_ref.at[pl.ds(pl.multiple_of(start, EXPERT_BLOCK),
                                                                 EXPERT_BLOCK), :], zsem)

        def pad_loop(fn):
            def body(e, _):
                @pl.when(padded_ref[e] > 0)
                def _():
                    fn(block_copy(pend_ref[e] - EXPERT_BLOCK))
                return 0
            lax.fori_loop(0, N_EXPERTS, body, 0)

        def tail_loop(fn):
            def body(b, _):
                fn(block_copy(b * EXPERT_BLOCK))
                return 0
            lax.fori_loop(nu_ref[0], n_blocks, body, 0)

        pad_loop(lambda cp: cp.start())
        tail_loop(lambda cp: cp.start())
        pad_loop(lambda cp: cp.wait())
        tail_loop(lambda cp: cp.wait())

    i = pl.program_id(0)
    slot = i % 2

    def store(g):
        pltpu.make_async_copy(sbuf.at[slot, pl.ds(pl.multiple_of(g * GRANULE, GRANULE), GRANULE), :],
                              xs_ref.at[pl.ds(pl.multiple_of(gtab_ref[0, 0, g], GRANULE), GRANULE), :],
                              sem.at[slot]).start()

    def wait_tile(which, count):
        def wait_rows(rows):
            pltpu.make_async_copy(sbuf.at[which, pl.ds(0, rows), :], xs_ref.at[pl.ds(0, rows), :],
                                  sem.at[which]).wait()
        wait_rows(MIN_GRANULES * GRANULE)
        _wait_granules(count - MIN_GRANULES, wait_rows)

    @pl.when(i >= 2)
    def _():
        wait_tile(slot, gt_ref[jnp.maximum(i - 2, 0)])

    pos = pos_ref[...]
    lo, hi = _unpack_halves(hp_ref[...])
    lo = lo.astype(BF16)
    hi = hi.astype(BF16)
    r_iota = lax.broadcasted_iota(I32, (MOE_ROW_CHUNK, MOE_TILE), 0).astype(F32).astype(BF16)
    one = jnp.ones((), BF16)
    for c in range(MOE_SORTED_ROWS // MOE_ROW_CHUNK):
        rel = pos - c * MOE_ROW_CHUNK
        rel = jnp.where(jnp.logical_and(rel >= 0, rel < MOE_ROW_CHUNK), rel, -1).astype(F32)
        perm = jnp.zeros((MOE_ROW_CHUNK, MOE_TILE), BF16)
        for k in range(TOP_K):
            perm = jnp.where(r_iota == rel[k:k + 1, :].astype(BF16), one, perm)
        sbuf[slot, c * MOE_ROW_CHUNK:(c + 1) * MOE_ROW_CHUNK, :] = _pack_bf16_values(
            jnp.dot(perm, lo, preferred_element_type=F32), jnp.dot(perm, hi, preferred_element_type=F32))
        g_lo, g_hi = c * MOE_ROW_CHUNK // GRANULE, (c + 1) * MOE_ROW_CHUNK // GRANULE
        for g in range(g_lo, min(g_hi, MIN_GRANULES)):
            store(g)
        if g_hi > MIN_GRANULES:
            def body(g, carry):
                store(g)
                return carry
            lax.fori_loop(max(g_lo, MIN_GRANULES), jnp.clip(gt_ref[i], max(g_lo, MIN_GRANULES), g_hi), body, 0)

    @pl.when(i == n_tiles - 1)
    def _():
        wait_tile(slot, gt_ref[i])
        if n_tiles > 1:
            wait_tile(1 - slot, gt_ref[jnp.maximum(i - 1, 0)])


def _dispatch(hp, plan, n_rows):
    N, W = hp.shape
    n_tiles = N // MOE_TILE
    n_blocks = n_rows // EXPERT_BLOCK
    return pl.pallas_call(
        functools.partial(_dispatch_kernel, n_blocks=n_blocks, n_tiles=n_tiles),
        grid_spec=pltpu.PrefetchScalarGridSpec(
            num_scalar_prefetch=4, grid=(n_tiles,),
            in_specs=[pl.BlockSpec((1, 1, MOE_TILE), lambda i, *_: (i, 0, 0), memory_space=pltpu.SMEM),
                      pl.BlockSpec((8, MOE_TILE), lambda i, *_: (0, i)),
                      pl.BlockSpec((MOE_TILE, W), lambda i, *_: (i, 0))],
            out_specs=pl.BlockSpec(memory_space=pl.ANY),
            scratch_shapes=[pltpu.VMEM((EXPERT_BLOCK, W), jnp.uint32),
                            pltpu.VMEM((2, MOE_SORTED_ROWS, W), jnp.uint32),
                            pltpu.SemaphoreType.DMA((2,)), pltpu.SemaphoreType.DMA]),
        out_shape=jax.ShapeDtypeStruct((n_rows, W), jnp.uint32),
        compiler_params=_cparams(("arbitrary",)),
        name="dispatch",
    )(plan["pad_end"], plan["padded"], plan["n_used"], plan["g_tile"], plan["gtab"], plan["pos"], hp)


def _experts_kernel(be_ref, nu_ref, xs_ref, wgf_ref, wuf_ref, wdf_ref, ys_ref, wg_ref, wu_ref, wd_ref,
                    xbuf, xsem):
    i = pl.program_id(0)
    n_live = nu_ref[0]
    live = i < n_live
    new_expert = jnp.logical_or(i == 0, be_ref[i] != be_ref[jnp.maximum(i - 1, 0)])

    def fetch(b):
        ring = lax.rem(b, EXPERT_RING)
        return pltpu.make_async_copy(xs_ref.at[pl.ds(pl.multiple_of(b * EXPERT_BLOCK, EXPERT_BLOCK), EXPERT_BLOCK), :],
                                     xbuf.at[ring], xsem.at[ring])

    @pl.when(i == 0)
    def _():
        for b in range(EXPERT_RING - 1):
            @pl.when(b < n_live)
            def _():
                fetch(b).start()

    @pl.when(i + EXPERT_RING - 1 < n_live)
    def _():
        fetch(i + EXPERT_RING - 1).start()

    @pl.when(jnp.logical_not(live))
    def _():
        ys_ref[...] = jnp.zeros_like(ys_ref)

    @pl.when(jnp.logical_and(live, new_expert))
    def _():
        wg_ref[...] = wgf_ref[...].astype(BF16)
        wu_ref[...] = wuf_ref[...].astype(BF16)
        wd_ref[...] = wdf_ref[...].astype(BF16)

    @pl.when(live)
    def _():
        half = D_MODEL // 2
        fetch(i).wait()
        lo, hi = _unpack_halves(xbuf[lax.rem(i, EXPERT_RING)])
        lo = lo.astype(BF16)
        hi = hi.astype(BF16)
        gate = (jnp.dot(lo, wg_ref[0:half, :], preferred_element_type=F32)
                + jnp.dot(hi, wg_ref[half:D_MODEL, :], preferred_element_type=F32))
        up = (jnp.dot(lo, wu_ref[0:half, :], preferred_element_type=F32)
              + jnp.dot(hi, wu_ref[half:D_MODEL, :], preferred_element_type=F32))
        y = jnp.dot((_silu(gate) * up).astype(BF16), wd_ref[...], preferred_element_type=F32)
        ys_ref[...] = _pack_halves(y)


def _experts(xs, n_rows, blk_e, n_used, wg, wu, wd):
    W = xs.shape[1]
    n_blocks = n_rows // EXPERT_BLOCK
    return pl.pallas_call(
        _experts_kernel,
        grid_spec=pltpu.PrefetchScalarGridSpec(
            num_scalar_prefetch=2, grid=(n_blocks,),
            in_specs=[pl.BlockSpec(memory_space=pl.ANY),
                      pl.BlockSpec((None, D_MODEL, D_EXPERT), lambda i, be, nu: (be[i], 0, 0)),
                      pl.BlockSpec((None, D_MODEL, D_EXPERT), lambda i, be, nu: (be[i], 0, 0)),
                      pl.BlockSpec((None, D_EXPERT, D_MODEL), lambda i, be, nu: (be[i], 0, 0))],
            out_specs=pl.BlockSpec((EXPERT_BLOCK, W), lambda i, be, nu: (i, 0)),
            scratch_shapes=[pltpu.VMEM((D_MODEL, D_EXPERT), BF16), pltpu.VMEM((D_MODEL, D_EXPERT), BF16),
                            pltpu.VMEM((D_EXPERT, D_MODEL), BF16),
                            pltpu.VMEM((EXPERT_RING, EXPERT_BLOCK, W), jnp.uint32),
                            pltpu.SemaphoreType.DMA((EXPERT_RING,))]),
        out_shape=jax.ShapeDtypeStruct((n_rows, W), jnp.uint32),
        compiler_params=_cparams(("arbitrary",)),
        name="experts",
    )(blk_e, n_used, xs, wg, wu, wd)


def _combine_kernel(gt_ref, gtab_ref, gtab_next_ref, ys_ref, pos_ref, xs_ref, wts_ref, g2_ref, fg_ref, o_ref,
                    buf, sem, *, n_tiles):
    i = pl.program_id(0)
    slot = i % 2
    nxt = jnp.minimum(i + 1, n_tiles - 1)

    def fetch(which, table_ref, g):
        pltpu.make_async_copy(ys_ref.at[pl.ds(pl.multiple_of(table_ref[0, 0, g], GRANULE), GRANULE), :],
                              buf.at[which, pl.ds(pl.multiple_of(g * GRANULE, GRANULE), GRANULE), :],
                              sem.at[which]).start()

    def fetch_rest(which, table_ref, count):
        def body(g, c):
            fetch(which, table_ref, g)
            return c
        lax.fori_loop(MIN_GRANULES, count, body, 0)

    def wait_tile(which, count):
        def wait_rows(rows):
            pltpu.make_async_copy(ys_ref.at[pl.ds(0, rows), :], buf.at[which, pl.ds(0, rows), :],
                                  sem.at[which]).wait()
        wait_rows(MIN_GRANULES * GRANULE)
        _wait_granules(count - MIN_GRANULES, wait_rows)

    @pl.when(i == 0)
    def _():
        buf[...] = jnp.zeros_like(buf)
        for g in range(MIN_GRANULES):
            fetch(0, gtab_ref, g)
        fetch_rest(0, gtab_ref, gt_ref[0])

    wait_tile(slot, gt_ref[i])

    pos = pos_ref[...]
    w = wts_ref[...]
    c_iota = lax.broadcasted_iota(I32, (MOE_TILE, MOE_ROW_CHUNK), 1).astype(F32).astype(BF16)
    half = D_MODEL // 2
    r_lo = jnp.zeros((MOE_TILE, half), F32)
    r_hi = jnp.zeros((MOE_TILE, half), F32)
    for c in range(MOE_SORTED_ROWS // MOE_ROW_CHUNK):
        rel = pos - c * MOE_ROW_CHUNK
        rel = jnp.where(jnp.logical_and(rel >= 0, rel < MOE_ROW_CHUNK), rel, -1).astype(F32)
        wb = jnp.zeros((MOE_TILE, MOE_ROW_CHUNK), BF16)
        for k in range(TOP_K):
            wb = jnp.where(c_iota == rel[:, k:k + 1].astype(BF16), w[:, k:k + 1].astype(BF16), wb)
        lo, hi = _unpack_halves(buf[slot, c * MOE_ROW_CHUNK:(c + 1) * MOE_ROW_CHUNK, :])
        r_lo = r_lo + jnp.dot(wb, lo.astype(BF16), preferred_element_type=F32)
        r_hi = r_hi + jnp.dot(wb, hi.astype(BF16), preferred_element_type=F32)
        for g in range(min(c * FETCH_PER_CHUNK, MIN_GRANULES), min((c + 1) * FETCH_PER_CHUNK, MIN_GRANULES)):
            fetch(1 - slot, gtab_next_ref, g)
        if c == MIN_GRANULES // FETCH_PER_CHUNK:
            fetch_rest(1 - slot, gtab_next_ref, gt_ref[nxt])
    x3 = xs_ref[...] + g2_ref[...] * jnp.concatenate([r_lo, r_hi], axis=1)
    y = x3 * lax.rsqrt(jnp.mean(x3 * x3, axis=-1, keepdims=True) + RMS_EPS) * fg_ref[...]
    o_ref[...] = y

    @pl.when(i == n_tiles - 1)
    def _():
        wait_tile(1 - slot, gt_ref[nxt])


def _combine(ys, plan, xsr, wts_t, mod4, final_g, T):
    N, D = xsr.shape
    W = ys.shape[1]
    tm = MOE_TILE
    tps = T // tm
    n_tiles = N // tm
    tile_tab = lambda off: pl.BlockSpec((1, 1, MOE_TILE), lambda i, gt: (jnp.minimum(i + off, n_tiles - 1), 0, 0),
                                        memory_space=pltpu.SMEM)
    return pl.pallas_call(
        functools.partial(_combine_kernel, n_tiles=n_tiles),
        grid_spec=pltpu.PrefetchScalarGridSpec(
            num_scalar_prefetch=1, grid=(n_tiles,),
            in_specs=[tile_tab(0), tile_tab(1),
                      pl.BlockSpec(memory_space=pl.ANY),
                      pl.BlockSpec((tm, 8), lambda i, gt: (i, 0)),
                      pl.BlockSpec((tm, D), lambda i, gt: (i, 0)),
                      pl.BlockSpec((tm, 8), lambda i, gt: (i, 0)),
                      pl.BlockSpec((None, None, 1, D), lambda i, gt: (i // tps, 5, 0, 0)),
                      pl.BlockSpec((1, D), lambda i, gt: (0, 0))],
            out_specs=pl.BlockSpec((tm, D), lambda i, gt: (i, 0)),
            scratch_shapes=[pltpu.VMEM((2, MOE_SORTED_ROWS, W), jnp.uint32), pltpu.SemaphoreType.DMA((2,))]),
        out_shape=jax.ShapeDtypeStruct((N, D), F32),
        compiler_params=_cparams(("arbitrary",)),
        name="combine",
    )(plan["g_tile"], plan["gtab"], plan["gtab"], ys, plan["pos"].T, xsr, wts_t, mod4,
      final_g.reshape(1, D))


def _pack_w_in(w_in):
    D = w_in.shape[0]
    fox_cols = 4 * HALF + 3 * N_HEADS
    wf = w_in[:, :fox_cols]
    wr = w_in[:, fox_cols:]
    o = 3 * HALF
    pad = lambda a, n: jnp.concatenate([a, jnp.zeros((D, n - a.shape[1]), a.dtype)], axis=1)
    parts = [wf[:, :4 * HALF], wr[:, :o],
             pad(wf[:, 4 * HALF:], LANES),
             pad(wr[:, o:o + DECAY_LORA], LANES),
             pad(wr[:, o + DECAY_LORA:o + DECAY_LORA + ICLR_LORA], LANES),
             pad(wr[:, o + DECAY_LORA + ICLR_LORA:], 2 * LANES)]
    return jnp.concatenate(parts, axis=1).astype(BF16)


def _pack_mu(mu):
    o = 3 * HALF
    pad = lambda a, n: jnp.concatenate([a, jnp.zeros((n - a.shape[0],), a.dtype)])
    small = jnp.concatenate([jnp.zeros((LANES,), mu.dtype),
                             pad(mu[o:o + DECAY_LORA], LANES),
                             pad(mu[o + DECAY_LORA:o + DECAY_LORA + ICLR_LORA], LANES),
                             pad(mu[o + DECAY_LORA + ICLR_LORA:], 2 * LANES)])
    return mu[:o].reshape(1, o), small.reshape(1, Z_SMALL)


def kernel(x, c, norm1_g, norm2_g, ada_w, ada_b, w_in, w_out, fox_qn_g, fox_kn_g, fox_on_g, fox_forget_b,
           rw_mu, rw_w0, rw_decay_up, rw_a0, rw_iclr_up, rw_gate_up, rw_k_k, rw_k_a, rw_r_k, rw_lnx_g,
           rw_lnx_b, router_w, router_bias, exp_w_gate, exp_w_up, exp_w_down, sh_w_gate, sh_w_up,
           sh_w_down, final_g):
    B, T, D = x.shape
    N = B * T
    depth = norm1_g.shape[0]
    assert depth == 1, "the combine kernel fuses the final RMSNorm, so exactly one layer is supported"
    xf = x.reshape(N, D)
    for l in range(depth):
        mod4 = _ada(c, ada_w[l], ada_b[l]).reshape(B, 6, 1, D)
        mu_big, mu_small = _pack_mu(rw_mu[l])
        zm, zs = _inproj(xf, mod4, norm1_g[l], _pack_w_in(w_in[l]), mu_big, mu_small, T)
        qp, kp, vp = _foxprep(zm, zs, fox_forget_b[l], fox_qn_g[l], fox_kn_g[l], T)
        y_fox = _attention(qp, kp, vp, zm, fox_on_g[l], B, T)
        rw = dict(rw_w0=rw_w0[l], rw_decay_up=rw_decay_up[l], rw_a0=rw_a0[l], rw_iclr_up=rw_iclr_up[l],
                  rw_gate_up=rw_gate_up[l], rw_k_k=rw_k_k[l], rw_k_a=rw_k_a[l], rw_r_k=rw_r_k[l],
                  rw_lnx_g=rw_lnx_g[l], rw_lnx_b=rw_lnx_b[l])
        y_rwkv = _rwkv(zm, zs, rw, B, T)
        xsr, hp, idx_t, wts_t, rank_t = _outproj(
            xf, y_fox, y_rwkv, mod4, norm2_g[l], w_out[l], router_w[l], router_bias[l],
            sh_w_gate[l], sh_w_up[l], sh_w_down[l], T)
        plan, n_rows = _moe_plan(idx_t, rank_t)
        xs = _dispatch(hp, plan, n_rows)
        ys = _experts(xs, n_rows, plan["blk_e"], plan["n_used"], exp_w_gate[l], exp_w_up[l], exp_w_down[l])
        xf = _combine(ys, plan, xsr, wts_t.T, mod4, final_g, T)
    return xf.reshape(B, T, D)
```

```python
import functools

import jax
import jax.numpy as jnp
import numpy as np
from jax import lax
from jax.experimental import pallas as pl
from jax.experimental.pallas import tpu as pltpu

F32 = jnp.float32
BF16 = jnp.bfloat16
I32 = jnp.int32
HIGHEST = lax.Precision.HIGHEST

D_MODEL = 1024
HEAD_DIM = 64
N_HEADS = 8
HALF = N_HEADS * HEAD_DIM
RMS_EPS = 1e-6
LNX_EPS = 64e-5
LOG2E = 1.4426950408889634
DECAY_LORA = 64
ICLR_LORA = 64
N_EXPERTS = 64
N_GROUPS = 8
GROUP_SIZE = N_EXPERTS // N_GROUPS
TOPK_GROUPS = 4
TOP_K = 6
D_EXPERT = 256
ROUTED_SCALE = 2.5
EXPERT_BLOCK = 1024
EXPERT_RING = 3

LANES = 128
Z_MAIN = 4 * HALF + 3 * HALF
Z_SMALL = 5 * LANES
VMEM_LIMIT = 56 * 1024 * 1024
ATTN_BLOCK = 512
ATTN_HEADS = 4
WAIT_CHUNK = 16
MOE_TILE = 256
GRANULE = 8
MOE_SORTED_ROWS = MOE_TILE * TOP_K + N_EXPERTS * GRANULE
MIN_GRANULES = MOE_TILE * TOP_K // GRANULE
FETCH_PER_CHUNK = 96
MOE_ROW_CHUNK = 256
RWKV_CHUNK = 64
RWKV_SEQS_PER_STEP = 4


def _cparams(semantics):
    return pltpu.CompilerParams(dimension_semantics=semantics, vmem_limit_bytes=VMEM_LIMIT)


def _mm(a, b):
    return jnp.dot(a.astype(BF16), b.astype(BF16), preferred_element_type=F32)


def _mm_nt(a, b):
    return lax.dot_general(a.astype(BF16), b.astype(BF16), (((1,), (1,)), ((), ())),
                           preferred_element_type=F32)


def _mm_f32(a, b):
    return jnp.dot(a, b, precision=HIGHEST, preferred_element_type=F32)


def _bf16_pieces(x, passes):
    pieces = []
    for _ in range(passes):
        piece = x.astype(BF16)
        pieces.append(piece)
        x = x - piece.astype(F32)
    return pieces


def _mm_split(m01, x, passes=3):
    return sum(jnp.dot(m01, p, preferred_element_type=F32) for p in _bf16_pieces(x, passes))


def _mm_split_r(x, m01, passes=2):
    return sum(jnp.dot(p, m01, preferred_element_type=F32) for p in _bf16_pieces(x, passes))


def _sigmoid(x):
    return 1.0 / (1.0 + jnp.exp(-x))


def _softplus(x):
    return jnp.maximum(x, 0.0) + jnp.log(1.0 + jnp.exp(-jnp.abs(x)))


def _silu(x):
    return x * _sigmoid(x)


def _pack_halves(x):
    w = x.shape[1] // 2
    bits = lambda t: lax.bitcast_convert_type(t.astype(jnp.bfloat16).astype(F32), jnp.uint32)
    return (bits(x[:, 0:w]) >> 16) | (bits(x[:, w:2 * w]) & jnp.uint32(0xFFFF0000))


def _pack_bf16_values(lo, hi):
    return (lax.bitcast_convert_type(lo, jnp.uint32) >> 16) | lax.bitcast_convert_type(hi, jnp.uint32)


def _unpack_halves(p):
    lo = lax.bitcast_convert_type(p << 16, F32)
    hi = lax.bitcast_convert_type(p & jnp.uint32(0xFFFF0000), F32)
    return lo, hi


def _shift_rows(z, carry_ref, first):
    rows = z.shape[0]
    prev_row = jnp.where(first, 0.0, carry_ref[0:1, :])
    prev = pltpu.roll(z, 1, 0)
    row0 = lax.broadcasted_iota(I32, (rows, 1), 0) == 0
    prev = jnp.where(row0, prev_row, prev)
    carry_ref[0:1, :] = z[rows - 1:rows, :]
    return prev


def _ada_kernel(c_ref, w_ref, b_ref, o_ref):
    o_ref[...] = _mm_f32(_silu(c_ref[...]), w_ref[...]) + b_ref[...]


def _ada(c, ada_w, ada_b):
    B, D = c.shape
    n_out = ada_w.shape[1]
    tn = 512
    return pl.pallas_call(
        _ada_kernel,
        grid=(n_out // tn,),
        in_specs=[pl.BlockSpec((B, D), lambda j: (0, 0)),
                  pl.BlockSpec((D, tn), lambda j: (0, j)),
                  pl.BlockSpec((1, tn), lambda j: (0, j))],
        out_specs=pl.BlockSpec((B, tn), lambda j: (0, j)),
        out_shape=jax.ShapeDtypeStruct((B, n_out), F32),
        compiler_params=_cparams(("arbitrary",)),
        name="ada",
    )(c, ada_w, ada_b.reshape(1, n_out))


def _inproj_kernel(x_ref, g_ref, sh_ref, sc_ref, w_ref, mub_ref, mus_ref, zm_ref, zs_ref,
                   carry_b, carry_s, *, tiles_per_seq):
    first = (pl.program_id(0) % tiles_per_seq) == 0
    x = x_ref[...]
    h = x * lax.rsqrt(jnp.mean(x * x, axis=-1, keepdims=True) + RMS_EPS) * g_ref[...]
    hb = (h * (1.0 + sc_ref[...]) + sh_ref[...]).astype(BF16)
    nf = 4 * HALF
    zm_ref[:, 0:nf] = jnp.dot(hb, w_ref[:, 0:nf], preferred_element_type=F32).astype(BF16)
    zr = jnp.dot(hb, w_ref[:, nf:Z_MAIN], preferred_element_type=F32)
    zr = zr + mub_ref[...] * (_shift_rows(zr, carry_b, first) - zr)
    zm_ref[:, nf:Z_MAIN] = zr.astype(BF16)
    zs = jnp.dot(hb, w_ref[:, Z_MAIN:Z_MAIN + Z_SMALL], preferred_element_type=F32)
    zs_ref[...] = zs + mus_ref[...] * (_shift_rows(zs, carry_s, first) - zs)


def _inproj(x2d, mod4, norm_g, w_all, mu_big, mu_small, T):
    N, D = x2d.shape
    tm = min(512, T)
    tps = T // tm
    vec = lambda j: pl.BlockSpec((None, None, 1, D), lambda i: (i // tps, j, 0, 0))
    return pl.pallas_call(
        functools.partial(_inproj_kernel, tiles_per_seq=tps),
        grid=(N // tm,),
        in_specs=[pl.BlockSpec((tm, D), lambda i: (i, 0)),
                  pl.BlockSpec((1, D), lambda i: (0, 0)),
                  vec(0), vec(1),
                  pl.BlockSpec((D, Z_MAIN + Z_SMALL), lambda i: (0, 0)),
                  pl.BlockSpec((1, 3 * HALF), lambda i: (0, 0)),
                  pl.BlockSpec((1, Z_SMALL), lambda i: (0, 0))],
        out_specs=[pl.BlockSpec((tm, Z_MAIN), lambda i: (i, 0)),
                   pl.BlockSpec((tm, Z_SMALL), lambda i: (i, 0))],
        out_shape=[jax.ShapeDtypeStruct((N, Z_MAIN), BF16),
                   jax.ShapeDtypeStruct((N, Z_SMALL), F32)],
        scratch_shapes=[pltpu.VMEM((8, 3 * HALF), F32), pltpu.VMEM((8, Z_SMALL), F32)],
        compiler_params=_cparams(("arbitrary",)),
        name="inproj",
    )(x2d, norm_g.reshape(1, D), mod4, mod4, w_all, mu_big, mu_small)


def _foxprep_kernel(q_ref, k_ref, v_ref, zs_ref, fb_ref, qg_ref, kg_ref, tri_ref, eexp_ref,
                    esum_ref, e8_ref, plq_ref, plk_ref, plv_ref, cq_ref, ck_ref, cv_ref,
                    qp_ref, kp_ref, vp_ref, carry_k, carry_v, carry_c, *, tiles_per_seq):
    first = (pl.program_id(0) % tiles_per_seq) == 0
    zs = zs_ref[...]
    logf = -_softplus(-(zs + fb_ref[...]))
    cum = _mm_split(tri_ref[...], logf) + jnp.where(first, 0.0, carry_c[0:1, :])
    carry_c[0:1, :] = cum[cum.shape[0] - 1:, :]
    cum2 = cum * LOG2E
    head_lane = lax.broadcasted_iota(I32, cum2.shape, 1) < N_HEADS
    c_hi, c_mid, c_lo = [jnp.where(head_lane, p.astype(F32), 0.0) for p in _bf16_pieces(cum2, 3)]
    aug = (c_hi + pltpu.roll(c_mid, N_HEADS, 1) + pltpu.roll(c_lo, 2 * N_HEADS, 1)).astype(BF16)
    a_full = _mm_split_r(_sigmoid(zs), eexp_ref[...])
    k = k_ref[...].astype(F32)
    v = v_ref[...].astype(F32)
    a_k = a_full[:, 0:HALF]
    a_v = a_full[:, HALF:2 * HALF]
    k = a_k * _shift_rows(k, carry_k, first) + (1.0 - a_k) * k
    v = a_v * _shift_rows(v, carry_v, first) + (1.0 - a_v) * v
    q = q_ref[...].astype(F32)

    def head_rms(t, gain):
        ms = _mm_split_r(t * t, esum_ref[...]) * (1.0 / HEAD_DIM)
        inv = _mm_split_r(lax.rsqrt(ms + RMS_EPS), e8_ref[...])
        return t * inv * gain

    qn = head_rms(q, qg_ref[...]) * (HEAD_DIM ** -0.5 * LOG2E)
    kn = head_rms(k, kg_ref[...])
    lhs_q = jnp.concatenate([qn.astype(BF16), aug], axis=1)
    lhs_k = jnp.concatenate([kn.astype(BF16), aug], axis=1)
    qp_ref[...] = (jnp.dot(lhs_q, plq_ref[...], preferred_element_type=F32) + cq_ref[...]).astype(BF16)
    kp_ref[...] = (jnp.dot(lhs_k, plk_ref[...], preferred_element_type=F32) + ck_ref[...]).astype(BF16)
    vp_ref[...] = (jnp.dot(v.astype(BF16), plv_ref[...], preferred_element_type=F32)
                   + cv_ref[...]).astype(BF16)


def _fox_constants(tm):
    hp = N_HEADS * LANES
    eexp = np.zeros((LANES, 2 * HALF), np.float32)
    esum = np.zeros((HALF, LANES), np.float32)
    e8 = np.zeros((LANES, HALF), np.float32)
    plq = np.zeros((HALF + LANES, hp), np.float32)
    plk = np.zeros((HALF + LANES, hp), np.float32)
    plv = np.zeros((HALF, hp), np.float32)
    cq = np.zeros((1, hp), np.float32)
    ck = np.zeros((1, hp), np.float32)
    cv = np.zeros((1, hp), np.float32)
    for h in range(N_HEADS):
        sl = slice(h * HEAD_DIM, (h + 1) * HEAD_DIM)
        eexp[8 + h, sl] = 1.0
        eexp[16 + h, HALF + h * HEAD_DIM:HALF + (h + 1) * HEAD_DIM] = 1.0
        esum[sl, h] = 1.0
        e8[h, sl] = 1.0
        base = h * LANES
        for d in range(HEAD_DIM):
            plq[h * HEAD_DIM + d, base + d] = 1.0
            plk[h * HEAD_DIM + d, base + d] = 1.0
            plv[h * HEAD_DIM + d, base + d] = 1.0
        for j in range(3):
            plq[HALF + j * N_HEADS + h, base + HEAD_DIM + j] = 1.0
            plk[HALF + j * N_HEADS + h, base + HEAD_DIM + 3 + j] = -1.0
            cq[0, base + HEAD_DIM + 3 + j] = 1.0
            ck[0, base + HEAD_DIM + j] = 1.0
        cv[0, base + HEAD_DIM] = 1.0
    tri = np.tril(np.ones((tm, tm), np.float32))
    bf = lambda a: jnp.asarray(a, BF16)
    return dict(tri=bf(tri), eexp=bf(eexp), esum=bf(esum), e8=bf(e8),
                plq=bf(plq), plk=bf(plk), plv=bf(plv), cq=jnp.asarray(cq), ck=jnp.asarray(ck),
                cv=jnp.asarray(cv))


def _foxprep(zm, zs, forget_b, qn_g, kn_g, T):
    N = zm.shape[0]
    tm = min(512, T)
    tps = T // tm
    cst = _fox_constants(tm)
    hp = N_HEADS * LANES
    fb = jnp.zeros((1, LANES), F32).at[0, :N_HEADS].set(forget_b)
    full = lambda a: pl.BlockSpec(a.shape, lambda i: (0, 0))
    consts = [fb, qn_g.reshape(1, HALF), kn_g.reshape(1, HALF), cst["tri"], cst["eexp"], cst["esum"],
              cst["e8"], cst["plq"], cst["plk"], cst["plv"], cst["cq"], cst["ck"], cst["cv"]]
    return pl.pallas_call(
        functools.partial(_foxprep_kernel, tiles_per_seq=tps),
        grid=(N // tm,),
        in_specs=[pl.BlockSpec((tm, HALF), lambda i: (i, 0)),
                  pl.BlockSpec((tm, HALF), lambda i: (i, 1)),
                  pl.BlockSpec((tm, HALF), lambda i: (i, 2)),
                  pl.BlockSpec((tm, LANES), lambda i: (i, 0))] + [full(a) for a in consts],
        out_specs=[pl.BlockSpec((tm, hp), lambda i: (i, 0))] * 3,
        out_shape=[jax.ShapeDtypeStruct((N, hp), BF16)] * 3,
        scratch_shapes=[pltpu.VMEM((8, HALF), F32), pltpu.VMEM((8, HALF), F32),
                        pltpu.VMEM((8, LANES), F32)],
        compiler_params=_cparams(("arbitrary",)),
        name="foxprep",
    )(zm, zm, zm, zs, *consts)


def _attn_kernel(q_ref, k_ref, v_ref, g_ref, ong_ref, o_ref, vt_ref, *, tq):
    i = pl.program_id(2)
    n_kv = vt_ref.shape[1]
    heads = range(ATTN_HEADS)
    lanes = [slice(hh * LANES, (hh + 1) * LANES) for hh in heads]

    @pl.when(i == 0)
    def _():
        for hh in heads:
            for c in range(n_kv):
                vt_ref[hh, c] = v_ref[c * tq:(c + 1) * tq, lanes[hh]].T

    key = lax.broadcasted_iota(I32, (tq, tq), 0)
    qry = lax.broadcasted_iota(I32, (tq, tq), 1)
    causal = key <= qry
    qs = [q_ref[:, lanes[hh]] for hh in heads]

    def step(j, carry, masked):
        m, acc = carry
        start = pl.multiple_of(j * tq, tq)
        s = [lax.dot_general(k_ref[pl.ds(start, tq), lanes[hh]], qs[hh], (((1,), (1,)), ((), ())),
                             preferred_element_type=F32) for hh in heads]
        m_out, acc_out = [], []
        for hh in heads:
            sh = jnp.where(causal, s[hh], -jnp.inf) if masked else s[hh]
            m_new = jnp.maximum(m[hh], jnp.max(sh, axis=0, keepdims=True))
            p = jnp.exp2(sh - m_new)
            acc_out.append(jnp.exp2(m[hh] - m_new) * acc[hh]
                           + jnp.dot(vt_ref[hh, j], p.astype(BF16), preferred_element_type=F32))
            m_out.append(m_new)
        return tuple(m_out), tuple(acc_out)

    init = (tuple(jnp.full((1, tq), -jnp.inf, F32) for _ in heads),
            tuple(jnp.zeros((LANES, tq), F32) for _ in heads))
    carry = lax.fori_loop(0, i, functools.partial(step, masked=False), init)
    _, acc = step(i, carry, True)
    chan = lax.broadcasted_iota(I32, (LANES, tq), 0)
    chan_w = jnp.where(chan < HEAD_DIM, 1.0 / HEAD_DIM, jnp.where(chan == HEAD_DIM, RMS_EPS, 0.0))
    lane = lax.broadcasted_iota(I32, (tq, LANES), 1)
    outs = []
    for hh in heads:
        t = jnp.sum(acc[hh] * acc[hh] * chan_w, axis=0, keepdims=True)
        outs.append((acc[hh] * lax.rsqrt(t)).T)
    o = jnp.concatenate([jnp.where(lane < HEAD_DIM, outs[2 * p], pltpu.roll(outs[2 * p + 1], HEAD_DIM, 1))
                         for p in range(ATTN_HEADS // 2)], axis=1)
    y = o * ong_ref[...] * _sigmoid(g_ref[...].astype(F32))
    o_ref[...] = y.astype(BF16)


def _attention(qp, kp, vp, zm, on_g, B, T):
    N = qp.shape[0]
    tq = min(ATTN_BLOCK, T)
    nq = T // tq
    groups = N_HEADS // ATTN_HEADS
    wp = ATTN_HEADS * LANES
    wo = ATTN_HEADS * HEAD_DIM
    g_col0 = 3 * HALF // wo
    return pl.pallas_call(
        functools.partial(_attn_kernel, tq=tq),
        grid=(B, groups, nq),
        in_specs=[pl.BlockSpec((tq, wp), lambda b, p, i: (b * nq + i, p)),
                  pl.BlockSpec((T, wp), lambda b, p, i: (b, p)),
                  pl.BlockSpec((T, wp), lambda b, p, i: (b, p)),
                  pl.BlockSpec((tq, wo), lambda b, p, i: (b * nq + i, g_col0 + p)),
                  pl.BlockSpec((None, 1, wo), lambda b, p, i: (p, 0, 0))],
        out_specs=pl.BlockSpec((tq, wo), lambda b, p, i: (b * nq + i, p)),
        out_shape=jax.ShapeDtypeStruct((N, HALF), BF16),
        scratch_shapes=[pltpu.VMEM((ATTN_HEADS, nq, LANES, tq), BF16)],
        compiler_params=_cparams(("arbitrary", "arbitrary", "arbitrary")),
        name="attn",
    )(qp, kp, vp, zm, on_g.reshape(groups, 1, wo))


def _rwkv_kernel(r_ref, k_ref, v_ref, zs_ref, w0_ref, dup_ref, a0_ref, iup_ref, gup_ref, kk_ref,
                 ka_ref, rk_ref, lng_ref, lnb_ref, tri_ref, o_ref, state, *, chunk, nb):
    C = chunk
    c_idx = pl.program_id(1)

    @pl.when(c_idx == 0)
    def _():
        state[...] = jnp.zeros_like(state)

    R = nb * C
    r = r_ref[...].reshape(R, HALF).astype(F32)
    k = k_ref[...].reshape(R, HALF).astype(F32)
    v = v_ref[...].reshape(R, HALF).astype(F32)
    zs = zs_ref[...].reshape(R, Z_SMALL)
    wd = zs[:, LANES:2 * LANES]
    ad = zs[:, 2 * LANES:3 * LANES]
    gd = zs[:, 3 * LANES:5 * LANES]
    wl = w0_ref[...] + _mm(jnp.tanh(wd), dup_ref[...])
    lw = -jnp.exp(-_softplus(-wl) - 0.5)
    a = _sigmoid(a0_ref[...] + _mm(ad, iup_ref[...]))
    g = _mm(_sigmoid(gd), gup_ref[...])
    kk = k * kk_ref[...]
    k2 = k * (1.0 + (a - 1.0) * ka_ref[...])
    cl = _mm_split(tri_ref[...], lw)
    cl_end = jnp.concatenate(
        [jnp.broadcast_to(cl[(bi + 1) * C - 1:(bi + 1) * C, :], (C, HALF)) for bi in range(nb)], axis=0)
    e_neg = jnp.exp(-cl)
    e_tail = jnp.exp(cl_end - cl)
    pre = dict(rt=r * jnp.exp(cl), at=-kk * jnp.exp(cl - lw), kh=k2 * e_neg, bh=kk * a * e_neg,
               kb=k2 * e_tail, bb=kk * a * e_tail, v=v, kk=kk, pend=jnp.exp(cl_end),
               rkr=r * k2 * rk_ref[...], g=g)

    C2 = 2 * C
    row = lax.broadcasted_iota(I32, (C2, C2), 0)
    col = lax.broadcasted_iota(I32, (C2, C2), 1)
    lower = (col & (C - 1)) <= (row & (C - 1))
    strict = (col & (C - 1)) < (row & (C - 1))
    eye = row == col
    head0 = lax.broadcasted_iota(I32, (C, LANES), 1) < HEAD_DIM
    n_sq = int(np.log2(C)) - 1
    units = [(bi, p) for bi in range(nb) for p in range(N_HEADS // 2)]

    def part(name, bi, p):
        return pre[name][bi * C:(bi + 1) * C, p * LANES:(p + 1) * LANES]

    def stack(x):
        return jnp.concatenate([jnp.where(head0, x, 0.0), jnp.where(head0, 0.0, x)], axis=0)

    def unstack(x):
        return x[0:C] + x[C:C2]

    def head_sum(x):
        s0 = jnp.sum(jnp.where(head0, x, 0.0), axis=1, keepdims=True)
        s1 = jnp.sum(jnp.where(head0, 0.0, x), axis=1, keepdims=True)
        return jnp.where(head0, s0, s1)

    ops = []
    for bi, p in units:
        kk_p = part("kk", bi, p)
        inv = 1.0 / jnp.maximum(jnp.sqrt(head_sum(kk_p * kk_p)), 1e-12)
        ops.append(dict(rt=stack(part("rt", bi, p)), at=stack(part("at", bi, p) * inv),
                        kh=stack(part("kh", bi, p)).astype(BF16), bh=stack(part("bh", bi, p) * inv).astype(BF16),
                        kb=stack(part("kb", bi, p)).astype(BF16), bb=stack(part("bb", bi, p) * inv).astype(BF16),
                        v=stack(part("v", bi, p)).astype(BF16)))
    ra = [jnp.concatenate([o["rt"], o["at"]], axis=0).astype(BF16) for o in ops]
    gg = [_mm_nt(x, jnp.concatenate([o["kh"], o["bh"]], axis=0)) for x, o in zip(ra, ops)]
    g1 = [t[:, 0:C2] for t in gg]
    g2 = [t[:, C2:2 * C2] for t in gg]
    a_rb = [jnp.where(lower, t[0:C2], 0.0).astype(BF16) for t in g2]
    pw = [jnp.where(strict, t[C2:2 * C2], 0.0) for t in g2]
    av = [_mm(jnp.concatenate([jnp.where(lower, t[0:C2], 0.0).astype(BF16),
                               jnp.where(strict, t[C2:2 * C2], 0.0).astype(BF16), o["kb"].T], axis=0),
              o["v"]) for t, o in zip(g1, ops)]
    swap = lambda t: pltpu.roll(t, HEAD_DIM, 1)
    own = (row < C) == (col < HEAD_DIM)
    xs = [o["at"] + swap(t[C2:2 * C2]) for o, t in zip(ops, av)]
    for level in range(n_sq + 1):
        pb = [p.astype(BF16) for p in pw]
        xs = [x + _mm(p, x) for p, x in zip(pb, xs)]
        if level < n_sq:
            pw = [jnp.dot(p, p, preferred_element_type=F32) for p in pb]
    rbx = [_mm(jnp.concatenate([p, o["bb"].T], axis=0), x) for p, o, x in zip(a_rb, ops, xs)]
    for u, (bi, p) in enumerate(units):
        o = ops[u]
        sl = slice(p * LANES, (p + 1) * LANES)
        rb, bx, kv = rbx[u][0:C2], rbx[u][C2:2 * C2], av[u][2 * C2:3 * C2]
        r2 = unstack(o["rt"] + jnp.where(own, rb, 0.0))
        y0 = swap(unstack(jnp.where(own, 0.0, rb))) + unstack(av[u][0:C2])
        m_mat = jnp.where(eye, part("pend", bi, p)[0:1, :], 0.0) + jnp.where(own, bx, 0.0)
        g_mat = swap(jnp.where(own, 0.0, bx)) + kv
        out = _mm(jnp.concatenate([r2, m_mat], axis=0), state[bi, p])
        state[bi, p] = out[C:C + LANES] + g_mat
        y = out[0:C] + y0
        cen = y - head_sum(y) * (1.0 / HEAD_DIM)
        var = head_sum(cen * cen) * (1.0 / HEAD_DIM)
        bonus = head_sum(part("rkr", bi, p)) * part("v", bi, p)
        y = cen * lax.rsqrt(var + LNX_EPS) * lng_ref[:, sl] + lnb_ref[:, sl] + bonus
        o_ref[bi, :, sl] = (y * part("g", bi, p)).astype(BF16)


def _rwkv(zm, zs, p, B, T):
    N = zm.shape[0]
    C = RWKV_CHUNK
    assert 2 * C == LANES and T % C == 0, "a head pair's stacked chunk must fill one 128-row tile"
    nc = T // C
    nb = RWKV_SEQS_PER_STEP if B % RWKV_SEQS_PER_STEP == 0 else 1
    tri = np.kron(np.eye(nb, dtype=np.float32), np.tril(np.ones((C, C), np.float32)))
    pad_rows = lambda w, rows: jnp.zeros((rows, HALF), F32).at[:w.shape[0]].set(w)
    row = lambda a: a.reshape(1, HALF)
    consts = [row(p["rw_w0"]), pad_rows(p["rw_decay_up"], LANES), row(p["rw_a0"]),
              pad_rows(p["rw_iclr_up"], LANES), pad_rows(p["rw_gate_up"], 2 * LANES),
              row(p["rw_k_k"]), row(p["rw_k_a"]), row(p["rw_r_k"]), row(p["rw_lnx_g"]),
              row(p["rw_lnx_b"]), jnp.asarray(tri, BF16)]
    full = lambda a: pl.BlockSpec(a.shape, lambda b, c: (0, 0))
    rcol = 4 * HALF // HALF
    zm3 = zm.reshape(B, T, Z_MAIN)
    out = pl.pallas_call(
        functools.partial(_rwkv_kernel, chunk=C, nb=nb),
        grid=(B // nb, nc),
        in_specs=[pl.BlockSpec((nb, C, HALF), lambda b, c: (b, c, rcol)),
                  pl.BlockSpec((nb, C, HALF), lambda b, c: (b, c, rcol + 1)),
                  pl.BlockSpec((nb, C, HALF), lambda b, c: (b, c, rcol + 2)),
                  pl.BlockSpec((nb, C, Z_SMALL), lambda b, c: (b, c, 0))] + [full(a) for a in consts],
        out_specs=pl.BlockSpec((nb, C, HALF), lambda b, c: (b, c, 0)),
        out_shape=jax.ShapeDtypeStruct((B, T, HALF), BF16),
        scratch_shapes=[pltpu.VMEM((nb, N_HEADS // 2, LANES, LANES), F32)],
        compiler_params=_cparams(("arbitrary", "arbitrary")),
        name="rwkv",
    )(zm3, zm3, zm3, zs.reshape(B, T, Z_SMALL), *consts)
    return out.reshape(N, HALF)


def _first_index(mask, iota, big):
    return jnp.min(jnp.where(mask, iota, big), axis=0, keepdims=True)


def _outproj_kernel(x_ref, yf_ref, yr_ref, g1_ref, sh2_ref, sc2_ref, g2_ref, n2g_ref, wo_ref, rwt_ref,
                    rb_ref, swg_ref, swu_ref, swd_ref, ustrict_ref,
                    xs_ref, hp_ref, idx_ref, wts_ref, rank_ref, *, tm):
    d = (jnp.dot(yf_ref[...], wo_ref[0:HALF, :], preferred_element_type=F32)
         + jnp.dot(yr_ref[...], wo_ref[HALF:2 * HALF, :], preferred_element_type=F32))
    x2 = x_ref[...] + g1_ref[...] * d
    h = x2 * lax.rsqrt(jnp.mean(x2 * x2, axis=-1, keepdims=True) + RMS_EPS) * n2g_ref[...]
    h = h * (1.0 + sc2_ref[...]) + sh2_ref[...]
    hb = h.astype(BF16)
    act = _silu(jnp.dot(hb, swg_ref[...], preferred_element_type=F32)) * jnp.dot(
        hb, swu_ref[...], preferred_element_type=F32)
    shared = jnp.dot(act.astype(BF16), swd_ref[...], preferred_element_type=F32)
    xs_ref[...] = x2 + g2_ref[...] * shared
    half = D_MODEL // 2
    hp_ref[...] = _pack_halves(h)

    r_hi, r_lo = _bf16_pieces(rwt_ref[...], 2)
    h_lo = (h - hb.astype(F32)).astype(BF16)
    nt = lambda a, b: lax.dot_general(a, b, (((1,), (1,)), ((), ())), preferred_element_type=F32)
    logits = nt(r_hi, hb) + nt(r_hi, h_lo) + nt(r_lo, hb)
    scores = _sigmoid(logits)
    sel = scores + rb_ref[:, 0:1]
    neg = -jnp.inf
    sel3 = sel.reshape(N_GROUPS, GROUP_SIZE, tm)
    io_in = lax.broadcasted_iota(I32, (N_GROUPS, GROUP_SIZE, tm), 1)
    m1 = jnp.max(sel3, axis=1, keepdims=True)
    f1 = jnp.min(jnp.where(sel3 == m1, io_in, GROUP_SIZE), axis=1, keepdims=True)
    m2 = jnp.max(jnp.where(io_in == f1, neg, sel3), axis=1, keepdims=True)
    gs = (m1 + m2).reshape(N_GROUPS, tm)
    io_g = lax.broadcasted_iota(I32, (N_GROUPS, tm), 0)
    gmask = jnp.zeros((N_GROUPS, tm), jnp.bool_)
    for _ in range(TOPK_GROUPS):
        mg = jnp.max(gs, axis=0, keepdims=True)
        fg = _first_index(gs == mg, io_g, N_GROUPS)
        pick = io_g == fg
        gmask = jnp.logical_or(gmask, pick)
        gs = jnp.where(pick, neg, gs)
    emask = jnp.broadcast_to(gmask.reshape(N_GROUPS, 1, tm), (N_GROUPS, GROUP_SIZE, tm)).reshape(N_EXPERTS, tm)
    cur = jnp.where(emask, sel, neg)
    io_e = lax.broadcasted_iota(I32, (N_EXPERTS, tm), 0)
    picks, idxs, wts = [], [], []
    for _ in range(TOP_K):
        me = jnp.max(cur, axis=0, keepdims=True)
        fe = _first_index(cur == me, io_e, N_EXPERTS)
        pick = io_e == fe
        picks.append(pick)
        idxs.append(fe)
        wts.append(jnp.sum(jnp.where(pick, scores, 0.0), axis=0, keepdims=True))
        cur = jnp.where(pick, neg, cur)
    wsum = wts[0]
    for w in wts[1:]:
        wsum = wsum + w
    zero_i = jnp.zeros((1, tm), I32)
    zero_f = jnp.zeros((1, tm), F32)
    idx_ref[...] = jnp.concatenate(idxs + [zero_i, zero_i], axis=0)
    wts_ref[...] = jnp.concatenate([w / wsum * ROUTED_SCALE for w in wts] + [zero_f, zero_f], axis=0)
    cnt = picks[0].astype(F32)
    for pk in picks[1:]:
        cnt = cnt + pk.astype(F32)
    excl = jnp.dot(cnt.astype(BF16), ustrict_ref[...], preferred_element_type=F32)
    ranks = [jnp.sum(jnp.where(pk, excl, 0.0), axis=0, keepdims=True).astype(I32) for pk in picks]
    rank_ref[...] = jnp.concatenate(ranks + [zero_i, zero_i], axis=0)


def _outproj(x2d, yf, yr, mod4, norm2_g, w_out, router_w, router_bias, swg, swu, swd, T):
    N, D = x2d.shape
    tm = min(512, T)
    tps = T // tm
    vec = lambda j: pl.BlockSpec((None, None, 1, D), lambda i: (i // tps, j, 0, 0))
    full = lambda a: pl.BlockSpec(a.shape, lambda i: (0, 0))
    ts = min(MOE_TILE, tm)
    ustrict = jnp.asarray(np.kron(np.eye(tm // ts, dtype=np.float32),
                                  np.triu(np.ones((ts, ts), np.float32), 1)), BF16)
    rb = jnp.broadcast_to(router_bias.reshape(N_EXPERTS, 1), (N_EXPERTS, LANES))
    consts = [norm2_g.reshape(1, D), w_out.astype(BF16), router_w.T, rb, swg.astype(BF16),
              swu.astype(BF16), swd.astype(BF16), ustrict]
    small = lambda dt: jax.ShapeDtypeStruct((8, N), dt)
    return pl.pallas_call(
        functools.partial(_outproj_kernel, tm=tm),
        grid=(N // tm,),
        in_specs=[pl.BlockSpec((tm, D), lambda i: (i, 0)),
                  pl.BlockSpec((tm, HALF), lambda i: (i, 0)),
                  pl.BlockSpec((tm, HALF), lambda i: (i, 0)),
                  vec(2), vec(3), vec(4), vec(5)] + [full(a) for a in consts],
        out_specs=[pl.BlockSpec((tm, D), lambda i: (i, 0)),
                   pl.BlockSpec((tm, D // 2), lambda i: (i, 0)),
                   pl.BlockSpec((8, tm), lambda i: (0, i)),
                   pl.BlockSpec((8, tm), lambda i: (0, i)),
                   pl.BlockSpec((8, tm), lambda i: (0, i))],
        out_shape=[jax.ShapeDtypeStruct((N, D), F32), jax.ShapeDtypeStruct((N, D // 2), jnp.uint32),
                   small(I32), small(F32), small(I32)],
        compiler_params=_cparams(("arbitrary",)),
        name="outproj",
    )(x2d, yf, yr, mod4, mod4, mod4, mod4, *consts)


def _wait_granules(count, wait_rows):
    full = count // WAIT_CHUNK

    def chunk(_, c):
        wait_rows(WAIT_CHUNK * GRANULE)
        return c

    def single(_, c):
        wait_rows(GRANULE)
        return c

    lax.fori_loop(0, full, chunk, 0)
    lax.fori_loop(full * WAIT_CHUNK, count, single, 0)


def _moe_plan(idx_t, rank_t):
    n = idx_t.shape[1]
    n_tiles = n // MOE_TILE
    experts = jnp.arange(N_EXPERTS, dtype=I32)
    hot = idx_t[:TOP_K, :, None] == experts
    cnt = jnp.sum(hot.reshape(TOP_K, n_tiles, MOE_TILE, N_EXPERTS).astype(I32), axis=(0, 2))
    gran = (cnt + GRANULE - 1) // GRANULE
    loc_end = jnp.cumsum(gran, axis=1)
    loc_off = loc_end - gran
    g_tile = loc_end[:, N_EXPERTS - 1]
    padded = (GRANULE * jnp.sum(gran, axis=0) + EXPERT_BLOCK - 1) // EXPERT_BLOCK * EXPERT_BLOCK
    pad_end = jnp.cumsum(padded)
    glob_off = (pad_end - padded)[None, :] + GRANULE * (jnp.cumsum(gran, axis=0) - gran)
    loc_tok = jnp.repeat(GRANULE * loc_off, MOE_TILE, axis=0)
    pos = rank_t[:TOP_K] + jnp.sum(jnp.where(hot, loc_tok[None], 0), axis=-1)
    pos = jnp.concatenate([pos, jnp.full((8 - TOP_K, n), -1, I32)], axis=0)
    g = jnp.arange(MOE_TILE, dtype=I32)
    e_of_g = jnp.minimum(jnp.sum((loc_end[:, None, :] <= g[None, :, None]).astype(I32), axis=-1), N_EXPERTS - 1)
    dst = jnp.sum(jnp.where(e_of_g[:, :, None] == experts,
                            glob_off[:, None, :] + GRANULE * (g[None, :, None] - loc_off[:, None, :]), 0), axis=-1)
    n_blocks = -(-(n * TOP_K + GRANULE * N_EXPERTS * n_tiles) // EXPERT_BLOCK) + N_EXPERTS
    n_rows = n_blocks * EXPERT_BLOCK
    gtab = jnp.where(g[None, :] < g_tile[:, None], dst, 0).reshape(n_tiles, 1, MOE_TILE)
    blk_start = jnp.arange(n_blocks, dtype=I32) * EXPERT_BLOCK
    blk_e = jnp.minimum(jnp.sum((pad_end[None, :] <= blk_start[:, None]).astype(I32), axis=1), N_EXPERTS - 1)
    plan = dict(pos=pos.astype(I32), gtab=gtab.astype(I32), g_tile=g_tile.astype(I32),
                pad_end=pad_end.astype(I32), padded=padded.astype(I32), blk_e=blk_e.astype(I32),
                n_used=(pad_end[N_EXPERTS - 1:] // EXPERT_BLOCK).astype(I32))
    return plan, n_rows


def _dispatch_kernel(pend_ref, padded_ref, nu_ref, gt_ref, gtab_ref, pos_ref, hp_ref, xs_ref, zeros, sbuf,
                     sem, zsem, *, n_blocks, n_tiles):
    @pl.when(pl.program_id(0) == 0)
    def _():
        zeros[...] = jnp.zeros_like(zeros)

        def block_copy(start):
            return pltpu.make_async_copy(zeros, xs_ref.at[pl.ds(pl.multiple_of(start, EXPERT_BLOCK),
                                                                 EXPERT_BLOCK), :], zsem)

        def pad_loop(fn):
            def body(e, _):
                @pl.when(padded_ref[e] > 0)
                def _():
                    fn(block_copy(pend_ref[e] - EXPERT_BLOCK))
                return 0
            lax.fori_loop(0, N_EXPERTS, body, 0)

        def tail_loop(fn):
            def body(b, _):
                fn(block_copy(b * EXPERT_BLOCK))
                return 0
            lax.fori_loop(nu_ref[0], n_blocks, body, 0)

        pad_loop(lambda cp: cp.start())
        tail_loop(lambda cp: cp.start())
        pad_loop(lambda cp: cp.wait())
        tail_loop(lambda cp: cp.wait())

    i = pl.program_id(0)
    slot = i % 2

    def store(g):
        pltpu.make_async_copy(sbuf.at[slot, pl.ds(pl.multiple_of(g * GRANULE, GRANULE), GRANULE), :],
                              xs_ref.at[pl.ds(pl.multiple_of(gtab_ref[0, 0, g], GRANULE), GRANULE), :],
                              sem.at[slot]).start()

    def wait_tile(which, count):
        def wait_rows(rows):
            pltpu.make_async_copy(sbuf.at[which, pl.ds(0, rows), :], xs_ref.at[pl.ds(0, rows), :],
                                  sem.at[which]).wait()
        wait_rows(MIN_GRANULES * GRANULE)
        _wait_granules(count - MIN_GRANULES, wait_rows)

    @pl.when(i >= 2)
    def _():
        wait_tile(slot, gt_ref[jnp.maximum(i - 2, 0)])

    pos = pos_ref[...]
    lo, hi = _unpack_halves(hp_ref[...])
    lo = lo.astype(BF16)
    hi = hi.astype(BF16)
    r_iota = lax.broadcasted_iota(I32, (MOE_ROW_CHUNK, MOE_TILE), 0).astype(F32).astype(BF16)
    one = jnp.ones((), BF16)
    for c in range(MOE_SORTED_ROWS // MOE_ROW_CHUNK):
        rel = pos - c * MOE_ROW_CHUNK
        rel = jnp.where(jnp.logical_and(rel >= 0, rel < MOE_ROW_CHUNK), rel, -1).astype(F32)
        perm = jnp.zeros((MOE_ROW_CHUNK, MOE_TILE), BF16)
        for k in range(TOP_K):
            perm = jnp.where(r_iota == rel[k:k + 1, :].astype(BF16), one, perm)
        sbuf[slot, c * MOE_ROW_CHUNK:(c + 1) * MOE_ROW_CHUNK, :] = _pack_bf16_values(
            jnp.dot(perm, lo, preferred_element_type=F32), jnp.dot(perm, hi, preferred_element_type=F32))
        g_lo, g_hi = c * MOE_ROW_CHUNK // GRANULE, (c + 1) * MOE_ROW_CHUNK // GRANULE
        for g in range(g_lo, min(g_hi, MIN_GRANULES)):
            store(g)
        if g_hi > MIN_GRANULES:
            def body(g, carry):
                store(g)
                return carry
            lax.fori_loop(max(g_lo, MIN_GRANULES), jnp.clip(gt_ref[i], max(g_lo, MIN_GRANULES), g_hi), body, 0)

    @pl.when(i == n_tiles - 1)
    def _():
        wait_tile(slot, gt_ref[i])
        if n_tiles > 1:
            wait_tile(1 - slot, gt_ref[jnp.maximum(i - 1, 0)])


def _dispatch(hp, plan, n_rows):
    N, W = hp.shape
    n_tiles = N // MOE_TILE
    n_blocks = n_rows // EXPERT_BLOCK
    return pl.pallas_call(
        functools.partial(_dispatch_kernel, n_blocks=n_blocks, n_tiles=n_tiles),
        grid_spec=pltpu.PrefetchScalarGridSpec(
            num_scalar_prefetch=4, grid=(n_tiles,),
            in_specs=[pl.BlockSpec((1, 1, MOE_TILE), lambda i, *_: (i, 0, 0), memory_space=pltpu.SMEM),
                      pl.BlockSpec((8, MOE_TILE), lambda i, *_: (0, i)),
                      pl.BlockSpec((MOE_TILE, W), lambda i, *_: (i, 0))],
            out_specs=pl.BlockSpec(memory_space=pl.ANY),
            scratch_shapes=[pltpu.VMEM((EXPERT_BLOCK, W), jnp.uint32),
                            pltpu.VMEM((2, MOE_SORTED_ROWS, W), jnp.uint32),
                            pltpu.SemaphoreType.DMA((2,)), pltpu.SemaphoreType.DMA]),
        out_shape=jax.ShapeDtypeStruct((n_rows, W), jnp.uint32),
        compiler_params=_cparams(("arbitrary",)),
        name="dispatch",
    )(plan["pad_end"], plan["padded"], plan["n_used"], plan["g_tile"], plan["gtab"], plan["pos"], hp)


def _experts_kernel(be_ref, nu_ref, xs_ref, wgf_ref, wuf_ref, wdf_ref, ys_ref, wg_ref, wu_ref, wd_ref,
                    xbuf, xsem):
    i = pl.program_id(0)
    n_live = nu_ref[0]
    live = i < n_live
    new_expert = jnp.logical_or(i == 0, be_ref[i] != be_ref[jnp.maximum(i - 1, 0)])

    def fetch(b):
        ring = lax.rem(b, EXPERT_RING)
        return pltpu.make_async_copy(xs_ref.at[pl.ds(pl.multiple_of(b * EXPERT_BLOCK, EXPERT_BLOCK), EXPERT_BLOCK), :],
                                     xbuf.at[ring], xsem.at[ring])

    @pl.when(i == 0)
    def _():
        for b in range(EXPERT_RING - 1):
            @pl.when(b < n_live)
            def _():
                fetch(b).start()

    @pl.when(i + EXPERT_RING - 1 < n_live)
    def _():
        fetch(i + EXPERT_RING - 1).start()

    @pl.when(jnp.logical_not(live))
    def _():
        ys_ref[...] = jnp.zeros_like(ys_ref)

    @pl.when(jnp.logical_and(live, new_expert))
    def _():
        wg_ref[...] = wgf_ref[...].astype(BF16)
        wu_ref[...] = wuf_ref[...].astype(BF16)
        wd_ref[...] = wdf_ref[...].astype(BF16)

    @pl.when(live)
    def _():
        half = D_MODEL // 2
        fetch(i).wait()
        lo, hi = _unpack_halves(xbuf[lax.rem(i, EXPERT_RING)])
        lo = lo.astype(BF16)
        hi = hi.astype(BF16)
        gate = (jnp.dot(lo, wg_ref[0:half, :], preferred_element_type=F32)
                + jnp.dot(hi, wg_ref[half:D_MODEL, :], preferred_element_type=F32))
        up = (jnp.dot(lo, wu_ref[0:half, :], preferred_element_type=F32)
              + jnp.dot(hi, wu_ref[half:D_MODEL, :], preferred_element_type=F32))
        y = jnp.dot((_silu(gate) * up).astype(BF16), wd_ref[...], preferred_element_type=F32)
        ys_ref[...] = _pack_halves(y)


def _experts(xs, n_rows, blk_e, n_used, wg, wu, wd):
    W = xs.shape[1]
    n_blocks = n_rows // EXPERT_BLOCK
    return pl.pallas_call(
        _experts_kernel,
        grid_spec=pltpu.PrefetchScalarGridSpec(
            num_scalar_prefetch=2, grid=(n_blocks,),
            in_specs=[pl.BlockSpec(memory_space=pl.ANY),
                      pl.BlockSpec((None, D_MODEL, D_EXPERT), lambda i, be, nu: (be[i], 0, 0)),
                      pl.BlockSpec((None, D_MODEL, D_EXPERT), lambda i, be, nu: (be[i], 0, 0)),
                      pl.BlockSpec((None, D_EXPERT, D_MODEL), lambda i, be, nu: (be[i], 0, 0))],
            out_specs=pl.BlockSpec((EXPERT_BLOCK, W), lambda i, be, nu: (i, 0)),
            scratch_shapes=[pltpu.VMEM((D_MODEL, D_EXPERT), BF16), pltpu.VMEM((D_MODEL, D_EXPERT), BF16),
                            pltpu.VMEM((D_EXPERT, D_MODEL), BF16),
                            pltpu.VMEM((EXPERT_RING, EXPERT_BLOCK, W), jnp.uint32),
                            pltpu.SemaphoreType.DMA((EXPERT_RING,))]),
        out_shape=jax.ShapeDtypeStruct((n_rows, W), jnp.uint32),
        compiler_params=_cparams(("arbitrary",)),
        name="experts",
    )(blk_e, n_used, xs, wg, wu, wd)


def _combine_kernel(gt_ref, gtab_ref, gtab_next_ref, ys_ref, pos_ref, xs_ref, wts_ref, g2_ref, fg_ref, o_ref,
                    buf, sem, *, n_tiles):
    i = pl.program_id(0)
    slot = i % 2
    nxt = jnp.minimum(i + 1, n_tiles - 1)

    def fetch(which, table_ref, g):
        pltpu.make_async_copy(ys_ref.at[pl.ds(pl.multiple_of(table_ref[0, 0, g], GRANULE), GRANULE), :],
                              buf.at[which, pl.ds(pl.multiple_of(g * GRANULE, GRANULE), GRANULE), :],
                              sem.at[which]).start()

    def fetch_rest(which, table_ref, count):
        def body(g, c):
            fetch(which, table_ref, g)
            return c
        lax.fori_loop(MIN_GRANULES, count, body, 0)

    def wait_tile(which, count):
        def wait_rows(rows):
            pltpu.make_async_copy(ys_ref.at[pl.ds(0, rows), :], buf.at[which, pl.ds(0, rows), :],
                                  sem.at[which]).wait()
        wait_rows(MIN_GRANULES * GRANULE)
        _wait_granules(count - MIN_GRANULES, wait_rows)

    @pl.when(i == 0)
    def _():
        buf[...] = jnp.zeros_like(buf)
        for g in range(MIN_GRANULES):
            fetch(0, gtab_ref, g)
        fetch_rest(0, gtab_ref, gt_ref[0])

    wait_tile(slot, gt_ref[i])

    pos = pos_ref[...]
    w = wts_ref[...]
    c_iota = lax.broadcasted_iota(I32, (MOE_TILE, MOE_ROW_CHUNK), 1).astype(F32).astype(BF16)
    half = D_MODEL // 2
    r_lo = jnp.zeros((MOE_TILE, half), F32)
    r_hi = jnp.zeros((MOE_TILE, half), F32)
    for c in range(MOE_SORTED_ROWS // MOE_ROW_CHUNK):
        rel = pos - c * MOE_ROW_CHUNK
        rel = jnp.where(jnp.logical_and(rel >= 0, rel < MOE_ROW_CHUNK), rel, -1).astype(F32)
        wb = jnp.zeros((MOE_TILE, MOE_ROW_CHUNK), BF16)
        for k in range(TOP_K):
            wb = jnp.where(c_iota == rel[:, k:k + 1].astype(BF16), w[:, k:k + 1].astype(BF16), wb)
        lo, hi = _unpack_halves(buf[slot, c * MOE_ROW_CHUNK:(c + 1) * MOE_ROW_CHUNK, :])
        r_lo = r_lo + jnp.dot(wb, lo.astype(BF16), preferred_element_type=F32)
        r_hi = r_hi + jnp.dot(wb, hi.astype(BF16), preferred_element_type=F32)
        for g in range(min(c * FETCH_PER_CHUNK, MIN_GRANULES), min((c + 1) * FETCH_PER_CHUNK, MIN_GRANULES)):
            fetch(1 - slot, gtab_next_ref, g)
        if c == MIN_GRANULES // FETCH_PER_CHUNK:
            fetch_rest(1 - slot, gtab_next_ref, gt_ref[nxt])
    x3 = xs_ref[...] + g2_ref[...] * jnp.concatenate([r_lo, r_hi], axis=1)
    y = x3 * lax.rsqrt(jnp.mean(x3 * x3, axis=-1, keepdims=True) + RMS_EPS) * fg_ref[...]
    o_ref[...] = y

    @pl.when(i == n_tiles - 1)
    def _():
        wait_tile(1 - slot, gt_ref[nxt])


def _combine(ys, plan, xsr, wts_t, mod4, final_g, T):
    N, D = xsr.shape
    W = ys.shape[1]
    tm = MOE_TILE
    tps = T // tm
    n_tiles = N // tm
    tile_tab = lambda off: pl.BlockSpec((1, 1, MOE_TILE), lambda i, gt: (jnp.minimum(i + off, n_tiles - 1), 0, 0),
                                        memory_space=pltpu.SMEM)
    return pl.pallas_call(
        functools.partial(_combine_kernel, n_tiles=n_tiles),
        grid_spec=pltpu.PrefetchScalarGridSpec(
            num_scalar_prefetch=1, grid=(n_tiles,),
            in_specs=[tile_tab(0), tile_tab(1),
                      pl.BlockSpec(memory_space=pl.ANY),
                      pl.BlockSpec((tm, 8), lambda i, gt: (i, 0)),
                      pl.BlockSpec((tm, D), lambda i, gt: (i, 0)),
                      pl.BlockSpec((tm, 8), lambda i, gt: (i, 0)),
                      pl.BlockSpec((None, None, 1, D), lambda i, gt: (i // tps, 5, 0, 0)),
                      pl.BlockSpec((1, D), lambda i, gt: (0, 0))],
            out_specs=pl.BlockSpec((tm, D), lambda i, gt: (i, 0)),
            scratch_shapes=[pltpu.VMEM((2, MOE_SORTED_ROWS, W), jnp.uint32), pltpu.SemaphoreType.DMA((2,))]),
        out_shape=jax.ShapeDtypeStruct((N, D), F32),
        compiler_params=_cparams(("arbitrary",)),
        name="combine",
    )(plan["g_tile"], plan["gtab"], plan["gtab"], ys, plan["pos"].T, xsr, wts_t, mod4,
      final_g.reshape(1, D))


def _pack_w_in(w_in):
    D = w_in.shape[0]
    fox_cols = 4 * HALF + 3 * N_HEADS
    wf = w_in[:, :fox_cols]
    wr = w_in[:, fox_cols:]
    o = 3 * HALF
    pad = lambda a, n: jnp.concatenate([a, jnp.zeros((D, n - a.shape[1]), a.dtype)], axis=1)
    parts = [wf[:, :4 * HALF], wr[:, :o],
             pad(wf[:, 4 * HALF:], LANES),
             pad(wr[:, o:o + DECAY_LORA], LANES),
             pad(wr[:, o + DECAY_LORA:o + DECAY_LORA + ICLR_LORA], LANES),
             pad(wr[:, o + DECAY_LORA + ICLR_LORA:], 2 * LANES)]
    return jnp.concatenate(parts, axis=1).astype(BF16)


def _pack_mu(mu):
    o = 3 * HALF
    pad = lambda a, n: jnp.concatenate([a, jnp.zeros((n - a.shape[0],), a.dtype)])
    small = jnp.concatenate([jnp.zeros((LANES,), mu.dtype),
                             pad(mu[o:o + DECAY_LORA], LANES),
                             pad(mu[o + DECAY_LORA:o + DECAY_LORA + ICLR_LORA], LANES),
                             pad(mu[o + DECAY_LORA + ICLR_LORA:], 2 * LANES)])
    return mu[:o].reshape(1, o), small.reshape(1, Z_SMALL)


def kernel(x, c, norm1_g, norm2_g, ada_w, ada_b, w_in, w_out, fox_qn_g, fox_kn_g, fox_on_g, fox_forget_b,
           rw_mu, rw_w0, rw_decay_up, rw_a0, rw_iclr_up, rw_gate_up, rw_k_k, rw_k_a, rw_r_k, rw_lnx_g,
           rw_lnx_b, router_w, router_bias, exp_w_gate, exp_w_up, exp_w_down, sh_w_gate, sh_w_up,
           sh_w_down, final_g):
    B, T, D = x.shape
    N = B * T
    depth = norm1_g.shape[0]
    assert depth == 1, "the combine kernel fuses the final RMSNorm, so exactly one layer is supported"
    xf = x.reshape(N, D)
    for l in range(depth):
        mod4 = _ada(c, ada_w[l], ada_b[l]).reshape(B, 6, 1, D)
        mu_big, mu_small = _pack_mu(rw_mu[l])
        zm, zs = _inproj(xf, mod4, norm1_g[l], _pack_w_in(w_in[l]), mu_big, mu_small, T)
        qp, kp, vp = _foxprep(zm, zs, fox_forget_b[l], fox_qn_g[l], fox_kn_g[l], T)
        y_fox = _attention(qp, kp, vp, zm, fox_on_g[l], B, T)
        rw = dict(rw_w0=rw_w0[l], rw_decay_up=rw_decay_up[l], rw_a0=rw_a0[l], rw_iclr_up=rw_iclr_up[l],
                  rw_gate_up=rw_gate_up[l], rw_k_k=rw_k_k[l], rw_k_a=rw_k_a[l], rw_r_k=rw_r_k[l],
                  rw_lnx_g=rw_lnx_g[l], rw_lnx_b=rw_lnx_b[l])
        y_rwkv = _rwkv(zm, zs, rw, B, T)
        xsr, hp, idx_t, wts_t, rank_t = _outproj(
            xf, y_fox, y_rwkv, mod4, norm2_g[l], w_out[l], router_w[l], router_bias[l],
            sh_w_gate[l], sh_w_up[l], sh_w_down[l], T)
        plan, n_rows = _moe_plan(idx_t, rank_t)
        xs = _dispatch(hp, plan, n_rows)
        ys = _experts(xs, n_rows, plan["blk_e"], plan["n_used"], exp_w_gate[l], exp_w_up[l], exp_w_down[l])
        xf = _combine(ys, plan, xsr, wts_t.T, mod4, final_g, T)
    return xf.reshape(B, T, D)
```

```python
import functools

import jax
import jax.numpy as jnp
import numpy as np
from jax import lax
from jax.experimental import pallas as pl
from jax.experimental.pallas import tpu as pltpu

F32 = jnp.float32
BF16 = jnp.bfloat16
I32 = jnp.int32
HIGHEST = lax.Precision.HIGHEST

D_MODEL = 1024
HEAD_DIM = 64
N_HEADS = 8
HALF = N_HEADS * HEAD_DIM
RMS_EPS = 1e-6
LNX_EPS = 64e-5
LOG2E = 1.4426950408889634
DECAY_LORA = 64
ICLR_LORA = 64
N_EXPERTS = 64
N_GROUPS = 8
GROUP_SIZE = N_EXPERTS // N_GROUPS
TOPK_GROUPS = 4
TOP_K = 6
D_EXPERT = 256
ROUTED_SCALE = 2.5
EXPERT_BLOCK = 1024
EXPERT_RING = 3

LANES = 128
Z_MAIN = 4 * HALF + 3 * HALF
Z_SMALL = 5 * LANES
VMEM_LIMIT = 56 * 1024 * 1024
ATTN_BLOCK = 512
ATTN_HEADS = 4
WAIT_CHUNK = 16
MOE_TILE = 256
GRANULE = 8
MOE_SORTED_ROWS = MOE_TILE * TOP_K + N_EXPERTS * GRANULE
MIN_GRANULES = MOE_TILE * TOP_K // GRANULE
FETCH_PER_CHUNK = 96
MOE_ROW_CHUNK = 256
RWKV_CHUNK = 64
RWKV_SEQS_PER_STEP = 4


def _cparams(semantics):
    return pltpu.CompilerParams(dimension_semantics=semantics, vmem_limit_bytes=VMEM_LIMIT)


def _mm(a, b):
    return jnp.dot(a.astype(BF16), b.astype(BF16), preferred_element_type=F32)


def _mm_nt(a, b):
    return lax.dot_general(a.astype(BF16), b.astype(BF16), (((1,), (1,)), ((), ())),
                           preferred_element_type=F32)


def _mm_f32(a, b):
    return jnp.dot(a, b, precision=HIGHEST, preferred_element_type=F32)


def _bf16_pieces(x, passes):
    pieces = []
    for _ in range(passes):
        piece = x.astype(BF16)
        pieces.append(piece)
        x = x - piece.astype(F32)
    return pieces


def _mm_split(m01, x, passes=3):
    return sum(jnp.dot(m01, p, preferred_element_type=F32) for p in _bf16_pieces(x, passes))


def _mm_split_r(x, m01, passes=2):
    return sum(jnp.dot(p, m01, preferred_element_type=F32) for p in _bf16_pieces(x, passes))


def _sigmoid(x):
    return 1.0 / (1.0 + jnp.exp(-x))


def _softplus(x):
    return jnp.maximum(x, 0.0) + jnp.log(1.0 + jnp.exp(-jnp.abs(x)))


def _silu(x):
    return x * _sigmoid(x)


def _pack_halves(x):
    w = x.shape[1] // 2
    bits = lambda t: lax.bitcast_convert_type(t.astype(jnp.bfloat16).astype(F32), jnp.uint32)
    return (bits(x[:, 0:w]) >> 16) | (bits(x[:, w:2 * w]) & jnp.uint32(0xFFFF0000))


def _pack_bf16_values(lo, hi):
    return (lax.bitcast_convert_type(lo, jnp.uint32) >> 16) | lax.bitcast_convert_type(hi, jnp.uint32)


def _unpack_halves(p):
    lo = lax.bitcast_convert_type(p << 16, F32)
    hi = lax.bitcast_convert_type(p & jnp.uint32(0xFFFF0000), F32)
    return lo, hi


def _shift_rows(z, carry_ref, first):
    rows = z.shape[0]
    prev_row = jnp.where(first, 0.0, carry_ref[0:1, :])
    prev = pltpu.roll(z, 1, 0)
    row0 = lax.broadcasted_iota(I32, (rows, 1), 0) == 0
    prev = jnp.where(row0, prev_row, prev)
    carry_ref[0:1, :] = z[rows - 1:rows, :]
    return prev


def _ada_kernel(c_ref, w_ref, b_ref, o_ref):
    o_ref[...] = _mm_f32(_silu(c_ref[...]), w_ref[...]) + b_ref[...]


def _ada(c, ada_w, ada_b):
    B, D = c.shape
    n_out = ada_w.shape[1]
    tn = 512
    return pl.pallas_call(
        _ada_kernel,
        grid=(n_out // tn,),
        in_specs=[pl.BlockSpec((B, D), lambda j: (0, 0)),
                  pl.BlockSpec((D, tn), lambda j: (0, j)),
                  pl.BlockSpec((1, tn), lambda j: (0, j))],
        out_specs=pl.BlockSpec((B, tn), lambda j: (0, j)),
        out_shape=jax.ShapeDtypeStruct((B, n_out), F32),
        compiler_params=_cparams(("arbitrary",)),
        name="ada",
    )(c, ada_w, ada_b.reshape(1, n_out))


def _inproj_kernel(x_ref, g_ref, sh_ref, sc_ref, w_ref, mub_ref, mus_ref, zm_ref, zs_ref,
                   carry_b, carry_s, *, tiles_per_seq):
    first = (pl.program_id(0) % tiles_per_seq) == 0
    x = x_ref[...]
    h = x * lax.rsqrt(jnp.mean(x * x, axis=-1, keepdims=True) + RMS_EPS) * g_ref[...]
    hb = (h * (1.0 + sc_ref[...]) + sh_ref[...]).astype(BF16)
    nf = 4 * HALF
    zm_ref[:, 0:nf] = jnp.dot(hb, w_ref[:, 0:nf], preferred_element_type=F32).astype(BF16)
    zr = jnp.dot(hb, w_ref[:, nf:Z_MAIN], preferred_element_type=F32)
    zr = zr + mub_ref[...] * (_shift_rows(zr, carry_b, first) - zr)
    zm_ref[:, nf:Z_MAIN] = zr.astype(BF16)
    zs = jnp.dot(hb, w_ref[:, Z_MAIN:Z_MAIN + Z_SMALL], preferred_element_type=F32)
    zs_ref[...] = zs + mus_ref[...] * (_shift_rows(zs, carry_s, first) - zs)


def _inproj(x2d, mod4, norm_g, w_all, mu_big, mu_small, T):
    N, D = x2d.shape
    tm = min(512, T)
    tps = T // tm
    vec = lambda j: pl.BlockSpec((None, None, 1, D), lambda i: (i // tps, j, 0, 0))
    return pl.pallas_call(
        functools.partial(_inproj_kernel, tiles_per_seq=tps),
        grid=(N // tm,),
        in_specs=[pl.BlockSpec((tm, D), lambda i: (i, 0)),
                  pl.BlockSpec((1, D), lambda i: (0, 0)),
                  vec(0), vec(1),
                  pl.BlockSpec((D, Z_MAIN + Z_SMALL), lambda i: (0, 0)),
                  pl.BlockSpec((1, 3 * HALF), lambda i: (0, 0)),
                  pl.BlockSpec((1, Z_SMALL), lambda i: (0, 0))],
        out_specs=[pl.BlockSpec((tm, Z_MAIN), lambda i: (i, 0)),
                   pl.BlockSpec((tm, Z_SMALL), lambda i: (i, 0))],
        out_shape=[jax.ShapeDtypeStruct((N, Z_MAIN), BF16),
                   jax.ShapeDtypeStruct((N, Z_SMALL), F32)],
        scratch_shapes=[pltpu.VMEM((8, 3 * HALF), F32), pltpu.VMEM((8, Z_SMALL), F32)],
        compiler_params=_cparams(("arbitrary",)),
        name="inproj",
    )(x2d, norm_g.reshape(1, D), mod4, mod4, w_all, mu_big, mu_small)


def _foxprep_kernel(q_ref, k_ref, v_ref, zs_ref, fb_ref, qg_ref, kg_ref, tri_ref, eexp_ref,
                    esum_ref, e8_ref, plq_ref, plk_ref, plv_ref, cq_ref, ck_ref, cv_ref,
                    qp_ref, kp_ref, vp_ref, carry_k, carry_v, carry_c, *, tiles_per_seq):
    first = (pl.program_id(0) % tiles_per_seq) == 0
    zs = zs_ref[...]
    logf = -_softplus(-(zs + fb_ref[...]))
    cum = _mm_split(tri_ref[...], logf) + jnp.where(first, 0.0, carry_c[0:1, :])
    carry_c[0:1, :] = cum[cum.shape[0] - 1:, :]
    cum2 = cum * LOG2E
    head_lane = lax.broadcasted_iota(I32, cum2.shape, 1) < N_HEADS
    c_hi, c_mid, c_lo = [jnp.where(head_lane, p.astype(F32), 0.0) for p in _bf16_pieces(cum2, 3)]
    aug = (c_hi + pltpu.roll(c_mid, N_HEADS, 1) + pltpu.roll(c_lo, 2 * N_HEADS, 1)).astype(BF16)
    a_full = _mm_split_r(_sigmoid(zs), eexp_ref[...])
    k = k_ref[...].astype(F32)
    v = v_ref[...].astype(F32)
    a_k = a_full[:, 0:HALF]
    a_v = a_full[:, HALF:2 * HALF]
    k = a_k * _shift_rows(k, carry_k, first) + (1.0 - a_k) * k
    v = a_v * _shift_rows(v, carry_v, first) + (1.0 - a_v) * v
    q = q_ref[...].astype(F32)

    def head_rms(t, gain):
        ms = _mm_split_r(t * t, esum_ref[...]) * (1.0 / HEAD_DIM)
        inv = _mm_split_r(lax.rsqrt(ms + RMS_EPS), e8_ref[...])
        return t * inv * gain

    qn = head_rms(q, qg_ref[...]) * (HEAD_DIM ** -0.5 * LOG2E)
    kn = head_rms(k, kg_ref[...])
    lhs_q = jnp.concatenate([qn.astype(BF16), aug], axis=1)
    lhs_k = jnp.concatenate([kn.astype(BF16), aug], axis=1)
    qp_ref[...] = (jnp.dot(lhs_q, plq_ref[...], preferred_element_type=F32) + cq_ref[...]).astype(BF16)
    kp_ref[...] = (jnp.dot(lhs_k, plk_ref[...], preferred_element_type=F32) + ck_ref[...]).astype(BF16)
    vp_ref[...] = (jnp.dot(v.astype(BF16), plv_ref[...], preferred_element_type=F32)
                   + cv_ref[...]).astype(BF16)


def _fox_constants(tm):
    hp = N_HEADS * LANES
    eexp = np.zeros((LANES, 2 * HALF), np.float32)
    esum = np.zeros((HALF, LANES), np.float32)
    e8 = np.zeros((LANES, HALF), np.float32)
    plq = np.zeros((HALF + LANES, hp), np.float32)
    plk = np.zeros((HALF + LANES, hp), np.float32)
    plv = np.zeros((HALF, hp), np.float32)
    cq = np.zeros((1, hp), np.float32)
    ck = np.zeros((1, hp), np.float32)
    cv = np.zeros((1, hp), np.float32)
    for h in range(N_HEADS):
        sl = slice(h * HEAD_DIM, (h + 1) * HEAD_DIM)
        eexp[8 + h, sl] = 1.0
        eexp[16 + h, HALF + h * HEAD_DIM:HALF + (h + 1) * HEAD_DIM] = 1.0
        esum[sl, h] = 1.0
        e8[h, sl] = 1.0
        base = h * LANES
        for d in range(HEAD_DIM):
            plq[h * HEAD_DIM + d, base + d] = 1.0
            plk[h * HEAD_DIM + d, base + d] = 1.0
            plv[h * HEAD_DIM + d, base + d] = 1.0
        for j in range(3):
            plq[HALF + j * N_HEADS + h, base + HEAD_DIM + j] = 1.0
            plk[HALF + j * N_HEADS + h, base + HEAD_DIM + 3 + j] = -1.0
            cq[0, base + HEAD_DIM + 3 + j] = 1.0
            ck[0, base + HEAD_DIM + j] = 1.0
        cv[0, base + HEAD_DIM] = 1.0
    tri = np.tril(np.ones((tm, tm), np.float32))
    bf = lambda a: jnp.asarray(a, BF16)
    return dict(tri=bf(tri), eexp=bf(eexp), esum=bf(esum), e8=bf(e8),
                plq=bf(plq), plk=bf(plk), plv=bf(plv), cq=jnp.asarray(cq), ck=jnp.asarray(ck),
                cv=jnp.asarray(cv))


def _foxprep(zm, zs, forget_b, qn_g, kn_g, T):
    N = zm.shape[0]
    tm = min(512, T)
    tps = T // tm
    cst = _fox_constants(tm)
    hp = N_HEADS * LANES
    fb = jnp.zeros((1, LANES), F32).at[0, :N_HEADS].set(forget_b)
    full = lambda a: pl.BlockSpec(a.shape, lambda i: (0, 0))
    consts = [fb, qn_g.reshape(1, HALF), kn_g.reshape(1, HALF), cst["tri"], cst["eexp"], cst["esum"],
              cst["e8"], cst["plq"], cst["plk"], cst["plv"], cst["cq"], cst["ck"], cst["cv"]]
    return pl.pallas_call(
        functools.partial(_foxprep_kernel, tiles_per_seq=tps),
        grid=(N // tm,),
        in_specs=[pl.BlockSpec((tm, HALF), lambda i: (i, 0)),
                  pl.BlockSpec((tm, HALF), lambda i: (i, 1)),
                  pl.BlockSpec((tm, HALF), lambda i: (i, 2)),
                  pl.BlockSpec((tm, LANES), lambda i: (i, 0))] + [full(a) for a in consts],
        out_specs=[pl.BlockSpec((tm, hp), lambda i: (i, 0))] * 3,
        out_shape=[jax.ShapeDtypeStruct((N, hp), BF16)] * 3,
        scratch_shapes=[pltpu.VMEM((8, HALF), F32), pltpu.VMEM((8, HALF), F32),
                        pltpu.VMEM((8, LANES), F32)],
        compiler_params=_cparams(("arbitrary",)),
        name="foxprep",
    )(zm, zm, zm, zs, *consts)


def _attn_kernel(q_ref, k_ref, v_ref, g_ref, ong_ref, o_ref, vt_ref, *, tq):
    i = pl.program_id(2)
    n_kv = vt_ref.shape[1]
    heads = range(ATTN_HEADS)
    lanes = [slice(hh * LANES, (hh + 1) * LANES) for hh in heads]

    @pl.when(i == 0)
    def _():
        for hh in heads:
            for c in range(n_kv):
                vt_ref[hh, c] = v_ref[c * tq:(c + 1) * tq, lanes[hh]].T

    key = lax.broadcasted_iota(I32, (tq, tq), 0)
    qry = lax.broadcasted_iota(I32, (tq, tq), 1)
    causal = key <= qry
    qs = [q_ref[:, lanes[hh]] for hh in heads]

    def step(j, carry, masked):
        m, acc = carry
        start = pl.multiple_of(j * tq, tq)
        s = [lax.dot_general(k_ref[pl.ds(start, tq), lanes[hh]], qs[hh], (((1,), (1,)), ((), ())),
                             preferred_element_type=F32) for hh in heads]
        m_out, acc_out = [], []
        for hh in heads:
            sh = jnp.where(causal, s[hh], -jnp.inf) if masked else s[hh]
            m_new = jnp.maximum(m[hh], jnp.max(sh, axis=0, keepdims=True))
            p = jnp.exp2(sh - m_new)
            acc_out.append(jnp.exp2(m[hh] - m_new) * acc[hh]
                           + jnp.dot(vt_ref[hh, j], p.astype(BF16), preferred_element_type=F32))
            m_out.append(m_new)
        return tuple(m_out), tuple(acc_out)

    init = (tuple(jnp.full((1, tq), -jnp.inf, F32) for _ in heads),
            tuple(jnp.zeros((LANES, tq), F32) for _ in heads))
    carry = lax.fori_loop(0, i, functools.partial(step, masked=False), init)
    _, acc = step(i, carry, True)
    chan = lax.broadcasted_iota(I32, (LANES, tq), 0)
    chan_w = jnp.where(chan < HEAD_DIM, 1.0 / HEAD_DIM, jnp.where(chan == HEAD_DIM, RMS_EPS, 0.0))
    lane = lax.broadcasted_iota(I32, (tq, LANES), 1)
    outs = []
    for hh in heads:
        t = jnp.sum(acc[hh] * acc[hh] * chan_w, axis=0, keepdims=True)
        outs.append((acc[hh] * lax.rsqrt(t)).T)
    o = jnp.concatenate([jnp.where(lane < HEAD_DIM, outs[2 * p], pltpu.roll(outs[2 * p + 1], HEAD_DIM, 1))
                         for p in range(ATTN_HEADS // 2)], axis=1)
    y = o * ong_ref[...] * _sigmoid(g_ref[...].astype(F32))
    o_ref[...] = y.astype(BF16)


def _attention(qp, kp, vp, zm, on_g, B, T):
    N = qp.shape[0]
    tq = min(ATTN_BLOCK, T)
    nq = T // tq
    groups = N_HEADS // ATTN_HEADS
    wp = ATTN_HEADS * LANES
    wo = ATTN_HEADS * HEAD_DIM
    g_col0 = 3 * HALF // wo
    return pl.pallas_call(
        functools.partial(_attn_kernel, tq=tq),
        grid=(B, groups, nq),
        in_specs=[pl.BlockSpec((tq, wp), lambda b, p, i: (b * nq + i, p)),
                  pl.BlockSpec((T, wp), lambda b, p, i: (b, p)),
                  pl.BlockSpec((T, wp), lambda b, p, i: (b, p)),
                  pl.BlockSpec((tq, wo), lambda b, p, i: (b * nq + i, g_col0 + p)),
                  pl.BlockSpec((None, 1, wo), lambda b, p, i: (p, 0, 0))],
        out_specs=pl.BlockSpec((tq, wo), lambda b, p, i: (b * nq + i, p)),
        out_shape=jax.ShapeDtypeStruct((N, HALF), BF16),
        scratch_shapes=[pltpu.VMEM((ATTN_HEADS, nq, LANES, tq), BF16)],
        compiler_params=_cparams(("arbitrary", "arbitrary", "arbitrary")),
        name="attn",
    )(qp, kp, vp, zm, on_g.reshape(groups, 1, wo))


def _rwkv_kernel(r_ref, k_ref, v_ref, zs_ref, w0_ref, dup_ref, a0_ref, iup_ref, gup_ref, kk_ref,
                 ka_ref, rk_ref, lng_ref, lnb_ref, tri_ref, o_ref, state, *, chunk, nb):
    C = chunk
    c_idx = pl.program_id(1)

    @pl.when(c_idx == 0)
    def _():
        state[...] = jnp.zeros_like(state)

    R = nb * C
    r = r_ref[...].reshape(R, HALF).astype(F32)
    k = k_ref[...].reshape(R, HALF).astype(F32)
    v = v_ref[...].reshape(R, HALF).astype(F32)
    zs = zs_ref[...].reshape(R, Z_SMALL)
    wd = zs[:, LANES:2 * LANES]
    ad = zs[:, 2 * LANES:3 * LANES]
    gd = zs[:, 3 * LANES:5 * LANES]
    wl = w0_ref[...] + _mm(jnp.tanh(wd), dup_ref[...])
    lw = -jnp.exp(-_softplus(-wl) - 0.5)
    a = _sigmoid(a0_ref[...] + _mm(ad, iup_ref[...]))
    g = _mm(_sigmoid(gd), gup_ref[...])
    kk = k * kk_ref[...]
    k2 = k * (1.0 + (a - 1.0) * ka_ref[...])
    cl = _mm_split(tri_ref[...], lw)
    cl_end = jnp.concatenate(
        [jnp.broadcast_to(cl[(bi + 1) * C - 1:(bi + 1) * C, :], (C, HALF)) for bi in range(nb)], axis=0)
    e_neg = jnp.exp(-cl)
    e_tail = jnp.exp(cl_end - cl)
    pre = dict(rt=r * jnp.exp(cl), at=-kk * jnp.exp(cl - lw), kh=k2 * e_neg, bh=kk * a * e_neg,
               kb=k2 * e_tail, bb=kk * a * e_tail, v=v, kk=kk, pend=jnp.exp(cl_end),
               rkr=r * k2 * rk_ref[...], g=g)

    C2 = 2 * C
    row = lax.broadcasted_iota(I32, (C2, C2), 0)
    col = lax.broadcasted_iota(I32, (C2, C2), 1)
    lower = (col & (C - 1)) <= (row & (C - 1))
    strict = (col & (C - 1)) < (row & (C - 1))
    eye = row == col
    head0 = lax.broadcasted_iota(I32, (C, LANES), 1) < HEAD_DIM
    n_sq = int(np.log2(C)) - 1
    units = [(bi, p) for bi in range(nb) for p in range(N_HEADS // 2)]

    def part(name, bi, p):
        return pre[name][bi * C:(bi + 1) * C, p * LANES:(p + 1) * LANES]

    def stack(x):
        return jnp.concatenate([jnp.where(head0, x, 0.0), jnp.where(head0, 0.0, x)], axis=0)

    def unstack(x):
        return x[0:C] + x[C:C2]

    def head_sum(x):
        s0 = jnp.sum(jnp.where(head0, x, 0.0), axis=1, keepdims=True)
        s1 = jnp.sum(jnp.where(head0, 0.0, x), axis=1, keepdims=True)
        return jnp.where(head0, s0, s1)

    ops = []
    for bi, p in units:
        kk_p = part("kk", bi, p)
        inv = 1.0 / jnp.maximum(jnp.sqrt(head_sum(kk_p * kk_p)), 1e-12)
        ops.append(dict(rt=stack(part("rt", bi, p)), at=stack(part("at", bi, p) * inv),
                        kh=stack(part("kh", bi, p)).astype(BF16), bh=stack(part("bh", bi, p) * inv).astype(BF16),
                        kb=stack(part("kb", bi, p)).astype(BF16), bb=stack(part("bb", bi, p) * inv).astype(BF16),
                        v=stack(part("v", bi, p)).astype(BF16)))
    ra = [jnp.concatenate([o["rt"], o["at"]], axis=0).astype(BF16) for o in ops]
    gg = [_mm_nt(x, jnp.concatenate([o["kh"], o["bh"]], axis=0)) for x, o in zip(ra, ops)]
    g1 = [t[:, 0:C2] for t in gg]
    g2 = [t[:, C2:2 * C2] for t in gg]
    a_rb = [jnp.where(lower, t[0:C2], 0.0).astype(BF16) for t in g2]
    pw = [jnp.where(strict, t[C2:2 * C2], 0.0) for t in g2]
    av = [_mm(jnp.concatenate([jnp.where(lower, t[0:C2], 0.0).astype(BF16),
                               jnp.where(strict, t[C2:2 * C2], 0.0).astype(BF16), o["kb"].T], axis=0),
              o["v"]) for t, o in zip(g1, ops)]
    swap = lambda t: pltpu.roll(t, HEAD_DIM, 1)
    own = (row < C) == (col < HEAD_DIM)
    xs = [o["at"] + swap(t[C2:2 * C2]) for o, t in zip(ops, av)]
    for level in range(n_sq + 1):
        pb = [p.astype(BF16) for p in pw]
        xs = [x + _mm(p, x) for p, x in zip(pb, xs)]
        if level < n_sq:
            pw = [jnp.dot(p, p, preferred_element_type=F32) for p in pb]
    rbx = [_mm(jnp.concatenate([p, o["bb"].T], axis=0), x) for p, o, x in zip(a_rb, ops, xs)]
    for u, (bi, p) in enumerate(units):
        o = ops[u]
        sl = slice(p * LANES, (p + 1) * LANES)
        rb, bx, kv = rbx[u][0:C2], rbx[u][C2:2 * C2], av[u][2 * C2:3 * C2]
        r2 = unstack(o["rt"] + jnp.where(own, rb, 0.0))
        y0 = swap(unstack(jnp.where(own, 0.0, rb))) + unstack(av[u][0:C2])
        m_mat = jnp.where(eye, part("pend", bi, p)[0:1, :], 0.0) + jnp.where(own, bx, 0.0)
        g_mat = swap(jnp.where(own, 0.0, bx)) + kv
        out = _mm(jnp.concatenate([r2, m_mat], axis=0), state[bi, p])
        state[bi, p] = out[C:C + LANES] + g_mat
        y = out[0:C] + y0
        cen = y - head_sum(y) * (1.0 / HEAD_DIM)
        var = head_sum(cen * cen) * (1.0 / HEAD_DIM)
        bonus = head_sum(part("rkr", bi, p)) * part("v", bi, p)
        y = cen * lax.rsqrt(var + LNX_EPS) * lng_ref[:, sl] + lnb_ref[:, sl] + bonus
        o_ref[bi, :, sl] = (y * part("g", bi, p)).astype(BF16)


def _rwkv(zm, zs, p, B, T):
    N = zm.shape[0]
    C = RWKV_CHUNK
    assert 2 * C == LANES and T % C == 0, "a head pair's stacked chunk must fill one 128-row tile"
    nc = T // C
    nb = RWKV_SEQS_PER_STEP if B % RWKV_SEQS_PER_STEP == 0 else 1
    tri = np.kron(np.eye(nb, dtype=np.float32), np.tril(np.ones((C, C), np.float32)))
    pad_rows = lambda w, rows: jnp.zeros((rows, HALF), F32).at[:w.shape[0]].set(w)
    row = lambda a: a.reshape(1, HALF)
    consts = [row(p["rw_w0"]), pad_rows(p["rw_decay_up"], LANES), row(p["rw_a0"]),
              pad_rows(p["rw_iclr_up"], LANES), pad_rows(p["rw_gate_up"], 2 * LANES),
              row(p["rw_k_k"]), row(p["rw_k_a"]), row(p["rw_r_k"]), row(p["rw_lnx_g"]),
              row(p["rw_lnx_b"]), jnp.asarray(tri, BF16)]
    full = lambda a: pl.BlockSpec(a.shape, lambda b, c: (0, 0))
    rcol = 4 * HALF // HALF
    zm3 = zm.reshape(B, T, Z_MAIN)
    out = pl.pallas_call(
        functools.partial(_rwkv_kernel, chunk=C, nb=nb),
        grid=(B // nb, nc),
        in_specs=[pl.BlockSpec((nb, C, HALF), lambda b, c: (b, c, rcol)),
                  pl.BlockSpec((nb, C, HALF), lambda b, c: (b, c, rcol + 1)),
                  pl.BlockSpec((nb, C, HALF), lambda b, c: (b, c, rcol + 2)),
                  pl.BlockSpec((nb, C, Z_SMALL), lambda b, c: (b, c, 0))] + [full(a) for a in consts],
        out_specs=pl.BlockSpec((nb, C, HALF), lambda b, c: (b, c, 0)),
        out_shape=jax.ShapeDtypeStruct((B, T, HALF), BF16),
        scratch_shapes=[pltpu.VMEM((nb, N_HEADS // 2, LANES, LANES), F32)],
        compiler_params=_cparams(("arbitrary", "arbitrary")),
        name="rwkv",
    )(zm3, zm3, zm3, zs.reshape(B, T, Z_SMALL), *consts)
    return out.reshape(N, HALF)


def _first_index(mask, iota, big):
    return jnp.min(jnp.where(mask, iota, big), axis=0, keepdims=True)


def _outproj_kernel(x_ref, yf_ref, yr_ref, g1_ref, sh2_ref, sc2_ref, g2_ref, n2g_ref, wo_ref, rwt_ref,
                    rb_ref, swg_ref, swu_ref, swd_ref, ustrict_ref,
                    xs_ref, hp_ref, idx_ref, wts_ref, rank_ref, *, tm):
    d = (jnp.dot(yf_ref[...], wo_ref[0:HALF, :], preferred_element_type=F32)
         + jnp.dot(yr_ref[...], wo_ref[HALF:2 * HALF, :], preferred_element_type=F32))
    x2 = x_ref[...] + g1_ref[...] * d
    h = x2 * lax.rsqrt(jnp.mean(x2 * x2, axis=-1, keepdims=True) + RMS_EPS) * n2g_ref[...]
    h = h * (1.0 + sc2_ref[...]) + sh2_ref[...]
    hb = h.astype(BF16)
    act = _silu(jnp.dot(hb, swg_ref[...], preferred_element_type=F32)) * jnp.dot(
        hb, swu_ref[...], preferred_element_type=F32)
    shared = jnp.dot(act.astype(BF16), swd_ref[...], preferred_element_type=F32)
    xs_ref[...] = x2 + g2_ref[...] * shared
    half = D_MODEL // 2
    hp_ref[...] = _pack_halves(h)

    r_hi, r_lo = _bf16_pieces(rwt_ref[...], 2)
    h_lo = (h - hb.astype(F32)).astype(BF16)
    nt = lambda a, b: lax.dot_general(a, b, (((1,), (1,)), ((), ())), preferred_element_type=F32)
    logits = nt(r_hi, hb) + nt(r_hi, h_lo) + nt(r_lo, hb)
    scores = _sigmoid(logits)
    sel = scores + rb_ref[:, 0:1]
    neg = -jnp.inf
    sel3 = sel.reshape(N_GROUPS, GROUP_SIZE, tm)
    io_in = lax.broadcasted_iota(I32, (N_GROUPS, GROUP_SIZE, tm), 1)
    m1 = jnp.max(sel3, axis=1, keepdims=True)
    f1 = jnp.min(jnp.where(sel3 == m1, io_in, GROUP_SIZE), axis=1, keepdims=True)
    m2 = jnp.max(jnp.where(io_in == f1, neg, sel3), axis=1, keepdims=True)
    gs = (m1 + m2).reshape(N_GROUPS, tm)
    io_g = lax.broadcasted_iota(I32, (N_GROUPS, tm), 0)
    gmask = jnp.zeros((N_GROUPS, tm), jnp.bool_)
    for _ in range(TOPK_GROUPS):
        mg = jnp.max(gs, axis=0, keepdims=True)
        fg = _first_index(gs == mg, io_g, N_GROUPS)
        pick = io_g == fg
        gmask = jnp.logical_or(gmask, pick)
        gs = jnp.where(pick, neg, gs)
    emask = jnp.broadcast_to(gmask.reshape(N_GROUPS, 1, tm), (N_GROUPS, GROUP_SIZE, tm)).reshape(N_EXPERTS, tm)
    cur = jnp.where(emask, sel, neg)
    io_e = lax.broadcasted_iota(I32, (N_EXPERTS, tm), 0)
    picks, idxs, wts = [], [], []
    for _ in range(TOP_K):
        me = jnp.max(cur, axis=0, keepdims=True)
        fe = _first_index(cur == me, io_e, N_EXPERTS)
        pick = io_e == fe
        picks.append(pick)
        idxs.append(fe)
        wts.append(jnp.sum(jnp.where(pick, scores, 0.0), axis=0, keepdims=True))
        cur = jnp.where(pick, neg, cur)
    wsum = wts[0]
    for w in wts[1:]:
        wsum = wsum + w
    zero_i = jnp.zeros((1, tm), I32)
    zero_f = jnp.zeros((1, tm), F32)
    idx_ref[...] = jnp.concatenate(idxs + [zero_i, zero_i], axis=0)
    wts_ref[...] = jnp.concatenate([w / wsum * ROUTED_SCALE for w in wts] + [zero_f, zero_f], axis=0)
    cnt = picks[0].astype(F32)
    for pk in picks[1:]:
        cnt = cnt + pk.astype(F32)
    excl = jnp.dot(cnt.astype(BF16), ustrict_ref[...], preferred_element_type=F32)
    ranks = [jnp.sum(jnp.where(pk, excl, 0.0), axis=0, keepdims=True).astype(I32) for pk in picks]
    rank_ref[...] = jnp.concatenate(ranks + [zero_i, zero_i], axis=0)


def _outproj(x2d, yf, yr, mod4, norm2_g, w_out, router_w, router_bias, swg, swu, swd, T):
    N, D = x2d.shape
    tm = min(512, T)
    tps = T // tm
    vec = lambda j: pl.BlockSpec((None, None, 1, D), lambda i: (i // tps, j, 0, 0))
    full = lambda a: pl.BlockSpec(a.shape, lambda i: (0, 0))
    ts = min(MOE_TILE, tm)
    ustrict = jnp.asarray(np.kron(np.eye(tm // ts, dtype=np.float32),
                                  np.triu(np.ones((ts, ts), np.float32), 1)), BF16)
    rb = jnp.broadcast_to(router_bias.reshape(N_EXPERTS, 1), (N_EXPERTS, LANES))
    consts = [norm2_g.reshape(1, D), w_out.astype(BF16), router_w.T, rb, swg.astype(BF16),
              swu.astype(BF16), swd.astype(BF16), ustrict]
    small = lambda dt: jax.ShapeDtypeStruct((8, N), dt)
    return pl.pallas_call(
        functools.partial(_outproj_kernel, tm=tm),
        grid=(N // tm,),
        in_specs=[pl.BlockSpec((tm, D), lambda i: (i, 0)),
                  pl.BlockSpec((tm, HALF), lambda i: (i, 0)),
                  pl.BlockSpec((tm, HALF), lambda i: (i, 0)),
                  vec(2), vec(3), vec(4), vec(5)] + [full(a) for a in consts],
        out_specs=[pl.BlockSpec((tm, D), lambda i: (i, 0)),
                   pl.BlockSpec((tm, D // 2), lambda i: (i, 0)),
                   pl.BlockSpec((8, tm), lambda i: (0, i)),
                   pl.BlockSpec((8, tm), lambda i: (0, i)),
                   pl.BlockSpec((8, tm), lambda i: (0, i))],
        out_shape=[jax.ShapeDtypeStruct((N, D), F32), jax.ShapeDtypeStruct((N, D // 2), jnp.uint32),
                   small(I32), small(F32), small(I32)],
        compiler_params=_cparams(("arbitrary",)),
        name="outproj",
    )(x2d, yf, yr, mod4, mod4, mod4, mod4, *consts)


def _wait_granules(count, wait_rows):
    full = count // WAIT_CHUNK

    def chunk(_, c):
        wait_rows(WAIT_CHUNK * GRANULE)
        return c

    def single(_, c):
        wait_rows(GRANULE)
        return c

    lax.fori_loop(0, full, chunk, 0)
    lax.fori_loop(full * WAIT_CHUNK, count, single, 0)


def _moe_plan(idx_t, rank_t):
    n = idx_t.shape[1]
    n_tiles = n // MOE_TILE
    experts = jnp.arange(N_EXPERTS, dtype=I32)
    hot = idx_t[:TOP_K, :, None] == experts
    cnt = jnp.sum(hot.reshape(TOP_K, n_tiles, MOE_TILE, N_EXPERTS).astype(I32), axis=(0, 2))
    gran = (cnt + GRANULE - 1) // GRANULE
    loc_end = jnp.cumsum(gran, axis=1)
    loc_off = loc_end - gran
    g_tile = loc_end[:, N_EXPERTS - 1]
    padded = (GRANULE * jnp.sum(gran, axis=0) + EXPERT_BLOCK - 1) // EXPERT_BLOCK * EXPERT_BLOCK
    pad_end = jnp.cumsum(padded)
    glob_off = (pad_end - padded)[None, :] + GRANULE * (jnp.cumsum(gran, axis=0) - gran)
    loc_tok = jnp.repeat(GRANULE * loc_off, MOE_TILE, axis=0)
    pos = rank_t[:TOP_K] + jnp.sum(jnp.where(hot, loc_tok[None], 0), axis=-1)
    pos = jnp.concatenate([pos, jnp.full((8 - TOP_K, n), -1, I32)], axis=0)
    g = jnp.arange(MOE_TILE, dtype=I32)
    e_of_g = jnp.minimum(jnp.sum((loc_end[:, None, :] <= g[None, :, None]).astype(I32), axis=-1), N_EXPERTS - 1)
    dst = jnp.sum(jnp.where(e_of_g[:, :, None] == experts,
                            glob_off[:, None, :] + GRANULE * (g[None, :, None] - loc_off[:, None, :]), 0), axis=-1)
    n_blocks = -(-(n * TOP_K + GRANULE * N_EXPERTS * n_tiles) // EXPERT_BLOCK) + N_EXPERTS
    n_rows = n_blocks * EXPERT_BLOCK
    gtab = jnp.where(g[None, :] < g_tile[:, None], dst, 0).reshape(n_tiles, 1, MOE_TILE)
    blk_start = jnp.arange(n_blocks, dtype=I32) * EXPERT_BLOCK
    blk_e = jnp.minimum(jnp.sum((pad_end[None, :] <= blk_start[:, None]).astype(I32), axis=1), N_EXPERTS - 1)
    plan = dict(pos=pos.astype(I32), gtab=gtab.astype(I32), g_tile=g_tile.astype(I32),
                pad_end=pad_end.astype(I32), padded=padded.astype(I32), blk_e=blk_e.astype(I32),
                n_used=(pad_end[N_EXPERTS - 1:] // EXPERT_BLOCK).astype(I32))
    return plan, n_rows


def _dispatch_kernel(pend_ref, padded_ref, nu_ref, gt_ref, gtab_ref, pos_ref, hp_ref, xs_ref, zeros, sbuf,
                     sem, zsem, *, n_blocks, n_tiles):
    @pl.when(pl.program_id(0) == 0)
    def _():
        zeros[...] = jnp.zeros_like(zeros)

        def block_copy(start):
            return pltpu.make_async_copy(zeros, xs_ref.at[pl.ds(pl.multiple_of(start, EXPERT_BLOCK),
                                                                 EXPERT_BLOCK), :], zsem)

        def pad_loop(fn):
            def body(e, _):
                @pl.when(padded_ref[e] > 0)
                def _():
                    fn(block_copy(pend_ref[e] - EXPERT_BLOCK))
                return 0
            lax.fori_loop(0, N_EXPERTS, body, 0)

        def tail_loop(fn):
            def body(b, _):
                fn(block_copy(b * EXPERT_BLOCK))
                return 0
            lax.fori_loop(nu_ref[0], n_blocks, body, 0)

        pad_loop(lambda cp: cp.start())
        tail_loop(lambda cp: cp.start())
        pad_loop(lambda cp: cp.wait())
        tail_loop(lambda cp: cp.wait())

    i = pl.program_id(0)
    slot = i % 2

    def store(g):
        pltpu.make_async_copy(sbuf.at[slot, pl.ds(pl.multiple_of(g * GRANULE, GRANULE), GRANULE), :],
                              xs_ref.at[pl.ds(pl.multiple_of(gtab_ref[0, 0, g], GRANULE), GRANULE), :],
                              sem.at[slot]).start(priority=g % 2 if isinstance(g, int) else 0)

    def wait_tile(which, count):
        def wait_rows(rows):
            pltpu.make_async_copy(sbuf.at[which, pl.ds(0, rows), :], xs_ref.at[pl.ds(0, rows), :],
                                  sem.at[which]).wait()
        wait_rows(MIN_GRANULES * GRANULE)
        _wait_granules(count - MIN_GRANULES, wait_rows)

    @pl.when(i >= 2)
    def _():
        wait_tile(slot, gt_ref[jnp.maximum(i - 2, 0)])

    pos = pos_ref[...]
    lo, hi = _unpack_halves(hp_ref[...])
    lo = lo.astype(BF16)
    hi = hi.astype(BF16)
    r_iota = lax.broadcasted_iota(I32, (MOE_ROW_CHUNK, MOE_TILE), 0).astype(F32).astype(BF16)
    one = jnp.ones((), BF16)
    for c in range(MOE_SORTED_ROWS // MOE_ROW_CHUNK):
        rel = pos - c * MOE_ROW_CHUNK
        rel = jnp.where(jnp.logical_and(rel >= 0, rel < MOE_ROW_CHUNK), rel, -1).astype(F32)
        perm = jnp.zeros((MOE_ROW_CHUNK, MOE_TILE), BF16)
        for k in range(TOP_K):
            perm = jnp.where(r_iota == rel[k:k + 1, :].astype(BF16), one, perm)
        sbuf[slot, c * MOE_ROW_CHUNK:(c + 1) * MOE_ROW_CHUNK, :] = _pack_bf16_values(
            jnp.dot(perm, lo, preferred_element_type=F32), jnp.dot(perm, hi, preferred_element_type=F32))
        g_lo, g_hi = c * MOE_ROW_CHUNK // GRANULE, (c + 1) * MOE_ROW_CHUNK // GRANULE
        for g in range(g_lo, min(g_hi, MIN_GRANULES)):
            store(g)
        if g_hi > MIN_GRANULES:
            def body(g, carry):
                store(g)
                return carry
            lax.fori_loop(max(g_lo, MIN_GRANULES), jnp.clip(gt_ref[i], max(g_lo, MIN_GRANULES), g_hi), body, 0)

    @pl.when(i == n_tiles - 1)
    def _():
        wait_tile(slot, gt_ref[i])
        if n_tiles > 1:
            wait_tile(1 - slot, gt_ref[jnp.maximum(i - 1, 0)])


def _dispatch(hp, plan, n_rows):
    N, W = hp.shape
    n_tiles = N // MOE_TILE
    n_blocks = n_rows // EXPERT_BLOCK
    return pl.pallas_call(
        functools.partial(_dispatch_kernel, n_blocks=n_blocks, n_tiles=n_tiles),
        grid_spec=pltpu.PrefetchScalarGridSpec(
            num_scalar_prefetch=4, grid=(n_tiles,),
            in_specs=[pl.BlockSpec((1, 1, MOE_TILE), lambda i, *_: (i, 0, 0), memory_space=pltpu.SMEM),
                      pl.BlockSpec((8, MOE_TILE), lambda i, *_: (0, i)),
                      pl.BlockSpec((MOE_TILE, W), lambda i, *_: (i, 0))],
            out_specs=pl.BlockSpec(memory_space=pl.ANY),
            scratch_shapes=[pltpu.VMEM((EXPERT_BLOCK, W), jnp.uint32),
                            pltpu.VMEM((2, MOE_SORTED_ROWS, W), jnp.uint32),
                            pltpu.SemaphoreType.DMA((2,)), pltpu.SemaphoreType.DMA]),
        out_shape=jax.ShapeDtypeStruct((n_rows, W), jnp.uint32),
        compiler_params=_cparams(("arbitrary",)),
        name="dispatch",
    )(plan["pad_end"], plan["padded"], plan["n_used"], plan["g_tile"], plan["gtab"], plan["pos"], hp)


def _experts_kernel(be_ref, nu_ref, xs_ref, wgf_ref, wuf_ref, wdf_ref, ys_ref, wg_ref, wu_ref, wd_ref,
                    xbuf, xsem):
    i = pl.program_id(0)
    n_live = nu_ref[0]
    live = i < n_live
    new_expert = jnp.logical_or(i == 0, be_ref[i] != be_ref[jnp.maximum(i - 1, 0)])

    def fetch(b):
        ring = lax.rem(b, EXPERT_RING)
        return pltpu.make_async_copy(xs_ref.at[pl.ds(pl.multiple_of(b * EXPERT_BLOCK, EXPERT_BLOCK), EXPERT_BLOCK), :],
                                     xbuf.at[ring], xsem.at[ring])

    @pl.when(i == 0)
    def _():
        for b in range(EXPERT_RING - 1):
            @pl.when(b < n_live)
            def _():
                fetch(b).start()

    @pl.when(i + EXPERT_RING - 1 < n_live)
    def _():
        fetch(i + EXPERT_RING - 1).start()

    @pl.when(jnp.logical_not(live))
    def _():
        ys_ref[...] = jnp.zeros_like(ys_ref)

    @pl.when(jnp.logical_and(live, new_expert))
    def _():
        wg_ref[...] = wgf_ref[...].astype(BF16)
        wu_ref[...] = wuf_ref[...].astype(BF16)
        wd_ref[...] = wdf_ref[...].astype(BF16)

    @pl.when(live)
    def _():
        half = D_MODEL // 2
        fetch(i).wait()
        lo, hi = _unpack_halves(xbuf[lax.rem(i, EXPERT_RING)])
        lo = lo.astype(BF16)
        hi = hi.astype(BF16)
        gate = (jnp.dot(lo, wg_ref[0:half, :], preferred_element_type=F32)
                + jnp.dot(hi, wg_ref[half:D_MODEL, :], preferred_element_type=F32))
        up = (jnp.dot(lo, wu_ref[0:half, :], preferred_element_type=F32)
              + jnp.dot(hi, wu_ref[half:D_MODEL, :], preferred_element_type=F32))
        y = jnp.dot((_silu(gate) * up).astype(BF16), wd_ref[...], preferred_element_type=F32)
        ys_ref[...] = _pack_halves(y)


def _experts(xs, n_rows, blk_e, n_used, wg, wu, wd):
    W = xs.shape[1]
    n_blocks = n_rows // EXPERT_BLOCK
    return pl.pallas_call(
        _experts_kernel,
        grid_spec=pltpu.PrefetchScalarGridSpec(
            num_scalar_prefetch=2, grid=(n_blocks,),
            in_specs=[pl.BlockSpec(memory_space=pl.ANY),
                      pl.BlockSpec((None, D_MODEL, D_EXPERT), lambda i, be, nu: (be[i], 0, 0)),
                      pl.BlockSpec((None, D_MODEL, D_EXPERT), lambda i, be, nu: (be[i], 0, 0)),
                      pl.BlockSpec((None, D_EXPERT, D_MODEL), lambda i, be, nu: (be[i], 0, 0))],
            out_specs=pl.BlockSpec((EXPERT_BLOCK, W), lambda i, be, nu: (i, 0)),
            scratch_shapes=[pltpu.VMEM((D_MODEL, D_EXPERT), BF16), pltpu.VMEM((D_MODEL, D_EXPERT), BF16),
                            pltpu.VMEM((D_EXPERT, D_MODEL), BF16),
                            pltpu.VMEM((EXPERT_RING, EXPERT_BLOCK, W), jnp.uint32),
                            pltpu.SemaphoreType.DMA((EXPERT_RING,))]),
        out_shape=jax.ShapeDtypeStruct((n_rows, W), jnp.uint32),
        compiler_params=_cparams(("arbitrary",)),
        name="experts",
    )(blk_e, n_used, xs, wg, wu, wd)


def _combine_kernel(gt_ref, gtab_ref, gtab_next_ref, ys_ref, pos_ref, xs_ref, wts_ref, g2_ref, fg_ref, o_ref,
                    buf, sem, *, n_tiles):
    i = pl.program_id(0)
    slot = i % 2
    nxt = jnp.minimum(i + 1, n_tiles - 1)

    def fetch(which, table_ref, g):
        pltpu.make_async_copy(ys_ref.at[pl.ds(pl.multiple_of(table_ref[0, 0, g], GRANULE), GRANULE), :],
                              buf.at[which, pl.ds(pl.multiple_of(g * GRANULE, GRANULE), GRANULE), :],
                              sem.at[which]).start(priority=g % 2 if isinstance(g, int) else 0)

    def fetch_rest(which, table_ref, count):
        def body(g, c):
            fetch(which, table_ref, g)
            return c
        lax.fori_loop(MIN_GRANULES, count, body, 0)

    def wait_tile(which, count):
        def wait_rows(rows):
            pltpu.make_async_copy(ys_ref.at[pl.ds(0, rows), :], buf.at[which, pl.ds(0, rows), :],
                                  sem.at[which]).wait()
        wait_rows(MIN_GRANULES * GRANULE)
        _wait_granules(count - MIN_GRANULES, wait_rows)

    @pl.when(i == 0)
    def _():
        buf[...] = jnp.zeros_like(buf)
        for g in range(MIN_GRANULES):
            fetch(0, gtab_ref, g)
        fetch_rest(0, gtab_ref, gt_ref[0])

    wait_tile(slot, gt_ref[i])

    pos = pos_ref[...]
    w = wts_ref[...]
    c_iota = lax.broadcasted_iota(I32, (MOE_TILE, MOE_ROW_CHUNK), 1).astype(F32).astype(BF16)
    half = D_MODEL // 2
    r_lo = jnp.zeros((MOE_TILE, half), F32)
    r_hi = jnp.zeros((MOE_TILE, half), F32)
    for c in range(MOE_SORTED_ROWS // MOE_ROW_CHUNK):
        rel = pos - c * MOE_ROW_CHUNK
        rel = jnp.where(jnp.logical_and(rel >= 0, rel < MOE_ROW_CHUNK), rel, -1).astype(F32)
        wb = jnp.zeros((MOE_TILE, MOE_ROW_CHUNK), BF16)
        for k in range(TOP_K):
            wb = jnp.where(c_iota == rel[:, k:k + 1].astype(BF16), w[:, k:k + 1].astype(BF16), wb)
        lo, hi = _unpack_halves(buf[slot, c * MOE_ROW_CHUNK:(c + 1) * MOE_ROW_CHUNK, :])
        r_lo = r_lo + jnp.dot(wb, lo.astype(BF16), preferred_element_type=F32)
        r_hi = r_hi + jnp.dot(wb, hi.astype(BF16), preferred_element_type=F32)
        for g in range(min(c * FETCH_PER_CHUNK, MIN_GRANULES), min((c + 1) * FETCH_PER_CHUNK, MIN_GRANULES)):
            fetch(1 - slot, gtab_next_ref, g)
        if c == MIN_GRANULES // FETCH_PER_CHUNK:
            fetch_rest(1 - slot, gtab_next_ref, gt_ref[nxt])
    x3 = xs_ref[...] + g2_ref[...] * jnp.concatenate([r_lo, r_hi], axis=1)
    y = x3 * lax.rsqrt(jnp.mean(x3 * x3, axis=-1, keepdims=True) + RMS_EPS) * fg_ref[...]
    o_ref[...] = y

    @pl.when(i == n_tiles - 1)
    def _():
        wait_tile(1 - slot, gt_ref[nxt])


def _combine(ys, plan, xsr, wts_t, mod4, final_g, T):
    N, D = xsr.shape
    W = ys.shape[1]
    tm = MOE_TILE
    tps = T // tm
    n_tiles = N // tm
    tile_tab = lambda off: pl.BlockSpec((1, 1, MOE_TILE), lambda i, gt: (jnp.minimum(i + off, n_tiles - 1), 0, 0),
                                        memory_space=pltpu.SMEM)
    return pl.pallas_call(
        functools.partial(_combine_kernel, n_tiles=n_tiles),
        grid_spec=pltpu.PrefetchScalarGridSpec(
            num_scalar_prefetch=1, grid=(n_tiles,),
            in_specs=[tile_tab(0), tile_tab(1),
                      pl.BlockSpec(memory_space=pl.ANY),
                      pl.BlockSpec((tm, 8), lambda i, gt: (i, 0)),
                      pl.BlockSpec((tm, D), lambda i, gt: (i, 0)),
                      pl.BlockSpec((tm, 8), lambda i, gt: (i, 0)),
                      pl.BlockSpec((None, None, 1, D), lambda i, gt: (i // tps, 5, 0, 0)),
                      pl.BlockSpec((1, D), lambda i, gt: (0, 0))],
            out_specs=pl.BlockSpec((tm, D), lambda i, gt: (i, 0)),
            scratch_shapes=[pltpu.VMEM((2, MOE_SORTED_ROWS, W), jnp.uint32), pltpu.SemaphoreType.DMA((2,))]),
        out_shape=jax.ShapeDtypeStruct((N, D), F32),
        compiler_params=_cparams(("arbitrary",)),
        name="combine",
    )(plan["g_tile"], plan["gtab"], plan["gtab"], ys, plan["pos"].T, xsr, wts_t, mod4,
      final_g.reshape(1, D))


def _pack_w_in(w_in):
    D = w_in.shape[0]
    fox_cols = 4 * HALF + 3 * N_HEADS
    wf = w_in[:, :fox_cols]
    wr = w_in[:, fox_cols:]
    o = 3 * HALF
    pad = lambda a, n: jnp.concatenate([a, jnp.zeros((D, n - a.shape[1]), a.dtype)], axis=1)
    parts = [wf[:, :4 * HALF], wr[:, :o],
             pad(wf[:, 4 * HALF:], LANES),
             pad(wr[:, o:o + DECAY_LORA], LANES),
             pad(wr[:, o + DECAY_LORA:o + DECAY_LORA + ICLR_LORA], LANES),
             pad(wr[:, o + DECAY_LORA + ICLR_LORA:], 2 * LANES)]
    return jnp.concatenate(parts, axis=1).astype(BF16)


def _pack_mu(mu):
    o = 3 * HALF
    pad = lambda a, n: jnp.concatenate([a, jnp.zeros((n - a.shape[0],), a.dtype)])
    small = jnp.concatenate([jnp.zeros((LANES,), mu.dtype),
                             pad(mu[o:o + DECAY_LORA], LANES),
                             pad(mu[o + DECAY_LORA:o + DECAY_LORA + ICLR_LORA], LANES),
                             pad(mu[o + DECAY_LORA + ICLR_LORA:], 2 * LANES)])
    return mu[:o].reshape(1, o), small.reshape(1, Z_SMALL)


def kernel(x, c, norm1_g, norm2_g, ada_w, ada_b, w_in, w_out, fox_qn_g, fox_kn_g, fox_on_g, fox_forget_b,
           rw_mu, rw_w0, rw_decay_up, rw_a0, rw_iclr_up, rw_gate_up, rw_k_k, rw_k_a, rw_r_k, rw_lnx_g,
           rw_lnx_b, router_w, router_bias, exp_w_gate, exp_w_up, exp_w_down, sh_w_gate, sh_w_up,
           sh_w_down, final_g):
    B, T, D = x.shape
    N = B * T
    depth = norm1_g.shape[0]
    assert depth == 1, "the combine kernel fuses the final RMSNorm, so exactly one layer is supported"
    xf = x.reshape(N, D)
    for l in range(depth):
        mod4 = _ada(c, ada_w[l], ada_b[l]).reshape(B, 6, 1, D)
        mu_big, mu_small = _pack_mu(rw_mu[l])
        zm, zs = _inproj(xf, mod4, norm1_g[l], _pack_w_in(w_in[l]), mu_big, mu_small, T)
        qp, kp, vp = _foxprep(zm, zs, fox_forget_b[l], fox_qn_g[l], fox_kn_g[l], T)
        y_fox = _attention(qp, kp, vp, zm, fox_on_g[l], B, T)
        rw = dict(rw_w0=rw_w0[l], rw_decay_up=rw_decay_up[l], rw_a0=rw_a0[l], rw_iclr_up=rw_iclr_up[l],
                  rw_gate_up=rw_gate_up[l], rw_k_k=rw_k_k[l], rw_k_a=rw_k_a[l], rw_r_k=rw_r_k[l],
                  rw_lnx_g=rw_lnx_g[l], rw_lnx_b=rw_lnx_b[l])
        y_rwkv = _rwkv(zm, zs, rw, B, T)
        xsr, hp, idx_t, wts_t, rank_t = _outproj(
            xf, y_fox, y_rwkv, mod4, norm2_g[l], w_out[l], router_w[l], router_bias[l],
            sh_w_gate[l], sh_w_up[l], sh_w_down[l], T)
        plan, n_rows = _moe_plan(idx_t, rank_t)
        xs = _dispatch(hp, plan, n_rows)
        ys = _experts(xs, n_rows, plan["blk_e"], plan["n_used"], exp_w_gate[l], exp_w_up[l], exp_w_down[l])
        xf = _combine(ys, plan, xsr, wts_t.T, mod4, final_g, T)
    return xf.reshape(B, T, D)
```
